```python
import math
import jax, jax.numpy as jnp
from jax import lax
import numpy as np

D_MODEL = 1024
BATCH = 8
SEQ = 4096
DEPTH = 1

CHUNK = 64
D_MIX = D_MODEL
S5_WIDTH = D_MIX // 2
S5_GROUP_CH = 16
S5_GROUPS = S5_WIDTH // S5_GROUP_CH
S5_STATE = 64
CONV_WIDTH = D_MIX - S5_WIDTH
CONV_HEAD_DIM = 64
CONV_HEADS = CONV_WIDTH // CONV_HEAD_DIM
CONV_K = 31
D_FF = 2816
IN_COLS = S5_WIDTH + 2 * CONV_WIDTH
EPS = 1e-6

kernel_name = "hybrid_s5_conformer_conv_macaron"


def rms_norm(x, g):
    xf = x.astype(jnp.float32)
    y = xf * lax.rsqrt(jnp.mean(xf * xf, axis=-1, keepdims=True) + EPS)
    return (y * g.astype(jnp.float32)).astype(x.dtype)


def swiglu_ffn(h, w_gate, w_up, w_down):
    return (jax.nn.silu(h @ w_gate) * (h @ w_up)) @ w_down


def _complex_affine_combine(left, right):
    a_re_i, a_im_i, b_re_i, b_im_i = left
    a_re_j, a_im_j, b_re_j, b_im_j = right
    a_re = a_re_j * a_re_i - a_im_j * a_im_i
    a_im = a_re_j * a_im_i + a_im_j * a_re_i
    b_re = a_re_j * b_re_i - a_im_j * b_im_i + b_re_j
    b_im = a_re_j * b_im_i + a_im_j * b_re_i + b_im_j
    return (a_re, a_im, b_re, b_im)


def s5_mixer(u, lam_re, lam_im, log_dt, b_re, b_im, c_re, c_im, d_skip, w_glu, b_glu):
    bsz, seq_len, _ = u.shape
    uf = u.astype(jnp.float32).reshape(bsz, seq_len, S5_GROUPS, S5_GROUP_CH)
    lr = lam_re.astype(jnp.float32)
    li = lam_im.astype(jnp.float32)
    dt = jnp.exp(log_dt.astype(jnp.float32))[:, None]
    mag = jnp.exp(lr * dt)
    abar_re = mag * jnp.cos(li * dt)
    abar_im = mag * jnp.sin(li * dt)
    den = lr * lr + li * li
    num_re = abar_re - 1.0
    num_im = abar_im
    f_re = ((num_re * lr + num_im * li) / den)[..., None]
    f_im = ((num_im * lr - num_re * li) / den)[..., None]
    br = b_re.astype(jnp.float32)
    bi = b_im.astype(jnp.float32)
    bbar_re = f_re * br - f_im * bi
    bbar_im = f_re * bi + f_im * br
    bu_re = jnp.einsum('blgc,gpc->blgp', uf, bbar_re)
    bu_im = jnp.einsum('blgc,gpc->blgp', uf, bbar_im)
    a_re = jnp.broadcast_to(abar_re[None, None], (1, seq_len, S5_GROUPS, S5_STATE))
    a_im = jnp.broadcast_to(abar_im[None, None], (1, seq_len, S5_GROUPS, S5_STATE))
    _, _, s_re, s_im = lax.associative_scan(_complex_affine_combine,
                                            (a_re, a_im, bu_re, bu_im), axis=1)
    y = (jnp.einsum('blgp,gcp->blgc', s_re, c_re.astype(jnp.float32))
         - jnp.einsum('blgp,gcp->blgc', s_im, c_im.astype(jnp.float32)))
    y = y + d_skip.astype(jnp.float32).reshape(S5_GROUPS, S5_GROUP_CH) * uf
    y = jax.nn.gelu(y.reshape(bsz, seq_len, S5_WIDTH)).astype(u.dtype)
    return y * jax.nn.sigmoid(y @ w_glu + b_glu)


def conv_module_mixer(v, w_dw, b_dw, ln_g, ln_b):
    bsz, seq_len, _ = v.shape
    z = v[..., :CONV_WIDTH] * jax.nn.sigmoid(v[..., CONV_WIDTH:])
    z = lax.conv_general_dilated(
        z, w_dw[:, None, :], window_strides=(1,), padding=[(CONV_K - 1, 0)],
        dimension_numbers=('NWC', 'WIO', 'NWC'), feature_group_count=CONV_WIDTH) + b_dw
    zf = z.astype(jnp.float32).reshape(bsz, seq_len, CONV_HEADS, CONV_HEAD_DIM)
    mu = jnp.mean(zf, axis=-1, keepdims=True)
    var = jnp.mean(jnp.square(zf - mu), axis=-1, keepdims=True)
    zn = ((zf - mu) * lax.rsqrt(var + EPS)).reshape(bsz, seq_len, CONV_WIDTH)
    zn = zn * ln_g.astype(jnp.float32) + ln_b.astype(jnp.float32)
    return jax.nn.silu(zn).astype(v.dtype)


def _fwd_setup_inputs(seed: int = 0) -> dict:
    key = jax.random.key(seed)
    ks = jax.random.split(key, 32)
    f32 = jnp.float32

    def nrm(k, shape, scale):
        return jax.random.normal(k, shape, f32) * scale

    def gain(k, shape):
        return 1.0 + 0.02 * jax.random.normal(k, shape, f32)

    L = DEPTH
    n_idx = jnp.arange(S5_STATE, dtype=f32)
    lam_re = -0.5 * (1.0 + 0.05 * jax.random.normal(ks[10], (L, S5_GROUPS, S5_STATE), f32))
    lam_im = jnp.broadcast_to(math.pi * n_idx, (L, S5_GROUPS, S5_STATE)) \
        + 0.01 * jax.random.normal(ks[11], (L, S5_GROUPS, S5_STATE), f32)
    log_dt = jax.random.uniform(ks[12], (L, S5_GROUPS), f32, math.log(1e-3), math.log(1e-1))
    return {
        "x": jax.random.normal(ks[0], (BATCH, SEQ, D_MODEL), f32),
        "ffn1_norm": gain(ks[1], (L, D_MODEL)),
        "ffn1_w_gate": nrm(ks[2], (L, D_MODEL, D_FF), D_MODEL ** -0.5),
        "ffn1_w_up": nrm(ks[3], (L, D_MODEL, D_FF), D_MODEL ** -0.5),
        "ffn1_w_down": nrm(ks[4], (L, D_FF, D_MODEL), D_FF ** -0.5),
        "mix_norm": gain(ks[5], (L, D_MODEL)),
        "w_in": nrm(ks[6], (L, D_MODEL, IN_COLS), D_MODEL ** -0.5),
        "s5_lam_re": lam_re,
        "s5_lam_im": lam_im,
        "s5_log_dt": log_dt,
        "s5_b_re": nrm(ks[13], (L, S5_GROUPS, S5_STATE, S5_GROUP_CH), (2 * S5_GROUP_CH) ** -0.5),
        "s5_b_im": nrm(ks[14], (L, S5_GROUPS, S5_STATE, S5_GROUP_CH), (2 * S5_GROUP_CH) ** -0.5),
        "s5_c_re": nrm(ks[15], (L, S5_GROUPS, S5_GROUP_CH, S5_STATE), (2 * S5_STATE) ** -0.5),
        "s5_c_im": nrm(ks[16], (L, S5_GROUPS, S5_GROUP_CH, S5_STATE), (2 * S5_STATE) ** -0.5),
        "s5_d": gain(ks[17], (L, S5_WIDTH)),
        "s5_w_glu": nrm(ks[18], (L, S5_WIDTH, S5_WIDTH), S5_WIDTH ** -0.5),
        "s5_b_glu": nrm(ks[19], (L, S5_WIDTH), 0.02),
        "conv_w_dw": nrm(ks[20], (L, CONV_K, CONV_WIDTH), CONV_K ** -0.5),
        "conv_b_dw": nrm(ks[21], (L, CONV_WIDTH), 0.02),
        "conv_ln_g": gain(ks[22], (L, CONV_WIDTH)),
        "conv_ln_b": nrm(ks[23], (L, CONV_WIDTH), 0.02),
        "w_out": nrm(ks[24], (L, D_MIX, D_MODEL), D_MIX ** -0.5),
        "ffn2_norm": gain(ks[25], (L, D_MODEL)),
        "ffn2_w_gate": nrm(ks[26], (L, D_MODEL, D_FF), D_MODEL ** -0.5),
        "ffn2_w_up": nrm(ks[27], (L, D_MODEL, D_FF), D_MODEL ** -0.5),
        "ffn2_w_down": nrm(ks[28], (L, D_FF, D_MODEL), D_FF ** -0.5),
        "final_norm": gain(ks[29], (D_MODEL,)),
    }


def _fwd_reference(x, ffn1_norm, ffn1_w_gate, ffn1_w_up, ffn1_w_down, mix_norm, w_in,
              s5_lam_re, s5_lam_im, s5_log_dt, s5_b_re, s5_b_im, s5_c_re, s5_c_im,
              s5_d, s5_w_glu, s5_b_glu, conv_w_dw, conv_b_dw, conv_ln_g, conv_ln_b,
              w_out, ffn2_norm, ffn2_w_gate, ffn2_w_up, ffn2_w_down, final_norm):
    for l in range(DEPTH):
        h = rms_norm(x, ffn1_norm[l])
        x = x + 0.5 * swiglu_ffn(h, ffn1_w_gate[l], ffn1_w_up[l], ffn1_w_down[l])
        h = rms_norm(x, mix_norm[l])
        u = h @ w_in[l]
        y_s5 = s5_mixer(u[..., :S5_WIDTH], s5_lam_re[l], s5_lam_im[l], s5_log_dt[l],
                        s5_b_re[l], s5_b_im[l], s5_c_re[l], s5_c_im[l], s5_d[l],
                        s5_w_glu[l], s5_b_glu[l])
        y_conv = conv_module_mixer(u[..., S5_WIDTH:], conv_w_dw[l], conv_b_dw[l],
                                   conv_ln_g[l], conv_ln_b[l])
        x = x + jnp.concatenate([y_s5, y_conv], axis=-1) @ w_out[l]
        h = rms_norm(x, ffn2_norm[l])
        x = x + 0.5 * swiglu_ffn(h, ffn2_w_gate[l], ffn2_w_up[l], ffn2_w_down[l])
    return rms_norm(x, final_norm)


import jax as _jax
import jax.numpy as _jnp

TWIN_FORMAT = 'train_step'
FWD_PARAMS = ['x', 'ffn1_norm', 'ffn1_w_gate', 'ffn1_w_up', 'ffn1_w_down', 'mix_norm', 'w_in', 's5_lam_re', 's5_lam_im', 's5_log_dt', 's5_b_re', 's5_b_im', 's5_c_re', 's5_c_im', 's5_d', 's5_w_glu', 's5_b_glu', 'conv_w_dw', 'conv_b_dw', 'conv_ln_g', 'conv_ln_b', 'w_out', 'ffn2_norm', 'ffn2_w_gate', 'ffn2_w_up', 'ffn2_w_down', 'final_norm']
TWIN_WEIGHTS = ['ffn1_norm', 'ffn1_w_gate', 'ffn1_w_up', 'ffn1_w_down', 'mix_norm', 'w_in', 's5_lam_re', 's5_lam_im', 's5_log_dt', 's5_b_re', 's5_b_im', 's5_c_re', 's5_c_im', 's5_d', 's5_w_glu', 's5_b_glu', 'conv_w_dw', 'conv_b_dw', 'conv_ln_g', 'conv_ln_b', 'w_out', 'ffn2_norm', 'ffn2_w_gate', 'ffn2_w_up', 'ffn2_w_down', 'final_norm']
TWIN_DIFF_INPUT = 'x'
TWIN_INPUTS = ['x', 'ffn1_norm', 'ffn1_w_gate', 'ffn1_w_up', 'ffn1_w_down', 'mix_norm', 'w_in', 's5_lam_re', 's5_lam_im', 's5_log_dt', 's5_b_re', 's5_b_im', 's5_c_re', 's5_c_im', 's5_d', 's5_w_glu', 's5_b_glu', 'conv_w_dw', 'conv_b_dw', 'conv_ln_g', 'conv_ln_b', 'w_out', 'ffn2_norm', 'ffn2_w_gate', 'ffn2_w_up', 'ffn2_w_down', 'final_norm', 'loss_target', 'm_ffn1_norm', 'm_ffn1_w_gate', 'm_ffn1_w_up', 'm_ffn1_w_down', 'm_mix_norm', 'm_w_in', 'm_s5_lam_re', 'm_s5_lam_im', 'm_s5_log_dt', 'm_s5_b_re', 'm_s5_b_im', 'm_s5_c_re', 'm_s5_c_im', 'm_s5_d', 'm_s5_w_glu', 'm_s5_b_glu', 'm_conv_w_dw', 'm_conv_b_dw', 'm_conv_ln_g', 'm_conv_ln_b', 'm_w_out', 'm_ffn2_norm', 'm_ffn2_w_gate', 'm_ffn2_w_up', 'm_ffn2_w_down', 'm_final_norm', 'v_ffn1_norm', 'v_ffn1_w_gate', 'v_ffn1_w_up', 'v_ffn1_w_down', 'v_mix_norm', 'v_w_in', 'v_s5_lam_re', 'v_s5_lam_im', 'v_s5_log_dt', 'v_s5_b_re', 'v_s5_b_im', 'v_s5_c_re', 'v_s5_c_im', 'v_s5_d', 'v_s5_w_glu', 'v_s5_b_glu', 'v_conv_w_dw', 'v_conv_b_dw', 'v_conv_ln_g', 'v_conv_ln_b', 'v_w_out', 'v_ffn2_norm', 'v_ffn2_w_gate', 'v_ffn2_w_up', 'v_ffn2_w_down', 'v_final_norm']
TWIN_OUTPUTS = ['loss', 'grad_x', 'grad_ffn1_norm', 'grad_ffn1_w_gate', 'grad_ffn1_w_up', 'grad_ffn1_w_down', 'grad_mix_norm', 'grad_w_in', 'grad_s5_lam_re', 'grad_s5_lam_im', 'grad_s5_log_dt', 'grad_s5_b_re', 'grad_s5_b_im', 'grad_s5_c_re', 'grad_s5_c_im', 'grad_s5_d', 'grad_s5_w_glu', 'grad_s5_b_glu', 'grad_conv_w_dw', 'grad_conv_b_dw', 'grad_conv_ln_g', 'grad_conv_ln_b', 'grad_w_out', 'grad_ffn2_norm', 'grad_ffn2_w_gate', 'grad_ffn2_w_up', 'grad_ffn2_w_down', 'grad_final_norm', 'delta_ffn1_norm', 'delta_ffn1_w_gate', 'delta_ffn1_w_up', 'delta_ffn1_w_down', 'delta_mix_norm', 'delta_w_in', 'delta_s5_lam_re', 'delta_s5_lam_im', 'delta_s5_log_dt', 'delta_s5_b_re', 'delta_s5_b_im', 'delta_s5_c_re', 'delta_s5_c_im', 'delta_s5_d', 'delta_s5_w_glu', 'delta_s5_b_glu', 'delta_conv_w_dw', 'delta_conv_b_dw', 'delta_conv_ln_g', 'delta_conv_ln_b', 'delta_w_out', 'delta_ffn2_norm', 'delta_ffn2_w_gate', 'delta_ffn2_w_up', 'delta_ffn2_w_down', 'delta_final_norm', 'new_m_ffn1_norm', 'new_m_ffn1_w_gate', 'new_m_ffn1_w_up', 'new_m_ffn1_w_down', 'new_m_mix_norm', 'new_m_w_in', 'new_m_s5_lam_re', 'new_m_s5_lam_im', 'new_m_s5_log_dt', 'new_m_s5_b_re', 'new_m_s5_b_im', 'new_m_s5_c_re', 'new_m_s5_c_im', 'new_m_s5_d', 'new_m_s5_w_glu', 'new_m_s5_b_glu', 'new_m_conv_w_dw', 'new_m_conv_b_dw', 'new_m_conv_ln_g', 'new_m_conv_ln_b', 'new_m_w_out', 'new_m_ffn2_norm', 'new_m_ffn2_w_gate', 'new_m_ffn2_w_up', 'new_m_ffn2_w_down', 'new_m_final_norm', 'new_v_ffn1_norm', 'new_v_ffn1_w_gate', 'new_v_ffn1_w_up', 'new_v_ffn1_w_down', 'new_v_mix_norm', 'new_v_w_in', 'new_v_s5_lam_re', 'new_v_s5_lam_im', 'new_v_s5_log_dt', 'new_v_s5_b_re', 'new_v_s5_b_im', 'new_v_s5_c_re', 'new_v_s5_c_im', 'new_v_s5_d', 'new_v_s5_w_glu', 'new_v_s5_b_glu', 'new_v_conv_w_dw', 'new_v_conv_b_dw', 'new_v_conv_ln_g', 'new_v_conv_ln_b', 'new_v_w_out', 'new_v_ffn2_norm', 'new_v_ffn2_w_gate', 'new_v_ffn2_w_up', 'new_v_ffn2_w_down', 'new_v_final_norm']
TWIN_LEAF_KINDS = {'loss': 'loss', 'grad_x': 'grad_x', 'grad_ffn1_norm': 'grad_w', 'grad_ffn1_w_gate': 'grad_w', 'grad_ffn1_w_up': 'grad_w', 'grad_ffn1_w_down': 'grad_w', 'grad_mix_norm': 'grad_w', 'grad_w_in': 'grad_w', 'grad_s5_lam_re': 'grad_w', 'grad_s5_lam_im': 'grad_w', 'grad_s5_log_dt': 'grad_w', 'grad_s5_b_re': 'grad_w', 'grad_s5_b_im': 'grad_w', 'grad_s5_c_re': 'grad_w', 'grad_s5_c_im': 'grad_w', 'grad_s5_d': 'grad_w', 'grad_s5_w_glu': 'grad_w', 'grad_s5_b_glu': 'grad_w', 'grad_conv_w_dw': 'grad_w', 'grad_conv_b_dw': 'grad_w', 'grad_conv_ln_g': 'grad_w', 'grad_conv_ln_b': 'grad_w', 'grad_w_out': 'grad_w', 'grad_ffn2_norm': 'grad_w', 'grad_ffn2_w_gate': 'grad_w', 'grad_ffn2_w_up': 'grad_w', 'grad_ffn2_w_down': 'grad_w', 'grad_final_norm': 'grad_w', 'delta_ffn1_norm': 'delta_w', 'delta_ffn1_w_gate': 'delta_w', 'delta_ffn1_w_up': 'delta_w', 'delta_ffn1_w_down': 'delta_w', 'delta_mix_norm': 'delta_w', 'delta_w_in': 'delta_w', 'delta_s5_lam_re': 'delta_w', 'delta_s5_lam_im': 'delta_w', 'delta_s5_log_dt': 'delta_w', 'delta_s5_b_re': 'delta_w', 'delta_s5_b_im': 'delta_w', 'delta_s5_c_re': 'delta_w', 'delta_s5_c_im': 'delta_w', 'delta_s5_d': 'delta_w', 'delta_s5_w_glu': 'delta_w', 'delta_s5_b_glu': 'delta_w', 'delta_conv_w_dw': 'delta_w', 'delta_conv_b_dw': 'delta_w', 'delta_conv_ln_g': 'delta_w', 'delta_conv_ln_b': 'delta_w', 'delta_w_out': 'delta_w', 'delta_ffn2_norm': 'delta_w', 'delta_ffn2_w_gate': 'delta_w', 'delta_ffn2_w_up': 'delta_w', 'delta_ffn2_w_down': 'delta_w', 'delta_final_norm': 'delta_w', 'new_m_ffn1_norm': 'new_m', 'new_m_ffn1_w_gate': 'new_m', 'new_m_ffn1_w_up': 'new_m', 'new_m_ffn1_w_down': 'new_m', 'new_m_mix_norm': 'new_m', 'new_m_w_in': 'new_m', 'new_m_s5_lam_re': 'new_m', 'new_m_s5_lam_im': 'new_m', 'new_m_s5_log_dt': 'new_m', 'new_m_s5_b_re': 'new_m', 'new_m_s5_b_im': 'new_m', 'new_m_s5_c_re': 'new_m', 'new_m_s5_c_im': 'new_m', 'new_m_s5_d': 'new_m', 'new_m_s5_w_glu': 'new_m', 'new_m_s5_b_glu': 'new_m', 'new_m_conv_w_dw': 'new_m', 'new_m_conv_b_dw': 'new_m', 'new_m_conv_ln_g': 'new_m', 'new_m_conv_ln_b': 'new_m', 'new_m_w_out': 'new_m', 'new_m_ffn2_norm': 'new_m', 'new_m_ffn2_w_gate': 'new_m', 'new_m_ffn2_w_up': 'new_m', 'new_m_ffn2_w_down': 'new_m', 'new_m_final_norm': 'new_m', 'new_v_ffn1_norm': 'new_v', 'new_v_ffn1_w_gate': 'new_v', 'new_v_ffn1_w_up': 'new_v', 'new_v_ffn1_w_down': 'new_v', 'new_v_mix_norm': 'new_v', 'new_v_w_in': 'new_v', 'new_v_s5_lam_re': 'new_v', 'new_v_s5_lam_im': 'new_v', 'new_v_s5_log_dt': 'new_v', 'new_v_s5_b_re': 'new_v', 'new_v_s5_b_im': 'new_v', 'new_v_s5_c_re': 'new_v', 'new_v_s5_c_im': 'new_v', 'new_v_s5_d': 'new_v', 'new_v_s5_w_glu': 'new_v', 'new_v_s5_b_glu': 'new_v', 'new_v_conv_w_dw': 'new_v', 'new_v_conv_b_dw': 'new_v', 'new_v_conv_ln_g': 'new_v', 'new_v_conv_ln_b': 'new_v', 'new_v_w_out': 'new_v', 'new_v_ffn2_norm': 'new_v', 'new_v_ffn2_w_gate': 'new_v', 'new_v_ffn2_w_up': 'new_v', 'new_v_ffn2_w_down': 'new_v', 'new_v_final_norm': 'new_v'}


def _forward(args):
    return _fwd_reference(*[args[k] for k in FWD_PARAMS])


def _output_shape():
    out = _jax.eval_shape(lambda: _forward(_fwd_setup_inputs(0)))
    return out.shape, out.dtype

N_MICROBATCH = 1
ADAM_LR = 0.001
ADAM_B1 = 0.9
ADAM_B2 = 0.999
ADAM_EPS = 1e-08
ADAM_WD = 0.01
ADAM_STEP = 10
PER_EXAMPLE_BATCH_AXIS = {'x': 0, 'loss_target': 0}
SHARED_INPUTS = []
_WEIGHT_DTYPES = {'ffn1_norm': _jnp.float32, 'ffn1_w_gate': _jnp.float32, 'ffn1_w_up': _jnp.float32, 'ffn1_w_down': _jnp.float32, 'mix_norm': _jnp.float32, 'w_in': _jnp.float32, 's5_lam_re': _jnp.float32, 's5_lam_im': _jnp.float32, 's5_log_dt': _jnp.float32, 's5_b_re': _jnp.float32, 's5_b_im': _jnp.float32, 's5_c_re': _jnp.float32, 's5_c_im': _jnp.float32, 's5_d': _jnp.float32, 's5_w_glu': _jnp.float32, 's5_b_glu': _jnp.float32, 'conv_w_dw': _jnp.float32, 'conv_b_dw': _jnp.float32, 'conv_ln_g': _jnp.float32, 'conv_ln_b': _jnp.float32, 'w_out': _jnp.float32, 'ffn2_norm': _jnp.float32, 'ffn2_w_gate': _jnp.float32, 'ffn2_w_up': _jnp.float32, 'ffn2_w_down': _jnp.float32, 'final_norm': _jnp.float32}
MOMENT_SCALE = {'ffn1_norm': 8.542051e-02, 'ffn1_w_gate': 3.325983e-02, 'ffn1_w_up': 3.216567e-02, 'ffn1_w_down': 5.329673e-02, 'mix_norm': 8.372426e-02, 'w_in': 7.006865e-02, 's5_lam_re': 3.939701e-03, 's5_lam_im': 3.783870e-03, 's5_log_dt': 3.055591e+00, 's5_b_re': 2.129314e-03, 's5_b_im': 2.122634e-03, 's5_c_re': 4.298053e-03, 's5_c_im': 4.257200e-03, 's5_d': 7.064601e-02, 's5_w_glu': 1.580675e-02, 's5_b_glu': 2.458956e-02, 'conv_w_dw': 1.011542e-01, 'conv_b_dw': 2.161297e-01, 'conv_ln_g': 1.173412e-01, 'conv_ln_b': 1.066173e-01, 'w_out': 7.829839e-02, 'ffn2_norm': 6.612380e-02, 'ffn2_w_gate': 2.771636e-02, 'ffn2_w_up': 2.691061e-02, 'ffn2_w_down': 4.466622e-02, 'final_norm': 3.195239e+01}


def _to_microbatches(a, axis):
    t = _jnp.moveaxis(a, axis, 0)
    t = t.reshape((N_MICROBATCH, t.shape[0] // N_MICROBATCH) + t.shape[1:])
    return _jnp.moveaxis(t, 1, axis + 1)


def setup_inputs(seed: int = 0) -> dict:
    inp = _fwd_setup_inputs(seed)
    key = _jax.random.fold_in(_jax.random.key(seed), 7919)
    shape, _ = _output_shape()
    out = dict(inp)
    out["loss_target"] = _jax.random.normal(_jax.random.fold_in(key, 0), shape, _jnp.float32)
    for i, name in enumerate(TWIN_WEIGHTS):
        w = inp[name].astype(_jnp.float32)
        if MOMENT_SCALE is None:
            s = _jnp.sqrt(_jnp.mean(_jnp.square(w)) + 1e-30)
        else:
            s = MOMENT_SCALE[name]
        km, kv = _jax.random.split(_jax.random.fold_in(key, i + 1))
        out[name] = w
        out["m_" + name] = s * _jax.random.normal(km, w.shape, _jnp.float32)
        out["v_" + name] = (s * s) * _jax.random.uniform(kv, w.shape, _jnp.float32, 0.5, 1.5)
    if N_MICROBATCH > 1:
        for name, axis in PER_EXAMPLE_BATCH_AXIS.items():
            out[name] = _to_microbatches(out[name], axis)
    return {'x': out['x'], 'ffn1_norm': out['ffn1_norm'], 'ffn1_w_gate': out['ffn1_w_gate'], 'ffn1_w_up': out['ffn1_w_up'], 'ffn1_w_down': out['ffn1_w_down'], 'mix_norm': out['mix_norm'], 'w_in': out['w_in'], 's5_lam_re': out['s5_lam_re'], 's5_lam_im': out['s5_lam_im'], 's5_log_dt': out['s5_log_dt'], 's5_b_re': out['s5_b_re'], 's5_b_im': out['s5_b_im'], 's5_c_re': out['s5_c_re'], 's5_c_im': out['s5_c_im'], 's5_d': out['s5_d'], 's5_w_glu': out['s5_w_glu'], 's5_b_glu': out['s5_b_glu'], 'conv_w_dw': out['conv_w_dw'], 'conv_b_dw': out['conv_b_dw'], 'conv_ln_g': out['conv_ln_g'], 'conv_ln_b': out['conv_ln_b'], 'w_out': out['w_out'], 'ffn2_norm': out['ffn2_norm'], 'ffn2_w_gate': out['ffn2_w_gate'], 'ffn2_w_up': out['ffn2_w_up'], 'ffn2_w_down': out['ffn2_w_down'], 'final_norm': out['final_norm'], 'loss_target': out['loss_target'], 'm_ffn1_norm': out['m_ffn1_norm'], 'm_ffn1_w_gate': out['m_ffn1_w_gate'], 'm_ffn1_w_up': out['m_ffn1_w_up'], 'm_ffn1_w_down': out['m_ffn1_w_down'], 'm_mix_norm': out['m_mix_norm'], 'm_w_in': out['m_w_in'], 'm_s5_lam_re': out['m_s5_lam_re'], 'm_s5_lam_im': out['m_s5_lam_im'], 'm_s5_log_dt': out['m_s5_log_dt'], 'm_s5_b_re': out['m_s5_b_re'], 'm_s5_b_im': out['m_s5_b_im'], 'm_s5_c_re': out['m_s5_c_re'], 'm_s5_c_im': out['m_s5_c_im'], 'm_s5_d': out['m_s5_d'], 'm_s5_w_glu': out['m_s5_w_glu'], 'm_s5_b_glu': out['m_s5_b_glu'], 'm_conv_w_dw': out['m_conv_w_dw'], 'm_conv_b_dw': out['m_conv_b_dw'], 'm_conv_ln_g': out['m_conv_ln_g'], 'm_conv_ln_b': out['m_conv_ln_b'], 'm_w_out': out['m_w_out'], 'm_ffn2_norm': out['m_ffn2_norm'], 'm_ffn2_w_gate': out['m_ffn2_w_gate'], 'm_ffn2_w_up': out['m_ffn2_w_up'], 'm_ffn2_w_down': out['m_ffn2_w_down'], 'm_final_norm': out['m_final_norm'], 'v_ffn1_norm': out['v_ffn1_norm'], 'v_ffn1_w_gate': out['v_ffn1_w_gate'], 'v_ffn1_w_up': out['v_ffn1_w_up'], 'v_ffn1_w_down': out['v_ffn1_w_down'], 'v_mix_norm': out['v_mix_norm'], 'v_w_in': out['v_w_in'], 'v_s5_lam_re': out['v_s5_lam_re'], 'v_s5_lam_im': out['v_s5_lam_im'], 'v_s5_log_dt': out['v_s5_log_dt'], 'v_s5_b_re': out['v_s5_b_re'], 'v_s5_b_im': out['v_s5_b_im'], 'v_s5_c_re': out['v_s5_c_re'], 'v_s5_c_im': out['v_s5_c_im'], 'v_s5_d': out['v_s5_d'], 'v_s5_w_glu': out['v_s5_w_glu'], 'v_s5_b_glu': out['v_s5_b_glu'], 'v_conv_w_dw': out['v_conv_w_dw'], 'v_conv_b_dw': out['v_conv_b_dw'], 'v_conv_ln_g': out['v_conv_ln_g'], 'v_conv_ln_b': out['v_conv_ln_b'], 'v_w_out': out['v_w_out'], 'v_ffn2_norm': out['v_ffn2_norm'], 'v_ffn2_w_gate': out['v_ffn2_w_gate'], 'v_ffn2_w_up': out['v_ffn2_w_up'], 'v_ffn2_w_down': out['v_ffn2_w_down'], 'v_final_norm': out['v_final_norm']}


def _loss(weights, diff, rest, loss_target):
    with _jax.named_scope("forward"):
        args = {**rest, TWIN_DIFF_INPUT: diff, **{k: w.astype(_WEIGHT_DTYPES[k]) for k, w in weights.items()}}
        y = _forward(args)
    with _jax.named_scope("loss_head"):
        err = _jnp.square(y.astype(_jnp.float32) - loss_target)
        return 0.5 * _jnp.sum(_jnp.mean(err, axis=-1)) if err.ndim else 0.5 * err


def _adamw(w, g, m, v):
    m = ADAM_B1 * m + (1.0 - ADAM_B1) * g
    v = ADAM_B2 * v + (1.0 - ADAM_B2) * _jnp.square(g)
    m_hat = m / (1.0 - ADAM_B1 ** ADAM_STEP)
    v_hat = v / (1.0 - ADAM_B2 ** ADAM_STEP)
    delta = -ADAM_LR * (m_hat / (_jnp.sqrt(v_hat) + ADAM_EPS) + ADAM_WD * w)
    return delta, m, v


def reference(x, ffn1_norm, ffn1_w_gate, ffn1_w_up, ffn1_w_down, mix_norm, w_in, s5_lam_re, s5_lam_im, s5_log_dt, s5_b_re, s5_b_im, s5_c_re, s5_c_im, s5_d, s5_w_glu, s5_b_glu, conv_w_dw, conv_b_dw, conv_ln_g, conv_ln_b, w_out, ffn2_norm, ffn2_w_gate, ffn2_w_up, ffn2_w_down, final_norm, loss_target, m_ffn1_norm, m_ffn1_w_gate, m_ffn1_w_up, m_ffn1_w_down, m_mix_norm, m_w_in, m_s5_lam_re, m_s5_lam_im, m_s5_log_dt, m_s5_b_re, m_s5_b_im, m_s5_c_re, m_s5_c_im, m_s5_d, m_s5_w_glu, m_s5_b_glu, m_conv_w_dw, m_conv_b_dw, m_conv_ln_g, m_conv_ln_b, m_w_out, m_ffn2_norm, m_ffn2_w_gate, m_ffn2_w_up, m_ffn2_w_down, m_final_norm, v_ffn1_norm, v_ffn1_w_gate, v_ffn1_w_up, v_ffn1_w_down, v_mix_norm, v_w_in, v_s5_lam_re, v_s5_lam_im, v_s5_log_dt, v_s5_b_re, v_s5_b_im, v_s5_c_re, v_s5_c_im, v_s5_d, v_s5_w_glu, v_s5_b_glu, v_conv_w_dw, v_conv_b_dw, v_conv_ln_g, v_conv_ln_b, v_w_out, v_ffn2_norm, v_ffn2_w_gate, v_ffn2_w_up, v_ffn2_w_down, v_final_norm):
    given = dict(x=x, ffn1_norm=ffn1_norm, ffn1_w_gate=ffn1_w_gate, ffn1_w_up=ffn1_w_up, ffn1_w_down=ffn1_w_down, mix_norm=mix_norm, w_in=w_in, s5_lam_re=s5_lam_re, s5_lam_im=s5_lam_im, s5_log_dt=s5_log_dt, s5_b_re=s5_b_re, s5_b_im=s5_b_im, s5_c_re=s5_c_re, s5_c_im=s5_c_im, s5_d=s5_d, s5_w_glu=s5_w_glu, s5_b_glu=s5_b_glu, conv_w_dw=conv_w_dw, conv_b_dw=conv_b_dw, conv_ln_g=conv_ln_g, conv_ln_b=conv_ln_b, w_out=w_out, ffn2_norm=ffn2_norm, ffn2_w_gate=ffn2_w_gate, ffn2_w_up=ffn2_w_up, ffn2_w_down=ffn2_w_down, final_norm=final_norm, loss_target=loss_target, m_ffn1_norm=m_ffn1_norm, m_ffn1_w_gate=m_ffn1_w_gate, m_ffn1_w_up=m_ffn1_w_up, m_ffn1_w_down=m_ffn1_w_down, m_mix_norm=m_mix_norm, m_w_in=m_w_in, m_s5_lam_re=m_s5_lam_re, m_s5_lam_im=m_s5_lam_im, m_s5_log_dt=m_s5_log_dt, m_s5_b_re=m_s5_b_re, m_s5_b_im=m_s5_b_im, m_s5_c_re=m_s5_c_re, m_s5_c_im=m_s5_c_im, m_s5_d=m_s5_d, m_s5_w_glu=m_s5_w_glu, m_s5_b_glu=m_s5_b_glu, m_conv_w_dw=m_conv_w_dw, m_conv_b_dw=m_conv_b_dw, m_conv_ln_g=m_conv_ln_g, m_conv_ln_b=m_conv_ln_b, m_w_out=m_w_out, m_ffn2_norm=m_ffn2_norm, m_ffn2_w_gate=m_ffn2_w_gate, m_ffn2_w_up=m_ffn2_w_up, m_ffn2_w_down=m_ffn2_w_down, m_final_norm=m_final_norm, v_ffn1_norm=v_ffn1_norm, v_ffn1_w_gate=v_ffn1_w_gate, v_ffn1_w_up=v_ffn1_w_up, v_ffn1_w_down=v_ffn1_w_down, v_mix_norm=v_mix_norm, v_w_in=v_w_in, v_s5_lam_re=v_s5_lam_re, v_s5_lam_im=v_s5_lam_im, v_s5_log_dt=v_s5_log_dt, v_s5_b_re=v_s5_b_re, v_s5_b_im=v_s5_b_im, v_s5_c_re=v_s5_c_re, v_s5_c_im=v_s5_c_im, v_s5_d=v_s5_d, v_s5_w_glu=v_s5_w_glu, v_s5_b_glu=v_s5_b_glu, v_conv_w_dw=v_conv_w_dw, v_conv_b_dw=v_conv_b_dw, v_conv_ln_g=v_conv_ln_g, v_conv_ln_b=v_conv_ln_b, v_w_out=v_w_out, v_ffn2_norm=v_ffn2_norm, v_ffn2_w_gate=v_ffn2_w_gate, v_ffn2_w_up=v_ffn2_w_up, v_ffn2_w_down=v_ffn2_w_down, v_final_norm=v_final_norm)
    weights = {n: given[n] for n in TWIN_WEIGHTS}
    shared = {n: given[n] for n in SHARED_INPUTS}
    per_example = {n: given[n] for n in ['x']}
    grad_fn = _jax.value_and_grad(_loss, argnums=(0, 1))

    def one_microbatch(ex, loss_target):
        ex = dict(ex)
        diff = ex.pop(TWIN_DIFF_INPUT)
        return grad_fn(weights, diff, {**shared, **ex}, loss_target)

    if N_MICROBATCH == 1:
        loss, (grad_w, grad_x) = one_microbatch(per_example, given["loss_target"])
    else:
        def body(carry, xs):
            loss_sum, grad_sum = carry
            l_k, (gw_k, gx_k) = one_microbatch(xs[0], xs[1])
            with _jax.named_scope("update"):
                return (loss_sum + l_k, _jax.tree.map(_jnp.add, grad_sum, gw_k)), gx_k

        init = (_jnp.zeros((), _jnp.float32), _jax.tree.map(_jnp.zeros_like, weights))
        (loss, grad_w), grad_x = _jax.lax.scan(body, init, (per_example, given["loss_target"]))
    with _jax.named_scope("update"):
        delta_w, new_m, new_v = {}, {}, {}
        for n in TWIN_WEIGHTS:
            delta_w[n], new_m[n], new_v[n] = _adamw(weights[n], grad_w[n], given["m_" + n], given["v_" + n])
    return (loss, grad_x, *[grad_w[n] for n in TWIN_WEIGHTS], *[delta_w[n] for n in TWIN_WEIGHTS],
            *[new_m[n] for n in TWIN_WEIGHTS], *[new_v[n] for n in TWIN_WEIGHTS])
```

```python
import functools

import jax
import jax.numpy as jnp
from jax import lax
from jax.experimental import pallas as pl
from jax.experimental.pallas import tpu as pltpu

F32, BF16 = jnp.float32, jnp.bfloat16
MESH = pl.DeviceIdType.MESH

EPS = 1e-6
ADAM_LR, ADAM_B1, ADAM_B2, ADAM_EPS, ADAM_WD, ADAM_STEP = 0.001, 0.9, 0.999, 1e-08, 0.01, 10

N_CHIPS = 4
N_DEV = 8
LANES = 128
S5_W, S5_G, S5_GC, S5_P = 512, 32, 16, 64
S5_N = S5_G * S5_P
S5_TILES = S5_N // LANES
CONV_W, CONV_K, CONV_HD = 512, 31, 64
HALO = 32
TM = 512
TS = 256
VMEM_LIMIT = 48 << 20
GELU_C0, GELU_C1 = 0.7978845608028654, 0.044715

SHARDED = ("ffn1_w_gate", "ffn1_w_up", "ffn1_w_down", "w_in", "s5_w_glu", "conv_w_dw", "w_out",
           "ffn2_w_gate", "ffn2_w_up", "ffn2_w_down")
SMALL = ("ffn1_norm", "mix_norm", "s5_lam_re", "s5_lam_im", "s5_log_dt", "s5_b_re", "s5_b_im", "s5_c_re",
         "s5_c_im", "s5_d", "s5_b_glu", "conv_b_dw", "conv_ln_g", "conv_ln_b", "ffn2_norm", "final_norm")
WEIGHTS = ("ffn1_norm", "ffn1_w_gate", "ffn1_w_up", "ffn1_w_down", "mix_norm", "w_in", "s5_lam_re", "s5_lam_im",
           "s5_log_dt", "s5_b_re", "s5_b_im", "s5_c_re", "s5_c_im", "s5_d", "s5_w_glu", "s5_b_glu", "conv_w_dw",
           "conv_b_dw", "conv_ln_g", "conv_ln_b", "w_out", "ffn2_norm", "ffn2_w_gate", "ffn2_w_up", "ffn2_w_down",
           "final_norm")


def _cparams(*sem):
    return pltpu.CompilerParams(dimension_semantics=sem, vmem_limit_bytes=VMEM_LIMIT)


def _dot(a, b):
    return jnp.dot(a, b, preferred_element_type=F32)


def _dot_nt(a, b):
    return lax.dot_general(a, b, (((1,), (1,)), ((), ())), preferred_element_type=F32)


def _dot_tn(a, b):
    return lax.dot_general(a, b, (((0,), (0,)), ((), ())), preferred_element_type=F32)


def _dot_hi(a, b):
    return jnp.dot(a, b, precision=lax.Precision.HIGHEST, preferred_element_type=F32)


def _colsum(v):
    return jnp.sum(v, axis=0, keepdims=True)


def _sigmoid(v):
    return 1.0 / (1.0 + jnp.exp(-v))


def _accumulate(ref, first, value):
    @pl.when(first)
    def _():
        ref[...] = value

    @pl.when(jnp.logical_not(first))
    def _():
        ref[...] += value


def _row_tile(rows, cols, itemsize=4, budget=1 << 20):
    t = rows
    while t % 16 == 0 and t * cols * itemsize > budget:
        t //= 2
    return t


def _cast_bf16(w, name):
    rows, cols = w.shape
    tr = _row_tile(rows, cols)

    def body(w_ref, o_ref):
        o_ref[...] = w_ref[...].astype(BF16)

    return pl.pallas_call(
        body, name=name, grid=(rows // tr,),
        in_specs=[pl.BlockSpec((tr, cols), lambda i: (i, 0))],
        out_specs=pl.BlockSpec((tr, cols), lambda i: (i, 0)),
        out_shape=jax.ShapeDtypeStruct((rows, cols), BF16), compiler_params=_cparams("parallel"))(w)


def _rms_fwd(x, g, name):
    L, D = x.shape

    def body(x_ref, g_ref, h_ref):
        xf = x_ref[...]
        r = lax.rsqrt(jnp.mean(xf * xf, axis=-1, keepdims=True) + EPS)
        h_ref[...] = (xf * r * g_ref[...]).astype(BF16)

    return pl.pallas_call(
        body, name=name, grid=(L // TM,),
        in_specs=[pl.BlockSpec((TM, D), lambda i: (i, 0)), pl.BlockSpec((1, D), lambda i: (0, 0))],
        out_specs=pl.BlockSpec((TM, D), lambda i: (i, 0)),
        out_shape=jax.ShapeDtypeStruct((L, D), BF16), compiler_params=_cparams("parallel"))(x, g)


def _ffn_up(h, wg, wu, name):
    L, D = h.shape
    G, _, FS = wg.shape

    def body(h_ref, wg_ref, wu_ref, a_ref, b_ref, act_ref):
        hv = h_ref[...]
        a = _dot(hv, wg_ref[...])
        b = _dot(hv, wu_ref[...])
        a_ref[...] = a.astype(BF16)
        b_ref[...] = b.astype(BF16)
        act_ref[...] = (a * _sigmoid(a) * b).astype(BF16)

    wspec = pl.BlockSpec((None, D, FS), lambda j, i: (j, 0, 0))
    ospec = pl.BlockSpec((None, TM, FS), lambda j, i: (j, i, 0))
    oshape = jax.ShapeDtypeStruct((G, L, FS), BF16)
    return pl.pallas_call(
        body, name=name, grid=(G, L // TM),
        in_specs=[pl.BlockSpec((TM, D), lambda j, i: (i, 0)), wspec, wspec],
        out_specs=[ospec, ospec, ospec], out_shape=[oshape, oshape, oshape],
        compiler_params=_cparams("parallel", "parallel"))(h, wg, wu)


def _ffn_down(act, wd, x, g_next, name):
    G, L, FS = act.shape
    D = wd.shape[2]

    def body(act_ref, wd_ref, x_ref, g_ref, xn_ref, hn_ref, acc):
        j = pl.program_id(1)
        _accumulate(acc, j == 0, _dot(act_ref[...], wd_ref[...]))

        @pl.when(j == G - 1)
        def _():
            xn = x_ref[...] + 0.5 * acc[...]
            xn_ref[...] = xn
            r = lax.rsqrt(jnp.mean(xn * xn, axis=-1, keepdims=True) + EPS)
            hn_ref[...] = (xn * r * g_ref[...]).astype(BF16)

    row = pl.BlockSpec((TM, D), lambda i, j: (i, 0))
    return pl.pallas_call(
        body, name=name, grid=(L // TM, G),
        in_specs=[pl.BlockSpec((None, TM, FS), lambda i, j: (j, i, 0)),
                  pl.BlockSpec((None, FS, D), lambda i, j: (j, 0, 0)), row,
                  pl.BlockSpec((1, D), lambda i, j: (0, 0))],
        out_specs=[row, row],
        out_shape=[jax.ShapeDtypeStruct((L, D), F32), jax.ShapeDtypeStruct((L, D), BF16)],
        scratch_shapes=[pltpu.VMEM((TM, D), F32)],
        compiler_params=_cparams("parallel", "arbitrary"))(act, wd, x, g_next)


def _ffn_down_loss(act, wd, x, gf, tgt, name):
    G, L, FS = act.shape
    D = wd.shape[2]

    def body(act_ref, wd_ref, x_ref, g_ref, t_ref, dx_ref, loss_ref, dg_ref, acc):
        i, j = pl.program_id(0), pl.program_id(1)
        _accumulate(acc, j == 0, _dot(act_ref[...], wd_ref[...]))

        @pl.when(j == G - 1)
        def _():
            xn = x_ref[...] + 0.5 * acc[...]
            r = lax.rsqrt(jnp.mean(xn * xn, axis=-1, keepdims=True) + EPS)
            xh = xn * r
            gv = g_ref[...]
            e = xh * gv - t_ref[...]
            part = 0.5 * jnp.sum(_colsum(e * e), axis=1, keepdims=True) / D
            dy = e / D
            _accumulate(loss_ref, i == 0, jnp.broadcast_to(part, (1, LANES)))
            _accumulate(dg_ref, i == 0, _colsum(dy * xh))
            dxh = dy * gv
            dx_ref[...] = r * (dxh - xh * jnp.mean(dxh * xh, axis=-1, keepdims=True))

    row = pl.BlockSpec((TM, D), lambda i, j: (i, 0))
    return pl.pallas_call(
        body, name=name, grid=(L // TM, G),
        in_specs=[pl.BlockSpec((None, TM, FS), lambda i, j: (j, i, 0)),
                  pl.BlockSpec((None, FS, D), lambda i, j: (j, 0, 0)), row,
                  pl.BlockSpec((1, D), lambda i, j: (0, 0)), row],
        out_specs=[row, pl.BlockSpec((1, LANES), lambda i, j: (0, 0)), pl.BlockSpec((1, D), lambda i, j: (0, 0))],
        out_shape=[jax.ShapeDtypeStruct((L, D), F32), jax.ShapeDtypeStruct((1, LANES), F32),
                   jax.ShapeDtypeStruct((1, D), F32)],
        scratch_shapes=[pltpu.VMEM((TM, D), F32)],
        compiler_params=_cparams("arbitrary", "arbitrary"))(act, wd, x, gf, tgt)


def _mm_nn(a, b, tn, out_dtype, name):
    L, K = a.shape
    N = b.shape[1]

    def body(a_ref, b_ref, o_ref):
        o_ref[...] = _dot(a_ref[...], b_ref[...]).astype(o_ref.dtype)

    return pl.pallas_call(
        body, name=name, grid=(N // tn, L // TM),
        in_specs=[pl.BlockSpec((TM, K), lambda n, i: (i, 0)), pl.BlockSpec((K, tn), lambda n, i: (0, n))],
        out_specs=pl.BlockSpec((TM, tn), lambda n, i: (i, n)),
        out_shape=jax.ShapeDtypeStruct((L, N), out_dtype), compiler_params=_cparams("parallel", "parallel"))(a, b)


def _mm_nt(a, w, name):
    L, K = a.shape
    N = w.shape[0]

    def body(a_ref, w_ref, o_ref):
        o_ref[...] = _dot_nt(a_ref[...].astype(BF16), w_ref[...]).astype(BF16)

    return pl.pallas_call(
        body, name=name, grid=(L // TM,),
        in_specs=[pl.BlockSpec((TM, K), lambda i: (i, 0)), pl.BlockSpec((N, K), lambda i: (0, 0))],
        out_specs=pl.BlockSpec((TM, N), lambda i: (i, 0)),
        out_shape=jax.ShapeDtypeStruct((L, N), BF16), compiler_params=_cparams("parallel"))(a, w)


def _mm_tn(a, b, scale, name, out_dtype=BF16):
    Ga, L, M = a.shape
    Gb, _, N = b.shape
    G = max(Ga, Gb)
    nk = L // TM

    def body(a_ref, b_ref, o_ref, acc):
        k = pl.program_id(1)
        _accumulate(acc, k == 0, _dot_tn(a_ref[...].astype(BF16), b_ref[...].astype(BF16)))

        @pl.when(k == nk - 1)
        def _():
            o_ref[...] = (acc[...] * scale).astype(o_ref.dtype)

    return pl.pallas_call(
        body, name=name, grid=(G, nk),
        in_specs=[pl.BlockSpec((None, TM, M), (lambda g, k: (g, k, 0)) if Ga > 1 else (lambda g, k: (0, k, 0))),
                  pl.BlockSpec((None, TM, N), (lambda g, k: (g, k, 0)) if Gb > 1 else (lambda g, k: (0, k, 0)))],
        out_specs=pl.BlockSpec((None, M, N), lambda g, k: (g, 0, 0)),
        out_shape=jax.ShapeDtypeStruct((G, M, N), out_dtype),
        scratch_shapes=[pltpu.VMEM((M, N), F32)],
        compiler_params=_cparams("parallel", "arbitrary"))(a, b)


def _mm_tn2(a, b1, b2, name):
    L, M = a.shape
    G, _, N = b1.shape
    nk = L // TM

    def body(a_ref, b1_ref, b2_ref, o1_ref, o2_ref, acc1, acc2):
        k = pl.program_id(1)
        av = a_ref[...]
        _accumulate(acc1, k == 0, _dot_tn(av, b1_ref[...]))
        _accumulate(acc2, k == 0, _dot_tn(av, b2_ref[...]))

        @pl.when(k == nk - 1)
        def _():
            o1_ref[...] = acc1[...].astype(BF16)
            o2_ref[...] = acc2[...].astype(BF16)

    bspec = pl.BlockSpec((None, TM, N), lambda g, k: (g, k, 0))
    ospec = pl.BlockSpec((None, M, N), lambda g, k: (g, 0, 0))
    oshape = jax.ShapeDtypeStruct((G, M, N), BF16)
    return pl.pallas_call(
        body, name=name, grid=(G, nk),
        in_specs=[pl.BlockSpec((TM, M), lambda g, k: (k, 0)), bspec, bspec],
        out_specs=[ospec, ospec], out_shape=[oshape, oshape],
        scratch_shapes=[pltpu.VMEM((M, N), F32), pltpu.VMEM((M, N), F32)],
        compiler_params=_cparams("parallel", "arbitrary"))(a, b1, b2)


def _mm_nt_rmsbwd(a_list, w_list, x_in, g, dx_out, name):
    P = len(a_list)
    G, L, _ = a_list[0].shape
    D = x_in.shape[1]

    def body(*refs):
        a_refs, w_refs = refs[:P], refs[P:2 * P]
        x_ref, g_ref, dxo_ref, dx_ref, dg_ref, acc = refs[2 * P:]
        i, j = pl.program_id(0), pl.program_id(1)
        p = _dot_nt(a_refs[0][...], w_refs[0][...])
        for a_ref, w_ref in zip(a_refs[1:], w_refs[1:]):
            p = p + _dot_nt(a_ref[...], w_ref[...])
        _accumulate(acc, j == 0, p)

        @pl.when(j == G - 1)
        def _():
            dh = acc[...]
            xf = x_ref[...]
            r = lax.rsqrt(jnp.mean(xf * xf, axis=-1, keepdims=True) + EPS)
            xh = xf * r
            _accumulate(dg_ref, i == 0, _colsum(dh * xh))
            dxh = dh * g_ref[...]
            dx_ref[...] = dxo_ref[...] + r * (dxh - xh * jnp.mean(dxh * xh, axis=-1, keepdims=True))

    row = pl.BlockSpec((TM, D), lambda i, j: (i, 0))
    vec = pl.BlockSpec((1, D), lambda i, j: (0, 0))
    in_specs = [pl.BlockSpec((None, TM, a.shape[2]), lambda i, j: (j, i, 0)) for a in a_list]
    in_specs += [pl.BlockSpec((None, D, w.shape[2]), lambda i, j: (j, 0, 0)) for w in w_list]
    return pl.pallas_call(
        body, name=name, grid=(L // TM, G),
        in_specs=in_specs + [row, vec, row], out_specs=[row, vec],
        out_shape=[jax.ShapeDtypeStruct((L, D), F32), jax.ShapeDtypeStruct((1, D), F32)],
        scratch_shapes=[pltpu.VMEM((TM, D), F32)],
        compiler_params=_cparams("arbitrary", "arbitrary"))(*a_list, *w_list, x_in, g, dx_out)


def _ffn_bwd_act(dx_out, wd, a, b, name):
    L, D = dx_out.shape
    G, FS, _ = wd.shape

    def body(dx_ref, wd_ref, a_ref, b_ref, da_ref, db_ref):
        dact = 0.5 * _dot_nt(dx_ref[...].astype(BF16), wd_ref[...])
        av = a_ref[...].astype(F32)
        bv = b_ref[...].astype(F32)
        sg = _sigmoid(av)
        da_ref[...] = (dact * bv * sg * (1.0 + av * (1.0 - sg))).astype(BF16)
        db_ref[...] = (dact * av * sg).astype(BF16)

    gspec = pl.BlockSpec((None, TM, FS), lambda j, i: (j, i, 0))
    oshape = jax.ShapeDtypeStruct((G, L, FS), BF16)
    return pl.pallas_call(
        body, name=name, grid=(G, L // TM),
        in_specs=[pl.BlockSpec((TM, D), lambda j, i: (i, 0)), pl.BlockSpec((None, FS, D), lambda j, i: (j, 0, 0)),
                  gspec, gspec],
        out_specs=[gspec, gspec], out_shape=[oshape, oshape],
        compiler_params=_cparams("parallel", "parallel"))(dx_out, wd, a, b)


def _mix_out(o_s5, o_conv, wout, x1, g_next, name):
    L, H = o_s5.shape
    D = wout.shape[1]

    def body(s_ref, c_ref, w_ref, x_ref, g_ref, xn_ref, hn_ref):
        xn = x_ref[...] + _dot(s_ref[...], w_ref[0:H, :]) + _dot(c_ref[...], w_ref[H:2 * H, :])
        xn_ref[...] = xn
        r = lax.rsqrt(jnp.mean(xn * xn, axis=-1, keepdims=True) + EPS)
        hn_ref[...] = (xn * r * g_ref[...]).astype(BF16)

    half = pl.BlockSpec((TM, H), lambda i: (i, 0))
    row = pl.BlockSpec((TM, D), lambda i: (i, 0))
    return pl.pallas_call(
        body, name=name, grid=(L // TM,),
        in_specs=[half, half, pl.BlockSpec((2 * H, D), lambda i: (0, 0)), row, pl.BlockSpec((1, D), lambda i: (0, 0))],
        out_specs=[row, row],
        out_shape=[jax.ShapeDtypeStruct((L, D), F32), jax.ShapeDtypeStruct((L, D), BF16)],
        compiler_params=_cparams("parallel"))(o_s5, o_conv, wout, x1, g_next)


def _s5_disc(lr, li, ldt, brc, bic):
    dt = jnp.exp(ldt)
    mag = jnp.exp(lr * dt)
    are = mag * jnp.cos(li * dt)
    aim = mag * jnp.sin(li * dt)
    den = lr * lr + li * li
    nre = are - 1.0
    fre = (nre * lr + aim * li) / den
    fim = (aim * lr - nre * li) / den
    return are, aim, fre * brc - fim * bic, fre * bic + fim * brc


def _s5_params_fwd(lr, li, ldt, brc, bic, crc, cic):
    def body(lr_ref, li_ref, ldt_ref, br_ref, bi_ref, cr_ref, ci_ref, are_ref, aim_ref, bre_ref, bim_ref, cre_ref, cim_ref):
        are, aim, bre, bim = _s5_disc(lr_ref[...], li_ref[...], ldt_ref[...], br_ref[...], bi_ref[...])
        are_ref[...] = are
        aim_ref[...] = aim
        bre_ref[...] = bre.astype(BF16)
        bim_ref[...] = bim.astype(BF16)
        cre_ref[...] = cr_ref[...].astype(BF16)
        cim_ref[...] = ci_ref[...].astype(BF16)

    vec = jax.ShapeDtypeStruct((1, S5_N), F32)
    return pl.pallas_call(
        body, name="s5_params_fwd",
        out_shape=[vec, vec, jax.ShapeDtypeStruct((LANES, S5_N), BF16), jax.ShapeDtypeStruct((LANES, S5_N), BF16),
                   jax.ShapeDtypeStruct((S5_N, LANES), BF16), jax.ShapeDtypeStruct((S5_N, LANES), BF16)],
        compiler_params=pltpu.CompilerParams(vmem_limit_bytes=VMEM_LIMIT))(lr, li, ldt, brc, bic, crc, cic)


def _s5_params_bwd(lr, li, ldt, brc, bic, dare, daim, dbre, dbim):
    def body(lr_ref, li_ref, ldt_ref, br_ref, bi_ref, dare_ref, daim_ref, dbre_ref, dbim_ref,
             glr_ref, gli_ref, gldt_ref, gbr_ref, gbi_ref):
        _, vjp = jax.vjp(_s5_disc, lr_ref[...], li_ref[...], ldt_ref[...], br_ref[...], bi_ref[...])
        glr, gli, gldt, gbr, gbi = vjp((dare_ref[...], daim_ref[...], dbre_ref[...], dbim_ref[...]))
        glr_ref[...] = glr
        gli_ref[...] = gli
        gldt_ref[...] = gldt
        gbr_ref[...] = gbr
        gbi_ref[...] = gbi

    vec = jax.ShapeDtypeStruct((1, S5_N), F32)
    mat = jax.ShapeDtypeStruct((LANES, S5_N), F32)
    return pl.pallas_call(
        body, name="s5_params_bwd", out_shape=[vec, vec, vec, mat, mat],
        compiler_params=pltpu.CompilerParams(vmem_limit_bytes=VMEM_LIMIT))(lr, li, ldt, brc, bic, dare, daim, dbre, dbim)


def _gelu_parts(y):
    th = jnp.tanh(GELU_C0 * (y + GELU_C1 * y * y * y))
    return 0.5 * y * (1.0 + th), th


def _state_rows(q, T):
    return pl.ds(q, T, stride=S5_TILES)


def _s5_fwd(u, are, aim, bre, bim, cre, cim, d_skip, wglu, bglu):
    L = u.shape[0]
    T = min(TS, L)
    n = L // T

    def body(u_ref, are_ref, aim_ref, bre_ref, bim_ref, cre_ref, cim_ref, d_ref, wg_ref, bg_ref,
             sre_ref, sim_ref, y_ref, o_ref, st_re, st_im):
        i = pl.program_id(0)

        @pl.when(i == 0)
        def _():
            st_re[...] = jnp.zeros_like(st_re)
            st_im[...] = jnp.zeros_like(st_im)

        uf = u_ref[...]
        ub = uf.astype(BF16)
        for q in range(S5_TILES):
            ct = q // 4
            uq = ub[:, ct * LANES:(ct + 1) * LANES]
            sre_ref[_state_rows(q, T), :] = _dot(uq, bre_ref[:, q * LANES:(q + 1) * LANES])
            sim_ref[_state_rows(q, T), :] = _dot(uq, bim_ref[:, q * LANES:(q + 1) * LANES])
        a_re, a_im = are_ref[...], aim_ref[...]

        def step(t, carry):
            s_re, s_im = carry
            rows = pl.ds(pl.multiple_of(t * S5_TILES, S5_TILES), S5_TILES)
            n_re = a_re * s_re - a_im * s_im + sre_ref[rows, :]
            n_im = a_re * s_im + a_im * s_re + sim_ref[rows, :]
            sre_ref[rows, :] = n_re
            sim_ref[rows, :] = n_im
            return n_re, n_im

        s_re, s_im = lax.fori_loop(0, T, step, (st_re[...], st_im[...]), unroll=8)
        st_re[...] = s_re
        st_im[...] = s_im
        tiles = []
        for ct in range(4):
            acc = jnp.zeros((T, LANES), F32)
            for q in range(4 * ct, 4 * ct + 4):
                acc = acc + _dot(sre_ref[_state_rows(q, T), :].astype(BF16), cre_ref[q * LANES:(q + 1) * LANES, :])
                acc = acc - _dot(sim_ref[_state_rows(q, T), :].astype(BF16), cim_ref[q * LANES:(q + 1) * LANES, :])
            tiles.append(acc)
        y = jnp.concatenate(tiles, axis=1) + d_ref[...] * uf
        y_ref[...] = y
        yg, _ = _gelu_parts(y)
        gate = _sigmoid(_dot(yg.astype(BF16), wg_ref[...]) + bg_ref[...])
        o_ref[...] = (yg * gate).astype(BF16)

    const = lambda shape: pl.BlockSpec(shape, lambda i: (0, 0))
    sspec = pl.BlockSpec((T * S5_TILES, LANES), lambda i: (i, 0))
    return pl.pallas_call(
        body, name="s5_fwd", grid=(n,),
        in_specs=[pl.BlockSpec((T, S5_W), lambda i: (i, 0)), const((S5_TILES, LANES)), const((S5_TILES, LANES)),
                  const((LANES, S5_N)), const((LANES, S5_N)), const((S5_N, LANES)), const((S5_N, LANES)),
                  const((1, S5_W)), const((S5_W, S5_W)), const((1, S5_W))],
        out_specs=[sspec, sspec, pl.BlockSpec((T, S5_W), lambda i: (i, 0)), pl.BlockSpec((T, S5_W), lambda i: (i, 0))],
        out_shape=[jax.ShapeDtypeStruct((L * S5_TILES, LANES), F32), jax.ShapeDtypeStruct((L * S5_TILES, LANES), F32),
                   jax.ShapeDtypeStruct((L, S5_W), F32), jax.ShapeDtypeStruct((L, S5_W), BF16)],
        scratch_shapes=[pltpu.VMEM((S5_TILES, LANES), F32), pltpu.VMEM((S5_TILES, LANES), F32)],
        compiler_params=_cparams("arbitrary"))(u, are, aim, bre, bim, cre, cim, d_skip, wglu, bglu)


def _s5_bwd(dm, y_pre, u, s_re, s_im, are, aim, bre, bim, cre, cim, d_skip, wglu, bglu):
    L = u.shape[0]
    T = min(TS, L)
    n = L // T

    def body(dm_ref, y_ref, u_ref, sre_ref, sim_ref, spre_ref, spim_ref, are_ref, aim_ref, bre_ref, bim_ref,
             cre_ref, cim_ref, d_ref, wg_ref, bg_ref,
             du_ref, dwg_ref, dbg_ref, dd_ref, dcre_ref, dcim_ref, dbre_ref, dbim_ref, dare_ref, daim_ref,
             gre, gim, car_re, car_im):
        i = pl.program_id(0)
        first = i == 0

        @pl.when(first)
        def _():
            car_re[...] = jnp.zeros_like(car_re)
            car_im[...] = jnp.zeros_like(car_im)
            dcre_ref[...] = jnp.zeros_like(dcre_ref)
            dcim_ref[...] = jnp.zeros_like(dcim_ref)
            dbre_ref[...] = jnp.zeros_like(dbre_ref)
            dbim_ref[...] = jnp.zeros_like(dbim_ref)

        y = y_ref[...]
        uf = u_ref[...]
        yg, th = _gelu_parts(y)
        dgelu = 0.5 * (1.0 + th) + 0.5 * y * (1.0 - th * th) * GELU_C0 * (1.0 + 3.0 * GELU_C1 * y * y)
        ygb = yg.astype(BF16)
        sg = _sigmoid(_dot(ygb, wg_ref[...]) + bg_ref[...])
        dout = dm_ref[...].astype(F32)
        dgp = dout * yg * sg * (1.0 - sg)
        dgpb = dgp.astype(BF16)
        dyg = dout * sg + _dot_nt(dgpb, wg_ref[...])
        _accumulate(dwg_ref, first, _dot_tn(ygb, dgpb))
        _accumulate(dbg_ref, first, _colsum(dgp))
        dy = dyg * dgelu
        _accumulate(dd_ref, first, _colsum(dy * uf))
        dyb = dy.astype(BF16)
        ub = uf.astype(BF16)

        for q in range(S5_TILES):
            ct = q // 4
            win = slice(q * LANES, (q + 1) * LANES)
            dyq = dyb[:, ct * LANES:(ct + 1) * LANES]
            dcre_ref[win, :] += _dot_tn(sre_ref[_state_rows(q, T), :].astype(BF16), dyq)
            dcim_ref[win, :] -= _dot_tn(sim_ref[_state_rows(q, T), :].astype(BF16), dyq)
            gre[_state_rows(q, T), :] = _dot_nt(dyq, cre_ref[win, :])
            gim[_state_rows(q, T), :] = -_dot_nt(dyq, cim_ref[win, :])

        a_re, a_im = are_ref[...], aim_ref[...]

        def adjoint(t, g_re, g_im):
            rows = pl.ds(pl.multiple_of(t * S5_TILES, S5_TILES), S5_TILES)
            n_re = gre[rows, :] + a_re * g_re + a_im * g_im
            n_im = gim[rows, :] + a_re * g_im - a_im * g_re
            gre[rows, :] = n_re
            gim[rows, :] = n_im
            return n_re, n_im

        def step(k, carry):
            g_re, g_im, da_re, da_im = carry
            t = T - 1 - k
            g_re, g_im = adjoint(t, g_re, g_im)
            prev = pl.ds(pl.multiple_of((t - 1) * S5_TILES, S5_TILES), S5_TILES)
            p_re, p_im = sre_ref[prev, :], sim_ref[prev, :]
            return g_re, g_im, da_re + g_re * p_re + g_im * p_im, da_im + g_im * p_re - g_re * p_im

        zero = jnp.zeros((S5_TILES, LANES), F32)
        g_re, g_im, da_re, da_im = lax.fori_loop(0, T - 1, step, (car_re[...], car_im[...], zero, zero), unroll=8)
        g_re, g_im = adjoint(0, g_re, g_im)
        keep = (i < n - 1).astype(F32)
        p_re, p_im = spre_ref[...] * keep, spim_ref[...] * keep
        car_re[...] = g_re
        car_im[...] = g_im
        _accumulate(dare_ref, first, da_re + g_re * p_re + g_im * p_im)
        _accumulate(daim_ref, first, da_im + g_im * p_re - g_re * p_im)

        tiles = []
        for ct in range(4):
            uq = ub[:, ct * LANES:(ct + 1) * LANES]
            acc = d_ref[:, ct * LANES:(ct + 1) * LANES] * dy[:, ct * LANES:(ct + 1) * LANES]
            for q in range(4 * ct, 4 * ct + 4):
                win = slice(q * LANES, (q + 1) * LANES)
                gq_re = gre[_state_rows(q, T), :].astype(BF16)
                gq_im = gim[_state_rows(q, T), :].astype(BF16)
                acc = acc + _dot_nt(gq_re, bre_ref[:, win]) + _dot_nt(gq_im, bim_ref[:, win])
                dbre_ref[:, win] += _dot_tn(uq, gq_re)
                dbim_ref[:, win] += _dot_tn(uq, gq_im)
            tiles.append(acc)
        du_ref[...] = jnp.concatenate(tiles, axis=1).astype(BF16)

    rev = lambda i: (n - 1 - i, 0)
    const = lambda shape: pl.BlockSpec(shape, lambda i: (0, 0))
    chunk = pl.BlockSpec((T, S5_W), rev)
    sspec = pl.BlockSpec((T * S5_TILES, LANES), rev)
    pspec = pl.BlockSpec((S5_TILES, LANES), lambda i: (jnp.maximum((n - 1 - i) * T - 1, 0), 0))
    tile = jax.ShapeDtypeStruct((S5_TILES, LANES), F32)
    vec = jax.ShapeDtypeStruct((1, S5_W), F32)
    return pl.pallas_call(
        body, name="s5_bwd", grid=(n,),
        in_specs=[chunk, chunk, chunk, sspec, sspec, pspec, pspec, const((S5_TILES, LANES)), const((S5_TILES, LANES)),
                  const((LANES, S5_N)), const((LANES, S5_N)), const((S5_N, LANES)), const((S5_N, LANES)),
                  const((1, S5_W)), const((S5_W, S5_W)), const((1, S5_W))],
        out_specs=[chunk, const((S5_W, S5_W)), const((1, S5_W)), const((1, S5_W)), const((S5_N, LANES)),
                   const((S5_N, LANES)), const((LANES, S5_N)), const((LANES, S5_N)), const((S5_TILES, LANES)),
                   const((S5_TILES, LANES))],
        out_shape=[jax.ShapeDtypeStruct((L, S5_W), BF16), jax.ShapeDtypeStruct((S5_W, S5_W), F32), vec, vec,
                   jax.ShapeDtypeStruct((S5_N, LANES), F32), jax.ShapeDtypeStruct((S5_N, LANES), F32),
                   jax.ShapeDtypeStruct((LANES, S5_N), F32), jax.ShapeDtypeStruct((LANES, S5_N), F32), tile, tile],
        scratch_shapes=[pltpu.VMEM((T * S5_TILES, LANES), F32), pltpu.VMEM((T * S5_TILES, LANES), F32),
                        pltpu.VMEM((S5_TILES, LANES), F32), pltpu.VMEM((S5_TILES, LANES), F32)],
        compiler_params=_cparams("arbitrary"))(dm, y_pre, u, s_re, s_im, s_re, s_im, are, aim, bre, bim, cre, cim,
                                               d_skip, wglu, bglu)


def _compact_b(b):
    dense = jnp.einsum("gpc,gh->gchp", b, jnp.eye(S5_G, dtype=b.dtype)).reshape(S5_W, S5_N)
    return jnp.concatenate([dense[(q // 4) * LANES:(q // 4 + 1) * LANES, q * LANES:(q + 1) * LANES]
                            for q in range(S5_TILES)], axis=1)


def _uncompact_b(m):
    zero = jnp.zeros((LANES, LANES), m.dtype)
    dense = jnp.concatenate([jnp.concatenate([m[:, q * LANES:(q + 1) * LANES] if q // 4 == ct else zero
                                              for q in range(S5_TILES)], axis=1) for ct in range(4)], axis=0)
    return jnp.einsum("gcgp->gpc", dense.reshape(S5_G, S5_GC, S5_G, S5_P))


def _head_norm(zc, mavg):
    mu = _dot_hi(zc, mavg)
    d = zc - mu
    rstd = lax.rsqrt(_dot_hi(d * d, mavg) + EPS)
    return d * rstd, rstd


def _conv_fwd(u, wdw, bdw, lng, lnb, mavg):
    L = u.shape[0]
    T = min(TS, L)
    n = L // T

    def body(v1_ref, v2_ref, w_ref, b_ref, g_ref, be_ref, m_ref, zc_ref, o_ref, zbuf):
        i = pl.program_id(0)

        @pl.when(i == 0)
        def _():
            zbuf[0:HALO, :] = jnp.zeros((HALO, CONV_W), F32)

        zbuf[HALO:HALO + T, :] = v1_ref[...] * _sigmoid(v2_ref[...])
        acc = jnp.broadcast_to(b_ref[...], (T, CONV_W))
        for k in range(CONV_K):
            acc = acc + w_ref[k:k + 1, :] * zbuf[pl.ds(HALO - (CONV_K - 1) + k, T), :]
        zc_ref[...] = acc
        zbuf[0:HALO, :] = zbuf[T:T + HALO, :]
        zn, _ = _head_norm(acc, m_ref[...])
        zz = zn * g_ref[...] + be_ref[...]
        o_ref[...] = (zz * _sigmoid(zz)).astype(BF16)

    const = lambda shape: pl.BlockSpec(shape, lambda i: (0, 0))
    vec = const((1, CONV_W))
    return pl.pallas_call(
        body, name="conv_fwd", grid=(n,),
        in_specs=[pl.BlockSpec((T, CONV_W), lambda i: (i, 1)), pl.BlockSpec((T, CONV_W), lambda i: (i, 2)),
                  const((HALO, CONV_W)), vec, vec, vec, const((CONV_W, CONV_W))],
        out_specs=[pl.BlockSpec((T, CONV_W), lambda i: (i, 0)), pl.BlockSpec((T, CONV_W), lambda i: (i, 0))],
        out_shape=[jax.ShapeDtypeStruct((L, CONV_W), F32), jax.ShapeDtypeStruct((L, CONV_W), BF16)],
        scratch_shapes=[pltpu.VMEM((T + HALO, CONV_W), F32)],
        compiler_params=_cparams("arbitrary"))(u, u, wdw, bdw, lng, lnb, mavg)


def _conv_bwd(dm, zc, u, wdw, lng, lnb, mavg):
    L = u.shape[0]
    T = min(TS, L)
    n = L // T
    hb = T // HALO

    def body(dm_ref, zc_ref, v1_ref, v2_ref, p1_ref, p2_ref, w_ref, g_ref, be_ref, m_ref,
             dv_ref, dw_ref, db_ref, dg_ref, dbe_ref, zbuf, dzbuf, head):
        i = pl.program_id(0)
        first = i == 0

        @pl.when(first)
        def _():
            head[...] = jnp.zeros_like(head)
            dw_ref[...] = jnp.zeros_like(dw_ref)

        zn, rstd = _head_norm(zc_ref[...], m_ref[...])
        zz = zn * g_ref[...] + be_ref[...]
        sg = _sigmoid(zz)
        dzz = dm_ref[...].astype(F32) * sg * (1.0 + zz * (1.0 - sg))
        _accumulate(dbe_ref, first, _colsum(dzz))
        _accumulate(dg_ref, first, _colsum(dzz * zn))
        dzn = dzz * g_ref[...]
        dzc = rstd * (dzn - _dot_hi(dzn, m_ref[...]) - zn * _dot_hi(dzn * zn, m_ref[...]))
        _accumulate(db_ref, first, _colsum(dzc))

        dzbuf[0:T, :] = dzc
        dzbuf[T:T + HALO, :] = head[...]
        head[...] = dzc[0:HALO, :]
        keep = (i < n - 1).astype(F32)
        v1 = v1_ref[...]
        sg2 = _sigmoid(v2_ref[...])
        zbuf[0:HALO, :] = p1_ref[...] * _sigmoid(p2_ref[...]) * keep
        zbuf[HALO:HALO + T, :] = v1 * sg2

        dz = jnp.zeros((T, CONV_W), F32)
        for k in range(CONV_K):
            prod = dzc * zbuf[pl.ds(HALO - (CONV_K - 1) + k, T), :]
            dw_ref[8 * k:8 * k + 8, :] += jnp.sum(prod.reshape(T // 8, 8, CONV_W), axis=0)
            dz = dz + w_ref[k:k + 1, :] * dzbuf[pl.ds(CONV_K - 1 - k, T), :]
        dv_ref[:, 0:CONV_W] = (dz * sg2).astype(BF16)
        dv_ref[:, CONV_W:2 * CONV_W] = (dz * v1 * sg2 * (1.0 - sg2)).astype(BF16)

    rev = lambda c: (lambda i: (n - 1 - i, c))
    prev = lambda c: (lambda i: (jnp.maximum((n - 1 - i) * hb - 1, 0), c))
    const = lambda shape: pl.BlockSpec(shape, lambda i: (0, 0))
    vec = const((1, CONV_W))
    vshape = jax.ShapeDtypeStruct((1, CONV_W), F32)
    return pl.pallas_call(
        body, name="conv_bwd", grid=(n,),
        in_specs=[pl.BlockSpec((T, CONV_W), rev(1)), pl.BlockSpec((T, CONV_W), rev(0)),
                  pl.BlockSpec((T, CONV_W), rev(1)), pl.BlockSpec((T, CONV_W), rev(2)),
                  pl.BlockSpec((HALO, CONV_W), prev(1)), pl.BlockSpec((HALO, CONV_W), prev(2)),
                  const((HALO, CONV_W)), vec, vec, const((CONV_W, CONV_W))],
        out_specs=[pl.BlockSpec((T, 2 * CONV_W), rev(0)), const((8 * HALO, CONV_W)), vec, vec, vec],
        out_shape=[jax.ShapeDtypeStruct((L, 2 * CONV_W), BF16), jax.ShapeDtypeStruct((8 * HALO, CONV_W), F32),
                   vshape, vshape, vshape],
        scratch_shapes=[pltpu.VMEM((T + HALO, CONV_W), F32), pltpu.VMEM((T + HALO, CONV_W), F32),
                        pltpu.VMEM((HALO, CONV_W), F32)],
        compiler_params=_cparams("arbitrary"))(dm, zc, u, u, u, u, wdw, lng, lnb, mavg)


def _local_step(x, tgt, w, s):
    L, D = x.shape
    row = lambda v: v.reshape(1, -1)

    w_in = w["w_in"].transpose(1, 0, 2).reshape(D, -1)
    w_in_s5, w_in_cv = w_in[None, :, :S5_W], w_in[None, :, S5_W:]
    w_out = w["w_out"].reshape(-1, D)
    w_glu = w["s5_w_glu"].reshape(S5_W, S5_W)
    w_dw = w["conv_w_dw"].transpose(1, 0, 2).reshape(CONV_K, CONV_W)
    w_dw = jnp.concatenate([w_dw, jnp.zeros((HALO - CONV_K, CONV_W), F32)], axis=0)
    mavg = jnp.kron(jnp.eye(CONV_W // CONV_HD, dtype=F32), jnp.full((CONV_HD, CONV_HD), 1.0 / CONV_HD, F32))
    lr, li = s["s5_lam_re"].reshape(1, S5_N), s["s5_lam_im"].reshape(1, S5_N)
    ldt = jnp.repeat(s["s5_log_dt"].reshape(S5_G), S5_P).reshape(1, S5_N)
    brc, bic = _compact_b(s["s5_b_re"].reshape(S5_G, S5_P, S5_GC)), _compact_b(s["s5_b_im"].reshape(S5_G, S5_P, S5_GC))
    crc = _compact_b(s["s5_c_re"].reshape(S5_G, S5_GC, S5_P).transpose(0, 2, 1)).T
    cic = _compact_b(s["s5_c_im"].reshape(S5_G, S5_GC, S5_P).transpose(0, 2, 1)).T
    d_skip, b_glu = row(s["s5_d"]), row(s["s5_b_glu"])
    b_dw, ln_g, ln_b = row(s["conv_b_dw"]), row(s["conv_ln_g"]), row(s["conv_ln_b"])
    g1, gm, g2, gf = row(s["ffn1_norm"]), row(s["mix_norm"]), row(s["ffn2_norm"]), row(s["final_norm"])

    h1 = _rms_fwd(x, g1, "rms1")
    a1, b1, act1 = _ffn_up(h1, w["ffn1_w_gate"], w["ffn1_w_up"], "ffn1_up")
    x1, h2 = _ffn_down(act1, w["ffn1_w_down"], x, gm, "ffn1_down")
    u = _mm_nn(h2, w_in, S5_W, F32, "in_proj")
    are, aim, bre, bim, cre, cim = _s5_params_fwd(lr, li, ldt, brc, bic, crc, cic)
    are_t, aim_t = are.reshape(S5_TILES, LANES), aim.reshape(S5_TILES, LANES)
    s_re, s_im, y_pre, o_s5 = _s5_fwd(u, are_t, aim_t, bre, bim, cre, cim, d_skip, w_glu, b_glu)
    zc, o_cv = _conv_fwd(u, w_dw, b_dw, ln_g, ln_b, mavg)
    x2, h3 = _mix_out(o_s5, o_cv, w_out, x1, g2, "mix_out")
    a2, b2, act2 = _ffn_up(h3, w["ffn2_w_gate"], w["ffn2_w_up"], "ffn2_up")
    dx3, loss, d_gf = _ffn_down_loss(act2, w["ffn2_w_down"], x2, gf, tgt, "ffn2_down_loss")

    gw, gs = {}, {}
    gs["final_norm"] = d_gf
    da2, db2 = _ffn_bwd_act(dx3, w["ffn2_w_down"], a2, b2, "ffn2_bwd_act")
    gw["ffn2_w_down"] = _mm_tn(act2, dx3[None], 0.5, "ffn2_dwd")
    gw["ffn2_w_gate"], gw["ffn2_w_up"] = _mm_tn2(h3, da2, db2, "ffn2_dwgu")
    dx2, gs["ffn2_norm"] = _mm_nt_rmsbwd([da2, db2], [w["ffn2_w_gate"], w["ffn2_w_up"]], x2, g2, dx3, "ffn2_bwd_dx")

    dm = _mm_nt(dx2, w_out, "mix_bwd")
    gw["w_out"] = jnp.concatenate([_mm_tn(o_s5[None], dx2[None], 1.0, "dwout_s5"),
                                   _mm_tn(o_cv[None], dx2[None], 1.0, "dwout_cv")], axis=1).reshape(N_CHIPS, -1, D)
    (du_s5, d_wglu, gs["s5_b_glu"], gs["s5_d"], d_crc, d_cic, d_bre, d_bim, d_are, d_aim) = _s5_bwd(
        dm, y_pre, u, s_re, s_im, are_t, aim_t, bre, bim, cre, cim, d_skip, w_glu, b_glu)
    gw["s5_w_glu"] = d_wglu.astype(BF16).reshape(N_CHIPS, -1, S5_W)
    g_lr, g_li, g_ldt, g_brc, g_bic = _s5_params_bwd(lr, li, ldt, brc, bic, d_are.reshape(1, S5_N),
                                                     d_aim.reshape(1, S5_N), d_bre, d_bim)
    gs["s5_lam_re"], gs["s5_lam_im"] = g_lr, g_li
    gs["s5_log_dt"] = jnp.sum(g_ldt.reshape(S5_G, S5_P), axis=1)
    gs["s5_b_re"], gs["s5_b_im"] = _uncompact_b(g_brc), _uncompact_b(g_bic)
    gs["s5_c_re"] = _uncompact_b(d_crc.T).transpose(0, 2, 1)
    gs["s5_c_im"] = _uncompact_b(d_cic.T).transpose(0, 2, 1)
    dv, d_wdw, gs["conv_b_dw"], gs["conv_ln_g"], gs["conv_ln_b"] = _conv_bwd(dm, zc, u, w_dw, ln_g, ln_b, mavg)
    d_wdw = jnp.sum(d_wdw.reshape(HALO, 8, CONV_W), axis=1)[:CONV_K]
    gw["conv_w_dw"] = d_wdw.reshape(CONV_K, N_CHIPS, -1).transpose(1, 0, 2)
    d_win = jnp.concatenate([_mm_tn(h2[None], du_s5[None], 1.0, "dwin_s5")[0],
                             _mm_tn(h2[None], dv[None], 1.0, "dwin_cv")[0]], axis=1)
    gw["w_in"] = d_win.reshape(D, N_CHIPS, -1).transpose(1, 0, 2)
    dx1, gs["mix_norm"] = _mm_nt_rmsbwd([du_s5[None], dv[None]], [w_in_s5, w_in_cv], x1, gm, dx2, "in_proj_bwd")

    da1, db1 = _ffn_bwd_act(dx1, w["ffn1_w_down"], a1, b1, "ffn1_bwd_act")
    gw["ffn1_w_down"] = _mm_tn(act1, dx1[None], 0.5, "ffn1_dwd")
    gw["ffn1_w_gate"], gw["ffn1_w_up"] = _mm_tn2(h1, da1, db1, "ffn1_dwgu")
    dx0, gs["ffn1_norm"] = _mm_nt_rmsbwd([da1, db1], [w["ffn1_w_gate"], w["ffn1_w_up"]], x, g1, dx1, "ffn1_bwd_dx")
    return loss, dx0, gw, gs


def _position():
    x, y, c = lax.axis_index("x"), lax.axis_index("y"), lax.axis_index("c")
    return x, y, c, [(1 - x, y), (x, 1 - y), (1 - x, 1 - y)]


def _gather_shards(shards):
    n = len(shards)

    def body(*refs):
        ins, outs = refs[:n], refs[n:2 * n]
        send_sems, recv_sems, local_sems = refs[2 * n:]
        x, y, c, chips = _position()
        me = 2 * x + y
        started = []
        for k in range(n):
            loc = pltpu.make_async_copy(ins[k], outs[k].at[me], local_sems.at[k])
            loc.start()
            started.append(loc)
        sends = []
        for k in range(n):
            for j, (ox, oy) in enumerate(chips):
                cp = pltpu.make_async_remote_copy(
                    src_ref=ins[k], dst_ref=outs[k].at[me], send_sem=send_sems.at[3 * k + j],
                    recv_sem=recv_sems.at[3 * k + j], device_id=(ox, oy, c), device_id_type=MESH)
                cp.start()
                sends.append(cp)
        for k in range(n):
            for j, (ox, oy) in enumerate(chips):
                pltpu.make_async_remote_copy(
                    src_ref=ins[k], dst_ref=outs[k].at[2 * ox + oy], send_sem=send_sems.at[3 * k + j],
                    recv_sem=recv_sems.at[3 * k + j], device_id=(ox, oy, c), device_id_type=MESH).wait_recv()
        for cp in sends:
            cp.wait_send()
        for loc in started:
            loc.wait()

    hbm = pl.BlockSpec(memory_space=pl.ANY)
    return pl.pallas_call(
        body, name="gather_weights",
        in_specs=[hbm] * n, out_specs=[hbm] * n,
        out_shape=[jax.ShapeDtypeStruct((N_CHIPS,) + v.shape, v.dtype) for v in shards],
        scratch_shapes=[pltpu.SemaphoreType.DMA((3 * n,)), pltpu.SemaphoreType.DMA((3 * n,)),
                        pltpu.SemaphoreType.DMA((n,))],
        compiler_params=pltpu.CompilerParams(has_side_effects=True))(*shards)


def _scatter_grads(grads):
    n = len(grads)

    def body(*refs):
        ins, outs = refs[:n], refs[n:2 * n]
        send_sems, recv_sems, local_sems = refs[2 * n:]
        x, y, c, chips = _position()
        me = 2 * x + y
        started = []
        for k in range(n):
            loc = pltpu.make_async_copy(ins[k].at[me], outs[k].at[me], local_sems.at[k])
            loc.start()
            started.append(loc)
        sends = []
        for k in range(n):
            for j, (ox, oy) in enumerate(chips):
                cp = pltpu.make_async_remote_copy(
                    src_ref=ins[k].at[2 * ox + oy], dst_ref=outs[k].at[me], send_sem=send_sems.at[3 * k + j],
                    recv_sem=recv_sems.at[3 * k + j], device_id=(ox, oy, c), device_id_type=MESH)
                cp.start()
                sends.append(cp)
        for k in range(n):
            for j, (ox, oy) in enumerate(chips):
                pltpu.make_async_remote_copy(
                    src_ref=ins[k].at[me], dst_ref=outs[k].at[2 * ox + oy], send_sem=send_sems.at[3 * k + j],
                    recv_sem=recv_sems.at[3 * k + j], device_id=(ox, oy, c), device_id_type=MESH).wait_recv()
        for cp in sends:
            cp.wait_send()
        for loc in started:
            loc.wait()

    hbm = pl.BlockSpec(memory_space=pl.ANY)
    return pl.pallas_call(
        body, name="scatter_grads",
        in_specs=[hbm] * n, out_specs=[hbm] * n,
        out_shape=[jax.ShapeDtypeStruct(v.shape, v.dtype) for v in grads],
        scratch_shapes=[pltpu.SemaphoreType.DMA((3 * n,)), pltpu.SemaphoreType.DMA((3 * n,)),
                        pltpu.SemaphoreType.DMA((n,))],
        compiler_params=pltpu.CompilerParams(has_side_effects=True))(*grads)


def _swap_sibling(parts):
    n = len(parts)

    def body(*refs):
        ins, outs = refs[:n], refs[n:2 * n]
        send_sems, recv_sems = refs[2 * n:]
        x, y, c, _ = _position()
        copies = [pltpu.make_async_remote_copy(
            src_ref=ins[k], dst_ref=outs[k], send_sem=send_sems.at[k], recv_sem=recv_sems.at[k],
            device_id=(x, y, 1 - c), device_id_type=MESH) for k in range(n)]
        for cp in copies:
            cp.start()
        for cp in copies:
            cp.wait()

    hbm = pl.BlockSpec(memory_space=pl.ANY)
    return pl.pallas_call(
        body, name="swap_sibling",
        in_specs=[hbm] * n, out_specs=[hbm] * n,
        out_shape=[jax.ShapeDtypeStruct(v.shape, v.dtype) for v in parts],
        scratch_shapes=[pltpu.SemaphoreType.DMA((n,)), pltpu.SemaphoreType.DMA((n,))],
        compiler_params=pltpu.CompilerParams(has_side_effects=True))(*parts)


def _gather_all(v):
    rows, cols = v.shape

    def body(x_ref, out_ref, send_sems, recv_sems, local_sem):
        x, y, c, chips = _position()
        me, sibling = (x, y, c), (x, y, 1 - c)

        def block(px, py, pc):
            return out_ref.at[pl.ds((4 * px + 2 * py + pc) * rows, rows), :]

        def copy(k, blk, to, src=None):
            return pltpu.make_async_remote_copy(
                src_ref=block(*blk) if src is None else src, dst_ref=block(*blk), send_sem=send_sems.at[k],
                recv_sem=recv_sems.at[k], device_id=to, device_id_type=MESH)

        mine = pltpu.make_async_copy(x_ref, block(*me), local_sem)
        mine.start()
        first = [copy(0, me, sibling, src=x_ref)]
        first += [copy(1 + j, me, (*chip, c), src=x_ref) for j, chip in enumerate(chips)]
        for cp in first:
            cp.start()
        passed = [copy(4 + j, (*chip, c), sibling) for j, chip in enumerate(chips)]
        for j, chip in enumerate(chips):
            copy(1 + j, (*chip, c), me).wait_recv()
            passed[j].start()
        copy(0, sibling, me).wait_recv()
        for j, chip in enumerate(chips):
            copy(4 + j, (*chip, 1 - c), me).wait_recv()
        for cp in first + passed:
            cp.wait_send()
        mine.wait()

    return pl.pallas_call(
        body, name="gather_small",
        in_specs=[pl.BlockSpec(memory_space=pltpu.VMEM)], out_specs=pl.BlockSpec(memory_space=pltpu.VMEM),
        out_shape=jax.ShapeDtypeStruct((N_DEV * rows, cols), v.dtype),
        scratch_shapes=[pltpu.SemaphoreType.DMA((7,)), pltpu.SemaphoreType.DMA((7,)), pltpu.SemaphoreType.DMA],
        compiler_params=pltpu.CompilerParams(vmem_limit_bytes=VMEM_LIMIT, has_side_effects=True))(v)


def _adamw(w, g, m, v):
    m = ADAM_B1 * m + (1.0 - ADAM_B1) * g
    v = ADAM_B2 * v + (1.0 - ADAM_B2) * jnp.square(g)
    m_hat = m / (1.0 - ADAM_B1 ** ADAM_STEP)
    v_hat = v / (1.0 - ADAM_B2 ** ADAM_STEP)
    return -ADAM_LR * (m_hat / (jnp.sqrt(v_hat) + ADAM_EPS) + ADAM_WD * w), m, v


def _sum_slots(recv, name):
    _, rows, cols = recv.shape
    tr = _row_tile(rows, cols)

    def body(r_ref, o_ref):
        acc = r_ref[0].astype(F32)
        for s in range(1, N_CHIPS):
            acc = acc + r_ref[s].astype(F32)
        o_ref[...] = acc

    return pl.pallas_call(
        body, name=name, grid=(rows // tr,),
        in_specs=[pl.BlockSpec((N_CHIPS, tr, cols), lambda i: (0, i, 0))],
        out_specs=pl.BlockSpec((tr, cols), lambda i: (i, 0)),
        out_shape=jax.ShapeDtypeStruct((rows, cols), F32), compiler_params=_cparams("parallel"))(recv)


def _adamw_sharded(w, part, other, m, v, name):
    rows, cols = w.shape
    tr = _row_tile(rows, cols)

    def body(w_ref, p_ref, q_ref, m_ref, v_ref, g_ref, d_ref, nm_ref, nv_ref):
        g = p_ref[...] + q_ref[...]
        g_ref[...] = g
        d_ref[...], nm_ref[...], nv_ref[...] = _adamw(w_ref[...], g, m_ref[...], v_ref[...])

    spec = pl.BlockSpec((tr, cols), lambda i: (i, 0))
    shape = jax.ShapeDtypeStruct((rows, cols), F32)
    return pl.pallas_call(
        body, name=name, grid=(rows // tr,), in_specs=[spec] * 5, out_specs=[spec] * 4, out_shape=[shape] * 4,
        compiler_params=_cparams("parallel"))(w, part, other, m, v)


def _adamw_small(w, gathered, m, v):
    rows, cols = w.shape

    def body(w_ref, a_ref, m_ref, v_ref, g_ref, d_ref, nm_ref, nv_ref):
        g = a_ref[0:rows, :]
        for dev in range(1, N_DEV):
            g = g + a_ref[dev * rows:(dev + 1) * rows, :]
        g_ref[...] = g
        d_ref[...], nm_ref[...], nv_ref[...] = _adamw(w_ref[...], g, m_ref[...], v_ref[...])

    shape = jax.ShapeDtypeStruct((rows, cols), F32)
    return pl.pallas_call(
        body, name="adamw_small", out_shape=[shape] * 4,
        compiler_params=pltpu.CompilerParams(vmem_limit_bytes=VMEM_LIMIT))(w, gathered, m, v)


def _pack_small(vals):
    rows = []
    for name in SMALL:
        flat = vals[name].reshape(-1).astype(F32)
        rows.append(jnp.pad(flat, (0, -flat.size % LANES)).reshape(-1, LANES))
    packed = jnp.concatenate(rows, axis=0)
    return jnp.pad(packed, ((0, -packed.shape[0] % 8), (0, 0)))


def _unpack_small(packed, like):
    out, r = {}, 0
    for name in SMALL:
        size = like[name].size
        nrows = -(-size // LANES)
        out[name] = packed[r:r + nrows].reshape(-1)[:size].reshape(like[name].shape)
        r += nrows
    return out


def _flat2d(v):
    return v.reshape(-1, v.shape[-1])


def _train_step(x, tgt, wts, ms, vs):
    shards = {k: _flat2d(wts[k]) for k in SHARDED}
    to_send = [shards[k] if k == "conv_w_dw" else _cast_bf16(shards[k], "cast_" + k) for k in SHARDED]
    gathered = dict(zip(SHARDED, _gather_shards(to_send)))
    small = {k: wts[k] for k in SMALL}

    loss_part, grad_x, gw, gs = _local_step(x[0], tgt[0], gathered, small)

    recv = _scatter_grads([gw[k] for k in SHARDED])
    parts = [_sum_slots(r, "sum_" + k) for k, r in zip(SHARDED, recv)]
    others = _swap_sibling(parts)
    out = {}
    for k, p, o in zip(SHARDED, parts, others):
        res = _adamw_sharded(shards[k], p, o, _flat2d(ms[k]), _flat2d(vs[k]), "adamw_" + k)
        out[k] = [r.reshape(wts[k].shape) for r in res]

    gs = {k: gs[k].reshape(wts[k].shape) for k in SMALL}
    g_all = _gather_all(_pack_small(gs))
    res = _adamw_small(_pack_small(small), g_all, _pack_small({k: ms[k] for k in SMALL}),
                       _pack_small({k: vs[k] for k in SMALL}))
    unpacked = [_unpack_small(r, small) for r in res]
    for k in SMALL:
        out[k] = [u[k] for u in unpacked]

    loss = lax.psum(loss_part[0, 0], ("x", "y", "c"))
    return loss, grad_x[None], out


def kernel(x, ffn1_norm, ffn1_w_gate, ffn1_w_up, ffn1_w_down, mix_norm, w_in, s5_lam_re, s5_lam_im, s5_log_dt, s5_b_re, s5_b_im, s5_c_re, s5_c_im, s5_d, s5_w_glu, s5_b_glu, conv_w_dw, conv_b_dw, conv_ln_g, conv_ln_b, w_out, ffn2_norm, ffn2_w_gate, ffn2_w_up, ffn2_w_down, final_norm, loss_target, m_ffn1_norm, m_ffn1_w_gate, m_ffn1_w_up, m_ffn1_w_down, m_mix_norm, m_w_in, m_s5_lam_re, m_s5_lam_im, m_s5_log_dt, m_s5_b_re, m_s5_b_im, m_s5_c_re, m_s5_c_im, m_s5_d, m_s5_w_glu, m_s5_b_glu, m_conv_w_dw, m_conv_b_dw, m_conv_ln_g, m_conv_ln_b, m_w_out, m_ffn2_norm, m_ffn2_w_gate, m_ffn2_w_up, m_ffn2_w_down, m_final_norm, v_ffn1_norm, v_ffn1_w_gate, v_ffn1_w_up, v_ffn1_w_down, v_mix_norm, v_w_in, v_s5_lam_re, v_s5_lam_im, v_s5_log_dt, v_s5_b_re, v_s5_b_im, v_s5_c_re, v_s5_c_im, v_s5_d, v_s5_w_glu, v_s5_b_glu, v_conv_w_dw, v_conv_b_dw, v_conv_ln_g, v_conv_ln_b, v_w_out, v_ffn2_norm, v_ffn2_w_gate, v_ffn2_w_up, v_ffn2_w_down, v_final_norm):
    given = dict(locals())
    wts = {k: given[k] for k in WEIGHTS}
    ms = {k: given["m_" + k] for k in WEIGHTS}
    vs = {k: given["v_" + k] for k in WEIGHTS}
    loss, grad_x, out = _train_step(x, loss_target, wts, ms, vs)
    return (loss, grad_x, *[out[k][0] for k in WEIGHTS], *[out[k][1] for k in WEIGHTS],
            *[out[k][2] for k in WEIGHTS], *[out[k][3] for k in WEIGHTS])
```

```python
import functools

import jax
import jax.numpy as jnp
import numpy as np
from jax import lax
from jax.experimental import pallas as pl
from jax.experimental.pallas import tpu as pltpu

F32, BF16 = jnp.float32, jnp.bfloat16
MESH = pl.DeviceIdType.MESH

EPS = 1e-6
ADAM_LR, ADAM_B1, ADAM_B2, ADAM_EPS, ADAM_WD, ADAM_STEP = 0.001, 0.9, 0.999, 1e-08, 0.01, 10

N_CHIPS = 4
N_DEV = 8
LANES = 128
BF16_ROWS = 16
S5_W, S5_G, S5_GC, S5_P = 512, 32, 16, 64
S5_N = S5_G * S5_P
S5_TILES = S5_N // LANES
CONV_W, CONV_K, CONV_HD = 512, 31, 64
HALO = 32
TM = 512
TS = 256
VMEM_LIMIT = 48 << 20
GELU_C0, GELU_C1 = 0.7978845608028654, 0.044715

FFN_T = ("ffn1_w_gate", "ffn1_w_up", "ffn2_w_gate", "ffn2_w_up")
SHARDED = ("ffn1_w_gate", "ffn1_w_up", "ffn1_w_down", "w_in", "s5_w_glu", "conv_w_dw", "w_out",
           "ffn2_w_gate", "ffn2_w_up", "ffn2_w_down")
SMALL = ("ffn1_norm", "mix_norm", "s5_lam_re", "s5_lam_im", "s5_log_dt", "s5_b_re", "s5_b_im", "s5_c_re",
         "s5_c_im", "s5_d", "s5_b_glu", "conv_b_dw", "conv_ln_g", "conv_ln_b", "ffn2_norm", "final_norm")
WEIGHTS = ("ffn1_norm", "ffn1_w_gate", "ffn1_w_up", "ffn1_w_down", "mix_norm", "w_in", "s5_lam_re", "s5_lam_im",
           "s5_log_dt", "s5_b_re", "s5_b_im", "s5_c_re", "s5_c_im", "s5_d", "s5_w_glu", "s5_b_glu", "conv_w_dw",
           "conv_b_dw", "conv_ln_g", "conv_ln_b", "w_out", "ffn2_norm", "ffn2_w_gate", "ffn2_w_up", "ffn2_w_down",
           "final_norm")


def _dot(a, b):
    return jnp.dot(a, b, preferred_element_type=F32)


def _dot_nt(a, b):
    return lax.dot_general(a, b, (((1,), (1,)), ((), ())), preferred_element_type=F32)


def _dot_tn(a, b):
    return lax.dot_general(a, b, (((0,), (0,)), ((), ())), preferred_element_type=F32)


def _dot_hi(a, b):
    return jnp.dot(a, b, precision=lax.Precision.HIGHEST, preferred_element_type=F32)


def _colsum(v):
    return jnp.sum(v, axis=0, keepdims=True)


def _sigmoid(v):
    return 1.0 / (1.0 + jnp.exp(-v))


def _accumulate(ref, first, value):
    @pl.when(first)
    def _():
        ref[...] = value

    @pl.when(jnp.logical_not(first))
    def _():
        ref[...] += value


def _position():
    x, y, c = lax.axis_index("x"), lax.axis_index("y"), lax.axis_index("c")
    return x, y, c, [(1 - x, y), (x, 1 - y), (1 - x, 1 - y)]


def _remote(src, dst, sems, send, recv, device):
    return pltpu.make_async_remote_copy(src_ref=src, dst_ref=dst, send_sem=sems.at[send], recv_sem=sems.at[recv],
                                        device_id=device, device_id_type=MESH)


class _Gather:
    def __init__(self, shard):
        self.ins = [shard]
        self.outs = [jax.ShapeDtypeStruct((N_CHIPS,) + shard.shape, shard.dtype)]
        self.rows = shard.shape[0]
        self.halve = shard.dtype == BF16 and self.rows % (2 * BF16_ROWS) == 0
        self.n_sem = 13 if self.halve else 7
        self.result = None

    def _copies(self, ins, outs, sems, s0, pos):
        x, y, c, chips = pos
        src, dst = ins[0], outs[0]
        me = 2 * x + y
        if self.halve:
            hr = self.rows // 2
            mine, theirs = pl.ds(c * hr, hr), pl.ds((1 - c) * hr, hr)
            part = lambda slot, rows: dst.at[slot, rows]
            my_src = src.at[mine]
        else:
            mine = theirs = None
            part = lambda slot, rows: dst.at[slot]
            my_src = src
        slot = lambda j: 2 * chips[j][0] + chips[j][1]
        local = lambda: pltpu.make_async_copy(src, dst.at[me], sems.at[s0])
        send = lambda j: _remote(my_src, part(me, mine), sems, s0 + 1 + j, s0 + 4 + j, (*chips[j], c))
        land = lambda j: _remote(my_src, part(slot(j), mine), sems, s0 + 1 + j, s0 + 4 + j, (*chips[j], c))
        fwd = lambda j: _remote(part(slot(j), mine), part(slot(j), mine), sems, s0 + 7 + j, s0 + 10 + j, (x, y, 1 - c))
        got = lambda j: _remote(part(slot(j), theirs), part(slot(j), theirs), sems, s0 + 7 + j, s0 + 10 + j,
                                (x, y, 1 - c))
        return local, send, land, fwd, got

    def start(self, ins, outs, sems, s0, pos):
        local, send, _, _, _ = self._copies(ins, outs, sems, s0, pos)
        local().start()
        for j in range(N_CHIPS - 1):
            send(j).start()

    def finish(self, ins, outs, sems, s0, pos):
        local, send, land, fwd, got = self._copies(ins, outs, sems, s0, pos)
        others = range(N_CHIPS - 1)
        for j in others:
            land(j).wait_recv()
            if self.halve:
                fwd(j).start()
        for j in others:
            if self.halve:
                got(j).wait_recv()
        for j in others:
            send(j).wait_send()
            if self.halve:
                fwd(j).wait_send()
        local().wait()


class _Scatter:
    def __init__(self, grad):
        self.ins = [grad]
        self.outs = [jax.ShapeDtypeStruct(grad.shape, grad.dtype)]
        self.n_sem = 7
        self.result = None

    def _copies(self, ins, outs, sems, s0, pos):
        x, y, c, chips = pos
        src, dst = ins[0], outs[0]
        me = 2 * x + y
        slot = lambda j: 2 * chips[j][0] + chips[j][1]
        local = lambda: pltpu.make_async_copy(src.at[me], dst.at[me], sems.at[s0])
        send = lambda j: _remote(src.at[slot(j)], dst.at[me], sems, s0 + 1 + j, s0 + 4 + j, (*chips[j], c))
        land = lambda j: _remote(src.at[me], dst.at[slot(j)], sems, s0 + 1 + j, s0 + 4 + j, (*chips[j], c))
        return local, send, land

    def start(self, ins, outs, sems, s0, pos):
        local, send, _ = self._copies(ins, outs, sems, s0, pos)
        local().start()
        for j in range(N_CHIPS - 1):
            send(j).start()

    def finish(self, ins, outs, sems, s0, pos):
        local, send, land = self._copies(ins, outs, sems, s0, pos)
        for j in range(N_CHIPS - 1):
            land(j).wait_recv()
        for j in range(N_CHIPS - 1):
            send(j).wait_send()
        local().wait()


class _Swap:
    def __init__(self, part):
        self.ins = [part]
        self.outs = [jax.ShapeDtypeStruct(part.shape, part.dtype)]
        self.n_sem = 2
        self.result = None

    def _copy(self, ins, outs, sems, s0, pos):
        x, y, c, _ = pos
        return _remote(ins[0], outs[0], sems, s0, s0 + 1, (x, y, 1 - c))

    def start(self, ins, outs, sems, s0, pos):
        self._copy(ins, outs, sems, s0, pos).start()

    def finish(self, ins, outs, sems, s0, pos):
        self._copy(ins, outs, sems, s0, pos).wait()


def _pallas(body, args, *, name, grid, in_specs, out_specs, out_shape, scratch_shapes=(), comm=()):
    comm = list(comm)
    n_in, n_out, n_scr = len(in_specs), len(out_specs), len(scratch_shapes)
    c_in = [a for op in comm for a in op.ins]
    c_out = [s for op in comm for s in op.outs]
    n_sem = sum(op.n_sem for op in comm)

    def full(*refs):
        o0 = n_in + len(c_in)
        s0 = o0 + n_out + len(c_out)
        ins, cin = refs[:n_in], refs[n_in:o0]
        outs, cout = refs[o0:o0 + n_out], refs[o0 + n_out:s0]
        scratch = refs[s0:s0 + n_scr]
        if comm:
            sems = refs[s0 + n_scr]
            ids = [pl.program_id(d) for d in range(len(grid))]
            first = functools.reduce(jnp.logical_and, [i == 0 for i in ids])
            last = functools.reduce(jnp.logical_and, [i == g - 1 for i, g in zip(ids, grid)])
            pos = _position()

            def each(step):
                ci = co = cs = 0
                for op in comm:
                    getattr(op, step)(cin[ci:ci + len(op.ins)], cout[co:co + len(op.outs)], sems, cs, pos)
                    ci, co, cs = ci + len(op.ins), co + len(op.outs), cs + op.n_sem

            @pl.when(first)
            def _():
                each("start")

        body(*ins, *outs, *scratch)
        if comm:
            @pl.when(last)
            def _():
                each("finish")

    hbm = pl.BlockSpec(memory_space=pl.ANY)
    res = pl.pallas_call(
        full, name=name, grid=grid,
        in_specs=list(in_specs) + [hbm] * len(c_in), out_specs=list(out_specs) + [hbm] * len(c_out),
        out_shape=list(out_shape) + c_out,
        scratch_shapes=list(scratch_shapes) + ([pltpu.SemaphoreType.DMA((n_sem,))] if comm else []),
        compiler_params=pltpu.CompilerParams(dimension_semantics=("arbitrary",) * len(grid),
                                             vmem_limit_bytes=VMEM_LIMIT))(*args, *c_in)
    k = n_out
    for op in comm:
        op.result = list(res[k:k + len(op.outs)])
        k += len(op.outs)
    return list(res[:n_out])


def _row_tile(rows, cols, itemsize=4, budget=1 << 20):
    t = rows
    while t % 16 == 0 and t * cols * itemsize > budget:
        t //= 2
    return t


def _cast_bf16(w, name):
    rows, cols = w.shape
    tr = _row_tile(rows, cols)

    def body(w_ref, o_ref):
        o_ref[...] = w_ref[...].astype(BF16)

    spec = pl.BlockSpec((tr, cols), lambda i: (i, 0))
    return _pallas(body, [w], name=name, grid=(rows // tr,), in_specs=[spec], out_specs=[spec],
                   out_shape=[jax.ShapeDtypeStruct((rows, cols), BF16)])[0]


def _rms_fwd(x, g, name, comm=()):
    L, D = x.shape

    def body(x_ref, g_ref, h_ref):
        xf = x_ref[...]
        r = lax.rsqrt(jnp.mean(xf * xf, axis=-1, keepdims=True) + EPS)
        h_ref[...] = (xf * r * g_ref[...]).astype(BF16)

    row = pl.BlockSpec((TM, D), lambda i: (i, 0))
    return _pallas(body, [x, g], name=name, grid=(L // TM,),
                   in_specs=[row, pl.BlockSpec((1, D), lambda i: (0, 0))], out_specs=[row],
                   out_shape=[jax.ShapeDtypeStruct((L, D), BF16)], comm=comm)[0]


def _ffn_up(h, wg_t, wu_t, name, comm=()):
    L, D = h.shape
    G, FS, _ = wg_t.shape

    def body(h_ref, wg_ref, wu_ref, a_ref, b_ref, act_ref):
        hv = h_ref[...]
        a = _dot_nt(hv, wg_ref[...])
        b = _dot_nt(hv, wu_ref[...])
        a_ref[...] = a.astype(BF16)
        b_ref[...] = b.astype(BF16)
        act_ref[...] = (a * _sigmoid(a) * b).astype(BF16)

    wspec = pl.BlockSpec((None, FS, D), lambda j, i: (j, 0, 0))
    ospec = pl.BlockSpec((None, TM, FS), lambda j, i: (j, i, 0))
    oshape = jax.ShapeDtypeStruct((G, L, FS), BF16)
    return _pallas(body, [h, wg_t, wu_t], name=name, grid=(G, L // TM),
                   in_specs=[pl.BlockSpec((TM, D), lambda j, i: (i, 0)), wspec, wspec],
                   out_specs=[ospec, ospec, ospec], out_shape=[oshape, oshape, oshape], comm=comm)


def _ffn_down(act, wd, x, g_next, name, comm=()):
    G, L, FS = act.shape
    D = wd.shape[2]

    def body(act_ref, wd_ref, x_ref, g_ref, xn_ref, hn_ref, acc):
        j = pl.program_id(1)
        _accumulate(acc, j == 0, _dot(act_ref[...], wd_ref[...]))

        @pl.when(j == G - 1)
        def _():
            xn = x_ref[...] + 0.5 * acc[...]
            xn_ref[...] = xn
            r = lax.rsqrt(jnp.mean(xn * xn, axis=-1, keepdims=True) + EPS)
            hn_ref[...] = (xn * r * g_ref[...]).astype(BF16)

    row = pl.BlockSpec((TM, D), lambda i, j: (i, 0))
    return _pallas(body, [act, wd, x, g_next], name=name, grid=(L // TM, G),
                   in_specs=[pl.BlockSpec((None, TM, FS), lambda i, j: (j, i, 0)),
                             pl.BlockSpec((None, FS, D), lambda i, j: (j, 0, 0)), row,
                             pl.BlockSpec((1, D), lambda i, j: (0, 0))],
                   out_specs=[row, row],
                   out_shape=[jax.ShapeDtypeStruct((L, D), F32), jax.ShapeDtypeStruct((L, D), BF16)],
                   scratch_shapes=[pltpu.VMEM((TM, D), F32)], comm=comm)


def _ffn_down_loss(act, wd, x, gf, tgt, name):
    G, L, FS = act.shape
    D = wd.shape[2]

    def body(act_ref, wd_ref, x_ref, g_ref, t_ref, dx_ref, loss_ref, dg_ref, acc):
        i, j = pl.program_id(0), pl.program_id(1)
        _accumulate(acc, j == 0, _dot(act_ref[...], wd_ref[...]))

        @pl.when(j == G - 1)
        def _():
            xn = x_ref[...] + 0.5 * acc[...]
            r = lax.rsqrt(jnp.mean(xn * xn, axis=-1, keepdims=True) + EPS)
            xh = xn * r
            gv = g_ref[...]
            e = xh * gv - t_ref[...]
            part = 0.5 * jnp.sum(_colsum(e * e), axis=1, keepdims=True) / D
            dy = e / D
            _accumulate(loss_ref, i == 0, jnp.broadcast_to(part, (1, LANES)))
            _accumulate(dg_ref, i == 0, _colsum(dy * xh))
            dxh = dy * gv
            dx_ref[...] = r * (dxh - xh * jnp.mean(dxh * xh, axis=-1, keepdims=True))

    row = pl.BlockSpec((TM, D), lambda i, j: (i, 0))
    return _pallas(body, [act, wd, x, gf, tgt], name=name, grid=(L // TM, G),
                   in_specs=[pl.BlockSpec((None, TM, FS), lambda i, j: (j, i, 0)),
                             pl.BlockSpec((None, FS, D), lambda i, j: (j, 0, 0)), row,
                             pl.BlockSpec((1, D), lambda i, j: (0, 0)), row],
                   out_specs=[row, pl.BlockSpec((1, LANES), lambda i, j: (0, 0)),
                              pl.BlockSpec((1, D), lambda i, j: (0, 0))],
                   out_shape=[jax.ShapeDtypeStruct((L, D), F32), jax.ShapeDtypeStruct((1, LANES), F32),
                              jax.ShapeDtypeStruct((1, D), F32)],
                   scratch_shapes=[pltpu.VMEM((TM, D), F32)])


def _ffn_bwd_act(dx_out, wd, a, b, name):
    L, D = dx_out.shape
    G, FS, _ = wd.shape

    def body(dx_ref, wd_ref, a_ref, b_ref, da_ref, db_ref):
        dact = 0.5 * _dot_nt(dx_ref[...].astype(BF16), wd_ref[...])
        av = a_ref[...].astype(F32)
        bv = b_ref[...].astype(F32)
        sg = _sigmoid(av)
        da_ref[...] = (dact * bv * sg * (1.0 + av * (1.0 - sg))).astype(BF16)
        db_ref[...] = (dact * av * sg).astype(BF16)

    gspec = pl.BlockSpec((None, TM, FS), lambda j, i: (j, i, 0))
    oshape = jax.ShapeDtypeStruct((G, L, FS), BF16)
    return _pallas(body, [dx_out, wd, a, b], name=name, grid=(G, L // TM),
                   in_specs=[pl.BlockSpec((TM, D), lambda j, i: (i, 0)),
                             pl.BlockSpec((None, FS, D), lambda j, i: (j, 0, 0)), gspec, gspec],
                   out_specs=[gspec, gspec], out_shape=[oshape, oshape])


def _mm_grouped(a, w, name):
    L, K = a.shape
    G, _, N = w.shape

    def body(a_ref, w_ref, o_ref):
        o_ref[...] = _dot(a_ref[...], w_ref[...])

    return _pallas(body, [a, w], name=name, grid=(G, L // TM),
                   in_specs=[pl.BlockSpec((TM, K), lambda g, i: (i, 0)), pl.BlockSpec((None, K, N), lambda g, i: (g, 0, 0))],
                   out_specs=[pl.BlockSpec((TM, N), lambda g, i: (i, g))],
                   out_shape=[jax.ShapeDtypeStruct((L, G * N), F32)])[0]


def _mm_nt(a, w, name):
    L, K = a.shape
    N = w.shape[0]

    def body(a_ref, w_ref, o_ref):
        o_ref[...] = _dot_nt(a_ref[...].astype(BF16), w_ref[...]).astype(BF16)

    return _pallas(body, [a, w], name=name, grid=(L // TM,),
                   in_specs=[pl.BlockSpec((TM, K), lambda i: (i, 0)), pl.BlockSpec((N, K), lambda i: (0, 0))],
                   out_specs=[pl.BlockSpec((TM, N), lambda i: (i, 0))],
                   out_shape=[jax.ShapeDtypeStruct((L, N), BF16)])[0]


def _operand_spec(v, groups, cols):
    if cols:
        return pl.BlockSpec((TM, v.shape[1] // groups), lambda g, k: (k, g))
    if v.shape[0] > 1:
        return pl.BlockSpec((None, TM, v.shape[2]), lambda g, k: (g, k, 0))
    return pl.BlockSpec((None, TM, v.shape[2]), lambda g, k: (0, k, 0))


def _mm_tn(a, b_list, scale, name, groups, b_cols=False, comm=()):
    L, M = a.shape[1], a.shape[2]
    N = b_list[0].shape[1] // groups if b_cols else b_list[0].shape[2]
    nb, nk = len(b_list), L // TM

    def body(a_ref, *refs):
        b_refs, o_refs, accs = refs[:nb], refs[nb:2 * nb], refs[2 * nb:]
        k = pl.program_id(1)
        av = a_ref[...].astype(BF16)
        for b_ref, acc in zip(b_refs, accs):
            _accumulate(acc, k == 0, _dot_tn(av, b_ref[...].astype(BF16)))

        @pl.when(k == nk - 1)
        def _():
            for o_ref, acc in zip(o_refs, accs):
                o_ref[...] = (acc[...] * scale).astype(BF16)

    ospec = pl.BlockSpec((None, M, N), lambda g, k: (g, 0, 0))
    return _pallas(body, [a, *b_list], name=name, grid=(groups, nk),
                   in_specs=[_operand_spec(a, groups, False)] + [_operand_spec(b, groups, b_cols) for b in b_list],
                   out_specs=[ospec] * nb, out_shape=[jax.ShapeDtypeStruct((groups, M, N), BF16)] * nb,
                   scratch_shapes=[pltpu.VMEM((M, N), F32)] * nb, comm=comm)


def _mm_rmsbwd(a_list, w_list, nt, a_cols, x_in, g, dx_out, name, comm=()):
    P = len(a_list)
    G = w_list[0].shape[0]
    L, D = x_in.shape
    mm = _dot_nt if nt else _dot

    def body(*refs):
        a_refs, w_refs = refs[:P], refs[P:2 * P]
        x_ref, g_ref, dxo_ref, dx_ref, dg_ref, acc = refs[2 * P:]
        i, j = pl.program_id(0), pl.program_id(1)
        p = mm(a_refs[0][...], w_refs[0][...])
        for a_ref, w_ref in zip(a_refs[1:], w_refs[1:]):
            p = p + mm(a_ref[...], w_ref[...])
        _accumulate(acc, j == 0, p)

        @pl.when(j == G - 1)
        def _():
            dh = acc[...]
            xf = x_ref[...]
            r = lax.rsqrt(jnp.mean(xf * xf, axis=-1, keepdims=True) + EPS)
            xh = xf * r
            _accumulate(dg_ref, i == 0, _colsum(dh * xh))
            dxh = dh * g_ref[...]
            dx_ref[...] = dxo_ref[...] + r * (dxh - xh * jnp.mean(dxh * xh, axis=-1, keepdims=True))

    row = pl.BlockSpec((TM, D), lambda i, j: (i, 0))
    vec = pl.BlockSpec((1, D), lambda i, j: (0, 0))
    if a_cols:
        a_specs = [pl.BlockSpec((TM, a.shape[1] // G), lambda i, j: (i, j)) for a in a_list]
    else:
        a_specs = [pl.BlockSpec((None, TM, a.shape[2]), lambda i, j: (j, i, 0)) for a in a_list]
    w_specs = [pl.BlockSpec((None,) + w.shape[1:], lambda i, j: (j, 0, 0)) for w in w_list]
    return _pallas(body, [*a_list, *w_list, x_in, g, dx_out], name=name, grid=(L // TM, G),
                   in_specs=a_specs + w_specs + [row, vec, row], out_specs=[row, vec],
                   out_shape=[jax.ShapeDtypeStruct((L, D), F32), jax.ShapeDtypeStruct((1, D), F32)],
                   scratch_shapes=[pltpu.VMEM((TM, D), F32)], comm=comm)


def _mix_out(cat, wout, x1, g_next, name):
    L, K = cat.shape
    D = wout.shape[1]

    def body(c_ref, w_ref, x_ref, g_ref, xn_ref, hn_ref):
        xn = x_ref[...] + _dot(c_ref[...], w_ref[...])
        xn_ref[...] = xn
        r = lax.rsqrt(jnp.mean(xn * xn, axis=-1, keepdims=True) + EPS)
        hn_ref[...] = (xn * r * g_ref[...]).astype(BF16)

    row = pl.BlockSpec((TM, D), lambda i: (i, 0))
    return _pallas(body, [cat, wout, x1, g_next], name=name, grid=(L // TM,),
                   in_specs=[pl.BlockSpec((TM, K), lambda i: (i, 0)), pl.BlockSpec((K, D), lambda i: (0, 0)), row,
                             pl.BlockSpec((1, D), lambda i: (0, 0))],
                   out_specs=[row, row],
                   out_shape=[jax.ShapeDtypeStruct((L, D), F32), jax.ShapeDtypeStruct((L, D), BF16)])


def _s5_disc(lr, li, ldt, brc, bic):
    dt = jnp.exp(ldt)
    mag = jnp.exp(lr * dt)
    are = mag * jnp.cos(li * dt)
    aim = mag * jnp.sin(li * dt)
    den = lr * lr + li * li
    nre = are - 1.0
    fre = (nre * lr + aim * li) / den
    fim = (aim * lr - nre * li) / den
    return are, aim, fre * brc - fim * bic, fre * bic + fim * brc


def _s5_params_fwd(lr, li, ldt, brc, bic, crc, cic):
    def body(lr_ref, li_ref, ldt_ref, br_ref, bi_ref, cr_ref, ci_ref, are_ref, aim_ref, bre_ref, bim_ref, cre_ref, cim_ref):
        are, aim, bre, bim = _s5_disc(lr_ref[...], li_ref[...], ldt_ref[...], br_ref[...], bi_ref[...])
        are_ref[...] = are
        aim_ref[...] = aim
        bre_ref[...] = bre.astype(BF16)
        bim_ref[...] = bim.astype(BF16)
        cre_ref[...] = cr_ref[...].astype(BF16)
        cim_ref[...] = ci_ref[...].astype(BF16)

    vec = jax.ShapeDtypeStruct((1, S5_N), F32)
    return pl.pallas_call(
        body, name="s5_params_fwd",
        out_shape=[vec, vec, jax.ShapeDtypeStruct((LANES, S5_N), BF16), jax.ShapeDtypeStruct((LANES, S5_N), BF16),
                   jax.ShapeDtypeStruct((S5_N, LANES), BF16), jax.ShapeDtypeStruct((S5_N, LANES), BF16)],
        compiler_params=pltpu.CompilerParams(vmem_limit_bytes=VMEM_LIMIT))(lr, li, ldt, brc, bic, crc, cic)


def _s5_params_bwd(lr, li, ldt, brc, bic, dare, daim, dbre, dbim):
    def body(lr_ref, li_ref, ldt_ref, br_ref, bi_ref, dare_ref, daim_ref, dbre_ref, dbim_ref,
             glr_ref, gli_ref, gldt_ref, gbr_ref, gbi_ref):
        _, vjp = jax.vjp(_s5_disc, lr_ref[...], li_ref[...], ldt_ref[...], br_ref[...], bi_ref[...])
        glr, gli, gldt, gbr, gbi = vjp((dare_ref[...], daim_ref[...], dbre_ref[...], dbim_ref[...]))
        glr_ref[...] = glr
        gli_ref[...] = gli
        gldt_ref[...] = gldt
        gbr_ref[...] = gbr
        gbi_ref[...] = gbi

    vec = jax.ShapeDtypeStruct((1, S5_N), F32)
    mat = jax.ShapeDtypeStruct((LANES, S5_N), F32)
    return pl.pallas_call(
        body, name="s5_params_bwd", out_shape=[vec, vec, vec, mat, mat],
        compiler_params=pltpu.CompilerParams(vmem_limit_bytes=VMEM_LIMIT))(lr, li, ldt, brc, bic, dare, daim, dbre, dbim)


def _gelu_parts(y):
    th = jnp.tanh(GELU_C0 * (y + GELU_C1 * y * y * y))
    return 0.5 * y * (1.0 + th), th


def _state_rows(q, T):
    return pl.ds(q, T, stride=S5_TILES)


def _s5_fwd(u, are, aim, bre, bim, cre, cim, d_skip, wglu, bglu, comm=()):
    L = u.shape[0]
    T = min(TS, L)
    n = L // T

    def body(u_ref, are_ref, aim_ref, bre_ref, bim_ref, cre_ref, cim_ref, d_ref, wg_ref, bg_ref,
             sre_ref, sim_ref, y_ref, o_ref, st_re, st_im):
        i = pl.program_id(0)

        @pl.when(i == 0)
        def _():
            st_re[...] = jnp.zeros_like(st_re)
            st_im[...] = jnp.zeros_like(st_im)

        uf = u_ref[...]
        ub = uf.astype(BF16)
        for q in range(S5_TILES):
            ct = q // 4
            uq = ub[:, ct * LANES:(ct + 1) * LANES]
            sre_ref[_state_rows(q, T), :] = _dot(uq, bre_ref[:, q * LANES:(q + 1) * LANES])
            sim_ref[_state_rows(q, T), :] = _dot(uq, bim_ref[:, q * LANES:(q + 1) * LANES])
        a_re, a_im = are_ref[...], aim_ref[...]

        def step(t, carry):
            s_re, s_im = carry
            rows = pl.ds(pl.multiple_of(t * S5_TILES, S5_TILES), S5_TILES)
            n_re = a_re * s_re - a_im * s_im + sre_ref[rows, :]
            n_im = a_re * s_im + a_im * s_re + sim_ref[rows, :]
            sre_ref[rows, :] = n_re
            sim_ref[rows, :] = n_im
            return n_re, n_im

        s_re, s_im = lax.fori_loop(0, T, step, (st_re[...], st_im[...]), unroll=8)
        st_re[...] = s_re
        st_im[...] = s_im
        tiles = []
        for ct in range(4):
            acc = jnp.zeros((T, LANES), F32)
            for q in range(4 * ct, 4 * ct + 4):
                acc = acc + _dot(sre_ref[_state_rows(q, T), :].astype(BF16), cre_ref[q * LANES:(q + 1) * LANES, :])
                acc = acc - _dot(sim_ref[_state_rows(q, T), :].astype(BF16), cim_ref[q * LANES:(q + 1) * LANES, :])
            tiles.append(acc)
        y = jnp.concatenate(tiles, axis=1) + d_ref[...] * uf
        y_ref[...] = y
        yg, _ = _gelu_parts(y)
        gate = _sigmoid(_dot(yg.astype(BF16), wg_ref[...]) + bg_ref[...])
        o_ref[...] = (yg * gate).astype(BF16)

    const = lambda shape: pl.BlockSpec(shape, lambda i: (0, 0))
    sspec = pl.BlockSpec((T * S5_TILES, LANES), lambda i: (i, 0))
    chunk = pl.BlockSpec((T, S5_W), lambda i: (i, 0))
    return _pallas(body, [u, are, aim, bre, bim, cre, cim, d_skip, wglu, bglu], name="s5_fwd", grid=(n,),
                   in_specs=[chunk, const((S5_TILES, LANES)), const((S5_TILES, LANES)),
                             const((LANES, S5_N)), const((LANES, S5_N)), const((S5_N, LANES)), const((S5_N, LANES)),
                             const((1, S5_W)), const((S5_W, S5_W)), const((1, S5_W))],
                   out_specs=[sspec, sspec, chunk, chunk],
                   out_shape=[jax.ShapeDtypeStruct((L * S5_TILES, LANES), F32),
                              jax.ShapeDtypeStruct((L * S5_TILES, LANES), F32),
                              jax.ShapeDtypeStruct((L, S5_W), F32), jax.ShapeDtypeStruct((L, S5_W), BF16)],
                   scratch_shapes=[pltpu.VMEM((S5_TILES, LANES), F32), pltpu.VMEM((S5_TILES, LANES), F32)], comm=comm)


def _s5_bwd(dm, y_pre, u, s_re, s_im, are, aim, bre, bim, cre, cim, d_skip, wglu, bglu, comm=()):
    L = u.shape[0]
    T = min(TS, L)
    n = L // T

    def body(dm_ref, y_ref, u_ref, sre_ref, sim_ref, spre_ref, spim_ref, are_ref, aim_ref, bre_ref, bim_ref,
             cre_ref, cim_ref, d_ref, wg_ref, bg_ref,
             du_ref, dwg_ref, dbg_ref, dd_ref, dcre_ref, dcim_ref, dbre_ref, dbim_ref, dare_ref, daim_ref,
             gre, gim, car_re, car_im):
        i = pl.program_id(0)
        first = i == 0

        @pl.when(first)
        def _():
            car_re[...] = jnp.zeros_like(car_re)
            car_im[...] = jnp.zeros_like(car_im)
            dcre_ref[...] = jnp.zeros_like(dcre_ref)
            dcim_ref[...] = jnp.zeros_like(dcim_ref)
            dbre_ref[...] = jnp.zeros_like(dbre_ref)
            dbim_ref[...] = jnp.zeros_like(dbim_ref)

        y = y_ref[...]
        uf = u_ref[...]
        yg, th = _gelu_parts(y)
        dgelu = 0.5 * (1.0 + th) + 0.5 * y * (1.0 - th * th) * GELU_C0 * (1.0 + 3.0 * GELU_C1 * y * y)
        ygb = yg.astype(BF16)
        sg = _sigmoid(_dot(ygb, wg_ref[...]) + bg_ref[...])
        dout = dm_ref[...].astype(F32)
        dgp = dout * yg * sg * (1.0 - sg)
        dgpb = dgp.astype(BF16)
        dyg = dout * sg + _dot_nt(dgpb, wg_ref[...])
        _accumulate(dwg_ref, first, _dot_tn(ygb, dgpb))
        _accumulate(dbg_ref, first, _colsum(dgp))
        dy = dyg * dgelu
        _accumulate(dd_ref, first, _colsum(dy * uf))
        dyb = dy.astype(BF16)
        ub = uf.astype(BF16)

        for q in range(S5_TILES):
            ct = q // 4
            win = slice(q * LANES, (q + 1) * LANES)
            dyq = dyb[:, ct * LANES:(ct + 1) * LANES]
            dcre_ref[win, :] += _dot_tn(sre_ref[_state_rows(q, T), :].astype(BF16), dyq)
            dcim_ref[win, :] -= _dot_tn(sim_ref[_state_rows(q, T), :].astype(BF16), dyq)
            gre[_state_rows(q, T), :] = _dot_nt(dyq, cre_ref[win, :])
            gim[_state_rows(q, T), :] = -_dot_nt(dyq, cim_ref[win, :])

        a_re, a_im = are_ref[...], aim_ref[...]

        def adjoint(t, g_re, g_im):
            rows = pl.ds(pl.multiple_of(t * S5_TILES, S5_TILES), S5_TILES)
            n_re = gre[rows, :] + a_re * g_re + a_im * g_im
            n_im = gim[rows, :] + a_re * g_im - a_im * g_re
            gre[rows, :] = n_re
            gim[rows, :] = n_im
            return n_re, n_im

        def step(k, carry):
            g_re, g_im, da_re, da_im = carry
            t = T - 1 - k
            g_re, g_im = adjoint(t, g_re, g_im)
            prev = pl.ds(pl.multiple_of((t - 1) * S5_TILES, S5_TILES), S5_TILES)
            p_re, p_im = sre_ref[prev, :], sim_ref[prev, :]
            return g_re, g_im, da_re + g_re * p_re + g_im * p_im, da_im + g_im * p_re - g_re * p_im

        zero = jnp.zeros((S5_TILES, LANES), F32)
        g_re, g_im, da_re, da_im = lax.fori_loop(0, T - 1, step, (car_re[...], car_im[...], zero, zero), unroll=8)
        g_re, g_im = adjoint(0, g_re, g_im)
        keep = (i < n - 1).astype(F32)
        p_re, p_im = spre_ref[...] * keep, spim_ref[...] * keep
        car_re[...] = g_re
        car_im[...] = g_im
        _accumulate(dare_ref, first, da_re + g_re * p_re + g_im * p_im)
        _accumulate(daim_ref, first, da_im + g_im * p_re - g_re * p_im)

        tiles = []
        for ct in range(4):
            uq = ub[:, ct * LANES:(ct + 1) * LANES]
            acc = d_ref[:, ct * LANES:(ct + 1) * LANES] * dy[:, ct * LANES:(ct + 1) * LANES]
            for q in range(4 * ct, 4 * ct + 4):
                win = slice(q * LANES, (q + 1) * LANES)
                gq_re = gre[_state_rows(q, T), :].astype(BF16)
                gq_im = gim[_state_rows(q, T), :].astype(BF16)
                acc = acc + _dot_nt(gq_re, bre_ref[:, win]) + _dot_nt(gq_im, bim_ref[:, win])
                dbre_ref[:, win] += _dot_tn(uq, gq_re)
                dbim_ref[:, win] += _dot_tn(uq, gq_im)
            tiles.append(acc)
        du_ref[...] = jnp.concatenate(tiles, axis=1).astype(BF16)

    rev = lambda i: (n - 1 - i, 0)
    const = lambda shape: pl.BlockSpec(shape, lambda i: (0, 0))
    chunk = pl.BlockSpec((T, S5_W), rev)
    sspec = pl.BlockSpec((T * S5_TILES, LANES), rev)
    pspec = pl.BlockSpec((S5_TILES, LANES), lambda i: (jnp.maximum((n - 1 - i) * T - 1, 0), 0))
    tile = jax.ShapeDtypeStruct((S5_TILES, LANES), F32)
    vec = jax.ShapeDtypeStruct((1, S5_W), F32)
    return _pallas(
        body, [dm, y_pre, u, s_re, s_im, s_re, s_im, are, aim, bre, bim, cre, cim, d_skip, wglu, bglu],
        name="s5_bwd", grid=(n,),
        in_specs=[chunk, chunk, chunk, sspec, sspec, pspec, pspec, const((S5_TILES, LANES)), const((S5_TILES, LANES)),
                  const((LANES, S5_N)), const((LANES, S5_N)), const((S5_N, LANES)), const((S5_N, LANES)),
                  const((1, S5_W)), const((S5_W, S5_W)), const((1, S5_W))],
        out_specs=[chunk, const((S5_W, S5_W)), const((1, S5_W)), const((1, S5_W)), const((S5_N, LANES)),
                   const((S5_N, LANES)), const((LANES, S5_N)), const((LANES, S5_N)), const((S5_TILES, LANES)),
                   const((S5_TILES, LANES))],
        out_shape=[jax.ShapeDtypeStruct((L, S5_W), BF16), jax.ShapeDtypeStruct((S5_W, S5_W), F32), vec, vec,
                   jax.ShapeDtypeStruct((S5_N, LANES), F32), jax.ShapeDtypeStruct((S5_N, LANES), F32),
                   jax.ShapeDtypeStruct((LANES, S5_N), F32), jax.ShapeDtypeStruct((LANES, S5_N), F32), tile, tile],
        scratch_shapes=[pltpu.VMEM((T * S5_TILES, LANES), F32), pltpu.VMEM((T * S5_TILES, LANES), F32),
                        pltpu.VMEM((S5_TILES, LANES), F32), pltpu.VMEM((S5_TILES, LANES), F32)], comm=comm)


_EYE8 = np.eye(8, dtype=np.float32)


def _compact_b(b):
    return jnp.einsum("akpc,kj->jcakp", b.reshape(4, 8, S5_P, S5_GC), _EYE8).reshape(LANES, S5_N)


def _uncompact_b(m):
    return jnp.einsum("kcakp->akpc", m.reshape(8, S5_GC, 4, 8, S5_P)).reshape(S5_G, S5_P, S5_GC)


_HEAD_MEAN = np.kron(np.eye(CONV_W // CONV_HD, dtype=np.float32), np.full((CONV_HD, CONV_HD), 1.0 / CONV_HD, np.float32))


def _head_norm(zc, mavg):
    mu = _dot_hi(zc, mavg)
    d = zc - mu
    rstd = lax.rsqrt(_dot_hi(d * d, mavg) + EPS)
    return d * rstd, rstd


def _conv_fwd(u, o_s5, wdw, bdw, lng, lnb, mavg, comm=()):
    L = u.shape[0]
    T = min(TS, L)
    n = L // T

    def body(v1_ref, v2_ref, s5_ref, w_ref, b_ref, g_ref, be_ref, m_ref, zc_ref, o_ref, zbuf):
        i = pl.program_id(0)

        @pl.when(i == 0)
        def _():
            zbuf[0:HALO, :] = jnp.zeros((HALO, CONV_W), F32)

        zbuf[HALO:HALO + T, :] = v1_ref[...] * _sigmoid(v2_ref[...])
        acc = jnp.broadcast_to(b_ref[...], (T, CONV_W))
        for k in range(CONV_K):
            acc = acc + w_ref[k:k + 1, :] * zbuf[pl.ds(HALO - (CONV_K - 1) + k, T), :]
        zc_ref[...] = acc
        zbuf[0:HALO, :] = zbuf[T:T + HALO, :]
        zn, _ = _head_norm(acc, m_ref[...])
        zz = zn * g_ref[...] + be_ref[...]
        o_ref[:, 0:S5_W] = s5_ref[...]
        o_ref[:, S5_W:S5_W + CONV_W] = (zz * _sigmoid(zz)).astype(BF16)

    const = lambda shape: pl.BlockSpec(shape, lambda i: (0, 0))
    vec = const((1, CONV_W))
    return _pallas(body, [u, u, o_s5, wdw, bdw, lng, lnb, mavg], name="conv_fwd", grid=(n,),
                   in_specs=[pl.BlockSpec((T, CONV_W), lambda i: (i, 1)), pl.BlockSpec((T, CONV_W), lambda i: (i, 2)),
                             pl.BlockSpec((T, S5_W), lambda i: (i, 0)), const((HALO, CONV_W)), vec, vec, vec,
                             const((CONV_W, CONV_W))],
                   out_specs=[pl.BlockSpec((T, CONV_W), lambda i: (i, 0)),
                              pl.BlockSpec((T, S5_W + CONV_W), lambda i: (i, 0))],
                   out_shape=[jax.ShapeDtypeStruct((L, CONV_W), F32), jax.ShapeDtypeStruct((L, S5_W + CONV_W), BF16)],
                   scratch_shapes=[pltpu.VMEM((T + HALO, CONV_W), F32)], comm=comm)


def _conv_bwd(dm, zc, u, du_s5, wdw, lng, lnb, mavg, comm=()):
    L = u.shape[0]
    T = min(TS, L)
    n = L // T
    hb = T // HALO

    def body(dm_ref, zc_ref, v1_ref, v2_ref, p1_ref, p2_ref, s5_ref, w_ref, g_ref, be_ref, m_ref,
             du_ref, dw_ref, db_ref, dg_ref, dbe_ref, zbuf, dzbuf, head):
        i = pl.program_id(0)
        first = i == 0

        @pl.when(first)
        def _():
            head[...] = jnp.zeros_like(head)
            dw_ref[...] = jnp.zeros_like(dw_ref)

        zn, rstd = _head_norm(zc_ref[...], m_ref[...])
        zz = zn * g_ref[...] + be_ref[...]
        sg = _sigmoid(zz)
        dzz = dm_ref[...].astype(F32) * sg * (1.0 + zz * (1.0 - sg))
        _accumulate(dbe_ref, first, _colsum(dzz))
        _accumulate(dg_ref, first, _colsum(dzz * zn))
        dzn = dzz * g_ref[...]
        dzc = rstd * (dzn - _dot_hi(dzn, m_ref[...]) - zn * _dot_hi(dzn * zn, m_ref[...]))
        _accumulate(db_ref, first, _colsum(dzc))

        dzbuf[0:T, :] = dzc
        dzbuf[T:T + HALO, :] = head[...]
        head[...] = dzc[0:HALO, :]
        keep = (i < n - 1).astype(F32)
        v1 = v1_ref[...]
        sg2 = _sigmoid(v2_ref[...])
        zbuf[0:HALO, :] = p1_ref[...] * _sigmoid(p2_ref[...]) * keep
        zbuf[HALO:HALO + T, :] = v1 * sg2

        dz = jnp.zeros((T, CONV_W), F32)
        for k in range(CONV_K):
            prod = dzc * zbuf[pl.ds(HALO - (CONV_K - 1) + k, T), :]
            dw_ref[8 * k:8 * k + 8, :] += jnp.sum(prod.reshape(T // 8, 8, CONV_W), axis=0)
            dz = dz + w_ref[k:k + 1, :] * dzbuf[pl.ds(CONV_K - 1 - k, T), :]
        du_ref[:, 0:S5_W] = s5_ref[...]
        du_ref[:, S5_W:S5_W + CONV_W] = (dz * sg2).astype(BF16)
        du_ref[:, S5_W + CONV_W:S5_W + 2 * CONV_W] = (dz * v1 * sg2 * (1.0 - sg2)).astype(BF16)

    rev = lambda c: (lambda i: (n - 1 - i, c))
    prev = lambda c: (lambda i: (jnp.maximum((n - 1 - i) * hb - 1, 0), c))
    const = lambda shape: pl.BlockSpec(shape, lambda i: (0, 0))
    vec = const((1, CONV_W))
    vshape = jax.ShapeDtypeStruct((1, CONV_W), F32)
    return _pallas(
        body, [dm, zc, u, u, u, u, du_s5, wdw, lng, lnb, mavg], name="conv_bwd", grid=(n,),
        in_specs=[pl.BlockSpec((T, CONV_W), rev(1)), pl.BlockSpec((T, CONV_W), rev(0)),
                  pl.BlockSpec((T, CONV_W), rev(1)), pl.BlockSpec((T, CONV_W), rev(2)),
                  pl.BlockSpec((HALO, CONV_W), prev(1)), pl.BlockSpec((HALO, CONV_W), prev(2)),
                  pl.BlockSpec((T, S5_W), rev(0)), const((HALO, CONV_W)), vec, vec, const((CONV_W, CONV_W))],
        out_specs=[pl.BlockSpec((T, S5_W + 2 * CONV_W), rev(0)), const((8 * HALO, CONV_W)), vec, vec, vec],
        out_shape=[jax.ShapeDtypeStruct((L, S5_W + 2 * CONV_W), BF16), jax.ShapeDtypeStruct((8 * HALO, CONV_W), F32),
                   vshape, vshape, vshape],
        scratch_shapes=[pltpu.VMEM((T + HALO, CONV_W), F32), pltpu.VMEM((T + HALO, CONV_W), F32),
                        pltpu.VMEM((HALO, CONV_W), F32)], comm=comm)


def _gather_all(v):
    rows, cols = v.shape

    def body(x_ref, out_ref, send_sems, recv_sems, local_sem):
        x, y, c, chips = _position()
        me, sibling = (x, y, c), (x, y, 1 - c)

        def block(px, py, pc):
            return out_ref.at[pl.ds((4 * px + 2 * py + pc) * rows, rows), :]

        def copy(k, blk, to, src=None):
            return pltpu.make_async_remote_copy(
                src_ref=block(*blk) if src is None else src, dst_ref=block(*blk), send_sem=send_sems.at[k],
                recv_sem=recv_sems.at[k], device_id=to, device_id_type=MESH)

        mine = pltpu.make_async_copy(x_ref, block(*me), local_sem)
        mine.start()
        first = [copy(0, me, sibling, src=x_ref)]
        first += [copy(1 + j, me, (*chip, c), src=x_ref) for j, chip in enumerate(chips)]
        for cp in first:
            cp.start()
        passed = [copy(4 + j, (*chip, c), sibling) for j, chip in enumerate(chips)]
        for j, chip in enumerate(chips):
            copy(1 + j, (*chip, c), me).wait_recv()
            passed[j].start()
        copy(0, sibling, me).wait_recv()
        for j, chip in enumerate(chips):
            copy(4 + j, (*chip, 1 - c), me).wait_recv()
        for cp in first + passed:
            cp.wait_send()
        mine.wait()

    return pl.pallas_call(
        body, name="gather_small",
        in_specs=[pl.BlockSpec(memory_space=pltpu.VMEM)], out_specs=pl.BlockSpec(memory_space=pltpu.VMEM),
        out_shape=jax.ShapeDtypeStruct((N_DEV * rows, cols), v.dtype),
        scratch_shapes=[pltpu.SemaphoreType.DMA((7,)), pltpu.SemaphoreType.DMA((7,)), pltpu.SemaphoreType.DMA],
        compiler_params=pltpu.CompilerParams(vmem_limit_bytes=VMEM_LIMIT))(v)


def _adamw(w, g, m, v):
    m = ADAM_B1 * m + (1.0 - ADAM_B1) * g
    v = ADAM_B2 * v + (1.0 - ADAM_B2) * jnp.square(g)
    m_hat = m / (1.0 - ADAM_B1 ** ADAM_STEP)
    v_hat = v / (1.0 - ADAM_B2 ** ADAM_STEP)
    return -ADAM_LR * (m_hat / (jnp.sqrt(v_hat) + ADAM_EPS) + ADAM_WD * w), m, v


def _sum_slots(recv, name, comm=()):
    _, rows, cols = recv.shape
    tr = _row_tile(rows, cols)

    def body(r_ref, o_ref):
        acc = r_ref[0].astype(F32)
        for s in range(1, N_CHIPS):
            acc = acc + r_ref[s].astype(F32)
        o_ref[...] = acc

    return _pallas(body, [recv], name=name, grid=(rows // tr,),
                   in_specs=[pl.BlockSpec((N_CHIPS, tr, cols), lambda i: (0, i, 0))],
                   out_specs=[pl.BlockSpec((tr, cols), lambda i: (i, 0))],
                   out_shape=[jax.ShapeDtypeStruct((rows, cols), F32)], comm=comm)[0]


def _adamw_sharded(w, part, other, m, v, name, comm=()):
    rows, cols = w.shape
    tr = _row_tile(rows, cols)

    def body(w_ref, p_ref, q_ref, m_ref, v_ref, g_ref, d_ref, nm_ref, nv_ref):
        g = p_ref[...] + q_ref[...]
        g_ref[...] = g
        d_ref[...], nm_ref[...], nv_ref[...] = _adamw(w_ref[...], g, m_ref[...], v_ref[...])

    spec = pl.BlockSpec((tr, cols), lambda i: (i, 0))
    shape = jax.ShapeDtypeStruct((rows, cols), F32)
    return _pallas(body, [w, part, other, m, v], name=name, grid=(rows // tr,), in_specs=[spec] * 5,
                   out_specs=[spec] * 4, out_shape=[shape] * 4, comm=comm)


def _adamw_small(w, gathered, m, v):
    rows, cols = w.shape

    def body(w_ref, a_ref, m_ref, v_ref, g_ref, d_ref, nm_ref, nv_ref):
        g = a_ref[0:rows, :]
        for dev in range(1, N_DEV):
            g = g + a_ref[dev * rows:(dev + 1) * rows, :]
        g_ref[...] = g
        d_ref[...], nm_ref[...], nv_ref[...] = _adamw(w_ref[...], g, m_ref[...], v_ref[...])

    shape = jax.ShapeDtypeStruct((rows, cols), F32)
    return pl.pallas_call(
        body, name="adamw_small", out_shape=[shape] * 4,
        compiler_params=pltpu.CompilerParams(vmem_limit_bytes=VMEM_LIMIT))(w, gathered, m, v)


def _pack_small(vals, last_row):
    rows = []
    for name in SMALL:
        flat = vals[name].reshape(-1).astype(F32)
        rows.append(jnp.pad(flat, (0, -flat.size % LANES)).reshape(-1, LANES))
    packed = jnp.concatenate(rows + [last_row], axis=0)
    return jnp.pad(packed, ((0, -packed.shape[0] % 8), (0, 0)))


def _unpack_small(packed, like):
    out, r = {}, 0
    for name in SMALL:
        size = like[name].size
        nrows = -(-size // LANES)
        out[name] = packed[r:r + nrows].reshape(-1)[:size].reshape(like[name].shape)
        r += nrows
    return out, packed[r, 0]


def _shard2d(name, v):
    v = v.reshape(v.shape[-2:])
    return v.T if name in FFN_T else v


def _unshard(name, v, shape):
    return (v.T if name in FFN_T else v).reshape(shape)


def _train_step(x3, tgt3, wts, ms, vs):
    x, tgt = x3[0], tgt3[0]
    L, D = x.shape
    row = lambda v: v.reshape(1, -1)
    shards = {k: _shard2d(k, wts[k]) for k in SHARDED}
    sends = {k: shards[k] if k == "conv_w_dw" else _cast_bf16(shards[k], "cast_" + k) for k in SHARDED}
    gat = {k: _Gather(sends[k]) for k in SHARDED}
    w = lambda k: gat[k].result[0]

    s = {k: wts[k] for k in SMALL}
    lr, li = s["s5_lam_re"].reshape(1, S5_N), s["s5_lam_im"].reshape(1, S5_N)
    ldt = jnp.repeat(s["s5_log_dt"].reshape(S5_G), S5_P).reshape(1, S5_N)
    brc, bic = _compact_b(s["s5_b_re"].reshape(S5_G, S5_P, S5_GC)), _compact_b(s["s5_b_im"].reshape(S5_G, S5_P, S5_GC))
    crc = _compact_b(s["s5_c_re"].reshape(S5_G, S5_GC, S5_P).transpose(0, 2, 1)).T
    cic = _compact_b(s["s5_c_im"].reshape(S5_G, S5_GC, S5_P).transpose(0, 2, 1)).T
    d_skip, b_glu = row(s["s5_d"]), row(s["s5_b_glu"])
    b_dw, ln_g, ln_b = row(s["conv_b_dw"]), row(s["conv_ln_g"]), row(s["conv_ln_b"])
    g1, gm, g2, gf = row(s["ffn1_norm"]), row(s["mix_norm"]), row(s["ffn2_norm"]), row(s["final_norm"])
    mavg = jnp.asarray(_HEAD_MEAN)

    h1 = _rms_fwd(x, g1, "rms1", comm=[gat["ffn1_w_gate"], gat["ffn1_w_up"]])
    a1, b1, act1 = _ffn_up(h1, w("ffn1_w_gate"), w("ffn1_w_up"), "ffn1_up", comm=[gat["ffn1_w_down"]])
    x1, h2 = _ffn_down(act1, w("ffn1_w_down"), x, gm, "ffn1_down",
                       comm=[gat["w_in"], gat["s5_w_glu"], gat["conv_w_dw"], gat["w_out"]])
    u = _mm_grouped(h2, w("w_in"), "in_proj")
    are, aim, bre, bim, cre, cim = _s5_params_fwd(lr, li, ldt, brc, bic, crc, cic)
    are_t, aim_t = are.reshape(S5_TILES, LANES), aim.reshape(S5_TILES, LANES)
    w_glu = w("s5_w_glu").reshape(S5_W, S5_W)
    s_re, s_im, y_pre, o_s5 = _s5_fwd(u, are_t, aim_t, bre, bim, cre, cim, d_skip, w_glu, b_glu,
                                      comm=[gat["ffn2_w_gate"], gat["ffn2_w_up"]])
    w_dw = w("conv_w_dw").transpose(1, 0, 2).reshape(CONV_K, CONV_W)
    w_dw = jnp.concatenate([w_dw, jnp.zeros((HALO - CONV_K, CONV_W), F32)], axis=0)
    zc, cat = _conv_fwd(u, o_s5, w_dw, b_dw, ln_g, ln_b, mavg, comm=[gat["ffn2_w_down"]])
    w_out = w("w_out").reshape(-1, D)
    x2, h3 = _mix_out(cat, w_out, x1, g2, "mix_out")
    a2, b2, act2 = _ffn_up(h3, w("ffn2_w_gate"), w("ffn2_w_up"), "ffn2_up")
    dx3, loss_part, d_gf = _ffn_down_loss(act2, w("ffn2_w_down"), x2, gf, tgt, "ffn2_down_loss")

    gs, sc = {"final_norm": d_gf}, {}
    da2, db2 = _ffn_bwd_act(dx3, w("ffn2_w_down"), a2, b2, "ffn2_bwd_act")
    sc["ffn2_w_down"] = _Scatter(_mm_tn(act2, [dx3[None]], 0.5, "ffn2_dwd", N_CHIPS)[0])
    sc["ffn2_w_gate"] = _Scatter(_mm_tn(da2, [h3[None]], 1.0, "ffn2_dwg", N_CHIPS, comm=[sc["ffn2_w_down"]])[0])
    sc["ffn2_w_up"] = _Scatter(_mm_tn(db2, [h3[None]], 1.0, "ffn2_dwu", N_CHIPS, comm=[sc["ffn2_w_gate"]])[0])
    dx2, gs["ffn2_norm"] = _mm_rmsbwd([da2, db2], [w("ffn2_w_gate"), w("ffn2_w_up")], False, False, x2, g2, dx3,
                                      "ffn2_bwd_dx", comm=[sc["ffn2_w_up"]])

    dm = _mm_nt(dx2, w_out, "mix_bwd")
    sc["w_out"] = _Scatter(_mm_tn(cat[None], [dx2[None]], 1.0, "dwout", 1)[0].reshape(N_CHIPS, -1, D))
    (du_s5, d_wglu, gs["s5_b_glu"], gs["s5_d"], d_crc, d_cic, d_bre, d_bim, d_are, d_aim) = _s5_bwd(
        dm, y_pre, u, s_re, s_im, are_t, aim_t, bre, bim, cre, cim, d_skip, w_glu, b_glu, comm=[sc["w_out"]])
    sc["s5_w_glu"] = _Scatter(d_wglu.astype(BF16).reshape(N_CHIPS, -1, S5_W))
    g_lr, g_li, g_ldt, g_brc, g_bic = _s5_params_bwd(lr, li, ldt, brc, bic, d_are.reshape(1, S5_N),
                                                     d_aim.reshape(1, S5_N), d_bre, d_bim)
    gs["s5_lam_re"], gs["s5_lam_im"] = g_lr, g_li
    gs["s5_log_dt"] = jnp.sum(g_ldt.reshape(S5_G, S5_P), axis=1)
    gs["s5_b_re"], gs["s5_b_im"] = _uncompact_b(g_brc), _uncompact_b(g_bic)
    gs["s5_c_re"] = _uncompact_b(d_crc.T).transpose(0, 2, 1)
    gs["s5_c_im"] = _uncompact_b(d_cic.T).transpose(0, 2, 1)
    du, d_wdw, gs["conv_b_dw"], gs["conv_ln_g"], gs["conv_ln_b"] = _conv_bwd(
        dm, zc, u, du_s5, w_dw, ln_g, ln_b, mavg, comm=[sc["s5_w_glu"]])
    d_wdw = jnp.sum(d_wdw.reshape(HALO, 8, CONV_W), axis=1)[:CONV_K]
    sc["conv_w_dw"] = _Scatter(d_wdw.reshape(CONV_K, N_CHIPS, -1).transpose(1, 0, 2))
    sc["w_in"] = _Scatter(_mm_tn(h2[None], [du], 1.0, "dwin", N_CHIPS, b_cols=True, comm=[sc["conv_w_dw"]])[0])
    dx1, gs["mix_norm"] = _mm_rmsbwd([du], [w("w_in")], True, True, x1, gm, dx2, "in_proj_bwd", comm=[sc["w_in"]])

    da1, db1 = _ffn_bwd_act(dx1, w("ffn1_w_down"), a1, b1, "ffn1_bwd_act")
    sc["ffn1_w_down"] = _Scatter(_mm_tn(act1, [dx1[None]], 0.5, "ffn1_dwd", N_CHIPS)[0])
    sc["ffn1_w_gate"] = _Scatter(_mm_tn(da1, [h1[None]], 1.0, "ffn1_dwg", N_CHIPS, comm=[sc["ffn1_w_down"]])[0])
    sc["ffn1_w_up"] = _Scatter(_mm_tn(db1, [h1[None]], 1.0, "ffn1_dwu", N_CHIPS, comm=[sc["ffn1_w_gate"]])[0])
    grad_x, gs["ffn1_norm"] = _mm_rmsbwd([da1, db1], [w("ffn1_w_gate"), w("ffn1_w_up")], False, False, x, g1, dx1,
                                         "ffn1_bwd_dx", comm=[sc["ffn1_w_up"]])

    out = {}
    gsmall = {k: gs[k].reshape(wts[k].shape) for k in SMALL}
    zero_row = jnp.zeros((1, LANES), F32)
    g_all = _gather_all(_pack_small(gsmall, loss_part))
    res = _adamw_small(_pack_small(s, zero_row), g_all, _pack_small({k: ms[k] for k in SMALL}, zero_row),
                       _pack_small({k: vs[k] for k in SMALL}, zero_row))
    unpacked = [_unpack_small(r, s) for r in res]
    loss = unpacked[0][1]
    for k in SMALL:
        out[k] = [u_[0][k] for u_ in unpacked]

    order = ("ffn2_w_down", "ffn2_w_gate", "ffn2_w_up", "w_out", "s5_w_glu", "conv_w_dw", "w_in", "ffn1_w_down",
             "ffn1_w_gate", "ffn1_w_up")
    swaps, riding = {}, []
    for k in order:
        swaps[k] = _Swap(_sum_slots(sc[k].result[0], "sum_" + k, comm=riding))
        riding = [swaps[k]]
    for k in order:
        res = _adamw_sharded(shards[k], swaps[k].ins[0], swaps[k].result[0], _shard2d(k, ms[k]), _shard2d(k, vs[k]),
                             "adamw_" + k, comm=riding)
        riding = []
        out[k] = [_unshard(k, r, wts[k].shape) for r in res]
    return loss, grad_x[None], out


def kernel(x, ffn1_norm, ffn1_w_gate, ffn1_w_up, ffn1_w_down, mix_norm, w_in, s5_lam_re, s5_lam_im, s5_log_dt, s5_b_re, s5_b_im, s5_c_re, s5_c_im, s5_d, s5_w_glu, s5_b_glu, conv_w_dw, conv_b_dw, conv_ln_g, conv_ln_b, w_out, ffn2_norm, ffn2_w_gate, ffn2_w_up, ffn2_w_down, final_norm, loss_target, m_ffn1_norm, m_ffn1_w_gate, m_ffn1_w_up, m_ffn1_w_down, m_mix_norm, m_w_in, m_s5_lam_re, m_s5_lam_im, m_s5_log_dt, m_s5_b_re, m_s5_b_im, m_s5_c_re, m_s5_c_im, m_s5_d, m_s5_w_glu, m_s5_b_glu, m_conv_w_dw, m_conv_b_dw, m_conv_ln_g, m_conv_ln_b, m_w_out, m_ffn2_norm, m_ffn2_w_gate, m_ffn2_w_up, m_ffn2_w_down, m_final_norm, v_ffn1_norm, v_ffn1_w_gate, v_ffn1_w_up, v_ffn1_w_down, v_mix_norm, v_w_in, v_s5_lam_re, v_s5_lam_im, v_s5_log_dt, v_s5_b_re, v_s5_b_im, v_s5_c_re, v_s5_c_im, v_s5_d, v_s5_w_glu, v_s5_b_glu, v_conv_w_dw, v_conv_b_dw, v_conv_ln_g, v_conv_ln_b, v_w_out, v_ffn2_norm, v_ffn2_w_gate, v_ffn2_w_up, v_ffn2_w_down, v_final_norm):
    given = dict(locals())
    wts = {k: given[k] for k in WEIGHTS}
    ms = {k: given["m_" + k] for k in WEIGHTS}
    vs = {k: given["v_" + k] for k in WEIGHTS}
    loss, grad_x, out = _train_step(x, loss_target, wts, ms, vs)
    return (loss, grad_x, *[out[k][0] for k in WEIGHTS], *[out[k][1] for k in WEIGHTS],
            *[out[k][2] for k in WEIGHTS], *[out[k][3] for k in WEIGHTS])
```

```python
import functools

import jax
import jax.numpy as jnp
import numpy as np
from jax import lax
from jax.experimental import pallas as pl
from jax.experimental.pallas import tpu as pltpu

F32, BF16 = jnp.float32, jnp.bfloat16
MESH = pl.DeviceIdType.MESH

EPS = 1e-6
ADAM_LR, ADAM_B1, ADAM_B2, ADAM_EPS, ADAM_WD, ADAM_STEP = 0.001, 0.9, 0.999, 1e-08, 0.01, 10

N_CHIPS = 4
N_DEV = 8
LANES = 128
BF16_ROWS = 16
S5_W, S5_G, S5_GC, S5_P = 512, 32, 16, 64
S5_N = S5_G * S5_P
S5_TILES = S5_N // LANES
S5_HALF = 8
CONV_W, CONV_K, CONV_HD = 512, 31, 64
HALO = 32
CONV_SB = 32
TM = 512
TK = 2048
TS = 256
VMEM_LIMIT = 48 << 20
GELU_C0, GELU_C1 = 0.7978845608028654, 0.044715

FFN_T = ("ffn1_w_gate", "ffn1_w_up", "ffn2_w_gate", "ffn2_w_up")
SHARDED = ("ffn1_w_gate", "ffn1_w_up", "ffn1_w_down", "w_in", "s5_w_glu", "conv_w_dw", "w_out",
           "ffn2_w_gate", "ffn2_w_up", "ffn2_w_down")
SMALL = ("ffn1_norm", "mix_norm", "s5_lam_re", "s5_lam_im", "s5_log_dt", "s5_b_re", "s5_b_im", "s5_c_re",
         "s5_c_im", "s5_d", "s5_b_glu", "conv_b_dw", "conv_ln_g", "conv_ln_b", "ffn2_norm", "final_norm")
WEIGHTS = ("ffn1_norm", "ffn1_w_gate", "ffn1_w_up", "ffn1_w_down", "mix_norm", "w_in", "s5_lam_re", "s5_lam_im",
           "s5_log_dt", "s5_b_re", "s5_b_im", "s5_c_re", "s5_c_im", "s5_d", "s5_w_glu", "s5_b_glu", "conv_w_dw",
           "conv_b_dw", "conv_ln_g", "conv_ln_b", "w_out", "ffn2_norm", "ffn2_w_gate", "ffn2_w_up", "ffn2_w_down",
           "final_norm")


def _dot(a, b):
    return jnp.dot(a, b, preferred_element_type=F32)


def _dot_nt(a, b):
    return lax.dot_general(a, b, (((1,), (1,)), ((), ())), preferred_element_type=F32)


def _dot_tn(a, b):
    return lax.dot_general(a, b, (((0,), (0,)), ((), ())), preferred_element_type=F32)


def _colsum(v):
    return jnp.sum(v, axis=0, keepdims=True)


def _sigmoid(v):
    return 1.0 / (1.0 + jnp.exp(-v))


def _accumulate(ref, first, value):
    @pl.when(first)
    def _():
        ref[...] = value

    @pl.when(jnp.logical_not(first))
    def _():
        ref[...] += value


def _position():
    x, y, c = lax.axis_index("x"), lax.axis_index("y"), lax.axis_index("c")
    return x, y, c, [(1 - x, y), (x, 1 - y), (1 - x, 1 - y)]


def _remote(src, dst, sems, send, recv, device):
    return pltpu.make_async_remote_copy(src_ref=src, dst_ref=dst, send_sem=sems.at[send], recv_sem=sems.at[recv],
                                        device_id=device, device_id_type=MESH)


class _Gather:
    def __init__(self, shard):
        self.ins = [shard]
        self.outs = [jax.ShapeDtypeStruct((N_CHIPS,) + shard.shape, shard.dtype)]
        self.rows = shard.shape[0]
        self.halve = shard.dtype == BF16 and self.rows % (2 * BF16_ROWS) == 0
        self.n_sem = 13 if self.halve else 7
        self.result = None

    def _copies(self, ins, outs, sems, s0, pos):
        x, y, c, chips = pos
        src, dst = ins[0], outs[0]
        me = 2 * x + y
        if self.halve:
            hr = self.rows // 2
            mine, theirs = pl.ds(c * hr, hr), pl.ds((1 - c) * hr, hr)
            part = lambda slot, rows: dst.at[slot, rows]
            my_src = src.at[mine]
        else:
            mine = theirs = None
            part = lambda slot, rows: dst.at[slot]
            my_src = src
        slot = lambda j: 2 * chips[j][0] + chips[j][1]
        local = lambda: pltpu.make_async_copy(src, dst.at[me], sems.at[s0])
        send = lambda j: _remote(my_src, part(me, mine), sems, s0 + 1 + j, s0 + 4 + j, (*chips[j], c))
        land = lambda j: _remote(my_src, part(slot(j), mine), sems, s0 + 1 + j, s0 + 4 + j, (*chips[j], c))
        fwd = lambda j: _remote(part(slot(j), mine), part(slot(j), mine), sems, s0 + 7 + j, s0 + 10 + j, (x, y, 1 - c))
        got = lambda j: _remote(part(slot(j), theirs), part(slot(j), theirs), sems, s0 + 7 + j, s0 + 10 + j,
                                (x, y, 1 - c))
        return local, send, land, fwd, got

    def start(self, ins, outs, sems, s0, pos):
        local, send, _, _, _ = self._copies(ins, outs, sems, s0, pos)
        local().start()
        for j in range(N_CHIPS - 1):
            send(j).start()

    def finish(self, ins, outs, sems, s0, pos):
        local, send, land, fwd, got = self._copies(ins, outs, sems, s0, pos)
        others = range(N_CHIPS - 1)
        for j in others:
            land(j).wait_recv()
            if self.halve:
                fwd(j).start()
        for j in others:
            if self.halve:
                got(j).wait_recv()
        for j in others:
            send(j).wait_send()
            if self.halve:
                fwd(j).wait_send()
        local().wait()


class _Scatter:
    def __init__(self, grad):
        self.ins = [grad]
        self.outs = [jax.ShapeDtypeStruct(grad.shape, grad.dtype)]
        self.n_sem = 7
        self.result = None

    def _copies(self, ins, outs, sems, s0, pos):
        x, y, c, chips = pos
        src, dst = ins[0], outs[0]
        me = 2 * x + y
        slot = lambda j: 2 * chips[j][0] + chips[j][1]
        local = lambda: pltpu.make_async_copy(src.at[me], dst.at[me], sems.at[s0])
        send = lambda j: _remote(src.at[slot(j)], dst.at[me], sems, s0 + 1 + j, s0 + 4 + j, (*chips[j], c))
        land = lambda j: _remote(src.at[me], dst.at[slot(j)], sems, s0 + 1 + j, s0 + 4 + j, (*chips[j], c))
        return local, send, land

    def start(self, ins, outs, sems, s0, pos):
        local, send, _ = self._copies(ins, outs, sems, s0, pos)
        local().start()
        for j in range(N_CHIPS - 1):
            send(j).start()

    def finish(self, ins, outs, sems, s0, pos):
        local, send, land = self._copies(ins, outs, sems, s0, pos)
        for j in range(N_CHIPS - 1):
            land(j).wait_recv()
        for j in range(N_CHIPS - 1):
            send(j).wait_send()
        local().wait()


class _Swap:
    def __init__(self, part):
        self.ins = [part]
        self.outs = [jax.ShapeDtypeStruct(part.shape, part.dtype)]
        self.n_sem = 2
        self.result = None

    def _copy(self, ins, outs, sems, s0, pos):
        x, y, c, _ = pos
        return _remote(ins[0], outs[0], sems, s0, s0 + 1, (x, y, 1 - c))

    def start(self, ins, outs, sems, s0, pos):
        self._copy(ins, outs, sems, s0, pos).start()

    def finish(self, ins, outs, sems, s0, pos):
        self._copy(ins, outs, sems, s0, pos).wait()


def _pallas(body, args, *, name, grid, in_specs, out_specs, out_shape, scratch_shapes=(), comm=()):
    comm = list(comm)
    n_in, n_out, n_scr = len(in_specs), len(out_specs), len(scratch_shapes)
    c_in = [a for op in comm for a in op.ins]
    c_out = [s for op in comm for s in op.outs]
    n_sem = sum(op.n_sem for op in comm)

    def full(*refs):
        o0 = n_in + len(c_in)
        s0 = o0 + n_out + len(c_out)
        ins, cin = refs[:n_in], refs[n_in:o0]
        outs, cout = refs[o0:o0 + n_out], refs[o0 + n_out:s0]
        scratch = refs[s0:s0 + n_scr]
        if comm:
            sems = refs[s0 + n_scr]
            ids = [pl.program_id(d) for d in range(len(grid))]
            first = functools.reduce(jnp.logical_and, [i == 0 for i in ids])
            last = functools.reduce(jnp.logical_and, [i == g - 1 for i, g in zip(ids, grid)])
            pos = _position()

            def each(step):
                ci = co = cs = 0
                for op in comm:
                    getattr(op, step)(cin[ci:ci + len(op.ins)], cout[co:co + len(op.outs)], sems, cs, pos)
                    ci, co, cs = ci + len(op.ins), co + len(op.outs), cs + op.n_sem

            @pl.when(first)
            def _():
                each("start")

        body(*ins, *outs, *scratch)
        if comm:
            @pl.when(last)
            def _():
                each("finish")

    hbm = pl.BlockSpec(memory_space=pl.ANY)
    res = pl.pallas_call(
        full, name=name, grid=grid,
        in_specs=list(in_specs) + [hbm] * len(c_in), out_specs=list(out_specs) + [hbm] * len(c_out),
        out_shape=list(out_shape) + c_out,
        scratch_shapes=list(scratch_shapes) + ([pltpu.SemaphoreType.DMA((n_sem,))] if comm else []),
        compiler_params=pltpu.CompilerParams(dimension_semantics=("arbitrary",) * len(grid),
                                             vmem_limit_bytes=VMEM_LIMIT))(*args, *c_in)
    k = n_out
    for op in comm:
        op.result = list(res[k:k + len(op.outs)])
        k += len(op.outs)
    return list(res[:n_out])


def _row_tile(rows, cols, itemsize=4, budget=1 << 20):
    t = rows
    while t % 16 == 0 and t * cols * itemsize > budget:
        t //= 2
    return t


def _cast_bf16(w, name):
    rows, cols = w.shape
    tr = _row_tile(rows, cols)

    def body(w_ref, o_ref):
        o_ref[...] = w_ref[...].astype(BF16)

    spec = pl.BlockSpec((tr, cols), lambda i: (i, 0))
    return _pallas(body, [w], name=name, grid=(rows // tr,), in_specs=[spec], out_specs=[spec],
                   out_shape=[jax.ShapeDtypeStruct((rows, cols), BF16)])[0]


def _rms_fwd(x, g, name, comm=()):
    L, D = x.shape

    def body(x_ref, g_ref, h_ref):
        xf = x_ref[...]
        r = lax.rsqrt(jnp.mean(xf * xf, axis=-1, keepdims=True) + EPS)
        h_ref[...] = (xf * r * g_ref[...]).astype(BF16)

    row = pl.BlockSpec((TM, D), lambda i: (i, 0))
    return _pallas(body, [x, g], name=name, grid=(L // TM,),
                   in_specs=[row, pl.BlockSpec((1, D), lambda i: (0, 0))], out_specs=[row],
                   out_shape=[jax.ShapeDtypeStruct((L, D), BF16)], comm=comm)[0]


def _ffn_up(h, wg_t, wu_t, name, comm=()):
    L, D = h.shape
    G, FS, _ = wg_t.shape

    def body(h_ref, wg_ref, wu_ref, a_ref, b_ref, act_ref):
        hv = h_ref[...]
        a = _dot_nt(hv, wg_ref[...])
        b = _dot_nt(hv, wu_ref[...])
        a_ref[...] = a.astype(BF16)
        b_ref[...] = b.astype(BF16)
        act_ref[...] = (a * _sigmoid(a) * b).astype(BF16)

    wspec = pl.BlockSpec((None, FS, D), lambda j, i: (j, 0, 0))
    ospec = pl.BlockSpec((None, TM, FS), lambda j, i: (j, i, 0))
    oshape = jax.ShapeDtypeStruct((G, L, FS), BF16)
    return _pallas(body, [h, wg_t, wu_t], name=name, grid=(G, L // TM),
                   in_specs=[pl.BlockSpec((TM, D), lambda j, i: (i, 0)), wspec, wspec],
                   out_specs=[ospec, ospec, ospec], out_shape=[oshape, oshape, oshape], comm=comm)


def _resident(shape):
    return pl.BlockSpec(shape, lambda *_: (0,) * len(shape), pipeline_mode=pl.Buffered(1))


def _group_sum(a_ref, w_ref, groups, mm=_dot):
    acc = mm(a_ref[0], w_ref[0])
    for j in range(1, groups):
        acc = acc + mm(a_ref[j], w_ref[j])
    return acc


def _ffn_down(act, wd, x, g_next, name, comm=()):
    G, L, FS = act.shape
    D = wd.shape[2]

    def body(act_ref, wd_ref, x_ref, g_ref, xn_ref, hn_ref):
        xn = x_ref[...] + 0.5 * _group_sum(act_ref, wd_ref, G)
        xn_ref[...] = xn
        r = lax.rsqrt(jnp.mean(xn * xn, axis=-1, keepdims=True) + EPS)
        hn_ref[...] = (xn * r * g_ref[...]).astype(BF16)

    row = pl.BlockSpec((TM, D), lambda i: (i, 0))
    return _pallas(body, [act, wd, x, g_next], name=name, grid=(L // TM,),
                   in_specs=[pl.BlockSpec((G, TM, FS), lambda i: (0, i, 0)), _resident((G, FS, D)), row,
                             pl.BlockSpec((1, D), lambda i: (0, 0))],
                   out_specs=[row, row],
                   out_shape=[jax.ShapeDtypeStruct((L, D), F32), jax.ShapeDtypeStruct((L, D), BF16)], comm=comm)


def _ffn_down_loss(act, wd, x, gf, tgt, name):
    G, L, FS = act.shape
    D = wd.shape[2]

    def body(act_ref, wd_ref, x_ref, g_ref, t_ref, dx_ref, dxb_ref, loss_ref, dg_ref):
        i = pl.program_id(0)
        xn = x_ref[...] + 0.5 * _group_sum(act_ref, wd_ref, G)
        r = lax.rsqrt(jnp.mean(xn * xn, axis=-1, keepdims=True) + EPS)
        xh = xn * r
        gv = g_ref[...]
        e = xh * gv - t_ref[...]
        part = 0.5 * jnp.sum(_colsum(e * e), axis=1, keepdims=True) / D
        dy = e / D
        _accumulate(loss_ref, i == 0, jnp.broadcast_to(part, (1, LANES)))
        _accumulate(dg_ref, i == 0, _colsum(dy * xh))
        dxh = dy * gv
        dx = r * (dxh - xh * jnp.mean(dxh * xh, axis=-1, keepdims=True))
        dx_ref[...] = dx
        dxb_ref[...] = dx.astype(BF16)

    row = pl.BlockSpec((TM, D), lambda i: (i, 0))
    return _pallas(body, [act, wd, x, gf, tgt], name=name, grid=(L // TM,),
                   in_specs=[pl.BlockSpec((G, TM, FS), lambda i: (0, i, 0)), _resident((G, FS, D)), row,
                             pl.BlockSpec((1, D), lambda i: (0, 0)), row],
                   out_specs=[row, row, pl.BlockSpec((1, LANES), lambda i: (0, 0)),
                              pl.BlockSpec((1, D), lambda i: (0, 0))],
                   out_shape=[jax.ShapeDtypeStruct((L, D), F32), jax.ShapeDtypeStruct((L, D), BF16),
                              jax.ShapeDtypeStruct((1, LANES), F32), jax.ShapeDtypeStruct((1, D), F32)])


def _ffn_bwd_act(dxb, wd, a, b, name):
    L, D = dxb.shape
    G, FS, _ = wd.shape

    def body(dx_ref, wd_ref, a_ref, b_ref, da_ref, db_ref):
        dact = 0.5 * _dot_nt(dx_ref[...], wd_ref[...])
        av = a_ref[...].astype(F32)
        bv = b_ref[...].astype(F32)
        sg = _sigmoid(av)
        da_ref[...] = (dact * bv * sg * (1.0 + av * (1.0 - sg))).astype(BF16)
        db_ref[...] = (dact * av * sg).astype(BF16)

    gspec = pl.BlockSpec((None, TM, FS), lambda j, i: (j, i, 0))
    oshape = jax.ShapeDtypeStruct((G, L, FS), BF16)
    return _pallas(body, [dxb, wd, a, b], name=name, grid=(G, L // TM),
                   in_specs=[pl.BlockSpec((TM, D), lambda j, i: (i, 0)),
                             pl.BlockSpec((None, FS, D), lambda j, i: (j, 0, 0)), gspec, gspec],
                   out_specs=[gspec, gspec], out_shape=[oshape, oshape])


def _mm_grouped(a, w, name):
    L, K = a.shape
    G, _, N = w.shape

    def body(a_ref, w_ref, o_ref):
        o_ref[...] = _dot(a_ref[...], w_ref[...])

    return _pallas(body, [a, w], name=name, grid=(G, L // TM),
                   in_specs=[pl.BlockSpec((TM, K), lambda g, i: (i, 0)), pl.BlockSpec((None, K, N), lambda g, i: (g, 0, 0))],
                   out_specs=[pl.BlockSpec((TM, N), lambda g, i: (i, g))],
                   out_shape=[jax.ShapeDtypeStruct((L, G * N), F32)])[0]


def _mm_nt(a, w, name):
    L, K = a.shape
    N = w.shape[0]

    def body(a_ref, w_ref, o_ref):
        o_ref[...] = _dot_nt(a_ref[...], w_ref[...]).astype(BF16)

    return _pallas(body, [a, w], name=name, grid=(L // TM,),
                   in_specs=[pl.BlockSpec((TM, K), lambda i: (i, 0)), _resident((N, K))],
                   out_specs=[pl.BlockSpec((TM, N), lambda i: (i, 0))],
                   out_shape=[jax.ShapeDtypeStruct((L, N), BF16)])[0]


def _mm_tn(a, b, scale, name, groups, b_cols=False, comm=()):
    L, M = a.shape[1], a.shape[2]
    N = b.shape[1] // groups if b_cols else b.shape[2]
    tk = min(L, TK)
    nk = L // tk

    def spec(v, cols):
        if cols:
            return pl.BlockSpec((tk, v.shape[1] // groups), lambda g, k: (k, g))
        if v.shape[0] > 1:
            return pl.BlockSpec((None, tk, v.shape[2]), lambda g, k: (g, k, 0))
        return pl.BlockSpec((None, tk, v.shape[2]), lambda g, k: (0, k, 0))

    def body(a_ref, b_ref, o_ref, acc):
        k = pl.program_id(1)
        p = _dot_tn(a_ref[...], b_ref[...])
        if nk == 1:
            o_ref[...] = (p * scale).astype(BF16)
        else:
            _accumulate(acc, k == 0, p)

            @pl.when(k == nk - 1)
            def _():
                o_ref[...] = (acc[...] * scale).astype(BF16)

    return _pallas(body, [a, b], name=name, grid=(groups, nk),
                   in_specs=[spec(a, False), spec(b, b_cols)],
                   out_specs=[pl.BlockSpec((None, M, N), lambda g, k: (g, 0, 0))],
                   out_shape=[jax.ShapeDtypeStruct((groups, M, N), BF16)],
                   scratch_shapes=[pltpu.VMEM((M, N), F32)], comm=comm)[0]


def _mm_rmsbwd(a_list, w_list, nt, a_cols, x_in, g, dx_out, name, comm=()):
    P = len(a_list)
    G = w_list[0].shape[0]
    L, D = x_in.shape
    mm = _dot_nt if nt else _dot

    def body(*refs):
        a_refs, w_refs = refs[:P], refs[P:2 * P]
        x_ref, g_ref, dxo_ref, dx_ref, dxb_ref, dg_ref = refs[2 * P:]
        i = pl.program_id(0)
        dh = None
        for a_ref, w_ref in zip(a_refs, w_refs):
            for j in range(G):
                if a_cols:
                    kw = a_ref.shape[1] // G
                    term = mm(a_ref[:, j * kw:(j + 1) * kw], w_ref[j])
                else:
                    term = mm(a_ref[j], w_ref[j])
                dh = term if dh is None else dh + term
        xf = x_ref[...]
        r = lax.rsqrt(jnp.mean(xf * xf, axis=-1, keepdims=True) + EPS)
        xh = xf * r
        _accumulate(dg_ref, i == 0, _colsum(dh * xh))
        dxh = dh * g_ref[...]
        dx = dxo_ref[...] + r * (dxh - xh * jnp.mean(dxh * xh, axis=-1, keepdims=True))
        dx_ref[...] = dx
        dxb_ref[...] = dx.astype(BF16)

    row = pl.BlockSpec((TM, D), lambda i: (i, 0))
    vec = pl.BlockSpec((1, D), lambda i: (0, 0))
    if a_cols:
        a_specs = [pl.BlockSpec((TM, a.shape[1]), lambda i: (i, 0)) for a in a_list]
    else:
        a_specs = [pl.BlockSpec((G, TM, a.shape[2]), lambda i: (0, i, 0)) for a in a_list]
    w_specs = [_resident(w.shape) for w in w_list]
    return _pallas(body, [*a_list, *w_list, x_in, g, dx_out], name=name, grid=(L // TM,),
                   in_specs=a_specs + w_specs + [row, vec, row], out_specs=[row, row, vec],
                   out_shape=[jax.ShapeDtypeStruct((L, D), F32), jax.ShapeDtypeStruct((L, D), BF16),
                              jax.ShapeDtypeStruct((1, D), F32)], comm=comm)


def _mix_out(cat, wout, x1, g_next, name):
    L, K = cat.shape
    D = wout.shape[1]

    def body(c_ref, w_ref, x_ref, g_ref, xn_ref, hn_ref):
        xn = x_ref[...] + _dot(c_ref[...], w_ref[...])
        xn_ref[...] = xn
        r = lax.rsqrt(jnp.mean(xn * xn, axis=-1, keepdims=True) + EPS)
        hn_ref[...] = (xn * r * g_ref[...]).astype(BF16)

    row = pl.BlockSpec((TM, D), lambda i: (i, 0))
    return _pallas(body, [cat, wout, x1, g_next], name=name, grid=(L // TM,),
                   in_specs=[pl.BlockSpec((TM, K), lambda i: (i, 0)), pl.BlockSpec((K, D), lambda i: (0, 0)), row,
                             pl.BlockSpec((1, D), lambda i: (0, 0))],
                   out_specs=[row, row],
                   out_shape=[jax.ShapeDtypeStruct((L, D), F32), jax.ShapeDtypeStruct((L, D), BF16)])


def _s5_disc(lr, li, ldt, brc, bic):
    dt = jnp.exp(ldt)
    mag = jnp.exp(lr * dt)
    are = mag * jnp.cos(li * dt)
    aim = mag * jnp.sin(li * dt)
    den = lr * lr + li * li
    nre = are - 1.0
    fre = (nre * lr + aim * li) / den
    fim = (aim * lr - nre * li) / den
    return are, aim, fre * brc - fim * bic, fre * bic + fim * brc


def _s5_params_fwd(lr, li, ldt, brc, bic, crc, cic):
    def body(lr_ref, li_ref, ldt_ref, br_ref, bi_ref, cr_ref, ci_ref, are_ref, aim_ref, bre_ref, bim_ref, cre_ref, cim_ref):
        are, aim, bre, bim = _s5_disc(lr_ref[...], li_ref[...], ldt_ref[...], br_ref[...], bi_ref[...])
        are_ref[...] = are
        aim_ref[...] = aim
        bre_ref[...] = bre.astype(BF16)
        bim_ref[...] = bim.astype(BF16)
        cre_ref[...] = cr_ref[...].astype(BF16)
        cim_ref[...] = ci_ref[...].astype(BF16)

    vec = jax.ShapeDtypeStruct((1, S5_N), F32)
    return pl.pallas_call(
        body, name="s5_params_fwd",
        out_shape=[vec, vec, jax.ShapeDtypeStruct((LANES, S5_N), BF16), jax.ShapeDtypeStruct((LANES, S5_N), BF16),
                   jax.ShapeDtypeStruct((S5_N, LANES), BF16), jax.ShapeDtypeStruct((S5_N, LANES), BF16)],
        compiler_params=pltpu.CompilerParams(vmem_limit_bytes=VMEM_LIMIT))(lr, li, ldt, brc, bic, crc, cic)


def _s5_params_bwd(lr, li, ldt, brc, bic, dare, daim, dbre, dbim):
    def body(lr_ref, li_ref, ldt_ref, br_ref, bi_ref, dare_ref, daim_ref, dbre_ref, dbim_ref,
             glr_ref, gli_ref, gldt_ref, gbr_ref, gbi_ref):
        _, vjp = jax.vjp(_s5_disc, lr_ref[...], li_ref[...], ldt_ref[...], br_ref[...], bi_ref[...])
        glr, gli, gldt, gbr, gbi = vjp((dare_ref[...], daim_ref[...], dbre_ref[...], dbim_ref[...]))
        glr_ref[...] = glr
        gli_ref[...] = gli
        gldt_ref[...] = gldt
        gbr_ref[...] = gbr
        gbi_ref[...] = gbi

    vec = jax.ShapeDtypeStruct((1, S5_N), F32)
    mat = jax.ShapeDtypeStruct((LANES, S5_N), F32)
    return pl.pallas_call(
        body, name="s5_params_bwd", out_shape=[vec, vec, vec, mat, mat],
        compiler_params=pltpu.CompilerParams(vmem_limit_bytes=VMEM_LIMIT))(lr, li, ldt, brc, bic, dare, daim, dbre, dbim)


def _gelu_parts(y):
    th = jnp.tanh(GELU_C0 * (y + GELU_C1 * y * y * y))
    return 0.5 * y * (1.0 + th), th


def _state_rows(q, T):
    return pl.ds(q % S5_HALF, T, stride=S5_HALF)


def _s5_fwd(u, are, aim, bre, bim, cre, cim, d_skip, wglu, bglu, comm=()):
    L = u.shape[0]
    T = min(TS, L)
    n = L // T

    def body(u_ref, are_ref, aim_ref, bre_ref, bim_ref, cre_ref, cim_ref, d_ref, wg_ref, bg_ref,
             sre_lo, sre_hi, sim_lo, sim_hi, y_ref, o_ref, st_re, st_im):
        i = pl.program_id(0)
        sre, sim = (sre_lo, sre_hi), (sim_lo, sim_hi)

        @pl.when(i == 0)
        def _():
            st_re[...] = jnp.zeros_like(st_re)
            st_im[...] = jnp.zeros_like(st_im)

        uf = u_ref[...]
        ub = uf.astype(BF16)
        for q in range(S5_TILES):
            ct = q // 4
            uq = ub[:, ct * LANES:(ct + 1) * LANES]
            sre[q // S5_HALF][_state_rows(q, T), :] = _dot(uq, bre_ref[:, q * LANES:(q + 1) * LANES])
            sim[q // S5_HALF][_state_rows(q, T), :] = _dot(uq, bim_ref[:, q * LANES:(q + 1) * LANES])
        halves = [slice(h * S5_HALF, (h + 1) * S5_HALF) for h in range(2)]
        a_re = [are_ref[hs, :] for hs in halves]
        a_im = [aim_ref[hs, :] for hs in halves]

        def step(t, carry):
            rows = pl.ds(pl.multiple_of(t * S5_HALF, S5_HALF), S5_HALF)
            out = []
            for h in range(2):
                s_re, s_im = carry[2 * h], carry[2 * h + 1]
                n_re = a_re[h] * s_re - a_im[h] * s_im + sre[h][rows, :]
                n_im = a_re[h] * s_im + a_im[h] * s_re + sim[h][rows, :]
                sre[h][rows, :] = n_re
                sim[h][rows, :] = n_im
                out += [n_re, n_im]
            return tuple(out)

        init = (st_re[halves[0], :], st_im[halves[0], :], st_re[halves[1], :], st_im[halves[1], :])
        fin = lax.fori_loop(0, T, step, init, unroll=4)
        for h in range(2):
            st_re[halves[h], :] = fin[2 * h]
            st_im[halves[h], :] = fin[2 * h + 1]
        tiles = []
        for ct in range(4):
            acc = jnp.zeros((T, LANES), F32)
            for q in range(4 * ct, 4 * ct + 4):
                win = slice(q * LANES, (q + 1) * LANES)
                acc = acc + _dot(sre[q // S5_HALF][_state_rows(q, T), :].astype(BF16), cre_ref[win, :])
                acc = acc - _dot(sim[q // S5_HALF][_state_rows(q, T), :].astype(BF16), cim_ref[win, :])
            tiles.append(acc)
        y = jnp.concatenate(tiles, axis=1) + d_ref[...] * uf
        y_ref[...] = y
        yg, _ = _gelu_parts(y)
        gate = _sigmoid(_dot(yg.astype(BF16), wg_ref[...]) + bg_ref[...])
        o_ref[...] = (yg * gate).astype(BF16)

    const = lambda shape: pl.BlockSpec(shape, lambda i: (0, 0))
    sspec = pl.BlockSpec((T * S5_HALF, LANES), lambda i: (i, 0))
    sshape = jax.ShapeDtypeStruct((L * S5_HALF, LANES), F32)
    chunk = pl.BlockSpec((T, S5_W), lambda i: (i, 0))
    return _pallas(body, [u, are, aim, bre, bim, cre, cim, d_skip, wglu, bglu], name="s5_fwd", grid=(n,),
                   in_specs=[chunk, const((S5_TILES, LANES)), const((S5_TILES, LANES)),
                             const((LANES, S5_N)), const((LANES, S5_N)), const((S5_N, LANES)), const((S5_N, LANES)),
                             const((1, S5_W)), const((S5_W, S5_W)), const((1, S5_W))],
                   out_specs=[sspec] * 4 + [chunk, chunk],
                   out_shape=[sshape] * 4 + [jax.ShapeDtypeStruct((L, S5_W), F32), jax.ShapeDtypeStruct((L, S5_W), BF16)],
                   scratch_shapes=[pltpu.VMEM((S5_TILES, LANES), F32), pltpu.VMEM((S5_TILES, LANES), F32)], comm=comm)


def _s5_bwd(dm, y_pre, u, states, are, aim, bre, bim, cre, cim, d_skip, wglu, bglu, comm=()):
    L = u.shape[0]
    T = min(TS, L)
    n = L // T

    def body(dm_ref, y_ref, u_ref, sre_lo, sre_hi, sim_lo, sim_hi, pre_lo, pre_hi, pim_lo, pim_hi,
             are_ref, aim_ref, bre_ref, bim_ref, cre_ref, cim_ref, d_ref, wg_ref, bg_ref,
             du_ref, dwg_ref, dbg_ref, dd_ref, dcre_ref, dcim_ref, dbre_ref, dbim_ref, dare_ref, daim_ref,
             gre_lo, gre_hi, gim_lo, gim_hi, car_re, car_im):
        i = pl.program_id(0)
        first = i == 0
        sre, sim = (sre_lo, sre_hi), (sim_lo, sim_hi)
        gre, gim = (gre_lo, gre_hi), (gim_lo, gim_hi)
        pre, pim = (pre_lo, pre_hi), (pim_lo, pim_hi)

        @pl.when(first)
        def _():
            car_re[...] = jnp.zeros_like(car_re)
            car_im[...] = jnp.zeros_like(car_im)
            dcre_ref[...] = jnp.zeros_like(dcre_ref)
            dcim_ref[...] = jnp.zeros_like(dcim_ref)
            dbre_ref[...] = jnp.zeros_like(dbre_ref)
            dbim_ref[...] = jnp.zeros_like(dbim_ref)

        y = y_ref[...]
        uf = u_ref[...]
        yg, th = _gelu_parts(y)
        dgelu = 0.5 * (1.0 + th) + 0.5 * y * (1.0 - th * th) * GELU_C0 * (1.0 + 3.0 * GELU_C1 * y * y)
        ygb = yg.astype(BF16)
        sg = _sigmoid(_dot(ygb, wg_ref[...]) + bg_ref[...])
        dout = dm_ref[...].astype(F32)
        dgp = dout * yg * sg * (1.0 - sg)
        dgpb = dgp.astype(BF16)
        dyg = dout * sg + _dot_nt(dgpb, wg_ref[...])
        _accumulate(dwg_ref, first, _dot_tn(ygb, dgpb))
        _accumulate(dbg_ref, first, _colsum(dgp))
        dy = dyg * dgelu
        _accumulate(dd_ref, first, _colsum(dy * uf))
        dyb = dy.astype(BF16)
        ub = uf.astype(BF16)

        for q in range(S5_TILES):
            ct, h = q // 4, q // S5_HALF
            win = slice(q * LANES, (q + 1) * LANES)
            dyq = dyb[:, ct * LANES:(ct + 1) * LANES]
            dcre_ref[win, :] += _dot_tn(sre[h][_state_rows(q, T), :].astype(BF16), dyq)
            dcim_ref[win, :] -= _dot_tn(sim[h][_state_rows(q, T), :].astype(BF16), dyq)
            gre[h][_state_rows(q, T), :] = _dot_nt(dyq, cre_ref[win, :])
            gim[h][_state_rows(q, T), :] = -_dot_nt(dyq, cim_ref[win, :])

        halves = [slice(h * S5_HALF, (h + 1) * S5_HALF) for h in range(2)]
        a_re = [are_ref[hs, :] for hs in halves]
        a_im = [aim_ref[hs, :] for hs in halves]

        def adjoint(t, h, g_re, g_im):
            rows = pl.ds(pl.multiple_of(t * S5_HALF, S5_HALF), S5_HALF)
            n_re = gre[h][rows, :] + a_re[h] * g_re + a_im[h] * g_im
            n_im = gim[h][rows, :] + a_re[h] * g_im - a_im[h] * g_re
            gre[h][rows, :] = n_re
            gim[h][rows, :] = n_im
            return n_re, n_im

        def step(k, carry):
            t = T - 1 - k
            prev = pl.ds(pl.multiple_of((t - 1) * S5_HALF, S5_HALF), S5_HALF)
            out = []
            for h in range(2):
                g_re, g_im, da_re, da_im = carry[4 * h:4 * h + 4]
                g_re, g_im = adjoint(t, h, g_re, g_im)
                p_re, p_im = sre[h][prev, :], sim[h][prev, :]
                out += [g_re, g_im, da_re + g_re * p_re + g_im * p_im, da_im + g_im * p_re - g_re * p_im]
            return tuple(out)

        zero = jnp.zeros((S5_HALF, LANES), F32)
        init = (car_re[halves[0], :], car_im[halves[0], :], zero, zero, car_re[halves[1], :], car_im[halves[1], :], zero, zero)
        fin = lax.fori_loop(0, T - 1, step, init, unroll=4)
        keep = (i < n - 1).astype(F32)
        for h in range(2):
            g_re, g_im, da_re, da_im = fin[4 * h:4 * h + 4]
            g_re, g_im = adjoint(0, h, g_re, g_im)
            p_re, p_im = pre[h][...] * keep, pim[h][...] * keep
            car_re[halves[h], :] = g_re
            car_im[halves[h], :] = g_im
            da_re = da_re + g_re * p_re + g_im * p_im
            da_im = da_im + g_im * p_re - g_re * p_im

            @pl.when(first)
            def _():
                dare_ref[halves[h], :] = da_re
                daim_ref[halves[h], :] = da_im

            @pl.when(jnp.logical_not(first))
            def _():
                dare_ref[halves[h], :] += da_re
                daim_ref[halves[h], :] += da_im

        tiles = []
        for ct in range(4):
            uq = ub[:, ct * LANES:(ct + 1) * LANES]
            acc = d_ref[:, ct * LANES:(ct + 1) * LANES] * dy[:, ct * LANES:(ct + 1) * LANES]
            for q in range(4 * ct, 4 * ct + 4):
                h = q // S5_HALF
                win = slice(q * LANES, (q + 1) * LANES)
                gq_re = gre[h][_state_rows(q, T), :].astype(BF16)
                gq_im = gim[h][_state_rows(q, T), :].astype(BF16)
                acc = acc + _dot_nt(gq_re, bre_ref[:, win]) + _dot_nt(gq_im, bim_ref[:, win])
                dbre_ref[:, win] += _dot_tn(uq, gq_re)
                dbim_ref[:, win] += _dot_tn(uq, gq_im)
            tiles.append(acc)
        du_ref[...] = jnp.concatenate(tiles, axis=1).astype(BF16)

    rev = lambda i: (n - 1 - i, 0)
    const = lambda shape: pl.BlockSpec(shape, lambda i: (0, 0))
    chunk = pl.BlockSpec((T, S5_W), rev)
    sspec = pl.BlockSpec((T * S5_HALF, LANES), rev)
    pspec = pl.BlockSpec((S5_HALF, LANES), lambda i: (jnp.maximum((n - 1 - i) * T - 1, 0), 0))
    tile = jax.ShapeDtypeStruct((S5_TILES, LANES), F32)
    vec = jax.ShapeDtypeStruct((1, S5_W), F32)
    sbuf = pltpu.VMEM((T * S5_HALF, LANES), F32)
    return _pallas(
        body, [dm, y_pre, u, *states, *states, are, aim, bre, bim, cre, cim, d_skip, wglu, bglu],
        name="s5_bwd", grid=(n,),
        in_specs=[chunk, chunk, chunk] + [sspec] * 4 + [pspec] * 4 + [
            const((S5_TILES, LANES)), const((S5_TILES, LANES)), const((LANES, S5_N)), const((LANES, S5_N)),
            const((S5_N, LANES)), const((S5_N, LANES)), const((1, S5_W)), const((S5_W, S5_W)), const((1, S5_W))],
        out_specs=[chunk, const((S5_W, S5_W)), const((1, S5_W)), const((1, S5_W)), const((S5_N, LANES)),
                   const((S5_N, LANES)), const((LANES, S5_N)), const((LANES, S5_N)), const((S5_TILES, LANES)),
                   const((S5_TILES, LANES))],
        out_shape=[jax.ShapeDtypeStruct((L, S5_W), BF16), jax.ShapeDtypeStruct((S5_W, S5_W), F32), vec, vec,
                   jax.ShapeDtypeStruct((S5_N, LANES), F32), jax.ShapeDtypeStruct((S5_N, LANES), F32),
                   jax.ShapeDtypeStruct((LANES, S5_N), F32), jax.ShapeDtypeStruct((LANES, S5_N), F32), tile, tile],
        scratch_shapes=[sbuf, sbuf, sbuf, sbuf, pltpu.VMEM((S5_TILES, LANES), F32), pltpu.VMEM((S5_TILES, LANES), F32)],
        comm=comm)


_EYE8 = np.eye(8, dtype=np.float32)


def _compact_b(b):
    return jnp.einsum("akpc,kj->jcakp", b.reshape(4, 8, S5_P, S5_GC), _EYE8).reshape(LANES, S5_N)


def _uncompact_b(m):
    return jnp.einsum("kcakp->akpc", m.reshape(8, S5_GC, 4, 8, S5_P)).reshape(S5_G, S5_P, S5_GC)


_HEAD_MEAN = np.kron(np.eye(CONV_W // CONV_HD, dtype=np.float32), np.full((CONV_HD, CONV_HD), 1.0 / CONV_HD, np.float32))


def _head_mean(v, m):
    hi = v.astype(BF16)
    lo = (v - hi.astype(F32)).astype(BF16)
    return _dot(hi, m) + _dot(lo, m)


def _head_norm(zc, m):
    d = zc - _head_mean(zc, m)
    rstd = lax.rsqrt(_head_mean(d * d, m) + EPS)
    return d * rstd, rstd


def _conv_fwd(u, o_s5, wdw, bdw, lng, lnb, mavg, comm=()):
    L = u.shape[0]
    T = min(TS, L)
    n = L // T
    first_tap = HALO - (CONV_K - 1)

    def body(v1_ref, v2_ref, s5_ref, w_ref, b_ref, g_ref, be_ref, m_ref, zc_ref, o_ref, zbuf):
        i = pl.program_id(0)

        @pl.when(i == 0)
        def _():
            zbuf[0:HALO, :] = jnp.zeros((HALO, CONV_W), F32)

        zbuf[HALO:HALO + T, :] = v1_ref[...] * _sigmoid(v2_ref[...])
        for r0 in range(0, T, CONV_SB):
            acc = jnp.broadcast_to(b_ref[...], (CONV_SB, CONV_W))
            for k in range(CONV_K):
                acc = acc + w_ref[k:k + 1, :] * zbuf[pl.ds(r0 + first_tap + k, CONV_SB), :]
            zc_ref[r0:r0 + CONV_SB, :] = acc
        zbuf[0:HALO, :] = zbuf[T:T + HALO, :]
        zn, _ = _head_norm(zc_ref[...], m_ref[...])
        zz = zn * g_ref[...] + be_ref[...]
        o_ref[:, 0:S5_W] = s5_ref[...]
        o_ref[:, S5_W:S5_W + CONV_W] = (zz * _sigmoid(zz)).astype(BF16)

    const = lambda shape: pl.BlockSpec(shape, lambda i: (0, 0))
    vec = const((1, CONV_W))
    return _pallas(body, [u, u, o_s5, wdw, bdw, lng, lnb, mavg], name="conv_fwd", grid=(n,),
                   in_specs=[pl.BlockSpec((T, CONV_W), lambda i: (i, 1)), pl.BlockSpec((T, CONV_W), lambda i: (i, 2)),
                             pl.BlockSpec((T, S5_W), lambda i: (i, 0)), const((HALO, CONV_W)), vec, vec, vec,
                             const((CONV_W, CONV_W))],
                   out_specs=[pl.BlockSpec((T, CONV_W), lambda i: (i, 0)),
                              pl.BlockSpec((T, S5_W + CONV_W), lambda i: (i, 0))],
                   out_shape=[jax.ShapeDtypeStruct((L, CONV_W), F32), jax.ShapeDtypeStruct((L, S5_W + CONV_W), BF16)],
                   scratch_shapes=[pltpu.VMEM((T + HALO, CONV_W), F32)], comm=comm)


def _conv_bwd(dm, zc, u, du_s5, wdw, lng, lnb, mavg, comm=()):
    L = u.shape[0]
    T = min(TS, L)
    n = L // T
    hb = T // HALO
    first_tap = HALO - (CONV_K - 1)

    def body(dm_ref, zc_ref, v1_ref, v2_ref, p1_ref, p2_ref, s5_ref, w_ref, g_ref, be_ref, m_ref,
             du_ref, dw_ref, db_ref, dg_ref, dbe_ref, zbuf, dzbuf, head):
        i = pl.program_id(0)
        first = i == 0

        @pl.when(first)
        def _():
            head[...] = jnp.zeros_like(head)
            dw_ref[...] = jnp.zeros_like(dw_ref)

        zn, rstd = _head_norm(zc_ref[...], m_ref[...])
        zz = zn * g_ref[...] + be_ref[...]
        sg = _sigmoid(zz)
        dzz = dm_ref[...].astype(F32) * sg * (1.0 + zz * (1.0 - sg))
        _accumulate(dbe_ref, first, _colsum(dzz))
        _accumulate(dg_ref, first, _colsum(dzz * zn))
        dzn = dzz * g_ref[...]
        dzc = rstd * (dzn - _head_mean(dzn, m_ref[...]) - zn * _head_mean(dzn * zn, m_ref[...]))
        _accumulate(db_ref, first, _colsum(dzc))

        dzbuf[0:T, :] = dzc
        dzbuf[T:T + HALO, :] = head[...]
        head[...] = dzbuf[0:HALO, :]
        keep = (i < n - 1).astype(F32)
        zbuf[0:HALO, :] = p1_ref[...] * _sigmoid(p2_ref[...]) * keep
        zbuf[HALO:HALO + T, :] = v1_ref[...] * _sigmoid(v2_ref[...])
        du_ref[:, 0:S5_W] = s5_ref[...]

        for r0 in range(0, T, CONV_SB):
            rows = slice(r0, r0 + CONV_SB)
            dzc_b = dzbuf[rows, :]
            dz = jnp.zeros((CONV_SB, CONV_W), F32)
            for k in range(CONV_K):
                prod = dzc_b * zbuf[pl.ds(r0 + first_tap + k, CONV_SB), :]
                dw_ref[8 * k:8 * k + 8, :] += jnp.sum(prod.reshape(CONV_SB // 8, 8, CONV_W), axis=0)
                dz = dz + w_ref[k:k + 1, :] * dzbuf[pl.ds(r0 + CONV_K - 1 - k, CONV_SB), :]
            v1 = v1_ref[rows, :]
            sg2 = _sigmoid(v2_ref[rows, :])
            du_ref[rows, S5_W:S5_W + CONV_W] = (dz * sg2).astype(BF16)
            du_ref[rows, S5_W + CONV_W:S5_W + 2 * CONV_W] = (dz * v1 * sg2 * (1.0 - sg2)).astype(BF16)

    rev = lambda c: (lambda i: (n - 1 - i, c))
    prev = lambda c: (lambda i: (jnp.maximum((n - 1 - i) * hb - 1, 0), c))
    const = lambda shape: pl.BlockSpec(shape, lambda i: (0, 0))
    vec = const((1, CONV_W))
    vshape = jax.ShapeDtypeStruct((1, CONV_W), F32)
    return _pallas(
        body, [dm, zc, u, u, u, u, du_s5, wdw, lng, lnb, mavg], name="conv_bwd", grid=(n,),
        in_specs=[pl.BlockSpec((T, CONV_W), rev(1)), pl.BlockSpec((T, CONV_W), rev(0)),
                  pl.BlockSpec((T, CONV_W), rev(1)), pl.BlockSpec((T, CONV_W), rev(2)),
                  pl.BlockSpec((HALO, CONV_W), prev(1)), pl.BlockSpec((HALO, CONV_W), prev(2)),
                  pl.BlockSpec((T, S5_W), rev(0)), const((HALO, CONV_W)), vec, vec, const((CONV_W, CONV_W))],
        out_specs=[pl.BlockSpec((T, S5_W + 2 * CONV_W), rev(0)), const((8 * HALO, CONV_W)), vec, vec, vec],
        out_shape=[jax.ShapeDtypeStruct((L, S5_W + 2 * CONV_W), BF16), jax.ShapeDtypeStruct((8 * HALO, CONV_W), F32),
                   vshape, vshape, vshape],
        scratch_shapes=[pltpu.VMEM((T + HALO, CONV_W), F32), pltpu.VMEM((T + HALO, CONV_W), F32),
                        pltpu.VMEM((HALO, CONV_W), F32)], comm=comm)


def _gather_all(v):
    rows, cols = v.shape

    def body(x_ref, out_ref, send_sems, recv_sems, local_sem):
        x, y, c, chips = _position()
        me, sibling = (x, y, c), (x, y, 1 - c)

        def block(px, py, pc):
            return out_ref.at[pl.ds((4 * px + 2 * py + pc) * rows, rows), :]

        def copy(k, blk, to, src=None):
            return pltpu.make_async_remote_copy(
                src_ref=block(*blk) if src is None else src, dst_ref=block(*blk), send_sem=send_sems.at[k],
                recv_sem=recv_sems.at[k], device_id=to, device_id_type=MESH)

        mine = pltpu.make_async_copy(x_ref, block(*me), local_sem)
        mine.start()
        first = [copy(0, me, sibling, src=x_ref)]
        first += [copy(1 + j, me, (*chip, c), src=x_ref) for j, chip in enumerate(chips)]
        for cp in first:
            cp.start()
        passed = [copy(4 + j, (*chip, c), sibling) for j, chip in enumerate(chips)]
        for j, chip in enumerate(chips):
            copy(1 + j, (*chip, c), me).wait_recv()
            passed[j].start()
        copy(0, sibling, me).wait_recv()
        for j, chip in enumerate(chips):
            copy(4 + j, (*chip, 1 - c), me).wait_recv()
        for cp in first + passed:
            cp.wait_send()
        mine.wait()

    return pl.pallas_call(
        body, name="gather_small",
        in_specs=[pl.BlockSpec(memory_space=pltpu.VMEM)], out_specs=pl.BlockSpec(memory_space=pltpu.VMEM),
        out_shape=jax.ShapeDtypeStruct((N_DEV * rows, cols), v.dtype),
        scratch_shapes=[pltpu.SemaphoreType.DMA((7,)), pltpu.SemaphoreType.DMA((7,)), pltpu.SemaphoreType.DMA],
        compiler_params=pltpu.CompilerParams(vmem_limit_bytes=VMEM_LIMIT))(v)


def _adamw(w, g, m, v):
    m = ADAM_B1 * m + (1.0 - ADAM_B1) * g
    v = ADAM_B2 * v + (1.0 - ADAM_B2) * jnp.square(g)
    m_hat = m / (1.0 - ADAM_B1 ** ADAM_STEP)
    v_hat = v / (1.0 - ADAM_B2 ** ADAM_STEP)
    return -ADAM_LR * (m_hat / (jnp.sqrt(v_hat) + ADAM_EPS) + ADAM_WD * w), m, v


def _sum_slots(recv, name, comm=()):
    _, rows, cols = recv.shape
    tr = _row_tile(rows, cols)

    def body(r_ref, o_ref):
        acc = r_ref[0].astype(F32)
        for s in range(1, N_CHIPS):
            acc = acc + r_ref[s].astype(F32)
        o_ref[...] = acc

    return _pallas(body, [recv], name=name, grid=(rows // tr,),
                   in_specs=[pl.BlockSpec((N_CHIPS, tr, cols), lambda i: (0, i, 0))],
                   out_specs=[pl.BlockSpec((tr, cols), lambda i: (i, 0))],
                   out_shape=[jax.ShapeDtypeStruct((rows, cols), F32)], comm=comm)[0]


def _adamw_sharded(w, part, other, m, v, name, comm=()):
    rows, cols = w.shape
    tr = _row_tile(rows, cols)

    def body(w_ref, p_ref, q_ref, m_ref, v_ref, g_ref, d_ref, nm_ref, nv_ref):
        g = p_ref[...] + q_ref[...]
        g_ref[...] = g
        d_ref[...], nm_ref[...], nv_ref[...] = _adamw(w_ref[...], g, m_ref[...], v_ref[...])

    spec = pl.BlockSpec((tr, cols), lambda i: (i, 0))
    shape = jax.ShapeDtypeStruct((rows, cols), F32)
    return _pallas(body, [w, part, other, m, v], name=name, grid=(rows // tr,), in_specs=[spec] * 5,
                   out_specs=[spec] * 4, out_shape=[shape] * 4, comm=comm)


def _adamw_small(w, gathered, m, v):
    rows, cols = w.shape

    def body(w_ref, a_ref, m_ref, v_ref, g_ref, d_ref, nm_ref, nv_ref):
        g = a_ref[0:rows, :]
        for dev in range(1, N_DEV):
            g = g + a_ref[dev * rows:(dev + 1) * rows, :]
        g_ref[...] = g
        d_ref[...], nm_ref[...], nv_ref[...] = _adamw(w_ref[...], g, m_ref[...], v_ref[...])

    shape = jax.ShapeDtypeStruct((rows, cols), F32)
    return pl.pallas_call(
        body, name="adamw_small", out_shape=[shape] * 4,
        compiler_params=pltpu.CompilerParams(vmem_limit_bytes=VMEM_LIMIT))(w, gathered, m, v)


def _pack_small(vals, last_row):
    rows = []
    for name in SMALL:
        flat = vals[name].reshape(-1).astype(F32)
        rows.append(jnp.pad(flat, (0, -flat.size % LANES)).reshape(-1, LANES))
    packed = jnp.concatenate(rows + [last_row], axis=0)
    return jnp.pad(packed, ((0, -packed.shape[0] % 8), (0, 0)))


def _unpack_small(packed, like):
    out, r = {}, 0
    for name in SMALL:
        size = like[name].size
        nrows = -(-size // LANES)
        out[name] = packed[r:r + nrows].reshape(-1)[:size].reshape(like[name].shape)
        r += nrows
    return out, packed[r, 0]


def _shard2d(name, v):
    v = v.reshape(v.shape[-2:])
    return v.T if name in FFN_T else v


def _unshard(name, v, shape):
    return (v.T if name in FFN_T else v).reshape(shape)


def _train_step(x3, tgt3, wts, ms, vs):
    x, tgt = x3[0], tgt3[0]
    L, D = x.shape
    row = lambda v: v.reshape(1, -1)
    shards = {k: _shard2d(k, wts[k]) for k in SHARDED}
    sends = {k: shards[k] if k == "conv_w_dw" else _cast_bf16(shards[k], "cast_" + k) for k in SHARDED}
    gat = {k: _Gather(sends[k]) for k in SHARDED}
    w = lambda k: gat[k].result[0]

    s = {k: wts[k] for k in SMALL}
    lr, li = s["s5_lam_re"].reshape(1, S5_N), s["s5_lam_im"].reshape(1, S5_N)
    ldt = jnp.repeat(s["s5_log_dt"].reshape(S5_G), S5_P).reshape(1, S5_N)
    brc, bic = _compact_b(s["s5_b_re"].reshape(S5_G, S5_P, S5_GC)), _compact_b(s["s5_b_im"].reshape(S5_G, S5_P, S5_GC))
    crc = _compact_b(s["s5_c_re"].reshape(S5_G, S5_GC, S5_P).transpose(0, 2, 1)).T
    cic = _compact_b(s["s5_c_im"].reshape(S5_G, S5_GC, S5_P).transpose(0, 2, 1)).T
    d_skip, b_glu = row(s["s5_d"]), row(s["s5_b_glu"])
    b_dw, ln_g, ln_b = row(s["conv_b_dw"]), row(s["conv_ln_g"]), row(s["conv_ln_b"])
    g1, gm, g2, gf = row(s["ffn1_norm"]), row(s["mix_norm"]), row(s["ffn2_norm"]), row(s["final_norm"])
    mavg = jnp.asarray(_HEAD_MEAN, dtype=BF16)

    h1 = _rms_fwd(x, g1, "rms1", comm=[gat["ffn1_w_gate"], gat["ffn1_w_up"]])
    a1, b1, act1 = _ffn_up(h1, w("ffn1_w_gate"), w("ffn1_w_up"), "ffn1_up", comm=[gat["ffn1_w_down"]])
    x1, h2 = _ffn_down(act1, w("ffn1_w_down"), x, gm, "ffn1_down",
                       comm=[gat["w_in"], gat["s5_w_glu"], gat["conv_w_dw"], gat["w_out"]])
    u = _mm_grouped(h2, w("w_in"), "in_proj")
    are, aim, bre, bim, cre, cim = _s5_params_fwd(lr, li, ldt, brc, bic, crc, cic)
    are_t, aim_t = are.reshape(S5_TILES, LANES), aim.reshape(S5_TILES, LANES)
    w_glu = w("s5_w_glu").reshape(S5_W, S5_W)
    *states, y_pre, o_s5 = _s5_fwd(u, are_t, aim_t, bre, bim, cre, cim, d_skip, w_glu, b_glu,
                                   comm=[gat["ffn2_w_gate"], gat["ffn2_w_up"]])
    w_dw = w("conv_w_dw").transpose(1, 0, 2).reshape(CONV_K, CONV_W)
    w_dw = jnp.concatenate([w_dw, jnp.zeros((HALO - CONV_K, CONV_W), F32)], axis=0)
    zc, cat = _conv_fwd(u, o_s5, w_dw, b_dw, ln_g, ln_b, mavg, comm=[gat["ffn2_w_down"]])
    w_out = w("w_out").reshape(-1, D)
    x2, h3 = _mix_out(cat, w_out, x1, g2, "mix_out")
    a2, b2, act2 = _ffn_up(h3, w("ffn2_w_gate"), w("ffn2_w_up"), "ffn2_up")
    dx3, dx3b, loss_part, d_gf = _ffn_down_loss(act2, w("ffn2_w_down"), x2, gf, tgt, "ffn2_down_loss")

    gs, sc = {"final_norm": d_gf}, {}
    da2, db2 = _ffn_bwd_act(dx3b, w("ffn2_w_down"), a2, b2, "ffn2_bwd_act")
    sc["ffn2_w_down"] = _Scatter(_mm_tn(act2, dx3b[None], 0.5, "ffn2_dwd", N_CHIPS))
    sc["ffn2_w_gate"] = _Scatter(_mm_tn(da2, h3[None], 1.0, "ffn2_dwg", N_CHIPS, comm=[sc["ffn2_w_down"]]))
    sc["ffn2_w_up"] = _Scatter(_mm_tn(db2, h3[None], 1.0, "ffn2_dwu", N_CHIPS, comm=[sc["ffn2_w_gate"]]))
    dx2, dx2b, gs["ffn2_norm"] = _mm_rmsbwd([da2, db2], [w("ffn2_w_gate"), w("ffn2_w_up")], False, False, x2, g2, dx3,
                                            "ffn2_bwd_dx", comm=[sc["ffn2_w_up"]])

    dm = _mm_nt(dx2b, w_out, "mix_bwd")
    sc["w_out"] = _Scatter(_mm_tn(cat[None], dx2b[None], 1.0, "dwout", 1).reshape(N_CHIPS, -1, D))
    (du_s5, d_wglu, gs["s5_b_glu"], gs["s5_d"], d_crc, d_cic, d_bre, d_bim, d_are, d_aim) = _s5_bwd(
        dm, y_pre, u, states, are_t, aim_t, bre, bim, cre, cim, d_skip, w_glu, b_glu, comm=[sc["w_out"]])
    sc["s5_w_glu"] = _Scatter(d_wglu.astype(BF16).reshape(N_CHIPS, -1, S5_W))
    g_lr, g_li, g_ldt, g_brc, g_bic = _s5_params_bwd(lr, li, ldt, brc, bic, d_are.reshape(1, S5_N),
                                                     d_aim.reshape(1, S5_N), d_bre, d_bim)
    gs["s5_lam_re"], gs["s5_lam_im"] = g_lr, g_li
    gs["s5_log_dt"] = jnp.sum(g_ldt.reshape(S5_G, S5_P), axis=1)
    gs["s5_b_re"], gs["s5_b_im"] = _uncompact_b(g_brc), _uncompact_b(g_bic)
    gs["s5_c_re"] = _uncompact_b(d_crc.T).transpose(0, 2, 1)
    gs["s5_c_im"] = _uncompact_b(d_cic.T).transpose(0, 2, 1)
    du, d_wdw, gs["conv_b_dw"], gs["conv_ln_g"], gs["conv_ln_b"] = _conv_bwd(
        dm, zc, u, du_s5, w_dw, ln_g, ln_b, mavg, comm=[sc["s5_w_glu"]])
    d_wdw = jnp.sum(d_wdw.reshape(HALO, 8, CONV_W), axis=1)[:CONV_K]
    sc["conv_w_dw"] = _Scatter(d_wdw.reshape(CONV_K, N_CHIPS, -1).transpose(1, 0, 2))
    sc["w_in"] = _Scatter(_mm_tn(h2[None], du, 1.0, "dwin", N_CHIPS, b_cols=True, comm=[sc["conv_w_dw"]]))
    dx1, dx1b, gs["mix_norm"] = _mm_rmsbwd([du], [w("w_in")], True, True, x1, gm, dx2, "in_proj_bwd", comm=[sc["w_in"]])

    da1, db1 = _ffn_bwd_act(dx1b, w("ffn1_w_down"), a1, b1, "ffn1_bwd_act")
    sc["ffn1_w_down"] = _Scatter(_mm_tn(act1, dx1b[None], 0.5, "ffn1_dwd", N_CHIPS))
    sc["ffn1_w_gate"] = _Scatter(_mm_tn(da1, h1[None], 1.0, "ffn1_dwg", N_CHIPS, comm=[sc["ffn1_w_down"]]))
    sc["ffn1_w_up"] = _Scatter(_mm_tn(db1, h1[None], 1.0, "ffn1_dwu", N_CHIPS, comm=[sc["ffn1_w_gate"]]))
    grad_x, _, gs["ffn1_norm"] = _mm_rmsbwd([da1, db1], [w("ffn1_w_gate"), w("ffn1_w_up")], False, False, x, g1, dx1,
                                            "ffn1_bwd_dx", comm=[sc["ffn1_w_up"]])

    out = {}
    gsmall = {k: gs[k].reshape(wts[k].shape) for k in SMALL}
    zero_row = jnp.zeros((1, LANES), F32)
    g_all = _gather_all(_pack_small(gsmall, loss_part))
    res = _adamw_small(_pack_small(s, zero_row), g_all, _pack_small({k: ms[k] for k in SMALL}, zero_row),
                       _pack_small({k: vs[k] for k in SMALL}, zero_row))
    unpacked = [_unpack_small(r, s) for r in res]
    loss = unpacked[0][1]
    for k in SMALL:
        out[k] = [u_[0][k] for u_ in unpacked]

    order = ("ffn2_w_down", "ffn2_w_gate", "ffn2_w_up", "w_out", "s5_w_glu", "conv_w_dw", "w_in", "ffn1_w_down",
             "ffn1_w_gate", "ffn1_w_up")
    swaps, riding = {}, []
    for k in order:
        swaps[k] = _Swap(_sum_slots(sc[k].result[0], "sum_" + k, comm=riding))
        riding = [swaps[k]]
    for k in order:
        res = _adamw_sharded(shards[k], swaps[k].ins[0], swaps[k].result[0], _shard2d(k, ms[k]), _shard2d(k, vs[k]),
                             "adamw_" + k, comm=riding)
        riding = []
        out[k] = [_unshard(k, r, wts[k].shape) for r in res]
    return loss, grad_x[None], out


def kernel(x, ffn1_norm, ffn1_w_gate, ffn1_w_up, ffn1_w_down, mix_norm, w_in, s5_lam_re, s5_lam_im, s5_log_dt, s5_b_re, s5_b_im, s5_c_re, s5_c_im, s5_d, s5_w_glu, s5_b_glu, conv_w_dw, conv_b_dw, conv_ln_g, conv_ln_b, w_out, ffn2_norm, ffn2_w_gate, ffn2_w_up, ffn2_w_down, final_norm, loss_target, m_ffn1_norm, m_ffn1_w_gate, m_ffn1_w_up, m_ffn1_w_down, m_mix_norm, m_w_in, m_s5_lam_re, m_s5_lam_im, m_s5_log_dt, m_s5_b_re, m_s5_b_im, m_s5_c_re, m_s5_c_im, m_s5_d, m_s5_w_glu, m_s5_b_glu, m_conv_w_dw, m_conv_b_dw, m_conv_ln_g, m_conv_ln_b, m_w_out, m_ffn2_norm, m_ffn2_w_gate, m_ffn2_w_up, m_ffn2_w_down, m_final_norm, v_ffn1_norm, v_ffn1_w_gate, v_ffn1_w_up, v_ffn1_w_down, v_mix_norm, v_w_in, v_s5_lam_re, v_s5_lam_im, v_s5_log_dt, v_s5_b_re, v_s5_b_im, v_s5_c_re, v_s5_c_im, v_s5_d, v_s5_w_glu, v_s5_b_glu, v_conv_w_dw, v_conv_b_dw, v_conv_ln_g, v_conv_ln_b, v_w_out, v_ffn2_norm, v_ffn2_w_gate, v_ffn2_w_up, v_ffn2_w_down, v_final_norm):
    given = dict(locals())
    wts = {k: given[k] for k in WEIGHTS}
    ms = {k: given["m_" + k] for k in WEIGHTS}
    vs = {k: given["v_" + k] for k in WEIGHTS}
    loss, grad_x, out = _train_step(x, loss_target, wts, ms, vs)
    return (loss, grad_x, *[out[k][0] for k in WEIGHTS], *[out[k][1] for k in WEIGHTS],
            *[out[k][2] for k in WEIGHTS], *[out[k][3] for k in WEIGHTS])
```

```python
import functools

import jax
import jax.numpy as jnp
import numpy as np
from jax import lax
from jax.experimental import pallas as pl
from jax.experimental.pallas import tpu as pltpu

F32, BF16 = jnp.float32, jnp.bfloat16
MESH = pl.DeviceIdType.MESH

EPS = 1e-6
ADAM_LR, ADAM_B1, ADAM_B2, ADAM_EPS, ADAM_WD, ADAM_STEP = 0.001, 0.9, 0.999, 1e-08, 0.01, 10

N_CHIPS = 4
N_DEV = 8
LANES = 128
BF16_ROWS = 16
S5_W, S5_G, S5_GC, S5_P = 512, 32, 16, 64
S5_N = S5_G * S5_P
S5_TILES = S5_N // LANES
S5_HALF = 8
CONV_W, CONV_K, CONV_HD = 512, 31, 64
HALO = 32
CONV_SB = 32
TM = 512
TK = 2048
TS = 256
VMEM_LIMIT = 48 << 20
GELU_C0, GELU_C1 = 0.7978845608028654, 0.044715

FFN_T = ("ffn1_w_gate", "ffn1_w_up", "ffn2_w_gate", "ffn2_w_up")
SHARDED = ("ffn1_w_gate", "ffn1_w_up", "ffn1_w_down", "w_in", "s5_w_glu", "conv_w_dw", "w_out",
           "ffn2_w_gate", "ffn2_w_up", "ffn2_w_down")
SMALL = ("ffn1_norm", "mix_norm", "s5_lam_re", "s5_lam_im", "s5_log_dt", "s5_b_re", "s5_b_im", "s5_c_re",
         "s5_c_im", "s5_d", "s5_b_glu", "conv_b_dw", "conv_ln_g", "conv_ln_b", "ffn2_norm", "final_norm")
WEIGHTS = ("ffn1_norm", "ffn1_w_gate", "ffn1_w_up", "ffn1_w_down", "mix_norm", "w_in", "s5_lam_re", "s5_lam_im",
           "s5_log_dt", "s5_b_re", "s5_b_im", "s5_c_re", "s5_c_im", "s5_d", "s5_w_glu", "s5_b_glu", "conv_w_dw",
           "conv_b_dw", "conv_ln_g", "conv_ln_b", "w_out", "ffn2_norm", "ffn2_w_gate", "ffn2_w_up", "ffn2_w_down",
           "final_norm")


def _dot(a, b):
    return jnp.dot(a, b, preferred_element_type=F32)


def _dot_nt(a, b):
    return lax.dot_general(a, b, (((1,), (1,)), ((), ())), preferred_element_type=F32)


def _dot_tn(a, b):
    return lax.dot_general(a, b, (((0,), (0,)), ((), ())), preferred_element_type=F32)


def _colsum(v):
    return jnp.sum(v, axis=0, keepdims=True)


def _sigmoid(v):
    return 1.0 / (1.0 + jnp.exp(-v))


def _accumulate(ref, first, value):
    @pl.when(first)
    def _():
        ref[...] = value

    @pl.when(jnp.logical_not(first))
    def _():
        ref[...] += value


def _position():
    x, y, c = lax.axis_index("x"), lax.axis_index("y"), lax.axis_index("c")
    return x, y, c, [(1 - x, y), (x, 1 - y), (1 - x, 1 - y)]


def _remote(src, dst, sems, send, recv, device):
    return pltpu.make_async_remote_copy(src_ref=src, dst_ref=dst, send_sem=sems.at[send], recv_sem=sems.at[recv],
                                        device_id=device, device_id_type=MESH)


class _Gather:
    def __init__(self, shard):
        self.ins = [shard]
        self.outs = [jax.ShapeDtypeStruct((N_CHIPS,) + shard.shape, shard.dtype)]
        self.rows = shard.shape[0]
        self.halve = shard.dtype == BF16 and self.rows % (2 * BF16_ROWS) == 0
        self.n_sem = 13 if self.halve else 7
        self.result = None

    def _copies(self, ins, outs, sems, s0, pos):
        x, y, c, chips = pos
        src, dst = ins[0], outs[0]
        me = 2 * x + y
        if self.halve:
            hr = self.rows // 2
            mine, theirs = pl.ds(c * hr, hr), pl.ds((1 - c) * hr, hr)
            part = lambda slot, rows: dst.at[slot, rows]
            my_src = src.at[mine]
        else:
            mine = theirs = None
            part = lambda slot, rows: dst.at[slot]
            my_src = src
        slot = lambda j: 2 * chips[j][0] + chips[j][1]
        local = lambda: pltpu.make_async_copy(src, dst.at[me], sems.at[s0])
        send = lambda j: _remote(my_src, part(me, mine), sems, s0 + 1 + j, s0 + 4 + j, (*chips[j], c))
        land = lambda j: _remote(my_src, part(slot(j), mine), sems, s0 + 1 + j, s0 + 4 + j, (*chips[j], c))
        fwd = lambda j: _remote(part(slot(j), mine), part(slot(j), mine), sems, s0 + 7 + j, s0 + 10 + j, (x, y, 1 - c))
        got = lambda j: _remote(part(slot(j), theirs), part(slot(j), theirs), sems, s0 + 7 + j, s0 + 10 + j,
                                (x, y, 1 - c))
        return local, send, land, fwd, got

    def start(self, ins, outs, sems, s0, pos):
        local, send, _, _, _ = self._copies(ins, outs, sems, s0, pos)
        local().start()
        for j in range(N_CHIPS - 1):
            send(j).start()

    def finish(self, ins, outs, sems, s0, pos):
        local, send, land, fwd, got = self._copies(ins, outs, sems, s0, pos)
        others = range(N_CHIPS - 1)
        for j in others:
            land(j).wait_recv()
            if self.halve:
                fwd(j).start()
        for j in others:
            if self.halve:
                got(j).wait_recv()
        for j in others:
            send(j).wait_send()
            if self.halve:
                fwd(j).wait_send()
        local().wait()


class _Scatter:
    def __init__(self, grad):
        self.ins = [grad]
        self.outs = [jax.ShapeDtypeStruct(grad.shape, grad.dtype)]
        self.n_sem = 7
        self.result = None

    def _copies(self, ins, outs, sems, s0, pos):
        x, y, c, chips = pos
        src, dst = ins[0], outs[0]
        me = 2 * x + y
        slot = lambda j: 2 * chips[j][0] + chips[j][1]
        local = lambda: pltpu.make_async_copy(src.at[me], dst.at[me], sems.at[s0])
        send = lambda j: _remote(src.at[slot(j)], dst.at[me], sems, s0 + 1 + j, s0 + 4 + j, (*chips[j], c))
        land = lambda j: _remote(src.at[me], dst.at[slot(j)], sems, s0 + 1 + j, s0 + 4 + j, (*chips[j], c))
        return local, send, land

    def start(self, ins, outs, sems, s0, pos):
        local, send, _ = self._copies(ins, outs, sems, s0, pos)
        local().start()
        for j in range(N_CHIPS - 1):
            send(j).start()

    def finish(self, ins, outs, sems, s0, pos):
        local, send, land = self._copies(ins, outs, sems, s0, pos)
        for j in range(N_CHIPS - 1):
            land(j).wait_recv()
        for j in range(N_CHIPS - 1):
            send(j).wait_send()
        local().wait()


class _Swap:
    def __init__(self, part):
        self.ins = [part]
        self.outs = [jax.ShapeDtypeStruct(part.shape, part.dtype)]
        self.n_sem = 2
        self.result = None

    def _copy(self, ins, outs, sems, s0, pos):
        x, y, c, _ = pos
        return _remote(ins[0], outs[0], sems, s0, s0 + 1, (x, y, 1 - c))

    def start(self, ins, outs, sems, s0, pos):
        self._copy(ins, outs, sems, s0, pos).start()

    def finish(self, ins, outs, sems, s0, pos):
        self._copy(ins, outs, sems, s0, pos).wait()


class _SwapHalf:
    def __init__(self, grad, key):
        slots, rows, cols = grad.shape
        half = jax.ShapeDtypeStruct((slots, rows // 2, cols), grad.dtype)
        self.ins, self.outs, self.key = [grad], [half, half], key
        self.hr = rows // 2
        self.n_sem = 3
        self.result = None

    def _copies(self, ins, outs, sems, s0, pos):
        x, y, c, _ = pos
        mine, theirs = pl.ds(c * self.hr, self.hr), pl.ds((1 - c) * self.hr, self.hr)
        local = pltpu.make_async_copy(ins[0].at[:, mine], outs[0], sems.at[s0])
        remote = _remote(ins[0].at[:, theirs], outs[1], sems, s0 + 1, s0 + 2, (x, y, 1 - c))
        return local, remote

    def start(self, ins, outs, sems, s0, pos):
        for cp in self._copies(ins, outs, sems, s0, pos):
            cp.start()

    def finish(self, ins, outs, sems, s0, pos):
        for cp in self._copies(ins, outs, sems, s0, pos):
            cp.wait()


class _SwapBack:
    def __init__(self, part):
        hr, cols = part.shape
        self.ins, self.outs = [part], [jax.ShapeDtypeStruct((2 * hr, cols), part.dtype)]
        self.hr = hr
        self.n_sem = 3
        self.result = None

    def _copies(self, ins, outs, sems, s0, pos):
        x, y, c, _ = pos
        mine, theirs = pl.ds(c * self.hr, self.hr), pl.ds((1 - c) * self.hr, self.hr)
        local = lambda: pltpu.make_async_copy(ins[0], outs[0].at[mine], sems.at[s0])
        send = lambda: _remote(ins[0], outs[0].at[mine], sems, s0 + 1, s0 + 2, (x, y, 1 - c))
        land = lambda: _remote(ins[0], outs[0].at[theirs], sems, s0 + 1, s0 + 2, (x, y, 1 - c))
        return local, send, land

    def start(self, ins, outs, sems, s0, pos):
        local, send, _ = self._copies(ins, outs, sems, s0, pos)
        local().start()
        send().start()

    def finish(self, ins, outs, sems, s0, pos):
        local, send, land = self._copies(ins, outs, sems, s0, pos)
        land().wait_recv()
        send().wait_send()
        local().wait()


def _pallas(body, args, *, name, grid, in_specs, out_specs, out_shape, scratch_shapes=(), comm=()):
    comm = list(comm)
    n_in, n_out, n_scr = len(in_specs), len(out_specs), len(scratch_shapes)
    c_in = [a for op in comm for a in op.ins]
    c_out = [s for op in comm for s in op.outs]
    n_sem = sum(op.n_sem for op in comm)

    def full(*refs):
        o0 = n_in + len(c_in)
        s0 = o0 + n_out + len(c_out)
        ins, cin = refs[:n_in], refs[n_in:o0]
        outs, cout = refs[o0:o0 + n_out], refs[o0 + n_out:s0]
        scratch = refs[s0:s0 + n_scr]
        if comm:
            sems = refs[s0 + n_scr]
            ids = [pl.program_id(d) for d in range(len(grid))]
            first = functools.reduce(jnp.logical_and, [i == 0 for i in ids])
            last = functools.reduce(jnp.logical_and, [i == g - 1 for i, g in zip(ids, grid)])
            pos = _position()

            def each(step):
                ci = co = cs = 0
                for op in comm:
                    getattr(op, step)(cin[ci:ci + len(op.ins)], cout[co:co + len(op.outs)], sems, cs, pos)
                    ci, co, cs = ci + len(op.ins), co + len(op.outs), cs + op.n_sem

            @pl.when(first)
            def _():
                each("start")

        body(*ins, *outs, *scratch)
        if comm:
            @pl.when(last)
            def _():
                each("finish")

    hbm = pl.BlockSpec(memory_space=pl.ANY)
    res = pl.pallas_call(
        full, name=name, grid=grid,
        in_specs=list(in_specs) + [hbm] * len(c_in), out_specs=list(out_specs) + [hbm] * len(c_out),
        out_shape=list(out_shape) + c_out,
        scratch_shapes=list(scratch_shapes) + ([pltpu.SemaphoreType.DMA((n_sem,))] if comm else []),
        compiler_params=pltpu.CompilerParams(dimension_semantics=("arbitrary",) * len(grid),
                                             vmem_limit_bytes=VMEM_LIMIT))(*args, *c_in)
    k = n_out
    for op in comm:
        op.result = list(res[k:k + len(op.outs)])
        k += len(op.outs)
    return list(res[:n_out])


def _row_tile(rows, cols, itemsize=4, budget=1 << 20):
    t = rows
    while t % (2 * BF16_ROWS) == 0 and t * cols * itemsize > budget:
        t //= 2
    return t


def _cast_bf16(w, name):
    rows, cols = w.shape
    tr = _row_tile(rows, cols)

    def body(w_ref, o_ref):
        o_ref[...] = w_ref[...].astype(BF16)

    spec = pl.BlockSpec((tr, cols), lambda i: (i, 0))
    return _pallas(body, [w], name=name, grid=(rows // tr,), in_specs=[spec], out_specs=[spec],
                   out_shape=[jax.ShapeDtypeStruct((rows, cols), BF16)])[0]


def _rms_fwd(x, g, name, comm=()):
    L, D = x.shape

    def body(x_ref, g_ref, h_ref):
        xf = x_ref[...]
        r = lax.rsqrt(jnp.mean(xf * xf, axis=-1, keepdims=True) + EPS)
        h_ref[...] = (xf * r * g_ref[...]).astype(BF16)

    row = pl.BlockSpec((TM, D), lambda i: (i, 0))
    return _pallas(body, [x, g], name=name, grid=(L // TM,),
                   in_specs=[row, pl.BlockSpec((1, D), lambda i: (0, 0))], out_specs=[row],
                   out_shape=[jax.ShapeDtypeStruct((L, D), BF16)], comm=comm)[0]


def _ffn_up(h, wg_t, wu_t, name, comm=()):
    L, D = h.shape
    G, FS, _ = wg_t.shape

    def body(h_ref, wg_ref, wu_ref, a_ref, b_ref, act_ref):
        hv = h_ref[...]
        a = _dot_nt(hv, wg_ref[...])
        b = _dot_nt(hv, wu_ref[...])
        a_ref[...] = a.astype(BF16)
        b_ref[...] = b.astype(BF16)
        act_ref[...] = (a * _sigmoid(a) * b).astype(BF16)

    wspec = pl.BlockSpec((None, FS, D), lambda j, i: (j, 0, 0))
    ospec = pl.BlockSpec((None, TM, FS), lambda j, i: (j, i, 0))
    oshape = jax.ShapeDtypeStruct((G, L, FS), BF16)
    return _pallas(body, [h, wg_t, wu_t], name=name, grid=(G, L // TM),
                   in_specs=[pl.BlockSpec((TM, D), lambda j, i: (i, 0)), wspec, wspec],
                   out_specs=[ospec, ospec, ospec], out_shape=[oshape, oshape, oshape], comm=comm)


def _resident(shape):
    return pl.BlockSpec(shape, lambda *_: (0,) * len(shape), pipeline_mode=pl.Buffered(1))


def _group_sum(a_ref, w_ref, groups, mm=_dot):
    acc = mm(a_ref[0], w_ref[0])
    for j in range(1, groups):
        acc = acc + mm(a_ref[j], w_ref[j])
    return acc


def _ffn_down(act, wd, x, g_next, name, comm=()):
    G, L, FS = act.shape
    D = wd.shape[2]

    def body(act_ref, wd_ref, x_ref, g_ref, xn_ref, hn_ref):
        xn = x_ref[...] + 0.5 * _group_sum(act_ref, wd_ref, G)
        xn_ref[...] = xn
        r = lax.rsqrt(jnp.mean(xn * xn, axis=-1, keepdims=True) + EPS)
        hn_ref[...] = (xn * r * g_ref[...]).astype(BF16)

    row = pl.BlockSpec((TM, D), lambda i: (i, 0))
    return _pallas(body, [act, wd, x, g_next], name=name, grid=(L // TM,),
                   in_specs=[pl.BlockSpec((G, TM, FS), lambda i: (0, i, 0)), _resident((G, FS, D)), row,
                             pl.BlockSpec((1, D), lambda i: (0, 0))],
                   out_specs=[row, row],
                   out_shape=[jax.ShapeDtypeStruct((L, D), F32), jax.ShapeDtypeStruct((L, D), BF16)], comm=comm)


def _ffn_down_loss(act, wd, x, gf, tgt, name):
    G, L, FS = act.shape
    D = wd.shape[2]

    def body(act_ref, wd_ref, x_ref, g_ref, t_ref, dx_ref, dxb_ref, loss_ref, dg_ref):
        i = pl.program_id(0)
        xn = x_ref[...] + 0.5 * _group_sum(act_ref, wd_ref, G)
        r = lax.rsqrt(jnp.mean(xn * xn, axis=-1, keepdims=True) + EPS)
        xh = xn * r
        gv = g_ref[...]
        e = xh * gv - t_ref[...]
        part = 0.5 * jnp.sum(_colsum(e * e), axis=1, keepdims=True) / D
        dy = e / D
        _accumulate(loss_ref, i == 0, jnp.broadcast_to(part, (1, LANES)))
        _accumulate(dg_ref, i == 0, _colsum(dy * xh))
        dxh = dy * gv
        dx = r * (dxh - xh * jnp.mean(dxh * xh, axis=-1, keepdims=True))
        dx_ref[...] = dx
        dxb_ref[...] = dx.astype(BF16)

    row = pl.BlockSpec((TM, D), lambda i: (i, 0))
    return _pallas(body, [act, wd, x, gf, tgt], name=name, grid=(L // TM,),
                   in_specs=[pl.BlockSpec((G, TM, FS), lambda i: (0, i, 0)), _resident((G, FS, D)), row,
                             pl.BlockSpec((1, D), lambda i: (0, 0)), row],
                   out_specs=[row, row, pl.BlockSpec((1, LANES), lambda i: (0, 0)),
                              pl.BlockSpec((1, D), lambda i: (0, 0))],
                   out_shape=[jax.ShapeDtypeStruct((L, D), F32), jax.ShapeDtypeStruct((L, D), BF16),
                              jax.ShapeDtypeStruct((1, LANES), F32), jax.ShapeDtypeStruct((1, D), F32)])


def _ffn_bwd_act(dxb, wd, a, b, name, comm=()):
    L, D = dxb.shape
    G, FS, _ = wd.shape

    def body(dx_ref, wd_ref, a_ref, b_ref, da_ref, db_ref):
        dact = 0.5 * _dot_nt(dx_ref[...], wd_ref[...])
        av = a_ref[...].astype(F32)
        bv = b_ref[...].astype(F32)
        sg = _sigmoid(av)
        da_ref[...] = (dact * bv * sg * (1.0 + av * (1.0 - sg))).astype(BF16)
        db_ref[...] = (dact * av * sg).astype(BF16)

    gspec = pl.BlockSpec((None, TM, FS), lambda j, i: (j, i, 0))
    oshape = jax.ShapeDtypeStruct((G, L, FS), BF16)
    return _pallas(body, [dxb, wd, a, b], name=name, grid=(G, L // TM),
                   in_specs=[pl.BlockSpec((TM, D), lambda j, i: (i, 0)),
                             pl.BlockSpec((None, FS, D), lambda j, i: (j, 0, 0)), gspec, gspec],
                   out_specs=[gspec, gspec], out_shape=[oshape, oshape], comm=comm)


def _mm_grouped(a, w, name):
    L, K = a.shape
    G, _, N = w.shape

    def body(a_ref, w_ref, o_ref):
        o_ref[...] = _dot(a_ref[...], w_ref[...])

    return _pallas(body, [a, w], name=name, grid=(G, L // TM),
                   in_specs=[pl.BlockSpec((TM, K), lambda g, i: (i, 0)), pl.BlockSpec((None, K, N), lambda g, i: (g, 0, 0))],
                   out_specs=[pl.BlockSpec((TM, N), lambda g, i: (i, g))],
                   out_shape=[jax.ShapeDtypeStruct((L, G * N), F32)])[0]


def _mm_nt(a, w, name):
    L, K = a.shape
    N = w.shape[0]

    def body(a_ref, w_ref, o_ref):
        o_ref[...] = _dot_nt(a_ref[...], w_ref[...]).astype(BF16)

    return _pallas(body, [a, w], name=name, grid=(L // TM,),
                   in_specs=[pl.BlockSpec((TM, K), lambda i: (i, 0)), _resident((N, K))],
                   out_specs=[pl.BlockSpec((TM, N), lambda i: (i, 0))],
                   out_shape=[jax.ShapeDtypeStruct((L, N), BF16)])[0]


def _mm_tn(a, b, scale, name, groups, b_cols=False, comm=()):
    L, M = a.shape[1], a.shape[2]
    N = b.shape[1] // groups if b_cols else b.shape[2]
    tk = min(L, TK)
    nk = L // tk

    def spec(v, cols):
        if cols:
            return pl.BlockSpec((tk, v.shape[1] // groups), lambda g, k: (k, g))
        if v.shape[0] > 1:
            return pl.BlockSpec((None, tk, v.shape[2]), lambda g, k: (g, k, 0))
        return pl.BlockSpec((None, tk, v.shape[2]), lambda g, k: (0, k, 0))

    def body(a_ref, b_ref, o_ref, acc):
        k = pl.program_id(1)
        p = _dot_tn(a_ref[...], b_ref[...])
        if nk == 1:
            o_ref[...] = (p * scale).astype(BF16)
        else:
            _accumulate(acc, k == 0, p)

            @pl.when(k == nk - 1)
            def _():
                o_ref[...] = (acc[...] * scale).astype(BF16)

    return _pallas(body, [a, b], name=name, grid=(groups, nk),
                   in_specs=[spec(a, False), spec(b, b_cols)],
                   out_specs=[pl.BlockSpec((None, M, N), lambda g, k: (g, 0, 0))],
                   out_shape=[jax.ShapeDtypeStruct((groups, M, N), BF16)],
                   scratch_shapes=[pltpu.VMEM((M, N), F32)], comm=comm)[0]


def _mm_rmsbwd(a_list, w_list, nt, a_cols, x_in, g, dx_out, name, comm=()):
    P = len(a_list)
    G = w_list[0].shape[0]
    L, D = x_in.shape
    mm = _dot_nt if nt else _dot

    def body(*refs):
        a_refs, w_refs = refs[:P], refs[P:2 * P]
        x_ref, g_ref, dxo_ref, dx_ref, dxb_ref, dg_ref = refs[2 * P:]
        i = pl.program_id(0)
        dh = None
        for a_ref, w_ref in zip(a_refs, w_refs):
            for j in range(G):
                if a_cols:
                    kw = a_ref.shape[1] // G
                    term = mm(a_ref[:, j * kw:(j + 1) * kw], w_ref[j])
                else:
                    term = mm(a_ref[j], w_ref[j])
                dh = term if dh is None else dh + term
        xf = x_ref[...]
        r = lax.rsqrt(jnp.mean(xf * xf, axis=-1, keepdims=True) + EPS)
        xh = xf * r
        _accumulate(dg_ref, i == 0, _colsum(dh * xh))
        dxh = dh * g_ref[...]
        dx = dxo_ref[...] + r * (dxh - xh * jnp.mean(dxh * xh, axis=-1, keepdims=True))
        dx_ref[...] = dx
        dxb_ref[...] = dx.astype(BF16)

    row = pl.BlockSpec((TM, D), lambda i: (i, 0))
    vec = pl.BlockSpec((1, D), lambda i: (0, 0))
    if a_cols:
        a_specs = [pl.BlockSpec((TM, a.shape[1]), lambda i: (i, 0)) for a in a_list]
    else:
        a_specs = [pl.BlockSpec((G, TM, a.shape[2]), lambda i: (0, i, 0)) for a in a_list]
    w_specs = [_resident(w.shape) for w in w_list]
    return _pallas(body, [*a_list, *w_list, x_in, g, dx_out], name=name, grid=(L // TM,),
                   in_specs=a_specs + w_specs + [row, vec, row], out_specs=[row, row, vec],
                   out_shape=[jax.ShapeDtypeStruct((L, D), F32), jax.ShapeDtypeStruct((L, D), BF16),
                              jax.ShapeDtypeStruct((1, D), F32)], comm=comm)


def _mix_out(cat, wout, x1, g_next, name):
    L, K = cat.shape
    D = wout.shape[1]

    def body(c_ref, w_ref, x_ref, g_ref, xn_ref, hn_ref):
        xn = x_ref[...] + _dot(c_ref[...], w_ref[...])
        xn_ref[...] = xn
        r = lax.rsqrt(jnp.mean(xn * xn, axis=-1, keepdims=True) + EPS)
        hn_ref[...] = (xn * r * g_ref[...]).astype(BF16)

    row = pl.BlockSpec((TM, D), lambda i: (i, 0))
    return _pallas(body, [cat, wout, x1, g_next], name=name, grid=(L // TM,),
                   in_specs=[pl.BlockSpec((TM, K), lambda i: (i, 0)), pl.BlockSpec((K, D), lambda i: (0, 0)), row,
                             pl.BlockSpec((1, D), lambda i: (0, 0))],
                   out_specs=[row, row],
                   out_shape=[jax.ShapeDtypeStruct((L, D), F32), jax.ShapeDtypeStruct((L, D), BF16)])


def _s5_disc(lr, li, ldt, brc, bic):
    dt = jnp.exp(ldt)
    mag = jnp.exp(lr * dt)
    are = mag * jnp.cos(li * dt)
    aim = mag * jnp.sin(li * dt)
    den = lr * lr + li * li
    nre = are - 1.0
    fre = (nre * lr + aim * li) / den
    fim = (aim * lr - nre * li) / den
    return are, aim, fre * brc - fim * bic, fre * bic + fim * brc


def _s5_params_fwd(lr, li, ldt, brc, bic, crc, cic):
    def body(lr_ref, li_ref, ldt_ref, br_ref, bi_ref, cr_ref, ci_ref, are_ref, aim_ref, bre_ref, bim_ref, cre_ref, cim_ref):
        are, aim, bre, bim = _s5_disc(lr_ref[...], li_ref[...], ldt_ref[...], br_ref[...], bi_ref[...])
        are_ref[...] = are
        aim_ref[...] = aim
        bre_ref[...] = bre.astype(BF16)
        bim_ref[...] = bim.astype(BF16)
        cre_ref[...] = cr_ref[...].astype(BF16)
        cim_ref[...] = ci_ref[...].astype(BF16)

    vec = jax.ShapeDtypeStruct((1, S5_N), F32)
    return pl.pallas_call(
        body, name="s5_params_fwd",
        out_shape=[vec, vec, jax.ShapeDtypeStruct((LANES, S5_N), BF16), jax.ShapeDtypeStruct((LANES, S5_N), BF16),
                   jax.ShapeDtypeStruct((S5_N, LANES), BF16), jax.ShapeDtypeStruct((S5_N, LANES), BF16)],
        compiler_params=pltpu.CompilerParams(vmem_limit_bytes=VMEM_LIMIT))(lr, li, ldt, brc, bic, crc, cic)


def _s5_params_bwd(lr, li, ldt, brc, bic, dare, daim, dbre, dbim):
    def body(lr_ref, li_ref, ldt_ref, br_ref, bi_ref, dare_ref, daim_ref, dbre_ref, dbim_ref,
             glr_ref, gli_ref, gldt_ref, gbr_ref, gbi_ref):
        _, vjp = jax.vjp(_s5_disc, lr_ref[...], li_ref[...], ldt_ref[...], br_ref[...], bi_ref[...])
        glr, gli, gldt, gbr, gbi = vjp((dare_ref[...], daim_ref[...], dbre_ref[...], dbim_ref[...]))
        glr_ref[...] = glr
        gli_ref[...] = gli
        gldt_ref[...] = gldt
        gbr_ref[...] = gbr
        gbi_ref[...] = gbi

    vec = jax.ShapeDtypeStruct((1, S5_N), F32)
    mat = jax.ShapeDtypeStruct((LANES, S5_N), F32)
    return pl.pallas_call(
        body, name="s5_params_bwd", out_shape=[vec, vec, vec, mat, mat],
        compiler_params=pltpu.CompilerParams(vmem_limit_bytes=VMEM_LIMIT))(lr, li, ldt, brc, bic, dare, daim, dbre, dbim)


def _gelu_parts(y):
    th = jnp.tanh(GELU_C0 * (y + GELU_C1 * y * y * y))
    return 0.5 * y * (1.0 + th), th


def _state_rows(q, T):
    return pl.ds(q % S5_HALF, T, stride=S5_HALF)


def _s5_fwd(u, are, aim, bre, bim, cre, cim, d_skip, wglu, bglu, comm=()):
    L = u.shape[0]
    T = min(TS, L)
    n = L // T

    def body(u_ref, are_ref, aim_ref, bre_ref, bim_ref, cre_ref, cim_ref, d_ref, wg_ref, bg_ref,
             sre_lo, sre_hi, sim_lo, sim_hi, y_ref, o_ref, st_re, st_im):
        i = pl.program_id(0)
        sre, sim = (sre_lo, sre_hi), (sim_lo, sim_hi)

        @pl.when(i == 0)
        def _():
            st_re[...] = jnp.zeros_like(st_re)
            st_im[...] = jnp.zeros_like(st_im)

        uf = u_ref[...]
        ub = uf.astype(BF16)
        for q in range(S5_TILES):
            ct = q // 4
            uq = ub[:, ct * LANES:(ct + 1) * LANES]
            sre[q // S5_HALF][_state_rows(q, T), :] = _dot(uq, bre_ref[:, q * LANES:(q + 1) * LANES])
            sim[q // S5_HALF][_state_rows(q, T), :] = _dot(uq, bim_ref[:, q * LANES:(q + 1) * LANES])
        halves = [slice(h * S5_HALF, (h + 1) * S5_HALF) for h in range(2)]
        a_re = [are_ref[hs, :] for hs in halves]
        a_im = [aim_ref[hs, :] for hs in halves]

        def step(t, carry):
            rows = pl.ds(pl.multiple_of(t * S5_HALF, S5_HALF), S5_HALF)
            out = []
            for h in range(2):
                s_re, s_im = carry[2 * h], carry[2 * h + 1]
                n_re = a_re[h] * s_re - a_im[h] * s_im + sre[h][rows, :]
                n_im = a_re[h] * s_im + a_im[h] * s_re + sim[h][rows, :]
                sre[h][rows, :] = n_re
                sim[h][rows, :] = n_im
                out += [n_re, n_im]
            return tuple(out)

        init = (st_re[halves[0], :], st_im[halves[0], :], st_re[halves[1], :], st_im[halves[1], :])
        fin = lax.fori_loop(0, T, step, init, unroll=4)
        for h in range(2):
            st_re[halves[h], :] = fin[2 * h]
            st_im[halves[h], :] = fin[2 * h + 1]
        tiles = []
        for ct in range(4):
            acc = jnp.zeros((T, LANES), F32)
            for q in range(4 * ct, 4 * ct + 4):
                win = slice(q * LANES, (q + 1) * LANES)
                acc = acc + _dot(sre[q // S5_HALF][_state_rows(q, T), :].astype(BF16), cre_ref[win, :])
                acc = acc - _dot(sim[q // S5_HALF][_state_rows(q, T), :].astype(BF16), cim_ref[win, :])
            tiles.append(acc)
        y = jnp.concatenate(tiles, axis=1) + d_ref[...] * uf
        y_ref[...] = y
        yg, _ = _gelu_parts(y)
        gate = _sigmoid(_dot(yg.astype(BF16), wg_ref[...]) + bg_ref[...])
        o_ref[...] = (yg * gate).astype(BF16)

    const = lambda shape: pl.BlockSpec(shape, lambda i: (0, 0))
    sspec = pl.BlockSpec((T * S5_HALF, LANES), lambda i: (i, 0))
    sshape = jax.ShapeDtypeStruct((L * S5_HALF, LANES), F32)
    chunk = pl.BlockSpec((T, S5_W), lambda i: (i, 0))
    return _pallas(body, [u, are, aim, bre, bim, cre, cim, d_skip, wglu, bglu], name="s5_fwd", grid=(n,),
                   in_specs=[chunk, const((S5_TILES, LANES)), const((S5_TILES, LANES)),
                             const((LANES, S5_N)), const((LANES, S5_N)), const((S5_N, LANES)), const((S5_N, LANES)),
                             const((1, S5_W)), const((S5_W, S5_W)), const((1, S5_W))],
                   out_specs=[sspec] * 4 + [chunk, chunk],
                   out_shape=[sshape] * 4 + [jax.ShapeDtypeStruct((L, S5_W), F32), jax.ShapeDtypeStruct((L, S5_W), BF16)],
                   scratch_shapes=[pltpu.VMEM((S5_TILES, LANES), F32), pltpu.VMEM((S5_TILES, LANES), F32)], comm=comm)


def _s5_bwd(dm, y_pre, u, states, are, aim, bre, bim, cre, cim, d_skip, wglu, bglu, comm=()):
    L = u.shape[0]
    T = min(TS, L)
    n = L // T

    def body(dm_ref, y_ref, u_ref, sre_lo, sre_hi, sim_lo, sim_hi, pre_lo, pre_hi, pim_lo, pim_hi,
             are_ref, aim_ref, bre_ref, bim_ref, cre_ref, cim_ref, d_ref, wg_ref, bg_ref,
             du_ref, dwg_ref, dbg_ref, dd_ref, dcre_ref, dcim_ref, dbre_ref, dbim_ref, dare_ref, daim_ref,
             gre_lo, gre_hi, gim_lo, gim_hi, car_re, car_im):
        i = pl.program_id(0)
        first = i == 0
        sre, sim = (sre_lo, sre_hi), (sim_lo, sim_hi)
        gre, gim = (gre_lo, gre_hi), (gim_lo, gim_hi)
        pre, pim = (pre_lo, pre_hi), (pim_lo, pim_hi)

        @pl.when(first)
        def _():
            car_re[...] = jnp.zeros_like(car_re)
            car_im[...] = jnp.zeros_like(car_im)
            dcre_ref[...] = jnp.zeros_like(dcre_ref)
            dcim_ref[...] = jnp.zeros_like(dcim_ref)
            dbre_ref[...] = jnp.zeros_like(dbre_ref)
            dbim_ref[...] = jnp.zeros_like(dbim_ref)

        y = y_ref[...]
        uf = u_ref[...]
        yg, th = _gelu_parts(y)
        dgelu = 0.5 * (1.0 + th) + 0.5 * y * (1.0 - th * th) * GELU_C0 * (1.0 + 3.0 * GELU_C1 * y * y)
        ygb = yg.astype(BF16)
        sg = _sigmoid(_dot(ygb, wg_ref[...]) + bg_ref[...])
        dout = dm_ref[...].astype(F32)
        dgp = dout * yg * sg * (1.0 - sg)
        dgpb = dgp.astype(BF16)
        dyg = dout * sg + _dot_nt(dgpb, wg_ref[...])
        _accumulate(dwg_ref, first, _dot_tn(ygb, dgpb))
        _accumulate(dbg_ref, first, _colsum(dgp))
        dy = dyg * dgelu
        _accumulate(dd_ref, first, _colsum(dy * uf))
        dyb = dy.astype(BF16)
        ub = uf.astype(BF16)

        for q in range(S5_TILES):
            ct, h = q // 4, q // S5_HALF
            win = slice(q * LANES, (q + 1) * LANES)
            dyq = dyb[:, ct * LANES:(ct + 1) * LANES]
            dcre_ref[win, :] += _dot_tn(sre[h][_state_rows(q, T), :].astype(BF16), dyq)
            dcim_ref[win, :] -= _dot_tn(sim[h][_state_rows(q, T), :].astype(BF16), dyq)
            gre[h][_state_rows(q, T), :] = _dot_nt(dyq, cre_ref[win, :])
            gim[h][_state_rows(q, T), :] = -_dot_nt(dyq, cim_ref[win, :])

        halves = [slice(h * S5_HALF, (h + 1) * S5_HALF) for h in range(2)]
        a_re = [are_ref[hs, :] for hs in halves]
        a_im = [aim_ref[hs, :] for hs in halves]

        def adjoint(t, h, g_re, g_im):
            rows = pl.ds(pl.multiple_of(t * S5_HALF, S5_HALF), S5_HALF)
            n_re = gre[h][rows, :] + a_re[h] * g_re + a_im[h] * g_im
            n_im = gim[h][rows, :] + a_re[h] * g_im - a_im[h] * g_re
            gre[h][rows, :] = n_re
            gim[h][rows, :] = n_im
            return n_re, n_im

        def step(k, carry):
            t = T - 1 - k
            prev = pl.ds(pl.multiple_of((t - 1) * S5_HALF, S5_HALF), S5_HALF)
            out = []
            for h in range(2):
                g_re, g_im, da_re, da_im = carry[4 * h:4 * h + 4]
                g_re, g_im = adjoint(t, h, g_re, g_im)
                p_re, p_im = sre[h][prev, :], sim[h][prev, :]
                out += [g_re, g_im, da_re + g_re * p_re + g_im * p_im, da_im + g_im * p_re - g_re * p_im]
            return tuple(out)

        zero = jnp.zeros((S5_HALF, LANES), F32)
        init = (car_re[halves[0], :], car_im[halves[0], :], zero, zero, car_re[halves[1], :], car_im[halves[1], :], zero, zero)
        fin = lax.fori_loop(0, T - 1, step, init, unroll=4)
        keep = (i < n - 1).astype(F32)
        for h in range(2):
            g_re, g_im, da_re, da_im = fin[4 * h:4 * h + 4]
            g_re, g_im = adjoint(0, h, g_re, g_im)
            p_re, p_im = pre[h][...] * keep, pim[h][...] * keep
            car_re[halves[h], :] = g_re
            car_im[halves[h], :] = g_im
            da_re = da_re + g_re * p_re + g_im * p_im
            da_im = da_im + g_im * p_re - g_re * p_im

            @pl.when(first)
            def _():
                dare_ref[halves[h], :] = da_re
                daim_ref[halves[h], :] = da_im

            @pl.when(jnp.logical_not(first))
            def _():
                dare_ref[halves[h], :] += da_re
                daim_ref[halves[h], :] += da_im

        tiles = []
        for ct in range(4):
            uq = ub[:, ct * LANES:(ct + 1) * LANES]
            acc = d_ref[:, ct * LANES:(ct + 1) * LANES] * dy[:, ct * LANES:(ct + 1) * LANES]
            for q in range(4 * ct, 4 * ct + 4):
                h = q // S5_HALF
                win = slice(q * LANES, (q + 1) * LANES)
                gq_re = gre[h][_state_rows(q, T), :].astype(BF16)
                gq_im = gim[h][_state_rows(q, T), :].astype(BF16)
                acc = acc + _dot_nt(gq_re, bre_ref[:, win]) + _dot_nt(gq_im, bim_ref[:, win])
                dbre_ref[:, win] += _dot_tn(uq, gq_re)
                dbim_ref[:, win] += _dot_tn(uq, gq_im)
            tiles.append(acc)
        du_ref[...] = jnp.concatenate(tiles, axis=1).astype(BF16)

    rev = lambda i: (n - 1 - i, 0)
    const = lambda shape: pl.BlockSpec(shape, lambda i: (0, 0))
    chunk = pl.BlockSpec((T, S5_W), rev)
    sspec = pl.BlockSpec((T * S5_HALF, LANES), rev)
    pspec = pl.BlockSpec((S5_HALF, LANES), lambda i: (jnp.maximum((n - 1 - i) * T - 1, 0), 0))
    tile = jax.ShapeDtypeStruct((S5_TILES, LANES), F32)
    vec = jax.ShapeDtypeStruct((1, S5_W), F32)
    sbuf = pltpu.VMEM((T * S5_HALF, LANES), F32)
    return _pallas(
        body, [dm, y_pre, u, *states, *states, are, aim, bre, bim, cre, cim, d_skip, wglu, bglu],
        name="s5_bwd", grid=(n,),
        in_specs=[chunk, chunk, chunk] + [sspec] * 4 + [pspec] * 4 + [
            const((S5_TILES, LANES)), const((S5_TILES, LANES)), const((LANES, S5_N)), const((LANES, S5_N)),
            const((S5_N, LANES)), const((S5_N, LANES)), const((1, S5_W)), const((S5_W, S5_W)), const((1, S5_W))],
        out_specs=[chunk, const((S5_W, S5_W)), const((1, S5_W)), const((1, S5_W)), const((S5_N, LANES)),
                   const((S5_N, LANES)), const((LANES, S5_N)), const((LANES, S5_N)), const((S5_TILES, LANES)),
                   const((S5_TILES, LANES))],
        out_shape=[jax.ShapeDtypeStruct((L, S5_W), BF16), jax.ShapeDtypeStruct((S5_W, S5_W), F32), vec, vec,
                   jax.ShapeDtypeStruct((S5_N, LANES), F32), jax.ShapeDtypeStruct((S5_N, LANES), F32),
                   jax.ShapeDtypeStruct((LANES, S5_N), F32), jax.ShapeDtypeStruct((LANES, S5_N), F32), tile, tile],
        scratch_shapes=[sbuf, sbuf, sbuf, sbuf, pltpu.VMEM((S5_TILES, LANES), F32), pltpu.VMEM((S5_TILES, LANES), F32)],
        comm=comm)


_EYE8 = np.eye(8, dtype=np.float32)


def _compact_b(b):
    return jnp.einsum("akpc,kj->jcakp", b.reshape(4, 8, S5_P, S5_GC), _EYE8).reshape(LANES, S5_N)


def _uncompact_b(m):
    return jnp.einsum("kcakp->akpc", m.reshape(8, S5_GC, 4, 8, S5_P)).reshape(S5_G, S5_P, S5_GC)


_HEAD_MEAN = np.kron(np.eye(CONV_W // CONV_HD, dtype=np.float32), np.full((CONV_HD, CONV_HD), 1.0 / CONV_HD, np.float32))


def _head_mean(v, m):
    hi = v.astype(BF16)
    lo = (v - hi.astype(F32)).astype(BF16)
    return _dot(hi, m) + _dot(lo, m)


def _head_norm(zc, m):
    d = zc - _head_mean(zc, m)
    rstd = lax.rsqrt(_head_mean(d * d, m) + EPS)
    return d * rstd, rstd


def _conv_fwd(u, o_s5, wdw, bdw, lng, lnb, mavg, comm=()):
    L = u.shape[0]
    T = min(TS, L)
    n = L // T
    first_tap = HALO - (CONV_K - 1)

    def body(v1_ref, v2_ref, s5_ref, w_ref, b_ref, g_ref, be_ref, m_ref, zc_ref, o_ref, zbuf):
        i = pl.program_id(0)

        @pl.when(i == 0)
        def _():
            zbuf[0:HALO, :] = jnp.zeros((HALO, CONV_W), F32)

        zbuf[HALO:HALO + T, :] = v1_ref[...] * _sigmoid(v2_ref[...])
        for r0 in range(0, T, CONV_SB):
            acc = jnp.broadcast_to(b_ref[...], (CONV_SB, CONV_W))
            for k in range(CONV_K):
                acc = acc + w_ref[k:k + 1, :] * zbuf[pl.ds(r0 + first_tap + k, CONV_SB), :]
            zc_ref[r0:r0 + CONV_SB, :] = acc
        zbuf[0:HALO, :] = zbuf[T:T + HALO, :]
        zn, _ = _head_norm(zc_ref[...], m_ref[...])
        zz = zn * g_ref[...] + be_ref[...]
        o_ref[:, 0:S5_W] = s5_ref[...]
        o_ref[:, S5_W:S5_W + CONV_W] = (zz * _sigmoid(zz)).astype(BF16)

    const = lambda shape: pl.BlockSpec(shape, lambda i: (0, 0))
    vec = const((1, CONV_W))
    return _pallas(body, [u, u, o_s5, wdw, bdw, lng, lnb, mavg], name="conv_fwd", grid=(n,),
                   in_specs=[pl.BlockSpec((T, CONV_W), lambda i: (i, 1)), pl.BlockSpec((T, CONV_W), lambda i: (i, 2)),
                             pl.BlockSpec((T, S5_W), lambda i: (i, 0)), const((CONV_K, CONV_W)), vec, vec, vec,
                             const((CONV_W, CONV_W))],
                   out_specs=[pl.BlockSpec((T, CONV_W), lambda i: (i, 0)),
                              pl.BlockSpec((T, S5_W + CONV_W), lambda i: (i, 0))],
                   out_shape=[jax.ShapeDtypeStruct((L, CONV_W), F32), jax.ShapeDtypeStruct((L, S5_W + CONV_W), BF16)],
                   scratch_shapes=[pltpu.VMEM((T + HALO, CONV_W), F32)], comm=comm)


def _conv_bwd(dm, zc, u, du_s5, wdw, lng, lnb, mavg, comm=()):
    L = u.shape[0]
    T = min(TS, L)
    n = L // T
    hb = T // HALO
    first_tap = HALO - (CONV_K - 1)

    def body(dm_ref, zc_ref, v1_ref, v2_ref, p1_ref, p2_ref, s5_ref, w_ref, g_ref, be_ref, m_ref,
             du_ref, dw_ref, db_ref, dg_ref, dbe_ref, zbuf, dzbuf, head):
        i = pl.program_id(0)
        first = i == 0

        @pl.when(first)
        def _():
            head[...] = jnp.zeros_like(head)
            dw_ref[...] = jnp.zeros_like(dw_ref)

        zn, rstd = _head_norm(zc_ref[...], m_ref[...])
        zz = zn * g_ref[...] + be_ref[...]
        sg = _sigmoid(zz)
        dzz = dm_ref[...].astype(F32) * sg * (1.0 + zz * (1.0 - sg))
        _accumulate(dbe_ref, first, _colsum(dzz))
        _accumulate(dg_ref, first, _colsum(dzz * zn))
        dzn = dzz * g_ref[...]
        dzc = rstd * (dzn - _head_mean(dzn, m_ref[...]) - zn * _head_mean(dzn * zn, m_ref[...]))
        _accumulate(db_ref, first, _colsum(dzc))

        dzbuf[0:T, :] = dzc
        dzbuf[T:T + HALO, :] = head[...]
        head[...] = dzbuf[0:HALO, :]
        keep = (i < n - 1).astype(F32)
        zbuf[0:HALO, :] = p1_ref[...] * _sigmoid(p2_ref[...]) * keep
        zbuf[HALO:HALO + T, :] = v1_ref[...] * _sigmoid(v2_ref[...])
        du_ref[:, 0:S5_W] = s5_ref[...]

        for r0 in range(0, T, CONV_SB):
            rows = slice(r0, r0 + CONV_SB)
            dzc_b = dzbuf[rows, :]
            dz = jnp.zeros((CONV_SB, CONV_W), F32)
            for k in range(CONV_K):
                prod = dzc_b * zbuf[pl.ds(r0 + first_tap + k, CONV_SB), :]
                dw_ref[8 * k:8 * k + 8, :] += jnp.sum(prod.reshape(CONV_SB // 8, 8, CONV_W), axis=0)
                dz = dz + w_ref[k:k + 1, :] * dzbuf[pl.ds(r0 + CONV_K - 1 - k, CONV_SB), :]
            v1 = v1_ref[rows, :]
            sg2 = _sigmoid(v2_ref[rows, :])
            du_ref[rows, S5_W:S5_W + CONV_W] = (dz * sg2).astype(BF16)
            du_ref[rows, S5_W + CONV_W:S5_W + 2 * CONV_W] = (dz * v1 * sg2 * (1.0 - sg2)).astype(BF16)

    rev = lambda c: (lambda i: (n - 1 - i, c))
    prev = lambda c: (lambda i: (jnp.maximum((n - 1 - i) * hb - 1, 0), c))
    const = lambda shape: pl.BlockSpec(shape, lambda i: (0, 0))
    vec = const((1, CONV_W))
    vshape = jax.ShapeDtypeStruct((1, CONV_W), F32)
    return _pallas(
        body, [dm, zc, u, u, u, u, du_s5, wdw, lng, lnb, mavg], name="conv_bwd", grid=(n,),
        in_specs=[pl.BlockSpec((T, CONV_W), rev(1)), pl.BlockSpec((T, CONV_W), rev(0)),
                  pl.BlockSpec((T, CONV_W), rev(1)), pl.BlockSpec((T, CONV_W), rev(2)),
                  pl.BlockSpec((HALO, CONV_W), prev(1)), pl.BlockSpec((HALO, CONV_W), prev(2)),
                  pl.BlockSpec((T, S5_W), rev(0)), const((CONV_K, CONV_W)), vec, vec, const((CONV_W, CONV_W))],
        out_specs=[pl.BlockSpec((T, S5_W + 2 * CONV_W), rev(0)), const((8 * HALO, CONV_W)), vec, vec, vec],
        out_shape=[jax.ShapeDtypeStruct((L, S5_W + 2 * CONV_W), BF16), jax.ShapeDtypeStruct((8 * HALO, CONV_W), F32),
                   vshape, vshape, vshape],
        scratch_shapes=[pltpu.VMEM((T + HALO, CONV_W), F32), pltpu.VMEM((T + HALO, CONV_W), F32),
                        pltpu.VMEM((HALO, CONV_W), F32)], comm=comm)


def _gather_all(v):
    rows, cols = v.shape

    def body(x_ref, out_ref, send_sems, recv_sems, local_sem):
        x, y, c, chips = _position()
        me, sibling = (x, y, c), (x, y, 1 - c)

        def block(px, py, pc):
            return out_ref.at[pl.ds((4 * px + 2 * py + pc) * rows, rows), :]

        def copy(k, blk, to, src=None):
            return pltpu.make_async_remote_copy(
                src_ref=block(*blk) if src is None else src, dst_ref=block(*blk), send_sem=send_sems.at[k],
                recv_sem=recv_sems.at[k], device_id=to, device_id_type=MESH)

        mine = pltpu.make_async_copy(x_ref, block(*me), local_sem)
        mine.start()
        first = [copy(0, me, sibling, src=x_ref)]
        first += [copy(1 + j, me, (*chip, c), src=x_ref) for j, chip in enumerate(chips)]
        for cp in first:
            cp.start()
        passed = [copy(4 + j, (*chip, c), sibling) for j, chip in enumerate(chips)]
        for j, chip in enumerate(chips):
            copy(1 + j, (*chip, c), me).wait_recv()
            passed[j].start()
        copy(0, sibling, me).wait_recv()
        for j, chip in enumerate(chips):
            copy(4 + j, (*chip, 1 - c), me).wait_recv()
        for cp in first + passed:
            cp.wait_send()
        mine.wait()

    return pl.pallas_call(
        body, name="gather_small",
        in_specs=[pl.BlockSpec(memory_space=pltpu.VMEM)], out_specs=pl.BlockSpec(memory_space=pltpu.VMEM),
        out_shape=jax.ShapeDtypeStruct((N_DEV * rows, cols), v.dtype),
        scratch_shapes=[pltpu.SemaphoreType.DMA((7,)), pltpu.SemaphoreType.DMA((7,)), pltpu.SemaphoreType.DMA],
        compiler_params=pltpu.CompilerParams(vmem_limit_bytes=VMEM_LIMIT))(v)


def _adamw(w, g, m, v):
    m = ADAM_B1 * m + (1.0 - ADAM_B1) * g
    v = ADAM_B2 * v + (1.0 - ADAM_B2) * jnp.square(g)
    m_hat = m / (1.0 - ADAM_B1 ** ADAM_STEP)
    v_hat = v / (1.0 - ADAM_B2 ** ADAM_STEP)
    return -ADAM_LR * (m_hat / (jnp.sqrt(v_hat) + ADAM_EPS) + ADAM_WD * w), m, v


def _sum_slots(recv, name, comm=()):
    _, rows, cols = recv.shape
    tr = _row_tile(rows, cols)

    def body(r_ref, o_ref):
        acc = r_ref[0].astype(F32)
        for s in range(1, N_CHIPS):
            acc = acc + r_ref[s].astype(F32)
        o_ref[...] = acc

    return _pallas(body, [recv], name=name, grid=(rows // tr,),
                   in_specs=[pl.BlockSpec((N_CHIPS, tr, cols), lambda i: (0, i, 0))],
                   out_specs=[pl.BlockSpec((tr, cols), lambda i: (i, 0))],
                   out_shape=[jax.ShapeDtypeStruct((rows, cols), F32)], comm=comm)[0]


def _add_halves(mine, theirs, name):
    slots, rows, cols = mine.shape
    tr = _row_tile(rows, cols * slots)

    def body(a_ref, b_ref, o_ref):
        o_ref[...] = (a_ref[...].astype(F32) + b_ref[...].astype(F32)).astype(o_ref.dtype)

    spec = pl.BlockSpec((slots, tr, cols), lambda i: (0, i, 0))
    return _pallas(body, [mine, theirs], name=name, grid=(rows // tr,), in_specs=[spec, spec], out_specs=[spec],
                   out_shape=[jax.ShapeDtypeStruct(mine.shape, mine.dtype)])[0]


def _adamw_sharded(w, parts, m, v, name, comm=()):
    rows, cols = w.shape
    tr = _row_tile(rows, cols)
    n = len(parts)

    def body(w_ref, *refs):
        p_refs, (m_ref, v_ref, g_ref, d_ref, nm_ref, nv_ref) = refs[:n], refs[n:]
        g = p_refs[0][...]
        for p_ref in p_refs[1:]:
            g = g + p_ref[...]
        g_ref[...] = g
        d_ref[...], nm_ref[...], nv_ref[...] = _adamw(w_ref[...], g, m_ref[...], v_ref[...])

    spec = pl.BlockSpec((tr, cols), lambda i: (i, 0))
    shape = jax.ShapeDtypeStruct((rows, cols), F32)
    return _pallas(body, [w, *parts, m, v], name=name, grid=(rows // tr,), in_specs=[spec] * (n + 3),
                   out_specs=[spec] * 4, out_shape=[shape] * 4, comm=comm)


def _adamw_small(w, gathered, m, v):
    rows, cols = w.shape

    def body(w_ref, a_ref, m_ref, v_ref, g_ref, d_ref, nm_ref, nv_ref):
        g = a_ref[0:rows, :]
        for dev in range(1, N_DEV):
            g = g + a_ref[dev * rows:(dev + 1) * rows, :]
        g_ref[...] = g
        d_ref[...], nm_ref[...], nv_ref[...] = _adamw(w_ref[...], g, m_ref[...], v_ref[...])

    shape = jax.ShapeDtypeStruct((rows, cols), F32)
    return pl.pallas_call(
        body, name="adamw_small", out_shape=[shape] * 4,
        compiler_params=pltpu.CompilerParams(vmem_limit_bytes=VMEM_LIMIT))(w, gathered, m, v)


PACK_TILE = 8 * LANES


def _pack_small(vals, last_row):
    rows = []
    for name in SMALL:
        flat = vals[name].reshape(-1).astype(F32)
        rows.append(jnp.pad(flat, (0, -flat.size % PACK_TILE)).reshape(-1, LANES))
    rows.append(jnp.pad(last_row, ((0, 7), (0, 0))))
    return jnp.concatenate(rows, axis=0)


def _unpack_small(packed, like):
    out, r = {}, 0
    for name in SMALL:
        size = like[name].size
        out[name] = packed[r:r + -(-size // LANES)].reshape(-1)[:size].reshape(like[name].shape)
        r += 8 * -(-size // PACK_TILE)
    return out, packed[r, 0]


def _shard2d(name, v):
    v = v.reshape(v.shape[-2:])
    return v.T if name in FFN_T else v


def _unshard(name, v, shape):
    return (v.T if name in FFN_T else v).reshape(shape)


def _train_step(x3, tgt3, wts, ms, vs):
    x, tgt = x3[0], tgt3[0]
    L, D = x.shape
    row = lambda v: v.reshape(1, -1)
    shards = {k: _shard2d(k, wts[k]) for k in SHARDED}
    sends = {k: shards[k] if k == "conv_w_dw" else _cast_bf16(shards[k], "cast_" + k) for k in SHARDED}
    gat = {k: _Gather(sends[k]) for k in SHARDED}
    w = lambda k: gat[k].result[0]

    s = {k: wts[k] for k in SMALL}
    lr, li = s["s5_lam_re"].reshape(1, S5_N), s["s5_lam_im"].reshape(1, S5_N)
    ldt = jnp.repeat(s["s5_log_dt"].reshape(S5_G), S5_P).reshape(1, S5_N)
    brc, bic = _compact_b(s["s5_b_re"].reshape(S5_G, S5_P, S5_GC)), _compact_b(s["s5_b_im"].reshape(S5_G, S5_P, S5_GC))
    crc = _compact_b(s["s5_c_re"].reshape(S5_G, S5_GC, S5_P).transpose(0, 2, 1)).T
    cic = _compact_b(s["s5_c_im"].reshape(S5_G, S5_GC, S5_P).transpose(0, 2, 1)).T
    d_skip, b_glu = row(s["s5_d"]), row(s["s5_b_glu"])
    b_dw, ln_g, ln_b = row(s["conv_b_dw"]), row(s["conv_ln_g"]), row(s["conv_ln_b"])
    g1, gm, g2, gf = row(s["ffn1_norm"]), row(s["mix_norm"]), row(s["ffn2_norm"]), row(s["final_norm"])
    mavg = jnp.asarray(_HEAD_MEAN, dtype=BF16)

    h1 = _rms_fwd(x, g1, "rms1", comm=[gat["ffn1_w_gate"], gat["ffn1_w_up"]])
    a1, b1, act1 = _ffn_up(h1, w("ffn1_w_gate"), w("ffn1_w_up"), "ffn1_up", comm=[gat["ffn1_w_down"]])
    x1, h2 = _ffn_down(act1, w("ffn1_w_down"), x, gm, "ffn1_down",
                       comm=[gat["w_in"], gat["s5_w_glu"], gat["conv_w_dw"], gat["w_out"]])
    u = _mm_grouped(h2, w("w_in"), "in_proj")
    are, aim, bre, bim, cre, cim = _s5_params_fwd(lr, li, ldt, brc, bic, crc, cic)
    are_t, aim_t = are.reshape(S5_TILES, LANES), aim.reshape(S5_TILES, LANES)
    w_glu = w("s5_w_glu").reshape(S5_W, S5_W)
    *states, y_pre, o_s5 = _s5_fwd(u, are_t, aim_t, bre, bim, cre, cim, d_skip, w_glu, b_glu,
                                   comm=[gat["ffn2_w_gate"], gat["ffn2_w_up"]])
    w_dw = w("conv_w_dw").transpose(1, 0, 2).reshape(CONV_K, CONV_W)
    zc, cat = _conv_fwd(u, o_s5, w_dw, b_dw, ln_g, ln_b, mavg, comm=[gat["ffn2_w_down"]])
    w_out = w("w_out").reshape(-1, D)
    x2, h3 = _mix_out(cat, w_out, x1, g2, "mix_out")
    a2, b2, act2 = _ffn_up(h3, w("ffn2_w_gate"), w("ffn2_w_up"), "ffn2_up")
    dx3, dx3b, loss_part, d_gf = _ffn_down_loss(act2, w("ffn2_w_down"), x2, gf, tgt, "ffn2_down_loss")

    gs, sc, waiting = {"final_norm": d_gf}, {}, []

    def grad(key, g):
        if g.shape[1] % (2 * BF16_ROWS) == 0 and g.dtype == BF16:
            waiting.append(_SwapHalf(g, key))
        else:
            sc[key] = _Scatter(g)
            waiting.append(sc[key])

    def carry(call, *args, **kw):
        ops = list(waiting)
        waiting.clear()
        res = call(*args, comm=ops, **kw)
        for op in ops:
            if isinstance(op, _SwapHalf):
                sc[op.key] = _Scatter(_add_halves(*op.result, "add_" + op.key))
                waiting.append(sc[op.key])
        return res

    da2, db2 = _ffn_bwd_act(dx3b, w("ffn2_w_down"), a2, b2, "ffn2_bwd_act")
    grad("ffn2_w_down", _mm_tn(act2, dx3b[None], 0.5, "ffn2_dwd", N_CHIPS))
    grad("ffn2_w_gate", carry(_mm_tn, da2, h3[None], 1.0, "ffn2_dwg", N_CHIPS))
    grad("ffn2_w_up", carry(_mm_tn, db2, h3[None], 1.0, "ffn2_dwu", N_CHIPS))
    dx2, dx2b, gs["ffn2_norm"] = carry(_mm_rmsbwd, [da2, db2], [w("ffn2_w_gate"), w("ffn2_w_up")], False, False, x2, g2,
                                       dx3, "ffn2_bwd_dx")

    dm = _mm_nt(dx2b, w_out, "mix_bwd")
    grad("w_out", _mm_tn(cat[None], dx2b[None], 1.0, "dwout", 1).reshape(N_CHIPS, -1, D))
    (du_s5, d_wglu, gs["s5_b_glu"], gs["s5_d"], d_crc, d_cic, d_bre, d_bim, d_are, d_aim) = carry(
        _s5_bwd, dm, y_pre, u, states, are_t, aim_t, bre, bim, cre, cim, d_skip, w_glu, b_glu)
    grad("s5_w_glu", d_wglu.astype(BF16).reshape(N_CHIPS, -1, S5_W))
    g_lr, g_li, g_ldt, g_brc, g_bic = _s5_params_bwd(lr, li, ldt, brc, bic, d_are.reshape(1, S5_N),
                                                     d_aim.reshape(1, S5_N), d_bre, d_bim)
    gs["s5_lam_re"], gs["s5_lam_im"] = g_lr, g_li
    gs["s5_log_dt"] = jnp.sum(g_ldt.reshape(S5_G, S5_P), axis=1)
    gs["s5_b_re"], gs["s5_b_im"] = _uncompact_b(g_brc), _uncompact_b(g_bic)
    gs["s5_c_re"] = _uncompact_b(d_crc.T).transpose(0, 2, 1)
    gs["s5_c_im"] = _uncompact_b(d_cic.T).transpose(0, 2, 1)
    du, d_wdw, gs["conv_b_dw"], gs["conv_ln_g"], gs["conv_ln_b"] = carry(_conv_bwd, dm, zc, u, du_s5, w_dw, ln_g, ln_b, mavg)
    d_wdw = jnp.sum(d_wdw.reshape(HALO, 8, CONV_W), axis=1)[:CONV_K]
    grad("conv_w_dw", d_wdw.reshape(CONV_K, N_CHIPS, -1).transpose(1, 0, 2))
    grad("w_in", carry(_mm_tn, h2[None], du, 1.0, "dwin", N_CHIPS, b_cols=True))
    dx1, dx1b, gs["mix_norm"] = carry(_mm_rmsbwd, [du], [w("w_in")], True, True, x1, gm, dx2, "in_proj_bwd")

    da1, db1 = carry(_ffn_bwd_act, dx1b, w("ffn1_w_down"), a1, b1, "ffn1_bwd_act")
    grad("ffn1_w_down", _mm_tn(act1, dx1b[None], 0.5, "ffn1_dwd", N_CHIPS))
    grad("ffn1_w_gate", carry(_mm_tn, da1, h1[None], 1.0, "ffn1_dwg", N_CHIPS))
    grad("ffn1_w_up", carry(_mm_tn, db1, h1[None], 1.0, "ffn1_dwu", N_CHIPS))
    grad_x, _, gs["ffn1_norm"] = carry(_mm_rmsbwd, [da1, db1], [w("ffn1_w_gate"), w("ffn1_w_up")], False, False, x, g1,
                                       dx1, "ffn1_bwd_dx")

    out = {}
    gsmall = {k: gs[k].reshape(wts[k].shape) for k in SMALL}
    zero_row = jnp.zeros((1, LANES), F32)
    g_all = _gather_all(_pack_small(gsmall, loss_part))
    res = _adamw_small(_pack_small(s, zero_row), g_all, _pack_small({k: ms[k] for k in SMALL}, zero_row),
                       _pack_small({k: vs[k] for k in SMALL}, zero_row))
    unpacked = [_unpack_small(r, s) for r in res]
    loss = unpacked[0][1]
    for k in SMALL:
        out[k] = [u_[0][k] for u_ in unpacked]

    order = ("ffn2_w_down", "ffn2_w_gate", "ffn2_w_up", "w_out", "s5_w_glu", "conv_w_dw", "w_in", "ffn1_w_down",
             "ffn1_w_gate", "ffn1_w_up")
    back = {}
    for k in order:
        part = carry(_sum_slots, sc[k].result[0], "sum_" + k)
        back[k] = _SwapBack(part) if part.shape != shards[k].shape else _Swap(part)
        waiting.append(back[k])
    for k in order:
        parts = [back[k].result[0]] if isinstance(back[k], _SwapBack) else [back[k].ins[0], back[k].result[0]]
        res = carry(_adamw_sharded, shards[k], parts, _shard2d(k, ms[k]), _shard2d(k, vs[k]), "adamw_" + k)
        out[k] = [_unshard(k, r, wts[k].shape) for r in res]
    return loss, grad_x[None], out


def kernel(x, ffn1_norm, ffn1_w_gate, ffn1_w_up, ffn1_w_down, mix_norm, w_in, s5_lam_re, s5_lam_im, s5_log_dt, s5_b_re, s5_b_im, s5_c_re, s5_c_im, s5_d, s5_w_glu, s5_b_glu, conv_w_dw, conv_b_dw, conv_ln_g, conv_ln_b, w_out, ffn2_norm, ffn2_w_gate, ffn2_w_up, ffn2_w_down, final_norm, loss_target, m_ffn1_norm, m_ffn1_w_gate, m_ffn1_w_up, m_ffn1_w_down, m_mix_norm, m_w_in, m_s5_lam_re, m_s5_lam_im, m_s5_log_dt, m_s5_b_re, m_s5_b_im, m_s5_c_re, m_s5_c_im, m_s5_d, m_s5_w_glu, m_s5_b_glu, m_conv_w_dw, m_conv_b_dw, m_conv_ln_g, m_conv_ln_b, m_w_out, m_ffn2_norm, m_ffn2_w_gate, m_ffn2_w_up, m_ffn2_w_down, m_final_norm, v_ffn1_norm, v_ffn1_w_gate, v_ffn1_w_up, v_ffn1_w_down, v_mix_norm, v_w_in, v_s5_lam_re, v_s5_lam_im, v_s5_log_dt, v_s5_b_re, v_s5_b_im, v_s5_c_re, v_s5_c_im, v_s5_d, v_s5_w_glu, v_s5_b_glu, v_conv_w_dw, v_conv_b_dw, v_conv_ln_g, v_conv_ln_b, v_w_out, v_ffn2_norm, v_ffn2_w_gate, v_ffn2_w_up, v_ffn2_w_down, v_final_norm):
    given = dict(locals())
    wts = {k: given[k] for k in WEIGHTS}
    ms = {k: given["m_" + k] for k in WEIGHTS}
    vs = {k: given["v_" + k] for k in WEIGHTS}
    loss, grad_x, out = _train_step(x, loss_target, wts, ms, vs)
    return (loss, grad_x, *[out[k][0] for k in WEIGHTS], *[out[k][1] for k in WEIGHTS],
            *[out[k][2] for k in WEIGHTS], *[out[k][3] for k in WEIGHTS])
```

```python
import functools

import jax
import jax.numpy as jnp
import numpy as np
from jax import lax
from jax.experimental import pallas as pl
from jax.experimental.pallas import tpu as pltpu

F32, BF16 = jnp.float32, jnp.bfloat16
MESH = pl.DeviceIdType.MESH

EPS = 1e-6
ADAM_LR, ADAM_B1, ADAM_B2, ADAM_EPS, ADAM_WD, ADAM_STEP = 0.001, 0.9, 0.999, 1e-08, 0.01, 10

N_CHIPS = 4
N_DEV = 8
LANES = 128
BF16_ROWS = 16
S5_W, S5_G, S5_GC, S5_P = 512, 32, 16, 64
S5_N = S5_G * S5_P
S5_TILES = S5_N // LANES
S5_HALF = 8
CONV_W, CONV_K, CONV_HD = 512, 31, 64
HALO = 32
CONV_SB = 32
TM = 512
TMS = 1024
TK = 2048
TS = 256
VMEM_LIMIT = 48 << 20
GELU_C0, GELU_C1 = 0.7978845608028654, 0.044715

FFN_T = ("ffn1_w_gate", "ffn1_w_up", "ffn2_w_gate", "ffn2_w_up")
SHARDED = ("ffn1_w_gate", "ffn1_w_up", "ffn1_w_down", "w_in", "s5_w_glu", "conv_w_dw", "w_out",
           "ffn2_w_gate", "ffn2_w_up", "ffn2_w_down")
SMALL = ("ffn1_norm", "mix_norm", "s5_lam_re", "s5_lam_im", "s5_log_dt", "s5_b_re", "s5_b_im", "s5_c_re",
         "s5_c_im", "s5_d", "s5_b_glu", "conv_b_dw", "conv_ln_g", "conv_ln_b", "ffn2_norm", "final_norm")
WEIGHTS = ("ffn1_norm", "ffn1_w_gate", "ffn1_w_up", "ffn1_w_down", "mix_norm", "w_in", "s5_lam_re", "s5_lam_im",
           "s5_log_dt", "s5_b_re", "s5_b_im", "s5_c_re", "s5_c_im", "s5_d", "s5_w_glu", "s5_b_glu", "conv_w_dw",
           "conv_b_dw", "conv_ln_g", "conv_ln_b", "w_out", "ffn2_norm", "ffn2_w_gate", "ffn2_w_up", "ffn2_w_down",
           "final_norm")


def _dot(a, b):
    return jnp.dot(a, b, preferred_element_type=F32)


def _dot_nt(a, b):
    return lax.dot_general(a, b, (((1,), (1,)), ((), ())), preferred_element_type=F32)


def _dot_tn(a, b):
    return lax.dot_general(a, b, (((0,), (0,)), ((), ())), preferred_element_type=F32)


def _colsum(v):
    return jnp.sum(v, axis=0, keepdims=True)


def _sigmoid(v):
    return 1.0 / (1.0 + jnp.exp(-v))


def _accumulate(ref, first, value):
    @pl.when(first)
    def _():
        ref[...] = value

    @pl.when(jnp.logical_not(first))
    def _():
        ref[...] += value


def _position():
    x, y, c = lax.axis_index("x"), lax.axis_index("y"), lax.axis_index("c")
    return x, y, c, [(1 - x, y), (x, 1 - y), (1 - x, 1 - y)]


def _remote(src, dst, sems, send, recv, device):
    return pltpu.make_async_remote_copy(src_ref=src, dst_ref=dst, send_sem=sems.at[send], recv_sem=sems.at[recv],
                                        device_id=device, device_id_type=MESH)


class _Gather:
    def __init__(self, shard):
        self.ins = [shard]
        self.outs = [jax.ShapeDtypeStruct((N_CHIPS,) + shard.shape, shard.dtype)]
        self.rows = shard.shape[0]
        self.halve = shard.dtype == BF16 and self.rows % (2 * BF16_ROWS) == 0
        self.n_sem = 13 if self.halve else 7
        self.result = None

    def _copies(self, ins, outs, sems, s0, pos):
        x, y, c, chips = pos
        src, dst = ins[0], outs[0]
        me = 2 * x + y
        if self.halve:
            hr = self.rows // 2
            mine, theirs = pl.ds(c * hr, hr), pl.ds((1 - c) * hr, hr)
            part = lambda slot, rows: dst.at[slot, rows]
            my_src = src.at[mine]
        else:
            mine = theirs = None
            part = lambda slot, rows: dst.at[slot]
            my_src = src
        slot = lambda j: 2 * chips[j][0] + chips[j][1]
        local = lambda: pltpu.make_async_copy(src, dst.at[me], sems.at[s0])
        send = lambda j: _remote(my_src, part(me, mine), sems, s0 + 1 + j, s0 + 4 + j, (*chips[j], c))
        land = lambda j: _remote(my_src, part(slot(j), mine), sems, s0 + 1 + j, s0 + 4 + j, (*chips[j], c))
        fwd = lambda j: _remote(part(slot(j), mine), part(slot(j), mine), sems, s0 + 7 + j, s0 + 10 + j, (x, y, 1 - c))
        got = lambda j: _remote(part(slot(j), theirs), part(slot(j), theirs), sems, s0 + 7 + j, s0 + 10 + j,
                                (x, y, 1 - c))
        return local, send, land, fwd, got

    def start(self, ins, outs, sems, s0, pos):
        local, send, _, _, _ = self._copies(ins, outs, sems, s0, pos)
        local().start()
        for j in range(N_CHIPS - 1):
            send(j).start()

    def finish(self, ins, outs, sems, s0, pos):
        local, send, land, fwd, got = self._copies(ins, outs, sems, s0, pos)
        others = range(N_CHIPS - 1)
        for j in others:
            land(j).wait_recv()
            if self.halve:
                fwd(j).start()
        for j in others:
            if self.halve:
                got(j).wait_recv()
        for j in others:
            send(j).wait_send()
            if self.halve:
                fwd(j).wait_send()
        local().wait()


class _Scatter:
    def __init__(self, grad):
        self.ins = [grad]
        self.outs = [jax.ShapeDtypeStruct(grad.shape, grad.dtype)]
        self.n_sem = 7
        self.result = None

    def _copies(self, ins, outs, sems, s0, pos):
        x, y, c, chips = pos
        src, dst = ins[0], outs[0]
        me = 2 * x + y
        slot = lambda j: 2 * chips[j][0] + chips[j][1]
        local = lambda: pltpu.make_async_copy(src.at[me], dst.at[me], sems.at[s0])
        send = lambda j: _remote(src.at[slot(j)], dst.at[me], sems, s0 + 1 + j, s0 + 4 + j, (*chips[j], c))
        land = lambda j: _remote(src.at[me], dst.at[slot(j)], sems, s0 + 1 + j, s0 + 4 + j, (*chips[j], c))
        return local, send, land

    def start(self, ins, outs, sems, s0, pos):
        local, send, _ = self._copies(ins, outs, sems, s0, pos)
        local().start()
        for j in range(N_CHIPS - 1):
            send(j).start()

    def finish(self, ins, outs, sems, s0, pos):
        local, send, land = self._copies(ins, outs, sems, s0, pos)
        for j in range(N_CHIPS - 1):
            land(j).wait_recv()
        for j in range(N_CHIPS - 1):
            send(j).wait_send()
        local().wait()


class _Swap:
    def __init__(self, part):
        self.ins = [part]
        self.outs = [jax.ShapeDtypeStruct(part.shape, part.dtype)]
        self.n_sem = 2
        self.result = None

    def _copy(self, ins, outs, sems, s0, pos):
        x, y, c, _ = pos
        return _remote(ins[0], outs[0], sems, s0, s0 + 1, (x, y, 1 - c))

    def start(self, ins, outs, sems, s0, pos):
        self._copy(ins, outs, sems, s0, pos).start()

    def finish(self, ins, outs, sems, s0, pos):
        self._copy(ins, outs, sems, s0, pos).wait()


class _SwapHalf:
    def __init__(self, grad, key):
        slots, rows, cols = grad.shape
        half = jax.ShapeDtypeStruct((slots, rows // 2, cols), grad.dtype)
        self.ins, self.outs, self.key = [grad], [half, half], key
        self.hr = rows // 2
        self.n_sem = 3
        self.result = None

    def _copies(self, ins, outs, sems, s0, pos):
        x, y, c, _ = pos
        mine, theirs = pl.ds(c * self.hr, self.hr), pl.ds((1 - c) * self.hr, self.hr)
        local = pltpu.make_async_copy(ins[0].at[:, mine], outs[0], sems.at[s0])
        remote = _remote(ins[0].at[:, theirs], outs[1], sems, s0 + 1, s0 + 2, (x, y, 1 - c))
        return local, remote

    def start(self, ins, outs, sems, s0, pos):
        for cp in self._copies(ins, outs, sems, s0, pos):
            cp.start()

    def finish(self, ins, outs, sems, s0, pos):
        for cp in self._copies(ins, outs, sems, s0, pos):
            cp.wait()


class _SwapBack:
    def __init__(self, part):
        hr, cols = part.shape
        self.ins, self.outs = [part], [jax.ShapeDtypeStruct((2 * hr, cols), part.dtype)]
        self.hr = hr
        self.n_sem = 3
        self.result = None

    def _copies(self, ins, outs, sems, s0, pos):
        x, y, c, _ = pos
        mine, theirs = pl.ds(c * self.hr, self.hr), pl.ds((1 - c) * self.hr, self.hr)
        local = lambda: pltpu.make_async_copy(ins[0], outs[0].at[mine], sems.at[s0])
        send = lambda: _remote(ins[0], outs[0].at[mine], sems, s0 + 1, s0 + 2, (x, y, 1 - c))
        land = lambda: _remote(ins[0], outs[0].at[theirs], sems, s0 + 1, s0 + 2, (x, y, 1 - c))
        return local, send, land

    def start(self, ins, outs, sems, s0, pos):
        local, send, _ = self._copies(ins, outs, sems, s0, pos)
        local().start()
        send().start()

    def finish(self, ins, outs, sems, s0, pos):
        local, send, land = self._copies(ins, outs, sems, s0, pos)
        land().wait_recv()
        send().wait_send()
        local().wait()


def _pallas(body, args, *, name, grid, in_specs, out_specs, out_shape, scratch_shapes=(), comm=()):
    comm = list(comm)
    n_in, n_out, n_scr = len(in_specs), len(out_specs), len(scratch_shapes)
    c_in = [a for op in comm for a in op.ins]
    c_out = [s for op in comm for s in op.outs]
    n_sem = sum(op.n_sem for op in comm)

    def full(*refs):
        o0 = n_in + len(c_in)
        s0 = o0 + n_out + len(c_out)
        ins, cin = refs[:n_in], refs[n_in:o0]
        outs, cout = refs[o0:o0 + n_out], refs[o0 + n_out:s0]
        scratch = refs[s0:s0 + n_scr]
        if comm:
            sems = refs[s0 + n_scr]
            ids = [pl.program_id(d) for d in range(len(grid))]
            first = functools.reduce(jnp.logical_and, [i == 0 for i in ids])
            last = functools.reduce(jnp.logical_and, [i == g - 1 for i, g in zip(ids, grid)])
            pos = _position()

            def each(step):
                ci = co = cs = 0
                for op in comm:
                    getattr(op, step)(cin[ci:ci + len(op.ins)], cout[co:co + len(op.outs)], sems, cs, pos)
                    ci, co, cs = ci + len(op.ins), co + len(op.outs), cs + op.n_sem

            @pl.when(first)
            def _():
                each("start")

        body(*ins, *outs, *scratch)
        if comm:
            @pl.when(last)
            def _():
                each("finish")

    hbm = pl.BlockSpec(memory_space=pl.ANY)
    res = pl.pallas_call(
        full, name=name, grid=grid,
        in_specs=list(in_specs) + [hbm] * len(c_in), out_specs=list(out_specs) + [hbm] * len(c_out),
        out_shape=list(out_shape) + c_out,
        scratch_shapes=list(scratch_shapes) + ([pltpu.SemaphoreType.DMA((n_sem,))] if comm else []),
        compiler_params=pltpu.CompilerParams(dimension_semantics=("arbitrary",) * len(grid),
                                             vmem_limit_bytes=VMEM_LIMIT))(*args, *c_in)
    k = n_out
    for op in comm:
        op.result = list(res[k:k + len(op.outs)])
        k += len(op.outs)
    return list(res[:n_out])


def _row_tile(rows, cols, itemsize=4, budget=1 << 20):
    t = rows
    while t % (2 * BF16_ROWS) == 0 and t * cols * itemsize > budget:
        t //= 2
    return t


def _cast_bf16(w, name):
    rows, cols = w.shape
    tr = _row_tile(rows, cols)

    def body(w_ref, o_ref):
        o_ref[...] = w_ref[...].astype(BF16)

    spec = pl.BlockSpec((tr, cols), lambda i: (i, 0))
    return _pallas(body, [w], name=name, grid=(rows // tr,), in_specs=[spec], out_specs=[spec],
                   out_shape=[jax.ShapeDtypeStruct((rows, cols), BF16)])[0]


def _rms_fwd(x, g, name, comm=()):
    L, D = x.shape

    def body(x_ref, g_ref, h_ref):
        xf = x_ref[...]
        r = lax.rsqrt(jnp.mean(xf * xf, axis=-1, keepdims=True) + EPS)
        h_ref[...] = (xf * r * g_ref[...]).astype(BF16)

    row = pl.BlockSpec((TMS, D), lambda i: (i, 0))
    return _pallas(body, [x, g], name=name, grid=(L // TMS,),
                   in_specs=[row, pl.BlockSpec((1, D), lambda i: (0, 0))], out_specs=[row],
                   out_shape=[jax.ShapeDtypeStruct((L, D), BF16)], comm=comm)[0]


def _resident(shape):
    return pl.BlockSpec(shape, lambda *_: (0,) * len(shape), pipeline_mode=pl.Buffered(1))


def _ffn_up(h, wg_t, wu_t, name, comm=()):
    L, D = h.shape
    G, FS, _ = wg_t.shape

    def body(h_ref, wg_ref, wu_ref, a_ref, b_ref, act_ref):
        j = pl.program_id(1)
        hv = h_ref[...]
        a = _dot_nt(hv, wg_ref[j])
        b = _dot_nt(hv, wu_ref[j])
        a_ref[...] = a.astype(BF16)
        b_ref[...] = b.astype(BF16)
        act_ref[...] = (a * _sigmoid(a) * b).astype(BF16)

    ospec = pl.BlockSpec((None, TMS, FS), lambda i, j: (j, i, 0))
    oshape = jax.ShapeDtypeStruct((G, L, FS), BF16)
    return _pallas(body, [h, wg_t, wu_t], name=name, grid=(L // TMS, G),
                   in_specs=[pl.BlockSpec((TMS, D), lambda i, j: (i, 0)), _resident((G, FS, D)), _resident((G, FS, D))],
                   out_specs=[ospec, ospec, ospec], out_shape=[oshape, oshape, oshape], comm=comm)


def _group_sum(a_ref, w_ref, groups, mm=_dot):
    acc = mm(a_ref[0], w_ref[0])
    for j in range(1, groups):
        acc = acc + mm(a_ref[j], w_ref[j])
    return acc


def _ffn_down(act, wd, x, g_next, name, comm=()):
    G, L, FS = act.shape
    D = wd.shape[2]

    def body(act_ref, wd_ref, x_ref, g_ref, xn_ref, hn_ref):
        xn = x_ref[...] + 0.5 * _group_sum(act_ref, wd_ref, G)
        xn_ref[...] = xn
        r = lax.rsqrt(jnp.mean(xn * xn, axis=-1, keepdims=True) + EPS)
        hn_ref[...] = (xn * r * g_ref[...]).astype(BF16)

    row = pl.BlockSpec((TM, D), lambda i: (i, 0))
    return _pallas(body, [act, wd, x, g_next], name=name, grid=(L // TM,),
                   in_specs=[pl.BlockSpec((G, TM, FS), lambda i: (0, i, 0)), _resident((G, FS, D)), row,
                             pl.BlockSpec((1, D), lambda i: (0, 0))],
                   out_specs=[row, row],
                   out_shape=[jax.ShapeDtypeStruct((L, D), F32), jax.ShapeDtypeStruct((L, D), BF16)], comm=comm)


def _ffn_down_loss(act, wd, x, gf, tgt, name):
    G, L, FS = act.shape
    D = wd.shape[2]

    def body(act_ref, wd_ref, x_ref, g_ref, t_ref, dx_ref, dxb_ref, loss_ref, dg_ref):
        i = pl.program_id(0)
        xn = x_ref[...] + 0.5 * _group_sum(act_ref, wd_ref, G)
        r = lax.rsqrt(jnp.mean(xn * xn, axis=-1, keepdims=True) + EPS)
        xh = xn * r
        gv = g_ref[...]
        e = xh * gv - t_ref[...]
        part = 0.5 * jnp.sum(_colsum(e * e), axis=1, keepdims=True) / D
        dy = e / D
        _accumulate(loss_ref, i == 0, jnp.broadcast_to(part, (1, LANES)))
        _accumulate(dg_ref, i == 0, _colsum(dy * xh))
        dxh = dy * gv
        dx = r * (dxh - xh * jnp.mean(dxh * xh, axis=-1, keepdims=True))
        dx_ref[...] = dx
        dxb_ref[...] = dx.astype(BF16)

    row = pl.BlockSpec((TM, D), lambda i: (i, 0))
    return _pallas(body, [act, wd, x, gf, tgt], name=name, grid=(L // TM,),
                   in_specs=[pl.BlockSpec((G, TM, FS), lambda i: (0, i, 0)), _resident((G, FS, D)), row,
                             pl.BlockSpec((1, D), lambda i: (0, 0)), row],
                   out_specs=[row, row, pl.BlockSpec((1, LANES), lambda i: (0, 0)),
                              pl.BlockSpec((1, D), lambda i: (0, 0))],
                   out_shape=[jax.ShapeDtypeStruct((L, D), F32), jax.ShapeDtypeStruct((L, D), BF16),
                              jax.ShapeDtypeStruct((1, LANES), F32), jax.ShapeDtypeStruct((1, D), F32)])


def _ffn_bwd_act(dxb, wd, a, b, name, comm=()):
    L, D = dxb.shape
    G, FS, _ = wd.shape

    def body(dx_ref, wd_ref, a_ref, b_ref, da_ref, db_ref):
        dact = 0.5 * _dot_nt(dx_ref[...], wd_ref[pl.program_id(1)])
        av = a_ref[...].astype(F32)
        bv = b_ref[...].astype(F32)
        sg = _sigmoid(av)
        da_ref[...] = (dact * bv * sg * (1.0 + av * (1.0 - sg))).astype(BF16)
        db_ref[...] = (dact * av * sg).astype(BF16)

    gspec = pl.BlockSpec((None, TMS, FS), lambda i, j: (j, i, 0))
    oshape = jax.ShapeDtypeStruct((G, L, FS), BF16)
    return _pallas(body, [dxb, wd, a, b], name=name, grid=(L // TMS, G),
                   in_specs=[pl.BlockSpec((TMS, D), lambda i, j: (i, 0)), _resident((G, FS, D)), gspec, gspec],
                   out_specs=[gspec, gspec], out_shape=[oshape, oshape], comm=comm)


def _mm_grouped(a, w, name):
    L, K = a.shape
    G, _, N = w.shape

    def body(a_ref, w_ref, o_ref):
        o_ref[...] = _dot(a_ref[...], w_ref[pl.program_id(1)])

    return _pallas(body, [a, w], name=name, grid=(L // TMS, G),
                   in_specs=[pl.BlockSpec((TMS, K), lambda i, g: (i, 0)), _resident((G, K, N))],
                   out_specs=[pl.BlockSpec((TMS, N), lambda i, g: (i, g))],
                   out_shape=[jax.ShapeDtypeStruct((L, G * N), F32)])[0]


def _mm_nt(a, w, name):
    L, K = a.shape
    N = w.shape[0]

    def body(a_ref, w_ref, o_ref):
        o_ref[...] = _dot_nt(a_ref[...], w_ref[...]).astype(BF16)

    return _pallas(body, [a, w], name=name, grid=(L // TMS,),
                   in_specs=[pl.BlockSpec((TMS, K), lambda i: (i, 0)), _resident((N, K))],
                   out_specs=[pl.BlockSpec((TMS, N), lambda i: (i, 0))],
                   out_shape=[jax.ShapeDtypeStruct((L, N), BF16)])[0]


def _mm_tn(a, b, scale, name, groups, b_cols=False, comm=()):
    L, M = a.shape[1], a.shape[2]
    N = b.shape[1] // groups if b_cols else b.shape[2]
    tk = min(L, TK)
    nk = L // tk

    def spec(v, cols):
        if cols:
            return pl.BlockSpec((tk, v.shape[1] // groups), lambda g, k: (k, g))
        if v.shape[0] > 1:
            return pl.BlockSpec((None, tk, v.shape[2]), lambda g, k: (g, k, 0))
        return pl.BlockSpec((None, tk, v.shape[2]), lambda g, k: (0, k, 0))

    def body(a_ref, b_ref, o_ref, acc):
        k = pl.program_id(1)
        p = _dot_tn(a_ref[...], b_ref[...])
        if nk == 1:
            o_ref[...] = (p * scale).astype(BF16)
        else:
            _accumulate(acc, k == 0, p)

            @pl.when(k == nk - 1)
            def _():
                o_ref[...] = (acc[...] * scale).astype(BF16)

    return _pallas(body, [a, b], name=name, grid=(groups, nk),
                   in_specs=[spec(a, False), spec(b, b_cols)],
                   out_specs=[pl.BlockSpec((None, M, N), lambda g, k: (g, 0, 0))],
                   out_shape=[jax.ShapeDtypeStruct((groups, M, N), BF16)],
                   scratch_shapes=[pltpu.VMEM((M, N), F32)], comm=comm)[0]


def _mm_rmsbwd(a_list, w_list, nt, a_cols, x_in, g, dx_out, name, comm=()):
    P = len(a_list)
    G = w_list[0].shape[0]
    L, D = x_in.shape
    mm = _dot_nt if nt else _dot

    def body(*refs):
        a_refs, w_refs = refs[:P], refs[P:2 * P]
        x_ref, g_ref, dxo_ref, dx_ref, dxb_ref, dg_ref = refs[2 * P:]
        i = pl.program_id(0)
        dh = None
        for a_ref, w_ref in zip(a_refs, w_refs):
            for j in range(G):
                if a_cols:
                    kw = a_ref.shape[1] // G
                    term = mm(a_ref[:, j * kw:(j + 1) * kw], w_ref[j])
                else:
                    term = mm(a_ref[j], w_ref[j])
                dh = term if dh is None else dh + term
        xf = x_ref[...]
        r = lax.rsqrt(jnp.mean(xf * xf, axis=-1, keepdims=True) + EPS)
        xh = xf * r
        _accumulate(dg_ref, i == 0, _colsum(dh * xh))
        dxh = dh * g_ref[...]
        dx = dxo_ref[...] + r * (dxh - xh * jnp.mean(dxh * xh, axis=-1, keepdims=True))
        dx_ref[...] = dx
        dxb_ref[...] = dx.astype(BF16)

    row = pl.BlockSpec((TM, D), lambda i: (i, 0))
    vec = pl.BlockSpec((1, D), lambda i: (0, 0))
    if a_cols:
        a_specs = [pl.BlockSpec((TM, a.shape[1]), lambda i: (i, 0)) for a in a_list]
    else:
        a_specs = [pl.BlockSpec((G, TM, a.shape[2]), lambda i: (0, i, 0)) for a in a_list]
    w_specs = [_resident(w.shape) for w in w_list]
    return _pallas(body, [*a_list, *w_list, x_in, g, dx_out], name=name, grid=(L // TM,),
                   in_specs=a_specs + w_specs + [row, vec, row], out_specs=[row, row, vec],
                   out_shape=[jax.ShapeDtypeStruct((L, D), F32), jax.ShapeDtypeStruct((L, D), BF16),
                              jax.ShapeDtypeStruct((1, D), F32)], comm=comm)


def _mix_out(cat, wout, x1, g_next, name):
    L, K = cat.shape
    D = wout.shape[1]

    def body(c_ref, w_ref, x_ref, g_ref, xn_ref, hn_ref):
        xn = x_ref[...] + _dot(c_ref[...], w_ref[...])
        xn_ref[...] = xn
        r = lax.rsqrt(jnp.mean(xn * xn, axis=-1, keepdims=True) + EPS)
        hn_ref[...] = (xn * r * g_ref[...]).astype(BF16)

    row = pl.BlockSpec((TMS, D), lambda i: (i, 0))
    return _pallas(body, [cat, wout, x1, g_next], name=name, grid=(L // TMS,),
                   in_specs=[pl.BlockSpec((TMS, K), lambda i: (i, 0)), pl.BlockSpec((K, D), lambda i: (0, 0)), row,
                             pl.BlockSpec((1, D), lambda i: (0, 0))],
                   out_specs=[row, row],
                   out_shape=[jax.ShapeDtypeStruct((L, D), F32), jax.ShapeDtypeStruct((L, D), BF16)])


def _s5_disc(lr, li, ldt, brc, bic):
    dt = jnp.exp(ldt)
    mag = jnp.exp(lr * dt)
    are = mag * jnp.cos(li * dt)
    aim = mag * jnp.sin(li * dt)
    den = lr * lr + li * li
    nre = are - 1.0
    fre = (nre * lr + aim * li) / den
    fim = (aim * lr - nre * li) / den
    return are, aim, fre * brc - fim * bic, fre * bic + fim * brc


def _s5_params_fwd(lr, li, ldt, brc, bic, crc, cic):
    def body(lr_ref, li_ref, ldt_ref, br_ref, bi_ref, cr_ref, ci_ref, are_ref, aim_ref, bre_ref, bim_ref, cre_ref, cim_ref):
        are, aim, bre, bim = _s5_disc(lr_ref[...], li_ref[...], ldt_ref[...], br_ref[...], bi_ref[...])
        are_ref[...] = are
        aim_ref[...] = aim
        bre_ref[...] = bre.astype(BF16)
        bim_ref[...] = bim.astype(BF16)
        cre_ref[...] = cr_ref[...].astype(BF16)
        cim_ref[...] = ci_ref[...].astype(BF16)

    vec = jax.ShapeDtypeStruct((1, S5_N), F32)
    return pl.pallas_call(
        body, name="s5_params_fwd",
        out_shape=[vec, vec, jax.ShapeDtypeStruct((LANES, S5_N), BF16), jax.ShapeDtypeStruct((LANES, S5_N), BF16),
                   jax.ShapeDtypeStruct((S5_N, LANES), BF16), jax.ShapeDtypeStruct((S5_N, LANES), BF16)],
        compiler_params=pltpu.CompilerParams(vmem_limit_bytes=VMEM_LIMIT))(lr, li, ldt, brc, bic, crc, cic)


def _s5_params_bwd(lr, li, ldt, brc, bic, dare, daim, dbre, dbim):
    def body(lr_ref, li_ref, ldt_ref, br_ref, bi_ref, dare_ref, daim_ref, dbre_ref, dbim_ref,
             glr_ref, gli_ref, gldt_ref, gbr_ref, gbi_ref):
        _, vjp = jax.vjp(_s5_disc, lr_ref[...], li_ref[...], ldt_ref[...], br_ref[...], bi_ref[...])
        glr, gli, gldt, gbr, gbi = vjp((dare_ref[...], daim_ref[...], dbre_ref[...], dbim_ref[...]))
        glr_ref[...] = glr
        gli_ref[...] = gli
        gldt_ref[...] = gldt
        gbr_ref[...] = gbr
        gbi_ref[...] = gbi

    vec = jax.ShapeDtypeStruct((1, S5_N), F32)
    mat = jax.ShapeDtypeStruct((LANES, S5_N), F32)
    return pl.pallas_call(
        body, name="s5_params_bwd", out_shape=[vec, vec, vec, mat, mat],
        compiler_params=pltpu.CompilerParams(vmem_limit_bytes=VMEM_LIMIT))(lr, li, ldt, brc, bic, dare, daim, dbre, dbim)


def _gelu_parts(y):
    th = jnp.tanh(GELU_C0 * (y + GELU_C1 * y * y * y))
    return 0.5 * y * (1.0 + th), th


def _state_rows(q, T):
    return pl.ds(q % S5_HALF, T, stride=S5_HALF)


def _s5_fwd(u, are, aim, bre, bim, cre, cim, d_skip, wglu, bglu, comm=()):
    L = u.shape[0]
    T = min(TS, L)
    n = L // T

    def body(u_ref, are_ref, aim_ref, bre_ref, bim_ref, cre_ref, cim_ref, d_ref, wg_ref, bg_ref,
             sre_lo, sre_hi, sim_lo, sim_hi, y_ref, o_ref, st_re, st_im):
        i = pl.program_id(0)
        sre, sim = (sre_lo, sre_hi), (sim_lo, sim_hi)

        @pl.when(i == 0)
        def _():
            st_re[...] = jnp.zeros_like(st_re)
            st_im[...] = jnp.zeros_like(st_im)

        uf = u_ref[...]
        ub = uf.astype(BF16)
        for q in range(S5_TILES):
            ct = q // 4
            uq = ub[:, ct * LANES:(ct + 1) * LANES]
            sre[q // S5_HALF][_state_rows(q, T), :] = _dot(uq, bre_ref[:, q * LANES:(q + 1) * LANES])
            sim[q // S5_HALF][_state_rows(q, T), :] = _dot(uq, bim_ref[:, q * LANES:(q + 1) * LANES])
        halves = [slice(h * S5_HALF, (h + 1) * S5_HALF) for h in range(2)]
        a_re = [are_ref[hs, :] for hs in halves]
        a_im = [aim_ref[hs, :] for hs in halves]

        def step(t, carry):
            rows = pl.ds(pl.multiple_of(t * S5_HALF, S5_HALF), S5_HALF)
            out = []
            for h in range(2):
                s_re, s_im = carry[2 * h], carry[2 * h + 1]
                n_re = a_re[h] * s_re - a_im[h] * s_im + sre[h][rows, :]
                n_im = a_re[h] * s_im + a_im[h] * s_re + sim[h][rows, :]
                sre[h][rows, :] = n_re
                sim[h][rows, :] = n_im
                out += [n_re, n_im]
            return tuple(out)

        init = (st_re[halves[0], :], st_im[halves[0], :], st_re[halves[1], :], st_im[halves[1], :])
        fin = lax.fori_loop(0, T, step, init, unroll=4)
        for h in range(2):
            st_re[halves[h], :] = fin[2 * h]
            st_im[halves[h], :] = fin[2 * h + 1]
        tiles = []
        for ct in range(4):
            acc = jnp.zeros((T, LANES), F32)
            for q in range(4 * ct, 4 * ct + 4):
                win = slice(q * LANES, (q + 1) * LANES)
                acc = acc + _dot(sre[q // S5_HALF][_state_rows(q, T), :].astype(BF16), cre_ref[win, :])
                acc = acc - _dot(sim[q // S5_HALF][_state_rows(q, T), :].astype(BF16), cim_ref[win, :])
            tiles.append(acc)
        y = jnp.concatenate(tiles, axis=1) + d_ref[...] * uf
        y_ref[...] = y
        yg, _ = _gelu_parts(y)
        gate = _sigmoid(_dot(yg.astype(BF16), wg_ref[...]) + bg_ref[...])
        o_ref[...] = (yg * gate).astype(BF16)

    const = lambda shape: pl.BlockSpec(shape, lambda i: (0, 0))
    sspec = pl.BlockSpec((T * S5_HALF, LANES), lambda i: (i, 0))
    sshape = jax.ShapeDtypeStruct((L * S5_HALF, LANES), F32)
    chunk = pl.BlockSpec((T, S5_W), lambda i: (i, 0))
    return _pallas(body, [u, are, aim, bre, bim, cre, cim, d_skip, wglu, bglu], name="s5_fwd", grid=(n,),
                   in_specs=[chunk, const((S5_TILES, LANES)), const((S5_TILES, LANES)),
                             const((LANES, S5_N)), const((LANES, S5_N)), const((S5_N, LANES)), const((S5_N, LANES)),
                             const((1, S5_W)), const((S5_W, S5_W)), const((1, S5_W))],
                   out_specs=[sspec] * 4 + [chunk, chunk],
                   out_shape=[sshape] * 4 + [jax.ShapeDtypeStruct((L, S5_W), F32), jax.ShapeDtypeStruct((L, S5_W), BF16)],
                   scratch_shapes=[pltpu.VMEM((S5_TILES, LANES), F32), pltpu.VMEM((S5_TILES, LANES), F32)], comm=comm)


def _s5_bwd(dm, y_pre, u, states, are, aim, bre, bim, cre, cim, d_skip, wglu, bglu, comm=()):
    L = u.shape[0]
    T = min(TS, L)
    n = L // T

    def body(dm_ref, y_ref, u_ref, sre_lo, sre_hi, sim_lo, sim_hi, pre_lo, pre_hi, pim_lo, pim_hi,
             are_ref, aim_ref, bre_ref, bim_ref, cre_ref, cim_ref, d_ref, wg_ref, bg_ref,
             du_ref, dwg_ref, dbg_ref, dd_ref, dcre_ref, dcim_ref, dbre_ref, dbim_ref, dare_ref, daim_ref,
             gre_lo, gre_hi, gim_lo, gim_hi, car_re, car_im):
        i = pl.program_id(0)
        first = i == 0
        sre, sim = (sre_lo, sre_hi), (sim_lo, sim_hi)
        gre, gim = (gre_lo, gre_hi), (gim_lo, gim_hi)
        pre, pim = (pre_lo, pre_hi), (pim_lo, pim_hi)

        @pl.when(first)
        def _():
            car_re[...] = jnp.zeros_like(car_re)
            car_im[...] = jnp.zeros_like(car_im)
            dcre_ref[...] = jnp.zeros_like(dcre_ref)
            dcim_ref[...] = jnp.zeros_like(dcim_ref)
            dbre_ref[...] = jnp.zeros_like(dbre_ref)
            dbim_ref[...] = jnp.zeros_like(dbim_ref)

        y = y_ref[...]
        uf = u_ref[...]
        yg, th = _gelu_parts(y)
        dgelu = 0.5 * (1.0 + th) + 0.5 * y * (1.0 - th * th) * GELU_C0 * (1.0 + 3.0 * GELU_C1 * y * y)
        ygb = yg.astype(BF16)
        sg = _sigmoid(_dot(ygb, wg_ref[...]) + bg_ref[...])
        dout = dm_ref[...].astype(F32)
        dgp = dout * yg * sg * (1.0 - sg)
        dgpb = dgp.astype(BF16)
        dyg = dout * sg + _dot_nt(dgpb, wg_ref[...])
        _accumulate(dwg_ref, first, _dot_tn(ygb, dgpb))
        _accumulate(dbg_ref, first, _colsum(dgp))
        dy = dyg * dgelu
        _accumulate(dd_ref, first, _colsum(dy * uf))
        dyb = dy.astype(BF16)
        ub = uf.astype(BF16)

        for q in range(S5_TILES):
            ct, h = q // 4, q // S5_HALF
            win = slice(q * LANES, (q + 1) * LANES)
            dyq = dyb[:, ct * LANES:(ct + 1) * LANES]
            dcre_ref[win, :] += _dot_tn(sre[h][_state_rows(q, T), :].astype(BF16), dyq)
            dcim_ref[win, :] -= _dot_tn(sim[h][_state_rows(q, T), :].astype(BF16), dyq)
            gre[h][_state_rows(q, T), :] = _dot_nt(dyq, cre_ref[win, :])
            gim[h][_state_rows(q, T), :] = -_dot_nt(dyq, cim_ref[win, :])

        halves = [slice(h * S5_HALF, (h + 1) * S5_HALF) for h in range(2)]
        a_re = [are_ref[hs, :] for hs in halves]
        a_im = [aim_ref[hs, :] for hs in halves]

        def adjoint(t, h, g_re, g_im):
            rows = pl.ds(pl.multiple_of(t * S5_HALF, S5_HALF), S5_HALF)
            n_re = gre[h][rows, :] + a_re[h] * g_re + a_im[h] * g_im
            n_im = gim[h][rows, :] + a_re[h] * g_im - a_im[h] * g_re
            gre[h][rows, :] = n_re
            gim[h][rows, :] = n_im
            return n_re, n_im

        def step(k, carry):
            t = T - 1 - k
            prev = pl.ds(pl.multiple_of((t - 1) * S5_HALF, S5_HALF), S5_HALF)
            out = []
            for h in range(2):
                g_re, g_im, da_re, da_im = carry[4 * h:4 * h + 4]
                g_re, g_im = adjoint(t, h, g_re, g_im)
                p_re, p_im = sre[h][prev, :], sim[h][prev, :]
                out += [g_re, g_im, da_re + g_re * p_re + g_im * p_im, da_im + g_im * p_re - g_re * p_im]
            return tuple(out)

        zero = jnp.zeros((S5_HALF, LANES), F32)
        init = (car_re[halves[0], :], car_im[halves[0], :], zero, zero, car_re[halves[1], :], car_im[halves[1], :], zero, zero)
        fin = lax.fori_loop(0, T - 1, step, init, unroll=4)
        keep = (i < n - 1).astype(F32)
        for h in range(2):
            g_re, g_im, da_re, da_im = fin[4 * h:4 * h + 4]
            g_re, g_im = adjoint(0, h, g_re, g_im)
            p_re, p_im = pre[h][...] * keep, pim[h][...] * keep
            car_re[halves[h], :] = g_re
            car_im[halves[h], :] = g_im
            da_re = da_re + g_re * p_re + g_im * p_im
            da_im = da_im + g_im * p_re - g_re * p_im

            @pl.when(first)
            def _():
                dare_ref[halves[h], :] = da_re
                daim_ref[halves[h], :] = da_im

            @pl.when(jnp.logical_not(first))
            def _():
                dare_ref[halves[h], :] += da_re
                daim_ref[halves[h], :] += da_im

        tiles = []
        for ct in range(4):
            uq = ub[:, ct * LANES:(ct + 1) * LANES]
            acc = d_ref[:, ct * LANES:(ct + 1) * LANES] * dy[:, ct * LANES:(ct + 1) * LANES]
            for q in range(4 * ct, 4 * ct + 4):
                h = q // S5_HALF
                win = slice(q * LANES, (q + 1) * LANES)
                gq_re = gre[h][_state_rows(q, T), :].astype(BF16)
                gq_im = gim[h][_state_rows(q, T), :].astype(BF16)
                acc = acc + _dot_nt(gq_re, bre_ref[:, win]) + _dot_nt(gq_im, bim_ref[:, win])
                dbre_ref[:, win] += _dot_tn(uq, gq_re)
                dbim_ref[:, win] += _dot_tn(uq, gq_im)
            tiles.append(acc)
        du_ref[...] = jnp.concatenate(tiles, axis=1).astype(BF16)

    rev = lambda i: (n - 1 - i, 0)
    const = lambda shape: pl.BlockSpec(shape, lambda i: (0, 0))
    chunk = pl.BlockSpec((T, S5_W), rev)
    sspec = pl.BlockSpec((T * S5_HALF, LANES), rev)
    pspec = pl.BlockSpec((S5_HALF, LANES), lambda i: (jnp.maximum((n - 1 - i) * T - 1, 0), 0))
    tile = jax.ShapeDtypeStruct((S5_TILES, LANES), F32)
    vec = jax.ShapeDtypeStruct((1, S5_W), F32)
    sbuf = pltpu.VMEM((T * S5_HALF, LANES), F32)
    return _pallas(
        body, [dm, y_pre, u, *states, *states, are, aim, bre, bim, cre, cim, d_skip, wglu, bglu],
        name="s5_bwd", grid=(n,),
        in_specs=[chunk, chunk, chunk] + [sspec] * 4 + [pspec] * 4 + [
            const((S5_TILES, LANES)), const((S5_TILES, LANES)), const((LANES, S5_N)), const((LANES, S5_N)),
            const((S5_N, LANES)), const((S5_N, LANES)), const((1, S5_W)), const((S5_W, S5_W)), const((1, S5_W))],
        out_specs=[chunk, const((S5_W, S5_W)), const((1, S5_W)), const((1, S5_W)), const((S5_N, LANES)),
                   const((S5_N, LANES)), const((LANES, S5_N)), const((LANES, S5_N)), const((S5_TILES, LANES)),
                   const((S5_TILES, LANES))],
        out_shape=[jax.ShapeDtypeStruct((L, S5_W), BF16), jax.ShapeDtypeStruct((S5_W, S5_W), F32), vec, vec,
                   jax.ShapeDtypeStruct((S5_N, LANES), F32), jax.ShapeDtypeStruct((S5_N, LANES), F32),
                   jax.ShapeDtypeStruct((LANES, S5_N), F32), jax.ShapeDtypeStruct((LANES, S5_N), F32), tile, tile],
        scratch_shapes=[sbuf, sbuf, sbuf, sbuf, pltpu.VMEM((S5_TILES, LANES), F32), pltpu.VMEM((S5_TILES, LANES), F32)],
        comm=comm)


_EYE8 = np.eye(8, dtype=np.float32)


def _compact_b(b):
    return jnp.einsum("akpc,kj->jcakp", b.reshape(4, 8, S5_P, S5_GC), _EYE8).reshape(LANES, S5_N)


def _uncompact_b(m):
    return jnp.einsum("kcakp->akpc", m.reshape(8, S5_GC, 4, 8, S5_P)).reshape(S5_G, S5_P, S5_GC)


_HEAD_MEAN = np.kron(np.eye(CONV_W // CONV_HD, dtype=np.float32), np.full((CONV_HD, CONV_HD), 1.0 / CONV_HD, np.float32))


def _head_mean(v, m):
    hi = v.astype(BF16)
    lo = (v - hi.astype(F32)).astype(BF16)
    return _dot(hi, m) + _dot(lo, m)


def _head_norm(zc, m):
    d = zc - _head_mean(zc, m)
    rstd = lax.rsqrt(_head_mean(d * d, m) + EPS)
    return d * rstd, rstd


def _conv_fwd(u, o_s5, wdw, bdw, lng, lnb, mavg, comm=()):
    L = u.shape[0]
    T = min(TS, L)
    n = L // T
    first_tap = HALO - (CONV_K - 1)

    def body(v1_ref, v2_ref, s5_ref, w_ref, b_ref, g_ref, be_ref, m_ref, zc_ref, o_ref, zbuf):
        i = pl.program_id(0)

        @pl.when(i == 0)
        def _():
            zbuf[0:HALO, :] = jnp.zeros((HALO, CONV_W), F32)

        zbuf[HALO:HALO + T, :] = v1_ref[...] * _sigmoid(v2_ref[...])
        for r0 in range(0, T, CONV_SB):
            acc = jnp.broadcast_to(b_ref[...], (CONV_SB, CONV_W))
            for k in range(CONV_K):
                acc = acc + w_ref[k:k + 1, :] * zbuf[pl.ds(r0 + first_tap + k, CONV_SB), :]
            zc_ref[r0:r0 + CONV_SB, :] = acc
        zbuf[0:HALO, :] = zbuf[T:T + HALO, :]
        zn, _ = _head_norm(zc_ref[...], m_ref[...])
        zz = zn * g_ref[...] + be_ref[...]
        o_ref[:, 0:S5_W] = s5_ref[...]
        o_ref[:, S5_W:S5_W + CONV_W] = (zz * _sigmoid(zz)).astype(BF16)

    const = lambda shape: pl.BlockSpec(shape, lambda i: (0, 0))
    vec = const((1, CONV_W))
    return _pallas(body, [u, u, o_s5, wdw, bdw, lng, lnb, mavg], name="conv_fwd", grid=(n,),
                   in_specs=[pl.BlockSpec((T, CONV_W), lambda i: (i, 1)), pl.BlockSpec((T, CONV_W), lambda i: (i, 2)),
                             pl.BlockSpec((T, S5_W), lambda i: (i, 0)), const((CONV_K, CONV_W)), vec, vec, vec,
                             const((CONV_W, CONV_W))],
                   out_specs=[pl.BlockSpec((T, CONV_W), lambda i: (i, 0)),
                              pl.BlockSpec((T, S5_W + CONV_W), lambda i: (i, 0))],
                   out_shape=[jax.ShapeDtypeStruct((L, CONV_W), F32), jax.ShapeDtypeStruct((L, S5_W + CONV_W), BF16)],
                   scratch_shapes=[pltpu.VMEM((T + HALO, CONV_W), F32)], comm=comm)


def _conv_bwd(dm, zc, u, du_s5, wdw, lng, lnb, mavg, comm=()):
    L = u.shape[0]
    T = min(TS, L)
    n = L // T
    hb = T // HALO
    first_tap = HALO - (CONV_K - 1)

    def body(dm_ref, zc_ref, v1_ref, v2_ref, p1_ref, p2_ref, s5_ref, w_ref, g_ref, be_ref, m_ref,
             du_ref, dw_ref, db_ref, dg_ref, dbe_ref, zbuf, dzbuf, head):
        i = pl.program_id(0)
        first = i == 0

        @pl.when(first)
        def _():
            head[...] = jnp.zeros_like(head)
            dw_ref[...] = jnp.zeros_like(dw_ref)

        zn, rstd = _head_norm(zc_ref[...], m_ref[...])
        zz = zn * g_ref[...] + be_ref[...]
        sg = _sigmoid(zz)
        dzz = dm_ref[...].astype(F32) * sg * (1.0 + zz * (1.0 - sg))
        _accumulate(dbe_ref, first, _colsum(dzz))
        _accumulate(dg_ref, first, _colsum(dzz * zn))
        dzn = dzz * g_ref[...]
        dzc = rstd * (dzn - _head_mean(dzn, m_ref[...]) - zn * _head_mean(dzn * zn, m_ref[...]))
        _accumulate(db_ref, first, _colsum(dzc))

        dzbuf[0:T, :] = dzc
        dzbuf[T:T + HALO, :] = head[...]
        head[...] = dzbuf[0:HALO, :]
        keep = (i < n - 1).astype(F32)
        zbuf[0:HALO, :] = p1_ref[...] * _sigmoid(p2_ref[...]) * keep
        zbuf[HALO:HALO + T, :] = v1_ref[...] * _sigmoid(v2_ref[...])
        du_ref[:, 0:S5_W] = s5_ref[...]

        for r0 in range(0, T, CONV_SB):
            rows = slice(r0, r0 + CONV_SB)
            dzc_b = dzbuf[rows, :]
            dz = jnp.zeros((CONV_SB, CONV_W), F32)
            for k in range(CONV_K):
                prod = dzc_b * zbuf[pl.ds(r0 + first_tap + k, CONV_SB), :]
                dw_ref[8 * k:8 * k + 8, :] += jnp.sum(prod.reshape(CONV_SB // 8, 8, CONV_W), axis=0)
                dz = dz + w_ref[k:k + 1, :] * dzbuf[pl.ds(r0 + CONV_K - 1 - k, CONV_SB), :]
            v1 = v1_ref[rows, :]
            sg2 = _sigmoid(v2_ref[rows, :])
            du_ref[rows, S5_W:S5_W + CONV_W] = (dz * sg2).astype(BF16)
            du_ref[rows, S5_W + CONV_W:S5_W + 2 * CONV_W] = (dz * v1 * sg2 * (1.0 - sg2)).astype(BF16)

    rev = lambda c: (lambda i: (n - 1 - i, c))
    prev = lambda c: (lambda i: (jnp.maximum((n - 1 - i) * hb - 1, 0), c))
    const = lambda shape: pl.BlockSpec(shape, lambda i: (0, 0))
    vec = const((1, CONV_W))
    vshape = jax.ShapeDtypeStruct((1, CONV_W), F32)
    return _pallas(
        body, [dm, zc, u, u, u, u, du_s5, wdw, lng, lnb, mavg], name="conv_bwd", grid=(n,),
        in_specs=[pl.BlockSpec((T, CONV_W), rev(1)), pl.BlockSpec((T, CONV_W), rev(0)),
                  pl.BlockSpec((T, CONV_W), rev(1)), pl.BlockSpec((T, CONV_W), rev(2)),
                  pl.BlockSpec((HALO, CONV_W), prev(1)), pl.BlockSpec((HALO, CONV_W), prev(2)),
                  pl.BlockSpec((T, S5_W), rev(0)), const((CONV_K, CONV_W)), vec, vec, const((CONV_W, CONV_W))],
        out_specs=[pl.BlockSpec((T, S5_W + 2 * CONV_W), rev(0)), const((8 * HALO, CONV_W)), vec, vec, vec],
        out_shape=[jax.ShapeDtypeStruct((L, S5_W + 2 * CONV_W), BF16), jax.ShapeDtypeStruct((8 * HALO, CONV_W), F32),
                   vshape, vshape, vshape],
        scratch_shapes=[pltpu.VMEM((T + HALO, CONV_W), F32), pltpu.VMEM((T + HALO, CONV_W), F32),
                        pltpu.VMEM((HALO, CONV_W), F32)], comm=comm)


def _gather_all(v):
    rows, cols = v.shape

    def body(x_ref, out_ref, send_sems, recv_sems, local_sem):
        x, y, c, chips = _position()
        me, sibling = (x, y, c), (x, y, 1 - c)

        def block(px, py, pc):
            return out_ref.at[pl.ds((4 * px + 2 * py + pc) * rows, rows), :]

        def copy(k, blk, to, src=None):
            return pltpu.make_async_remote_copy(
                src_ref=block(*blk) if src is None else src, dst_ref=block(*blk), send_sem=send_sems.at[k],
                recv_sem=recv_sems.at[k], device_id=to, device_id_type=MESH)

        mine = pltpu.make_async_copy(x_ref, block(*me), local_sem)
        mine.start()
        first = [copy(0, me, sibling, src=x_ref)]
        first += [copy(1 + j, me, (*chip, c), src=x_ref) for j, chip in enumerate(chips)]
        for cp in first:
            cp.start()
        passed = [copy(4 + j, (*chip, c), sibling) for j, chip in enumerate(chips)]
        for j, chip in enumerate(chips):
            copy(1 + j, (*chip, c), me).wait_recv()
            passed[j].start()
        copy(0, sibling, me).wait_recv()
        for j, chip in enumerate(chips):
            copy(4 + j, (*chip, 1 - c), me).wait_recv()
        for cp in first + passed:
            cp.wait_send()
        mine.wait()

    return pl.pallas_call(
        body, name="gather_small",
        in_specs=[pl.BlockSpec(memory_space=pltpu.VMEM)], out_specs=pl.BlockSpec(memory_space=pltpu.VMEM),
        out_shape=jax.ShapeDtypeStruct((N_DEV * rows, cols), v.dtype),
        scratch_shapes=[pltpu.SemaphoreType.DMA((7,)), pltpu.SemaphoreType.DMA((7,)), pltpu.SemaphoreType.DMA],
        compiler_params=pltpu.CompilerParams(vmem_limit_bytes=VMEM_LIMIT))(v)


def _adamw(w, g, m, v):
    m = ADAM_B1 * m + (1.0 - ADAM_B1) * g
    v = ADAM_B2 * v + (1.0 - ADAM_B2) * jnp.square(g)
    m_hat = m / (1.0 - ADAM_B1 ** ADAM_STEP)
    v_hat = v / (1.0 - ADAM_B2 ** ADAM_STEP)
    return -ADAM_LR * (m_hat / (jnp.sqrt(v_hat) + ADAM_EPS) + ADAM_WD * w), m, v


def _sum_slots(recv, name, comm=()):
    _, rows, cols = recv.shape
    tr = _row_tile(rows, cols)

    def body(r_ref, o_ref):
        acc = r_ref[0].astype(F32)
        for s in range(1, N_CHIPS):
            acc = acc + r_ref[s].astype(F32)
        o_ref[...] = acc

    return _pallas(body, [recv], name=name, grid=(rows // tr,),
                   in_specs=[pl.BlockSpec((N_CHIPS, tr, cols), lambda i: (0, i, 0))],
                   out_specs=[pl.BlockSpec((tr, cols), lambda i: (i, 0))],
                   out_shape=[jax.ShapeDtypeStruct((rows, cols), F32)], comm=comm)[0]


def _add_halves(mine, theirs, name):
    slots, rows, cols = mine.shape
    tr = _row_tile(rows, cols * slots)

    def body(a_ref, b_ref, o_ref):
        o_ref[...] = (a_ref[...].astype(F32) + b_ref[...].astype(F32)).astype(o_ref.dtype)

    spec = pl.BlockSpec((slots, tr, cols), lambda i: (0, i, 0))
    return _pallas(body, [mine, theirs], name=name, grid=(rows // tr,), in_specs=[spec, spec], out_specs=[spec],
                   out_shape=[jax.ShapeDtypeStruct(mine.shape, mine.dtype)])[0]


def _adamw_sharded(w, parts, m, v, name, comm=()):
    rows, cols = w.shape
    tr = _row_tile(rows, cols)
    n = len(parts)

    def body(w_ref, *refs):
        p_refs, (m_ref, v_ref, g_ref, d_ref, nm_ref, nv_ref) = refs[:n], refs[n:]
        g = p_refs[0][...]
        for p_ref in p_refs[1:]:
            g = g + p_ref[...]
        g_ref[...] = g
        d_ref[...], nm_ref[...], nv_ref[...] = _adamw(w_ref[...], g, m_ref[...], v_ref[...])

    spec = pl.BlockSpec((tr, cols), lambda i: (i, 0))
    shape = jax.ShapeDtypeStruct((rows, cols), F32)
    return _pallas(body, [w, *parts, m, v], name=name, grid=(rows // tr,), in_specs=[spec] * (n + 3),
                   out_specs=[spec] * 4, out_shape=[shape] * 4, comm=comm)


def _adamw_small(w, gathered, m, v):
    rows, cols = w.shape

    def body(w_ref, a_ref, m_ref, v_ref, g_ref, d_ref, nm_ref, nv_ref):
        g = a_ref[0:rows, :]
        for dev in range(1, N_DEV):
            g = g + a_ref[dev * rows:(dev + 1) * rows, :]
        g_ref[...] = g
        d_ref[...], nm_ref[...], nv_ref[...] = _adamw(w_ref[...], g, m_ref[...], v_ref[...])

    shape = jax.ShapeDtypeStruct((rows, cols), F32)
    return pl.pallas_call(
        body, name="adamw_small", out_shape=[shape] * 4,
        compiler_params=pltpu.CompilerParams(vmem_limit_bytes=VMEM_LIMIT))(w, gathered, m, v)


PACK_TILE = 8 * LANES


def _pack_small(vals, last_row):
    rows = []
    for name in SMALL:
        flat = vals[name].reshape(-1).astype(F32)
        rows.append(jnp.pad(flat, (0, -flat.size % PACK_TILE)).reshape(-1, LANES))
    rows.append(jnp.pad(last_row, ((0, 7), (0, 0))))
    return jnp.concatenate(rows, axis=0)


def _unpack_small(packed, like):
    out, r = {}, 0
    for name in SMALL:
        size = like[name].size
        out[name] = packed[r:r + -(-size // LANES)].reshape(-1)[:size].reshape(like[name].shape)
        r += 8 * -(-size // PACK_TILE)
    return out, packed[r, 0]


def _shard2d(name, v):
    v = v.reshape(v.shape[-2:])
    return v.T if name in FFN_T else v


def _unshard(name, v, shape):
    return (v.T if name in FFN_T else v).reshape(shape)


def _train_step(x3, tgt3, wts, ms, vs):
    x, tgt = x3[0], tgt3[0]
    L, D = x.shape
    row = lambda v: v.reshape(1, -1)
    shards = {k: _shard2d(k, wts[k]) for k in SHARDED}
    sends = {k: shards[k] if k == "conv_w_dw" else _cast_bf16(shards[k], "cast_" + k) for k in SHARDED}
    gat = {k: _Gather(sends[k]) for k in SHARDED}
    w = lambda k: gat[k].result[0]

    s = {k: wts[k] for k in SMALL}
    lr, li = s["s5_lam_re"].reshape(1, S5_N), s["s5_lam_im"].reshape(1, S5_N)
    ldt = jnp.repeat(s["s5_log_dt"].reshape(S5_G), S5_P).reshape(1, S5_N)
    brc, bic = _compact_b(s["s5_b_re"].reshape(S5_G, S5_P, S5_GC)), _compact_b(s["s5_b_im"].reshape(S5_G, S5_P, S5_GC))
    crc = _compact_b(s["s5_c_re"].reshape(S5_G, S5_GC, S5_P).transpose(0, 2, 1)).T
    cic = _compact_b(s["s5_c_im"].reshape(S5_G, S5_GC, S5_P).transpose(0, 2, 1)).T
    d_skip, b_glu = row(s["s5_d"]), row(s["s5_b_glu"])
    b_dw, ln_g, ln_b = row(s["conv_b_dw"]), row(s["conv_ln_g"]), row(s["conv_ln_b"])
    g1, gm, g2, gf = row(s["ffn1_norm"]), row(s["mix_norm"]), row(s["ffn2_norm"]), row(s["final_norm"])
    mavg = jnp.asarray(_HEAD_MEAN, dtype=BF16)

    h1 = _rms_fwd(x, g1, "rms1", comm=[gat["ffn1_w_gate"], gat["ffn1_w_up"]])
    a1, b1, act1 = _ffn_up(h1, w("ffn1_w_gate"), w("ffn1_w_up"), "ffn1_up", comm=[gat["ffn1_w_down"]])
    x1, h2 = _ffn_down(act1, w("ffn1_w_down"), x, gm, "ffn1_down",
                       comm=[gat["w_in"], gat["s5_w_glu"], gat["conv_w_dw"], gat["w_out"]])
    u = _mm_grouped(h2, w("w_in"), "in_proj")
    are, aim, bre, bim, cre, cim = _s5_params_fwd(lr, li, ldt, brc, bic, crc, cic)
    are_t, aim_t = are.reshape(S5_TILES, LANES), aim.reshape(S5_TILES, LANES)
    w_glu = w("s5_w_glu").reshape(S5_W, S5_W)
    *states, y_pre, o_s5 = _s5_fwd(u, are_t, aim_t, bre, bim, cre, cim, d_skip, w_glu, b_glu,
                                   comm=[gat["ffn2_w_gate"], gat["ffn2_w_up"]])
    w_dw = w("conv_w_dw").transpose(1, 0, 2).reshape(CONV_K, CONV_W)
    zc, cat = _conv_fwd(u, o_s5, w_dw, b_dw, ln_g, ln_b, mavg, comm=[gat["ffn2_w_down"]])
    w_out = w("w_out").reshape(-1, D)
    x2, h3 = _mix_out(cat, w_out, x1, g2, "mix_out")
    a2, b2, act2 = _ffn_up(h3, w("ffn2_w_gate"), w("ffn2_w_up"), "ffn2_up")
    dx3, dx3b, loss_part, d_gf = _ffn_down_loss(act2, w("ffn2_w_down"), x2, gf, tgt, "ffn2_down_loss")

    gs, sc, waiting = {"final_norm": d_gf}, {}, []

    def grad(key, g):
        if g.shape[1] % (2 * BF16_ROWS) == 0 and g.dtype == BF16:
            waiting.append(_SwapHalf(g, key))
        else:
            sc[key] = _Scatter(g)
            waiting.append(sc[key])

    def carry(call, *args, **kw):
        ops = list(waiting)
        waiting.clear()
        res = call(*args, comm=ops, **kw)
        for op in ops:
            if isinstance(op, _SwapHalf):
                sc[op.key] = _Scatter(_add_halves(*op.result, "add_" + op.key))
                waiting.append(sc[op.key])
        return res

    da2, db2 = _ffn_bwd_act(dx3b, w("ffn2_w_down"), a2, b2, "ffn2_bwd_act")
    grad("ffn2_w_down", _mm_tn(act2, dx3b[None], 0.5, "ffn2_dwd", N_CHIPS))
    grad("ffn2_w_gate", carry(_mm_tn, da2, h3[None], 1.0, "ffn2_dwg", N_CHIPS))
    grad("ffn2_w_up", carry(_mm_tn, db2, h3[None], 1.0, "ffn2_dwu", N_CHIPS))
    dx2, dx2b, gs["ffn2_norm"] = carry(_mm_rmsbwd, [da2, db2], [w("ffn2_w_gate"), w("ffn2_w_up")], False, False, x2, g2,
                                       dx3, "ffn2_bwd_dx")

    dm = _mm_nt(dx2b, w_out, "mix_bwd")
    grad("w_out", _mm_tn(cat[None], dx2b[None], 1.0, "dwout", 1).reshape(N_CHIPS, -1, D))
    (du_s5, d_wglu, gs["s5_b_glu"], gs["s5_d"], d_crc, d_cic, d_bre, d_bim, d_are, d_aim) = carry(
        _s5_bwd, dm, y_pre, u, states, are_t, aim_t, bre, bim, cre, cim, d_skip, w_glu, b_glu)
    grad("s5_w_glu", d_wglu.astype(BF16).reshape(N_CHIPS, -1, S5_W))
    g_lr, g_li, g_ldt, g_brc, g_bic = _s5_params_bwd(lr, li, ldt, brc, bic, d_are.reshape(1, S5_N),
                                                     d_aim.reshape(1, S5_N), d_bre, d_bim)
    gs["s5_lam_re"], gs["s5_lam_im"] = g_lr, g_li
    gs["s5_log_dt"] = jnp.sum(g_ldt.reshape(S5_G, S5_P), axis=1)
    gs["s5_b_re"], gs["s5_b_im"] = _uncompact_b(g_brc), _uncompact_b(g_bic)
    gs["s5_c_re"] = _uncompact_b(d_crc.T).transpose(0, 2, 1)
    gs["s5_c_im"] = _uncompact_b(d_cic.T).transpose(0, 2, 1)
    du, d_wdw, gs["conv_b_dw"], gs["conv_ln_g"], gs["conv_ln_b"] = carry(_conv_bwd, dm, zc, u, du_s5, w_dw, ln_g, ln_b, mavg)
    d_wdw = jnp.sum(d_wdw.reshape(HALO, 8, CONV_W), axis=1)[:CONV_K]
    grad("conv_w_dw", d_wdw.reshape(CONV_K, N_CHIPS, -1).transpose(1, 0, 2))
    grad("w_in", carry(_mm_tn, h2[None], du, 1.0, "dwin", N_CHIPS, b_cols=True))
    dx1, dx1b, gs["mix_norm"] = carry(_mm_rmsbwd, [du], [w("w_in")], True, True, x1, gm, dx2, "in_proj_bwd")

    da1, db1 = carry(_ffn_bwd_act, dx1b, w("ffn1_w_down"), a1, b1, "ffn1_bwd_act")
    grad("ffn1_w_down", _mm_tn(act1, dx1b[None], 0.5, "ffn1_dwd", N_CHIPS))
    grad("ffn1_w_gate", carry(_mm_tn, da1, h1[None], 1.0, "ffn1_dwg", N_CHIPS))
    grad("ffn1_w_up", carry(_mm_tn, db1, h1[None], 1.0, "ffn1_dwu", N_CHIPS))
    grad_x, _, gs["ffn1_norm"] = carry(_mm_rmsbwd, [da1, db1], [w("ffn1_w_gate"), w("ffn1_w_up")], False, False, x, g1,
                                       dx1, "ffn1_bwd_dx")

    out = {}
    gsmall = {k: gs[k].reshape(wts[k].shape) for k in SMALL}
    zero_row = jnp.zeros((1, LANES), F32)
    g_all = _gather_all(_pack_small(gsmall, loss_part))
    res = _adamw_small(_pack_small(s, zero_row), g_all, _pack_small({k: ms[k] for k in SMALL}, zero_row),
                       _pack_small({k: vs[k] for k in SMALL}, zero_row))
    unpacked = [_unpack_small(r, s) for r in res]
    loss = unpacked[0][1]
    for k in SMALL:
        out[k] = [u_[0][k] for u_ in unpacked]

    order = ("ffn2_w_down", "ffn2_w_gate", "ffn2_w_up", "w_out", "s5_w_glu", "conv_w_dw", "w_in", "ffn1_w_down",
             "ffn1_w_gate", "ffn1_w_up")
    back = {}
    for k in order:
        part = carry(_sum_slots, sc[k].result[0], "sum_" + k)
        back[k] = _SwapBack(part) if part.shape != shards[k].shape else _Swap(part)
        waiting.append(back[k])
    for k in order:
        parts = [back[k].result[0]] if isinstance(back[k], _SwapBack) else [back[k].ins[0], back[k].result[0]]
        res = carry(_adamw_sharded, shards[k], parts, _shard2d(k, ms[k]), _shard2d(k, vs[k]), "adamw_" + k)
        out[k] = [_unshard(k, r, wts[k].shape) for r in res]
    return loss, grad_x[None], out


def kernel(x, ffn1_norm, ffn1_w_gate, ffn1_w_up, ffn1_w_down, mix_norm, w_in, s5_lam_re, s5_lam_im, s5_log_dt, s5_b_re, s5_b_im, s5_c_re, s5_c_im, s5_d, s5_w_glu, s5_b_glu, conv_w_dw, conv_b_dw, conv_ln_g, conv_ln_b, w_out, ffn2_norm, ffn2_w_gate, ffn2_w_up, ffn2_w_down, final_norm, loss_target, m_ffn1_norm, m_ffn1_w_gate, m_ffn1_w_up, m_ffn1_w_down, m_mix_norm, m_w_in, m_s5_lam_re, m_s5_lam_im, m_s5_log_dt, m_s5_b_re, m_s5_b_im, m_s5_c_re, m_s5_c_im, m_s5_d, m_s5_w_glu, m_s5_b_glu, m_conv_w_dw, m_conv_b_dw, m_conv_ln_g, m_conv_ln_b, m_w_out, m_ffn2_norm, m_ffn2_w_gate, m_ffn2_w_up, m_ffn2_w_down, m_final_norm, v_ffn1_norm, v_ffn1_w_gate, v_ffn1_w_up, v_ffn1_w_down, v_mix_norm, v_w_in, v_s5_lam_re, v_s5_lam_im, v_s5_log_dt, v_s5_b_re, v_s5_b_im, v_s5_c_re, v_s5_c_im, v_s5_d, v_s5_w_glu, v_s5_b_glu, v_conv_w_dw, v_conv_b_dw, v_conv_ln_g, v_conv_ln_b, v_w_out, v_ffn2_norm, v_ffn2_w_gate, v_ffn2_w_up, v_ffn2_w_down, v_final_norm):
    given = dict(locals())
    wts = {k: given[k] for k in WEIGHTS}
    ms = {k: given["m_" + k] for k in WEIGHTS}
    vs = {k: given["v_" + k] for k in WEIGHTS}
    loss, grad_x, out = _train_step(x, loss_target, wts, ms, vs)
    return (loss, grad_x, *[out[k][0] for k in WEIGHTS], *[out[k][1] for k in WEIGHTS],
            *[out[k][2] for k in WEIGHTS], *[out[k][3] for k in WEIGHTS])
```

```python
import functools

import jax
import jax.numpy as jnp
import numpy as np
from jax import lax
from jax.experimental import pallas as pl
from jax.experimental.pallas import tpu as pltpu

F32, BF16 = jnp.float32, jnp.bfloat16
MESH = pl.DeviceIdType.MESH

EPS = 1e-6
ADAM_LR, ADAM_B1, ADAM_B2, ADAM_EPS, ADAM_WD, ADAM_STEP = 0.001, 0.9, 0.999, 1e-08, 0.01, 10

N_CHIPS = 4
N_DEV = 8
LANES = 128
BF16_ROWS = 16
S5_W, S5_G, S5_GC, S5_P = 512, 32, 16, 64
S5_N = S5_G * S5_P
S5_TILES = S5_N // LANES
S5_HALF = 8
CONV_W, CONV_K, CONV_HD = 512, 31, 64
HALO = 32
CONV_SB = 32
TM = 512
TMS = 1024
TK = 2048
TS = 256
VMEM_LIMIT = 48 << 20
GELU_C0, GELU_C1 = 0.7978845608028654, 0.044715

FFN_T = ("ffn1_w_gate", "ffn1_w_up", "ffn2_w_gate", "ffn2_w_up")
SHARDED = ("ffn1_w_gate", "ffn1_w_up", "ffn1_w_down", "w_in", "s5_w_glu", "conv_w_dw", "w_out",
           "ffn2_w_gate", "ffn2_w_up", "ffn2_w_down")
SMALL = ("ffn1_norm", "mix_norm", "s5_lam_re", "s5_lam_im", "s5_log_dt", "s5_b_re", "s5_b_im", "s5_c_re",
         "s5_c_im", "s5_d", "s5_b_glu", "conv_b_dw", "conv_ln_g", "conv_ln_b", "ffn2_norm", "final_norm")
WEIGHTS = ("ffn1_norm", "ffn1_w_gate", "ffn1_w_up", "ffn1_w_down", "mix_norm", "w_in", "s5_lam_re", "s5_lam_im",
           "s5_log_dt", "s5_b_re", "s5_b_im", "s5_c_re", "s5_c_im", "s5_d", "s5_w_glu", "s5_b_glu", "conv_w_dw",
           "conv_b_dw", "conv_ln_g", "conv_ln_b", "w_out", "ffn2_norm", "ffn2_w_gate", "ffn2_w_up", "ffn2_w_down",
           "final_norm")


def _dot(a, b):
    return jnp.dot(a, b, preferred_element_type=F32)


def _dot_nt(a, b):
    return lax.dot_general(a, b, (((1,), (1,)), ((), ())), preferred_element_type=F32)


def _dot_tn(a, b):
    return lax.dot_general(a, b, (((0,), (0,)), ((), ())), preferred_element_type=F32)


def _colsum(v):
    return jnp.sum(v, axis=0, keepdims=True)


def _sigmoid(v):
    return 1.0 / (1.0 + jnp.exp(-v))


def _accumulate(ref, first, value):
    @pl.when(first)
    def _():
        ref[...] = value

    @pl.when(jnp.logical_not(first))
    def _():
        ref[...] += value


def _position():
    x, y, c = lax.axis_index("x"), lax.axis_index("y"), lax.axis_index("c")
    return x, y, c, [(1 - x, y), (x, 1 - y), (1 - x, 1 - y)]


def _remote(src, dst, sems, send, recv, device):
    return pltpu.make_async_remote_copy(src_ref=src, dst_ref=dst, send_sem=sems.at[send], recv_sem=sems.at[recv],
                                        device_id=device, device_id_type=MESH)


class _Gather:
    def __init__(self, shard):
        self.ins = [shard]
        self.outs = [jax.ShapeDtypeStruct((N_CHIPS,) + shard.shape, shard.dtype)]
        self.rows = shard.shape[0]
        self.halve = shard.dtype == BF16 and self.rows % (2 * BF16_ROWS) == 0
        self.n_sem = 13 if self.halve else 7
        self.result = None

    def _copies(self, ins, outs, sems, s0, pos):
        x, y, c, chips = pos
        src, dst = ins[0], outs[0]
        me = 2 * x + y
        if self.halve:
            hr = self.rows // 2
            mine, theirs = pl.ds(c * hr, hr), pl.ds((1 - c) * hr, hr)
            part = lambda slot, rows: dst.at[slot, rows]
            my_src = src.at[mine]
        else:
            mine = theirs = None
            part = lambda slot, rows: dst.at[slot]
            my_src = src
        slot = lambda j: 2 * chips[j][0] + chips[j][1]
        local = lambda: pltpu.make_async_copy(src, dst.at[me], sems.at[s0])
        send = lambda j: _remote(my_src, part(me, mine), sems, s0 + 1 + j, s0 + 4 + j, (*chips[j], c))
        land = lambda j: _remote(my_src, part(slot(j), mine), sems, s0 + 1 + j, s0 + 4 + j, (*chips[j], c))
        fwd = lambda j: _remote(part(slot(j), mine), part(slot(j), mine), sems, s0 + 7 + j, s0 + 10 + j, (x, y, 1 - c))
        got = lambda j: _remote(part(slot(j), theirs), part(slot(j), theirs), sems, s0 + 7 + j, s0 + 10 + j,
                                (x, y, 1 - c))
        return local, send, land, fwd, got

    def start(self, ins, outs, sems, s0, pos):
        local, send, _, _, _ = self._copies(ins, outs, sems, s0, pos)
        local().start()
        for j in range(N_CHIPS - 1):
            send(j).start()

    def finish(self, ins, outs, sems, s0, pos):
        local, send, land, fwd, got = self._copies(ins, outs, sems, s0, pos)
        others = range(N_CHIPS - 1)
        for j in others:
            land(j).wait_recv()
            if self.halve:
                fwd(j).start()
        for j in others:
            if self.halve:
                got(j).wait_recv()
        for j in others:
            send(j).wait_send()
            if self.halve:
                fwd(j).wait_send()
        local().wait()


class _Scatter:
    def __init__(self, grad):
        self.ins = [grad]
        self.outs = [jax.ShapeDtypeStruct(grad.shape, grad.dtype)]
        self.n_sem = 7
        self.result = None

    def _copies(self, ins, outs, sems, s0, pos):
        x, y, c, chips = pos
        src, dst = ins[0], outs[0]
        me = 2 * x + y
        slot = lambda j: 2 * chips[j][0] + chips[j][1]
        local = lambda: pltpu.make_async_copy(src.at[me], dst.at[me], sems.at[s0])
        send = lambda j: _remote(src.at[slot(j)], dst.at[me], sems, s0 + 1 + j, s0 + 4 + j, (*chips[j], c))
        land = lambda j: _remote(src.at[me], dst.at[slot(j)], sems, s0 + 1 + j, s0 + 4 + j, (*chips[j], c))
        return local, send, land

    def start(self, ins, outs, sems, s0, pos):
        local, send, _ = self._copies(ins, outs, sems, s0, pos)
        local().start()
        for j in range(N_CHIPS - 1):
            send(j).start()

    def finish(self, ins, outs, sems, s0, pos):
        local, send, land = self._copies(ins, outs, sems, s0, pos)
        for j in range(N_CHIPS - 1):
            land(j).wait_recv()
        for j in range(N_CHIPS - 1):
            send(j).wait_send()
        local().wait()


class _Swap:
    def __init__(self, part):
        self.ins = [part]
        self.outs = [jax.ShapeDtypeStruct(part.shape, part.dtype)]
        self.n_sem = 2
        self.result = None

    def _copy(self, ins, outs, sems, s0, pos):
        x, y, c, _ = pos
        return _remote(ins[0], outs[0], sems, s0, s0 + 1, (x, y, 1 - c))

    def start(self, ins, outs, sems, s0, pos):
        self._copy(ins, outs, sems, s0, pos).start()

    def finish(self, ins, outs, sems, s0, pos):
        self._copy(ins, outs, sems, s0, pos).wait()


class _SwapHalf:
    def __init__(self, grad, key):
        slots, rows, cols = grad.shape
        half = jax.ShapeDtypeStruct((slots, rows // 2, cols), grad.dtype)
        self.ins, self.outs, self.key = [grad], [half, half], key
        self.hr = rows // 2
        self.n_sem = 3
        self.result = None

    def _copies(self, ins, outs, sems, s0, pos):
        x, y, c, _ = pos
        mine, theirs = pl.ds(c * self.hr, self.hr), pl.ds((1 - c) * self.hr, self.hr)
        local = pltpu.make_async_copy(ins[0].at[:, mine], outs[0], sems.at[s0])
        remote = _remote(ins[0].at[:, theirs], outs[1], sems, s0 + 1, s0 + 2, (x, y, 1 - c))
        return local, remote

    def start(self, ins, outs, sems, s0, pos):
        for cp in self._copies(ins, outs, sems, s0, pos):
            cp.start()

    def finish(self, ins, outs, sems, s0, pos):
        for cp in self._copies(ins, outs, sems, s0, pos):
            cp.wait()


class _SwapBack:
    def __init__(self, part):
        hr, cols = part.shape
        self.ins, self.outs = [part], [jax.ShapeDtypeStruct((2 * hr, cols), part.dtype)]
        self.hr = hr
        self.n_sem = 3
        self.result = None

    def _copies(self, ins, outs, sems, s0, pos):
        x, y, c, _ = pos
        mine, theirs = pl.ds(c * self.hr, self.hr), pl.ds((1 - c) * self.hr, self.hr)
        local = lambda: pltpu.make_async_copy(ins[0], outs[0].at[mine], sems.at[s0])
        send = lambda: _remote(ins[0], outs[0].at[mine], sems, s0 + 1, s0 + 2, (x, y, 1 - c))
        land = lambda: _remote(ins[0], outs[0].at[theirs], sems, s0 + 1, s0 + 2, (x, y, 1 - c))
        return local, send, land

    def start(self, ins, outs, sems, s0, pos):
        local, send, _ = self._copies(ins, outs, sems, s0, pos)
        local().start()
        send().start()

    def finish(self, ins, outs, sems, s0, pos):
        local, send, land = self._copies(ins, outs, sems, s0, pos)
        land().wait_recv()
        send().wait_send()
        local().wait()


def _pallas(body, args, *, name, grid, in_specs, out_specs, out_shape, scratch_shapes=(), comm=()):
    comm = list(comm)
    n_in, n_out, n_scr = len(in_specs), len(out_specs), len(scratch_shapes)
    c_in = [a for op in comm for a in op.ins]
    c_out = [s for op in comm for s in op.outs]
    n_sem = sum(op.n_sem for op in comm)

    def full(*refs):
        o0 = n_in + len(c_in)
        s0 = o0 + n_out + len(c_out)
        ins, cin = refs[:n_in], refs[n_in:o0]
        outs, cout = refs[o0:o0 + n_out], refs[o0 + n_out:s0]
        scratch = refs[s0:s0 + n_scr]
        if comm:
            sems = refs[s0 + n_scr]
            ids = [pl.program_id(d) for d in range(len(grid))]
            first = functools.reduce(jnp.logical_and, [i == 0 for i in ids])
            last = functools.reduce(jnp.logical_and, [i == g - 1 for i, g in zip(ids, grid)])
            pos = _position()

            def each(step):
                ci = co = cs = 0
                for op in comm:
                    getattr(op, step)(cin[ci:ci + len(op.ins)], cout[co:co + len(op.outs)], sems, cs, pos)
                    ci, co, cs = ci + len(op.ins), co + len(op.outs), cs + op.n_sem

            @pl.when(first)
            def _():
                each("start")

        body(*ins, *outs, *scratch)
        if comm:
            @pl.when(last)
            def _():
                each("finish")

    hbm = pl.BlockSpec(memory_space=pl.ANY)
    res = pl.pallas_call(
        full, name=name, grid=grid,
        in_specs=list(in_specs) + [hbm] * len(c_in), out_specs=list(out_specs) + [hbm] * len(c_out),
        out_shape=list(out_shape) + c_out,
        scratch_shapes=list(scratch_shapes) + ([pltpu.SemaphoreType.DMA((n_sem,))] if comm else []),
        compiler_params=pltpu.CompilerParams(dimension_semantics=("arbitrary",) * len(grid),
                                             vmem_limit_bytes=VMEM_LIMIT))(*args, *c_in)
    k = n_out
    for op in comm:
        op.result = list(res[k:k + len(op.outs)])
        k += len(op.outs)
    return list(res[:n_out])


def _row_tile(rows, cols, itemsize=4, budget=1 << 20):
    t = rows
    while t % (2 * BF16_ROWS) == 0 and t * cols * itemsize > budget:
        t //= 2
    return t


def _cast_bf16(w, name):
    rows, cols = w.shape
    tr = _row_tile(rows, cols)

    def body(w_ref, o_ref):
        o_ref[...] = w_ref[...].astype(BF16)

    spec = pl.BlockSpec((tr, cols), lambda i: (i, 0))
    return _pallas(body, [w], name=name, grid=(rows // tr,), in_specs=[spec], out_specs=[spec],
                   out_shape=[jax.ShapeDtypeStruct((rows, cols), BF16)])[0]


def _rms_fwd(x, g, name, comm=()):
    L, D = x.shape

    def body(x_ref, g_ref, h_ref):
        xf = x_ref[...]
        r = lax.rsqrt(jnp.mean(xf * xf, axis=-1, keepdims=True) + EPS)
        h_ref[...] = (xf * r * g_ref[...]).astype(BF16)

    row = pl.BlockSpec((TMS, D), lambda i: (i, 0))
    return _pallas(body, [x, g], name=name, grid=(L // TMS,),
                   in_specs=[row, pl.BlockSpec((1, D), lambda i: (0, 0))], out_specs=[row],
                   out_shape=[jax.ShapeDtypeStruct((L, D), BF16)], comm=comm)[0]


def _resident(shape):
    return pl.BlockSpec(shape, lambda *_: (0,) * len(shape), pipeline_mode=pl.Buffered(1))


def _ffn_up(h, wg_t, wu_t, name, comm=()):
    L, D = h.shape
    G, FS, _ = wg_t.shape

    def body(h_ref, wg_ref, wu_ref, a_ref, b_ref, act_ref):
        j = pl.program_id(1)
        hv = h_ref[...]
        a = _dot_nt(hv, wg_ref[j])
        b = _dot_nt(hv, wu_ref[j])
        a_ref[...] = a.astype(BF16)
        b_ref[...] = b.astype(BF16)
        act_ref[...] = (a * _sigmoid(a) * b).astype(BF16)

    ospec = pl.BlockSpec((None, TMS, FS), lambda i, j: (j, i, 0))
    oshape = jax.ShapeDtypeStruct((G, L, FS), BF16)
    return _pallas(body, [h, wg_t, wu_t], name=name, grid=(L // TMS, G),
                   in_specs=[pl.BlockSpec((TMS, D), lambda i, j: (i, 0)), _resident((G, FS, D)), _resident((G, FS, D))],
                   out_specs=[ospec, ospec, ospec], out_shape=[oshape, oshape, oshape], comm=comm)


def _group_sum(a_ref, w_ref, groups, mm=_dot):
    acc = mm(a_ref[0], w_ref[0])
    for j in range(1, groups):
        acc = acc + mm(a_ref[j], w_ref[j])
    return acc


def _ffn_down(act, wd, x, g_next, name, comm=()):
    G, L, FS = act.shape
    D = wd.shape[2]

    def body(act_ref, wd_ref, x_ref, g_ref, xn_ref, hn_ref):
        xn = x_ref[...] + 0.5 * _group_sum(act_ref, wd_ref, G)
        xn_ref[...] = xn
        r = lax.rsqrt(jnp.mean(xn * xn, axis=-1, keepdims=True) + EPS)
        hn_ref[...] = (xn * r * g_ref[...]).astype(BF16)

    row = pl.BlockSpec((TM, D), lambda i: (i, 0))
    return _pallas(body, [act, wd, x, g_next], name=name, grid=(L // TM,),
                   in_specs=[pl.BlockSpec((G, TM, FS), lambda i: (0, i, 0)), _resident((G, FS, D)), row,
                             pl.BlockSpec((1, D), lambda i: (0, 0))],
                   out_specs=[row, row],
                   out_shape=[jax.ShapeDtypeStruct((L, D), F32), jax.ShapeDtypeStruct((L, D), BF16)], comm=comm)


def _ffn_down_loss(act, wd, x, gf, tgt, name):
    G, L, FS = act.shape
    D = wd.shape[2]

    def body(act_ref, wd_ref, x_ref, g_ref, t_ref, dx_ref, dxb_ref, loss_ref, dg_ref):
        i = pl.program_id(0)
        xn = x_ref[...] + 0.5 * _group_sum(act_ref, wd_ref, G)
        r = lax.rsqrt(jnp.mean(xn * xn, axis=-1, keepdims=True) + EPS)
        xh = xn * r
        gv = g_ref[...]
        e = xh * gv - t_ref[...]
        part = 0.5 * jnp.sum(_colsum(e * e), axis=1, keepdims=True) / D
        dy = e / D
        _accumulate(loss_ref, i == 0, jnp.broadcast_to(part, (1, LANES)))
        _accumulate(dg_ref, i == 0, _colsum(dy * xh))
        dxh = dy * gv
        dx = r * (dxh - xh * jnp.mean(dxh * xh, axis=-1, keepdims=True))
        dx_ref[...] = dx
        dxb_ref[...] = dx.astype(BF16)

    row = pl.BlockSpec((TM, D), lambda i: (i, 0))
    return _pallas(body, [act, wd, x, gf, tgt], name=name, grid=(L // TM,),
                   in_specs=[pl.BlockSpec((G, TM, FS), lambda i: (0, i, 0)), _resident((G, FS, D)), row,
                             pl.BlockSpec((1, D), lambda i: (0, 0)), row],
                   out_specs=[row, row, pl.BlockSpec((1, LANES), lambda i: (0, 0)),
                              pl.BlockSpec((1, D), lambda i: (0, 0))],
                   out_shape=[jax.ShapeDtypeStruct((L, D), F32), jax.ShapeDtypeStruct((L, D), BF16),
                              jax.ShapeDtypeStruct((1, LANES), F32), jax.ShapeDtypeStruct((1, D), F32)])


def _ffn_bwd_act(dxb, wd, a, b, name, comm=()):
    L, D = dxb.shape
    G, FS, _ = wd.shape

    def body(dx_ref, wd_ref, a_ref, b_ref, da_ref, db_ref):
        dact = 0.5 * _dot_nt(dx_ref[...], wd_ref[pl.program_id(1)])
        av = a_ref[...].astype(F32)
        bv = b_ref[...].astype(F32)
        sg = _sigmoid(av)
        da_ref[...] = (dact * bv * sg * (1.0 + av * (1.0 - sg))).astype(BF16)
        db_ref[...] = (dact * av * sg).astype(BF16)

    gspec = pl.BlockSpec((None, TMS, FS), lambda i, j: (j, i, 0))
    oshape = jax.ShapeDtypeStruct((G, L, FS), BF16)
    return _pallas(body, [dxb, wd, a, b], name=name, grid=(L // TMS, G),
                   in_specs=[pl.BlockSpec((TMS, D), lambda i, j: (i, 0)), _resident((G, FS, D)), gspec, gspec],
                   out_specs=[gspec, gspec], out_shape=[oshape, oshape], comm=comm)


def _mm_grouped(a, w, name):
    L, K = a.shape
    G, _, N = w.shape

    def body(a_ref, w_ref, o_ref):
        o_ref[...] = _dot(a_ref[...], w_ref[pl.program_id(1)])

    return _pallas(body, [a, w], name=name, grid=(L // TMS, G),
                   in_specs=[pl.BlockSpec((TMS, K), lambda i, g: (i, 0)), _resident((G, K, N))],
                   out_specs=[pl.BlockSpec((TMS, N), lambda i, g: (i, g))],
                   out_shape=[jax.ShapeDtypeStruct((L, G * N), F32)])[0]


def _mm_nt(a, w, name):
    L, K = a.shape
    N = w.shape[0]

    def body(a_ref, w_ref, o_ref):
        o_ref[...] = _dot_nt(a_ref[...], w_ref[...]).astype(BF16)

    return _pallas(body, [a, w], name=name, grid=(L // TMS,),
                   in_specs=[pl.BlockSpec((TMS, K), lambda i: (i, 0)), _resident((N, K))],
                   out_specs=[pl.BlockSpec((TMS, N), lambda i: (i, 0))],
                   out_shape=[jax.ShapeDtypeStruct((L, N), BF16)])[0]


def _mm_tn(a, b, scale, name, groups, b_cols=False, comm=()):
    L, M = a.shape[1], a.shape[2]
    N = b.shape[1] // groups if b_cols else b.shape[2]
    tk = min(L, TK)
    nk = L // tk

    def spec(v, cols):
        if cols:
            return pl.BlockSpec((tk, v.shape[1] // groups), lambda g, k: (k, g))
        if v.shape[0] > 1:
            return pl.BlockSpec((None, tk, v.shape[2]), lambda g, k: (g, k, 0))
        return pl.BlockSpec((None, tk, v.shape[2]), lambda g, k: (0, k, 0))

    def body(a_ref, b_ref, o_ref, acc):
        k = pl.program_id(1)
        p = _dot_tn(a_ref[...], b_ref[...])
        if nk == 1:
            o_ref[...] = (p * scale).astype(BF16)
        else:
            _accumulate(acc, k == 0, p)

            @pl.when(k == nk - 1)
            def _():
                o_ref[...] = (acc[...] * scale).astype(BF16)

    return _pallas(body, [a, b], name=name, grid=(groups, nk),
                   in_specs=[spec(a, False), spec(b, b_cols)],
                   out_specs=[pl.BlockSpec((None, M, N), lambda g, k: (g, 0, 0))],
                   out_shape=[jax.ShapeDtypeStruct((groups, M, N), BF16)],
                   scratch_shapes=[pltpu.VMEM((M, N), F32)], comm=comm)[0]


def _mm_rmsbwd(a_list, w_list, nt, a_cols, x_in, g, dx_out, name, comm=()):
    P = len(a_list)
    G = w_list[0].shape[0]
    L, D = x_in.shape
    mm = _dot_nt if nt else _dot

    def body(*refs):
        a_refs, w_refs = refs[:P], refs[P:2 * P]
        x_ref, g_ref, dxo_ref, dx_ref, dxb_ref, dg_ref = refs[2 * P:]
        i = pl.program_id(0)
        dh = None
        for a_ref, w_ref in zip(a_refs, w_refs):
            for j in range(G):
                if a_cols:
                    kw = a_ref.shape[1] // G
                    term = mm(a_ref[:, j * kw:(j + 1) * kw], w_ref[j])
                else:
                    term = mm(a_ref[j], w_ref[j])
                dh = term if dh is None else dh + term
        xf = x_ref[...]
        r = lax.rsqrt(jnp.mean(xf * xf, axis=-1, keepdims=True) + EPS)
        xh = xf * r
        _accumulate(dg_ref, i == 0, _colsum(dh * xh))
        dxh = dh * g_ref[...]
        dx = dxo_ref[...] + r * (dxh - xh * jnp.mean(dxh * xh, axis=-1, keepdims=True))
        dx_ref[...] = dx
        dxb_ref[...] = dx.astype(BF16)

    row = pl.BlockSpec((TM, D), lambda i: (i, 0))
    vec = pl.BlockSpec((1, D), lambda i: (0, 0))
    if a_cols:
        a_specs = [pl.BlockSpec((TM, a.shape[1]), lambda i: (i, 0)) for a in a_list]
    else:
        a_specs = [pl.BlockSpec((G, TM, a.shape[2]), lambda i: (0, i, 0)) for a in a_list]
    w_specs = [_resident(w.shape) for w in w_list]
    return _pallas(body, [*a_list, *w_list, x_in, g, dx_out], name=name, grid=(L // TM,),
                   in_specs=a_specs + w_specs + [row, vec, row], out_specs=[row, row, vec],
                   out_shape=[jax.ShapeDtypeStruct((L, D), F32), jax.ShapeDtypeStruct((L, D), BF16),
                              jax.ShapeDtypeStruct((1, D), F32)], comm=comm)


def _mix_out(cat, wout, x1, g_next, name):
    L, K = cat.shape
    D = wout.shape[1]

    def body(c_ref, w_ref, x_ref, g_ref, xn_ref, hn_ref):
        xn = x_ref[...] + _dot(c_ref[...], w_ref[...])
        xn_ref[...] = xn
        r = lax.rsqrt(jnp.mean(xn * xn, axis=-1, keepdims=True) + EPS)
        hn_ref[...] = (xn * r * g_ref[...]).astype(BF16)

    row = pl.BlockSpec((TMS, D), lambda i: (i, 0))
    return _pallas(body, [cat, wout, x1, g_next], name=name, grid=(L // TMS,),
                   in_specs=[pl.BlockSpec((TMS, K), lambda i: (i, 0)), pl.BlockSpec((K, D), lambda i: (0, 0)), row,
                             pl.BlockSpec((1, D), lambda i: (0, 0))],
                   out_specs=[row, row],
                   out_shape=[jax.ShapeDtypeStruct((L, D), F32), jax.ShapeDtypeStruct((L, D), BF16)])


def _s5_disc(lr, li, ldt, brc, bic):
    dt = jnp.exp(ldt)
    mag = jnp.exp(lr * dt)
    are = mag * jnp.cos(li * dt)
    aim = mag * jnp.sin(li * dt)
    den = lr * lr + li * li
    nre = are - 1.0
    fre = (nre * lr + aim * li) / den
    fim = (aim * lr - nre * li) / den
    return are, aim, fre * brc - fim * bic, fre * bic + fim * brc


def _s5_params_fwd(lr, li, ldt, brc, bic, crc, cic):
    def body(lr_ref, li_ref, ldt_ref, br_ref, bi_ref, cr_ref, ci_ref, are_ref, aim_ref, bre_ref, bim_ref, cre_ref, cim_ref):
        are, aim, bre, bim = _s5_disc(lr_ref[...], li_ref[...], ldt_ref[...], br_ref[...], bi_ref[...])
        are_ref[...] = are
        aim_ref[...] = aim
        bre_ref[...] = bre.astype(BF16)
        bim_ref[...] = bim.astype(BF16)
        cre_ref[...] = cr_ref[...].astype(BF16)
        cim_ref[...] = ci_ref[...].astype(BF16)

    vec = jax.ShapeDtypeStruct((1, S5_N), F32)
    return pl.pallas_call(
        body, name="s5_params_fwd",
        out_shape=[vec, vec, jax.ShapeDtypeStruct((LANES, S5_N), BF16), jax.ShapeDtypeStruct((LANES, S5_N), BF16),
                   jax.ShapeDtypeStruct((S5_N, LANES), BF16), jax.ShapeDtypeStruct((S5_N, LANES), BF16)],
        compiler_params=pltpu.CompilerParams(vmem_limit_bytes=VMEM_LIMIT))(lr, li, ldt, brc, bic, crc, cic)


def _s5_params_bwd(lr, li, ldt, brc, bic, dare, daim, dbre, dbim):
    def body(lr_ref, li_ref, ldt_ref, br_ref, bi_ref, dare_ref, daim_ref, dbre_ref, dbim_ref,
             glr_ref, gli_ref, gldt_ref, gbr_ref, gbi_ref):
        _, vjp = jax.vjp(_s5_disc, lr_ref[...], li_ref[...], ldt_ref[...], br_ref[...], bi_ref[...])
        glr, gli, gldt, gbr, gbi = vjp((dare_ref[...], daim_ref[...], dbre_ref[...], dbim_ref[...]))
        glr_ref[...] = glr
        gli_ref[...] = gli
        gldt_ref[...] = gldt
        gbr_ref[...] = gbr
        gbi_ref[...] = gbi

    vec = jax.ShapeDtypeStruct((1, S5_N), F32)
    mat = jax.ShapeDtypeStruct((LANES, S5_N), F32)
    return pl.pallas_call(
        body, name="s5_params_bwd", out_shape=[vec, vec, vec, mat, mat],
        compiler_params=pltpu.CompilerParams(vmem_limit_bytes=VMEM_LIMIT))(lr, li, ldt, brc, bic, dare, daim, dbre, dbim)


def _gelu_parts(y):
    th = jnp.tanh(GELU_C0 * (y + GELU_C1 * y * y * y))
    return 0.5 * y * (1.0 + th), th


def _state_rows(q, T):
    return pl.ds(q % S5_HALF, T, stride=S5_HALF)


_HALVES = [slice(h * S5_HALF, (h + 1) * S5_HALF) for h in range(2)]


def _s5_chunk_states(ub, bre_ref, bim_ref, are_ref, aim_ref, sre, sim, start_re, start_im, T):
    for q in range(S5_TILES):
        ct = q // 4
        uq = ub[:, ct * LANES:(ct + 1) * LANES]
        sre[q // S5_HALF][_state_rows(q, T), :] = _dot(uq, bre_ref[:, q * LANES:(q + 1) * LANES])
        sim[q // S5_HALF][_state_rows(q, T), :] = _dot(uq, bim_ref[:, q * LANES:(q + 1) * LANES])
    a_re = [are_ref[hs, :] for hs in _HALVES]
    a_im = [aim_ref[hs, :] for hs in _HALVES]

    def step(t, carry):
        rows = pl.ds(pl.multiple_of(t * S5_HALF, S5_HALF), S5_HALF)
        out = []
        for h in range(2):
            s_re, s_im = carry[2 * h], carry[2 * h + 1]
            n_re = a_re[h] * s_re - a_im[h] * s_im + sre[h][rows, :]
            n_im = a_re[h] * s_im + a_im[h] * s_re + sim[h][rows, :]
            sre[h][rows, :] = n_re
            sim[h][rows, :] = n_im
            out += [n_re, n_im]
        return tuple(out)

    init = (start_re[_HALVES[0], :], start_im[_HALVES[0], :], start_re[_HALVES[1], :], start_im[_HALVES[1], :])
    return lax.fori_loop(0, T, step, init, unroll=4)


def _s5_fwd(u, are, aim, bre, bim, cre, cim, d_skip, wglu, bglu, comm=()):
    L = u.shape[0]
    T = min(TS, L)
    n = L // T

    def body(u_ref, are_ref, aim_ref, bre_ref, bim_ref, cre_ref, cim_ref, d_ref, wg_ref, bg_ref,
             bnd_re, bnd_im, y_ref, o_ref, sre_lo, sre_hi, sim_lo, sim_hi, st_re, st_im):
        i = pl.program_id(0)
        sre, sim = (sre_lo, sre_hi), (sim_lo, sim_hi)

        @pl.when(i == 0)
        def _():
            st_re[...] = jnp.zeros_like(st_re)
            st_im[...] = jnp.zeros_like(st_im)

        bnd_re[...] = st_re[...]
        bnd_im[...] = st_im[...]
        uf = u_ref[...]
        fin = _s5_chunk_states(uf.astype(BF16), bre_ref, bim_ref, are_ref, aim_ref, sre, sim, st_re, st_im, T)
        for h in range(2):
            st_re[_HALVES[h], :] = fin[2 * h]
            st_im[_HALVES[h], :] = fin[2 * h + 1]
        tiles = []
        for ct in range(4):
            acc = jnp.zeros((T, LANES), F32)
            for q in range(4 * ct, 4 * ct + 4):
                win = slice(q * LANES, (q + 1) * LANES)
                acc = acc + _dot(sre[q // S5_HALF][_state_rows(q, T), :].astype(BF16), cre_ref[win, :])
                acc = acc - _dot(sim[q // S5_HALF][_state_rows(q, T), :].astype(BF16), cim_ref[win, :])
            tiles.append(acc)
        y = jnp.concatenate(tiles, axis=1) + d_ref[...] * uf
        y_ref[...] = y
        yg, _ = _gelu_parts(y)
        gate = _sigmoid(_dot(yg.astype(BF16), wg_ref[...]) + bg_ref[...])
        o_ref[...] = (yg * gate).astype(BF16)

    const = lambda shape: pl.BlockSpec(shape, lambda i: (0, 0))
    bspec = pl.BlockSpec((S5_TILES, LANES), lambda i: (i, 0))
    bshape = jax.ShapeDtypeStruct((n * S5_TILES, LANES), F32)
    chunk = pl.BlockSpec((T, S5_W), lambda i: (i, 0))
    sbuf = pltpu.VMEM((T * S5_HALF, LANES), F32)
    return _pallas(body, [u, are, aim, bre, bim, cre, cim, d_skip, wglu, bglu], name="s5_fwd", grid=(n,),
                   in_specs=[chunk, const((S5_TILES, LANES)), const((S5_TILES, LANES)),
                             const((LANES, S5_N)), const((LANES, S5_N)), const((S5_N, LANES)), const((S5_N, LANES)),
                             const((1, S5_W)), const((S5_W, S5_W)), const((1, S5_W))],
                   out_specs=[bspec, bspec, chunk, chunk],
                   out_shape=[bshape, bshape, jax.ShapeDtypeStruct((L, S5_W), F32), jax.ShapeDtypeStruct((L, S5_W), BF16)],
                   scratch_shapes=[sbuf, sbuf, sbuf, sbuf, pltpu.VMEM((S5_TILES, LANES), F32),
                                   pltpu.VMEM((S5_TILES, LANES), F32)], comm=comm)


def _s5_bwd(dm, y_pre, u, bounds, are, aim, bre, bim, cre, cim, d_skip, wglu, bglu, comm=()):
    L = u.shape[0]
    T = min(TS, L)
    n = L // T

    def body(dm_ref, y_ref, u_ref, bnd_re, bnd_im,
             are_ref, aim_ref, bre_ref, bim_ref, cre_ref, cim_ref, d_ref, wg_ref, bg_ref,
             du_ref, dwg_ref, dbg_ref, dd_ref, dcre_ref, dcim_ref, dbre_ref, dbim_ref, dare_ref, daim_ref,
             sre_lo, sre_hi, sim_lo, sim_hi, gre_lo, gre_hi, gim_lo, gim_hi, car_re, car_im):
        i = pl.program_id(0)
        first = i == 0
        sre, sim = (sre_lo, sre_hi), (sim_lo, sim_hi)
        gre, gim = (gre_lo, gre_hi), (gim_lo, gim_hi)

        @pl.when(first)
        def _():
            car_re[...] = jnp.zeros_like(car_re)
            car_im[...] = jnp.zeros_like(car_im)
            dcre_ref[...] = jnp.zeros_like(dcre_ref)
            dcim_ref[...] = jnp.zeros_like(dcim_ref)
            dbre_ref[...] = jnp.zeros_like(dbre_ref)
            dbim_ref[...] = jnp.zeros_like(dbim_ref)

        y = y_ref[...]
        uf = u_ref[...]
        yg, th = _gelu_parts(y)
        dgelu = 0.5 * (1.0 + th) + 0.5 * y * (1.0 - th * th) * GELU_C0 * (1.0 + 3.0 * GELU_C1 * y * y)
        ygb = yg.astype(BF16)
        sg = _sigmoid(_dot(ygb, wg_ref[...]) + bg_ref[...])
        dout = dm_ref[...].astype(F32)
        dgp = dout * yg * sg * (1.0 - sg)
        dgpb = dgp.astype(BF16)
        dyg = dout * sg + _dot_nt(dgpb, wg_ref[...])
        _accumulate(dwg_ref, first, _dot_tn(ygb, dgpb))
        _accumulate(dbg_ref, first, _colsum(dgp))
        dy = dyg * dgelu
        _accumulate(dd_ref, first, _colsum(dy * uf))
        dyb = dy.astype(BF16)
        ub = uf.astype(BF16)
        _s5_chunk_states(ub, bre_ref, bim_ref, are_ref, aim_ref, sre, sim, bnd_re, bnd_im, T)

        for q in range(S5_TILES):
            ct, h = q // 4, q // S5_HALF
            win = slice(q * LANES, (q + 1) * LANES)
            dyq = dyb[:, ct * LANES:(ct + 1) * LANES]
            dcre_ref[win, :] += _dot_tn(sre[h][_state_rows(q, T), :].astype(BF16), dyq)
            dcim_ref[win, :] -= _dot_tn(sim[h][_state_rows(q, T), :].astype(BF16), dyq)
            gre[h][_state_rows(q, T), :] = _dot_nt(dyq, cre_ref[win, :])
            gim[h][_state_rows(q, T), :] = -_dot_nt(dyq, cim_ref[win, :])

        halves = _HALVES
        a_re = [are_ref[hs, :] for hs in halves]
        a_im = [aim_ref[hs, :] for hs in halves]

        def adjoint(t, h, g_re, g_im):
            rows = pl.ds(pl.multiple_of(t * S5_HALF, S5_HALF), S5_HALF)
            n_re = gre[h][rows, :] + a_re[h] * g_re + a_im[h] * g_im
            n_im = gim[h][rows, :] + a_re[h] * g_im - a_im[h] * g_re
            gre[h][rows, :] = n_re
            gim[h][rows, :] = n_im
            return n_re, n_im

        def step(k, carry):
            t = T - 1 - k
            prev = pl.ds(pl.multiple_of((t - 1) * S5_HALF, S5_HALF), S5_HALF)
            out = []
            for h in range(2):
                g_re, g_im, da_re, da_im = carry[4 * h:4 * h + 4]
                g_re, g_im = adjoint(t, h, g_re, g_im)
                p_re, p_im = sre[h][prev, :], sim[h][prev, :]
                out += [g_re, g_im, da_re + g_re * p_re + g_im * p_im, da_im + g_im * p_re - g_re * p_im]
            return tuple(out)

        zero = jnp.zeros((S5_HALF, LANES), F32)
        init = (car_re[halves[0], :], car_im[halves[0], :], zero, zero, car_re[halves[1], :], car_im[halves[1], :], zero, zero)
        fin = lax.fori_loop(0, T - 1, step, init, unroll=4)
        for h in range(2):
            g_re, g_im, da_re, da_im = fin[4 * h:4 * h + 4]
            g_re, g_im = adjoint(0, h, g_re, g_im)
            p_re, p_im = bnd_re[halves[h], :], bnd_im[halves[h], :]
            car_re[halves[h], :] = g_re
            car_im[halves[h], :] = g_im
            da_re = da_re + g_re * p_re + g_im * p_im
            da_im = da_im + g_im * p_re - g_re * p_im

            @pl.when(first)
            def _():
                dare_ref[halves[h], :] = da_re
                daim_ref[halves[h], :] = da_im

            @pl.when(jnp.logical_not(first))
            def _():
                dare_ref[halves[h], :] += da_re
                daim_ref[halves[h], :] += da_im

        tiles = []
        for ct in range(4):
            uq = ub[:, ct * LANES:(ct + 1) * LANES]
            acc = d_ref[:, ct * LANES:(ct + 1) * LANES] * dy[:, ct * LANES:(ct + 1) * LANES]
            for q in range(4 * ct, 4 * ct + 4):
                h = q // S5_HALF
                win = slice(q * LANES, (q + 1) * LANES)
                gq_re = gre[h][_state_rows(q, T), :].astype(BF16)
                gq_im = gim[h][_state_rows(q, T), :].astype(BF16)
                acc = acc + _dot_nt(gq_re, bre_ref[:, win]) + _dot_nt(gq_im, bim_ref[:, win])
                dbre_ref[:, win] += _dot_tn(uq, gq_re)
                dbim_ref[:, win] += _dot_tn(uq, gq_im)
            tiles.append(acc)
        du_ref[...] = jnp.concatenate(tiles, axis=1).astype(BF16)

    rev = lambda i: (n - 1 - i, 0)
    const = lambda shape: pl.BlockSpec(shape, lambda i: (0, 0))
    chunk = pl.BlockSpec((T, S5_W), rev)
    bspec = pl.BlockSpec((S5_TILES, LANES), rev)
    tile = jax.ShapeDtypeStruct((S5_TILES, LANES), F32)
    vec = jax.ShapeDtypeStruct((1, S5_W), F32)
    sbuf = pltpu.VMEM((T * S5_HALF, LANES), F32)
    return _pallas(
        body, [dm, y_pre, u, *bounds, are, aim, bre, bim, cre, cim, d_skip, wglu, bglu],
        name="s5_bwd", grid=(n,),
        in_specs=[chunk, chunk, chunk, bspec, bspec] + [
            const((S5_TILES, LANES)), const((S5_TILES, LANES)), const((LANES, S5_N)), const((LANES, S5_N)),
            const((S5_N, LANES)), const((S5_N, LANES)), const((1, S5_W)), const((S5_W, S5_W)), const((1, S5_W))],
        out_specs=[chunk, const((S5_W, S5_W)), const((1, S5_W)), const((1, S5_W)), const((S5_N, LANES)),
                   const((S5_N, LANES)), const((LANES, S5_N)), const((LANES, S5_N)), const((S5_TILES, LANES)),
                   const((S5_TILES, LANES))],
        out_shape=[jax.ShapeDtypeStruct((L, S5_W), BF16), jax.ShapeDtypeStruct((S5_W, S5_W), F32), vec, vec,
                   jax.ShapeDtypeStruct((S5_N, LANES), F32), jax.ShapeDtypeStruct((S5_N, LANES), F32),
                   jax.ShapeDtypeStruct((LANES, S5_N), F32), jax.ShapeDtypeStruct((LANES, S5_N), F32), tile, tile],
        scratch_shapes=[sbuf] * 8 + [pltpu.VMEM((S5_TILES, LANES), F32), pltpu.VMEM((S5_TILES, LANES), F32)],
        comm=comm)


_EYE8 = np.eye(8, dtype=np.float32)


def _compact_b(b):
    return jnp.einsum("akpc,kj->jcakp", b.reshape(4, 8, S5_P, S5_GC), _EYE8).reshape(LANES, S5_N)


def _uncompact_b(m):
    return jnp.einsum("kcakp->akpc", m.reshape(8, S5_GC, 4, 8, S5_P)).reshape(S5_G, S5_P, S5_GC)


_HEAD_MEAN = np.kron(np.eye(CONV_W // CONV_HD, dtype=np.float32), np.full((CONV_HD, CONV_HD), 1.0 / CONV_HD, np.float32))


def _head_mean(v, m):
    hi = v.astype(BF16)
    lo = (v - hi.astype(F32)).astype(BF16)
    return _dot(hi, m) + _dot(lo, m)


def _head_norm(zc, m):
    d = zc - _head_mean(zc, m)
    rstd = lax.rsqrt(_head_mean(d * d, m) + EPS)
    return d * rstd, rstd


def _conv_fwd(u, o_s5, wdw, bdw, lng, lnb, mavg, comm=()):
    L = u.shape[0]
    T = min(TS, L)
    n = L // T
    first_tap = HALO - (CONV_K - 1)

    def body(v1_ref, v2_ref, s5_ref, w_ref, b_ref, g_ref, be_ref, m_ref, zc_ref, o_ref, zbuf):
        i = pl.program_id(0)

        @pl.when(i == 0)
        def _():
            zbuf[0:HALO, :] = jnp.zeros((HALO, CONV_W), F32)

        zbuf[HALO:HALO + T, :] = v1_ref[...] * _sigmoid(v2_ref[...])
        for r0 in range(0, T, CONV_SB):
            acc = jnp.broadcast_to(b_ref[...], (CONV_SB, CONV_W))
            for k in range(CONV_K):
                acc = acc + w_ref[k:k + 1, :] * zbuf[pl.ds(r0 + first_tap + k, CONV_SB), :]
            zc_ref[r0:r0 + CONV_SB, :] = acc
        zbuf[0:HALO, :] = zbuf[T:T + HALO, :]
        zn, _ = _head_norm(zc_ref[...], m_ref[...])
        zz = zn * g_ref[...] + be_ref[...]
        o_ref[:, 0:S5_W] = s5_ref[...]
        o_ref[:, S5_W:S5_W + CONV_W] = (zz * _sigmoid(zz)).astype(BF16)

    const = lambda shape: pl.BlockSpec(shape, lambda i: (0, 0))
    vec = const((1, CONV_W))
    return _pallas(body, [u, u, o_s5, wdw, bdw, lng, lnb, mavg], name="conv_fwd", grid=(n,),
                   in_specs=[pl.BlockSpec((T, CONV_W), lambda i: (i, 1)), pl.BlockSpec((T, CONV_W), lambda i: (i, 2)),
                             pl.BlockSpec((T, S5_W), lambda i: (i, 0)), const((CONV_K, CONV_W)), vec, vec, vec,
                             const((CONV_W, CONV_W))],
                   out_specs=[pl.BlockSpec((T, CONV_W), lambda i: (i, 0)),
                              pl.BlockSpec((T, S5_W + CONV_W), lambda i: (i, 0))],
                   out_shape=[jax.ShapeDtypeStruct((L, CONV_W), F32), jax.ShapeDtypeStruct((L, S5_W + CONV_W), BF16)],
                   scratch_shapes=[pltpu.VMEM((T + HALO, CONV_W), F32)], comm=comm)


def _conv_bwd(dm, zc, u, du_s5, wdw, lng, lnb, mavg, comm=()):
    L = u.shape[0]
    T = min(TS, L)
    n = L // T
    hb = T // HALO
    first_tap = HALO - (CONV_K - 1)

    def body(dm_ref, zc_ref, v1_ref, v2_ref, p1_ref, p2_ref, s5_ref, w_ref, g_ref, be_ref, m_ref,
             du_ref, dw_ref, db_ref, dg_ref, dbe_ref, zbuf, dzbuf, head):
        i = pl.program_id(0)
        first = i == 0

        @pl.when(first)
        def _():
            head[...] = jnp.zeros_like(head)
            dw_ref[...] = jnp.zeros_like(dw_ref)

        zn, rstd = _head_norm(zc_ref[...], m_ref[...])
        zz = zn * g_ref[...] + be_ref[...]
        sg = _sigmoid(zz)
        dzz = dm_ref[...].astype(F32) * sg * (1.0 + zz * (1.0 - sg))
        _accumulate(dbe_ref, first, _colsum(dzz))
        _accumulate(dg_ref, first, _colsum(dzz * zn))
        dzn = dzz * g_ref[...]
        dzc = rstd * (dzn - _head_mean(dzn, m_ref[...]) - zn * _head_mean(dzn * zn, m_ref[...]))
        _accumulate(db_ref, first, _colsum(dzc))

        dzbuf[0:T, :] = dzc
        dzbuf[T:T + HALO, :] = head[...]
        head[...] = dzbuf[0:HALO, :]
        keep = (i < n - 1).astype(F32)
        zbuf[0:HALO, :] = p1_ref[...] * _sigmoid(p2_ref[...]) * keep
        zbuf[HALO:HALO + T, :] = v1_ref[...] * _sigmoid(v2_ref[...])
        du_ref[:, 0:S5_W] = s5_ref[...]

        for r0 in range(0, T, CONV_SB):
            rows = slice(r0, r0 + CONV_SB)
            dzc_b = dzbuf[rows, :]
            dz = jnp.zeros((CONV_SB, CONV_W), F32)
            for k in range(CONV_K):
                prod = dzc_b * zbuf[pl.ds(r0 + first_tap + k, CONV_SB), :]
                dw_ref[8 * k:8 * k + 8, :] += jnp.sum(prod.reshape(CONV_SB // 8, 8, CONV_W), axis=0)
                dz = dz + w_ref[k:k + 1, :] * dzbuf[pl.ds(r0 + CONV_K - 1 - k, CONV_SB), :]
            v1 = v1_ref[rows, :]
            sg2 = _sigmoid(v2_ref[rows, :])
            du_ref[rows, S5_W:S5_W + CONV_W] = (dz * sg2).astype(BF16)
            du_ref[rows, S5_W + CONV_W:S5_W + 2 * CONV_W] = (dz * v1 * sg2 * (1.0 - sg2)).astype(BF16)

    rev = lambda c: (lambda i: (n - 1 - i, c))
    prev = lambda c: (lambda i: (jnp.maximum((n - 1 - i) * hb - 1, 0), c))
    const = lambda shape: pl.BlockSpec(shape, lambda i: (0, 0))
    vec = const((1, CONV_W))
    vshape = jax.ShapeDtypeStruct((1, CONV_W), F32)
    return _pallas(
        body, [dm, zc, u, u, u, u, du_s5, wdw, lng, lnb, mavg], name="conv_bwd", grid=(n,),
        in_specs=[pl.BlockSpec((T, CONV_W), rev(1)), pl.BlockSpec((T, CONV_W), rev(0)),
                  pl.BlockSpec((T, CONV_W), rev(1)), pl.BlockSpec((T, CONV_W), rev(2)),
                  pl.BlockSpec((HALO, CONV_W), prev(1)), pl.BlockSpec((HALO, CONV_W), prev(2)),
                  pl.BlockSpec((T, S5_W), rev(0)), const((CONV_K, CONV_W)), vec, vec, const((CONV_W, CONV_W))],
        out_specs=[pl.BlockSpec((T, S5_W + 2 * CONV_W), rev(0)), const((8 * HALO, CONV_W)), vec, vec, vec],
        out_shape=[jax.ShapeDtypeStruct((L, S5_W + 2 * CONV_W), BF16), jax.ShapeDtypeStruct((8 * HALO, CONV_W), F32),
                   vshape, vshape, vshape],
        scratch_shapes=[pltpu.VMEM((T + HALO, CONV_W), F32), pltpu.VMEM((T + HALO, CONV_W), F32),
                        pltpu.VMEM((HALO, CONV_W), F32)], comm=comm)


def _gather_all(v):
    rows, cols = v.shape

    def body(x_ref, out_ref, send_sems, recv_sems, local_sem):
        x, y, c, chips = _position()
        me, sibling = (x, y, c), (x, y, 1 - c)

        def block(px, py, pc):
            return out_ref.at[pl.ds((4 * px + 2 * py + pc) * rows, rows), :]

        def copy(k, blk, to, src=None):
            return pltpu.make_async_remote_copy(
                src_ref=block(*blk) if src is None else src, dst_ref=block(*blk), send_sem=send_sems.at[k],
                recv_sem=recv_sems.at[k], device_id=to, device_id_type=MESH)

        mine = pltpu.make_async_copy(x_ref, block(*me), local_sem)
        mine.start()
        first = [copy(0, me, sibling, src=x_ref)]
        first += [copy(1 + j, me, (*chip, c), src=x_ref) for j, chip in enumerate(chips)]
        for cp in first:
            cp.start()
        passed = [copy(4 + j, (*chip, c), sibling) for j, chip in enumerate(chips)]
        for j, chip in enumerate(chips):
            copy(1 + j, (*chip, c), me).wait_recv()
            passed[j].start()
        copy(0, sibling, me).wait_recv()
        for j, chip in enumerate(chips):
            copy(4 + j, (*chip, 1 - c), me).wait_recv()
        for cp in first + passed:
            cp.wait_send()
        mine.wait()

    return pl.pallas_call(
        body, name="gather_small",
        in_specs=[pl.BlockSpec(memory_space=pltpu.VMEM)], out_specs=pl.BlockSpec(memory_space=pltpu.VMEM),
        out_shape=jax.ShapeDtypeStruct((N_DEV * rows, cols), v.dtype),
        scratch_shapes=[pltpu.SemaphoreType.DMA((7,)), pltpu.SemaphoreType.DMA((7,)), pltpu.SemaphoreType.DMA],
        compiler_params=pltpu.CompilerParams(vmem_limit_bytes=VMEM_LIMIT))(v)


def _adamw(w, g, m, v):
    m = ADAM_B1 * m + (1.0 - ADAM_B1) * g
    v = ADAM_B2 * v + (1.0 - ADAM_B2) * jnp.square(g)
    m_hat = m / (1.0 - ADAM_B1 ** ADAM_STEP)
    v_hat = v / (1.0 - ADAM_B2 ** ADAM_STEP)
    return -ADAM_LR * (m_hat / (jnp.sqrt(v_hat) + ADAM_EPS) + ADAM_WD * w), m, v


def _sum_slots(recv, name, comm=()):
    _, rows, cols = recv.shape
    tr = _row_tile(rows, cols)

    def body(r_ref, o_ref):
        acc = r_ref[0].astype(F32)
        for s in range(1, N_CHIPS):
            acc = acc + r_ref[s].astype(F32)
        o_ref[...] = acc

    return _pallas(body, [recv], name=name, grid=(rows // tr,),
                   in_specs=[pl.BlockSpec((N_CHIPS, tr, cols), lambda i: (0, i, 0))],
                   out_specs=[pl.BlockSpec((tr, cols), lambda i: (i, 0))],
                   out_shape=[jax.ShapeDtypeStruct((rows, cols), F32)], comm=comm)[0]


def _add_halves(mine, theirs, name):
    slots, rows, cols = mine.shape
    tr = _row_tile(rows, cols * slots)

    def body(a_ref, b_ref, o_ref):
        o_ref[...] = (a_ref[...].astype(F32) + b_ref[...].astype(F32)).astype(o_ref.dtype)

    spec = pl.BlockSpec((slots, tr, cols), lambda i: (0, i, 0))
    return _pallas(body, [mine, theirs], name=name, grid=(rows // tr,), in_specs=[spec, spec], out_specs=[spec],
                   out_shape=[jax.ShapeDtypeStruct(mine.shape, mine.dtype)])[0]


def _adamw_sharded(w, parts, m, v, name, comm=()):
    rows, cols = w.shape
    tr = _row_tile(rows, cols)
    n = len(parts)

    def body(w_ref, *refs):
        p_refs, (m_ref, v_ref, g_ref, d_ref, nm_ref, nv_ref) = refs[:n], refs[n:]
        g = p_refs[0][...]
        for p_ref in p_refs[1:]:
            g = g + p_ref[...]
        g_ref[...] = g
        d_ref[...], nm_ref[...], nv_ref[...] = _adamw(w_ref[...], g, m_ref[...], v_ref[...])

    spec = pl.BlockSpec((tr, cols), lambda i: (i, 0))
    shape = jax.ShapeDtypeStruct((rows, cols), F32)
    return _pallas(body, [w, *parts, m, v], name=name, grid=(rows // tr,), in_specs=[spec] * (n + 3),
                   out_specs=[spec] * 4, out_shape=[shape] * 4, comm=comm)


def _adamw_small(w, gathered, m, v):
    rows, cols = w.shape

    def body(w_ref, a_ref, m_ref, v_ref, g_ref, d_ref, nm_ref, nv_ref):
        g = a_ref[0:rows, :]
        for dev in range(1, N_DEV):
            g = g + a_ref[dev * rows:(dev + 1) * rows, :]
        g_ref[...] = g
        d_ref[...], nm_ref[...], nv_ref[...] = _adamw(w_ref[...], g, m_ref[...], v_ref[...])

    shape = jax.ShapeDtypeStruct((rows, cols), F32)
    return pl.pallas_call(
        body, name="adamw_small", out_shape=[shape] * 4,
        compiler_params=pltpu.CompilerParams(vmem_limit_bytes=VMEM_LIMIT))(w, gathered, m, v)


PACK_TILE = 8 * LANES


def _pack_small(vals, last_row):
    rows = []
    for name in SMALL:
        flat = vals[name].reshape(-1).astype(F32)
        rows.append(jnp.pad(flat, (0, -flat.size % PACK_TILE)).reshape(-1, LANES))
    rows.append(jnp.pad(last_row, ((0, 7), (0, 0))))
    return jnp.concatenate(rows, axis=0)


def _unpack_small(packed, like):
    out, r = {}, 0
    for name in SMALL:
        size = like[name].size
        out[name] = packed[r:r + -(-size // LANES)].reshape(-1)[:size].reshape(like[name].shape)
        r += 8 * -(-size // PACK_TILE)
    return out, packed[r, 0]


def _shard2d(name, v):
    v = v.reshape(v.shape[-2:])
    return v.T if name in FFN_T else v


def _unshard(name, v, shape):
    return (v.T if name in FFN_T else v).reshape(shape)


def _train_step(x3, tgt3, wts, ms, vs):
    x, tgt = x3[0], tgt3[0]
    L, D = x.shape
    row = lambda v: v.reshape(1, -1)
    shards = {k: _shard2d(k, wts[k]) for k in SHARDED}
    sends = {k: shards[k] if k == "conv_w_dw" else _cast_bf16(shards[k], "cast_" + k) for k in SHARDED}
    gat = {k: _Gather(sends[k]) for k in SHARDED}
    w = lambda k: gat[k].result[0]

    s = {k: wts[k] for k in SMALL}
    lr, li = s["s5_lam_re"].reshape(1, S5_N), s["s5_lam_im"].reshape(1, S5_N)
    ldt = jnp.repeat(s["s5_log_dt"].reshape(S5_G), S5_P).reshape(1, S5_N)
    brc, bic = _compact_b(s["s5_b_re"].reshape(S5_G, S5_P, S5_GC)), _compact_b(s["s5_b_im"].reshape(S5_G, S5_P, S5_GC))
    crc = _compact_b(s["s5_c_re"].reshape(S5_G, S5_GC, S5_P).transpose(0, 2, 1)).T
    cic = _compact_b(s["s5_c_im"].reshape(S5_G, S5_GC, S5_P).transpose(0, 2, 1)).T
    d_skip, b_glu = row(s["s5_d"]), row(s["s5_b_glu"])
    b_dw, ln_g, ln_b = row(s["conv_b_dw"]), row(s["conv_ln_g"]), row(s["conv_ln_b"])
    g1, gm, g2, gf = row(s["ffn1_norm"]), row(s["mix_norm"]), row(s["ffn2_norm"]), row(s["final_norm"])
    mavg = jnp.asarray(_HEAD_MEAN, dtype=BF16)

    h1 = _rms_fwd(x, g1, "rms1", comm=[gat["ffn1_w_gate"], gat["ffn1_w_up"]])
    a1, b1, act1 = _ffn_up(h1, w("ffn1_w_gate"), w("ffn1_w_up"), "ffn1_up", comm=[gat["ffn1_w_down"]])
    x1, h2 = _ffn_down(act1, w("ffn1_w_down"), x, gm, "ffn1_down",
                       comm=[gat["w_in"], gat["s5_w_glu"], gat["conv_w_dw"], gat["w_out"]])
    u = _mm_grouped(h2, w("w_in"), "in_proj")
    are, aim, bre, bim, cre, cim = _s5_params_fwd(lr, li, ldt, brc, bic, crc, cic)
    are_t, aim_t = are.reshape(S5_TILES, LANES), aim.reshape(S5_TILES, LANES)
    w_glu = w("s5_w_glu").reshape(S5_W, S5_W)
    *bounds, y_pre, o_s5 = _s5_fwd(u, are_t, aim_t, bre, bim, cre, cim, d_skip, w_glu, b_glu,
                                   comm=[gat["ffn2_w_gate"], gat["ffn2_w_up"]])
    w_dw = w("conv_w_dw").transpose(1, 0, 2).reshape(CONV_K, CONV_W)
    zc, cat = _conv_fwd(u, o_s5, w_dw, b_dw, ln_g, ln_b, mavg, comm=[gat["ffn2_w_down"]])
    w_out = w("w_out").reshape(-1, D)
    x2, h3 = _mix_out(cat, w_out, x1, g2, "mix_out")
    a2, b2, act2 = _ffn_up(h3, w("ffn2_w_gate"), w("ffn2_w_up"), "ffn2_up")
    dx3, dx3b, loss_part, d_gf = _ffn_down_loss(act2, w("ffn2_w_down"), x2, gf, tgt, "ffn2_down_loss")

    gs, sc, waiting = {"final_norm": d_gf}, {}, []

    def grad(key, g):
        if g.shape[1] % (2 * BF16_ROWS) == 0 and g.dtype == BF16:
            waiting.append(_SwapHalf(g, key))
        else:
            sc[key] = _Scatter(g)
            waiting.append(sc[key])

    def carry(call, *args, **kw):
        ops = list(waiting)
        waiting.clear()
        res = call(*args, comm=ops, **kw)
        for op in ops:
            if isinstance(op, _SwapHalf):
                sc[op.key] = _Scatter(_add_halves(*op.result, "add_" + op.key))
                waiting.append(sc[op.key])
        return res

    da2, db2 = _ffn_bwd_act(dx3b, w("ffn2_w_down"), a2, b2, "ffn2_bwd_act")
    grad("ffn2_w_down", _mm_tn(act2, dx3b[None], 0.5, "ffn2_dwd", N_CHIPS))
    grad("ffn2_w_gate", carry(_mm_tn, da2, h3[None], 1.0, "ffn2_dwg", N_CHIPS))
    grad("ffn2_w_up", carry(_mm_tn, db2, h3[None], 1.0, "ffn2_dwu", N_CHIPS))
    dx2, dx2b, gs["ffn2_norm"] = carry(_mm_rmsbwd, [da2, db2], [w("ffn2_w_gate"), w("ffn2_w_up")], False, False, x2, g2,
                                       dx3, "ffn2_bwd_dx")

    dm = _mm_nt(dx2b, w_out, "mix_bwd")
    grad("w_out", _mm_tn(cat[None], dx2b[None], 1.0, "dwout", 1).reshape(N_CHIPS, -1, D))
    (du_s5, d_wglu, gs["s5_b_glu"], gs["s5_d"], d_crc, d_cic, d_bre, d_bim, d_are, d_aim) = carry(
        _s5_bwd, dm, y_pre, u, bounds, are_t, aim_t, bre, bim, cre, cim, d_skip, w_glu, b_glu)
    grad("s5_w_glu", d_wglu.astype(BF16).reshape(N_CHIPS, -1, S5_W))
    g_lr, g_li, g_ldt, g_brc, g_bic = _s5_params_bwd(lr, li, ldt, brc, bic, d_are.reshape(1, S5_N),
                                                     d_aim.reshape(1, S5_N), d_bre, d_bim)
    gs["s5_lam_re"], gs["s5_lam_im"] = g_lr, g_li
    gs["s5_log_dt"] = jnp.sum(g_ldt.reshape(S5_G, S5_P), axis=1)
    gs["s5_b_re"], gs["s5_b_im"] = _uncompact_b(g_brc), _uncompact_b(g_bic)
    gs["s5_c_re"] = _uncompact_b(d_crc.T).transpose(0, 2, 1)
    gs["s5_c_im"] = _uncompact_b(d_cic.T).transpose(0, 2, 1)
    du, d_wdw, gs["conv_b_dw"], gs["conv_ln_g"], gs["conv_ln_b"] = carry(_conv_bwd, dm, zc, u, du_s5, w_dw, ln_g, ln_b, mavg)
    d_wdw = jnp.sum(d_wdw.reshape(HALO, 8, CONV_W), axis=1)[:CONV_K]
    grad("conv_w_dw", d_wdw.reshape(CONV_K, N_CHIPS, -1).transpose(1, 0, 2))
    grad("w_in", carry(_mm_tn, h2[None], du, 1.0, "dwin", N_CHIPS, b_cols=True))
    dx1, dx1b, gs["mix_norm"] = carry(_mm_rmsbwd, [du], [w("w_in")], True, True, x1, gm, dx2, "in_proj_bwd")

    da1, db1 = carry(_ffn_bwd_act, dx1b, w("ffn1_w_down"), a1, b1, "ffn1_bwd_act")
    grad("ffn1_w_down", _mm_tn(act1, dx1b[None], 0.5, "ffn1_dwd", N_CHIPS))
    grad("ffn1_w_gate", carry(_mm_tn, da1, h1[None], 1.0, "ffn1_dwg", N_CHIPS))
    grad("ffn1_w_up", carry(_mm_tn, db1, h1[None], 1.0, "ffn1_dwu", N_CHIPS))
    grad_x, _, gs["ffn1_norm"] = carry(_mm_rmsbwd, [da1, db1], [w("ffn1_w_gate"), w("ffn1_w_up")], False, False, x, g1,
                                       dx1, "ffn1_bwd_dx")

    out = {}
    gsmall = {k: gs[k].reshape(wts[k].shape) for k in SMALL}
    zero_row = jnp.zeros((1, LANES), F32)
    g_all = _gather_all(_pack_small(gsmall, loss_part))
    res = _adamw_small(_pack_small(s, zero_row), g_all, _pack_small({k: ms[k] for k in SMALL}, zero_row),
                       _pack_small({k: vs[k] for k in SMALL}, zero_row))
    unpacked = [_unpack_small(r, s) for r in res]
    loss = unpacked[0][1]
    for k in SMALL:
        out[k] = [u_[0][k] for u_ in unpacked]

    order = ("ffn2_w_down", "ffn2_w_gate", "ffn2_w_up", "w_out", "s5_w_glu", "conv_w_dw", "w_in", "ffn1_w_down",
             "ffn1_w_gate", "ffn1_w_up")
    back = {}
    for k in order:
        part = carry(_sum_slots, sc[k].result[0], "sum_" + k)
        back[k] = _SwapBack(part) if part.shape != shards[k].shape else _Swap(part)
        waiting.append(back[k])
    for k in order:
        parts = [back[k].result[0]] if isinstance(back[k], _SwapBack) else [back[k].ins[0], back[k].result[0]]
        res = carry(_adamw_sharded, shards[k], parts, _shard2d(k, ms[k]), _shard2d(k, vs[k]), "adamw_" + k)
        out[k] = [_unshard(k, r, wts[k].shape) for r in res]
    return loss, grad_x[None], out


def kernel(x, ffn1_norm, ffn1_w_gate, ffn1_w_up, ffn1_w_down, mix_norm, w_in, s5_lam_re, s5_lam_im, s5_log_dt, s5_b_re, s5_b_im, s5_c_re, s5_c_im, s5_d, s5_w_glu, s5_b_glu, conv_w_dw, conv_b_dw, conv_ln_g, conv_ln_b, w_out, ffn2_norm, ffn2_w_gate, ffn2_w_up, ffn2_w_down, final_norm, loss_target, m_ffn1_norm, m_ffn1_w_gate, m_ffn1_w_up, m_ffn1_w_down, m_mix_norm, m_w_in, m_s5_lam_re, m_s5_lam_im, m_s5_log_dt, m_s5_b_re, m_s5_b_im, m_s5_c_re, m_s5_c_im, m_s5_d, m_s5_w_glu, m_s5_b_glu, m_conv_w_dw, m_conv_b_dw, m_conv_ln_g, m_conv_ln_b, m_w_out, m_ffn2_norm, m_ffn2_w_gate, m_ffn2_w_up, m_ffn2_w_down, m_final_norm, v_ffn1_norm, v_ffn1_w_gate, v_ffn1_w_up, v_ffn1_w_down, v_mix_norm, v_w_in, v_s5_lam_re, v_s5_lam_im, v_s5_log_dt, v_s5_b_re, v_s5_b_im, v_s5_c_re, v_s5_c_im, v_s5_d, v_s5_w_glu, v_s5_b_glu, v_conv_w_dw, v_conv_b_dw, v_conv_ln_g, v_conv_ln_b, v_w_out, v_ffn2_norm, v_ffn2_w_gate, v_ffn2_w_up, v_ffn2_w_down, v_final_norm):
    given = dict(locals())
    wts = {k: given[k] for k in WEIGHTS}
    ms = {k: given["m_" + k] for k in WEIGHTS}
    vs = {k: given["v_" + k] for k in WEIGHTS}
    loss, grad_x, out = _train_step(x, loss_target, wts, ms, vs)
    return (loss, grad_x, *[out[k][0] for k in WEIGHTS], *[out[k][1] for k in WEIGHTS],
            *[out[k][2] for k in WEIGHTS], *[out[k][3] for k in WEIGHTS])
```

```python
import functools

import jax
import jax.numpy as jnp
import numpy as np
from jax import lax
from jax.experimental import pallas as pl
from jax.experimental.pallas import tpu as pltpu

F32, BF16 = jnp.float32, jnp.bfloat16
MESH = pl.DeviceIdType.MESH

EPS = 1e-6
ADAM_LR, ADAM_B1, ADAM_B2, ADAM_EPS, ADAM_WD, ADAM_STEP = 0.001, 0.9, 0.999, 1e-08, 0.01, 10

N_CHIPS = 4
N_DEV = 8
LANES = 128
BF16_ROWS = 16
S5_W, S5_G, S5_GC, S5_P = 512, 32, 16, 64
S5_N = S5_G * S5_P
S5_TILES = S5_N // LANES
S5_HALF = 8
CONV_W, CONV_K, CONV_HD = 512, 31, 64
HALO = 32
CONV_SB = 32
TM = 512
TMS = 1024
TK = 2048
TS = 256
VMEM_LIMIT = 48 << 20
GELU_C0, GELU_C1 = 0.7978845608028654, 0.044715

FFN_T = ("ffn1_w_gate", "ffn1_w_up", "ffn2_w_gate", "ffn2_w_up")
SHARDED = ("ffn1_w_gate", "ffn1_w_up", "ffn1_w_down", "w_in", "s5_w_glu", "conv_w_dw", "w_out",
           "ffn2_w_gate", "ffn2_w_up", "ffn2_w_down")
SMALL = ("ffn1_norm", "mix_norm", "s5_lam_re", "s5_lam_im", "s5_log_dt", "s5_b_re", "s5_b_im", "s5_c_re",
         "s5_c_im", "s5_d", "s5_b_glu", "conv_b_dw", "conv_ln_g", "conv_ln_b", "ffn2_norm", "final_norm")
WEIGHTS = ("ffn1_norm", "ffn1_w_gate", "ffn1_w_up", "ffn1_w_down", "mix_norm", "w_in", "s5_lam_re", "s5_lam_im",
           "s5_log_dt", "s5_b_re", "s5_b_im", "s5_c_re", "s5_c_im", "s5_d", "s5_w_glu", "s5_b_glu", "conv_w_dw",
           "conv_b_dw", "conv_ln_g", "conv_ln_b", "w_out", "ffn2_norm", "ffn2_w_gate", "ffn2_w_up", "ffn2_w_down",
           "final_norm")


def _dot(a, b):
    return jnp.dot(a, b, preferred_element_type=F32)


def _dot_nt(a, b):
    return lax.dot_general(a, b, (((1,), (1,)), ((), ())), preferred_element_type=F32)


def _dot_tn(a, b):
    return lax.dot_general(a, b, (((0,), (0,)), ((), ())), preferred_element_type=F32)


def _colsum(v):
    return jnp.sum(v, axis=0, keepdims=True)


def _sigmoid(v):
    return 1.0 / (1.0 + jnp.exp(-v))


def _accumulate(ref, first, value):
    @pl.when(first)
    def _():
        ref[...] = value

    @pl.when(jnp.logical_not(first))
    def _():
        ref[...] += value


def _position():
    x, y, c = lax.axis_index("x"), lax.axis_index("y"), lax.axis_index("c")
    return x, y, c, [(1 - x, y), (x, 1 - y), (1 - x, 1 - y)]


def _remote(src, dst, sems, send, recv, device):
    return pltpu.make_async_remote_copy(src_ref=src, dst_ref=dst, send_sem=sems.at[send], recv_sem=sems.at[recv],
                                        device_id=device, device_id_type=MESH)


class _Gather:
    def __init__(self, shard):
        self.ins = [shard]
        self.outs = [jax.ShapeDtypeStruct((N_CHIPS,) + shard.shape, shard.dtype)]
        self.rows = shard.shape[0]
        self.halve = shard.dtype == BF16 and self.rows % (2 * BF16_ROWS) == 0
        self.n_sem = 13 if self.halve else 7
        self.result = None

    def _copies(self, ins, outs, sems, s0, pos):
        x, y, c, chips = pos
        src, dst = ins[0], outs[0]
        me = 2 * x + y
        if self.halve:
            hr = self.rows // 2
            mine, theirs = pl.ds(c * hr, hr), pl.ds((1 - c) * hr, hr)
            part = lambda slot, rows: dst.at[slot, rows]
            my_src = src.at[mine]
        else:
            mine = theirs = None
            part = lambda slot, rows: dst.at[slot]
            my_src = src
        slot = lambda j: 2 * chips[j][0] + chips[j][1]
        local = lambda: pltpu.make_async_copy(src, dst.at[me], sems.at[s0])
        send = lambda j: _remote(my_src, part(me, mine), sems, s0 + 1 + j, s0 + 4 + j, (*chips[j], c))
        land = lambda j: _remote(my_src, part(slot(j), mine), sems, s0 + 1 + j, s0 + 4 + j, (*chips[j], c))
        fwd = lambda j: _remote(part(slot(j), mine), part(slot(j), mine), sems, s0 + 7 + j, s0 + 10 + j, (x, y, 1 - c))
        got = lambda j: _remote(part(slot(j), theirs), part(slot(j), theirs), sems, s0 + 7 + j, s0 + 10 + j,
                                (x, y, 1 - c))
        return local, send, land, fwd, got

    def start(self, ins, outs, sems, s0, pos):
        local, send, _, _, _ = self._copies(ins, outs, sems, s0, pos)
        local().start()
        for j in range(N_CHIPS - 1):
            send(j).start()

    def finish(self, ins, outs, sems, s0, pos):
        local, send, land, fwd, got = self._copies(ins, outs, sems, s0, pos)
        others = range(N_CHIPS - 1)
        for j in others:
            land(j).wait_recv()
            if self.halve:
                fwd(j).start()
        for j in others:
            if self.halve:
                got(j).wait_recv()
        for j in others:
            send(j).wait_send()
            if self.halve:
                fwd(j).wait_send()
        local().wait()


class _Scatter:
    def __init__(self, grad):
        self.ins = [grad]
        self.outs = [jax.ShapeDtypeStruct(grad.shape, grad.dtype)]
        self.n_sem = 7
        self.result = None

    def _copies(self, ins, outs, sems, s0, pos):
        x, y, c, chips = pos
        src, dst = ins[0], outs[0]
        me = 2 * x + y
        slot = lambda j: 2 * chips[j][0] + chips[j][1]
        local = lambda: pltpu.make_async_copy(src.at[me], dst.at[me], sems.at[s0])
        send = lambda j: _remote(src.at[slot(j)], dst.at[me], sems, s0 + 1 + j, s0 + 4 + j, (*chips[j], c))
        land = lambda j: _remote(src.at[me], dst.at[slot(j)], sems, s0 + 1 + j, s0 + 4 + j, (*chips[j], c))
        return local, send, land

    def start(self, ins, outs, sems, s0, pos):
        local, send, _ = self._copies(ins, outs, sems, s0, pos)
        local().start()
        for j in range(N_CHIPS - 1):
            send(j).start()

    def finish(self, ins, outs, sems, s0, pos):
        local, send, land = self._copies(ins, outs, sems, s0, pos)
        for j in range(N_CHIPS - 1):
            land(j).wait_recv()
        for j in range(N_CHIPS - 1):
            send(j).wait_send()
        local().wait()


class _Swap:
    def __init__(self, part):
        self.ins = [part]
        self.outs = [jax.ShapeDtypeStruct(part.shape, part.dtype)]
        self.n_sem = 2
        self.result = None

    def _copy(self, ins, outs, sems, s0, pos):
        x, y, c, _ = pos
        return _remote(ins[0], outs[0], sems, s0, s0 + 1, (x, y, 1 - c))

    def start(self, ins, outs, sems, s0, pos):
        self._copy(ins, outs, sems, s0, pos).start()

    def finish(self, ins, outs, sems, s0, pos):
        self._copy(ins, outs, sems, s0, pos).wait()


class _SwapHalf:
    def __init__(self, grad, key):
        slots, rows, cols = grad.shape
        half = jax.ShapeDtypeStruct((slots, rows // 2, cols), grad.dtype)
        self.ins, self.outs, self.key = [grad], [half, half], key
        self.hr = rows // 2
        self.n_sem = 3
        self.result = None

    def _copies(self, ins, outs, sems, s0, pos):
        x, y, c, _ = pos
        mine, theirs = pl.ds(c * self.hr, self.hr), pl.ds((1 - c) * self.hr, self.hr)
        local = pltpu.make_async_copy(ins[0].at[:, mine], outs[0], sems.at[s0])
        remote = _remote(ins[0].at[:, theirs], outs[1], sems, s0 + 1, s0 + 2, (x, y, 1 - c))
        return local, remote

    def start(self, ins, outs, sems, s0, pos):
        for cp in self._copies(ins, outs, sems, s0, pos):
            cp.start()

    def finish(self, ins, outs, sems, s0, pos):
        for cp in self._copies(ins, outs, sems, s0, pos):
            cp.wait()


class _SwapBack:
    def __init__(self, part):
        hr, cols = part.shape
        self.ins, self.outs = [part], [jax.ShapeDtypeStruct((2 * hr, cols), part.dtype)]
        self.hr = hr
        self.n_sem = 3
        self.result = None

    def _copies(self, ins, outs, sems, s0, pos):
        x, y, c, _ = pos
        mine, theirs = pl.ds(c * self.hr, self.hr), pl.ds((1 - c) * self.hr, self.hr)
        local = lambda: pltpu.make_async_copy(ins[0], outs[0].at[mine], sems.at[s0])
        send = lambda: _remote(ins[0], outs[0].at[mine], sems, s0 + 1, s0 + 2, (x, y, 1 - c))
        land = lambda: _remote(ins[0], outs[0].at[theirs], sems, s0 + 1, s0 + 2, (x, y, 1 - c))
        return local, send, land

    def start(self, ins, outs, sems, s0, pos):
        local, send, _ = self._copies(ins, outs, sems, s0, pos)
        local().start()
        send().start()

    def finish(self, ins, outs, sems, s0, pos):
        local, send, land = self._copies(ins, outs, sems, s0, pos)
        land().wait_recv()
        send().wait_send()
        local().wait()


def _pallas(body, args, *, name, grid, in_specs, out_specs, out_shape, scratch_shapes=(), comm=()):
    comm = list(comm)
    n_in, n_out, n_scr = len(in_specs), len(out_specs), len(scratch_shapes)
    c_in = [a for op in comm for a in op.ins]
    c_out = [s for op in comm for s in op.outs]
    n_sem = sum(op.n_sem for op in comm)

    def full(*refs):
        o0 = n_in + len(c_in)
        s0 = o0 + n_out + len(c_out)
        ins, cin = refs[:n_in], refs[n_in:o0]
        outs, cout = refs[o0:o0 + n_out], refs[o0 + n_out:s0]
        scratch = refs[s0:s0 + n_scr]
        if comm:
            sems = refs[s0 + n_scr]
            ids = [pl.program_id(d) for d in range(len(grid))]
            first = functools.reduce(jnp.logical_and, [i == 0 for i in ids])
            last = functools.reduce(jnp.logical_and, [i == g - 1 for i, g in zip(ids, grid)])
            pos = _position()

            def each(step):
                ci = co = cs = 0
                for op in comm:
                    getattr(op, step)(cin[ci:ci + len(op.ins)], cout[co:co + len(op.outs)], sems, cs, pos)
                    ci, co, cs = ci + len(op.ins), co + len(op.outs), cs + op.n_sem

            @pl.when(first)
            def _():
                each("start")

        body(*ins, *outs, *scratch)
        if comm:
            @pl.when(last)
            def _():
                each("finish")

    hbm = pl.BlockSpec(memory_space=pl.ANY)
    res = pl.pallas_call(
        full, name=name, grid=grid,
        in_specs=list(in_specs) + [hbm] * len(c_in), out_specs=list(out_specs) + [hbm] * len(c_out),
        out_shape=list(out_shape) + c_out,
        scratch_shapes=list(scratch_shapes) + ([pltpu.SemaphoreType.DMA((n_sem,))] if comm else []),
        compiler_params=pltpu.CompilerParams(dimension_semantics=("arbitrary",) * len(grid),
                                             vmem_limit_bytes=VMEM_LIMIT))(*args, *c_in)
    k = n_out
    for op in comm:
        op.result = list(res[k:k + len(op.outs)])
        k += len(op.outs)
    return list(res[:n_out])


def _row_tile(rows, cols, itemsize=4, budget=1 << 20):
    t = rows
    while t % (2 * BF16_ROWS) == 0 and t * cols * itemsize > budget:
        t //= 2
    return t


def _cast_bf16(w, name):
    rows, cols = w.shape
    tr = _row_tile(rows, cols)

    def body(w_ref, o_ref):
        o_ref[...] = w_ref[...].astype(BF16)

    spec = pl.BlockSpec((tr, cols), lambda i: (i, 0))
    return _pallas(body, [w], name=name, grid=(rows // tr,), in_specs=[spec], out_specs=[spec],
                   out_shape=[jax.ShapeDtypeStruct((rows, cols), BF16)])[0]


def _rms_fwd(x, g, name, comm=()):
    L, D = x.shape

    def body(x_ref, g_ref, h_ref):
        xf = x_ref[...]
        r = lax.rsqrt(jnp.mean(xf * xf, axis=-1, keepdims=True) + EPS)
        h_ref[...] = (xf * r * g_ref[...]).astype(BF16)

    row = pl.BlockSpec((TMS, D), lambda i: (i, 0))
    return _pallas(body, [x, g], name=name, grid=(L // TMS,),
                   in_specs=[row, pl.BlockSpec((1, D), lambda i: (0, 0))], out_specs=[row],
                   out_shape=[jax.ShapeDtypeStruct((L, D), BF16)], comm=comm)[0]


def _resident(shape):
    return pl.BlockSpec(shape, lambda *_: (0,) * len(shape), pipeline_mode=pl.Buffered(1))


def _ffn_up(h, wg_t, wu_t, name, comm=()):
    L, D = h.shape
    G, FS, _ = wg_t.shape

    def body(h_ref, wg_ref, wu_ref, a_ref, b_ref, act_ref):
        j = pl.program_id(1)
        hv = h_ref[...]
        a = _dot_nt(hv, wg_ref[j])
        b = _dot_nt(hv, wu_ref[j])
        a_ref[...] = a.astype(BF16)
        b_ref[...] = b.astype(BF16)
        act_ref[...] = (a * _sigmoid(a) * b).astype(BF16)

    ospec = pl.BlockSpec((None, TMS, FS), lambda i, j: (j, i, 0))
    oshape = jax.ShapeDtypeStruct((G, L, FS), BF16)
    return _pallas(body, [h, wg_t, wu_t], name=name, grid=(L // TMS, G),
                   in_specs=[pl.BlockSpec((TMS, D), lambda i, j: (i, 0)), _resident((G, FS, D)), _resident((G, FS, D))],
                   out_specs=[ospec, ospec, ospec], out_shape=[oshape, oshape, oshape], comm=comm)


def _group_sum(a_ref, w_ref, groups, mm=_dot):
    acc = mm(a_ref[0], w_ref[0])
    for j in range(1, groups):
        acc = acc + mm(a_ref[j], w_ref[j])
    return acc


def _ffn_down(act, wd, x, g_next, name, comm=()):
    G, L, FS = act.shape
    D = wd.shape[2]

    def body(act_ref, wd_ref, x_ref, g_ref, xn_ref, hn_ref):
        xn = x_ref[...] + 0.5 * _group_sum(act_ref, wd_ref, G)
        xn_ref[...] = xn
        r = lax.rsqrt(jnp.mean(xn * xn, axis=-1, keepdims=True) + EPS)
        hn_ref[...] = (xn * r * g_ref[...]).astype(BF16)

    row = pl.BlockSpec((TM, D), lambda i: (i, 0))
    return _pallas(body, [act, wd, x, g_next], name=name, grid=(L // TM,),
                   in_specs=[pl.BlockSpec((G, TM, FS), lambda i: (0, i, 0)), _resident((G, FS, D)), row,
                             pl.BlockSpec((1, D), lambda i: (0, 0))],
                   out_specs=[row, row],
                   out_shape=[jax.ShapeDtypeStruct((L, D), F32), jax.ShapeDtypeStruct((L, D), BF16)], comm=comm)


def _ffn_down_loss(act, wd, x, gf, tgt, name):
    G, L, FS = act.shape
    D = wd.shape[2]

    def body(act_ref, wd_ref, x_ref, g_ref, t_ref, dx_ref, dxb_ref, loss_ref, dg_ref):
        i = pl.program_id(0)
        xn = x_ref[...] + 0.5 * _group_sum(act_ref, wd_ref, G)
        r = lax.rsqrt(jnp.mean(xn * xn, axis=-1, keepdims=True) + EPS)
        xh = xn * r
        gv = g_ref[...]
        e = xh * gv - t_ref[...]
        part = 0.5 * jnp.sum(_colsum(e * e), axis=1, keepdims=True) / D
        dy = e / D
        _accumulate(loss_ref, i == 0, jnp.broadcast_to(part, (1, LANES)))
        _accumulate(dg_ref, i == 0, _colsum(dy * xh))
        dxh = dy * gv
        dx = r * (dxh - xh * jnp.mean(dxh * xh, axis=-1, keepdims=True))
        dx_ref[...] = dx
        dxb_ref[...] = dx.astype(BF16)

    row = pl.BlockSpec((TM, D), lambda i: (i, 0))
    return _pallas(body, [act, wd, x, gf, tgt], name=name, grid=(L // TM,),
                   in_specs=[pl.BlockSpec((G, TM, FS), lambda i: (0, i, 0)), _resident((G, FS, D)), row,
                             pl.BlockSpec((1, D), lambda i: (0, 0)), row],
                   out_specs=[row, row, pl.BlockSpec((1, LANES), lambda i: (0, 0)),
                              pl.BlockSpec((1, D), lambda i: (0, 0))],
                   out_shape=[jax.ShapeDtypeStruct((L, D), F32), jax.ShapeDtypeStruct((L, D), BF16),
                              jax.ShapeDtypeStruct((1, LANES), F32), jax.ShapeDtypeStruct((1, D), F32)])


def _ffn_bwd_act(dxb, wd, a, b, name, comm=()):
    L, D = dxb.shape
    G, FS, _ = wd.shape

    def body(dx_ref, wd_ref, a_ref, b_ref, da_ref, db_ref):
        dact = 0.5 * _dot_nt(dx_ref[...], wd_ref[pl.program_id(1)])
        av = a_ref[...].astype(F32)
        bv = b_ref[...].astype(F32)
        sg = _sigmoid(av)
        da_ref[...] = (dact * bv * sg * (1.0 + av * (1.0 - sg))).astype(BF16)
        db_ref[...] = (dact * av * sg).astype(BF16)

    gspec = pl.BlockSpec((None, TMS, FS), lambda i, j: (j, i, 0))
    oshape = jax.ShapeDtypeStruct((G, L, FS), BF16)
    return _pallas(body, [dxb, wd, a, b], name=name, grid=(L // TMS, G),
                   in_specs=[pl.BlockSpec((TMS, D), lambda i, j: (i, 0)), _resident((G, FS, D)), gspec, gspec],
                   out_specs=[gspec, gspec], out_shape=[oshape, oshape], comm=comm)


def _mm_grouped(a, w, name):
    L, K = a.shape
    G, _, N = w.shape

    def body(a_ref, w_ref, o_ref):
        o_ref[...] = _dot(a_ref[...], w_ref[pl.program_id(1)])

    return _pallas(body, [a, w], name=name, grid=(L // TMS, G),
                   in_specs=[pl.BlockSpec((TMS, K), lambda i, g: (i, 0)), _resident((G, K, N))],
                   out_specs=[pl.BlockSpec((TMS, N), lambda i, g: (i, g))],
                   out_shape=[jax.ShapeDtypeStruct((L, G * N), F32)])[0]


def _mm_nt(a, w, name):
    L, K = a.shape
    N = w.shape[0]

    def body(a_ref, w_ref, o_ref):
        o_ref[...] = _dot_nt(a_ref[...], w_ref[...]).astype(BF16)

    return _pallas(body, [a, w], name=name, grid=(L // TMS,),
                   in_specs=[pl.BlockSpec((TMS, K), lambda i: (i, 0)), _resident((N, K))],
                   out_specs=[pl.BlockSpec((TMS, N), lambda i: (i, 0))],
                   out_shape=[jax.ShapeDtypeStruct((L, N), BF16)])[0]


def _mm_tn(a, b, scale, name, groups, b_cols=False, comm=()):
    L, M = a.shape[1], a.shape[2]
    N = b.shape[1] // groups if b_cols else b.shape[2]
    tk = min(L, TK)
    nk = L // tk

    def spec(v, cols):
        if cols:
            return pl.BlockSpec((tk, v.shape[1] // groups), lambda g, k: (k, g))
        if v.shape[0] > 1:
            return pl.BlockSpec((None, tk, v.shape[2]), lambda g, k: (g, k, 0))
        return pl.BlockSpec((None, tk, v.shape[2]), lambda g, k: (0, k, 0))

    def body(a_ref, b_ref, o_ref, acc):
        k = pl.program_id(1)
        p = _dot_tn(a_ref[...], b_ref[...])
        if nk == 1:
            o_ref[...] = (p * scale).astype(BF16)
        else:
            _accumulate(acc, k == 0, p)

            @pl.when(k == nk - 1)
            def _():
                o_ref[...] = (acc[...] * scale).astype(BF16)

    return _pallas(body, [a, b], name=name, grid=(groups, nk),
                   in_specs=[spec(a, False), spec(b, b_cols)],
                   out_specs=[pl.BlockSpec((None, M, N), lambda g, k: (g, 0, 0))],
                   out_shape=[jax.ShapeDtypeStruct((groups, M, N), BF16)],
                   scratch_shapes=[pltpu.VMEM((M, N), F32)], comm=comm)[0]


def _mm_rmsbwd(a_list, w_list, nt, a_cols, x_in, g, dx_out, name, comm=()):
    P = len(a_list)
    G = w_list[0].shape[0]
    L, D = x_in.shape
    mm = _dot_nt if nt else _dot

    def body(*refs):
        a_refs, w_refs = refs[:P], refs[P:2 * P]
        x_ref, g_ref, dxo_ref, dx_ref, dxb_ref, dg_ref = refs[2 * P:]
        i = pl.program_id(0)
        dh = None
        for a_ref, w_ref in zip(a_refs, w_refs):
            for j in range(G):
                if a_cols:
                    kw = a_ref.shape[1] // G
                    term = mm(a_ref[:, j * kw:(j + 1) * kw], w_ref[j])
                else:
                    term = mm(a_ref[j], w_ref[j])
                dh = term if dh is None else dh + term
        xf = x_ref[...]
        r = lax.rsqrt(jnp.mean(xf * xf, axis=-1, keepdims=True) + EPS)
        xh = xf * r
        _accumulate(dg_ref, i == 0, _colsum(dh * xh))
        dxh = dh * g_ref[...]
        dx = dxo_ref[...] + r * (dxh - xh * jnp.mean(dxh * xh, axis=-1, keepdims=True))
        dx_ref[...] = dx
        dxb_ref[...] = dx.astype(BF16)

    row = pl.BlockSpec((TM, D), lambda i: (i, 0))
    vec = pl.BlockSpec((1, D), lambda i: (0, 0))
    if a_cols:
        a_specs = [pl.BlockSpec((TM, a.shape[1]), lambda i: (i, 0)) for a in a_list]
    else:
        a_specs = [pl.BlockSpec((G, TM, a.shape[2]), lambda i: (0, i, 0)) for a in a_list]
    w_specs = [_resident(w.shape) for w in w_list]
    return _pallas(body, [*a_list, *w_list, x_in, g, dx_out], name=name, grid=(L // TM,),
                   in_specs=a_specs + w_specs + [row, vec, row], out_specs=[row, row, vec],
                   out_shape=[jax.ShapeDtypeStruct((L, D), F32), jax.ShapeDtypeStruct((L, D), BF16),
                              jax.ShapeDtypeStruct((1, D), F32)], comm=comm)


def _mix_out(cat, wout, x1, g_next, name):
    L, K = cat.shape
    D = wout.shape[1]

    def body(c_ref, w_ref, x_ref, g_ref, xn_ref, hn_ref):
        xn = x_ref[...] + _dot(c_ref[...], w_ref[...])
        xn_ref[...] = xn
        r = lax.rsqrt(jnp.mean(xn * xn, axis=-1, keepdims=True) + EPS)
        hn_ref[...] = (xn * r * g_ref[...]).astype(BF16)

    row = pl.BlockSpec((TMS, D), lambda i: (i, 0))
    return _pallas(body, [cat, wout, x1, g_next], name=name, grid=(L // TMS,),
                   in_specs=[pl.BlockSpec((TMS, K), lambda i: (i, 0)), pl.BlockSpec((K, D), lambda i: (0, 0)), row,
                             pl.BlockSpec((1, D), lambda i: (0, 0))],
                   out_specs=[row, row],
                   out_shape=[jax.ShapeDtypeStruct((L, D), F32), jax.ShapeDtypeStruct((L, D), BF16)])


def _s5_disc(lr, li, ldt, brc, bic):
    dt = jnp.exp(ldt)
    mag = jnp.exp(lr * dt)
    are = mag * jnp.cos(li * dt)
    aim = mag * jnp.sin(li * dt)
    den = lr * lr + li * li
    nre = are - 1.0
    fre = (nre * lr + aim * li) / den
    fim = (aim * lr - nre * li) / den
    return are, aim, fre * brc - fim * bic, fre * bic + fim * brc


def _s5_params_fwd(lr, li, ldt, brc, bic, crc, cic):
    def body(lr_ref, li_ref, ldt_ref, br_ref, bi_ref, cr_ref, ci_ref, are_ref, aim_ref, bre_ref, bim_ref, cre_ref, cim_ref):
        are, aim, bre, bim = _s5_disc(lr_ref[...], li_ref[...], ldt_ref[...], br_ref[...], bi_ref[...])
        are_ref[...] = are
        aim_ref[...] = aim
        bre_ref[...] = bre.astype(BF16)
        bim_ref[...] = bim.astype(BF16)
        cre_ref[...] = cr_ref[...].astype(BF16)
        cim_ref[...] = ci_ref[...].astype(BF16)

    vec = jax.ShapeDtypeStruct((1, S5_N), F32)
    return pl.pallas_call(
        body, name="s5_params_fwd",
        out_shape=[vec, vec, jax.ShapeDtypeStruct((LANES, S5_N), BF16), jax.ShapeDtypeStruct((LANES, S5_N), BF16),
                   jax.ShapeDtypeStruct((S5_N, LANES), BF16), jax.ShapeDtypeStruct((S5_N, LANES), BF16)],
        compiler_params=pltpu.CompilerParams(vmem_limit_bytes=VMEM_LIMIT))(lr, li, ldt, brc, bic, crc, cic)


def _s5_params_bwd(lr, li, ldt, brc, bic, dare, daim, dbre, dbim):
    def body(lr_ref, li_ref, ldt_ref, br_ref, bi_ref, dare_ref, daim_ref, dbre_ref, dbim_ref,
             glr_ref, gli_ref, gldt_ref, gbr_ref, gbi_ref):
        _, vjp = jax.vjp(_s5_disc, lr_ref[...], li_ref[...], ldt_ref[...], br_ref[...], bi_ref[...])
        glr, gli, gldt, gbr, gbi = vjp((dare_ref[...], daim_ref[...], dbre_ref[...], dbim_ref[...]))
        glr_ref[...] = glr
        gli_ref[...] = gli
        gldt_ref[...] = gldt
        gbr_ref[...] = gbr
        gbi_ref[...] = gbi

    vec = jax.ShapeDtypeStruct((1, S5_N), F32)
    mat = jax.ShapeDtypeStruct((LANES, S5_N), F32)
    return pl.pallas_call(
        body, name="s5_params_bwd", out_shape=[vec, vec, vec, mat, mat],
        compiler_params=pltpu.CompilerParams(vmem_limit_bytes=VMEM_LIMIT))(lr, li, ldt, brc, bic, dare, daim, dbre, dbim)


def _gelu_parts(y):
    th = jnp.tanh(GELU_C0 * (y + GELU_C1 * y * y * y))
    return 0.5 * y * (1.0 + th), th


def _state_rows(q, T):
    return pl.ds(q % S5_HALF, T, stride=S5_HALF)


def _load_tiles(bufs, ct, T, dtype):
    return jnp.concatenate([bufs[q // S5_HALF][_state_rows(q, T), :].astype(dtype) for q in range(4 * ct, 4 * ct + 4)],
                           axis=1)


def _store_tiles(bufs, ct, T, value):
    for k, q in enumerate(range(4 * ct, 4 * ct + 4)):
        bufs[q // S5_HALF][_state_rows(q, T), :] = value[:, k * LANES:(k + 1) * LANES]


def _s5_fwd(u, are, aim, bre, bim, cre, cim, d_skip, wglu, bglu, comm=()):
    L = u.shape[0]
    T = min(TS, L)
    n = L // T

    def body(u_ref, are_ref, aim_ref, bre_ref, bim_ref, cre_ref, cim_ref, d_ref, wg_ref, bg_ref,
             sre_lo, sre_hi, sim_lo, sim_hi, y_ref, o_ref, st_re, st_im):
        i = pl.program_id(0)
        sre, sim = (sre_lo, sre_hi), (sim_lo, sim_hi)

        @pl.when(i == 0)
        def _():
            st_re[...] = jnp.zeros_like(st_re)
            st_im[...] = jnp.zeros_like(st_im)

        uf = u_ref[...]
        ub = uf.astype(BF16)
        for ct in range(4):
            uq = ub[:, ct * LANES:(ct + 1) * LANES]
            win = slice(4 * ct * LANES, 4 * (ct + 1) * LANES)
            _store_tiles(sre, ct, T, _dot(uq, bre_ref[:, win]))
            _store_tiles(sim, ct, T, _dot(uq, bim_ref[:, win]))
        halves = [slice(h * S5_HALF, (h + 1) * S5_HALF) for h in range(2)]
        a_re = [are_ref[hs, :] for hs in halves]
        a_im = [aim_ref[hs, :] for hs in halves]

        def step(t, carry):
            rows = pl.ds(pl.multiple_of(t * S5_HALF, S5_HALF), S5_HALF)
            out = []
            for h in range(2):
                s_re, s_im = carry[2 * h], carry[2 * h + 1]
                n_re = a_re[h] * s_re - a_im[h] * s_im + sre[h][rows, :]
                n_im = a_re[h] * s_im + a_im[h] * s_re + sim[h][rows, :]
                sre[h][rows, :] = n_re
                sim[h][rows, :] = n_im
                out += [n_re, n_im]
            return tuple(out)

        init = (st_re[halves[0], :], st_im[halves[0], :], st_re[halves[1], :], st_im[halves[1], :])
        fin = lax.fori_loop(0, T, step, init, unroll=4)
        for h in range(2):
            st_re[halves[h], :] = fin[2 * h]
            st_im[halves[h], :] = fin[2 * h + 1]
        tiles = []
        for ct in range(4):
            win = slice(4 * ct * LANES, 4 * (ct + 1) * LANES)
            tiles.append(_dot(_load_tiles(sre, ct, T, BF16), cre_ref[win, :])
                         - _dot(_load_tiles(sim, ct, T, BF16), cim_ref[win, :]))
        y = jnp.concatenate(tiles, axis=1) + d_ref[...] * uf
        y_ref[...] = y
        yg, _ = _gelu_parts(y)
        gate = _sigmoid(_dot(yg.astype(BF16), wg_ref[...]) + bg_ref[...])
        o_ref[...] = (yg * gate).astype(BF16)

    const = lambda shape: pl.BlockSpec(shape, lambda i: (0, 0))
    sspec = pl.BlockSpec((T * S5_HALF, LANES), lambda i: (i, 0))
    sshape = jax.ShapeDtypeStruct((L * S5_HALF, LANES), F32)
    chunk = pl.BlockSpec((T, S5_W), lambda i: (i, 0))
    return _pallas(body, [u, are, aim, bre, bim, cre, cim, d_skip, wglu, bglu], name="s5_fwd", grid=(n,),
                   in_specs=[chunk, const((S5_TILES, LANES)), const((S5_TILES, LANES)),
                             const((LANES, S5_N)), const((LANES, S5_N)), const((S5_N, LANES)), const((S5_N, LANES)),
                             const((1, S5_W)), const((S5_W, S5_W)), const((1, S5_W))],
                   out_specs=[sspec] * 4 + [chunk, chunk],
                   out_shape=[sshape] * 4 + [jax.ShapeDtypeStruct((L, S5_W), F32), jax.ShapeDtypeStruct((L, S5_W), BF16)],
                   scratch_shapes=[pltpu.VMEM((S5_TILES, LANES), F32), pltpu.VMEM((S5_TILES, LANES), F32)], comm=comm)


def _s5_bwd(dm, y_pre, u, states, are, aim, bre, bim, cre, cim, d_skip, wglu, bglu, comm=()):
    L = u.shape[0]
    T = min(TS, L)
    n = L // T

    def body(dm_ref, y_ref, u_ref, sre_lo, sre_hi, sim_lo, sim_hi, pre_lo, pre_hi, pim_lo, pim_hi,
             are_ref, aim_ref, bre_ref, bim_ref, cre_ref, cim_ref, d_ref, wg_ref, bg_ref,
             du_ref, dwg_ref, dbg_ref, dd_ref, dcre_ref, dcim_ref, dbre_ref, dbim_ref, dare_ref, daim_ref,
             gre_lo, gre_hi, gim_lo, gim_hi, car_re, car_im):
        i = pl.program_id(0)
        first = i == 0
        sre, sim = (sre_lo, sre_hi), (sim_lo, sim_hi)
        gre, gim = (gre_lo, gre_hi), (gim_lo, gim_hi)
        pre, pim = (pre_lo, pre_hi), (pim_lo, pim_hi)

        @pl.when(first)
        def _():
            car_re[...] = jnp.zeros_like(car_re)
            car_im[...] = jnp.zeros_like(car_im)
            dcre_ref[...] = jnp.zeros_like(dcre_ref)
            dcim_ref[...] = jnp.zeros_like(dcim_ref)
            dbre_ref[...] = jnp.zeros_like(dbre_ref)
            dbim_ref[...] = jnp.zeros_like(dbim_ref)

        y = y_ref[...]
        uf = u_ref[...]
        yg, th = _gelu_parts(y)
        dgelu = 0.5 * (1.0 + th) + 0.5 * y * (1.0 - th * th) * GELU_C0 * (1.0 + 3.0 * GELU_C1 * y * y)
        ygb = yg.astype(BF16)
        sg = _sigmoid(_dot(ygb, wg_ref[...]) + bg_ref[...])
        dout = dm_ref[...].astype(F32)
        dgp = dout * yg * sg * (1.0 - sg)
        dgpb = dgp.astype(BF16)
        dyg = dout * sg + _dot_nt(dgpb, wg_ref[...])
        _accumulate(dwg_ref, first, _dot_tn(ygb, dgpb))
        _accumulate(dbg_ref, first, _colsum(dgp))
        dy = dyg * dgelu
        _accumulate(dd_ref, first, _colsum(dy * uf))
        dyb = dy.astype(BF16)
        ub = uf.astype(BF16)

        for ct in range(4):
            win = slice(4 * ct * LANES, 4 * (ct + 1) * LANES)
            dyq = dyb[:, ct * LANES:(ct + 1) * LANES]
            dcre_ref[win, :] += _dot_tn(_load_tiles(sre, ct, T, BF16), dyq)
            dcim_ref[win, :] -= _dot_tn(_load_tiles(sim, ct, T, BF16), dyq)
            _store_tiles(gre, ct, T, _dot_nt(dyq, cre_ref[win, :]))
            _store_tiles(gim, ct, T, -_dot_nt(dyq, cim_ref[win, :]))

        halves = [slice(h * S5_HALF, (h + 1) * S5_HALF) for h in range(2)]
        a_re = [are_ref[hs, :] for hs in halves]
        a_im = [aim_ref[hs, :] for hs in halves]

        def adjoint(t, h, g_re, g_im):
            rows = pl.ds(pl.multiple_of(t * S5_HALF, S5_HALF), S5_HALF)
            n_re = gre[h][rows, :] + a_re[h] * g_re + a_im[h] * g_im
            n_im = gim[h][rows, :] + a_re[h] * g_im - a_im[h] * g_re
            gre[h][rows, :] = n_re
            gim[h][rows, :] = n_im
            return n_re, n_im

        def step(k, carry):
            t = T - 1 - k
            prev = pl.ds(pl.multiple_of((t - 1) * S5_HALF, S5_HALF), S5_HALF)
            out = []
            for h in range(2):
                g_re, g_im, da_re, da_im = carry[4 * h:4 * h + 4]
                g_re, g_im = adjoint(t, h, g_re, g_im)
                p_re, p_im = sre[h][prev, :], sim[h][prev, :]
                out += [g_re, g_im, da_re + g_re * p_re + g_im * p_im, da_im + g_im * p_re - g_re * p_im]
            return tuple(out)

        zero = jnp.zeros((S5_HALF, LANES), F32)
        init = (car_re[halves[0], :], car_im[halves[0], :], zero, zero, car_re[halves[1], :], car_im[halves[1], :], zero, zero)
        fin = lax.fori_loop(0, T - 1, step, init, unroll=4)
        keep = (i < n - 1).astype(F32)
        for h in range(2):
            g_re, g_im, da_re, da_im = fin[4 * h:4 * h + 4]
            g_re, g_im = adjoint(0, h, g_re, g_im)
            p_re, p_im = pre[h][...] * keep, pim[h][...] * keep
            car_re[halves[h], :] = g_re
            car_im[halves[h], :] = g_im
            da_re = da_re + g_re * p_re + g_im * p_im
            da_im = da_im + g_im * p_re - g_re * p_im

            @pl.when(first)
            def _():
                dare_ref[halves[h], :] = da_re
                daim_ref[halves[h], :] = da_im

            @pl.when(jnp.logical_not(first))
            def _():
                dare_ref[halves[h], :] += da_re
                daim_ref[halves[h], :] += da_im

        tiles = []
        for ct in range(4):
            uq = ub[:, ct * LANES:(ct + 1) * LANES]
            win = slice(4 * ct * LANES, 4 * (ct + 1) * LANES)
            g_re, g_im = _load_tiles(gre, ct, T, BF16), _load_tiles(gim, ct, T, BF16)
            tiles.append(d_ref[:, ct * LANES:(ct + 1) * LANES] * dy[:, ct * LANES:(ct + 1) * LANES]
                         + _dot_nt(g_re, bre_ref[:, win]) + _dot_nt(g_im, bim_ref[:, win]))
            dbre_ref[:, win] += _dot_tn(uq, g_re)
            dbim_ref[:, win] += _dot_tn(uq, g_im)
        du_ref[...] = jnp.concatenate(tiles, axis=1).astype(BF16)

    rev = lambda i: (n - 1 - i, 0)
    const = lambda shape: pl.BlockSpec(shape, lambda i: (0, 0))
    chunk = pl.BlockSpec((T, S5_W), rev)
    sspec = pl.BlockSpec((T * S5_HALF, LANES), rev)
    pspec = pl.BlockSpec((S5_HALF, LANES), lambda i: (jnp.maximum((n - 1 - i) * T - 1, 0), 0))
    tile = jax.ShapeDtypeStruct((S5_TILES, LANES), F32)
    vec = jax.ShapeDtypeStruct((1, S5_W), F32)
    sbuf = pltpu.VMEM((T * S5_HALF, LANES), F32)
    return _pallas(
        body, [dm, y_pre, u, *states, *states, are, aim, bre, bim, cre, cim, d_skip, wglu, bglu],
        name="s5_bwd", grid=(n,),
        in_specs=[chunk, chunk, chunk] + [sspec] * 4 + [pspec] * 4 + [
            const((S5_TILES, LANES)), const((S5_TILES, LANES)), const((LANES, S5_N)), const((LANES, S5_N)),
            const((S5_N, LANES)), const((S5_N, LANES)), const((1, S5_W)), const((S5_W, S5_W)), const((1, S5_W))],
        out_specs=[chunk, const((S5_W, S5_W)), const((1, S5_W)), const((1, S5_W)), const((S5_N, LANES)),
                   const((S5_N, LANES)), const((LANES, S5_N)), const((LANES, S5_N)), const((S5_TILES, LANES)),
                   const((S5_TILES, LANES))],
        out_shape=[jax.ShapeDtypeStruct((L, S5_W), BF16), jax.ShapeDtypeStruct((S5_W, S5_W), F32), vec, vec,
                   jax.ShapeDtypeStruct((S5_N, LANES), F32), jax.ShapeDtypeStruct((S5_N, LANES), F32),
                   jax.ShapeDtypeStruct((LANES, S5_N), F32), jax.ShapeDtypeStruct((LANES, S5_N), F32), tile, tile],
        scratch_shapes=[sbuf, sbuf, sbuf, sbuf, pltpu.VMEM((S5_TILES, LANES), F32), pltpu.VMEM((S5_TILES, LANES), F32)],
        comm=comm)


_EYE8 = np.eye(8, dtype=np.float32)


def _compact_b(b):
    return jnp.einsum("akpc,kj->jcakp", b.reshape(4, 8, S5_P, S5_GC), _EYE8).reshape(LANES, S5_N)


def _uncompact_b(m):
    return jnp.einsum("kcakp->akpc", m.reshape(8, S5_GC, 4, 8, S5_P)).reshape(S5_G, S5_P, S5_GC)


_HEAD_MEAN = np.kron(np.eye(CONV_W // CONV_HD, dtype=np.float32), np.full((CONV_HD, CONV_HD), 1.0 / CONV_HD, np.float32))


def _head_mean(v, m):
    hi = v.astype(BF16)
    lo = (v - hi.astype(F32)).astype(BF16)
    return _dot(hi, m) + _dot(lo, m)


def _head_norm(zc, m):
    d = zc - _head_mean(zc, m)
    rstd = lax.rsqrt(_head_mean(d * d, m) + EPS)
    return d * rstd, rstd


def _taps_by_phase(first):
    groups = {}
    for k in range(CONV_K):
        m, s = divmod(first + k, 8)
        groups.setdefault(s, []).append((m, k))
    return groups


def _causal_taps(buf, w_ref, r0, first, flip):
    acc = None
    for s, taps in sorted(_taps_by_phase(first).items()):
        rows = CONV_SB + (8 if s else 0)
        y = None
        for m, k in taps:
            kk = CONV_K - 1 - k if flip else k
            term = w_ref[kk:kk + 1, :] * buf[pl.ds(r0 + 8 * m, rows), :]
            y = term if y is None else y + term
        y = y[s:s + CONV_SB, :]
        acc = y if acc is None else acc + y
    return acc


def _conv_fwd(u, o_s5, wdw, bdw, lng, lnb, mavg, comm=()):
    L = u.shape[0]
    T = min(TS, L)
    n = L // T
    first_tap = HALO - (CONV_K - 1)

    def body(v1_ref, v2_ref, s5_ref, w_ref, b_ref, g_ref, be_ref, m_ref, zc_ref, o_ref, zbuf):
        i = pl.program_id(0)

        @pl.when(i == 0)
        def _():
            zbuf[0:HALO, :] = jnp.zeros((HALO, CONV_W), F32)

        zbuf[HALO:HALO + T, :] = v1_ref[...] * _sigmoid(v2_ref[...])
        for r0 in range(0, T, CONV_SB):
            zc_ref[r0:r0 + CONV_SB, :] = b_ref[...] + _causal_taps(zbuf, w_ref, r0, first_tap, False)
        zbuf[0:HALO, :] = zbuf[T:T + HALO, :]
        zn, _ = _head_norm(zc_ref[...], m_ref[...])
        zz = zn * g_ref[...] + be_ref[...]
        o_ref[:, 0:S5_W] = s5_ref[...]
        o_ref[:, S5_W:S5_W + CONV_W] = (zz * _sigmoid(zz)).astype(BF16)

    const = lambda shape: pl.BlockSpec(shape, lambda i: (0, 0))
    vec = const((1, CONV_W))
    return _pallas(body, [u, u, o_s5, wdw, bdw, lng, lnb, mavg], name="conv_fwd", grid=(n,),
                   in_specs=[pl.BlockSpec((T, CONV_W), lambda i: (i, 1)), pl.BlockSpec((T, CONV_W), lambda i: (i, 2)),
                             pl.BlockSpec((T, S5_W), lambda i: (i, 0)), const((CONV_K, CONV_W)), vec, vec, vec,
                             const((CONV_W, CONV_W))],
                   out_specs=[pl.BlockSpec((T, CONV_W), lambda i: (i, 0)),
                              pl.BlockSpec((T, S5_W + CONV_W), lambda i: (i, 0))],
                   out_shape=[jax.ShapeDtypeStruct((L, CONV_W), F32), jax.ShapeDtypeStruct((L, S5_W + CONV_W), BF16)],
                   scratch_shapes=[pltpu.VMEM((T + HALO, CONV_W), F32)], comm=comm)


def _conv_bwd(dm, zc, u, du_s5, wdw, lng, lnb, mavg, comm=()):
    L = u.shape[0]
    T = min(TS, L)
    n = L // T
    hb = T // HALO
    first_tap = HALO - (CONV_K - 1)

    def body(dm_ref, zc_ref, v1_ref, v2_ref, p1_ref, p2_ref, s5_ref, w_ref, g_ref, be_ref, m_ref,
             du_ref, dw_ref, db_ref, dg_ref, dbe_ref, zbuf, dzbuf, head):
        i = pl.program_id(0)
        first = i == 0

        @pl.when(first)
        def _():
            head[...] = jnp.zeros_like(head)
            dw_ref[...] = jnp.zeros_like(dw_ref)

        zn, rstd = _head_norm(zc_ref[...], m_ref[...])
        zz = zn * g_ref[...] + be_ref[...]
        sg = _sigmoid(zz)
        dzz = dm_ref[...].astype(F32) * sg * (1.0 + zz * (1.0 - sg))
        _accumulate(dbe_ref, first, _colsum(dzz))
        _accumulate(dg_ref, first, _colsum(dzz * zn))
        dzn = dzz * g_ref[...]
        dzc = rstd * (dzn - _head_mean(dzn, m_ref[...]) - zn * _head_mean(dzn * zn, m_ref[...]))
        _accumulate(db_ref, first, _colsum(dzc))

        dzbuf[0:T, :] = dzc
        dzbuf[T:T + HALO, :] = head[...]
        head[...] = dzbuf[0:HALO, :]
        keep = (i < n - 1).astype(F32)
        zbuf[0:HALO, :] = p1_ref[...] * _sigmoid(p2_ref[...]) * keep
        zbuf[HALO:HALO + T, :] = v1_ref[...] * _sigmoid(v2_ref[...])
        du_ref[:, 0:S5_W] = s5_ref[...]

        for r0 in range(0, T, CONV_SB):
            rows = slice(r0, r0 + CONV_SB)
            dzc_b = dzbuf[rows, :]
            for s, taps in sorted(_taps_by_phase(first_tap).items()):
                pad = ([jnp.zeros((s, CONV_W), F32)] if s else []) + [dzc_b] + ([jnp.zeros((8 - s, CONV_W), F32)] if s else [])
                shifted = jnp.concatenate(pad, axis=0) if s else dzc_b
                n_rows = shifted.shape[0]
                for m, k in taps:
                    prod = shifted * zbuf[pl.ds(r0 + 8 * m, n_rows), :]
                    dw_ref[8 * k:8 * k + 8, :] += jnp.sum(prod.reshape(n_rows // 8, 8, CONV_W), axis=0)
            dz = _causal_taps(dzbuf, w_ref, r0, 0, True)
            v1 = v1_ref[rows, :]
            sg2 = _sigmoid(v2_ref[rows, :])
            du_ref[rows, S5_W:S5_W + CONV_W] = (dz * sg2).astype(BF16)
            du_ref[rows, S5_W + CONV_W:S5_W + 2 * CONV_W] = (dz * v1 * sg2 * (1.0 - sg2)).astype(BF16)

    rev = lambda c: (lambda i: (n - 1 - i, c))
    prev = lambda c: (lambda i: (jnp.maximum((n - 1 - i) * hb - 1, 0), c))
    const = lambda shape: pl.BlockSpec(shape, lambda i: (0, 0))
    vec = const((1, CONV_W))
    vshape = jax.ShapeDtypeStruct((1, CONV_W), F32)
    return _pallas(
        body, [dm, zc, u, u, u, u, du_s5, wdw, lng, lnb, mavg], name="conv_bwd", grid=(n,),
        in_specs=[pl.BlockSpec((T, CONV_W), rev(1)), pl.BlockSpec((T, CONV_W), rev(0)),
                  pl.BlockSpec((T, CONV_W), rev(1)), pl.BlockSpec((T, CONV_W), rev(2)),
                  pl.BlockSpec((HALO, CONV_W), prev(1)), pl.BlockSpec((HALO, CONV_W), prev(2)),
                  pl.BlockSpec((T, S5_W), rev(0)), const((CONV_K, CONV_W)), vec, vec, const((CONV_W, CONV_W))],
        out_specs=[pl.BlockSpec((T, S5_W + 2 * CONV_W), rev(0)), const((8 * HALO, CONV_W)), vec, vec, vec],
        out_shape=[jax.ShapeDtypeStruct((L, S5_W + 2 * CONV_W), BF16), jax.ShapeDtypeStruct((8 * HALO, CONV_W), F32),
                   vshape, vshape, vshape],
        scratch_shapes=[pltpu.VMEM((T + HALO, CONV_W), F32), pltpu.VMEM((T + HALO, CONV_W), F32),
                        pltpu.VMEM((HALO, CONV_W), F32)], comm=comm)


def _gather_all(v):
    rows, cols = v.shape

    def body(x_ref, out_ref, send_sems, recv_sems, local_sem):
        x, y, c, chips = _position()
        me, sibling = (x, y, c), (x, y, 1 - c)

        def block(px, py, pc):
            return out_ref.at[pl.ds((4 * px + 2 * py + pc) * rows, rows), :]

        def copy(k, blk, to, src=None):
            return pltpu.make_async_remote_copy(
                src_ref=block(*blk) if src is None else src, dst_ref=block(*blk), send_sem=send_sems.at[k],
                recv_sem=recv_sems.at[k], device_id=to, device_id_type=MESH)

        mine = pltpu.make_async_copy(x_ref, block(*me), local_sem)
        mine.start()
        first = [copy(0, me, sibling, src=x_ref)]
        first += [copy(1 + j, me, (*chip, c), src=x_ref) for j, chip in enumerate(chips)]
        for cp in first:
            cp.start()
        passed = [copy(4 + j, (*chip, c), sibling) for j, chip in enumerate(chips)]
        for j, chip in enumerate(chips):
            copy(1 + j, (*chip, c), me).wait_recv()
            passed[j].start()
        copy(0, sibling, me).wait_recv()
        for j, chip in enumerate(chips):
            copy(4 + j, (*chip, 1 - c), me).wait_recv()
        for cp in first + passed:
            cp.wait_send()
        mine.wait()

    return pl.pallas_call(
        body, name="gather_small",
        in_specs=[pl.BlockSpec(memory_space=pltpu.VMEM)], out_specs=pl.BlockSpec(memory_space=pltpu.VMEM),
        out_shape=jax.ShapeDtypeStruct((N_DEV * rows, cols), v.dtype),
        scratch_shapes=[pltpu.SemaphoreType.DMA((7,)), pltpu.SemaphoreType.DMA((7,)), pltpu.SemaphoreType.DMA],
        compiler_params=pltpu.CompilerParams(vmem_limit_bytes=VMEM_LIMIT))(v)


def _adamw(w, g, m, v):
    m = ADAM_B1 * m + (1.0 - ADAM_B1) * g
    v = ADAM_B2 * v + (1.0 - ADAM_B2) * jnp.square(g)
    m_hat = m / (1.0 - ADAM_B1 ** ADAM_STEP)
    v_hat = v / (1.0 - ADAM_B2 ** ADAM_STEP)
    return -ADAM_LR * (m_hat / (jnp.sqrt(v_hat) + ADAM_EPS) + ADAM_WD * w), m, v


def _sum_slots(recv, name, comm=()):
    _, rows, cols = recv.shape
    tr = _row_tile(rows, cols)

    def body(r_ref, o_ref):
        acc = r_ref[0].astype(F32)
        for s in range(1, N_CHIPS):
            acc = acc + r_ref[s].astype(F32)
        o_ref[...] = acc

    return _pallas(body, [recv], name=name, grid=(rows // tr,),
                   in_specs=[pl.BlockSpec((N_CHIPS, tr, cols), lambda i: (0, i, 0))],
                   out_specs=[pl.BlockSpec((tr, cols), lambda i: (i, 0))],
                   out_shape=[jax.ShapeDtypeStruct((rows, cols), F32)], comm=comm)[0]


def _add_halves(mine, theirs, name):
    slots, rows, cols = mine.shape
    tr = _row_tile(rows, cols * slots)

    def body(a_ref, b_ref, o_ref):
        o_ref[...] = (a_ref[...].astype(F32) + b_ref[...].astype(F32)).astype(o_ref.dtype)

    spec = pl.BlockSpec((slots, tr, cols), lambda i: (0, i, 0))
    return _pallas(body, [mine, theirs], name=name, grid=(rows // tr,), in_specs=[spec, spec], out_specs=[spec],
                   out_shape=[jax.ShapeDtypeStruct(mine.shape, mine.dtype)])[0]


def _adamw_sharded(w, parts, m, v, name, comm=()):
    rows, cols = w.shape
    tr = _row_tile(rows, cols)
    n = len(parts)

    def body(w_ref, *refs):
        p_refs, (m_ref, v_ref, g_ref, d_ref, nm_ref, nv_ref) = refs[:n], refs[n:]
        g = p_refs[0][...]
        for p_ref in p_refs[1:]:
            g = g + p_ref[...]
        g_ref[...] = g
        d_ref[...], nm_ref[...], nv_ref[...] = _adamw(w_ref[...], g, m_ref[...], v_ref[...])

    spec = pl.BlockSpec((tr, cols), lambda i: (i, 0))
    shape = jax.ShapeDtypeStruct((rows, cols), F32)
    return _pallas(body, [w, *parts, m, v], name=name, grid=(rows // tr,), in_specs=[spec] * (n + 3),
                   out_specs=[spec] * 4, out_shape=[shape] * 4, comm=comm)


def _adamw_small(w, gathered, m, v):
    rows, cols = w.shape

    def body(w_ref, a_ref, m_ref, v_ref, g_ref, d_ref, nm_ref, nv_ref):
        g = a_ref[0:rows, :]
        for dev in range(1, N_DEV):
            g = g + a_ref[dev * rows:(dev + 1) * rows, :]
        g_ref[...] = g
        d_ref[...], nm_ref[...], nv_ref[...] = _adamw(w_ref[...], g, m_ref[...], v_ref[...])

    shape = jax.ShapeDtypeStruct((rows, cols), F32)
    return pl.pallas_call(
        body, name="adamw_small", out_shape=[shape] * 4,
        compiler_params=pltpu.CompilerParams(vmem_limit_bytes=VMEM_LIMIT))(w, gathered, m, v)


PACK_TILE = 8 * LANES


def _pack_small(vals, last_row):
    rows = []
    for name in SMALL:
        flat = vals[name].reshape(-1).astype(F32)
        rows.append(jnp.pad(flat, (0, -flat.size % PACK_TILE)).reshape(-1, LANES))
    rows.append(jnp.pad(last_row, ((0, 7), (0, 0))))
    return jnp.concatenate(rows, axis=0)


def _unpack_small(packed, like):
    out, r = {}, 0
    for name in SMALL:
        size = like[name].size
        out[name] = packed[r:r + -(-size // LANES)].reshape(-1)[:size].reshape(like[name].shape)
        r += 8 * -(-size // PACK_TILE)
    return out, packed[r, 0]


def _shard2d(name, v):
    v = v.reshape(v.shape[-2:])
    return v.T if name in FFN_T else v


def _unshard(name, v, shape):
    return (v.T if name in FFN_T else v).reshape(shape)


def _train_step(x3, tgt3, wts, ms, vs):
    x, tgt = x3[0], tgt3[0]
    L, D = x.shape
    row = lambda v: v.reshape(1, -1)
    shards = {k: _shard2d(k, wts[k]) for k in SHARDED}
    sends = {k: shards[k] if k == "conv_w_dw" else _cast_bf16(shards[k], "cast_" + k) for k in SHARDED}
    gat = {k: _Gather(sends[k]) for k in SHARDED}
    w = lambda k: gat[k].result[0]

    s = {k: wts[k] for k in SMALL}
    lr, li = s["s5_lam_re"].reshape(1, S5_N), s["s5_lam_im"].reshape(1, S5_N)
    ldt = jnp.repeat(s["s5_log_dt"].reshape(S5_G), S5_P).reshape(1, S5_N)
    brc, bic = _compact_b(s["s5_b_re"].reshape(S5_G, S5_P, S5_GC)), _compact_b(s["s5_b_im"].reshape(S5_G, S5_P, S5_GC))
    crc = _compact_b(s["s5_c_re"].reshape(S5_G, S5_GC, S5_P).transpose(0, 2, 1)).T
    cic = _compact_b(s["s5_c_im"].reshape(S5_G, S5_GC, S5_P).transpose(0, 2, 1)).T
    d_skip, b_glu = row(s["s5_d"]), row(s["s5_b_glu"])
    b_dw, ln_g, ln_b = row(s["conv_b_dw"]), row(s["conv_ln_g"]), row(s["conv_ln_b"])
    g1, gm, g2, gf = row(s["ffn1_norm"]), row(s["mix_norm"]), row(s["ffn2_norm"]), row(s["final_norm"])
    mavg = jnp.asarray(_HEAD_MEAN, dtype=BF16)

    h1 = _rms_fwd(x, g1, "rms1", comm=[gat["ffn1_w_gate"], gat["ffn1_w_up"]])
    a1, b1, act1 = _ffn_up(h1, w("ffn1_w_gate"), w("ffn1_w_up"), "ffn1_up", comm=[gat["ffn1_w_down"]])
    x1, h2 = _ffn_down(act1, w("ffn1_w_down"), x, gm, "ffn1_down",
                       comm=[gat["w_in"], gat["s5_w_glu"], gat["conv_w_dw"], gat["w_out"]])
    u = _mm_grouped(h2, w("w_in"), "in_proj")
    are, aim, bre, bim, cre, cim = _s5_params_fwd(lr, li, ldt, brc, bic, crc, cic)
    are_t, aim_t = are.reshape(S5_TILES, LANES), aim.reshape(S5_TILES, LANES)
    w_glu = w("s5_w_glu").reshape(S5_W, S5_W)
    *states, y_pre, o_s5 = _s5_fwd(u, are_t, aim_t, bre, bim, cre, cim, d_skip, w_glu, b_glu,
                                   comm=[gat["ffn2_w_gate"], gat["ffn2_w_up"]])
    w_dw = w("conv_w_dw").transpose(1, 0, 2).reshape(CONV_K, CONV_W)
    zc, cat = _conv_fwd(u, o_s5, w_dw, b_dw, ln_g, ln_b, mavg, comm=[gat["ffn2_w_down"]])
    w_out = w("w_out").reshape(-1, D)
    x2, h3 = _mix_out(cat, w_out, x1, g2, "mix_out")
    a2, b2, act2 = _ffn_up(h3, w("ffn2_w_gate"), w("ffn2_w_up"), "ffn2_up")
    dx3, dx3b, loss_part, d_gf = _ffn_down_loss(act2, w("ffn2_w_down"), x2, gf, tgt, "ffn2_down_loss")

    gs, sc, waiting = {"final_norm": d_gf}, {}, []

    def grad(key, g):
        if g.shape[1] % (2 * BF16_ROWS) == 0 and g.dtype == BF16:
            waiting.append(_SwapHalf(g, key))
        else:
            sc[key] = _Scatter(g)
            waiting.append(sc[key])

    def carry(call, *args, **kw):
        ops = list(waiting)
        waiting.clear()
        res = call(*args, comm=ops, **kw)
        for op in ops:
            if isinstance(op, _SwapHalf):
                sc[op.key] = _Scatter(_add_halves(*op.result, "add_" + op.key))
                waiting.append(sc[op.key])
        return res

    da2, db2 = _ffn_bwd_act(dx3b, w("ffn2_w_down"), a2, b2, "ffn2_bwd_act")
    grad("ffn2_w_down", _mm_tn(act2, dx3b[None], 0.5, "ffn2_dwd", N_CHIPS))
    grad("ffn2_w_gate", carry(_mm_tn, da2, h3[None], 1.0, "ffn2_dwg", N_CHIPS))
    grad("ffn2_w_up", carry(_mm_tn, db2, h3[None], 1.0, "ffn2_dwu", N_CHIPS))
    dx2, dx2b, gs["ffn2_norm"] = carry(_mm_rmsbwd, [da2, db2], [w("ffn2_w_gate"), w("ffn2_w_up")], False, False, x2, g2,
                                       dx3, "ffn2_bwd_dx")

    dm = _mm_nt(dx2b, w_out, "mix_bwd")
    grad("w_out", _mm_tn(cat[None], dx2b[None], 1.0, "dwout", 1).reshape(N_CHIPS, -1, D))
    (du_s5, d_wglu, gs["s5_b_glu"], gs["s5_d"], d_crc, d_cic, d_bre, d_bim, d_are, d_aim) = carry(
        _s5_bwd, dm, y_pre, u, states, are_t, aim_t, bre, bim, cre, cim, d_skip, w_glu, b_glu)
    grad("s5_w_glu", d_wglu.astype(BF16).reshape(N_CHIPS, -1, S5_W))
    g_lr, g_li, g_ldt, g_brc, g_bic = _s5_params_bwd(lr, li, ldt, brc, bic, d_are.reshape(1, S5_N),
                                                     d_aim.reshape(1, S5_N), d_bre, d_bim)
    gs["s5_lam_re"], gs["s5_lam_im"] = g_lr, g_li
    gs["s5_log_dt"] = jnp.sum(g_ldt.reshape(S5_G, S5_P), axis=1)
    gs["s5_b_re"], gs["s5_b_im"] = _uncompact_b(g_brc), _uncompact_b(g_bic)
    gs["s5_c_re"] = _uncompact_b(d_crc.T).transpose(0, 2, 1)
    gs["s5_c_im"] = _uncompact_b(d_cic.T).transpose(0, 2, 1)
    du, d_wdw, gs["conv_b_dw"], gs["conv_ln_g"], gs["conv_ln_b"] = carry(_conv_bwd, dm, zc, u, du_s5, w_dw, ln_g, ln_b, mavg)
    d_wdw = jnp.sum(d_wdw.reshape(HALO, 8, CONV_W), axis=1)[:CONV_K]
    grad("conv_w_dw", d_wdw.reshape(CONV_K, N_CHIPS, -1).transpose(1, 0, 2))
    grad("w_in", carry(_mm_tn, h2[None], du, 1.0, "dwin", N_CHIPS, b_cols=True))
    dx1, dx1b, gs["mix_norm"] = carry(_mm_rmsbwd, [du], [w("w_in")], True, True, x1, gm, dx2, "in_proj_bwd")

    da1, db1 = carry(_ffn_bwd_act, dx1b, w("ffn1_w_down"), a1, b1, "ffn1_bwd_act")
    grad("ffn1_w_down", _mm_tn(act1, dx1b[None], 0.5, "ffn1_dwd", N_CHIPS))
    grad("ffn1_w_gate", carry(_mm_tn, da1, h1[None], 1.0, "ffn1_dwg", N_CHIPS))
    grad("ffn1_w_up", carry(_mm_tn, db1, h1[None], 1.0, "ffn1_dwu", N_CHIPS))
    grad_x, _, gs["ffn1_norm"] = carry(_mm_rmsbwd, [da1, db1], [w("ffn1_w_gate"), w("ffn1_w_up")], False, False, x, g1,
                                       dx1, "ffn1_bwd_dx")

    out = {}
    gsmall = {k: gs[k].reshape(wts[k].shape) for k in SMALL}
    zero_row = jnp.zeros((1, LANES), F32)
    g_all = _gather_all(_pack_small(gsmall, loss_part))
    res = _adamw_small(_pack_small(s, zero_row), g_all, _pack_small({k: ms[k] for k in SMALL}, zero_row),
                       _pack_small({k: vs[k] for k in SMALL}, zero_row))
    unpacked = [_unpack_small(r, s) for r in res]
    loss = unpacked[0][1]
    for k in SMALL:
        out[k] = [u_[0][k] for u_ in unpacked]

    order = ("ffn2_w_down", "ffn2_w_gate", "ffn2_w_up", "w_out", "s5_w_glu", "conv_w_dw", "w_in", "ffn1_w_down",
             "ffn1_w_gate", "ffn1_w_up")
    back = {}
    for k in order:
        part = carry(_sum_slots, sc[k].result[0], "sum_" + k)
        back[k] = _SwapBack(part) if part.shape != shards[k].shape else _Swap(part)
        waiting.append(back[k])
    for k in order:
        parts = [back[k].result[0]] if isinstance(back[k], _SwapBack) else [back[k].ins[0], back[k].result[0]]
        res = carry(_adamw_sharded, shards[k], parts, _shard2d(k, ms[k]), _shard2d(k, vs[k]), "adamw_" + k)
        out[k] = [_unshard(k, r, wts[k].shape) for r in res]
    return loss, grad_x[None], out


def kernel(x, ffn1_norm, ffn1_w_gate, ffn1_w_up, ffn1_w_down, mix_norm, w_in, s5_lam_re, s5_lam_im, s5_log_dt, s5_b_re, s5_b_im, s5_c_re, s5_c_im, s5_d, s5_w_glu, s5_b_glu, conv_w_dw, conv_b_dw, conv_ln_g, conv_ln_b, w_out, ffn2_norm, ffn2_w_gate, ffn2_w_up, ffn2_w_down, final_norm, loss_target, m_ffn1_norm, m_ffn1_w_gate, m_ffn1_w_up, m_ffn1_w_down, m_mix_norm, m_w_in, m_s5_lam_re, m_s5_lam_im, m_s5_log_dt, m_s5_b_re, m_s5_b_im, m_s5_c_re, m_s5_c_im, m_s5_d, m_s5_w_glu, m_s5_b_glu, m_conv_w_dw, m_conv_b_dw, m_conv_ln_g, m_conv_ln_b, m_w_out, m_ffn2_norm, m_ffn2_w_gate, m_ffn2_w_up, m_ffn2_w_down, m_final_norm, v_ffn1_norm, v_ffn1_w_gate, v_ffn1_w_up, v_ffn1_w_down, v_mix_norm, v_w_in, v_s5_lam_re, v_s5_lam_im, v_s5_log_dt, v_s5_b_re, v_s5_b_im, v_s5_c_re, v_s5_c_im, v_s5_d, v_s5_w_glu, v_s5_b_glu, v_conv_w_dw, v_conv_b_dw, v_conv_ln_g, v_conv_ln_b, v_w_out, v_ffn2_norm, v_ffn2_w_gate, v_ffn2_w_up, v_ffn2_w_down, v_final_norm):
    given = dict(locals())
    wts = {k: given[k] for k in WEIGHTS}
    ms = {k: given["m_" + k] for k in WEIGHTS}
    vs = {k: given["v_" + k] for k in WEIGHTS}
    loss, grad_x, out = _train_step(x, loss_target, wts, ms, vs)
    return (loss, grad_x, *[out[k][0] for k in WEIGHTS], *[out[k][1] for k in WEIGHTS],
            *[out[k][2] for k in WEIGHTS], *[out[k][3] for k in WEIGHTS])
```

```python
import functools

import jax
import jax.numpy as jnp
import numpy as np
from jax import lax
from jax.experimental import pallas as pl
from jax.experimental.pallas import tpu as pltpu

F32, BF16 = jnp.float32, jnp.bfloat16
MESH = pl.DeviceIdType.MESH

EPS = 1e-6
ADAM_LR, ADAM_B1, ADAM_B2, ADAM_EPS, ADAM_WD, ADAM_STEP = 0.001, 0.9, 0.999, 1e-08, 0.01, 10

N_CHIPS = 4
N_DEV = 8
LANES = 128
BF16_ROWS = 16
S5_W, S5_G, S5_GC, S5_P = 512, 32, 16, 64
S5_N = S5_G * S5_P
S5_TILES = S5_N // LANES
S5_HALF = 8
CONV_W, CONV_K, CONV_HD = 512, 31, 64
HALO = 32
CONV_SB = 32
TM = 512
TMS = 1024
TK = 2048
TS = 256
VMEM_LIMIT = 48 << 20
GELU_C0, GELU_C1 = 0.7978845608028654, 0.044715

FFN_T = ("ffn1_w_gate", "ffn1_w_up", "ffn2_w_gate", "ffn2_w_up")
SHARDED = ("ffn1_w_gate", "ffn1_w_up", "ffn1_w_down", "w_in", "s5_w_glu", "conv_w_dw", "w_out",
           "ffn2_w_gate", "ffn2_w_up", "ffn2_w_down")
SMALL = ("ffn1_norm", "mix_norm", "s5_lam_re", "s5_lam_im", "s5_log_dt", "s5_b_re", "s5_b_im", "s5_c_re",
         "s5_c_im", "s5_d", "s5_b_glu", "conv_b_dw", "conv_ln_g", "conv_ln_b", "ffn2_norm", "final_norm")
WEIGHTS = ("ffn1_norm", "ffn1_w_gate", "ffn1_w_up", "ffn1_w_down", "mix_norm", "w_in", "s5_lam_re", "s5_lam_im",
           "s5_log_dt", "s5_b_re", "s5_b_im", "s5_c_re", "s5_c_im", "s5_d", "s5_w_glu", "s5_b_glu", "conv_w_dw",
           "conv_b_dw", "conv_ln_g", "conv_ln_b", "w_out", "ffn2_norm", "ffn2_w_gate", "ffn2_w_up", "ffn2_w_down",
           "final_norm")


def _dot(a, b):
    return jnp.dot(a, b, preferred_element_type=F32)


def _dot_nt(a, b):
    return lax.dot_general(a, b, (((1,), (1,)), ((), ())), preferred_element_type=F32)


def _dot_tn(a, b):
    return lax.dot_general(a, b, (((0,), (0,)), ((), ())), preferred_element_type=F32)


def _colsum(v):
    return jnp.sum(v, axis=0, keepdims=True)


def _sigmoid(v):
    return 1.0 / (1.0 + jnp.exp(-v))


def _accumulate(ref, first, value):
    @pl.when(first)
    def _():
        ref[...] = value

    @pl.when(jnp.logical_not(first))
    def _():
        ref[...] += value


def _position():
    x, y, c = lax.axis_index("x"), lax.axis_index("y"), lax.axis_index("c")
    return x, y, c, [(1 - x, y), (x, 1 - y), (1 - x, 1 - y)]


def _remote(src, dst, sems, send, recv, device):
    return pltpu.make_async_remote_copy(src_ref=src, dst_ref=dst, send_sem=sems.at[send], recv_sem=sems.at[recv],
                                        device_id=device, device_id_type=MESH)


class _Gather:
    def __init__(self, shard):
        self.ins = [shard]
        self.outs = [jax.ShapeDtypeStruct((N_CHIPS,) + shard.shape, shard.dtype)]
        self.rows = shard.shape[0]
        self.halve = shard.dtype == BF16 and self.rows % (2 * BF16_ROWS) == 0
        self.n_sem = 13 if self.halve else 7
        self.result = None

    def _copies(self, ins, outs, sems, s0, pos):
        x, y, c, chips = pos
        src, dst = ins[0], outs[0]
        me = 2 * x + y
        if self.halve:
            hr = self.rows // 2
            mine, theirs = pl.ds(c * hr, hr), pl.ds((1 - c) * hr, hr)
            part = lambda slot, rows: dst.at[slot, rows]
            my_src = src.at[mine]
        else:
            mine = theirs = None
            part = lambda slot, rows: dst.at[slot]
            my_src = src
        slot = lambda j: 2 * chips[j][0] + chips[j][1]
        local = lambda: pltpu.make_async_copy(src, dst.at[me], sems.at[s0])
        send = lambda j: _remote(my_src, part(me, mine), sems, s0 + 1 + j, s0 + 4 + j, (*chips[j], c))
        land = lambda j: _remote(my_src, part(slot(j), mine), sems, s0 + 1 + j, s0 + 4 + j, (*chips[j], c))
        fwd = lambda j: _remote(part(slot(j), mine), part(slot(j), mine), sems, s0 + 7 + j, s0 + 10 + j, (x, y, 1 - c))
        got = lambda j: _remote(part(slot(j), theirs), part(slot(j), theirs), sems, s0 + 7 + j, s0 + 10 + j,
                                (x, y, 1 - c))
        return local, send, land, fwd, got

    def start(self, ins, outs, sems, s0, pos):
        local, send, _, _, _ = self._copies(ins, outs, sems, s0, pos)
        local().start()
        for j in range(N_CHIPS - 1):
            send(j).start()

    def finish(self, ins, outs, sems, s0, pos):
        local, send, land, fwd, got = self._copies(ins, outs, sems, s0, pos)
        others = range(N_CHIPS - 1)
        for j in others:
            land(j).wait_recv()
            if self.halve:
                fwd(j).start()
        for j in others:
            if self.halve:
                got(j).wait_recv()
        for j in others:
            send(j).wait_send()
            if self.halve:
                fwd(j).wait_send()
        local().wait()


class _Scatter:
    def __init__(self, grad):
        self.ins = [grad]
        self.outs = [jax.ShapeDtypeStruct(grad.shape, grad.dtype)]
        self.n_sem = 7
        self.result = None

    def _copies(self, ins, outs, sems, s0, pos):
        x, y, c, chips = pos
        src, dst = ins[0], outs[0]
        me = 2 * x + y
        slot = lambda j: 2 * chips[j][0] + chips[j][1]
        local = lambda: pltpu.make_async_copy(src.at[me], dst.at[me], sems.at[s0])
        send = lambda j: _remote(src.at[slot(j)], dst.at[me], sems, s0 + 1 + j, s0 + 4 + j, (*chips[j], c))
        land = lambda j: _remote(src.at[me], dst.at[slot(j)], sems, s0 + 1 + j, s0 + 4 + j, (*chips[j], c))
        return local, send, land

    def start(self, ins, outs, sems, s0, pos):
        local, send, _ = self._copies(ins, outs, sems, s0, pos)
        local().start()
        for j in range(N_CHIPS - 1):
            send(j).start()

    def finish(self, ins, outs, sems, s0, pos):
        local, send, land = self._copies(ins, outs, sems, s0, pos)
        for j in range(N_CHIPS - 1):
            land(j).wait_recv()
        for j in range(N_CHIPS - 1):
            send(j).wait_send()
        local().wait()


class _Swap:
    def __init__(self, part):
        self.ins = [part]
        self.outs = [jax.ShapeDtypeStruct(part.shape, part.dtype)]
        self.n_sem = 2
        self.result = None

    def _copy(self, ins, outs, sems, s0, pos):
        x, y, c, _ = pos
        return _remote(ins[0], outs[0], sems, s0, s0 + 1, (x, y, 1 - c))

    def start(self, ins, outs, sems, s0, pos):
        self._copy(ins, outs, sems, s0, pos).start()

    def finish(self, ins, outs, sems, s0, pos):
        self._copy(ins, outs, sems, s0, pos).wait()


class _SwapHalf:
    def __init__(self, grad, key):
        slots, rows, cols = grad.shape
        half = jax.ShapeDtypeStruct((slots, rows // 2, cols), grad.dtype)
        self.ins, self.outs, self.key = [grad], [half, half], key
        self.hr = rows // 2
        self.n_sem = 3
        self.result = None

    def _copies(self, ins, outs, sems, s0, pos):
        x, y, c, _ = pos
        mine, theirs = pl.ds(c * self.hr, self.hr), pl.ds((1 - c) * self.hr, self.hr)
        local = pltpu.make_async_copy(ins[0].at[:, mine], outs[0], sems.at[s0])
        remote = _remote(ins[0].at[:, theirs], outs[1], sems, s0 + 1, s0 + 2, (x, y, 1 - c))
        return local, remote

    def start(self, ins, outs, sems, s0, pos):
        for cp in self._copies(ins, outs, sems, s0, pos):
            cp.start()

    def finish(self, ins, outs, sems, s0, pos):
        for cp in self._copies(ins, outs, sems, s0, pos):
            cp.wait()


class _SwapBack:
    def __init__(self, part):
        hr, cols = part.shape
        self.ins, self.outs = [part], [jax.ShapeDtypeStruct((2 * hr, cols), part.dtype)]
        self.hr = hr
        self.n_sem = 3
        self.result = None

    def _copies(self, ins, outs, sems, s0, pos):
        x, y, c, _ = pos
        mine, theirs = pl.ds(c * self.hr, self.hr), pl.ds((1 - c) * self.hr, self.hr)
        local = lambda: pltpu.make_async_copy(ins[0], outs[0].at[mine], sems.at[s0])
        send = lambda: _remote(ins[0], outs[0].at[mine], sems, s0 + 1, s0 + 2, (x, y, 1 - c))
        land = lambda: _remote(ins[0], outs[0].at[theirs], sems, s0 + 1, s0 + 2, (x, y, 1 - c))
        return local, send, land

    def start(self, ins, outs, sems, s0, pos):
        local, send, _ = self._copies(ins, outs, sems, s0, pos)
        local().start()
        send().start()

    def finish(self, ins, outs, sems, s0, pos):
        local, send, land = self._copies(ins, outs, sems, s0, pos)
        land().wait_recv()
        send().wait_send()
        local().wait()


def _pallas(body, args, *, name, grid, in_specs, out_specs, out_shape, scratch_shapes=(), comm=()):
    comm = list(comm)
    n_in, n_out, n_scr = len(in_specs), len(out_specs), len(scratch_shapes)
    c_in = [a for op in comm for a in op.ins]
    c_out = [s for op in comm for s in op.outs]
    n_sem = sum(op.n_sem for op in comm)

    def full(*refs):
        o0 = n_in + len(c_in)
        s0 = o0 + n_out + len(c_out)
        ins, cin = refs[:n_in], refs[n_in:o0]
        outs, cout = refs[o0:o0 + n_out], refs[o0 + n_out:s0]
        scratch = refs[s0:s0 + n_scr]
        if comm:
            sems = refs[s0 + n_scr]
            ids = [pl.program_id(d) for d in range(len(grid))]
            first = functools.reduce(jnp.logical_and, [i == 0 for i in ids])
            last = functools.reduce(jnp.logical_and, [i == g - 1 for i, g in zip(ids, grid)])
            pos = _position()

            def each(step):
                ci = co = cs = 0
                for op in comm:
                    getattr(op, step)(cin[ci:ci + len(op.ins)], cout[co:co + len(op.outs)], sems, cs, pos)
                    ci, co, cs = ci + len(op.ins), co + len(op.outs), cs + op.n_sem

            @pl.when(first)
            def _():
                each("start")

        body(*ins, *outs, *scratch)
        if comm:
            @pl.when(last)
            def _():
                each("finish")

    hbm = pl.BlockSpec(memory_space=pl.ANY)
    res = pl.pallas_call(
        full, name=name, grid=grid,
        in_specs=list(in_specs) + [hbm] * len(c_in), out_specs=list(out_specs) + [hbm] * len(c_out),
        out_shape=list(out_shape) + c_out,
        scratch_shapes=list(scratch_shapes) + ([pltpu.SemaphoreType.DMA((n_sem,))] if comm else []),
        compiler_params=pltpu.CompilerParams(dimension_semantics=("arbitrary",) * len(grid),
                                             vmem_limit_bytes=VMEM_LIMIT))(*args, *c_in)
    k = n_out
    for op in comm:
        op.result = list(res[k:k + len(op.outs)])
        k += len(op.outs)
    return list(res[:n_out])


def _row_tile(rows, cols, itemsize=4, budget=1 << 20):
    t = rows
    while t % (2 * BF16_ROWS) == 0 and t * cols * itemsize > budget:
        t //= 2
    return t


def _cast_bf16(w, name):
    rows, cols = w.shape
    tr = _row_tile(rows, cols)

    def body(w_ref, o_ref):
        o_ref[...] = w_ref[...].astype(BF16)

    spec = pl.BlockSpec((tr, cols), lambda i: (i, 0))
    return _pallas(body, [w], name=name, grid=(rows // tr,), in_specs=[spec], out_specs=[spec],
                   out_shape=[jax.ShapeDtypeStruct((rows, cols), BF16)])[0]


def _cast_many(ws, name, comm=()):
    n = len(ws)

    def body(*refs):
        for w_ref, o_ref in zip(refs[:n], refs[n:]):
            o_ref[...] = w_ref[...].astype(BF16)

    return _pallas(body, list(ws), name=name, grid=(1,), in_specs=[_resident(v.shape) for v in ws],
                   out_specs=[pl.BlockSpec(v.shape, lambda i: (0, 0)) for v in ws],
                   out_shape=[jax.ShapeDtypeStruct(v.shape, BF16) for v in ws], comm=comm)


def _rms_fwd(x, g, name, comm=()):
    L, D = x.shape

    def body(x_ref, g_ref, h_ref):
        xf = x_ref[...]
        r = lax.rsqrt(jnp.mean(xf * xf, axis=-1, keepdims=True) + EPS)
        h_ref[...] = (xf * r * g_ref[...]).astype(BF16)

    row = pl.BlockSpec((TMS, D), lambda i: (i, 0))
    return _pallas(body, [x, g], name=name, grid=(L // TMS,),
                   in_specs=[row, pl.BlockSpec((1, D), lambda i: (0, 0))], out_specs=[row],
                   out_shape=[jax.ShapeDtypeStruct((L, D), BF16)], comm=comm)[0]


def _resident(shape):
    return pl.BlockSpec(shape, lambda *_: (0,) * len(shape), pipeline_mode=pl.Buffered(1))


def _ffn_up(h, wg_t, wu_t, name, comm=()):
    L, D = h.shape
    G, FS, _ = wg_t.shape

    def body(h_ref, wg_ref, wu_ref, a_ref, b_ref, act_ref):
        j = pl.program_id(1)
        hv = h_ref[...]
        a = _dot_nt(hv, wg_ref[j])
        b = _dot_nt(hv, wu_ref[j])
        a_ref[...] = a.astype(BF16)
        b_ref[...] = b.astype(BF16)
        act_ref[...] = (a * _sigmoid(a) * b).astype(BF16)

    ospec = pl.BlockSpec((None, TMS, FS), lambda i, j: (j, i, 0))
    oshape = jax.ShapeDtypeStruct((G, L, FS), BF16)
    return _pallas(body, [h, wg_t, wu_t], name=name, grid=(L // TMS, G),
                   in_specs=[pl.BlockSpec((TMS, D), lambda i, j: (i, 0)), _resident((G, FS, D)), _resident((G, FS, D))],
                   out_specs=[ospec, ospec, ospec], out_shape=[oshape, oshape, oshape], comm=comm)


def _group_sum(a_ref, w_ref, groups, mm=_dot):
    acc = mm(a_ref[0], w_ref[0])
    for j in range(1, groups):
        acc = acc + mm(a_ref[j], w_ref[j])
    return acc


def _ffn_down(act, wd, x, g_next, name, comm=()):
    G, L, FS = act.shape
    D = wd.shape[2]

    def body(act_ref, wd_ref, x_ref, g_ref, xn_ref, hn_ref):
        xn = x_ref[...] + 0.5 * _group_sum(act_ref, wd_ref, G)
        xn_ref[...] = xn
        r = lax.rsqrt(jnp.mean(xn * xn, axis=-1, keepdims=True) + EPS)
        hn_ref[...] = (xn * r * g_ref[...]).astype(BF16)

    row = pl.BlockSpec((TM, D), lambda i: (i, 0))
    return _pallas(body, [act, wd, x, g_next], name=name, grid=(L // TM,),
                   in_specs=[pl.BlockSpec((G, TM, FS), lambda i: (0, i, 0)), _resident((G, FS, D)), row,
                             pl.BlockSpec((1, D), lambda i: (0, 0))],
                   out_specs=[row, row],
                   out_shape=[jax.ShapeDtypeStruct((L, D), F32), jax.ShapeDtypeStruct((L, D), BF16)], comm=comm)


def _ffn_down_loss(act, wd, x, gf, tgt, name):
    G, L, FS = act.shape
    D = wd.shape[2]

    def body(act_ref, wd_ref, x_ref, g_ref, t_ref, dx_ref, dxb_ref, loss_ref, dg_ref):
        i = pl.program_id(0)
        xn = x_ref[...] + 0.5 * _group_sum(act_ref, wd_ref, G)
        r = lax.rsqrt(jnp.mean(xn * xn, axis=-1, keepdims=True) + EPS)
        xh = xn * r
        gv = g_ref[...]
        e = xh * gv - t_ref[...]
        part = 0.5 * jnp.sum(_colsum(e * e), axis=1, keepdims=True) / D
        dy = e / D
        _accumulate(loss_ref, i == 0, jnp.broadcast_to(part, (1, LANES)))
        _accumulate(dg_ref, i == 0, _colsum(dy * xh))
        dxh = dy * gv
        dx = r * (dxh - xh * jnp.mean(dxh * xh, axis=-1, keepdims=True))
        dx_ref[...] = dx
        dxb_ref[...] = dx.astype(BF16)

    row = pl.BlockSpec((TM, D), lambda i: (i, 0))
    return _pallas(body, [act, wd, x, gf, tgt], name=name, grid=(L // TM,),
                   in_specs=[pl.BlockSpec((G, TM, FS), lambda i: (0, i, 0)), _resident((G, FS, D)), row,
                             pl.BlockSpec((1, D), lambda i: (0, 0)), row],
                   out_specs=[row, row, pl.BlockSpec((1, LANES), lambda i: (0, 0)),
                              pl.BlockSpec((1, D), lambda i: (0, 0))],
                   out_shape=[jax.ShapeDtypeStruct((L, D), F32), jax.ShapeDtypeStruct((L, D), BF16),
                              jax.ShapeDtypeStruct((1, LANES), F32), jax.ShapeDtypeStruct((1, D), F32)])


def _ffn_bwd_act(dxb, wd, a, b, name, comm=()):
    L, D = dxb.shape
    G, FS, _ = wd.shape

    def body(dx_ref, wd_ref, a_ref, b_ref, da_ref, db_ref):
        dact = 0.5 * _dot_nt(dx_ref[...], wd_ref[pl.program_id(1)])
        av = a_ref[...].astype(F32)
        bv = b_ref[...].astype(F32)
        sg = _sigmoid(av)
        da_ref[...] = (dact * bv * sg * (1.0 + av * (1.0 - sg))).astype(BF16)
        db_ref[...] = (dact * av * sg).astype(BF16)

    gspec = pl.BlockSpec((None, TMS, FS), lambda i, j: (j, i, 0))
    oshape = jax.ShapeDtypeStruct((G, L, FS), BF16)
    return _pallas(body, [dxb, wd, a, b], name=name, grid=(L // TMS, G),
                   in_specs=[pl.BlockSpec((TMS, D), lambda i, j: (i, 0)), _resident((G, FS, D)), gspec, gspec],
                   out_specs=[gspec, gspec], out_shape=[oshape, oshape], comm=comm)


def _mm_grouped(a, w, name):
    L, K = a.shape
    G, _, N = w.shape

    def body(a_ref, w_ref, o_ref):
        o_ref[...] = _dot(a_ref[...], w_ref[pl.program_id(1)])

    return _pallas(body, [a, w], name=name, grid=(L // TMS, G),
                   in_specs=[pl.BlockSpec((TMS, K), lambda i, g: (i, 0)), _resident((G, K, N))],
                   out_specs=[pl.BlockSpec((TMS, N), lambda i, g: (i, g))],
                   out_shape=[jax.ShapeDtypeStruct((L, G * N), F32)])[0]


def _mm_nt(a, w, name):
    L, K = a.shape
    N = w.shape[0]

    def body(a_ref, w_ref, o_ref):
        o_ref[...] = _dot_nt(a_ref[...], w_ref[...]).astype(BF16)

    return _pallas(body, [a, w], name=name, grid=(L // TMS,),
                   in_specs=[pl.BlockSpec((TMS, K), lambda i: (i, 0)), _resident((N, K))],
                   out_specs=[pl.BlockSpec((TMS, N), lambda i: (i, 0))],
                   out_shape=[jax.ShapeDtypeStruct((L, N), BF16)])[0]


def _mm_tn(a, b, scale, name, groups, b_cols=False, comm=()):
    L, M = a.shape[1], a.shape[2]
    N = b.shape[1] // groups if b_cols else b.shape[2]
    tk = min(L, TK)
    nk = L // tk

    def spec(v, cols):
        if cols:
            return pl.BlockSpec((tk, v.shape[1] // groups), lambda g, k: (k, g))
        if v.shape[0] > 1:
            return pl.BlockSpec((None, tk, v.shape[2]), lambda g, k: (g, k, 0))
        return pl.BlockSpec((None, tk, v.shape[2]), lambda g, k: (0, k, 0))

    def body(a_ref, b_ref, o_ref, acc):
        k = pl.program_id(1)
        p = _dot_tn(a_ref[...], b_ref[...])
        if nk == 1:
            o_ref[...] = (p * scale).astype(BF16)
        else:
            _accumulate(acc, k == 0, p)

            @pl.when(k == nk - 1)
            def _():
                o_ref[...] = (acc[...] * scale).astype(BF16)

    return _pallas(body, [a, b], name=name, grid=(groups, nk),
                   in_specs=[spec(a, False), spec(b, b_cols)],
                   out_specs=[pl.BlockSpec((None, M, N), lambda g, k: (g, 0, 0))],
                   out_shape=[jax.ShapeDtypeStruct((groups, M, N), BF16)],
                   scratch_shapes=[pltpu.VMEM((M, N), F32)], comm=comm)[0]


def _mm_rmsbwd(a_list, w_list, nt, a_cols, x_in, g, dx_out, name, comm=()):
    P = len(a_list)
    G = w_list[0].shape[0]
    L, D = x_in.shape
    mm = _dot_nt if nt else _dot

    def body(*refs):
        a_refs, w_refs = refs[:P], refs[P:2 * P]
        x_ref, g_ref, dxo_ref, dx_ref, dxb_ref, dg_ref = refs[2 * P:]
        i = pl.program_id(0)
        dh = None
        for a_ref, w_ref in zip(a_refs, w_refs):
            for j in range(G):
                if a_cols:
                    kw = a_ref.shape[1] // G
                    term = mm(a_ref[:, j * kw:(j + 1) * kw], w_ref[j])
                else:
                    term = mm(a_ref[j], w_ref[j])
                dh = term if dh is None else dh + term
        xf = x_ref[...]
        r = lax.rsqrt(jnp.mean(xf * xf, axis=-1, keepdims=True) + EPS)
        xh = xf * r
        _accumulate(dg_ref, i == 0, _colsum(dh * xh))
        dxh = dh * g_ref[...]
        dx = dxo_ref[...] + r * (dxh - xh * jnp.mean(dxh * xh, axis=-1, keepdims=True))
        dx_ref[...] = dx
        dxb_ref[...] = dx.astype(BF16)

    row = pl.BlockSpec((TM, D), lambda i: (i, 0))
    vec = pl.BlockSpec((1, D), lambda i: (0, 0))
    if a_cols:
        a_specs = [pl.BlockSpec((TM, a.shape[1]), lambda i: (i, 0)) for a in a_list]
    else:
        a_specs = [pl.BlockSpec((G, TM, a.shape[2]), lambda i: (0, i, 0)) for a in a_list]
    w_specs = [_resident(w.shape) for w in w_list]
    return _pallas(body, [*a_list, *w_list, x_in, g, dx_out], name=name, grid=(L // TM,),
                   in_specs=a_specs + w_specs + [row, vec, row], out_specs=[row, row, vec],
                   out_shape=[jax.ShapeDtypeStruct((L, D), F32), jax.ShapeDtypeStruct((L, D), BF16),
                              jax.ShapeDtypeStruct((1, D), F32)], comm=comm)


def _mix_out(cat, wout, x1, g_next, name):
    L, K = cat.shape
    D = wout.shape[1]

    def body(c_ref, w_ref, x_ref, g_ref, xn_ref, hn_ref):
        xn = x_ref[...] + _dot(c_ref[...], w_ref[...])
        xn_ref[...] = xn
        r = lax.rsqrt(jnp.mean(xn * xn, axis=-1, keepdims=True) + EPS)
        hn_ref[...] = (xn * r * g_ref[...]).astype(BF16)

    row = pl.BlockSpec((TMS, D), lambda i: (i, 0))
    return _pallas(body, [cat, wout, x1, g_next], name=name, grid=(L // TMS,),
                   in_specs=[pl.BlockSpec((TMS, K), lambda i: (i, 0)), pl.BlockSpec((K, D), lambda i: (0, 0)), row,
                             pl.BlockSpec((1, D), lambda i: (0, 0))],
                   out_specs=[row, row],
                   out_shape=[jax.ShapeDtypeStruct((L, D), F32), jax.ShapeDtypeStruct((L, D), BF16)])


def _s5_disc(lr, li, ldt, brc, bic):
    dt = jnp.exp(ldt)
    mag = jnp.exp(lr * dt)
    are = mag * jnp.cos(li * dt)
    aim = mag * jnp.sin(li * dt)
    den = lr * lr + li * li
    nre = are - 1.0
    fre = (nre * lr + aim * li) / den
    fim = (aim * lr - nre * li) / den
    return are, aim, fre * brc - fim * bic, fre * bic + fim * brc


def _s5_params_fwd(lr, li, ldt, brc, bic, crc, cic):
    def body(lr_ref, li_ref, ldt_ref, br_ref, bi_ref, cr_ref, ci_ref, are_ref, aim_ref, bre_ref, bim_ref, cre_ref, cim_ref):
        are, aim, bre, bim = _s5_disc(lr_ref[...], li_ref[...], ldt_ref[...], br_ref[...], bi_ref[...])
        are_ref[...] = are
        aim_ref[...] = aim
        bre_ref[...] = bre.astype(BF16)
        bim_ref[...] = bim.astype(BF16)
        cre_ref[...] = cr_ref[...].astype(BF16)
        cim_ref[...] = ci_ref[...].astype(BF16)

    vec = jax.ShapeDtypeStruct((1, S5_N), F32)
    return pl.pallas_call(
        body, name="s5_params_fwd",
        out_shape=[vec, vec, jax.ShapeDtypeStruct((LANES, S5_N), BF16), jax.ShapeDtypeStruct((LANES, S5_N), BF16),
                   jax.ShapeDtypeStruct((S5_N, LANES), BF16), jax.ShapeDtypeStruct((S5_N, LANES), BF16)],
        compiler_params=pltpu.CompilerParams(vmem_limit_bytes=VMEM_LIMIT))(lr, li, ldt, brc, bic, crc, cic)


def _s5_params_bwd(lr, li, ldt, brc, bic, dare, daim, dbre, dbim):
    def body(lr_ref, li_ref, ldt_ref, br_ref, bi_ref, dare_ref, daim_ref, dbre_ref, dbim_ref,
             glr_ref, gli_ref, gldt_ref, gbr_ref, gbi_ref):
        _, vjp = jax.vjp(_s5_disc, lr_ref[...], li_ref[...], ldt_ref[...], br_ref[...], bi_ref[...])
        glr, gli, gldt, gbr, gbi = vjp((dare_ref[...], daim_ref[...], dbre_ref[...], dbim_ref[...]))
        glr_ref[...] = glr
        gli_ref[...] = gli
        gldt_ref[...] = gldt
        gbr_ref[...] = gbr
        gbi_ref[...] = gbi

    vec = jax.ShapeDtypeStruct((1, S5_N), F32)
    mat = jax.ShapeDtypeStruct((LANES, S5_N), F32)
    return pl.pallas_call(
        body, name="s5_params_bwd", out_shape=[vec, vec, vec, mat, mat],
        compiler_params=pltpu.CompilerParams(vmem_limit_bytes=VMEM_LIMIT))(lr, li, ldt, brc, bic, dare, daim, dbre, dbim)


def _gelu_parts(y):
    th = jnp.tanh(GELU_C0 * (y + GELU_C1 * y * y * y))
    return 0.5 * y * (1.0 + th), th


def _state_rows(q, T):
    return pl.ds(q % S5_HALF, T, stride=S5_HALF)


def _load_tiles(bufs, ct, T, dtype):
    return jnp.concatenate([bufs[q // S5_HALF][_state_rows(q, T), :].astype(dtype) for q in range(4 * ct, 4 * ct + 4)],
                           axis=1)


def _store_tiles(bufs, ct, T, value):
    for k, q in enumerate(range(4 * ct, 4 * ct + 4)):
        bufs[q // S5_HALF][_state_rows(q, T), :] = value[:, k * LANES:(k + 1) * LANES]


def _s5_fwd(u, are, aim, bre, bim, cre, cim, d_skip, wglu, bglu, comm=()):
    L = u.shape[0]
    T = min(TS, L)
    n = L // T

    def body(u_ref, are_ref, aim_ref, bre_ref, bim_ref, cre_ref, cim_ref, d_ref, wg_ref, bg_ref,
             sre_lo, sre_hi, sim_lo, sim_hi, y_ref, o_ref, st_re, st_im):
        i = pl.program_id(0)
        sre, sim = (sre_lo, sre_hi), (sim_lo, sim_hi)

        @pl.when(i == 0)
        def _():
            st_re[...] = jnp.zeros_like(st_re)
            st_im[...] = jnp.zeros_like(st_im)

        uf = u_ref[...]
        ub = uf.astype(BF16)
        for ct in range(4):
            uq = ub[:, ct * LANES:(ct + 1) * LANES]
            win = slice(4 * ct * LANES, 4 * (ct + 1) * LANES)
            _store_tiles(sre, ct, T, _dot(uq, bre_ref[:, win]))
            _store_tiles(sim, ct, T, _dot(uq, bim_ref[:, win]))
        halves = [slice(h * S5_HALF, (h + 1) * S5_HALF) for h in range(2)]
        a_re = [are_ref[hs, :] for hs in halves]
        a_im = [aim_ref[hs, :] for hs in halves]

        def step(t, carry):
            rows = pl.ds(pl.multiple_of(t * S5_HALF, S5_HALF), S5_HALF)
            out = []
            for h in range(2):
                s_re, s_im = carry[2 * h], carry[2 * h + 1]
                n_re = a_re[h] * s_re - a_im[h] * s_im + sre[h][rows, :]
                n_im = a_re[h] * s_im + a_im[h] * s_re + sim[h][rows, :]
                sre[h][rows, :] = n_re
                sim[h][rows, :] = n_im
                out += [n_re, n_im]
            return tuple(out)

        init = (st_re[halves[0], :], st_im[halves[0], :], st_re[halves[1], :], st_im[halves[1], :])
        fin = lax.fori_loop(0, T, step, init, unroll=4)
        for h in range(2):
            st_re[halves[h], :] = fin[2 * h]
            st_im[halves[h], :] = fin[2 * h + 1]
        tiles = []
        for ct in range(4):
            win = slice(4 * ct * LANES, 4 * (ct + 1) * LANES)
            tiles.append(_dot(_load_tiles(sre, ct, T, BF16), cre_ref[win, :])
                         - _dot(_load_tiles(sim, ct, T, BF16), cim_ref[win, :]))
        y = jnp.concatenate(tiles, axis=1) + d_ref[...] * uf
        y_ref[...] = y
        yg, _ = _gelu_parts(y)
        gate = _sigmoid(_dot(yg.astype(BF16), wg_ref[...]) + bg_ref[...])
        o_ref[...] = (yg * gate).astype(BF16)

    const = lambda shape: pl.BlockSpec(shape, lambda i: (0, 0))
    sspec = pl.BlockSpec((T * S5_HALF, LANES), lambda i: (i, 0))
    sshape = jax.ShapeDtypeStruct((L * S5_HALF, LANES), F32)
    chunk = pl.BlockSpec((T, S5_W), lambda i: (i, 0))
    return _pallas(body, [u, are, aim, bre, bim, cre, cim, d_skip, wglu, bglu], name="s5_fwd", grid=(n,),
                   in_specs=[chunk, const((S5_TILES, LANES)), const((S5_TILES, LANES)),
                             const((LANES, S5_N)), const((LANES, S5_N)), const((S5_N, LANES)), const((S5_N, LANES)),
                             const((1, S5_W)), const((S5_W, S5_W)), const((1, S5_W))],
                   out_specs=[sspec] * 4 + [chunk, chunk],
                   out_shape=[sshape] * 4 + [jax.ShapeDtypeStruct((L, S5_W), F32), jax.ShapeDtypeStruct((L, S5_W), BF16)],
                   scratch_shapes=[pltpu.VMEM((S5_TILES, LANES), F32), pltpu.VMEM((S5_TILES, LANES), F32)], comm=comm)


def _s5_bwd(dm, y_pre, u, states, are, aim, bre, bim, cre, cim, d_skip, wglu, bglu, comm=()):
    L = u.shape[0]
    T = min(TS, L)
    n = L // T

    def body(dm_ref, y_ref, u_ref, sre_lo, sre_hi, sim_lo, sim_hi, pre_lo, pre_hi, pim_lo, pim_hi,
             are_ref, aim_ref, bre_ref, bim_ref, cre_ref, cim_ref, d_ref, wg_ref, bg_ref,
             du_ref, dwg_ref, dbg_ref, dd_ref, dcre_ref, dcim_ref, dbre_ref, dbim_ref, dare_ref, daim_ref,
             gre_lo, gre_hi, gim_lo, gim_hi, car_re, car_im):
        i = pl.program_id(0)
        first = i == 0
        sre, sim = (sre_lo, sre_hi), (sim_lo, sim_hi)
        gre, gim = (gre_lo, gre_hi), (gim_lo, gim_hi)
        pre, pim = (pre_lo, pre_hi), (pim_lo, pim_hi)

        @pl.when(first)
        def _():
            car_re[...] = jnp.zeros_like(car_re)
            car_im[...] = jnp.zeros_like(car_im)
            dcre_ref[...] = jnp.zeros_like(dcre_ref)
            dcim_ref[...] = jnp.zeros_like(dcim_ref)
            dbre_ref[...] = jnp.zeros_like(dbre_ref)
            dbim_ref[...] = jnp.zeros_like(dbim_ref)

        y = y_ref[...]
        uf = u_ref[...]
        yg, th = _gelu_parts(y)
        dgelu = 0.5 * (1.0 + th) + 0.5 * y * (1.0 - th * th) * GELU_C0 * (1.0 + 3.0 * GELU_C1 * y * y)
        ygb = yg.astype(BF16)
        sg = _sigmoid(_dot(ygb, wg_ref[...]) + bg_ref[...])
        dout = dm_ref[...].astype(F32)
        dgp = dout * yg * sg * (1.0 - sg)
        dgpb = dgp.astype(BF16)
        dyg = dout * sg + _dot_nt(dgpb, wg_ref[...])
        _accumulate(dwg_ref, first, _dot_tn(ygb, dgpb))
        _accumulate(dbg_ref, first, _colsum(dgp))
        dy = dyg * dgelu
        _accumulate(dd_ref, first, _colsum(dy * uf))
        dyb = dy.astype(BF16)
        ub = uf.astype(BF16)

        for ct in range(4):
            win = slice(4 * ct * LANES, 4 * (ct + 1) * LANES)
            dyq = dyb[:, ct * LANES:(ct + 1) * LANES]
            dcre_ref[win, :] += _dot_tn(_load_tiles(sre, ct, T, BF16), dyq)
            dcim_ref[win, :] -= _dot_tn(_load_tiles(sim, ct, T, BF16), dyq)
            _store_tiles(gre, ct, T, _dot_nt(dyq, cre_ref[win, :]))
            _store_tiles(gim, ct, T, -_dot_nt(dyq, cim_ref[win, :]))

        halves = [slice(h * S5_HALF, (h + 1) * S5_HALF) for h in range(2)]
        a_re = [are_ref[hs, :] for hs in halves]
        a_im = [aim_ref[hs, :] for hs in halves]

        def adjoint(t, h, g_re, g_im):
            rows = pl.ds(pl.multiple_of(t * S5_HALF, S5_HALF), S5_HALF)
            n_re = gre[h][rows, :] + a_re[h] * g_re + a_im[h] * g_im
            n_im = gim[h][rows, :] + a_re[h] * g_im - a_im[h] * g_re
            gre[h][rows, :] = n_re
            gim[h][rows, :] = n_im
            return n_re, n_im

        def step(k, carry):
            out = []
            for h in range(2):
                out += adjoint(T - 1 - k, h, carry[2 * h], carry[2 * h + 1])
            return tuple(out)

        init = (car_re[halves[0], :], car_im[halves[0], :], car_re[halves[1], :], car_im[halves[1], :])
        fin = lax.fori_loop(0, T, step, init, unroll=4)
        keep = (i < n - 1).astype(F32)
        later, earlier = slice(S5_HALF, T * S5_HALF), slice(0, (T - 1) * S5_HALF)
        fold = lambda v: jnp.sum(v.reshape(T - 1, S5_HALF, LANES), axis=0)
        for h in range(2):
            g_re, g_im = fin[2 * h], fin[2 * h + 1]
            car_re[halves[h], :] = g_re
            car_im[halves[h], :] = g_im
            p_re, p_im = pre[h][...] * keep, pim[h][...] * keep
            da_re = fold(gre[h][later, :] * sre[h][earlier, :] + gim[h][later, :] * sim[h][earlier, :])
            da_im = fold(gim[h][later, :] * sre[h][earlier, :] - gre[h][later, :] * sim[h][earlier, :])
            da_re = da_re + g_re * p_re + g_im * p_im
            da_im = da_im + g_im * p_re - g_re * p_im

            @pl.when(first)
            def _():
                dare_ref[halves[h], :] = da_re
                daim_ref[halves[h], :] = da_im

            @pl.when(jnp.logical_not(first))
            def _():
                dare_ref[halves[h], :] += da_re
                daim_ref[halves[h], :] += da_im

        tiles = []
        for ct in range(4):
            uq = ub[:, ct * LANES:(ct + 1) * LANES]
            win = slice(4 * ct * LANES, 4 * (ct + 1) * LANES)
            g_re, g_im = _load_tiles(gre, ct, T, BF16), _load_tiles(gim, ct, T, BF16)
            tiles.append(d_ref[:, ct * LANES:(ct + 1) * LANES] * dy[:, ct * LANES:(ct + 1) * LANES]
                         + _dot_nt(g_re, bre_ref[:, win]) + _dot_nt(g_im, bim_ref[:, win]))
            dbre_ref[:, win] += _dot_tn(uq, g_re)
            dbim_ref[:, win] += _dot_tn(uq, g_im)
        du_ref[...] = jnp.concatenate(tiles, axis=1).astype(BF16)

    rev = lambda i: (n - 1 - i, 0)
    const = lambda shape: pl.BlockSpec(shape, lambda i: (0, 0))
    chunk = pl.BlockSpec((T, S5_W), rev)
    sspec = pl.BlockSpec((T * S5_HALF, LANES), rev)
    pspec = pl.BlockSpec((S5_HALF, LANES), lambda i: (jnp.maximum((n - 1 - i) * T - 1, 0), 0))
    tile = jax.ShapeDtypeStruct((S5_TILES, LANES), F32)
    vec = jax.ShapeDtypeStruct((1, S5_W), F32)
    sbuf = pltpu.VMEM((T * S5_HALF, LANES), F32)
    return _pallas(
        body, [dm, y_pre, u, *states, *states, are, aim, bre, bim, cre, cim, d_skip, wglu, bglu],
        name="s5_bwd", grid=(n,),
        in_specs=[chunk, chunk, chunk] + [sspec] * 4 + [pspec] * 4 + [
            const((S5_TILES, LANES)), const((S5_TILES, LANES)), const((LANES, S5_N)), const((LANES, S5_N)),
            const((S5_N, LANES)), const((S5_N, LANES)), const((1, S5_W)), const((S5_W, S5_W)), const((1, S5_W))],
        out_specs=[chunk, const((S5_W, S5_W)), const((1, S5_W)), const((1, S5_W)), const((S5_N, LANES)),
                   const((S5_N, LANES)), const((LANES, S5_N)), const((LANES, S5_N)), const((S5_TILES, LANES)),
                   const((S5_TILES, LANES))],
        out_shape=[jax.ShapeDtypeStruct((L, S5_W), BF16), jax.ShapeDtypeStruct((S5_W, S5_W), F32), vec, vec,
                   jax.ShapeDtypeStruct((S5_N, LANES), F32), jax.ShapeDtypeStruct((S5_N, LANES), F32),
                   jax.ShapeDtypeStruct((LANES, S5_N), F32), jax.ShapeDtypeStruct((LANES, S5_N), F32), tile, tile],
        scratch_shapes=[sbuf, sbuf, sbuf, sbuf, pltpu.VMEM((S5_TILES, LANES), F32), pltpu.VMEM((S5_TILES, LANES), F32)],
        comm=comm)


_EYE8 = np.eye(8, dtype=np.float32)


def _compact_b(b):
    return jnp.einsum("akpc,kj->jcakp", b.reshape(4, 8, S5_P, S5_GC), _EYE8).reshape(LANES, S5_N)


def _uncompact_b(m):
    return jnp.einsum("kcakp->akpc", m.reshape(8, S5_GC, 4, 8, S5_P)).reshape(S5_G, S5_P, S5_GC)


_HEAD_MEAN = np.kron(np.eye(CONV_W // CONV_HD, dtype=np.float32), np.full((CONV_HD, CONV_HD), 1.0 / CONV_HD, np.float32))


def _head_mean(v, m):
    hi = v.astype(BF16)
    lo = (v - hi.astype(F32)).astype(BF16)
    return _dot(hi, m) + _dot(lo, m)


def _head_norm(zc, m):
    d = zc - _head_mean(zc, m)
    rstd = lax.rsqrt(_head_mean(d * d, m) + EPS)
    return d * rstd, rstd


def _taps_by_phase(first):
    groups = {}
    for k in range(CONV_K):
        m, s = divmod(first + k, 8)
        groups.setdefault(s, []).append((m, k))
    return groups


def _causal_taps(buf, w_ref, r0, first, flip):
    acc = None
    for s, taps in sorted(_taps_by_phase(first).items()):
        rows = CONV_SB + (8 if s else 0)
        y = None
        for m, k in taps:
            kk = CONV_K - 1 - k if flip else k
            term = w_ref[kk:kk + 1, :] * buf[pl.ds(r0 + 8 * m, rows), :]
            y = term if y is None else y + term
        y = y[s:s + CONV_SB, :]
        acc = y if acc is None else acc + y
    return acc


def _conv_fwd(u, o_s5, wdw, bdw, lng, lnb, mavg, comm=()):
    L = u.shape[0]
    T = min(TS, L)
    n = L // T
    first_tap = HALO - (CONV_K - 1)

    def body(v1_ref, v2_ref, s5_ref, w_ref, b_ref, g_ref, be_ref, m_ref, zc_ref, o_ref, zbuf):
        i = pl.program_id(0)

        @pl.when(i == 0)
        def _():
            zbuf[0:HALO, :] = jnp.zeros((HALO, CONV_W), F32)

        zbuf[HALO:HALO + T, :] = v1_ref[...] * _sigmoid(v2_ref[...])
        for r0 in range(0, T, CONV_SB):
            zc_ref[r0:r0 + CONV_SB, :] = b_ref[...] + _causal_taps(zbuf, w_ref, r0, first_tap, False)
        zbuf[0:HALO, :] = zbuf[T:T + HALO, :]
        zn, _ = _head_norm(zc_ref[...], m_ref[...])
        zz = zn * g_ref[...] + be_ref[...]
        o_ref[:, 0:S5_W] = s5_ref[...]
        o_ref[:, S5_W:S5_W + CONV_W] = (zz * _sigmoid(zz)).astype(BF16)

    const = lambda shape: pl.BlockSpec(shape, lambda i: (0, 0))
    vec = const((1, CONV_W))
    return _pallas(body, [u, u, o_s5, wdw, bdw, lng, lnb, mavg], name="conv_fwd", grid=(n,),
                   in_specs=[pl.BlockSpec((T, CONV_W), lambda i: (i, 1)), pl.BlockSpec((T, CONV_W), lambda i: (i, 2)),
                             pl.BlockSpec((T, S5_W), lambda i: (i, 0)), const((CONV_K, CONV_W)), vec, vec, vec,
                             const((CONV_W, CONV_W))],
                   out_specs=[pl.BlockSpec((T, CONV_W), lambda i: (i, 0)),
                              pl.BlockSpec((T, S5_W + CONV_W), lambda i: (i, 0))],
                   out_shape=[jax.ShapeDtypeStruct((L, CONV_W), F32), jax.ShapeDtypeStruct((L, S5_W + CONV_W), BF16)],
                   scratch_shapes=[pltpu.VMEM((T + HALO, CONV_W), F32)], comm=comm)


def _conv_bwd(dm, zc, u, du_s5, wdw, lng, lnb, mavg, comm=()):
    L = u.shape[0]
    T = min(TS, L)
    n = L // T
    hb = T // HALO
    first_tap = HALO - (CONV_K - 1)

    def body(dm_ref, zc_ref, v1_ref, v2_ref, p1_ref, p2_ref, s5_ref, w_ref, g_ref, be_ref, m_ref,
             du_ref, dw_ref, db_ref, dg_ref, dbe_ref, zbuf, dzbuf, head):
        i = pl.program_id(0)
        first = i == 0

        @pl.when(first)
        def _():
            head[...] = jnp.zeros_like(head)
            dw_ref[...] = jnp.zeros_like(dw_ref)

        zn, rstd = _head_norm(zc_ref[...], m_ref[...])
        zz = zn * g_ref[...] + be_ref[...]
        sg = _sigmoid(zz)
        dzz = dm_ref[...].astype(F32) * sg * (1.0 + zz * (1.0 - sg))
        _accumulate(dbe_ref, first, _colsum(dzz))
        _accumulate(dg_ref, first, _colsum(dzz * zn))
        dzn = dzz * g_ref[...]
        dzc = rstd * (dzn - _head_mean(dzn, m_ref[...]) - zn * _head_mean(dzn * zn, m_ref[...]))
        _accumulate(db_ref, first, _colsum(dzc))

        dzbuf[0:T, :] = dzc
        dzbuf[T:T + HALO, :] = head[...]
        head[...] = dzbuf[0:HALO, :]
        keep = (i < n - 1).astype(F32)
        zbuf[0:HALO, :] = p1_ref[...] * _sigmoid(p2_ref[...]) * keep
        zbuf[HALO:HALO + T, :] = v1_ref[...] * _sigmoid(v2_ref[...])
        du_ref[:, 0:S5_W] = s5_ref[...]

        for r0 in range(0, T, CONV_SB):
            rows = slice(r0, r0 + CONV_SB)
            dzc_b = dzbuf[rows, :]
            for s, taps in sorted(_taps_by_phase(first_tap).items()):
                pad = ([jnp.zeros((s, CONV_W), F32)] if s else []) + [dzc_b] + ([jnp.zeros((8 - s, CONV_W), F32)] if s else [])
                shifted = jnp.concatenate(pad, axis=0) if s else dzc_b
                n_rows = shifted.shape[0]
                for m, k in taps:
                    prod = shifted * zbuf[pl.ds(r0 + 8 * m, n_rows), :]
                    dw_ref[8 * k:8 * k + 8, :] += jnp.sum(prod.reshape(n_rows // 8, 8, CONV_W), axis=0)
            dz = _causal_taps(dzbuf, w_ref, r0, 0, True)
            v1 = v1_ref[rows, :]
            sg2 = _sigmoid(v2_ref[rows, :])
            du_ref[rows, S5_W:S5_W + CONV_W] = (dz * sg2).astype(BF16)
            du_ref[rows, S5_W + CONV_W:S5_W + 2 * CONV_W] = (dz * v1 * sg2 * (1.0 - sg2)).astype(BF16)

    rev = lambda c: (lambda i: (n - 1 - i, c))
    prev = lambda c: (lambda i: (jnp.maximum((n - 1 - i) * hb - 1, 0), c))
    const = lambda shape: pl.BlockSpec(shape, lambda i: (0, 0))
    vec = const((1, CONV_W))
    vshape = jax.ShapeDtypeStruct((1, CONV_W), F32)
    return _pallas(
        body, [dm, zc, u, u, u, u, du_s5, wdw, lng, lnb, mavg], name="conv_bwd", grid=(n,),
        in_specs=[pl.BlockSpec((T, CONV_W), rev(1)), pl.BlockSpec((T, CONV_W), rev(0)),
                  pl.BlockSpec((T, CONV_W), rev(1)), pl.BlockSpec((T, CONV_W), rev(2)),
                  pl.BlockSpec((HALO, CONV_W), prev(1)), pl.BlockSpec((HALO, CONV_W), prev(2)),
                  pl.BlockSpec((T, S5_W), rev(0)), const((CONV_K, CONV_W)), vec, vec, const((CONV_W, CONV_W))],
        out_specs=[pl.BlockSpec((T, S5_W + 2 * CONV_W), rev(0)), const((8 * HALO, CONV_W)), vec, vec, vec],
        out_shape=[jax.ShapeDtypeStruct((L, S5_W + 2 * CONV_W), BF16), jax.ShapeDtypeStruct((8 * HALO, CONV_W), F32),
                   vshape, vshape, vshape],
        scratch_shapes=[pltpu.VMEM((T + HALO, CONV_W), F32), pltpu.VMEM((T + HALO, CONV_W), F32),
                        pltpu.VMEM((HALO, CONV_W), F32)], comm=comm)


def _gather_all(v):
    rows, cols = v.shape

    def body(x_ref, out_ref, send_sems, recv_sems, local_sem):
        x, y, c, chips = _position()
        me, sibling = (x, y, c), (x, y, 1 - c)

        def block(px, py, pc):
            return out_ref.at[pl.ds((4 * px + 2 * py + pc) * rows, rows), :]

        def copy(k, blk, to, src=None):
            return pltpu.make_async_remote_copy(
                src_ref=block(*blk) if src is None else src, dst_ref=block(*blk), send_sem=send_sems.at[k],
                recv_sem=recv_sems.at[k], device_id=to, device_id_type=MESH)

        mine = pltpu.make_async_copy(x_ref, block(*me), local_sem)
        mine.start()
        first = [copy(0, me, sibling, src=x_ref)]
        first += [copy(1 + j, me, (*chip, c), src=x_ref) for j, chip in enumerate(chips)]
        for cp in first:
            cp.start()
        passed = [copy(4 + j, (*chip, c), sibling) for j, chip in enumerate(chips)]
        for j, chip in enumerate(chips):
            copy(1 + j, (*chip, c), me).wait_recv()
            passed[j].start()
        copy(0, sibling, me).wait_recv()
        for j, chip in enumerate(chips):
            copy(4 + j, (*chip, 1 - c), me).wait_recv()
        for cp in first + passed:
            cp.wait_send()
        mine.wait()

    return pl.pallas_call(
        body, name="gather_small",
        in_specs=[pl.BlockSpec(memory_space=pltpu.VMEM)], out_specs=pl.BlockSpec(memory_space=pltpu.VMEM),
        out_shape=jax.ShapeDtypeStruct((N_DEV * rows, cols), v.dtype),
        scratch_shapes=[pltpu.SemaphoreType.DMA((7,)), pltpu.SemaphoreType.DMA((7,)), pltpu.SemaphoreType.DMA],
        compiler_params=pltpu.CompilerParams(vmem_limit_bytes=VMEM_LIMIT))(v)


def _adamw(w, g, m, v):
    m = ADAM_B1 * m + (1.0 - ADAM_B1) * g
    v = ADAM_B2 * v + (1.0 - ADAM_B2) * jnp.square(g)
    m_hat = m / (1.0 - ADAM_B1 ** ADAM_STEP)
    v_hat = v / (1.0 - ADAM_B2 ** ADAM_STEP)
    return -ADAM_LR * (m_hat / (jnp.sqrt(v_hat) + ADAM_EPS) + ADAM_WD * w), m, v


def _sum_slots(recv, name, comm=()):
    _, rows, cols = recv.shape
    tr = _row_tile(rows, cols)

    def body(r_ref, o_ref):
        acc = r_ref[0].astype(F32)
        for s in range(1, N_CHIPS):
            acc = acc + r_ref[s].astype(F32)
        o_ref[...] = acc

    return _pallas(body, [recv], name=name, grid=(rows // tr,),
                   in_specs=[pl.BlockSpec((N_CHIPS, tr, cols), lambda i: (0, i, 0))],
                   out_specs=[pl.BlockSpec((tr, cols), lambda i: (i, 0))],
                   out_shape=[jax.ShapeDtypeStruct((rows, cols), F32)], comm=comm)[0]


def _add_halves(mine, theirs, name):
    slots, rows, cols = mine.shape
    tr = _row_tile(rows, cols * slots)

    def body(a_ref, b_ref, o_ref):
        o_ref[...] = (a_ref[...].astype(F32) + b_ref[...].astype(F32)).astype(o_ref.dtype)

    spec = pl.BlockSpec((slots, tr, cols), lambda i: (0, i, 0))
    return _pallas(body, [mine, theirs], name=name, grid=(rows // tr,), in_specs=[spec, spec], out_specs=[spec],
                   out_shape=[jax.ShapeDtypeStruct(mine.shape, mine.dtype)])[0]


def _adamw_sharded(w, parts, m, v, name, comm=()):
    rows, cols = w.shape
    tr = _row_tile(rows, cols)
    n = len(parts)

    def body(w_ref, *refs):
        p_refs, (m_ref, v_ref, g_ref, d_ref, nm_ref, nv_ref) = refs[:n], refs[n:]
        g = p_refs[0][...]
        for p_ref in p_refs[1:]:
            g = g + p_ref[...]
        g_ref[...] = g
        d_ref[...], nm_ref[...], nv_ref[...] = _adamw(w_ref[...], g, m_ref[...], v_ref[...])

    spec = pl.BlockSpec((tr, cols), lambda i: (i, 0))
    shape = jax.ShapeDtypeStruct((rows, cols), F32)
    return _pallas(body, [w, *parts, m, v], name=name, grid=(rows // tr,), in_specs=[spec] * (n + 3),
                   out_specs=[spec] * 4, out_shape=[shape] * 4, comm=comm)


def _adamw_small(w, gathered, m, v):
    rows, cols = w.shape

    def body(w_ref, a_ref, m_ref, v_ref, g_ref, d_ref, nm_ref, nv_ref):
        g = a_ref[0:rows, :]
        for dev in range(1, N_DEV):
            g = g + a_ref[dev * rows:(dev + 1) * rows, :]
        g_ref[...] = g
        d_ref[...], nm_ref[...], nv_ref[...] = _adamw(w_ref[...], g, m_ref[...], v_ref[...])

    shape = jax.ShapeDtypeStruct((rows, cols), F32)
    return pl.pallas_call(
        body, name="adamw_small", out_shape=[shape] * 4,
        compiler_params=pltpu.CompilerParams(vmem_limit_bytes=VMEM_LIMIT))(w, gathered, m, v)


PACK_TILE = 8 * LANES


def _pack_small(vals, last_row):
    rows = []
    for name in SMALL:
        flat = vals[name].reshape(-1).astype(F32)
        rows.append(jnp.pad(flat, (0, -flat.size % PACK_TILE)).reshape(-1, LANES))
    rows.append(jnp.pad(last_row, ((0, 7), (0, 0))))
    return jnp.concatenate(rows, axis=0)


def _unpack_small(packed, like):
    out, r = {}, 0
    for name in SMALL:
        size = like[name].size
        out[name] = packed[r:r + -(-size // LANES)].reshape(-1)[:size].reshape(like[name].shape)
        r += 8 * -(-size // PACK_TILE)
    return out, packed[r, 0]


def _shard2d(name, v):
    v = v.reshape(v.shape[-2:])
    return v.T if name in FFN_T else v


def _unshard(name, v, shape):
    return (v.T if name in FFN_T else v).reshape(shape)


def _train_step(x3, tgt3, wts, ms, vs):
    x, tgt = x3[0], tgt3[0]
    L, D = x.shape
    row = lambda v: v.reshape(1, -1)
    shards = {k: _shard2d(k, wts[k]) for k in SHARDED}
    first = ("ffn1_w_gate", "ffn1_w_up")
    sends = {k: _cast_bf16(shards[k], "cast_" + k) for k in first}
    sends["conv_w_dw"] = shards["conv_w_dw"]
    gat = {k: _Gather(sends[k]) for k in first}
    rest = [k for k in SHARDED if k not in sends]
    sends.update(zip(rest, _cast_many([shards[k] for k in rest], "cast_rest", comm=[gat[k] for k in first])))
    gat.update({k: _Gather(sends[k]) for k in SHARDED if k not in gat})
    w = lambda k: gat[k].result[0]

    s = {k: wts[k] for k in SMALL}
    lr, li = s["s5_lam_re"].reshape(1, S5_N), s["s5_lam_im"].reshape(1, S5_N)
    ldt = jnp.repeat(s["s5_log_dt"].reshape(S5_G), S5_P).reshape(1, S5_N)
    brc, bic = _compact_b(s["s5_b_re"].reshape(S5_G, S5_P, S5_GC)), _compact_b(s["s5_b_im"].reshape(S5_G, S5_P, S5_GC))
    crc = _compact_b(s["s5_c_re"].reshape(S5_G, S5_GC, S5_P).transpose(0, 2, 1)).T
    cic = _compact_b(s["s5_c_im"].reshape(S5_G, S5_GC, S5_P).transpose(0, 2, 1)).T
    d_skip, b_glu = row(s["s5_d"]), row(s["s5_b_glu"])
    b_dw, ln_g, ln_b = row(s["conv_b_dw"]), row(s["conv_ln_g"]), row(s["conv_ln_b"])
    g1, gm, g2, gf = row(s["ffn1_norm"]), row(s["mix_norm"]), row(s["ffn2_norm"]), row(s["final_norm"])
    mavg = jnp.asarray(_HEAD_MEAN, dtype=BF16)

    h1 = _rms_fwd(x, g1, "rms1")
    a1, b1, act1 = _ffn_up(h1, w("ffn1_w_gate"), w("ffn1_w_up"), "ffn1_up", comm=[gat["ffn1_w_down"]])
    x1, h2 = _ffn_down(act1, w("ffn1_w_down"), x, gm, "ffn1_down",
                       comm=[gat["w_in"], gat["s5_w_glu"], gat["conv_w_dw"], gat["w_out"]])
    u = _mm_grouped(h2, w("w_in"), "in_proj")
    are, aim, bre, bim, cre, cim = _s5_params_fwd(lr, li, ldt, brc, bic, crc, cic)
    are_t, aim_t = are.reshape(S5_TILES, LANES), aim.reshape(S5_TILES, LANES)
    w_glu = w("s5_w_glu").reshape(S5_W, S5_W)
    *states, y_pre, o_s5 = _s5_fwd(u, are_t, aim_t, bre, bim, cre, cim, d_skip, w_glu, b_glu,
                                   comm=[gat["ffn2_w_gate"], gat["ffn2_w_up"]])
    w_dw = w("conv_w_dw").transpose(1, 0, 2).reshape(CONV_K, CONV_W)
    zc, cat = _conv_fwd(u, o_s5, w_dw, b_dw, ln_g, ln_b, mavg, comm=[gat["ffn2_w_down"]])
    w_out = w("w_out").reshape(-1, D)
    x2, h3 = _mix_out(cat, w_out, x1, g2, "mix_out")
    a2, b2, act2 = _ffn_up(h3, w("ffn2_w_gate"), w("ffn2_w_up"), "ffn2_up")
    dx3, dx3b, loss_part, d_gf = _ffn_down_loss(act2, w("ffn2_w_down"), x2, gf, tgt, "ffn2_down_loss")

    gs, sc, waiting = {"final_norm": d_gf}, {}, []

    def grad(key, g):
        if g.shape[1] % (2 * BF16_ROWS) == 0 and g.dtype == BF16:
            waiting.append(_SwapHalf(g, key))
        else:
            sc[key] = _Scatter(g)
            waiting.append(sc[key])

    def carry(call, *args, **kw):
        ops = list(waiting)
        waiting.clear()
        res = call(*args, comm=ops, **kw)
        for op in ops:
            if isinstance(op, _SwapHalf):
                sc[op.key] = _Scatter(_add_halves(*op.result, "add_" + op.key))
                waiting.append(sc[op.key])
        return res

    da2, db2 = _ffn_bwd_act(dx3b, w("ffn2_w_down"), a2, b2, "ffn2_bwd_act")
    grad("ffn2_w_down", _mm_tn(act2, dx3b[None], 0.5, "ffn2_dwd", N_CHIPS))
    grad("ffn2_w_gate", carry(_mm_tn, da2, h3[None], 1.0, "ffn2_dwg", N_CHIPS))
    grad("ffn2_w_up", carry(_mm_tn, db2, h3[None], 1.0, "ffn2_dwu", N_CHIPS))
    dx2, dx2b, gs["ffn2_norm"] = carry(_mm_rmsbwd, [da2, db2], [w("ffn2_w_gate"), w("ffn2_w_up")], False, False, x2, g2,
                                       dx3, "ffn2_bwd_dx")

    dm = _mm_nt(dx2b, w_out, "mix_bwd")
    grad("w_out", _mm_tn(cat[None], dx2b[None], 1.0, "dwout", 1).reshape(N_CHIPS, -1, D))
    (du_s5, d_wglu, gs["s5_b_glu"], gs["s5_d"], d_crc, d_cic, d_bre, d_bim, d_are, d_aim) = carry(
        _s5_bwd, dm, y_pre, u, states, are_t, aim_t, bre, bim, cre, cim, d_skip, w_glu, b_glu)
    grad("s5_w_glu", d_wglu.astype(BF16).reshape(N_CHIPS, -1, S5_W))
    g_lr, g_li, g_ldt, g_brc, g_bic = _s5_params_bwd(lr, li, ldt, brc, bic, d_are.reshape(1, S5_N),
                                                     d_aim.reshape(1, S5_N), d_bre, d_bim)
    gs["s5_lam_re"], gs["s5_lam_im"] = g_lr, g_li
    gs["s5_log_dt"] = jnp.sum(g_ldt.reshape(S5_G, S5_P), axis=1)
    gs["s5_b_re"], gs["s5_b_im"] = _uncompact_b(g_brc), _uncompact_b(g_bic)
    gs["s5_c_re"] = _uncompact_b(d_crc.T).transpose(0, 2, 1)
    gs["s5_c_im"] = _uncompact_b(d_cic.T).transpose(0, 2, 1)
    du, d_wdw, gs["conv_b_dw"], gs["conv_ln_g"], gs["conv_ln_b"] = carry(_conv_bwd, dm, zc, u, du_s5, w_dw, ln_g, ln_b, mavg)
    d_wdw = jnp.sum(d_wdw.reshape(HALO, 8, CONV_W), axis=1)[:CONV_K]
    grad("conv_w_dw", d_wdw.reshape(CONV_K, N_CHIPS, -1).transpose(1, 0, 2))
    grad("w_in", carry(_mm_tn, h2[None], du, 1.0, "dwin", N_CHIPS, b_cols=True))
    dx1, dx1b, gs["mix_norm"] = carry(_mm_rmsbwd, [du], [w("w_in")], True, True, x1, gm, dx2, "in_proj_bwd")

    da1, db1 = carry(_ffn_bwd_act, dx1b, w("ffn1_w_down"), a1, b1, "ffn1_bwd_act")
    grad("ffn1_w_down", _mm_tn(act1, dx1b[None], 0.5, "ffn1_dwd", N_CHIPS))
    grad("ffn1_w_gate", carry(_mm_tn, da1, h1[None], 1.0, "ffn1_dwg", N_CHIPS))
    grad("ffn1_w_up", carry(_mm_tn, db1, h1[None], 1.0, "ffn1_dwu", N_CHIPS))
    grad_x, _, gs["ffn1_norm"] = carry(_mm_rmsbwd, [da1, db1], [w("ffn1_w_gate"), w("ffn1_w_up")], False, False, x, g1,
                                       dx1, "ffn1_bwd_dx")

    out = {}
    gsmall = {k: gs[k].reshape(wts[k].shape) for k in SMALL}
    zero_row = jnp.zeros((1, LANES), F32)
    g_all = _gather_all(_pack_small(gsmall, loss_part))
    res = _adamw_small(_pack_small(s, zero_row), g_all, _pack_small({k: ms[k] for k in SMALL}, zero_row),
                       _pack_small({k: vs[k] for k in SMALL}, zero_row))
    unpacked = [_unpack_small(r, s) for r in res]
    loss = unpacked[0][1]
    for k in SMALL:
        out[k] = [u_[0][k] for u_ in unpacked]

    order = ("ffn2_w_down", "ffn2_w_gate", "ffn2_w_up", "w_out", "s5_w_glu", "conv_w_dw", "w_in", "ffn1_w_down",
             "ffn1_w_gate", "ffn1_w_up")
    back = {}
    for k in order:
        part = carry(_sum_slots, sc[k].result[0], "sum_" + k)
        back[k] = _SwapBack(part) if part.shape != shards[k].shape else _Swap(part)
        waiting.append(back[k])
    for k in order:
        parts = [back[k].result[0]] if isinstance(back[k], _SwapBack) else [back[k].ins[0], back[k].result[0]]
        res = carry(_adamw_sharded, shards[k], parts, _shard2d(k, ms[k]), _shard2d(k, vs[k]), "adamw_" + k)
        out[k] = [_unshard(k, r, wts[k].shape) for r in res]
    return loss, grad_x[None], out


def kernel(x, ffn1_norm, ffn1_w_gate, ffn1_w_up, ffn1_w_down, mix_norm, w_in, s5_lam_re, s5_lam_im, s5_log_dt, s5_b_re, s5_b_im, s5_c_re, s5_c_im, s5_d, s5_w_glu, s5_b_glu, conv_w_dw, conv_b_dw, conv_ln_g, conv_ln_b, w_out, ffn2_norm, ffn2_w_gate, ffn2_w_up, ffn2_w_down, final_norm, loss_target, m_ffn1_norm, m_ffn1_w_gate, m_ffn1_w_up, m_ffn1_w_down, m_mix_norm, m_w_in, m_s5_lam_re, m_s5_lam_im, m_s5_log_dt, m_s5_b_re, m_s5_b_im, m_s5_c_re, m_s5_c_im, m_s5_d, m_s5_w_glu, m_s5_b_glu, m_conv_w_dw, m_conv_b_dw, m_conv_ln_g, m_conv_ln_b, m_w_out, m_ffn2_norm, m_ffn2_w_gate, m_ffn2_w_up, m_ffn2_w_down, m_final_norm, v_ffn1_norm, v_ffn1_w_gate, v_ffn1_w_up, v_ffn1_w_down, v_mix_norm, v_w_in, v_s5_lam_re, v_s5_lam_im, v_s5_log_dt, v_s5_b_re, v_s5_b_im, v_s5_c_re, v_s5_c_im, v_s5_d, v_s5_w_glu, v_s5_b_glu, v_conv_w_dw, v_conv_b_dw, v_conv_ln_g, v_conv_ln_b, v_w_out, v_ffn2_norm, v_ffn2_w_gate, v_ffn2_w_up, v_ffn2_w_down, v_final_norm):
    given = dict(locals())
    wts = {k: given[k] for k in WEIGHTS}
    ms = {k: given["m_" + k] for k in WEIGHTS}
    vs = {k: given["v_" + k] for k in WEIGHTS}
    loss, grad_x, out = _train_step(x, loss_target, wts, ms, vs)
    return (loss, grad_x, *[out[k][0] for k in WEIGHTS], *[out[k][1] for k in WEIGHTS],
            *[out[k][2] for k in WEIGHTS], *[out[k][3] for k in WEIGHTS])
```

```python
import functools

import jax
import jax.numpy as jnp
import numpy as np
from jax import lax
from jax.experimental import pallas as pl
from jax.experimental.pallas import tpu as pltpu

F32, BF16 = jnp.float32, jnp.bfloat16
MESH = pl.DeviceIdType.MESH

EPS = 1e-6
ADAM_LR, ADAM_B1, ADAM_B2, ADAM_EPS, ADAM_WD, ADAM_STEP = 0.001, 0.9, 0.999, 1e-08, 0.01, 10

N_CHIPS = 4
N_DEV = 8
LANES = 128
BF16_ROWS = 16
S5_W, S5_G, S5_GC, S5_P = 512, 32, 16, 64
S5_N = S5_G * S5_P
S5_TILES = S5_N // LANES
S5_HALF = 8
CONV_W, CONV_K, CONV_HD = 512, 31, 64
HALO = 32
CONV_SB = 32
TM = 512
TMS = 1024
TK = 2048
TS = 256
VMEM_LIMIT = 48 << 20
GELU_C0, GELU_C1 = 0.7978845608028654, 0.044715

FFN_T = ("ffn1_w_gate", "ffn1_w_up", "ffn2_w_gate", "ffn2_w_up")
SHARDED = ("ffn1_w_gate", "ffn1_w_up", "ffn1_w_down", "w_in", "s5_w_glu", "conv_w_dw", "w_out",
           "ffn2_w_gate", "ffn2_w_up", "ffn2_w_down")
SMALL = ("ffn1_norm", "mix_norm", "s5_lam_re", "s5_lam_im", "s5_log_dt", "s5_b_re", "s5_b_im", "s5_c_re",
         "s5_c_im", "s5_d", "s5_b_glu", "conv_b_dw", "conv_ln_g", "conv_ln_b", "ffn2_norm", "final_norm")
WEIGHTS = ("ffn1_norm", "ffn1_w_gate", "ffn1_w_up", "ffn1_w_down", "mix_norm", "w_in", "s5_lam_re", "s5_lam_im",
           "s5_log_dt", "s5_b_re", "s5_b_im", "s5_c_re", "s5_c_im", "s5_d", "s5_w_glu", "s5_b_glu", "conv_w_dw",
           "conv_b_dw", "conv_ln_g", "conv_ln_b", "w_out", "ffn2_norm", "ffn2_w_gate", "ffn2_w_up", "ffn2_w_down",
           "final_norm")


def _dot(a, b):
    return jnp.dot(a, b, preferred_element_type=F32)


def _dot_nt(a, b):
    return lax.dot_general(a, b, (((1,), (1,)), ((), ())), preferred_element_type=F32)


def _dot_tn(a, b):
    return lax.dot_general(a, b, (((0,), (0,)), ((), ())), preferred_element_type=F32)


def _colsum(v):
    return jnp.sum(v, axis=0, keepdims=True)


def _sigmoid(v):
    return 1.0 / (1.0 + jnp.exp(-v))


def _accumulate(ref, first, value):
    @pl.when(first)
    def _():
        ref[...] = value

    @pl.when(jnp.logical_not(first))
    def _():
        ref[...] += value


def _position():
    x, y, c = lax.axis_index("x"), lax.axis_index("y"), lax.axis_index("c")
    return x, y, c, [(1 - x, y), (x, 1 - y), (1 - x, 1 - y)]


def _remote(src, dst, sems, send, recv, device):
    return pltpu.make_async_remote_copy(src_ref=src, dst_ref=dst, send_sem=sems.at[send], recv_sem=sems.at[recv],
                                        device_id=device, device_id_type=MESH)


class _Gather:
    def __init__(self, shard):
        self.ins = [shard]
        self.outs = [jax.ShapeDtypeStruct((N_CHIPS,) + shard.shape, shard.dtype)]
        self.rows = shard.shape[0]
        self.halve = shard.dtype == BF16 and self.rows % (2 * BF16_ROWS) == 0
        self.n_sem = 13 if self.halve else 7
        self.result = None

    def _copies(self, ins, outs, sems, s0, pos):
        x, y, c, chips = pos
        src, dst = ins[0], outs[0]
        me = 2 * x + y
        if self.halve:
            hr = self.rows // 2
            mine, theirs = pl.ds(c * hr, hr), pl.ds((1 - c) * hr, hr)
            part = lambda slot, rows: dst.at[slot, rows]
            my_src = src.at[mine]
        else:
            mine = theirs = None
            part = lambda slot, rows: dst.at[slot]
            my_src = src
        slot = lambda j: 2 * chips[j][0] + chips[j][1]
        local = lambda: pltpu.make_async_copy(src, dst.at[me], sems.at[s0])
        send = lambda j: _remote(my_src, part(me, mine), sems, s0 + 1 + j, s0 + 4 + j, (*chips[j], c))
        land = lambda j: _remote(my_src, part(slot(j), mine), sems, s0 + 1 + j, s0 + 4 + j, (*chips[j], c))
        fwd = lambda j: _remote(part(slot(j), mine), part(slot(j), mine), sems, s0 + 7 + j, s0 + 10 + j, (x, y, 1 - c))
        got = lambda j: _remote(part(slot(j), theirs), part(slot(j), theirs), sems, s0 + 7 + j, s0 + 10 + j,
                                (x, y, 1 - c))
        return local, send, land, fwd, got

    def start(self, ins, outs, sems, s0, pos):
        local, send, _, _, _ = self._copies(ins, outs, sems, s0, pos)
        local().start()
        for j in range(N_CHIPS - 1):
            send(j).start()

    def finish(self, ins, outs, sems, s0, pos):
        local, send, land, fwd, got = self._copies(ins, outs, sems, s0, pos)
        others = range(N_CHIPS - 1)
        for j in others:
            land(j).wait_recv()
            if self.halve:
                fwd(j).start()
        for j in others:
            if self.halve:
                got(j).wait_recv()
        for j in others:
            send(j).wait_send()
            if self.halve:
                fwd(j).wait_send()
        local().wait()


class _Scatter:
    def __init__(self, grad):
        self.ins = [grad]
        self.outs = [jax.ShapeDtypeStruct(grad.shape, grad.dtype)]
        self.n_sem = 7
        self.result = None

    def _copies(self, ins, outs, sems, s0, pos):
        x, y, c, chips = pos
        src, dst = ins[0], outs[0]
        me = 2 * x + y
        slot = lambda j: 2 * chips[j][0] + chips[j][1]
        local = lambda: pltpu.make_async_copy(src.at[me], dst.at[me], sems.at[s0])
        send = lambda j: _remote(src.at[slot(j)], dst.at[me], sems, s0 + 1 + j, s0 + 4 + j, (*chips[j], c))
        land = lambda j: _remote(src.at[me], dst.at[slot(j)], sems, s0 + 1 + j, s0 + 4 + j, (*chips[j], c))
        return local, send, land

    def start(self, ins, outs, sems, s0, pos):
        local, send, _ = self._copies(ins, outs, sems, s0, pos)
        local().start()
        for j in range(N_CHIPS - 1):
            send(j).start()

    def finish(self, ins, outs, sems, s0, pos):
        local, send, land = self._copies(ins, outs, sems, s0, pos)
        for j in range(N_CHIPS - 1):
            land(j).wait_recv()
        for j in range(N_CHIPS - 1):
            send(j).wait_send()
        local().wait()


class _Swap:
    def __init__(self, part):
        self.ins = [part]
        self.outs = [jax.ShapeDtypeStruct(part.shape, part.dtype)]
        self.n_sem = 2
        self.result = None

    def _copy(self, ins, outs, sems, s0, pos):
        x, y, c, _ = pos
        return _remote(ins[0], outs[0], sems, s0, s0 + 1, (x, y, 1 - c))

    def start(self, ins, outs, sems, s0, pos):
        self._copy(ins, outs, sems, s0, pos).start()

    def finish(self, ins, outs, sems, s0, pos):
        self._copy(ins, outs, sems, s0, pos).wait()


class _SwapHalf:
    def __init__(self, grad, key):
        slots, rows, cols = grad.shape
        half = jax.ShapeDtypeStruct((slots, rows // 2, cols), grad.dtype)
        self.ins, self.outs, self.key = [grad], [half, half], key
        self.hr = rows // 2
        self.n_sem = 3
        self.result = None

    def _copies(self, ins, outs, sems, s0, pos):
        x, y, c, _ = pos
        mine, theirs = pl.ds(c * self.hr, self.hr), pl.ds((1 - c) * self.hr, self.hr)
        local = pltpu.make_async_copy(ins[0].at[:, mine], outs[0], sems.at[s0])
        remote = _remote(ins[0].at[:, theirs], outs[1], sems, s0 + 1, s0 + 2, (x, y, 1 - c))
        return local, remote

    def start(self, ins, outs, sems, s0, pos):
        for cp in self._copies(ins, outs, sems, s0, pos):
            cp.start()

    def finish(self, ins, outs, sems, s0, pos):
        for cp in self._copies(ins, outs, sems, s0, pos):
            cp.wait()


class _SwapBack:
    def __init__(self, part):
        hr, cols = part.shape
        self.ins, self.outs = [part], [jax.ShapeDtypeStruct((2 * hr, cols), part.dtype)]
        self.hr = hr
        self.n_sem = 3
        self.result = None

    def _copies(self, ins, outs, sems, s0, pos):
        x, y, c, _ = pos
        mine, theirs = pl.ds(c * self.hr, self.hr), pl.ds((1 - c) * self.hr, self.hr)
        local = lambda: pltpu.make_async_copy(ins[0], outs[0].at[mine], sems.at[s0])
        send = lambda: _remote(ins[0], outs[0].at[mine], sems, s0 + 1, s0 + 2, (x, y, 1 - c))
        land = lambda: _remote(ins[0], outs[0].at[theirs], sems, s0 + 1, s0 + 2, (x, y, 1 - c))
        return local, send, land

    def start(self, ins, outs, sems, s0, pos):
        local, send, _ = self._copies(ins, outs, sems, s0, pos)
        local().start()
        send().start()

    def finish(self, ins, outs, sems, s0, pos):
        local, send, land = self._copies(ins, outs, sems, s0, pos)
        land().wait_recv()
        send().wait_send()
        local().wait()


def _pallas(body, args, *, name, grid, in_specs, out_specs, out_shape, scratch_shapes=(), comm=()):
    comm = list(comm)
    n_in, n_out, n_scr = len(in_specs), len(out_specs), len(scratch_shapes)
    c_in = [a for op in comm for a in op.ins]
    c_out = [s for op in comm for s in op.outs]
    n_sem = sum(op.n_sem for op in comm)

    def full(*refs):
        o0 = n_in + len(c_in)
        s0 = o0 + n_out + len(c_out)
        ins, cin = refs[:n_in], refs[n_in:o0]
        outs, cout = refs[o0:o0 + n_out], refs[o0 + n_out:s0]
        scratch = refs[s0:s0 + n_scr]
        if comm:
            sems = refs[s0 + n_scr]
            ids = [pl.program_id(d) for d in range(len(grid))]
            first = functools.reduce(jnp.logical_and, [i == 0 for i in ids])
            last = functools.reduce(jnp.logical_and, [i == g - 1 for i, g in zip(ids, grid)])
            pos = _position()

            def each(step):
                ci = co = cs = 0
                for op in comm:
                    getattr(op, step)(cin[ci:ci + len(op.ins)], cout[co:co + len(op.outs)], sems, cs, pos)
                    ci, co, cs = ci + len(op.ins), co + len(op.outs), cs + op.n_sem

            @pl.when(first)
            def _():
                each("start")

        body(*ins, *outs, *scratch)
        if comm:
            @pl.when(last)
            def _():
                each("finish")

    hbm = pl.BlockSpec(memory_space=pl.ANY)
    res = pl.pallas_call(
        full, name=name, grid=grid,
        in_specs=list(in_specs) + [hbm] * len(c_in), out_specs=list(out_specs) + [hbm] * len(c_out),
        out_shape=list(out_shape) + c_out,
        scratch_shapes=list(scratch_shapes) + ([pltpu.SemaphoreType.DMA((n_sem,))] if comm else []),
        compiler_params=pltpu.CompilerParams(dimension_semantics=("arbitrary",) * len(grid),
                                             vmem_limit_bytes=VMEM_LIMIT))(*args, *c_in)
    k = n_out
    for op in comm:
        op.result = list(res[k:k + len(op.outs)])
        k += len(op.outs)
    return list(res[:n_out])


def _row_tile(rows, cols, itemsize=4, budget=1 << 20):
    t = rows
    while t % (2 * BF16_ROWS) == 0 and t * cols * itemsize > budget:
        t //= 2
    return t


def _cast_bf16(w, name):
    rows, cols = w.shape
    tr = _row_tile(rows, cols)

    def body(w_ref, o_ref):
        o_ref[...] = w_ref[...].astype(BF16)

    spec = pl.BlockSpec((tr, cols), lambda i: (i, 0))
    return _pallas(body, [w], name=name, grid=(rows // tr,), in_specs=[spec], out_specs=[spec],
                   out_shape=[jax.ShapeDtypeStruct((rows, cols), BF16)])[0]


def _rms_fwd(x, g, name, comm=()):
    L, D = x.shape

    def body(x_ref, g_ref, h_ref):
        xf = x_ref[...]
        r = lax.rsqrt(jnp.mean(xf * xf, axis=-1, keepdims=True) + EPS)
        h_ref[...] = (xf * r * g_ref[...]).astype(BF16)

    row = pl.BlockSpec((TMS, D), lambda i: (i, 0))
    return _pallas(body, [x, g], name=name, grid=(L // TMS,),
                   in_specs=[row, pl.BlockSpec((1, D), lambda i: (0, 0))], out_specs=[row],
                   out_shape=[jax.ShapeDtypeStruct((L, D), BF16)], comm=comm)[0]


def _resident(shape):
    return pl.BlockSpec(shape, lambda *_: (0,) * len(shape), pipeline_mode=pl.Buffered(1))


def _ffn_up(h, wg_t, wu_t, name, comm=()):
    L, D = h.shape
    G, FS, _ = wg_t.shape

    def body(h_ref, wg_ref, wu_ref, a_ref, b_ref, act_ref):
        j = pl.program_id(1)
        hv = h_ref[...]
        a = _dot_nt(hv, wg_ref[j])
        b = _dot_nt(hv, wu_ref[j])
        a_ref[...] = a.astype(BF16)
        b_ref[...] = b.astype(BF16)
        act_ref[...] = (a * _sigmoid(a) * b).astype(BF16)

    ospec = pl.BlockSpec((None, TMS, FS), lambda i, j: (j, i, 0))
    oshape = jax.ShapeDtypeStruct((G, L, FS), BF16)
    return _pallas(body, [h, wg_t, wu_t], name=name, grid=(L // TMS, G),
                   in_specs=[pl.BlockSpec((TMS, D), lambda i, j: (i, 0)), _resident((G, FS, D)), _resident((G, FS, D))],
                   out_specs=[ospec, ospec, ospec], out_shape=[oshape, oshape, oshape], comm=comm)


def _group_sum(a_ref, w_ref, groups, mm=_dot):
    acc = mm(a_ref[0], w_ref[0])
    for j in range(1, groups):
        acc = acc + mm(a_ref[j], w_ref[j])
    return acc


def _ffn_down(act, wd, x, g_next, name, comm=()):
    G, L, FS = act.shape
    D = wd.shape[2]

    def body(act_ref, wd_ref, x_ref, g_ref, xn_ref, hn_ref):
        xn = x_ref[...] + 0.5 * _group_sum(act_ref, wd_ref, G)
        xn_ref[...] = xn
        r = lax.rsqrt(jnp.mean(xn * xn, axis=-1, keepdims=True) + EPS)
        hn_ref[...] = (xn * r * g_ref[...]).astype(BF16)

    row = pl.BlockSpec((TM, D), lambda i: (i, 0))
    return _pallas(body, [act, wd, x, g_next], name=name, grid=(L // TM,),
                   in_specs=[pl.BlockSpec((G, TM, FS), lambda i: (0, i, 0)), _resident((G, FS, D)), row,
                             pl.BlockSpec((1, D), lambda i: (0, 0))],
                   out_specs=[row, row],
                   out_shape=[jax.ShapeDtypeStruct((L, D), F32), jax.ShapeDtypeStruct((L, D), BF16)], comm=comm)


def _ffn_down_loss(act, wd, x, gf, tgt, name):
    G, L, FS = act.shape
    D = wd.shape[2]

    def body(act_ref, wd_ref, x_ref, g_ref, t_ref, dx_ref, dxb_ref, loss_ref, dg_ref):
        i = pl.program_id(0)
        xn = x_ref[...] + 0.5 * _group_sum(act_ref, wd_ref, G)
        r = lax.rsqrt(jnp.mean(xn * xn, axis=-1, keepdims=True) + EPS)
        xh = xn * r
        gv = g_ref[...]
        e = xh * gv - t_ref[...]
        part = 0.5 * jnp.sum(_colsum(e * e), axis=1, keepdims=True) / D
        dy = e / D
        _accumulate(loss_ref, i == 0, jnp.broadcast_to(part, (1, LANES)))
        _accumulate(dg_ref, i == 0, _colsum(dy * xh))
        dxh = dy * gv
        dx = r * (dxh - xh * jnp.mean(dxh * xh, axis=-1, keepdims=True))
        dx_ref[...] = dx
        dxb_ref[...] = dx.astype(BF16)

    row = pl.BlockSpec((TM, D), lambda i: (i, 0))
    return _pallas(body, [act, wd, x, gf, tgt], name=name, grid=(L // TM,),
                   in_specs=[pl.BlockSpec((G, TM, FS), lambda i: (0, i, 0)), _resident((G, FS, D)), row,
                             pl.BlockSpec((1, D), lambda i: (0, 0)), row],
                   out_specs=[row, row, pl.BlockSpec((1, LANES), lambda i: (0, 0)),
                              pl.BlockSpec((1, D), lambda i: (0, 0))],
                   out_shape=[jax.ShapeDtypeStruct((L, D), F32), jax.ShapeDtypeStruct((L, D), BF16),
                              jax.ShapeDtypeStruct((1, LANES), F32), jax.ShapeDtypeStruct((1, D), F32)])


def _ffn_bwd_act(dxb, wd, a, b, name, comm=()):
    L, D = dxb.shape
    G, FS, _ = wd.shape

    def body(dx_ref, wd_ref, a_ref, b_ref, da_ref, db_ref):
        dact = 0.5 * _dot_nt(dx_ref[...], wd_ref[pl.program_id(1)])
        av = a_ref[...].astype(F32)
        bv = b_ref[...].astype(F32)
        sg = _sigmoid(av)
        da_ref[...] = (dact * bv * sg * (1.0 + av * (1.0 - sg))).astype(BF16)
        db_ref[...] = (dact * av * sg).astype(BF16)

    gspec = pl.BlockSpec((None, TMS, FS), lambda i, j: (j, i, 0))
    oshape = jax.ShapeDtypeStruct((G, L, FS), BF16)
    return _pallas(body, [dxb, wd, a, b], name=name, grid=(L // TMS, G),
                   in_specs=[pl.BlockSpec((TMS, D), lambda i, j: (i, 0)), _resident((G, FS, D)), gspec, gspec],
                   out_specs=[gspec, gspec], out_shape=[oshape, oshape], comm=comm)


def _mm_grouped(a, w, name):
    L, K = a.shape
    G, _, N = w.shape

    def body(a_ref, w_ref, o_ref):
        o_ref[...] = _dot(a_ref[...], w_ref[pl.program_id(1)])

    return _pallas(body, [a, w], name=name, grid=(L // TMS, G),
                   in_specs=[pl.BlockSpec((TMS, K), lambda i, g: (i, 0)), _resident((G, K, N))],
                   out_specs=[pl.BlockSpec((TMS, N), lambda i, g: (i, g))],
                   out_shape=[jax.ShapeDtypeStruct((L, G * N), F32)])[0]


def _mm_nt(a, w, name):
    L, K = a.shape
    N = w.shape[0]

    def body(a_ref, w_ref, o_ref):
        o_ref[...] = _dot_nt(a_ref[...], w_ref[...]).astype(BF16)

    return _pallas(body, [a, w], name=name, grid=(L // TMS,),
                   in_specs=[pl.BlockSpec((TMS, K), lambda i: (i, 0)), _resident((N, K))],
                   out_specs=[pl.BlockSpec((TMS, N), lambda i: (i, 0))],
                   out_shape=[jax.ShapeDtypeStruct((L, N), BF16)])[0]


def _mm_tn(a, b, scale, name, groups, b_cols=False, comm=()):
    L, M = a.shape[1], a.shape[2]
    N = b.shape[1] // groups if b_cols else b.shape[2]
    tk = min(L, TK)
    nk = L // tk

    def spec(v, cols):
        if cols:
            return pl.BlockSpec((tk, v.shape[1] // groups), lambda g, k: (k, g))
        if v.shape[0] > 1:
            return pl.BlockSpec((None, tk, v.shape[2]), lambda g, k: (g, k, 0))
        return pl.BlockSpec((None, tk, v.shape[2]), lambda g, k: (0, k, 0))

    def body(a_ref, b_ref, o_ref, acc):
        k = pl.program_id(1)
        p = _dot_tn(a_ref[...], b_ref[...])
        if nk == 1:
            o_ref[...] = (p * scale).astype(BF16)
        else:
            _accumulate(acc, k == 0, p)

            @pl.when(k == nk - 1)
            def _():
                o_ref[...] = (acc[...] * scale).astype(BF16)

    return _pallas(body, [a, b], name=name, grid=(groups, nk),
                   in_specs=[spec(a, False), spec(b, b_cols)],
                   out_specs=[pl.BlockSpec((None, M, N), lambda g, k: (g, 0, 0))],
                   out_shape=[jax.ShapeDtypeStruct((groups, M, N), BF16)],
                   scratch_shapes=[pltpu.VMEM((M, N), F32)], comm=comm)[0]


def _mm_rmsbwd(a_list, w_list, nt, a_cols, x_in, g, dx_out, name, comm=()):
    P = len(a_list)
    G = w_list[0].shape[0]
    L, D = x_in.shape
    mm = _dot_nt if nt else _dot

    def body(*refs):
        a_refs, w_refs = refs[:P], refs[P:2 * P]
        x_ref, g_ref, dxo_ref, dx_ref, dxb_ref, dg_ref = refs[2 * P:]
        i = pl.program_id(0)
        dh = None
        for a_ref, w_ref in zip(a_refs, w_refs):
            for j in range(G):
                if a_cols:
                    kw = a_ref.shape[1] // G
                    term = mm(a_ref[:, j * kw:(j + 1) * kw], w_ref[j])
                else:
                    term = mm(a_ref[j], w_ref[j])
                dh = term if dh is None else dh + term
        xf = x_ref[...]
        r = lax.rsqrt(jnp.mean(xf * xf, axis=-1, keepdims=True) + EPS)
        xh = xf * r
        _accumulate(dg_ref, i == 0, _colsum(dh * xh))
        dxh = dh * g_ref[...]
        dx = dxo_ref[...] + r * (dxh - xh * jnp.mean(dxh * xh, axis=-1, keepdims=True))
        dx_ref[...] = dx
        dxb_ref[...] = dx.astype(BF16)

    row = pl.BlockSpec((TM, D), lambda i: (i, 0))
    vec = pl.BlockSpec((1, D), lambda i: (0, 0))
    if a_cols:
        a_specs = [pl.BlockSpec((TM, a.shape[1]), lambda i: (i, 0)) for a in a_list]
    else:
        a_specs = [pl.BlockSpec((G, TM, a.shape[2]), lambda i: (0, i, 0)) for a in a_list]
    w_specs = [_resident(w.shape) for w in w_list]
    return _pallas(body, [*a_list, *w_list, x_in, g, dx_out], name=name, grid=(L // TM,),
                   in_specs=a_specs + w_specs + [row, vec, row], out_specs=[row, row, vec],
                   out_shape=[jax.ShapeDtypeStruct((L, D), F32), jax.ShapeDtypeStruct((L, D), BF16),
                              jax.ShapeDtypeStruct((1, D), F32)], comm=comm)


def _mix_out(cat, wout, x1, g_next, name):
    L, K = cat.shape
    D = wout.shape[1]

    def body(c_ref, w_ref, x_ref, g_ref, xn_ref, hn_ref):
        xn = x_ref[...] + _dot(c_ref[...], w_ref[...])
        xn_ref[...] = xn
        r = lax.rsqrt(jnp.mean(xn * xn, axis=-1, keepdims=True) + EPS)
        hn_ref[...] = (xn * r * g_ref[...]).astype(BF16)

    row = pl.BlockSpec((TMS, D), lambda i: (i, 0))
    return _pallas(body, [cat, wout, x1, g_next], name=name, grid=(L // TMS,),
                   in_specs=[pl.BlockSpec((TMS, K), lambda i: (i, 0)), pl.BlockSpec((K, D), lambda i: (0, 0)), row,
                             pl.BlockSpec((1, D), lambda i: (0, 0))],
                   out_specs=[row, row],
                   out_shape=[jax.ShapeDtypeStruct((L, D), F32), jax.ShapeDtypeStruct((L, D), BF16)])


def _s5_disc(lr, li, ldt, brc, bic):
    dt = jnp.exp(ldt)
    mag = jnp.exp(lr * dt)
    are = mag * jnp.cos(li * dt)
    aim = mag * jnp.sin(li * dt)
    den = lr * lr + li * li
    nre = are - 1.0
    fre = (nre * lr + aim * li) / den
    fim = (aim * lr - nre * li) / den
    return are, aim, fre * brc - fim * bic, fre * bic + fim * brc


def _s5_params_fwd(lr, li, ldt, brc, bic, crc, cic):
    def body(lr_ref, li_ref, ldt_ref, br_ref, bi_ref, cr_ref, ci_ref, are_ref, aim_ref, bre_ref, bim_ref, cre_ref, cim_ref):
        are, aim, bre, bim = _s5_disc(lr_ref[...], li_ref[...], ldt_ref[...], br_ref[...], bi_ref[...])
        are_ref[...] = are
        aim_ref[...] = aim
        bre_ref[...] = bre.astype(BF16)
        bim_ref[...] = bim.astype(BF16)
        cre_ref[...] = cr_ref[...].astype(BF16)
        cim_ref[...] = ci_ref[...].astype(BF16)

    vec = jax.ShapeDtypeStruct((1, S5_N), F32)
    return pl.pallas_call(
        body, name="s5_params_fwd",
        out_shape=[vec, vec, jax.ShapeDtypeStruct((LANES, S5_N), BF16), jax.ShapeDtypeStruct((LANES, S5_N), BF16),
                   jax.ShapeDtypeStruct((S5_N, LANES), BF16), jax.ShapeDtypeStruct((S5_N, LANES), BF16)],
        compiler_params=pltpu.CompilerParams(vmem_limit_bytes=VMEM_LIMIT))(lr, li, ldt, brc, bic, crc, cic)


def _s5_params_bwd(lr, li, ldt, brc, bic, dare, daim, dbre, dbim):
    def body(lr_ref, li_ref, ldt_ref, br_ref, bi_ref, dare_ref, daim_ref, dbre_ref, dbim_ref,
             glr_ref, gli_ref, gldt_ref, gbr_ref, gbi_ref):
        _, vjp = jax.vjp(_s5_disc, lr_ref[...], li_ref[...], ldt_ref[...], br_ref[...], bi_ref[...])
        glr, gli, gldt, gbr, gbi = vjp((dare_ref[...], daim_ref[...], dbre_ref[...], dbim_ref[...]))
        glr_ref[...] = glr
        gli_ref[...] = gli
        gldt_ref[...] = gldt
        gbr_ref[...] = gbr
        gbi_ref[...] = gbi

    vec = jax.ShapeDtypeStruct((1, S5_N), F32)
    mat = jax.ShapeDtypeStruct((LANES, S5_N), F32)
    return pl.pallas_call(
        body, name="s5_params_bwd", out_shape=[vec, vec, vec, mat, mat],
        compiler_params=pltpu.CompilerParams(vmem_limit_bytes=VMEM_LIMIT))(lr, li, ldt, brc, bic, dare, daim, dbre, dbim)


def _gelu_parts(y):
    th = jnp.tanh(GELU_C0 * (y + GELU_C1 * y * y * y))
    return 0.5 * y * (1.0 + th), th


def _state_rows(q, T):
    return pl.ds(q % S5_HALF, T, stride=S5_HALF)


def _load_tiles(bufs, ct, T, dtype):
    return jnp.concatenate([bufs[q // S5_HALF][_state_rows(q, T), :].astype(dtype) for q in range(4 * ct, 4 * ct + 4)],
                           axis=1)


def _store_tiles(bufs, ct, T, value):
    for k, q in enumerate(range(4 * ct, 4 * ct + 4)):
        bufs[q // S5_HALF][_state_rows(q, T), :] = value[:, k * LANES:(k + 1) * LANES]


def _s5_fwd(u, are, aim, bre, bim, cre, cim, d_skip, wglu, bglu, comm=()):
    L = u.shape[0]
    T = min(TS, L)
    n = L // T

    def body(u_ref, are_ref, aim_ref, bre_ref, bim_ref, cre_ref, cim_ref, d_ref, wg_ref, bg_ref,
             sre_lo, sre_hi, sim_lo, sim_hi, y_ref, o_ref, st_re, st_im):
        i = pl.program_id(0)
        sre, sim = (sre_lo, sre_hi), (sim_lo, sim_hi)

        @pl.when(i == 0)
        def _():
            st_re[...] = jnp.zeros_like(st_re)
            st_im[...] = jnp.zeros_like(st_im)

        uf = u_ref[...]
        ub = uf.astype(BF16)
        for ct in range(4):
            uq = ub[:, ct * LANES:(ct + 1) * LANES]
            win = slice(4 * ct * LANES, 4 * (ct + 1) * LANES)
            _store_tiles(sre, ct, T, _dot(uq, bre_ref[:, win]))
            _store_tiles(sim, ct, T, _dot(uq, bim_ref[:, win]))
        halves = [slice(h * S5_HALF, (h + 1) * S5_HALF) for h in range(2)]
        a_re = [are_ref[hs, :] for hs in halves]
        a_im = [aim_ref[hs, :] for hs in halves]

        def step(t, carry):
            rows = pl.ds(pl.multiple_of(t * S5_HALF, S5_HALF), S5_HALF)
            out = []
            for h in range(2):
                s_re, s_im = carry[2 * h], carry[2 * h + 1]
                n_re = a_re[h] * s_re - a_im[h] * s_im + sre[h][rows, :]
                n_im = a_re[h] * s_im + a_im[h] * s_re + sim[h][rows, :]
                sre[h][rows, :] = n_re
                sim[h][rows, :] = n_im
                out += [n_re, n_im]
            return tuple(out)

        init = (st_re[halves[0], :], st_im[halves[0], :], st_re[halves[1], :], st_im[halves[1], :])
        fin = lax.fori_loop(0, T, step, init, unroll=4)
        for h in range(2):
            st_re[halves[h], :] = fin[2 * h]
            st_im[halves[h], :] = fin[2 * h + 1]
        tiles = []
        for ct in range(4):
            win = slice(4 * ct * LANES, 4 * (ct + 1) * LANES)
            tiles.append(_dot(_load_tiles(sre, ct, T, BF16), cre_ref[win, :])
                         - _dot(_load_tiles(sim, ct, T, BF16), cim_ref[win, :]))
        y = jnp.concatenate(tiles, axis=1) + d_ref[...] * uf
        y_ref[...] = y
        yg, _ = _gelu_parts(y)
        gate = _sigmoid(_dot(yg.astype(BF16), wg_ref[...]) + bg_ref[...])
        o_ref[...] = (yg * gate).astype(BF16)

    const = lambda shape: pl.BlockSpec(shape, lambda i: (0, 0))
    sspec = pl.BlockSpec((T * S5_HALF, LANES), lambda i: (i, 0))
    sshape = jax.ShapeDtypeStruct((L * S5_HALF, LANES), F32)
    chunk = pl.BlockSpec((T, S5_W), lambda i: (i, 0))
    return _pallas(body, [u, are, aim, bre, bim, cre, cim, d_skip, wglu, bglu], name="s5_fwd", grid=(n,),
                   in_specs=[chunk, const((S5_TILES, LANES)), const((S5_TILES, LANES)),
                             const((LANES, S5_N)), const((LANES, S5_N)), const((S5_N, LANES)), const((S5_N, LANES)),
                             const((1, S5_W)), const((S5_W, S5_W)), const((1, S5_W))],
                   out_specs=[sspec] * 4 + [chunk, chunk],
                   out_shape=[sshape] * 4 + [jax.ShapeDtypeStruct((L, S5_W), F32), jax.ShapeDtypeStruct((L, S5_W), BF16)],
                   scratch_shapes=[pltpu.VMEM((S5_TILES, LANES), F32), pltpu.VMEM((S5_TILES, LANES), F32)], comm=comm)


def _s5_bwd(dm, y_pre, u, states, are, aim, bre, bim, cre, cim, d_skip, wglu, bglu, comm=()):
    L = u.shape[0]
    T = min(TS, L)
    n = L // T

    def body(dm_ref, y_ref, u_ref, sre_lo, sre_hi, sim_lo, sim_hi, pre_lo, pre_hi, pim_lo, pim_hi,
             are_ref, aim_ref, bre_ref, bim_ref, cre_ref, cim_ref, d_ref, wg_ref, bg_ref,
             du_ref, dwg_ref, dbg_ref, dd_ref, dcre_ref, dcim_ref, dbre_ref, dbim_ref, dare_ref, daim_ref,
             gre_lo, gre_hi, gim_lo, gim_hi, car_re, car_im):
        i = pl.program_id(0)
        first = i == 0
        sre, sim = (sre_lo, sre_hi), (sim_lo, sim_hi)
        gre, gim = (gre_lo, gre_hi), (gim_lo, gim_hi)
        pre, pim = (pre_lo, pre_hi), (pim_lo, pim_hi)

        @pl.when(first)
        def _():
            car_re[...] = jnp.zeros_like(car_re)
            car_im[...] = jnp.zeros_like(car_im)
            dcre_ref[...] = jnp.zeros_like(dcre_ref)
            dcim_ref[...] = jnp.zeros_like(dcim_ref)
            dbre_ref[...] = jnp.zeros_like(dbre_ref)
            dbim_ref[...] = jnp.zeros_like(dbim_ref)

        y = y_ref[...]
        uf = u_ref[...]
        yg, th = _gelu_parts(y)
        dgelu = 0.5 * (1.0 + th) + 0.5 * y * (1.0 - th * th) * GELU_C0 * (1.0 + 3.0 * GELU_C1 * y * y)
        ygb = yg.astype(BF16)
        sg = _sigmoid(_dot(ygb, wg_ref[...]) + bg_ref[...])
        dout = dm_ref[...].astype(F32)
        dgp = dout * yg * sg * (1.0 - sg)
        dgpb = dgp.astype(BF16)
        dyg = dout * sg + _dot_nt(dgpb, wg_ref[...])
        _accumulate(dwg_ref, first, _dot_tn(ygb, dgpb))
        _accumulate(dbg_ref, first, _colsum(dgp))
        dy = dyg * dgelu
        _accumulate(dd_ref, first, _colsum(dy * uf))
        dyb = dy.astype(BF16)
        ub = uf.astype(BF16)

        for ct in range(4):
            win = slice(4 * ct * LANES, 4 * (ct + 1) * LANES)
            dyq = dyb[:, ct * LANES:(ct + 1) * LANES]
            dcre_ref[win, :] += _dot_tn(_load_tiles(sre, ct, T, BF16), dyq)
            dcim_ref[win, :] -= _dot_tn(_load_tiles(sim, ct, T, BF16), dyq)
            _store_tiles(gre, ct, T, _dot_nt(dyq, cre_ref[win, :]))
            _store_tiles(gim, ct, T, -_dot_nt(dyq, cim_ref[win, :]))

        halves = [slice(h * S5_HALF, (h + 1) * S5_HALF) for h in range(2)]
        a_re = [are_ref[hs, :] for hs in halves]
        a_im = [aim_ref[hs, :] for hs in halves]

        def adjoint(t, h, g_re, g_im):
            rows = pl.ds(pl.multiple_of(t * S5_HALF, S5_HALF), S5_HALF)
            n_re = gre[h][rows, :] + a_re[h] * g_re + a_im[h] * g_im
            n_im = gim[h][rows, :] + a_re[h] * g_im - a_im[h] * g_re
            gre[h][rows, :] = n_re
            gim[h][rows, :] = n_im
            return n_re, n_im

        def step(k, carry):
            out = []
            for h in range(2):
                out += adjoint(T - 1 - k, h, carry[2 * h], carry[2 * h + 1])
            return tuple(out)

        init = (car_re[halves[0], :], car_im[halves[0], :], car_re[halves[1], :], car_im[halves[1], :])
        fin = lax.fori_loop(0, T, step, init, unroll=4)
        keep = (i < n - 1).astype(F32)
        later, earlier = slice(S5_HALF, T * S5_HALF), slice(0, (T - 1) * S5_HALF)
        fold = lambda v: jnp.sum(v.reshape(T - 1, S5_HALF, LANES), axis=0)
        for h in range(2):
            g_re, g_im = fin[2 * h], fin[2 * h + 1]
            car_re[halves[h], :] = g_re
            car_im[halves[h], :] = g_im
            p_re, p_im = pre[h][...] * keep, pim[h][...] * keep
            da_re = fold(gre[h][later, :] * sre[h][earlier, :] + gim[h][later, :] * sim[h][earlier, :])
            da_im = fold(gim[h][later, :] * sre[h][earlier, :] - gre[h][later, :] * sim[h][earlier, :])
            da_re = da_re + g_re * p_re + g_im * p_im
            da_im = da_im + g_im * p_re - g_re * p_im

            @pl.when(first)
            def _():
                dare_ref[halves[h], :] = da_re
                daim_ref[halves[h], :] = da_im

            @pl.when(jnp.logical_not(first))
            def _():
                dare_ref[halves[h], :] += da_re
                daim_ref[halves[h], :] += da_im

        tiles = []
        for ct in range(4):
            uq = ub[:, ct * LANES:(ct + 1) * LANES]
            win = slice(4 * ct * LANES, 4 * (ct + 1) * LANES)
            g_re, g_im = _load_tiles(gre, ct, T, BF16), _load_tiles(gim, ct, T, BF16)
            tiles.append(d_ref[:, ct * LANES:(ct + 1) * LANES] * dy[:, ct * LANES:(ct + 1) * LANES]
                         + _dot_nt(g_re, bre_ref[:, win]) + _dot_nt(g_im, bim_ref[:, win]))
            dbre_ref[:, win] += _dot_tn(uq, g_re)
            dbim_ref[:, win] += _dot_tn(uq, g_im)
        du_ref[...] = jnp.concatenate(tiles, axis=1).astype(BF16)

    rev = lambda i: (n - 1 - i, 0)
    const = lambda shape: pl.BlockSpec(shape, lambda i: (0, 0))
    chunk = pl.BlockSpec((T, S5_W), rev)
    sspec = pl.BlockSpec((T * S5_HALF, LANES), rev)
    pspec = pl.BlockSpec((S5_HALF, LANES), lambda i: (jnp.maximum((n - 1 - i) * T - 1, 0), 0))
    tile = jax.ShapeDtypeStruct((S5_TILES, LANES), F32)
    vec = jax.ShapeDtypeStruct((1, S5_W), F32)
    sbuf = pltpu.VMEM((T * S5_HALF, LANES), F32)
    return _pallas(
        body, [dm, y_pre, u, *states, *states, are, aim, bre, bim, cre, cim, d_skip, wglu, bglu],
        name="s5_bwd", grid=(n,),
        in_specs=[chunk, chunk, chunk] + [sspec] * 4 + [pspec] * 4 + [
            const((S5_TILES, LANES)), const((S5_TILES, LANES)), const((LANES, S5_N)), const((LANES, S5_N)),
            const((S5_N, LANES)), const((S5_N, LANES)), const((1, S5_W)), const((S5_W, S5_W)), const((1, S5_W))],
        out_specs=[chunk, const((S5_W, S5_W)), const((1, S5_W)), const((1, S5_W)), const((S5_N, LANES)),
                   const((S5_N, LANES)), const((LANES, S5_N)), const((LANES, S5_N)), const((S5_TILES, LANES)),
                   const((S5_TILES, LANES))],
        out_shape=[jax.ShapeDtypeStruct((L, S5_W), BF16), jax.ShapeDtypeStruct((S5_W, S5_W), F32), vec, vec,
                   jax.ShapeDtypeStruct((S5_N, LANES), F32), jax.ShapeDtypeStruct((S5_N, LANES), F32),
                   jax.ShapeDtypeStruct((LANES, S5_N), F32), jax.ShapeDtypeStruct((LANES, S5_N), F32), tile, tile],
        scratch_shapes=[sbuf, sbuf, sbuf, sbuf, pltpu.VMEM((S5_TILES, LANES), F32), pltpu.VMEM((S5_TILES, LANES), F32)],
        comm=comm)


_EYE8 = np.eye(8, dtype=np.float32)


def _compact_b(b):
    return jnp.einsum("akpc,kj->jcakp", b.reshape(4, 8, S5_P, S5_GC), _EYE8).reshape(LANES, S5_N)


def _uncompact_b(m):
    return jnp.einsum("kcakp->akpc", m.reshape(8, S5_GC, 4, 8, S5_P)).reshape(S5_G, S5_P, S5_GC)


_HEAD_MEAN = np.kron(np.eye(CONV_W // CONV_HD, dtype=np.float32), np.full((CONV_HD, CONV_HD), 1.0 / CONV_HD, np.float32))


def _head_mean(v, m):
    hi = v.astype(BF16)
    lo = (v - hi.astype(F32)).astype(BF16)
    return _dot(hi, m) + _dot(lo, m)


def _head_norm(zc, m):
    d = zc - _head_mean(zc, m)
    rstd = lax.rsqrt(_head_mean(d * d, m) + EPS)
    return d * rstd, rstd


def _taps_by_phase(first):
    groups = {}
    for k in range(CONV_K):
        m, s = divmod(first + k, 8)
        groups.setdefault(s, []).append((m, k))
    return groups


def _causal_taps(buf, w_ref, r0, first, flip):
    acc = None
    for s, taps in sorted(_taps_by_phase(first).items()):
        rows = CONV_SB + (8 if s else 0)
        y = None
        for m, k in taps:
            kk = CONV_K - 1 - k if flip else k
            term = w_ref[kk:kk + 1, :] * buf[pl.ds(r0 + 8 * m, rows), :]
            y = term if y is None else y + term
        y = y[s:s + CONV_SB, :]
        acc = y if acc is None else acc + y
    return acc


def _conv_fwd(u, o_s5, wdw, bdw, lng, lnb, mavg, comm=()):
    L = u.shape[0]
    T = min(TS, L)
    n = L // T
    first_tap = HALO - (CONV_K - 1)

    def body(v1_ref, v2_ref, s5_ref, w_ref, b_ref, g_ref, be_ref, m_ref, zc_ref, o_ref, zbuf):
        i = pl.program_id(0)

        @pl.when(i == 0)
        def _():
            zbuf[0:HALO, :] = jnp.zeros((HALO, CONV_W), F32)

        zbuf[HALO:HALO + T, :] = v1_ref[...] * _sigmoid(v2_ref[...])
        for r0 in range(0, T, CONV_SB):
            zc_ref[r0:r0 + CONV_SB, :] = b_ref[...] + _causal_taps(zbuf, w_ref, r0, first_tap, False)
        zbuf[0:HALO, :] = zbuf[T:T + HALO, :]
        zn, _ = _head_norm(zc_ref[...], m_ref[...])
        zz = zn * g_ref[...] + be_ref[...]
        o_ref[:, 0:S5_W] = s5_ref[...]
        o_ref[:, S5_W:S5_W + CONV_W] = (zz * _sigmoid(zz)).astype(BF16)

    const = lambda shape: pl.BlockSpec(shape, lambda i: (0, 0))
    vec = const((1, CONV_W))
    return _pallas(body, [u, u, o_s5, wdw, bdw, lng, lnb, mavg], name="conv_fwd", grid=(n,),
                   in_specs=[pl.BlockSpec((T, CONV_W), lambda i: (i, 1)), pl.BlockSpec((T, CONV_W), lambda i: (i, 2)),
                             pl.BlockSpec((T, S5_W), lambda i: (i, 0)), const((CONV_K, CONV_W)), vec, vec, vec,
                             const((CONV_W, CONV_W))],
                   out_specs=[pl.BlockSpec((T, CONV_W), lambda i: (i, 0)),
                              pl.BlockSpec((T, S5_W + CONV_W), lambda i: (i, 0))],
                   out_shape=[jax.ShapeDtypeStruct((L, CONV_W), F32), jax.ShapeDtypeStruct((L, S5_W + CONV_W), BF16)],
                   scratch_shapes=[pltpu.VMEM((T + HALO, CONV_W), F32)], comm=comm)


def _conv_bwd(dm, zc, u, du_s5, wdw, lng, lnb, mavg, comm=()):
    L = u.shape[0]
    T = min(TS, L)
    n = L // T
    hb = T // HALO
    first_tap = HALO - (CONV_K - 1)

    def body(dm_ref, zc_ref, v1_ref, v2_ref, p1_ref, p2_ref, s5_ref, w_ref, g_ref, be_ref, m_ref,
             du_ref, dw_ref, db_ref, dg_ref, dbe_ref, zbuf, dzbuf, head):
        i = pl.program_id(0)
        first = i == 0

        @pl.when(first)
        def _():
            head[...] = jnp.zeros_like(head)
            dw_ref[...] = jnp.zeros_like(dw_ref)

        zn, rstd = _head_norm(zc_ref[...], m_ref[...])
        zz = zn * g_ref[...] + be_ref[...]
        sg = _sigmoid(zz)
        dzz = dm_ref[...].astype(F32) * sg * (1.0 + zz * (1.0 - sg))
        _accumulate(dbe_ref, first, _colsum(dzz))
        _accumulate(dg_ref, first, _colsum(dzz * zn))
        dzn = dzz * g_ref[...]
        dzc = rstd * (dzn - _head_mean(dzn, m_ref[...]) - zn * _head_mean(dzn * zn, m_ref[...]))
        _accumulate(db_ref, first, _colsum(dzc))

        dzbuf[0:T, :] = dzc
        dzbuf[T:T + HALO, :] = head[...]
        head[...] = dzbuf[0:HALO, :]
        keep = (i < n - 1).astype(F32)
        zbuf[0:HALO, :] = p1_ref[...] * _sigmoid(p2_ref[...]) * keep
        zbuf[HALO:HALO + T, :] = v1_ref[...] * _sigmoid(v2_ref[...])
        du_ref[:, 0:S5_W] = s5_ref[...]

        for r0 in range(0, T, CONV_SB):
            rows = slice(r0, r0 + CONV_SB)
            dzc_b = dzbuf[rows, :]
            for s, taps in sorted(_taps_by_phase(first_tap).items()):
                pad = ([jnp.zeros((s, CONV_W), F32)] if s else []) + [dzc_b] + ([jnp.zeros((8 - s, CONV_W), F32)] if s else [])
                shifted = jnp.concatenate(pad, axis=0) if s else dzc_b
                n_rows = shifted.shape[0]
                for m, k in taps:
                    prod = shifted * zbuf[pl.ds(r0 + 8 * m, n_rows), :]
                    dw_ref[8 * k:8 * k + 8, :] += jnp.sum(prod.reshape(n_rows // 8, 8, CONV_W), axis=0)
            dz = _causal_taps(dzbuf, w_ref, r0, 0, True)
            v1 = v1_ref[rows, :]
            sg2 = _sigmoid(v2_ref[rows, :])
            du_ref[rows, S5_W:S5_W + CONV_W] = (dz * sg2).astype(BF16)
            du_ref[rows, S5_W + CONV_W:S5_W + 2 * CONV_W] = (dz * v1 * sg2 * (1.0 - sg2)).astype(BF16)

    rev = lambda c: (lambda i: (n - 1 - i, c))
    prev = lambda c: (lambda i: (jnp.maximum((n - 1 - i) * hb - 1, 0), c))
    const = lambda shape: pl.BlockSpec(shape, lambda i: (0, 0))
    vec = const((1, CONV_W))
    vshape = jax.ShapeDtypeStruct((1, CONV_W), F32)
    return _pallas(
        body, [dm, zc, u, u, u, u, du_s5, wdw, lng, lnb, mavg], name="conv_bwd", grid=(n,),
        in_specs=[pl.BlockSpec((T, CONV_W), rev(1)), pl.BlockSpec((T, CONV_W), rev(0)),
                  pl.BlockSpec((T, CONV_W), rev(1)), pl.BlockSpec((T, CONV_W), rev(2)),
                  pl.BlockSpec((HALO, CONV_W), prev(1)), pl.BlockSpec((HALO, CONV_W), prev(2)),
                  pl.BlockSpec((T, S5_W), rev(0)), const((CONV_K, CONV_W)), vec, vec, const((CONV_W, CONV_W))],
        out_specs=[pl.BlockSpec((T, S5_W + 2 * CONV_W), rev(0)), const((8 * HALO, CONV_W)), vec, vec, vec],
        out_shape=[jax.ShapeDtypeStruct((L, S5_W + 2 * CONV_W), BF16), jax.ShapeDtypeStruct((8 * HALO, CONV_W), F32),
                   vshape, vshape, vshape],
        scratch_shapes=[pltpu.VMEM((T + HALO, CONV_W), F32), pltpu.VMEM((T + HALO, CONV_W), F32),
                        pltpu.VMEM((HALO, CONV_W), F32)], comm=comm)


def _gather_all(v, comm=()):
    rows, cols = v.shape

    def body(x_ref, out_ref, send_sems, recv_sems, local_sem):
        x, y, c, chips = _position()
        me, sibling = (x, y, c), (x, y, 1 - c)

        def block(px, py, pc):
            return out_ref.at[pl.ds((4 * px + 2 * py + pc) * rows, rows), :]

        def copy(k, blk, to, src=None):
            return pltpu.make_async_remote_copy(
                src_ref=block(*blk) if src is None else src, dst_ref=block(*blk), send_sem=send_sems.at[k],
                recv_sem=recv_sems.at[k], device_id=to, device_id_type=MESH)

        mine = pltpu.make_async_copy(x_ref, block(*me), local_sem)
        mine.start()
        first = [copy(0, me, sibling, src=x_ref)]
        first += [copy(1 + j, me, (*chip, c), src=x_ref) for j, chip in enumerate(chips)]
        for cp in first:
            cp.start()
        passed = [copy(4 + j, (*chip, c), sibling) for j, chip in enumerate(chips)]
        for j, chip in enumerate(chips):
            copy(1 + j, (*chip, c), me).wait_recv()
            passed[j].start()
        copy(0, sibling, me).wait_recv()
        for j, chip in enumerate(chips):
            copy(4 + j, (*chip, 1 - c), me).wait_recv()
        for cp in first + passed:
            cp.wait_send()
        mine.wait()

    whole = pl.BlockSpec(memory_space=pltpu.VMEM)
    return _pallas(body, [v], name="gather_small", grid=(1,), in_specs=[whole], out_specs=[whole],
                   out_shape=[jax.ShapeDtypeStruct((N_DEV * rows, cols), v.dtype)],
                   scratch_shapes=[pltpu.SemaphoreType.DMA((7,)), pltpu.SemaphoreType.DMA((7,)), pltpu.SemaphoreType.DMA],
                   comm=comm)[0]


def _adamw(w, g, m, v):
    m = ADAM_B1 * m + (1.0 - ADAM_B1) * g
    v = ADAM_B2 * v + (1.0 - ADAM_B2) * jnp.square(g)
    m_hat = m / (1.0 - ADAM_B1 ** ADAM_STEP)
    v_hat = v / (1.0 - ADAM_B2 ** ADAM_STEP)
    return -ADAM_LR * (m_hat / (jnp.sqrt(v_hat) + ADAM_EPS) + ADAM_WD * w), m, v


def _sum_slots(recv, name, comm=()):
    _, rows, cols = recv.shape
    tr = _row_tile(rows, cols)

    def body(r_ref, o_ref):
        acc = r_ref[0].astype(F32)
        for s in range(1, N_CHIPS):
            acc = acc + r_ref[s].astype(F32)
        o_ref[...] = acc

    return _pallas(body, [recv], name=name, grid=(rows // tr,),
                   in_specs=[pl.BlockSpec((N_CHIPS, tr, cols), lambda i: (0, i, 0))],
                   out_specs=[pl.BlockSpec((tr, cols), lambda i: (i, 0))],
                   out_shape=[jax.ShapeDtypeStruct((rows, cols), F32)], comm=comm)[0]


def _add_halves(mine, theirs, name):
    slots, rows, cols = mine.shape
    tr = _row_tile(rows, cols * slots)

    def body(a_ref, b_ref, o_ref):
        o_ref[...] = (a_ref[...].astype(F32) + b_ref[...].astype(F32)).astype(o_ref.dtype)

    spec = pl.BlockSpec((slots, tr, cols), lambda i: (0, i, 0))
    return _pallas(body, [mine, theirs], name=name, grid=(rows // tr,), in_specs=[spec, spec], out_specs=[spec],
                   out_shape=[jax.ShapeDtypeStruct(mine.shape, mine.dtype)])[0]


def _adamw_sharded(w, parts, m, v, name, comm=()):
    rows, cols = w.shape
    tr = _row_tile(rows, cols)
    n = len(parts)

    def body(w_ref, *refs):
        p_refs, (m_ref, v_ref, g_ref, d_ref, nm_ref, nv_ref) = refs[:n], refs[n:]
        g = p_refs[0][...]
        for p_ref in p_refs[1:]:
            g = g + p_ref[...]
        g_ref[...] = g
        d_ref[...], nm_ref[...], nv_ref[...] = _adamw(w_ref[...], g, m_ref[...], v_ref[...])

    spec = pl.BlockSpec((tr, cols), lambda i: (i, 0))
    shape = jax.ShapeDtypeStruct((rows, cols), F32)
    return _pallas(body, [w, *parts, m, v], name=name, grid=(rows // tr,), in_specs=[spec] * (n + 3),
                   out_specs=[spec] * 4, out_shape=[shape] * 4, comm=comm)


def _adamw_small(w, gathered, m, v):
    rows, cols = w.shape

    def body(w_ref, a_ref, m_ref, v_ref, g_ref, d_ref, nm_ref, nv_ref):
        g = a_ref[0:rows, :]
        for dev in range(1, N_DEV):
            g = g + a_ref[dev * rows:(dev + 1) * rows, :]
        g_ref[...] = g
        d_ref[...], nm_ref[...], nv_ref[...] = _adamw(w_ref[...], g, m_ref[...], v_ref[...])

    shape = jax.ShapeDtypeStruct((rows, cols), F32)
    return pl.pallas_call(
        body, name="adamw_small", out_shape=[shape] * 4,
        compiler_params=pltpu.CompilerParams(vmem_limit_bytes=VMEM_LIMIT))(w, gathered, m, v)


PACK_TILE = 8 * LANES


def _pack_small(vals, last_row):
    rows = []
    for name in SMALL:
        flat = vals[name].reshape(-1).astype(F32)
        rows.append(jnp.pad(flat, (0, -flat.size % PACK_TILE)).reshape(-1, LANES))
    rows.append(jnp.pad(last_row, ((0, 7), (0, 0))))
    return jnp.concatenate(rows, axis=0)


def _unpack_small(packed, like):
    out, r = {}, 0
    for name in SMALL:
        size = like[name].size
        out[name] = packed[r:r + -(-size // LANES)].reshape(-1)[:size].reshape(like[name].shape)
        r += 8 * -(-size // PACK_TILE)
    return out, packed[r, 0]


def _shard2d(name, v):
    v = v.reshape(v.shape[-2:])
    return v.T if name in FFN_T else v


def _unshard(name, v, shape):
    return (v.T if name in FFN_T else v).reshape(shape)


def _train_step(x3, tgt3, wts, ms, vs):
    x, tgt = x3[0], tgt3[0]
    L, D = x.shape
    row = lambda v: v.reshape(1, -1)
    shards = {k: _shard2d(k, wts[k]) for k in SHARDED}
    sends = {k: shards[k] if k == "conv_w_dw" else _cast_bf16(shards[k], "cast_" + k) for k in SHARDED}
    gat = {k: _Gather(sends[k]) for k in SHARDED}
    w = lambda k: gat[k].result[0]

    s = {k: wts[k] for k in SMALL}
    lr, li = s["s5_lam_re"].reshape(1, S5_N), s["s5_lam_im"].reshape(1, S5_N)
    ldt = jnp.repeat(s["s5_log_dt"].reshape(S5_G), S5_P).reshape(1, S5_N)
    brc, bic = _compact_b(s["s5_b_re"].reshape(S5_G, S5_P, S5_GC)), _compact_b(s["s5_b_im"].reshape(S5_G, S5_P, S5_GC))
    crc = _compact_b(s["s5_c_re"].reshape(S5_G, S5_GC, S5_P).transpose(0, 2, 1)).T
    cic = _compact_b(s["s5_c_im"].reshape(S5_G, S5_GC, S5_P).transpose(0, 2, 1)).T
    d_skip, b_glu = row(s["s5_d"]), row(s["s5_b_glu"])
    b_dw, ln_g, ln_b = row(s["conv_b_dw"]), row(s["conv_ln_g"]), row(s["conv_ln_b"])
    g1, gm, g2, gf = row(s["ffn1_norm"]), row(s["mix_norm"]), row(s["ffn2_norm"]), row(s["final_norm"])
    mavg = jnp.asarray(_HEAD_MEAN, dtype=BF16)

    h1 = _rms_fwd(x, g1, "rms1", comm=[gat["ffn1_w_gate"], gat["ffn1_w_up"]])
    a1, b1, act1 = _ffn_up(h1, w("ffn1_w_gate"), w("ffn1_w_up"), "ffn1_up", comm=[gat["ffn1_w_down"]])
    x1, h2 = _ffn_down(act1, w("ffn1_w_down"), x, gm, "ffn1_down",
                       comm=[gat["w_in"], gat["s5_w_glu"], gat["conv_w_dw"], gat["w_out"]])
    u = _mm_grouped(h2, w("w_in"), "in_proj")
    are, aim, bre, bim, cre, cim = _s5_params_fwd(lr, li, ldt, brc, bic, crc, cic)
    are_t, aim_t = are.reshape(S5_TILES, LANES), aim.reshape(S5_TILES, LANES)
    w_glu = w("s5_w_glu").reshape(S5_W, S5_W)
    *states, y_pre, o_s5 = _s5_fwd(u, are_t, aim_t, bre, bim, cre, cim, d_skip, w_glu, b_glu,
                                   comm=[gat["ffn2_w_gate"], gat["ffn2_w_up"]])
    w_dw = w("conv_w_dw").transpose(1, 0, 2).reshape(CONV_K, CONV_W)
    zc, cat = _conv_fwd(u, o_s5, w_dw, b_dw, ln_g, ln_b, mavg, comm=[gat["ffn2_w_down"]])
    w_out = w("w_out").reshape(-1, D)
    x2, h3 = _mix_out(cat, w_out, x1, g2, "mix_out")
    a2, b2, act2 = _ffn_up(h3, w("ffn2_w_gate"), w("ffn2_w_up"), "ffn2_up")
    dx3, dx3b, loss_part, d_gf = _ffn_down_loss(act2, w("ffn2_w_down"), x2, gf, tgt, "ffn2_down_loss")

    gs, sc, waiting = {"final_norm": d_gf}, {}, []

    def grad(key, g):
        if g.shape[1] % (2 * BF16_ROWS) == 0 and g.dtype == BF16:
            waiting.append(_SwapHalf(g, key))
        else:
            sc[key] = _Scatter(g)
            waiting.append(sc[key])

    def carry(call, *args, **kw):
        ops = list(waiting)
        waiting.clear()
        res = call(*args, comm=ops, **kw)
        for op in ops:
            if isinstance(op, _SwapHalf):
                sc[op.key] = _Scatter(_add_halves(*op.result, "add_" + op.key))
                waiting.append(sc[op.key])
        return res

    da2, db2 = _ffn_bwd_act(dx3b, w("ffn2_w_down"), a2, b2, "ffn2_bwd_act")
    grad("ffn2_w_down", _mm_tn(act2, dx3b[None], 0.5, "ffn2_dwd", N_CHIPS))
    grad("ffn2_w_gate", carry(_mm_tn, da2, h3[None], 1.0, "ffn2_dwg", N_CHIPS))
    grad("ffn2_w_up", carry(_mm_tn, db2, h3[None], 1.0, "ffn2_dwu", N_CHIPS))
    dx2, dx2b, gs["ffn2_norm"] = carry(_mm_rmsbwd, [da2, db2], [w("ffn2_w_gate"), w("ffn2_w_up")], False, False, x2, g2,
                                       dx3, "ffn2_bwd_dx")

    dm = _mm_nt(dx2b, w_out, "mix_bwd")
    grad("w_out", _mm_tn(cat[None], dx2b[None], 1.0, "dwout", 1).reshape(N_CHIPS, -1, D))
    (du_s5, d_wglu, gs["s5_b_glu"], gs["s5_d"], d_crc, d_cic, d_bre, d_bim, d_are, d_aim) = carry(
        _s5_bwd, dm, y_pre, u, states, are_t, aim_t, bre, bim, cre, cim, d_skip, w_glu, b_glu)
    grad("s5_w_glu", d_wglu.astype(BF16).reshape(N_CHIPS, -1, S5_W))
    g_lr, g_li, g_ldt, g_brc, g_bic = _s5_params_bwd(lr, li, ldt, brc, bic, d_are.reshape(1, S5_N),
                                                     d_aim.reshape(1, S5_N), d_bre, d_bim)
    gs["s5_lam_re"], gs["s5_lam_im"] = g_lr, g_li
    gs["s5_log_dt"] = jnp.sum(g_ldt.reshape(S5_G, S5_P), axis=1)
    gs["s5_b_re"], gs["s5_b_im"] = _uncompact_b(g_brc), _uncompact_b(g_bic)
    gs["s5_c_re"] = _uncompact_b(d_crc.T).transpose(0, 2, 1)
    gs["s5_c_im"] = _uncompact_b(d_cic.T).transpose(0, 2, 1)
    du, d_wdw, gs["conv_b_dw"], gs["conv_ln_g"], gs["conv_ln_b"] = carry(_conv_bwd, dm, zc, u, du_s5, w_dw, ln_g, ln_b, mavg)
    d_wdw = jnp.sum(d_wdw.reshape(HALO, 8, CONV_W), axis=1)[:CONV_K]
    grad("conv_w_dw", d_wdw.reshape(CONV_K, N_CHIPS, -1).transpose(1, 0, 2))
    grad("w_in", carry(_mm_tn, h2[None], du, 1.0, "dwin", N_CHIPS, b_cols=True))
    dx1, dx1b, gs["mix_norm"] = carry(_mm_rmsbwd, [du], [w("w_in")], True, True, x1, gm, dx2, "in_proj_bwd")

    da1, db1 = carry(_ffn_bwd_act, dx1b, w("ffn1_w_down"), a1, b1, "ffn1_bwd_act")
    grad("ffn1_w_down", _mm_tn(act1, dx1b[None], 0.5, "ffn1_dwd", N_CHIPS))
    grad("ffn1_w_gate", carry(_mm_tn, da1, h1[None], 1.0, "ffn1_dwg", N_CHIPS))
    grad("ffn1_w_up", carry(_mm_tn, db1, h1[None], 1.0, "ffn1_dwu", N_CHIPS))
    grad_x, _, gs["ffn1_norm"] = carry(_mm_rmsbwd, [da1, db1], [w("ffn1_w_gate"), w("ffn1_w_up")], False, False, x, g1,
                                       dx1, "ffn1_bwd_dx")

    out = {}
    gsmall = {k: gs[k].reshape(wts[k].shape) for k in SMALL}
    zero_row = jnp.zeros((1, LANES), F32)
    g_all = carry(_gather_all, _pack_small(gsmall, loss_part))
    res = _adamw_small(_pack_small(s, zero_row), g_all, _pack_small({k: ms[k] for k in SMALL}, zero_row),
                       _pack_small({k: vs[k] for k in SMALL}, zero_row))
    unpacked = [_unpack_small(r, s) for r in res]
    loss = unpacked[0][1]
    for k in SMALL:
        out[k] = [u_[0][k] for u_ in unpacked]

    order = ("ffn2_w_down", "ffn2_w_gate", "ffn2_w_up", "w_out", "s5_w_glu", "conv_w_dw", "w_in", "ffn1_w_down",
             "ffn1_w_gate", "ffn1_w_up")
    back = {}
    for k in order:
        part = carry(_sum_slots, sc[k].result[0], "sum_" + k)
        back[k] = _SwapBack(part) if part.shape != shards[k].shape else _Swap(part)
        waiting.append(back[k])
    for k in order:
        parts = [back[k].result[0]] if isinstance(back[k], _SwapBack) else [back[k].ins[0], back[k].result[0]]
        res = carry(_adamw_sharded, shards[k], parts, _shard2d(k, ms[k]), _shard2d(k, vs[k]), "adamw_" + k)
        out[k] = [_unshard(k, r, wts[k].shape) for r in res]
    return loss, grad_x[None], out


def kernel(x, ffn1_norm, ffn1_w_gate, ffn1_w_up, ffn1_w_down, mix_norm, w_in, s5_lam_re, s5_lam_im, s5_log_dt, s5_b_re, s5_b_im, s5_c_re, s5_c_im, s5_d, s5_w_glu, s5_b_glu, conv_w_dw, conv_b_dw, conv_ln_g, conv_ln_b, w_out, ffn2_norm, ffn2_w_gate, ffn2_w_up, ffn2_w_down, final_norm, loss_target, m_ffn1_norm, m_ffn1_w_gate, m_ffn1_w_up, m_ffn1_w_down, m_mix_norm, m_w_in, m_s5_lam_re, m_s5_lam_im, m_s5_log_dt, m_s5_b_re, m_s5_b_im, m_s5_c_re, m_s5_c_im, m_s5_d, m_s5_w_glu, m_s5_b_glu, m_conv_w_dw, m_conv_b_dw, m_conv_ln_g, m_conv_ln_b, m_w_out, m_ffn2_norm, m_ffn2_w_gate, m_ffn2_w_up, m_ffn2_w_down, m_final_norm, v_ffn1_norm, v_ffn1_w_gate, v_ffn1_w_up, v_ffn1_w_down, v_mix_norm, v_w_in, v_s5_lam_re, v_s5_lam_im, v_s5_log_dt, v_s5_b_re, v_s5_b_im, v_s5_c_re, v_s5_c_im, v_s5_d, v_s5_w_glu, v_s5_b_glu, v_conv_w_dw, v_conv_b_dw, v_conv_ln_g, v_conv_ln_b, v_w_out, v_ffn2_norm, v_ffn2_w_gate, v_ffn2_w_up, v_ffn2_w_down, v_final_norm):
    given = dict(locals())
    wts = {k: given[k] for k in WEIGHTS}
    ms = {k: given["m_" + k] for k in WEIGHTS}
    vs = {k: given["v_" + k] for k in WEIGHTS}
    loss, grad_x, out = _train_step(x, loss_target, wts, ms, vs)
    return (loss, grad_x, *[out[k][0] for k in WEIGHTS], *[out[k][1] for k in WEIGHTS],
            *[out[k][2] for k in WEIGHTS], *[out[k][3] for k in WEIGHTS])
```

```python
import functools

import jax
import jax.numpy as jnp
import numpy as np
from jax import lax
from jax.experimental import pallas as pl
from jax.experimental.pallas import tpu as pltpu

F32, BF16 = jnp.float32, jnp.bfloat16
MESH = pl.DeviceIdType.MESH

EPS = 1e-6
ADAM_LR, ADAM_B1, ADAM_B2, ADAM_EPS, ADAM_WD, ADAM_STEP = 0.001, 0.9, 0.999, 1e-08, 0.01, 10

N_CHIPS = 4
N_DEV = 8
LANES = 128
BF16_ROWS = 16
S5_W, S5_G, S5_GC, S5_P = 512, 32, 16, 64
S5_N = S5_G * S5_P
S5_TILES = S5_N // LANES
S5_HALF = 8
CONV_W, CONV_K, CONV_HD = 512, 31, 64
HALO = 32
CONV_SB = 32
TM = 512
TMS = 1024
TK = 2048
TS = 256
VMEM_LIMIT = 48 << 20
GELU_C0, GELU_C1 = 0.7978845608028654, 0.044715

FFN_T = ("ffn1_w_gate", "ffn1_w_up", "ffn2_w_gate", "ffn2_w_up")
SHARDED = ("ffn1_w_gate", "ffn1_w_up", "ffn1_w_down", "w_in", "s5_w_glu", "conv_w_dw", "w_out",
           "ffn2_w_gate", "ffn2_w_up", "ffn2_w_down")
SMALL = ("ffn1_norm", "mix_norm", "s5_lam_re", "s5_lam_im", "s5_log_dt", "s5_b_re", "s5_b_im", "s5_c_re",
         "s5_c_im", "s5_d", "s5_b_glu", "conv_b_dw", "conv_ln_g", "conv_ln_b", "ffn2_norm", "final_norm")
WEIGHTS = ("ffn1_norm", "ffn1_w_gate", "ffn1_w_up", "ffn1_w_down", "mix_norm", "w_in", "s5_lam_re", "s5_lam_im",
           "s5_log_dt", "s5_b_re", "s5_b_im", "s5_c_re", "s5_c_im", "s5_d", "s5_w_glu", "s5_b_glu", "conv_w_dw",
           "conv_b_dw", "conv_ln_g", "conv_ln_b", "w_out", "ffn2_norm", "ffn2_w_gate", "ffn2_w_up", "ffn2_w_down",
           "final_norm")


def _dot(a, b):
    return jnp.dot(a, b, preferred_element_type=F32)


def _dot_nt(a, b):
    return lax.dot_general(a, b, (((1,), (1,)), ((), ())), preferred_element_type=F32)


def _dot_tn(a, b):
    return lax.dot_general(a, b, (((0,), (0,)), ((), ())), preferred_element_type=F32)


def _colsum(v):
    return jnp.sum(v, axis=0, keepdims=True)


def _sigmoid(v):
    return 1.0 / (1.0 + jnp.exp(-v))


def _accumulate(ref, first, value):
    @pl.when(first)
    def _():
        ref[...] = value

    @pl.when(jnp.logical_not(first))
    def _():
        ref[...] += value


def _position():
    x, y, c = lax.axis_index("x"), lax.axis_index("y"), lax.axis_index("c")
    return x, y, c, [(1 - x, y), (x, 1 - y), (1 - x, 1 - y)]


def _remote(src, dst, sems, send, recv, device):
    return pltpu.make_async_remote_copy(src_ref=src, dst_ref=dst, send_sem=sems.at[send], recv_sem=sems.at[recv],
                                        device_id=device, device_id_type=MESH)


class _Gather:
    def __init__(self, shard):
        self.ins = [shard]
        self.outs = [jax.ShapeDtypeStruct((N_CHIPS,) + shard.shape, shard.dtype)]
        self.rows = shard.shape[0]
        self.halve = shard.dtype == BF16 and self.rows % (2 * BF16_ROWS) == 0
        self.n_sem = 13 if self.halve else 7
        self.result = None

    def _copies(self, ins, outs, sems, s0, pos):
        x, y, c, chips = pos
        src, dst = ins[0], outs[0]
        me = 2 * x + y
        if self.halve:
            hr = self.rows // 2
            mine, theirs = pl.ds(c * hr, hr), pl.ds((1 - c) * hr, hr)
            part = lambda slot, rows: dst.at[slot, rows]
            my_src = src.at[mine]
        else:
            mine = theirs = None
            part = lambda slot, rows: dst.at[slot]
            my_src = src
        slot = lambda j: 2 * chips[j][0] + chips[j][1]
        local = lambda: pltpu.make_async_copy(src, dst.at[me], sems.at[s0])
        send = lambda j: _remote(my_src, part(me, mine), sems, s0 + 1 + j, s0 + 4 + j, (*chips[j], c))
        land = lambda j: _remote(my_src, part(slot(j), mine), sems, s0 + 1 + j, s0 + 4 + j, (*chips[j], c))
        fwd = lambda j: _remote(part(slot(j), mine), part(slot(j), mine), sems, s0 + 7 + j, s0 + 10 + j, (x, y, 1 - c))
        got = lambda j: _remote(part(slot(j), theirs), part(slot(j), theirs), sems, s0 + 7 + j, s0 + 10 + j,
                                (x, y, 1 - c))
        return local, send, land, fwd, got

    def start(self, ins, outs, sems, s0, pos):
        local, send, _, _, _ = self._copies(ins, outs, sems, s0, pos)
        local().start()
        for j in range(N_CHIPS - 1):
            send(j).start()

    def finish(self, ins, outs, sems, s0, pos):
        local, send, land, fwd, got = self._copies(ins, outs, sems, s0, pos)
        others = range(N_CHIPS - 1)
        for j in others:
            land(j).wait_recv()
            if self.halve:
                fwd(j).start()
        for j in others:
            if self.halve:
                got(j).wait_recv()
        for j in others:
            send(j).wait_send()
            if self.halve:
                fwd(j).wait_send()
        local().wait()


class _Scatter:
    def __init__(self, grad):
        self.ins = [grad]
        self.outs = [jax.ShapeDtypeStruct(grad.shape, grad.dtype)]
        self.n_sem = 7
        self.result = None

    def _copies(self, ins, outs, sems, s0, pos):
        x, y, c, chips = pos
        src, dst = ins[0], outs[0]
        me = 2 * x + y
        slot = lambda j: 2 * chips[j][0] + chips[j][1]
        local = lambda: pltpu.make_async_copy(src.at[me], dst.at[me], sems.at[s0])
        send = lambda j: _remote(src.at[slot(j)], dst.at[me], sems, s0 + 1 + j, s0 + 4 + j, (*chips[j], c))
        land = lambda j: _remote(src.at[me], dst.at[slot(j)], sems, s0 + 1 + j, s0 + 4 + j, (*chips[j], c))
        return local, send, land

    def start(self, ins, outs, sems, s0, pos):
        local, send, _ = self._copies(ins, outs, sems, s0, pos)
        local().start()
        for j in range(N_CHIPS - 1):
            send(j).start()

    def finish(self, ins, outs, sems, s0, pos):
        local, send, land = self._copies(ins, outs, sems, s0, pos)
        for j in range(N_CHIPS - 1):
            land(j).wait_recv()
        for j in range(N_CHIPS - 1):
            send(j).wait_send()
        local().wait()


class _Swap:
    def __init__(self, part):
        self.ins = [part]
        self.outs = [jax.ShapeDtypeStruct(part.shape, part.dtype)]
        self.n_sem = 2
        self.result = None

    def _copy(self, ins, outs, sems, s0, pos):
        x, y, c, _ = pos
        return _remote(ins[0], outs[0], sems, s0, s0 + 1, (x, y, 1 - c))

    def start(self, ins, outs, sems, s0, pos):
        self._copy(ins, outs, sems, s0, pos).start()

    def finish(self, ins, outs, sems, s0, pos):
        self._copy(ins, outs, sems, s0, pos).wait()


class _SwapHalf:
    def __init__(self, grad, key):
        slots, rows, cols = grad.shape
        half = jax.ShapeDtypeStruct((slots, rows // 2, cols), grad.dtype)
        self.ins, self.outs, self.key = [grad], [half, half], key
        self.hr = rows // 2
        self.n_sem = 3
        self.result = None

    def _copies(self, ins, outs, sems, s0, pos):
        x, y, c, _ = pos
        mine, theirs = pl.ds(c * self.hr, self.hr), pl.ds((1 - c) * self.hr, self.hr)
        local = pltpu.make_async_copy(ins[0].at[:, mine], outs[0], sems.at[s0])
        remote = _remote(ins[0].at[:, theirs], outs[1], sems, s0 + 1, s0 + 2, (x, y, 1 - c))
        return local, remote

    def start(self, ins, outs, sems, s0, pos):
        for cp in self._copies(ins, outs, sems, s0, pos):
            cp.start()

    def finish(self, ins, outs, sems, s0, pos):
        for cp in self._copies(ins, outs, sems, s0, pos):
            cp.wait()


class _SwapBack:
    def __init__(self, part):
        hr, cols = part.shape
        self.ins, self.outs = [part], [jax.ShapeDtypeStruct((2 * hr, cols), part.dtype)]
        self.hr = hr
        self.n_sem = 3
        self.result = None

    def _copies(self, ins, outs, sems, s0, pos):
        x, y, c, _ = pos
        mine, theirs = pl.ds(c * self.hr, self.hr), pl.ds((1 - c) * self.hr, self.hr)
        local = lambda: pltpu.make_async_copy(ins[0], outs[0].at[mine], sems.at[s0])
        send = lambda: _remote(ins[0], outs[0].at[mine], sems, s0 + 1, s0 + 2, (x, y, 1 - c))
        land = lambda: _remote(ins[0], outs[0].at[theirs], sems, s0 + 1, s0 + 2, (x, y, 1 - c))
        return local, send, land

    def start(self, ins, outs, sems, s0, pos):
        local, send, _ = self._copies(ins, outs, sems, s0, pos)
        local().start()
        send().start()

    def finish(self, ins, outs, sems, s0, pos):
        local, send, land = self._copies(ins, outs, sems, s0, pos)
        land().wait_recv()
        send().wait_send()
        local().wait()


def _pallas(body, args, *, name, grid, in_specs, out_specs, out_shape, scratch_shapes=(), comm=()):
    comm = list(comm)
    n_in, n_out, n_scr = len(in_specs), len(out_specs), len(scratch_shapes)
    c_in = [a for op in comm for a in op.ins]
    c_out = [s for op in comm for s in op.outs]
    n_sem = sum(op.n_sem for op in comm)

    def full(*refs):
        o0 = n_in + len(c_in)
        s0 = o0 + n_out + len(c_out)
        ins, cin = refs[:n_in], refs[n_in:o0]
        outs, cout = refs[o0:o0 + n_out], refs[o0 + n_out:s0]
        scratch = refs[s0:s0 + n_scr]
        if comm:
            sems = refs[s0 + n_scr]
            ids = [pl.program_id(d) for d in range(len(grid))]
            first = functools.reduce(jnp.logical_and, [i == 0 for i in ids])
            last = functools.reduce(jnp.logical_and, [i == g - 1 for i, g in zip(ids, grid)])
            pos = _position()

            def each(step):
                ci = co = cs = 0
                for op in comm:
                    getattr(op, step)(cin[ci:ci + len(op.ins)], cout[co:co + len(op.outs)], sems, cs, pos)
                    ci, co, cs = ci + len(op.ins), co + len(op.outs), cs + op.n_sem

            @pl.when(first)
            def _():
                each("start")

        body(*ins, *outs, *scratch)
        if comm:
            @pl.when(last)
            def _():
                each("finish")

    hbm = pl.BlockSpec(memory_space=pl.ANY)
    res = pl.pallas_call(
        full, name=name, grid=grid,
        in_specs=list(in_specs) + [hbm] * len(c_in), out_specs=list(out_specs) + [hbm] * len(c_out),
        out_shape=list(out_shape) + c_out,
        scratch_shapes=list(scratch_shapes) + ([pltpu.SemaphoreType.DMA((n_sem,))] if comm else []),
        compiler_params=pltpu.CompilerParams(dimension_semantics=("arbitrary",) * len(grid),
                                             vmem_limit_bytes=VMEM_LIMIT))(*args, *c_in)
    k = n_out
    for op in comm:
        op.result = list(res[k:k + len(op.outs)])
        k += len(op.outs)
    return list(res[:n_out])


def _row_tile(rows, cols, itemsize=4, budget=1 << 20):
    t = rows
    while t % (2 * BF16_ROWS) == 0 and t * cols * itemsize > budget:
        t //= 2
    return t


def _cast_bf16(w, name):
    rows, cols = w.shape
    tr = _row_tile(rows, cols)

    def body(w_ref, o_ref):
        o_ref[...] = w_ref[...].astype(BF16)

    spec = pl.BlockSpec((tr, cols), lambda i: (i, 0))
    return _pallas(body, [w], name=name, grid=(rows // tr,), in_specs=[spec], out_specs=[spec],
                   out_shape=[jax.ShapeDtypeStruct((rows, cols), BF16)])[0]


def _rms_fwd(x, g, name, comm=()):
    L, D = x.shape

    def body(x_ref, g_ref, h_ref):
        xf = x_ref[...]
        r = lax.rsqrt(jnp.mean(xf * xf, axis=-1, keepdims=True) + EPS)
        h_ref[...] = (xf * r * g_ref[...]).astype(BF16)

    row = pl.BlockSpec((TMS, D), lambda i: (i, 0))
    return _pallas(body, [x, g], name=name, grid=(L // TMS,),
                   in_specs=[row, pl.BlockSpec((1, D), lambda i: (0, 0))], out_specs=[row],
                   out_shape=[jax.ShapeDtypeStruct((L, D), BF16)], comm=comm)[0]


def _resident(shape):
    return pl.BlockSpec(shape, lambda *_: (0,) * len(shape), pipeline_mode=pl.Buffered(1))


def _ffn_up(h, wg_t, wu_t, name, comm=()):
    L, D = h.shape
    G, FS, _ = wg_t.shape

    def body(h_ref, wg_ref, wu_ref, a_ref, b_ref, act_ref):
        j = pl.program_id(1)
        hv = h_ref[...]
        a = _dot_nt(hv, wg_ref[j])
        b = _dot_nt(hv, wu_ref[j])
        a_ref[...] = a.astype(BF16)
        b_ref[...] = b.astype(BF16)
        act_ref[...] = (a * _sigmoid(a) * b).astype(BF16)

    ospec = pl.BlockSpec((None, TMS, FS), lambda i, j: (j, i, 0))
    oshape = jax.ShapeDtypeStruct((G, L, FS), BF16)
    return _pallas(body, [h, wg_t, wu_t], name=name, grid=(L // TMS, G),
                   in_specs=[pl.BlockSpec((TMS, D), lambda i, j: (i, 0)), _resident((G, FS, D)), _resident((G, FS, D))],
                   out_specs=[ospec, ospec, ospec], out_shape=[oshape, oshape, oshape], comm=comm)


def _group_sum(a_ref, w_ref, groups, mm=_dot):
    acc = mm(a_ref[0], w_ref[0])
    for j in range(1, groups):
        acc = acc + mm(a_ref[j], w_ref[j])
    return acc


def _ffn_down(act, wd, x, g_next, name, comm=()):
    G, L, FS = act.shape
    D = wd.shape[2]

    def body(act_ref, wd_ref, x_ref, g_ref, xn_ref, hn_ref):
        xn = x_ref[...] + 0.5 * _group_sum(act_ref, wd_ref, G)
        xn_ref[...] = xn
        r = lax.rsqrt(jnp.mean(xn * xn, axis=-1, keepdims=True) + EPS)
        hn_ref[...] = (xn * r * g_ref[...]).astype(BF16)

    row = pl.BlockSpec((TM, D), lambda i: (i, 0))
    return _pallas(body, [act, wd, x, g_next], name=name, grid=(L // TM,),
                   in_specs=[pl.BlockSpec((G, TM, FS), lambda i: (0, i, 0)), _resident((G, FS, D)), row,
                             pl.BlockSpec((1, D), lambda i: (0, 0))],
                   out_specs=[row, row],
                   out_shape=[jax.ShapeDtypeStruct((L, D), F32), jax.ShapeDtypeStruct((L, D), BF16)], comm=comm)


def _ffn_down_loss(act, wd, x, gf, tgt, name):
    G, L, FS = act.shape
    D = wd.shape[2]

    def body(act_ref, wd_ref, x_ref, g_ref, t_ref, dx_ref, dxb_ref, loss_ref, dg_ref):
        i = pl.program_id(0)
        xn = x_ref[...] + 0.5 * _group_sum(act_ref, wd_ref, G)
        r = lax.rsqrt(jnp.mean(xn * xn, axis=-1, keepdims=True) + EPS)
        xh = xn * r
        gv = g_ref[...]
        e = xh * gv - t_ref[...]
        part = 0.5 * jnp.sum(_colsum(e * e), axis=1, keepdims=True) / D
        dy = e / D
        _accumulate(loss_ref, i == 0, jnp.broadcast_to(part, (1, LANES)))
        _accumulate(dg_ref, i == 0, _colsum(dy * xh))
        dxh = dy * gv
        dx = r * (dxh - xh * jnp.mean(dxh * xh, axis=-1, keepdims=True))
        dx_ref[...] = dx
        dxb_ref[...] = dx.astype(BF16)

    row = pl.BlockSpec((TM, D), lambda i: (i, 0))
    return _pallas(body, [act, wd, x, gf, tgt], name=name, grid=(L // TM,),
                   in_specs=[pl.BlockSpec((G, TM, FS), lambda i: (0, i, 0)), _resident((G, FS, D)), row,
                             pl.BlockSpec((1, D), lambda i: (0, 0)), row],
                   out_specs=[row, row, pl.BlockSpec((1, LANES), lambda i: (0, 0)),
                              pl.BlockSpec((1, D), lambda i: (0, 0))],
                   out_shape=[jax.ShapeDtypeStruct((L, D), F32), jax.ShapeDtypeStruct((L, D), BF16),
                              jax.ShapeDtypeStruct((1, LANES), F32), jax.ShapeDtypeStruct((1, D), F32)])


def _ffn_bwd_act(dxb, wd, a, b, name, comm=()):
    L, D = dxb.shape
    G, FS, _ = wd.shape

    def body(dx_ref, wd_ref, a_ref, b_ref, da_ref, db_ref):
        dact = 0.5 * _dot_nt(dx_ref[...], wd_ref[pl.program_id(1)])
        av = a_ref[...].astype(F32)
        bv = b_ref[...].astype(F32)
        sg = _sigmoid(av)
        da_ref[...] = (dact * bv * sg * (1.0 + av * (1.0 - sg))).astype(BF16)
        db_ref[...] = (dact * av * sg).astype(BF16)

    gspec = pl.BlockSpec((None, TMS, FS), lambda i, j: (j, i, 0))
    oshape = jax.ShapeDtypeStruct((G, L, FS), BF16)
    return _pallas(body, [dxb, wd, a, b], name=name, grid=(L // TMS, G),
                   in_specs=[pl.BlockSpec((TMS, D), lambda i, j: (i, 0)), _resident((G, FS, D)), gspec, gspec],
                   out_specs=[gspec, gspec], out_shape=[oshape, oshape], comm=comm)


def _mm_grouped(a, w, name, comm=()):
    L, K = a.shape
    G, _, N = w.shape

    def body(a_ref, w_ref, o_ref):
        o_ref[...] = _dot(a_ref[...], w_ref[pl.program_id(1)])

    return _pallas(body, [a, w], name=name, grid=(L // TMS, G),
                   in_specs=[pl.BlockSpec((TMS, K), lambda i, g: (i, 0)), _resident((G, K, N))],
                   out_specs=[pl.BlockSpec((TMS, N), lambda i, g: (i, g))],
                   out_shape=[jax.ShapeDtypeStruct((L, G * N), F32)], comm=comm)[0]


def _mm_nt(a, w, name):
    L, K = a.shape
    N = w.shape[0]

    def body(a_ref, w_ref, o_ref):
        o_ref[...] = _dot_nt(a_ref[...], w_ref[...]).astype(BF16)

    return _pallas(body, [a, w], name=name, grid=(L // TMS,),
                   in_specs=[pl.BlockSpec((TMS, K), lambda i: (i, 0)), _resident((N, K))],
                   out_specs=[pl.BlockSpec((TMS, N), lambda i: (i, 0))],
                   out_shape=[jax.ShapeDtypeStruct((L, N), BF16)])[0]


def _mm_tn(a, b, scale, name, groups, b_cols=False, comm=()):
    L, M = a.shape[1], a.shape[2]
    N = b.shape[1] // groups if b_cols else b.shape[2]
    tk = min(L, TK)
    nk = L // tk

    def spec(v, cols):
        if cols:
            return pl.BlockSpec((tk, v.shape[1] // groups), lambda g, k: (k, g))
        if v.shape[0] > 1:
            return pl.BlockSpec((None, tk, v.shape[2]), lambda g, k: (g, k, 0))
        return pl.BlockSpec((None, tk, v.shape[2]), lambda g, k: (0, k, 0))

    def body(a_ref, b_ref, o_ref, acc):
        k = pl.program_id(1)
        p = _dot_tn(a_ref[...], b_ref[...])
        if nk == 1:
            o_ref[...] = (p * scale).astype(BF16)
        else:
            _accumulate(acc, k == 0, p)

            @pl.when(k == nk - 1)
            def _():
                o_ref[...] = (acc[...] * scale).astype(BF16)

    return _pallas(body, [a, b], name=name, grid=(groups, nk),
                   in_specs=[spec(a, False), spec(b, b_cols)],
                   out_specs=[pl.BlockSpec((None, M, N), lambda g, k: (g, 0, 0))],
                   out_shape=[jax.ShapeDtypeStruct((groups, M, N), BF16)],
                   scratch_shapes=[pltpu.VMEM((M, N), F32)], comm=comm)[0]


def _mm_rmsbwd(a_list, w_list, nt, a_cols, x_in, g, dx_out, name, comm=()):
    P = len(a_list)
    G = w_list[0].shape[0]
    L, D = x_in.shape
    mm = _dot_nt if nt else _dot

    def body(*refs):
        a_refs, w_refs = refs[:P], refs[P:2 * P]
        x_ref, g_ref, dxo_ref, dx_ref, dxb_ref, dg_ref = refs[2 * P:]
        i = pl.program_id(0)
        dh = None
        for a_ref, w_ref in zip(a_refs, w_refs):
            for j in range(G):
                if a_cols:
                    kw = a_ref.shape[1] // G
                    term = mm(a_ref[:, j * kw:(j + 1) * kw], w_ref[j])
                else:
                    term = mm(a_ref[j], w_ref[j])
                dh = term if dh is None else dh + term
        xf = x_ref[...]
        r = lax.rsqrt(jnp.mean(xf * xf, axis=-1, keepdims=True) + EPS)
        xh = xf * r
        _accumulate(dg_ref, i == 0, _colsum(dh * xh))
        dxh = dh * g_ref[...]
        dx = dxo_ref[...] + r * (dxh - xh * jnp.mean(dxh * xh, axis=-1, keepdims=True))
        dx_ref[...] = dx
        dxb_ref[...] = dx.astype(BF16)

    row = pl.BlockSpec((TM, D), lambda i: (i, 0))
    vec = pl.BlockSpec((1, D), lambda i: (0, 0))
    if a_cols:
        a_specs = [pl.BlockSpec((TM, a.shape[1]), lambda i: (i, 0)) for a in a_list]
    else:
        a_specs = [pl.BlockSpec((G, TM, a.shape[2]), lambda i: (0, i, 0)) for a in a_list]
    w_specs = [_resident(w.shape) for w in w_list]
    return _pallas(body, [*a_list, *w_list, x_in, g, dx_out], name=name, grid=(L // TM,),
                   in_specs=a_specs + w_specs + [row, vec, row], out_specs=[row, row, vec],
                   out_shape=[jax.ShapeDtypeStruct((L, D), F32), jax.ShapeDtypeStruct((L, D), BF16),
                              jax.ShapeDtypeStruct((1, D), F32)], comm=comm)


def _mix_out(cat, wout, x1, g_next, name):
    L, K = cat.shape
    D = wout.shape[1]

    def body(c_ref, w_ref, x_ref, g_ref, xn_ref, hn_ref):
        xn = x_ref[...] + _dot(c_ref[...], w_ref[...])
        xn_ref[...] = xn
        r = lax.rsqrt(jnp.mean(xn * xn, axis=-1, keepdims=True) + EPS)
        hn_ref[...] = (xn * r * g_ref[...]).astype(BF16)

    row = pl.BlockSpec((TMS, D), lambda i: (i, 0))
    return _pallas(body, [cat, wout, x1, g_next], name=name, grid=(L // TMS,),
                   in_specs=[pl.BlockSpec((TMS, K), lambda i: (i, 0)), pl.BlockSpec((K, D), lambda i: (0, 0)), row,
                             pl.BlockSpec((1, D), lambda i: (0, 0))],
                   out_specs=[row, row],
                   out_shape=[jax.ShapeDtypeStruct((L, D), F32), jax.ShapeDtypeStruct((L, D), BF16)])


def _s5_disc(lr, li, ldt, brc, bic):
    dt = jnp.exp(ldt)
    mag = jnp.exp(lr * dt)
    are = mag * jnp.cos(li * dt)
    aim = mag * jnp.sin(li * dt)
    den = lr * lr + li * li
    nre = are - 1.0
    fre = (nre * lr + aim * li) / den
    fim = (aim * lr - nre * li) / den
    return are, aim, fre * brc - fim * bic, fre * bic + fim * brc


def _s5_params_fwd(lr, li, ldt, brc, bic, crc, cic):
    def body(lr_ref, li_ref, ldt_ref, br_ref, bi_ref, cr_ref, ci_ref, are_ref, aim_ref, bre_ref, bim_ref, cre_ref, cim_ref):
        are, aim, bre, bim = _s5_disc(lr_ref[...], li_ref[...], ldt_ref[...], br_ref[...], bi_ref[...])
        are_ref[...] = are
        aim_ref[...] = aim
        bre_ref[...] = bre.astype(BF16)
        bim_ref[...] = bim.astype(BF16)
        cre_ref[...] = cr_ref[...].astype(BF16)
        cim_ref[...] = ci_ref[...].astype(BF16)

    vec = jax.ShapeDtypeStruct((1, S5_N), F32)
    return pl.pallas_call(
        body, name="s5_params_fwd",
        out_shape=[vec, vec, jax.ShapeDtypeStruct((LANES, S5_N), BF16), jax.ShapeDtypeStruct((LANES, S5_N), BF16),
                   jax.ShapeDtypeStruct((S5_N, LANES), BF16), jax.ShapeDtypeStruct((S5_N, LANES), BF16)],
        compiler_params=pltpu.CompilerParams(vmem_limit_bytes=VMEM_LIMIT))(lr, li, ldt, brc, bic, crc, cic)


def _s5_params_bwd(lr, li, ldt, brc, bic, dare, daim, dbre, dbim):
    def body(lr_ref, li_ref, ldt_ref, br_ref, bi_ref, dare_ref, daim_ref, dbre_ref, dbim_ref,
             glr_ref, gli_ref, gldt_ref, gbr_ref, gbi_ref):
        _, vjp = jax.vjp(_s5_disc, lr_ref[...], li_ref[...], ldt_ref[...], br_ref[...], bi_ref[...])
        glr, gli, gldt, gbr, gbi = vjp((dare_ref[...], daim_ref[...], dbre_ref[...], dbim_ref[...]))
        glr_ref[...] = glr
        gli_ref[...] = gli
        gldt_ref[...] = gldt
        gbr_ref[...] = gbr
        gbi_ref[...] = gbi

    vec = jax.ShapeDtypeStruct((1, S5_N), F32)
    mat = jax.ShapeDtypeStruct((LANES, S5_N), F32)
    return pl.pallas_call(
        body, name="s5_params_bwd", out_shape=[vec, vec, vec, mat, mat],
        compiler_params=pltpu.CompilerParams(vmem_limit_bytes=VMEM_LIMIT))(lr, li, ldt, brc, bic, dare, daim, dbre, dbim)


def _gelu_parts(y):
    th = jnp.tanh(GELU_C0 * (y + GELU_C1 * y * y * y))
    return 0.5 * y * (1.0 + th), th


def _state_rows(q, T):
    return pl.ds(q % S5_HALF, T, stride=S5_HALF)


def _load_tiles(bufs, ct, T, dtype):
    return jnp.concatenate([bufs[q // S5_HALF][_state_rows(q, T), :].astype(dtype) for q in range(4 * ct, 4 * ct + 4)],
                           axis=1)


def _store_tiles(bufs, ct, T, value):
    for k, q in enumerate(range(4 * ct, 4 * ct + 4)):
        bufs[q // S5_HALF][_state_rows(q, T), :] = value[:, k * LANES:(k + 1) * LANES]


def _s5_fwd(u, are, aim, bre, bim, cre, cim, d_skip, wglu, bglu, comm=()):
    L = u.shape[0]
    T = min(TS, L)
    n = L // T

    def body(u_ref, are_ref, aim_ref, bre_ref, bim_ref, cre_ref, cim_ref, d_ref, wg_ref, bg_ref,
             sre_lo, sre_hi, sim_lo, sim_hi, y_ref, o_ref, st_re, st_im):
        i = pl.program_id(0)
        sre, sim = (sre_lo, sre_hi), (sim_lo, sim_hi)

        @pl.when(i == 0)
        def _():
            st_re[...] = jnp.zeros_like(st_re)
            st_im[...] = jnp.zeros_like(st_im)

        uf = u_ref[...]
        ub = uf.astype(BF16)
        for ct in range(4):
            uq = ub[:, ct * LANES:(ct + 1) * LANES]
            win = slice(4 * ct * LANES, 4 * (ct + 1) * LANES)
            _store_tiles(sre, ct, T, _dot(uq, bre_ref[:, win]))
            _store_tiles(sim, ct, T, _dot(uq, bim_ref[:, win]))
        halves = [slice(h * S5_HALF, (h + 1) * S5_HALF) for h in range(2)]
        a_re = [are_ref[hs, :] for hs in halves]
        a_im = [aim_ref[hs, :] for hs in halves]

        def step(t, carry):
            rows = pl.ds(pl.multiple_of(t * S5_HALF, S5_HALF), S5_HALF)
            out = []
            for h in range(2):
                s_re, s_im = carry[2 * h], carry[2 * h + 1]
                n_re = a_re[h] * s_re - a_im[h] * s_im + sre[h][rows, :]
                n_im = a_re[h] * s_im + a_im[h] * s_re + sim[h][rows, :]
                sre[h][rows, :] = n_re
                sim[h][rows, :] = n_im
                out += [n_re, n_im]
            return tuple(out)

        init = (st_re[halves[0], :], st_im[halves[0], :], st_re[halves[1], :], st_im[halves[1], :])
        fin = lax.fori_loop(0, T, step, init, unroll=4)
        for h in range(2):
            st_re[halves[h], :] = fin[2 * h]
            st_im[halves[h], :] = fin[2 * h + 1]
        tiles = []
        for ct in range(4):
            win = slice(4 * ct * LANES, 4 * (ct + 1) * LANES)
            tiles.append(_dot(_load_tiles(sre, ct, T, BF16), cre_ref[win, :])
                         - _dot(_load_tiles(sim, ct, T, BF16), cim_ref[win, :]))
        y = jnp.concatenate(tiles, axis=1) + d_ref[...] * uf
        y_ref[...] = y
        yg, _ = _gelu_parts(y)
        gate = _sigmoid(_dot(yg.astype(BF16), wg_ref[...]) + bg_ref[...])
        o_ref[...] = (yg * gate).astype(BF16)

    const = lambda shape: pl.BlockSpec(shape, lambda i: (0, 0))
    sspec = pl.BlockSpec((T * S5_HALF, LANES), lambda i: (i, 0))
    sshape = jax.ShapeDtypeStruct((L * S5_HALF, LANES), F32)
    chunk = pl.BlockSpec((T, S5_W), lambda i: (i, 0))
    return _pallas(body, [u, are, aim, bre, bim, cre, cim, d_skip, wglu, bglu], name="s5_fwd", grid=(n,),
                   in_specs=[chunk, const((S5_TILES, LANES)), const((S5_TILES, LANES)),
                             const((LANES, S5_N)), const((LANES, S5_N)), const((S5_N, LANES)), const((S5_N, LANES)),
                             const((1, S5_W)), const((S5_W, S5_W)), const((1, S5_W))],
                   out_specs=[sspec] * 4 + [chunk, chunk],
                   out_shape=[sshape] * 4 + [jax.ShapeDtypeStruct((L, S5_W), F32), jax.ShapeDtypeStruct((L, S5_W), BF16)],
                   scratch_shapes=[pltpu.VMEM((S5_TILES, LANES), F32), pltpu.VMEM((S5_TILES, LANES), F32)], comm=comm)


def _s5_bwd(dm, y_pre, u, states, are, aim, bre, bim, cre, cim, d_skip, wglu, bglu, comm=()):
    L = u.shape[0]
    T = min(TS, L)
    n = L // T

    def body(dm_ref, y_ref, u_ref, sre_lo, sre_hi, sim_lo, sim_hi, pre_lo, pre_hi, pim_lo, pim_hi,
             are_ref, aim_ref, bre_ref, bim_ref, cre_ref, cim_ref, d_ref, wg_ref, bg_ref,
             du_ref, dwg_ref, dbg_ref, dd_ref, dcre_ref, dcim_ref, dbre_ref, dbim_ref, dare_ref, daim_ref,
             gre_lo, gre_hi, gim_lo, gim_hi, car_re, car_im):
        i = pl.program_id(0)
        first = i == 0
        sre, sim = (sre_lo, sre_hi), (sim_lo, sim_hi)
        gre, gim = (gre_lo, gre_hi), (gim_lo, gim_hi)
        pre, pim = (pre_lo, pre_hi), (pim_lo, pim_hi)

        @pl.when(first)
        def _():
            car_re[...] = jnp.zeros_like(car_re)
            car_im[...] = jnp.zeros_like(car_im)
            dcre_ref[...] = jnp.zeros_like(dcre_ref)
            dcim_ref[...] = jnp.zeros_like(dcim_ref)
            dbre_ref[...] = jnp.zeros_like(dbre_ref)
            dbim_ref[...] = jnp.zeros_like(dbim_ref)

        y = y_ref[...]
        uf = u_ref[...]
        yg, th = _gelu_parts(y)
        dgelu = 0.5 * (1.0 + th) + 0.5 * y * (1.0 - th * th) * GELU_C0 * (1.0 + 3.0 * GELU_C1 * y * y)
        ygb = yg.astype(BF16)
        sg = _sigmoid(_dot(ygb, wg_ref[...]) + bg_ref[...])
        dout = dm_ref[...].astype(F32)
        dgp = dout * yg * sg * (1.0 - sg)
        dgpb = dgp.astype(BF16)
        dyg = dout * sg + _dot_nt(dgpb, wg_ref[...])
        _accumulate(dwg_ref, first, _dot_tn(ygb, dgpb))
        _accumulate(dbg_ref, first, _colsum(dgp))
        dy = dyg * dgelu
        _accumulate(dd_ref, first, _colsum(dy * uf))
        dyb = dy.astype(BF16)
        ub = uf.astype(BF16)

        for ct in range(4):
            win = slice(4 * ct * LANES, 4 * (ct + 1) * LANES)
            dyq = dyb[:, ct * LANES:(ct + 1) * LANES]
            dcre_ref[win, :] += _dot_tn(_load_tiles(sre, ct, T, BF16), dyq)
            dcim_ref[win, :] -= _dot_tn(_load_tiles(sim, ct, T, BF16), dyq)
            _store_tiles(gre, ct, T, _dot_nt(dyq, cre_ref[win, :]))
            _store_tiles(gim, ct, T, -_dot_nt(dyq, cim_ref[win, :]))

        halves = [slice(h * S5_HALF, (h + 1) * S5_HALF) for h in range(2)]
        a_re = [are_ref[hs, :] for hs in halves]
        a_im = [aim_ref[hs, :] for hs in halves]

        def adjoint(t, h, g_re, g_im):
            rows = pl.ds(pl.multiple_of(t * S5_HALF, S5_HALF), S5_HALF)
            n_re = gre[h][rows, :] + a_re[h] * g_re + a_im[h] * g_im
            n_im = gim[h][rows, :] + a_re[h] * g_im - a_im[h] * g_re
            gre[h][rows, :] = n_re
            gim[h][rows, :] = n_im
            return n_re, n_im

        def step(k, carry):
            out = []
            for h in range(2):
                out += adjoint(T - 1 - k, h, carry[2 * h], carry[2 * h + 1])
            return tuple(out)

        init = (car_re[halves[0], :], car_im[halves[0], :], car_re[halves[1], :], car_im[halves[1], :])
        fin = lax.fori_loop(0, T, step, init, unroll=4)
        keep = (i < n - 1).astype(F32)
        later, earlier = slice(S5_HALF, T * S5_HALF), slice(0, (T - 1) * S5_HALF)
        fold = lambda v: jnp.sum(v.reshape(T - 1, S5_HALF, LANES), axis=0)
        for h in range(2):
            g_re, g_im = fin[2 * h], fin[2 * h + 1]
            car_re[halves[h], :] = g_re
            car_im[halves[h], :] = g_im
            p_re, p_im = pre[h][...] * keep, pim[h][...] * keep
            da_re = fold(gre[h][later, :] * sre[h][earlier, :] + gim[h][later, :] * sim[h][earlier, :])
            da_im = fold(gim[h][later, :] * sre[h][earlier, :] - gre[h][later, :] * sim[h][earlier, :])
            da_re = da_re + g_re * p_re + g_im * p_im
            da_im = da_im + g_im * p_re - g_re * p_im

            @pl.when(first)
            def _():
                dare_ref[halves[h], :] = da_re
                daim_ref[halves[h], :] = da_im

            @pl.when(jnp.logical_not(first))
            def _():
                dare_ref[halves[h], :] += da_re
                daim_ref[halves[h], :] += da_im

        tiles = []
        for ct in range(4):
            uq = ub[:, ct * LANES:(ct + 1) * LANES]
            win = slice(4 * ct * LANES, 4 * (ct + 1) * LANES)
            g_re, g_im = _load_tiles(gre, ct, T, BF16), _load_tiles(gim, ct, T, BF16)
            tiles.append(d_ref[:, ct * LANES:(ct + 1) * LANES] * dy[:, ct * LANES:(ct + 1) * LANES]
                         + _dot_nt(g_re, bre_ref[:, win]) + _dot_nt(g_im, bim_ref[:, win]))
            dbre_ref[:, win] += _dot_tn(uq, g_re)
            dbim_ref[:, win] += _dot_tn(uq, g_im)
        du_ref[...] = jnp.concatenate(tiles, axis=1).astype(BF16)

    rev = lambda i: (n - 1 - i, 0)
    const = lambda shape: pl.BlockSpec(shape, lambda i: (0, 0))
    chunk = pl.BlockSpec((T, S5_W), rev)
    sspec = pl.BlockSpec((T * S5_HALF, LANES), rev)
    pspec = pl.BlockSpec((S5_HALF, LANES), lambda i: (jnp.maximum((n - 1 - i) * T - 1, 0), 0))
    tile = jax.ShapeDtypeStruct((S5_TILES, LANES), F32)
    vec = jax.ShapeDtypeStruct((1, S5_W), F32)
    sbuf = pltpu.VMEM((T * S5_HALF, LANES), F32)
    return _pallas(
        body, [dm, y_pre, u, *states, *states, are, aim, bre, bim, cre, cim, d_skip, wglu, bglu],
        name="s5_bwd", grid=(n,),
        in_specs=[chunk, chunk, chunk] + [sspec] * 4 + [pspec] * 4 + [
            const((S5_TILES, LANES)), const((S5_TILES, LANES)), const((LANES, S5_N)), const((LANES, S5_N)),
            const((S5_N, LANES)), const((S5_N, LANES)), const((1, S5_W)), const((S5_W, S5_W)), const((1, S5_W))],
        out_specs=[chunk, const((S5_W, S5_W)), const((1, S5_W)), const((1, S5_W)), const((S5_N, LANES)),
                   const((S5_N, LANES)), const((LANES, S5_N)), const((LANES, S5_N)), const((S5_TILES, LANES)),
                   const((S5_TILES, LANES))],
        out_shape=[jax.ShapeDtypeStruct((L, S5_W), BF16), jax.ShapeDtypeStruct((S5_W, S5_W), F32), vec, vec,
                   jax.ShapeDtypeStruct((S5_N, LANES), F32), jax.ShapeDtypeStruct((S5_N, LANES), F32),
                   jax.ShapeDtypeStruct((LANES, S5_N), F32), jax.ShapeDtypeStruct((LANES, S5_N), F32), tile, tile],
        scratch_shapes=[sbuf, sbuf, sbuf, sbuf, pltpu.VMEM((S5_TILES, LANES), F32), pltpu.VMEM((S5_TILES, LANES), F32)],
        comm=comm)


_EYE8 = np.eye(8, dtype=np.float32)


def _compact_b(b):
    return jnp.einsum("akpc,kj->jcakp", b.reshape(4, 8, S5_P, S5_GC), _EYE8).reshape(LANES, S5_N)


def _uncompact_b(m):
    return jnp.einsum("kcakp->akpc", m.reshape(8, S5_GC, 4, 8, S5_P)).reshape(S5_G, S5_P, S5_GC)


_HEAD_MEAN = np.kron(np.eye(CONV_W // CONV_HD, dtype=np.float32), np.full((CONV_HD, CONV_HD), 1.0 / CONV_HD, np.float32))


def _head_mean(v, m):
    hi = v.astype(BF16)
    lo = (v - hi.astype(F32)).astype(BF16)
    return _dot(hi, m) + _dot(lo, m)


def _head_norm(zc, m):
    d = zc - _head_mean(zc, m)
    rstd = lax.rsqrt(_head_mean(d * d, m) + EPS)
    return d * rstd, rstd


def _taps_by_phase(first):
    groups = {}
    for k in range(CONV_K):
        m, s = divmod(first + k, 8)
        groups.setdefault(s, []).append((m, k))
    return groups


def _causal_taps(buf, w_ref, r0, first, flip):
    acc = None
    for s, taps in sorted(_taps_by_phase(first).items()):
        rows = CONV_SB + (8 if s else 0)
        y = None
        for m, k in taps:
            kk = CONV_K - 1 - k if flip else k
            term = w_ref[kk:kk + 1, :] * buf[pl.ds(r0 + 8 * m, rows), :]
            y = term if y is None else y + term
        y = y[s:s + CONV_SB, :]
        acc = y if acc is None else acc + y
    return acc


def _conv_fwd(u, o_s5, wdw, bdw, lng, lnb, mavg, comm=()):
    L = u.shape[0]
    T = min(TS, L)
    n = L // T
    first_tap = HALO - (CONV_K - 1)

    def body(v1_ref, v2_ref, s5_ref, w_ref, b_ref, g_ref, be_ref, m_ref, zc_ref, o_ref, zbuf):
        i = pl.program_id(0)

        @pl.when(i == 0)
        def _():
            zbuf[0:HALO, :] = jnp.zeros((HALO, CONV_W), F32)

        zbuf[HALO:HALO + T, :] = v1_ref[...] * _sigmoid(v2_ref[...])
        for r0 in range(0, T, CONV_SB):
            zc_ref[r0:r0 + CONV_SB, :] = b_ref[...] + _causal_taps(zbuf, w_ref, r0, first_tap, False)
        zbuf[0:HALO, :] = zbuf[T:T + HALO, :]
        zn, _ = _head_norm(zc_ref[...], m_ref[...])
        zz = zn * g_ref[...] + be_ref[...]
        o_ref[:, 0:S5_W] = s5_ref[...]
        o_ref[:, S5_W:S5_W + CONV_W] = (zz * _sigmoid(zz)).astype(BF16)

    const = lambda shape: pl.BlockSpec(shape, lambda i: (0, 0))
    vec = const((1, CONV_W))
    return _pallas(body, [u, u, o_s5, wdw, bdw, lng, lnb, mavg], name="conv_fwd", grid=(n,),
                   in_specs=[pl.BlockSpec((T, CONV_W), lambda i: (i, 1)), pl.BlockSpec((T, CONV_W), lambda i: (i, 2)),
                             pl.BlockSpec((T, S5_W), lambda i: (i, 0)), const((CONV_K, CONV_W)), vec, vec, vec,
                             const((CONV_W, CONV_W))],
                   out_specs=[pl.BlockSpec((T, CONV_W), lambda i: (i, 0)),
                              pl.BlockSpec((T, S5_W + CONV_W), lambda i: (i, 0))],
                   out_shape=[jax.ShapeDtypeStruct((L, CONV_W), F32), jax.ShapeDtypeStruct((L, S5_W + CONV_W), BF16)],
                   scratch_shapes=[pltpu.VMEM((T + HALO, CONV_W), F32)], comm=comm)


def _conv_bwd(dm, zc, u, du_s5, wdw, lng, lnb, mavg, comm=()):
    L = u.shape[0]
    T = min(TS, L)
    n = L // T
    hb = T // HALO
    first_tap = HALO - (CONV_K - 1)

    def body(dm_ref, zc_ref, v1_ref, v2_ref, p1_ref, p2_ref, s5_ref, w_ref, g_ref, be_ref, m_ref,
             du_ref, dw_ref, db_ref, dg_ref, dbe_ref, zbuf, dzbuf, head):
        i = pl.program_id(0)
        first = i == 0

        @pl.when(first)
        def _():
            head[...] = jnp.zeros_like(head)
            dw_ref[...] = jnp.zeros_like(dw_ref)

        zn, rstd = _head_norm(zc_ref[...], m_ref[...])
        zz = zn * g_ref[...] + be_ref[...]
        sg = _sigmoid(zz)
        dzz = dm_ref[...].astype(F32) * sg * (1.0 + zz * (1.0 - sg))
        _accumulate(dbe_ref, first, _colsum(dzz))
        _accumulate(dg_ref, first, _colsum(dzz * zn))
        dzn = dzz * g_ref[...]
        dzc = rstd * (dzn - _head_mean(dzn, m_ref[...]) - zn * _head_mean(dzn * zn, m_ref[...]))
        _accumulate(db_ref, first, _colsum(dzc))

        dzbuf[0:T, :] = dzc
        dzbuf[T:T + HALO, :] = head[...]
        head[...] = dzbuf[0:HALO, :]
        keep = (i < n - 1).astype(F32)
        zbuf[0:HALO, :] = p1_ref[...] * _sigmoid(p2_ref[...]) * keep
        zbuf[HALO:HALO + T, :] = v1_ref[...] * _sigmoid(v2_ref[...])
        du_ref[:, 0:S5_W] = s5_ref[...]

        for r0 in range(0, T, CONV_SB):
            rows = slice(r0, r0 + CONV_SB)
            dzc_b = dzbuf[rows, :]
            for s, taps in sorted(_taps_by_phase(first_tap).items()):
                pad = ([jnp.zeros((s, CONV_W), F32)] if s else []) + [dzc_b] + ([jnp.zeros((8 - s, CONV_W), F32)] if s else [])
                shifted = jnp.concatenate(pad, axis=0) if s else dzc_b
                n_rows = shifted.shape[0]
                for m, k in taps:
                    prod = shifted * zbuf[pl.ds(r0 + 8 * m, n_rows), :]
                    dw_ref[8 * k:8 * k + 8, :] += jnp.sum(prod.reshape(n_rows // 8, 8, CONV_W), axis=0)
            dz = _causal_taps(dzbuf, w_ref, r0, 0, True)
            v1 = v1_ref[rows, :]
            sg2 = _sigmoid(v2_ref[rows, :])
            du_ref[rows, S5_W:S5_W + CONV_W] = (dz * sg2).astype(BF16)
            du_ref[rows, S5_W + CONV_W:S5_W + 2 * CONV_W] = (dz * v1 * sg2 * (1.0 - sg2)).astype(BF16)

    rev = lambda c: (lambda i: (n - 1 - i, c))
    prev = lambda c: (lambda i: (jnp.maximum((n - 1 - i) * hb - 1, 0), c))
    const = lambda shape: pl.BlockSpec(shape, lambda i: (0, 0))
    vec = const((1, CONV_W))
    vshape = jax.ShapeDtypeStruct((1, CONV_W), F32)
    return _pallas(
        body, [dm, zc, u, u, u, u, du_s5, wdw, lng, lnb, mavg], name="conv_bwd", grid=(n,),
        in_specs=[pl.BlockSpec((T, CONV_W), rev(1)), pl.BlockSpec((T, CONV_W), rev(0)),
                  pl.BlockSpec((T, CONV_W), rev(1)), pl.BlockSpec((T, CONV_W), rev(2)),
                  pl.BlockSpec((HALO, CONV_W), prev(1)), pl.BlockSpec((HALO, CONV_W), prev(2)),
                  pl.BlockSpec((T, S5_W), rev(0)), const((CONV_K, CONV_W)), vec, vec, const((CONV_W, CONV_W))],
        out_specs=[pl.BlockSpec((T, S5_W + 2 * CONV_W), rev(0)), const((8 * HALO, CONV_W)), vec, vec, vec],
        out_shape=[jax.ShapeDtypeStruct((L, S5_W + 2 * CONV_W), BF16), jax.ShapeDtypeStruct((8 * HALO, CONV_W), F32),
                   vshape, vshape, vshape],
        scratch_shapes=[pltpu.VMEM((T + HALO, CONV_W), F32), pltpu.VMEM((T + HALO, CONV_W), F32),
                        pltpu.VMEM((HALO, CONV_W), F32)], comm=comm)


def _gather_all(v, comm=()):
    rows, cols = v.shape

    def body(x_ref, out_ref, send_sems, recv_sems, local_sem):
        x, y, c, chips = _position()
        me, sibling = (x, y, c), (x, y, 1 - c)

        def block(px, py, pc):
            return out_ref.at[pl.ds((4 * px + 2 * py + pc) * rows, rows), :]

        def copy(k, blk, to, src=None):
            return pltpu.make_async_remote_copy(
                src_ref=block(*blk) if src is None else src, dst_ref=block(*blk), send_sem=send_sems.at[k],
                recv_sem=recv_sems.at[k], device_id=to, device_id_type=MESH)

        mine = pltpu.make_async_copy(x_ref, block(*me), local_sem)
        mine.start()
        first = [copy(0, me, sibling, src=x_ref)]
        first += [copy(1 + j, me, (*chip, c), src=x_ref) for j, chip in enumerate(chips)]
        for cp in first:
            cp.start()
        passed = [copy(4 + j, (*chip, c), sibling) for j, chip in enumerate(chips)]
        for j, chip in enumerate(chips):
            copy(1 + j, (*chip, c), me).wait_recv()
            passed[j].start()
        copy(0, sibling, me).wait_recv()
        for j, chip in enumerate(chips):
            copy(4 + j, (*chip, 1 - c), me).wait_recv()
        for cp in first + passed:
            cp.wait_send()
        mine.wait()

    whole = pl.BlockSpec(memory_space=pltpu.VMEM)
    return _pallas(body, [v], name="gather_small", grid=(1,), in_specs=[whole], out_specs=[whole],
                   out_shape=[jax.ShapeDtypeStruct((N_DEV * rows, cols), v.dtype)],
                   scratch_shapes=[pltpu.SemaphoreType.DMA((7,)), pltpu.SemaphoreType.DMA((7,)), pltpu.SemaphoreType.DMA],
                   comm=comm)[0]


def _adamw(w, g, m, v):
    m = ADAM_B1 * m + (1.0 - ADAM_B1) * g
    v = ADAM_B2 * v + (1.0 - ADAM_B2) * jnp.square(g)
    m_hat = m / (1.0 - ADAM_B1 ** ADAM_STEP)
    v_hat = v / (1.0 - ADAM_B2 ** ADAM_STEP)
    return -ADAM_LR * (m_hat / (jnp.sqrt(v_hat) + ADAM_EPS) + ADAM_WD * w), m, v


def _sum_slots(recv, name, comm=()):
    _, rows, cols = recv.shape
    tr = _row_tile(rows, cols)

    def body(r_ref, o_ref):
        acc = r_ref[0].astype(F32)
        for s in range(1, N_CHIPS):
            acc = acc + r_ref[s].astype(F32)
        o_ref[...] = acc

    return _pallas(body, [recv], name=name, grid=(rows // tr,),
                   in_specs=[pl.BlockSpec((N_CHIPS, tr, cols), lambda i: (0, i, 0))],
                   out_specs=[pl.BlockSpec((tr, cols), lambda i: (i, 0))],
                   out_shape=[jax.ShapeDtypeStruct((rows, cols), F32)], comm=comm)[0]


def _add_halves(mine, theirs, name):
    slots, rows, cols = mine.shape
    tr = _row_tile(rows, cols * slots)

    def body(a_ref, b_ref, o_ref):
        o_ref[...] = (a_ref[...].astype(F32) + b_ref[...].astype(F32)).astype(o_ref.dtype)

    spec = pl.BlockSpec((slots, tr, cols), lambda i: (0, i, 0))
    return _pallas(body, [mine, theirs], name=name, grid=(rows // tr,), in_specs=[spec, spec], out_specs=[spec],
                   out_shape=[jax.ShapeDtypeStruct(mine.shape, mine.dtype)])[0]


def _adamw_sharded(w, parts, m, v, name, comm=()):
    rows, cols = w.shape
    tr = _row_tile(rows, cols)
    n = len(parts)

    def body(w_ref, *refs):
        p_refs, (m_ref, v_ref, g_ref, d_ref, nm_ref, nv_ref) = refs[:n], refs[n:]
        g = p_refs[0][...]
        for p_ref in p_refs[1:]:
            g = g + p_ref[...]
        g_ref[...] = g
        d_ref[...], nm_ref[...], nv_ref[...] = _adamw(w_ref[...], g, m_ref[...], v_ref[...])

    spec = pl.BlockSpec((tr, cols), lambda i: (i, 0))
    shape = jax.ShapeDtypeStruct((rows, cols), F32)
    return _pallas(body, [w, *parts, m, v], name=name, grid=(rows // tr,), in_specs=[spec] * (n + 3),
                   out_specs=[spec] * 4, out_shape=[shape] * 4, comm=comm)


def _adamw_small(w, gathered, m, v):
    rows, cols = w.shape

    def body(w_ref, a_ref, m_ref, v_ref, g_ref, d_ref, nm_ref, nv_ref):
        g = a_ref[0:rows, :]
        for dev in range(1, N_DEV):
            g = g + a_ref[dev * rows:(dev + 1) * rows, :]
        g_ref[...] = g
        d_ref[...], nm_ref[...], nv_ref[...] = _adamw(w_ref[...], g, m_ref[...], v_ref[...])

    shape = jax.ShapeDtypeStruct((rows, cols), F32)
    return pl.pallas_call(
        body, name="adamw_small", out_shape=[shape] * 4,
        compiler_params=pltpu.CompilerParams(vmem_limit_bytes=VMEM_LIMIT))(w, gathered, m, v)


PACK_TILE = 8 * LANES


def _pack_small(vals, last_row):
    rows = []
    for name in SMALL:
        flat = vals[name].reshape(-1).astype(F32)
        rows.append(jnp.pad(flat, (0, -flat.size % PACK_TILE)).reshape(-1, LANES))
    rows.append(jnp.pad(last_row, ((0, 7), (0, 0))))
    return jnp.concatenate(rows, axis=0)


def _unpack_small(packed, like):
    out, r = {}, 0
    for name in SMALL:
        size = like[name].size
        out[name] = packed[r:r + -(-size // LANES)].reshape(-1)[:size].reshape(like[name].shape)
        r += 8 * -(-size // PACK_TILE)
    return out, packed[r, 0]


def _shard2d(name, v):
    v = v.reshape(v.shape[-2:])
    return v.T if name in FFN_T else v


def _unshard(name, v, shape):
    return (v.T if name in FFN_T else v).reshape(shape)


def _train_step(x3, tgt3, wts, ms, vs):
    x, tgt = x3[0], tgt3[0]
    L, D = x.shape
    row = lambda v: v.reshape(1, -1)
    shards = {k: _shard2d(k, wts[k]) for k in SHARDED}
    sends = {k: shards[k] if k == "conv_w_dw" else _cast_bf16(shards[k], "cast_" + k) for k in SHARDED}
    gat = {k: _Gather(sends[k]) for k in SHARDED}
    w = lambda k: gat[k].result[0]

    s = {k: wts[k] for k in SMALL}
    lr, li = s["s5_lam_re"].reshape(1, S5_N), s["s5_lam_im"].reshape(1, S5_N)
    ldt = jnp.repeat(s["s5_log_dt"].reshape(S5_G), S5_P).reshape(1, S5_N)
    brc, bic = _compact_b(s["s5_b_re"].reshape(S5_G, S5_P, S5_GC)), _compact_b(s["s5_b_im"].reshape(S5_G, S5_P, S5_GC))
    crc = _compact_b(s["s5_c_re"].reshape(S5_G, S5_GC, S5_P).transpose(0, 2, 1)).T
    cic = _compact_b(s["s5_c_im"].reshape(S5_G, S5_GC, S5_P).transpose(0, 2, 1)).T
    d_skip, b_glu = row(s["s5_d"]), row(s["s5_b_glu"])
    b_dw, ln_g, ln_b = row(s["conv_b_dw"]), row(s["conv_ln_g"]), row(s["conv_ln_b"])
    g1, gm, g2, gf = row(s["ffn1_norm"]), row(s["mix_norm"]), row(s["ffn2_norm"]), row(s["final_norm"])
    mavg = jnp.asarray(_HEAD_MEAN, dtype=BF16)

    h1 = _rms_fwd(x, g1, "rms1", comm=[gat["ffn1_w_gate"], gat["ffn1_w_up"]])
    a1, b1, act1 = _ffn_up(h1, w("ffn1_w_gate"), w("ffn1_w_up"), "ffn1_up", comm=[gat["ffn1_w_down"]])
    x1, h2 = _ffn_down(act1, w("ffn1_w_down"), x, gm, "ffn1_down", comm=[gat["w_in"], gat["s5_w_glu"]])
    u = _mm_grouped(h2, w("w_in"), "in_proj", comm=[gat["conv_w_dw"], gat["w_out"]])
    are, aim, bre, bim, cre, cim = _s5_params_fwd(lr, li, ldt, brc, bic, crc, cic)
    are_t, aim_t = are.reshape(S5_TILES, LANES), aim.reshape(S5_TILES, LANES)
    w_glu = w("s5_w_glu").reshape(S5_W, S5_W)
    *states, y_pre, o_s5 = _s5_fwd(u, are_t, aim_t, bre, bim, cre, cim, d_skip, w_glu, b_glu, comm=[gat["ffn2_w_gate"]])
    w_dw = w("conv_w_dw").transpose(1, 0, 2).reshape(CONV_K, CONV_W)
    zc, cat = _conv_fwd(u, o_s5, w_dw, b_dw, ln_g, ln_b, mavg, comm=[gat["ffn2_w_up"]])
    w_out = w("w_out").reshape(-1, D)
    x2, h3 = _mix_out(cat, w_out, x1, g2, "mix_out")
    a2, b2, act2 = _ffn_up(h3, w("ffn2_w_gate"), w("ffn2_w_up"), "ffn2_up", comm=[gat["ffn2_w_down"]])
    dx3, dx3b, loss_part, d_gf = _ffn_down_loss(act2, w("ffn2_w_down"), x2, gf, tgt, "ffn2_down_loss")

    gs, sc, waiting = {"final_norm": d_gf}, {}, []

    def grad(key, g):
        if g.shape[1] % (2 * BF16_ROWS) == 0 and g.dtype == BF16:
            waiting.append(_SwapHalf(g, key))
        else:
            sc[key] = _Scatter(g)
            waiting.append(sc[key])

    def carry(call, *args, **kw):
        ops = list(waiting)
        waiting.clear()
        res = call(*args, comm=ops, **kw)
        for op in ops:
            if isinstance(op, _SwapHalf):
                sc[op.key] = _Scatter(_add_halves(*op.result, "add_" + op.key))
                waiting.append(sc[op.key])
        return res

    da2, db2 = _ffn_bwd_act(dx3b, w("ffn2_w_down"), a2, b2, "ffn2_bwd_act")
    grad("ffn2_w_down", _mm_tn(act2, dx3b[None], 0.5, "ffn2_dwd", N_CHIPS))
    grad("ffn2_w_gate", carry(_mm_tn, da2, h3[None], 1.0, "ffn2_dwg", N_CHIPS))
    grad("ffn2_w_up", carry(_mm_tn, db2, h3[None], 1.0, "ffn2_dwu", N_CHIPS))
    dx2, dx2b, gs["ffn2_norm"] = carry(_mm_rmsbwd, [da2, db2], [w("ffn2_w_gate"), w("ffn2_w_up")], False, False, x2, g2,
                                       dx3, "ffn2_bwd_dx")

    dm = _mm_nt(dx2b, w_out, "mix_bwd")
    grad("w_out", _mm_tn(cat[None], dx2b[None], 1.0, "dwout", 1).reshape(N_CHIPS, -1, D))
    (du_s5, d_wglu, gs["s5_b_glu"], gs["s5_d"], d_crc, d_cic, d_bre, d_bim, d_are, d_aim) = carry(
        _s5_bwd, dm, y_pre, u, states, are_t, aim_t, bre, bim, cre, cim, d_skip, w_glu, b_glu)
    grad("s5_w_glu", d_wglu.astype(BF16).reshape(N_CHIPS, -1, S5_W))
    g_lr, g_li, g_ldt, g_brc, g_bic = _s5_params_bwd(lr, li, ldt, brc, bic, d_are.reshape(1, S5_N),
                                                     d_aim.reshape(1, S5_N), d_bre, d_bim)
    gs["s5_lam_re"], gs["s5_lam_im"] = g_lr, g_li
    gs["s5_log_dt"] = jnp.sum(g_ldt.reshape(S5_G, S5_P), axis=1)
    gs["s5_b_re"], gs["s5_b_im"] = _uncompact_b(g_brc), _uncompact_b(g_bic)
    gs["s5_c_re"] = _uncompact_b(d_crc.T).transpose(0, 2, 1)
    gs["s5_c_im"] = _uncompact_b(d_cic.T).transpose(0, 2, 1)
    du, d_wdw, gs["conv_b_dw"], gs["conv_ln_g"], gs["conv_ln_b"] = carry(_conv_bwd, dm, zc, u, du_s5, w_dw, ln_g, ln_b, mavg)
    d_wdw = jnp.sum(d_wdw.reshape(HALO, 8, CONV_W), axis=1)[:CONV_K]
    grad("conv_w_dw", d_wdw.reshape(CONV_K, N_CHIPS, -1).transpose(1, 0, 2))
    grad("w_in", carry(_mm_tn, h2[None], du, 1.0, "dwin", N_CHIPS, b_cols=True))
    dx1, dx1b, gs["mix_norm"] = carry(_mm_rmsbwd, [du], [w("w_in")], True, True, x1, gm, dx2, "in_proj_bwd")

    da1, db1 = carry(_ffn_bwd_act, dx1b, w("ffn1_w_down"), a1, b1, "ffn1_bwd_act")
    grad("ffn1_w_down", _mm_tn(act1, dx1b[None], 0.5, "ffn1_dwd", N_CHIPS))
    grad("ffn1_w_gate", carry(_mm_tn, da1, h1[None], 1.0, "ffn1_dwg", N_CHIPS))
    grad("ffn1_w_up", carry(_mm_tn, db1, h1[None], 1.0, "ffn1_dwu", N_CHIPS))
    grad_x, _, gs["ffn1_norm"] = carry(_mm_rmsbwd, [da1, db1], [w("ffn1_w_gate"), w("ffn1_w_up")], False, False, x, g1,
                                       dx1, "ffn1_bwd_dx")

    out = {}
    gsmall = {k: gs[k].reshape(wts[k].shape) for k in SMALL}
    zero_row = jnp.zeros((1, LANES), F32)
    g_all = carry(_gather_all, _pack_small(gsmall, loss_part))
    res = _adamw_small(_pack_small(s, zero_row), g_all, _pack_small({k: ms[k] for k in SMALL}, zero_row),
                       _pack_small({k: vs[k] for k in SMALL}, zero_row))
    unpacked = [_unpack_small(r, s) for r in res]
    loss = unpacked[0][1]
    for k in SMALL:
        out[k] = [u_[0][k] for u_ in unpacked]

    order = ("ffn2_w_down", "ffn2_w_gate", "ffn2_w_up", "w_out", "s5_w_glu", "conv_w_dw", "w_in", "ffn1_w_down",
             "ffn1_w_gate", "ffn1_w_up")
    back = {}
    for k in order:
        part = carry(_sum_slots, sc[k].result[0], "sum_" + k)
        back[k] = _SwapBack(part) if part.shape != shards[k].shape else _Swap(part)
        waiting.append(back[k])
    for k in order:
        parts = [back[k].result[0]] if isinstance(back[k], _SwapBack) else [back[k].ins[0], back[k].result[0]]
        res = carry(_adamw_sharded, shards[k], parts, _shard2d(k, ms[k]), _shard2d(k, vs[k]), "adamw_" + k)
        out[k] = [_unshard(k, r, wts[k].shape) for r in res]
    return loss, grad_x[None], out


def kernel(x, ffn1_norm, ffn1_w_gate, ffn1_w_up, ffn1_w_down, mix_norm, w_in, s5_lam_re, s5_lam_im, s5_log_dt, s5_b_re, s5_b_im, s5_c_re, s5_c_im, s5_d, s5_w_glu, s5_b_glu, conv_w_dw, conv_b_dw, conv_ln_g, conv_ln_b, w_out, ffn2_norm, ffn2_w_gate, ffn2_w_up, ffn2_w_down, final_norm, loss_target, m_ffn1_norm, m_ffn1_w_gate, m_ffn1_w_up, m_ffn1_w_down, m_mix_norm, m_w_in, m_s5_lam_re, m_s5_lam_im, m_s5_log_dt, m_s5_b_re, m_s5_b_im, m_s5_c_re, m_s5_c_im, m_s5_d, m_s5_w_glu, m_s5_b_glu, m_conv_w_dw, m_conv_b_dw, m_conv_ln_g, m_conv_ln_b, m_w_out, m_ffn2_norm, m_ffn2_w_gate, m_ffn2_w_up, m_ffn2_w_down, m_final_norm, v_ffn1_norm, v_ffn1_w_gate, v_ffn1_w_up, v_ffn1_w_down, v_mix_norm, v_w_in, v_s5_lam_re, v_s5_lam_im, v_s5_log_dt, v_s5_b_re, v_s5_b_im, v_s5_c_re, v_s5_c_im, v_s5_d, v_s5_w_glu, v_s5_b_glu, v_conv_w_dw, v_conv_b_dw, v_conv_ln_g, v_conv_ln_b, v_w_out, v_ffn2_norm, v_ffn2_w_gate, v_ffn2_w_up, v_ffn2_w_down, v_final_norm):
    given = dict(locals())
    wts = {k: given[k] for k in WEIGHTS}
    ms = {k: given["m_" + k] for k in WEIGHTS}
    vs = {k: given["v_" + k] for k in WEIGHTS}
    loss, grad_x, out = _train_step(x, loss_target, wts, ms, vs)
    return (loss, grad_x, *[out[k][0] for k in WEIGHTS], *[out[k][1] for k in WEIGHTS],
            *[out[k][2] for k in WEIGHTS], *[out[k][3] for k in WEIGHTS])
```

```python
import functools

import jax
import jax.numpy as jnp
import numpy as np
from jax import lax
from jax.experimental import pallas as pl
from jax.experimental.pallas import tpu as pltpu

F32, BF16 = jnp.float32, jnp.bfloat16
MESH = pl.DeviceIdType.MESH

EPS = 1e-6
ADAM_LR, ADAM_B1, ADAM_B2, ADAM_EPS, ADAM_WD, ADAM_STEP = 0.001, 0.9, 0.999, 1e-08, 0.01, 10

N_CHIPS = 4
N_DEV = 8
LANES = 128
BF16_ROWS = 16
S5_W, S5_G, S5_GC, S5_P = 512, 32, 16, 64
S5_N = S5_G * S5_P
S5_TILES = S5_N // LANES
S5_HALF = 8
CONV_W, CONV_K, CONV_HD = 512, 31, 64
HALO = 32
CONV_SB = 32
TM = 512
TMS = 1024
TK = 2048
TS = 256
VMEM_LIMIT = 48 << 20
GELU_C0, GELU_C1 = 0.7978845608028654, 0.044715

FFN_T = ("ffn1_w_gate", "ffn1_w_up", "ffn2_w_gate", "ffn2_w_up")
SHARDED = ("ffn1_w_gate", "ffn1_w_up", "ffn1_w_down", "w_in", "s5_w_glu", "conv_w_dw", "w_out",
           "ffn2_w_gate", "ffn2_w_up", "ffn2_w_down")
SMALL = ("ffn1_norm", "mix_norm", "s5_lam_re", "s5_lam_im", "s5_log_dt", "s5_b_re", "s5_b_im", "s5_c_re",
         "s5_c_im", "s5_d", "s5_b_glu", "conv_b_dw", "conv_ln_g", "conv_ln_b", "ffn2_norm", "final_norm")
WEIGHTS = ("ffn1_norm", "ffn1_w_gate", "ffn1_w_up", "ffn1_w_down", "mix_norm", "w_in", "s5_lam_re", "s5_lam_im",
           "s5_log_dt", "s5_b_re", "s5_b_im", "s5_c_re", "s5_c_im", "s5_d", "s5_w_glu", "s5_b_glu", "conv_w_dw",
           "conv_b_dw", "conv_ln_g", "conv_ln_b", "w_out", "ffn2_norm", "ffn2_w_gate", "ffn2_w_up", "ffn2_w_down",
           "final_norm")


def _dot(a, b):
    return jnp.dot(a, b, preferred_element_type=F32)


def _dot_nt(a, b):
    return lax.dot_general(a, b, (((1,), (1,)), ((), ())), preferred_element_type=F32)


def _dot_tn(a, b):
    return lax.dot_general(a, b, (((0,), (0,)), ((), ())), preferred_element_type=F32)


def _colsum(v):
    return jnp.sum(v, axis=0, keepdims=True)


def _sigmoid(v):
    return 1.0 / (1.0 + jnp.exp(-v))


def _accumulate(ref, first, value):
    @pl.when(first)
    def _():
        ref[...] = value

    @pl.when(jnp.logical_not(first))
    def _():
        ref[...] += value


def _position():
    x, y, c = lax.axis_index("x"), lax.axis_index("y"), lax.axis_index("c")
    return x, y, c, [(1 - x, y), (x, 1 - y), (1 - x, 1 - y)]


def _remote(src, dst, sems, send, recv, device):
    return pltpu.make_async_remote_copy(src_ref=src, dst_ref=dst, send_sem=sems.at[send], recv_sem=sems.at[recv],
                                        device_id=device, device_id_type=MESH)


class _Gather:
    def __init__(self, shard):
        self.ins = [shard]
        self.outs = [jax.ShapeDtypeStruct((N_CHIPS,) + shard.shape, shard.dtype)]
        self.rows = shard.shape[0]
        self.halve = shard.dtype == BF16 and self.rows % (2 * BF16_ROWS) == 0
        self.n_sem = 13 if self.halve else 7
        self.result = None

    def _copies(self, ins, outs, sems, s0, pos):
        x, y, c, chips = pos
        src, dst = ins[0], outs[0]
        me = 2 * x + y
        if self.halve:
            hr = self.rows // 2
            mine, theirs = pl.ds(c * hr, hr), pl.ds((1 - c) * hr, hr)
            part = lambda slot, rows: dst.at[slot, rows]
            my_src = src.at[mine]
        else:
            mine = theirs = None
            part = lambda slot, rows: dst.at[slot]
            my_src = src
        slot = lambda j: 2 * chips[j][0] + chips[j][1]
        local = lambda: pltpu.make_async_copy(src, dst.at[me], sems.at[s0])
        send = lambda j: _remote(my_src, part(me, mine), sems, s0 + 1 + j, s0 + 4 + j, (*chips[j], c))
        land = lambda j: _remote(my_src, part(slot(j), mine), sems, s0 + 1 + j, s0 + 4 + j, (*chips[j], c))
        fwd = lambda j: _remote(part(slot(j), mine), part(slot(j), mine), sems, s0 + 7 + j, s0 + 10 + j, (x, y, 1 - c))
        got = lambda j: _remote(part(slot(j), theirs), part(slot(j), theirs), sems, s0 + 7 + j, s0 + 10 + j,
                                (x, y, 1 - c))
        return local, send, land, fwd, got

    def start(self, ins, outs, sems, s0, pos):
        local, send, _, _, _ = self._copies(ins, outs, sems, s0, pos)
        local().start()
        for j in range(N_CHIPS - 1):
            send(j).start()

    def finish(self, ins, outs, sems, s0, pos):
        local, send, land, fwd, got = self._copies(ins, outs, sems, s0, pos)
        others = range(N_CHIPS - 1)
        for j in others:
            land(j).wait_recv()
            if self.halve:
                fwd(j).start()
        for j in others:
            if self.halve:
                got(j).wait_recv()
        for j in others:
            send(j).wait_send()
            if self.halve:
                fwd(j).wait_send()
        local().wait()


class _Scatter:
    def __init__(self, grad):
        self.ins = [grad]
        self.outs = [jax.ShapeDtypeStruct(grad.shape, grad.dtype)]
        self.n_sem = 7
        self.result = None

    def _copies(self, ins, outs, sems, s0, pos):
        x, y, c, chips = pos
        src, dst = ins[0], outs[0]
        me = 2 * x + y
        slot = lambda j: 2 * chips[j][0] + chips[j][1]
        local = lambda: pltpu.make_async_copy(src.at[me], dst.at[me], sems.at[s0])
        send = lambda j: _remote(src.at[slot(j)], dst.at[me], sems, s0 + 1 + j, s0 + 4 + j, (*chips[j], c))
        land = lambda j: _remote(src.at[me], dst.at[slot(j)], sems, s0 + 1 + j, s0 + 4 + j, (*chips[j], c))
        return local, send, land

    def start(self, ins, outs, sems, s0, pos):
        local, send, _ = self._copies(ins, outs, sems, s0, pos)
        local().start()
        for j in range(N_CHIPS - 1):
            send(j).start()

    def finish(self, ins, outs, sems, s0, pos):
        local, send, land = self._copies(ins, outs, sems, s0, pos)
        for j in range(N_CHIPS - 1):
            land(j).wait_recv()
        for j in range(N_CHIPS - 1):
            send(j).wait_send()
        local().wait()


class _Swap:
    def __init__(self, part):
        self.ins = [part]
        self.outs = [jax.ShapeDtypeStruct(part.shape, part.dtype)]
        self.n_sem = 2
        self.result = None

    def _copy(self, ins, outs, sems, s0, pos):
        x, y, c, _ = pos
        return _remote(ins[0], outs[0], sems, s0, s0 + 1, (x, y, 1 - c))

    def start(self, ins, outs, sems, s0, pos):
        self._copy(ins, outs, sems, s0, pos).start()

    def finish(self, ins, outs, sems, s0, pos):
        self._copy(ins, outs, sems, s0, pos).wait()


class _SwapHalf:
    def __init__(self, grad, key):
        slots, rows, cols = grad.shape
        half = jax.ShapeDtypeStruct((slots, rows // 2, cols), grad.dtype)
        self.ins, self.outs, self.key = [grad], [half, half], key
        self.hr = rows // 2
        self.n_sem = 3
        self.result = None

    def _copies(self, ins, outs, sems, s0, pos):
        x, y, c, _ = pos
        mine, theirs = pl.ds(c * self.hr, self.hr), pl.ds((1 - c) * self.hr, self.hr)
        local = pltpu.make_async_copy(ins[0].at[:, mine], outs[0], sems.at[s0])
        remote = _remote(ins[0].at[:, theirs], outs[1], sems, s0 + 1, s0 + 2, (x, y, 1 - c))
        return local, remote

    def start(self, ins, outs, sems, s0, pos):
        for cp in self._copies(ins, outs, sems, s0, pos):
            cp.start()

    def finish(self, ins, outs, sems, s0, pos):
        for cp in self._copies(ins, outs, sems, s0, pos):
            cp.wait()


class _SwapBack:
    def __init__(self, part):
        hr, cols = part.shape
        self.ins, self.outs = [part], [jax.ShapeDtypeStruct((2 * hr, cols), part.dtype)]
        self.hr = hr
        self.n_sem = 3
        self.result = None

    def _copies(self, ins, outs, sems, s0, pos):
        x, y, c, _ = pos
        mine, theirs = pl.ds(c * self.hr, self.hr), pl.ds((1 - c) * self.hr, self.hr)
        local = lambda: pltpu.make_async_copy(ins[0], outs[0].at[mine], sems.at[s0])
        send = lambda: _remote(ins[0], outs[0].at[mine], sems, s0 + 1, s0 + 2, (x, y, 1 - c))
        land = lambda: _remote(ins[0], outs[0].at[theirs], sems, s0 + 1, s0 + 2, (x, y, 1 - c))
        return local, send, land

    def start(self, ins, outs, sems, s0, pos):
        local, send, _ = self._copies(ins, outs, sems, s0, pos)
        local().start()
        send().start()

    def finish(self, ins, outs, sems, s0, pos):
        local, send, land = self._copies(ins, outs, sems, s0, pos)
        land().wait_recv()
        send().wait_send()
        local().wait()


def _pallas(body, args, *, name, grid, in_specs, out_specs, out_shape, scratch_shapes=(), comm=()):
    comm = list(comm)
    n_in, n_out, n_scr = len(in_specs), len(out_specs), len(scratch_shapes)
    c_in = [a for op in comm for a in op.ins]
    c_out = [s for op in comm for s in op.outs]
    n_sem = sum(op.n_sem for op in comm)

    def full(*refs):
        o0 = n_in + len(c_in)
        s0 = o0 + n_out + len(c_out)
        ins, cin = refs[:n_in], refs[n_in:o0]
        outs, cout = refs[o0:o0 + n_out], refs[o0 + n_out:s0]
        scratch = refs[s0:s0 + n_scr]
        if comm:
            sems = refs[s0 + n_scr]
            ids = [pl.program_id(d) for d in range(len(grid))]
            first = functools.reduce(jnp.logical_and, [i == 0 for i in ids])
            last = functools.reduce(jnp.logical_and, [i == g - 1 for i, g in zip(ids, grid)])
            pos = _position()

            def each(step):
                ci = co = cs = 0
                for op in comm:
                    getattr(op, step)(cin[ci:ci + len(op.ins)], cout[co:co + len(op.outs)], sems, cs, pos)
                    ci, co, cs = ci + len(op.ins), co + len(op.outs), cs + op.n_sem

            @pl.when(first)
            def _():
                each("start")

        body(*ins, *outs, *scratch)
        if comm:
            @pl.when(last)
            def _():
                each("finish")

    hbm = pl.BlockSpec(memory_space=pl.ANY)
    res = pl.pallas_call(
        full, name=name, grid=grid,
        in_specs=list(in_specs) + [hbm] * len(c_in), out_specs=list(out_specs) + [hbm] * len(c_out),
        out_shape=list(out_shape) + c_out,
        scratch_shapes=list(scratch_shapes) + ([pltpu.SemaphoreType.DMA((n_sem,))] if comm else []),
        compiler_params=pltpu.CompilerParams(dimension_semantics=("arbitrary",) * len(grid),
                                             vmem_limit_bytes=VMEM_LIMIT))(*args, *c_in)
    k = n_out
    for op in comm:
        op.result = list(res[k:k + len(op.outs)])
        k += len(op.outs)
    return list(res[:n_out])


def _row_tile(rows, cols, itemsize=4, budget=1 << 20):
    t = rows
    while t % (2 * BF16_ROWS) == 0 and t * cols * itemsize > budget:
        t //= 2
    return t


def _cast_bf16(w, name):
    rows, cols = w.shape
    tr = _row_tile(rows, cols)

    def body(w_ref, o_ref):
        o_ref[...] = w_ref[...].astype(BF16)

    spec = pl.BlockSpec((tr, cols), lambda i: (i, 0))
    return _pallas(body, [w], name=name, grid=(rows // tr,), in_specs=[spec], out_specs=[spec],
                   out_shape=[jax.ShapeDtypeStruct((rows, cols), BF16)])[0]


def _rms_fwd(x, g, name, comm=()):
    L, D = x.shape

    def body(x_ref, g_ref, h_ref):
        xf = x_ref[...]
        r = lax.rsqrt(jnp.mean(xf * xf, axis=-1, keepdims=True) + EPS)
        h_ref[...] = (xf * r * g_ref[...]).astype(BF16)

    row = pl.BlockSpec((TMS, D), lambda i: (i, 0))
    return _pallas(body, [x, g], name=name, grid=(L // TMS,),
                   in_specs=[row, pl.BlockSpec((1, D), lambda i: (0, 0))], out_specs=[row],
                   out_shape=[jax.ShapeDtypeStruct((L, D), BF16)], comm=comm)[0]


def _resident(shape):
    return pl.BlockSpec(shape, lambda *_: (0,) * len(shape), pipeline_mode=pl.Buffered(1))


def _ffn_up(h, wg_t, wu_t, name, comm=()):
    L, D = h.shape
    G, FS, _ = wg_t.shape

    def body(h_ref, wg_ref, wu_ref, a_ref, b_ref, act_ref):
        j = pl.program_id(1)
        hv = h_ref[...]
        a = _dot_nt(hv, wg_ref[j])
        b = _dot_nt(hv, wu_ref[j])
        a_ref[...] = a.astype(BF16)
        b_ref[...] = b.astype(BF16)
        act_ref[...] = (a * _sigmoid(a) * b).astype(BF16)

    ospec = pl.BlockSpec((None, TMS, FS), lambda i, j: (j, i, 0))
    oshape = jax.ShapeDtypeStruct((G, L, FS), BF16)
    return _pallas(body, [h, wg_t, wu_t], name=name, grid=(L // TMS, G),
                   in_specs=[pl.BlockSpec((TMS, D), lambda i, j: (i, 0)), _resident((G, FS, D)), _resident((G, FS, D))],
                   out_specs=[ospec, ospec, ospec], out_shape=[oshape, oshape, oshape], comm=comm)


def _group_sum(a_ref, w_ref, groups, mm=_dot):
    acc = mm(a_ref[0], w_ref[0])
    for j in range(1, groups):
        acc = acc + mm(a_ref[j], w_ref[j])
    return acc


def _ffn_down(act, wd, x, g_next, name, comm=()):
    G, L, FS = act.shape
    D = wd.shape[2]

    def body(act_ref, wd_ref, x_ref, g_ref, xn_ref, hn_ref):
        xn = x_ref[...] + 0.5 * _group_sum(act_ref, wd_ref, G)
        xn_ref[...] = xn
        r = lax.rsqrt(jnp.mean(xn * xn, axis=-1, keepdims=True) + EPS)
        hn_ref[...] = (xn * r * g_ref[...]).astype(BF16)

    row = pl.BlockSpec((TM, D), lambda i: (i, 0))
    return _pallas(body, [act, wd, x, g_next], name=name, grid=(L // TM,),
                   in_specs=[pl.BlockSpec((G, TM, FS), lambda i: (0, i, 0)), _resident((G, FS, D)), row,
                             pl.BlockSpec((1, D), lambda i: (0, 0))],
                   out_specs=[row, row],
                   out_shape=[jax.ShapeDtypeStruct((L, D), F32), jax.ShapeDtypeStruct((L, D), BF16)], comm=comm)


def _ffn_down_loss(act, wd, x, gf, tgt, name):
    G, L, FS = act.shape
    D = wd.shape[2]

    def body(act_ref, wd_ref, x_ref, g_ref, t_ref, dx_ref, dxb_ref, loss_ref, dg_ref):
        i = pl.program_id(0)
        xn = x_ref[...] + 0.5 * _group_sum(act_ref, wd_ref, G)
        r = lax.rsqrt(jnp.mean(xn * xn, axis=-1, keepdims=True) + EPS)
        xh = xn * r
        gv = g_ref[...]
        e = xh * gv - t_ref[...]
        part = 0.5 * jnp.sum(_colsum(e * e), axis=1, keepdims=True) / D
        dy = e / D
        _accumulate(loss_ref, i == 0, jnp.broadcast_to(part, (1, LANES)))
        _accumulate(dg_ref, i == 0, _colsum(dy * xh))
        dxh = dy * gv
        dx = r * (dxh - xh * jnp.mean(dxh * xh, axis=-1, keepdims=True))
        dx_ref[...] = dx
        dxb_ref[...] = dx.astype(BF16)

    row = pl.BlockSpec((TM, D), lambda i: (i, 0))
    return _pallas(body, [act, wd, x, gf, tgt], name=name, grid=(L // TM,),
                   in_specs=[pl.BlockSpec((G, TM, FS), lambda i: (0, i, 0)), _resident((G, FS, D)), row,
                             pl.BlockSpec((1, D), lambda i: (0, 0)), row],
                   out_specs=[row, row, pl.BlockSpec((1, LANES), lambda i: (0, 0)),
                              pl.BlockSpec((1, D), lambda i: (0, 0))],
                   out_shape=[jax.ShapeDtypeStruct((L, D), F32), jax.ShapeDtypeStruct((L, D), BF16),
                              jax.ShapeDtypeStruct((1, LANES), F32), jax.ShapeDtypeStruct((1, D), F32)])


def _ffn_bwd_act(dxb, wd, a, b, name, comm=()):
    L, D = dxb.shape
    G, FS, _ = wd.shape

    def body(dx_ref, wd_ref, a_ref, b_ref, da_ref, db_ref):
        dact = 0.5 * _dot_nt(dx_ref[...], wd_ref[pl.program_id(1)])
        av = a_ref[...].astype(F32)
        bv = b_ref[...].astype(F32)
        sg = _sigmoid(av)
        da_ref[...] = (dact * bv * sg * (1.0 + av * (1.0 - sg))).astype(BF16)
        db_ref[...] = (dact * av * sg).astype(BF16)

    gspec = pl.BlockSpec((None, TMS, FS), lambda i, j: (j, i, 0))
    oshape = jax.ShapeDtypeStruct((G, L, FS), BF16)
    return _pallas(body, [dxb, wd, a, b], name=name, grid=(L // TMS, G),
                   in_specs=[pl.BlockSpec((TMS, D), lambda i, j: (i, 0)), _resident((G, FS, D)), gspec, gspec],
                   out_specs=[gspec, gspec], out_shape=[oshape, oshape], comm=comm)


def _mm_grouped(a, w, name, comm=()):
    L, K = a.shape
    G, _, N = w.shape

    def body(a_ref, w_ref, o_ref):
        o_ref[...] = _dot(a_ref[...], w_ref[pl.program_id(1)])

    return _pallas(body, [a, w], name=name, grid=(L // TMS, G),
                   in_specs=[pl.BlockSpec((TMS, K), lambda i, g: (i, 0)), _resident((G, K, N))],
                   out_specs=[pl.BlockSpec((TMS, N), lambda i, g: (i, g))],
                   out_shape=[jax.ShapeDtypeStruct((L, G * N), F32)], comm=comm)[0]


def _mm_nt(a, w, name):
    L, K = a.shape
    N = w.shape[0]

    def body(a_ref, w_ref, o_ref):
        o_ref[...] = _dot_nt(a_ref[...], w_ref[...]).astype(BF16)

    return _pallas(body, [a, w], name=name, grid=(L // TMS,),
                   in_specs=[pl.BlockSpec((TMS, K), lambda i: (i, 0)), _resident((N, K))],
                   out_specs=[pl.BlockSpec((TMS, N), lambda i: (i, 0))],
                   out_shape=[jax.ShapeDtypeStruct((L, N), BF16)])[0]


def _mm_tn(a, b, scale, name, groups, b_cols=False, comm=()):
    L, M = a.shape[1], a.shape[2]
    N = b.shape[1] // groups if b_cols else b.shape[2]
    tk = min(L, TK)
    nk = L // tk

    def spec(v, cols):
        if cols:
            return pl.BlockSpec((tk, v.shape[1] // groups), lambda g, k: (k, g))
        if v.shape[0] > 1:
            return pl.BlockSpec((None, tk, v.shape[2]), lambda g, k: (g, k, 0))
        return pl.BlockSpec((None, tk, v.shape[2]), lambda g, k: (0, k, 0))

    def body(a_ref, b_ref, o_ref, acc):
        k = pl.program_id(1)
        p = _dot_tn(a_ref[...], b_ref[...])
        if nk == 1:
            o_ref[...] = (p * scale).astype(BF16)
        else:
            _accumulate(acc, k == 0, p)

            @pl.when(k == nk - 1)
            def _():
                o_ref[...] = (acc[...] * scale).astype(BF16)

    return _pallas(body, [a, b], name=name, grid=(groups, nk),
                   in_specs=[spec(a, False), spec(b, b_cols)],
                   out_specs=[pl.BlockSpec((None, M, N), lambda g, k: (g, 0, 0))],
                   out_shape=[jax.ShapeDtypeStruct((groups, M, N), BF16)],
                   scratch_shapes=[pltpu.VMEM((M, N), F32)], comm=comm)[0]


def _mm_rmsbwd(a_list, w_list, nt, a_cols, x_in, g, dx_out, name, comm=()):
    P = len(a_list)
    G = w_list[0].shape[0]
    L, D = x_in.shape
    mm = _dot_nt if nt else _dot

    def body(*refs):
        a_refs, w_refs = refs[:P], refs[P:2 * P]
        x_ref, g_ref, dxo_ref, dx_ref, dxb_ref, dg_ref = refs[2 * P:]
        i = pl.program_id(0)
        dh = None
        for a_ref, w_ref in zip(a_refs, w_refs):
            for j in range(G):
                if a_cols:
                    kw = a_ref.shape[1] // G
                    term = mm(a_ref[:, j * kw:(j + 1) * kw], w_ref[j])
                else:
                    term = mm(a_ref[j], w_ref[j])
                dh = term if dh is None else dh + term
        xf = x_ref[...]
        r = lax.rsqrt(jnp.mean(xf * xf, axis=-1, keepdims=True) + EPS)
        xh = xf * r
        _accumulate(dg_ref, i == 0, _colsum(dh * xh))
        dxh = dh * g_ref[...]
        dx = dxo_ref[...] + r * (dxh - xh * jnp.mean(dxh * xh, axis=-1, keepdims=True))
        dx_ref[...] = dx
        dxb_ref[...] = dx.astype(BF16)

    row = pl.BlockSpec((TM, D), lambda i: (i, 0))
    vec = pl.BlockSpec((1, D), lambda i: (0, 0))
    if a_cols:
        a_specs = [pl.BlockSpec((TM, a.shape[1]), lambda i: (i, 0)) for a in a_list]
    else:
        a_specs = [pl.BlockSpec((G, TM, a.shape[2]), lambda i: (0, i, 0)) for a in a_list]
    w_specs = [_resident(w.shape) for w in w_list]
    return _pallas(body, [*a_list, *w_list, x_in, g, dx_out], name=name, grid=(L // TM,),
                   in_specs=a_specs + w_specs + [row, vec, row], out_specs=[row, row, vec],
                   out_shape=[jax.ShapeDtypeStruct((L, D), F32), jax.ShapeDtypeStruct((L, D), BF16),
                              jax.ShapeDtypeStruct((1, D), F32)], comm=comm)


def _mix_out(cat, wout, x1, g_next, name):
    L, K = cat.shape
    D = wout.shape[1]

    def body(c_ref, w_ref, x_ref, g_ref, xn_ref, hn_ref):
        xn = x_ref[...] + _dot(c_ref[...], w_ref[...])
        xn_ref[...] = xn
        r = lax.rsqrt(jnp.mean(xn * xn, axis=-1, keepdims=True) + EPS)
        hn_ref[...] = (xn * r * g_ref[...]).astype(BF16)

    row = pl.BlockSpec((TMS, D), lambda i: (i, 0))
    return _pallas(body, [cat, wout, x1, g_next], name=name, grid=(L // TMS,),
                   in_specs=[pl.BlockSpec((TMS, K), lambda i: (i, 0)), pl.BlockSpec((K, D), lambda i: (0, 0)), row,
                             pl.BlockSpec((1, D), lambda i: (0, 0))],
                   out_specs=[row, row],
                   out_shape=[jax.ShapeDtypeStruct((L, D), F32), jax.ShapeDtypeStruct((L, D), BF16)])


def _s5_disc(lr, li, ldt, brc, bic):
    dt = jnp.exp(ldt)
    mag = jnp.exp(lr * dt)
    are = mag * jnp.cos(li * dt)
    aim = mag * jnp.sin(li * dt)
    den = lr * lr + li * li
    nre = are - 1.0
    fre = (nre * lr + aim * li) / den
    fim = (aim * lr - nre * li) / den
    return are, aim, fre * brc - fim * bic, fre * bic + fim * brc


def _s5_params_fwd(lr, li, ldt, brc, bic, crc, cic):
    def body(lr_ref, li_ref, ldt_ref, br_ref, bi_ref, cr_ref, ci_ref, are_ref, aim_ref, bre_ref, bim_ref, cre_ref, cim_ref):
        are, aim, bre, bim = _s5_disc(lr_ref[...], li_ref[...], ldt_ref[...], br_ref[...], bi_ref[...])
        are_ref[...] = are
        aim_ref[...] = aim
        bre_ref[...] = bre.astype(BF16)
        bim_ref[...] = bim.astype(BF16)
        cre_ref[...] = cr_ref[...].astype(BF16)
        cim_ref[...] = ci_ref[...].astype(BF16)

    vec = jax.ShapeDtypeStruct((1, S5_N), F32)
    return pl.pallas_call(
        body, name="s5_params_fwd",
        out_shape=[vec, vec, jax.ShapeDtypeStruct((LANES, S5_N), BF16), jax.ShapeDtypeStruct((LANES, S5_N), BF16),
                   jax.ShapeDtypeStruct((S5_N, LANES), BF16), jax.ShapeDtypeStruct((S5_N, LANES), BF16)],
        compiler_params=pltpu.CompilerParams(vmem_limit_bytes=VMEM_LIMIT))(lr, li, ldt, brc, bic, crc, cic)


def _s5_params_bwd(lr, li, ldt, brc, bic, dare, daim, dbre, dbim):
    def body(lr_ref, li_ref, ldt_ref, br_ref, bi_ref, dare_ref, daim_ref, dbre_ref, dbim_ref,
             glr_ref, gli_ref, gldt_ref, gbr_ref, gbi_ref):
        _, vjp = jax.vjp(_s5_disc, lr_ref[...], li_ref[...], ldt_ref[...], br_ref[...], bi_ref[...])
        glr, gli, gldt, gbr, gbi = vjp((dare_ref[...], daim_ref[...], dbre_ref[...], dbim_ref[...]))
        glr_ref[...] = glr
        gli_ref[...] = gli
        gldt_ref[...] = gldt
        gbr_ref[...] = gbr
        gbi_ref[...] = gbi

    vec = jax.ShapeDtypeStruct((1, S5_N), F32)
    mat = jax.ShapeDtypeStruct((LANES, S5_N), F32)
    return pl.pallas_call(
        body, name="s5_params_bwd", out_shape=[vec, vec, vec, mat, mat],
        compiler_params=pltpu.CompilerParams(vmem_limit_bytes=VMEM_LIMIT))(lr, li, ldt, brc, bic, dare, daim, dbre, dbim)


def _gelu_parts(y):
    th = jnp.tanh(GELU_C0 * (y + GELU_C1 * y * y * y))
    return 0.5 * y * (1.0 + th), th


def _state_rows(q, T):
    return pl.ds(q % S5_HALF, T, stride=S5_HALF)


def _load_tiles(bufs, ct, T, dtype):
    return jnp.concatenate([bufs[q // S5_HALF][_state_rows(q, T), :].astype(dtype) for q in range(4 * ct, 4 * ct + 4)],
                           axis=1)


def _store_tiles(bufs, ct, T, value):
    for k, q in enumerate(range(4 * ct, 4 * ct + 4)):
        bufs[q // S5_HALF][_state_rows(q, T), :] = value[:, k * LANES:(k + 1) * LANES]


def _s5_fwd(u, are, aim, bre, bim, cre, cim, d_skip, wglu, bglu, comm=()):
    L = u.shape[0]
    T = min(TS, L)
    n = L // T

    def body(u_ref, are_ref, aim_ref, bre_ref, bim_ref, cre_ref, cim_ref, d_ref, wg_ref, bg_ref,
             sre_lo, sre_hi, sim_lo, sim_hi, y_ref, o_ref, st_re, st_im):
        i = pl.program_id(0)
        sre, sim = (sre_lo, sre_hi), (sim_lo, sim_hi)

        @pl.when(i == 0)
        def _():
            st_re[...] = jnp.zeros_like(st_re)
            st_im[...] = jnp.zeros_like(st_im)

        uf = u_ref[...]
        ub = uf.astype(BF16)
        for ct in range(4):
            uq = ub[:, ct * LANES:(ct + 1) * LANES]
            win = slice(4 * ct * LANES, 4 * (ct + 1) * LANES)
            _store_tiles(sre, ct, T, _dot(uq, bre_ref[:, win]))
            _store_tiles(sim, ct, T, _dot(uq, bim_ref[:, win]))
        halves = [slice(h * S5_HALF, (h + 1) * S5_HALF) for h in range(2)]
        a_re = [are_ref[hs, :] for hs in halves]
        a_im = [aim_ref[hs, :] for hs in halves]

        def step(t, carry):
            rows = pl.ds(pl.multiple_of(t * S5_HALF, S5_HALF), S5_HALF)
            out = []
            for h in range(2):
                s_re, s_im = carry[2 * h], carry[2 * h + 1]
                n_re = a_re[h] * s_re - a_im[h] * s_im + sre[h][rows, :]
                n_im = a_re[h] * s_im + a_im[h] * s_re + sim[h][rows, :]
                sre[h][rows, :] = n_re
                sim[h][rows, :] = n_im
                out += [n_re, n_im]
            return tuple(out)

        init = (st_re[halves[0], :], st_im[halves[0], :], st_re[halves[1], :], st_im[halves[1], :])
        fin = lax.fori_loop(0, T, step, init, unroll=4)
        for h in range(2):
            st_re[halves[h], :] = fin[2 * h]
            st_im[halves[h], :] = fin[2 * h + 1]
        tiles = []
        for ct in range(4):
            win = slice(4 * ct * LANES, 4 * (ct + 1) * LANES)
            tiles.append(_dot(_load_tiles(sre, ct, T, BF16), cre_ref[win, :])
                         - _dot(_load_tiles(sim, ct, T, BF16), cim_ref[win, :]))
        y = jnp.concatenate(tiles, axis=1) + d_ref[...] * uf
        y_ref[...] = y
        yg, _ = _gelu_parts(y)
        gate = _sigmoid(_dot(yg.astype(BF16), wg_ref[...]) + bg_ref[...])
        o_ref[...] = (yg * gate).astype(BF16)

    const = lambda shape: pl.BlockSpec(shape, lambda i: (0, 0))
    sspec = pl.BlockSpec((T * S5_HALF, LANES), lambda i: (i, 0))
    sshape = jax.ShapeDtypeStruct((L * S5_HALF, LANES), F32)
    chunk = pl.BlockSpec((T, S5_W), lambda i: (i, 0))
    return _pallas(body, [u, are, aim, bre, bim, cre, cim, d_skip, wglu, bglu], name="s5_fwd", grid=(n,),
                   in_specs=[chunk, const((S5_TILES, LANES)), const((S5_TILES, LANES)),
                             const((LANES, S5_N)), const((LANES, S5_N)), const((S5_N, LANES)), const((S5_N, LANES)),
                             const((1, S5_W)), const((S5_W, S5_W)), const((1, S5_W))],
                   out_specs=[sspec] * 4 + [chunk, chunk],
                   out_shape=[sshape] * 4 + [jax.ShapeDtypeStruct((L, S5_W), F32), jax.ShapeDtypeStruct((L, S5_W), BF16)],
                   scratch_shapes=[pltpu.VMEM((S5_TILES, LANES), F32), pltpu.VMEM((S5_TILES, LANES), F32)], comm=comm)


def _s5_bwd(dm, y_pre, u, states, are, aim, bre, bim, cre, cim, d_skip, wglu, bglu, comm=()):
    L = u.shape[0]
    T = min(TS, L)
    n = L // T

    def body(dm_ref, y_ref, u_ref, sre_lo, sre_hi, sim_lo, sim_hi, pre_lo, pre_hi, pim_lo, pim_hi,
             are_ref, aim_ref, bre_ref, bim_ref, cre_ref, cim_ref, d_ref, wg_ref, bg_ref,
             du_ref, dwg_ref, dbg_ref, dd_ref, dcre_ref, dcim_ref, dbre_ref, dbim_ref, dare_ref, daim_ref,
             gre_lo, gre_hi, gim_lo, gim_hi, car_re, car_im):
        i = pl.program_id(0)
        first = i == 0
        sre, sim = (sre_lo, sre_hi), (sim_lo, sim_hi)
        gre, gim = (gre_lo, gre_hi), (gim_lo, gim_hi)
        pre, pim = (pre_lo, pre_hi), (pim_lo, pim_hi)

        @pl.when(first)
        def _():
            car_re[...] = jnp.zeros_like(car_re)
            car_im[...] = jnp.zeros_like(car_im)
            dcre_ref[...] = jnp.zeros_like(dcre_ref)
            dcim_ref[...] = jnp.zeros_like(dcim_ref)
            dbre_ref[...] = jnp.zeros_like(dbre_ref)
            dbim_ref[...] = jnp.zeros_like(dbim_ref)

        y = y_ref[...]
        uf = u_ref[...]
        yg, th = _gelu_parts(y)
        dgelu = 0.5 * (1.0 + th) + 0.5 * y * (1.0 - th * th) * GELU_C0 * (1.0 + 3.0 * GELU_C1 * y * y)
        ygb = yg.astype(BF16)
        sg = _sigmoid(_dot(ygb, wg_ref[...]) + bg_ref[...])
        dout = dm_ref[...].astype(F32)
        dgp = dout * yg * sg * (1.0 - sg)
        dgpb = dgp.astype(BF16)
        dyg = dout * sg + _dot_nt(dgpb, wg_ref[...])
        _accumulate(dwg_ref, first, _dot_tn(ygb, dgpb))
        _accumulate(dbg_ref, first, _colsum(dgp))
        dy = dyg * dgelu
        _accumulate(dd_ref, first, _colsum(dy * uf))
        dyb = dy.astype(BF16)
        ub = uf.astype(BF16)

        for ct in range(4):
            win = slice(4 * ct * LANES, 4 * (ct + 1) * LANES)
            dyq = dyb[:, ct * LANES:(ct + 1) * LANES]
            dcre_ref[win, :] += _dot_tn(_load_tiles(sre, ct, T, BF16), dyq)
            dcim_ref[win, :] -= _dot_tn(_load_tiles(sim, ct, T, BF16), dyq)
            _store_tiles(gre, ct, T, _dot_nt(dyq, cre_ref[win, :]))
            _store_tiles(gim, ct, T, -_dot_nt(dyq, cim_ref[win, :]))

        halves = [slice(h * S5_HALF, (h + 1) * S5_HALF) for h in range(2)]
        a_re = [are_ref[hs, :] for hs in halves]
        a_im = [aim_ref[hs, :] for hs in halves]

        def adjoint(t, h, g_re, g_im):
            rows = pl.ds(pl.multiple_of(t * S5_HALF, S5_HALF), S5_HALF)
            n_re = gre[h][rows, :] + a_re[h] * g_re + a_im[h] * g_im
            n_im = gim[h][rows, :] + a_re[h] * g_im - a_im[h] * g_re
            gre[h][rows, :] = n_re
            gim[h][rows, :] = n_im
            return n_re, n_im

        def step(k, carry):
            out = []
            for h in range(2):
                out += adjoint(T - 1 - k, h, carry[2 * h], carry[2 * h + 1])
            return tuple(out)

        init = (car_re[halves[0], :], car_im[halves[0], :], car_re[halves[1], :], car_im[halves[1], :])
        fin = lax.fori_loop(0, T, step, init, unroll=4)
        keep = (i < n - 1).astype(F32)
        later, earlier = slice(S5_HALF, T * S5_HALF), slice(0, (T - 1) * S5_HALF)
        fold = lambda v: jnp.sum(v.reshape(T - 1, S5_HALF, LANES), axis=0)
        for h in range(2):
            g_re, g_im = fin[2 * h], fin[2 * h + 1]
            car_re[halves[h], :] = g_re
            car_im[halves[h], :] = g_im
            p_re, p_im = pre[h][...] * keep, pim[h][...] * keep
            da_re = fold(gre[h][later, :] * sre[h][earlier, :] + gim[h][later, :] * sim[h][earlier, :])
            da_im = fold(gim[h][later, :] * sre[h][earlier, :] - gre[h][later, :] * sim[h][earlier, :])
            da_re = da_re + g_re * p_re + g_im * p_im
            da_im = da_im + g_im * p_re - g_re * p_im

            @pl.when(first)
            def _():
                dare_ref[halves[h], :] = da_re
                daim_ref[halves[h], :] = da_im

            @pl.when(jnp.logical_not(first))
            def _():
                dare_ref[halves[h], :] += da_re
                daim_ref[halves[h], :] += da_im

        tiles = []
        for ct in range(4):
            uq = ub[:, ct * LANES:(ct + 1) * LANES]
            win = slice(4 * ct * LANES, 4 * (ct + 1) * LANES)
            g_re, g_im = _load_tiles(gre, ct, T, BF16), _load_tiles(gim, ct, T, BF16)
            tiles.append(d_ref[:, ct * LANES:(ct + 1) * LANES] * dy[:, ct * LANES:(ct + 1) * LANES]
                         + _dot_nt(g_re, bre_ref[:, win]) + _dot_nt(g_im, bim_ref[:, win]))
            dbre_ref[:, win] += _dot_tn(uq, g_re)
            dbim_ref[:, win] += _dot_tn(uq, g_im)
        du_ref[...] = jnp.concatenate(tiles, axis=1).astype(BF16)

    rev = lambda i: (n - 1 - i, 0)
    const = lambda shape: pl.BlockSpec(shape, lambda i: (0, 0))
    chunk = pl.BlockSpec((T, S5_W), rev)
    sspec = pl.BlockSpec((T * S5_HALF, LANES), rev)
    pspec = pl.BlockSpec((S5_HALF, LANES), lambda i: (jnp.maximum((n - 1 - i) * T - 1, 0), 0))
    tile = jax.ShapeDtypeStruct((S5_TILES, LANES), F32)
    vec = jax.ShapeDtypeStruct((1, S5_W), F32)
    sbuf = pltpu.VMEM((T * S5_HALF, LANES), F32)
    return _pallas(
        body, [dm, y_pre, u, *states, *states, are, aim, bre, bim, cre, cim, d_skip, wglu, bglu],
        name="s5_bwd", grid=(n,),
        in_specs=[chunk, chunk, chunk] + [sspec] * 4 + [pspec] * 4 + [
            const((S5_TILES, LANES)), const((S5_TILES, LANES)), const((LANES, S5_N)), const((LANES, S5_N)),
            const((S5_N, LANES)), const((S5_N, LANES)), const((1, S5_W)), const((S5_W, S5_W)), const((1, S5_W))],
        out_specs=[chunk, const((S5_W, S5_W)), const((1, S5_W)), const((1, S5_W)), const((S5_N, LANES)),
                   const((S5_N, LANES)), const((LANES, S5_N)), const((LANES, S5_N)), const((S5_TILES, LANES)),
                   const((S5_TILES, LANES))],
        out_shape=[jax.ShapeDtypeStruct((L, S5_W), BF16), jax.ShapeDtypeStruct((S5_W, S5_W), F32), vec, vec,
                   jax.ShapeDtypeStruct((S5_N, LANES), F32), jax.ShapeDtypeStruct((S5_N, LANES), F32),
                   jax.ShapeDtypeStruct((LANES, S5_N), F32), jax.ShapeDtypeStruct((LANES, S5_N), F32), tile, tile],
        scratch_shapes=[sbuf, sbuf, sbuf, sbuf, pltpu.VMEM((S5_TILES, LANES), F32), pltpu.VMEM((S5_TILES, LANES), F32)],
        comm=comm)


_EYE8 = np.eye(8, dtype=np.float32)


def _compact_b(b):
    return jnp.einsum("akpc,kj->jcakp", b.reshape(4, 8, S5_P, S5_GC), _EYE8).reshape(LANES, S5_N)


def _uncompact_b(m):
    return jnp.einsum("kcakp->akpc", m.reshape(8, S5_GC, 4, 8, S5_P)).reshape(S5_G, S5_P, S5_GC)


_HEAD_MEAN = np.kron(np.eye(CONV_W // CONV_HD, dtype=np.float32), np.full((CONV_HD, CONV_HD), 1.0 / CONV_HD, np.float32))


def _head_mean(v, m):
    hi = v.astype(BF16)
    lo = (v - hi.astype(F32)).astype(BF16)
    return _dot(hi, m) + _dot(lo, m)


def _head_norm(zc, m):
    d = zc - _head_mean(zc, m)
    rstd = lax.rsqrt(_head_mean(d * d, m) + EPS)
    return d * rstd, rstd


def _taps_by_phase(first):
    groups = {}
    for k in range(CONV_K):
        m, s = divmod(first + k, 8)
        groups.setdefault(s, []).append((m, k))
    return groups


def _causal_taps(buf, w_ref, r0, first, flip):
    acc = None
    for s, taps in sorted(_taps_by_phase(first).items()):
        rows = CONV_SB + (8 if s else 0)
        y = None
        for m, k in taps:
            kk = CONV_K - 1 - k if flip else k
            term = w_ref[kk:kk + 1, :] * buf[pl.ds(r0 + 8 * m, rows), :]
            y = term if y is None else y + term
        y = y[s:s + CONV_SB, :]
        acc = y if acc is None else acc + y
    return acc


def _conv_fwd(u, o_s5, wdw, bdw, lng, lnb, mavg, comm=()):
    L = u.shape[0]
    T = min(TS, L)
    n = L // T
    first_tap = HALO - (CONV_K - 1)

    def body(v1_ref, v2_ref, s5_ref, w_ref, b_ref, g_ref, be_ref, m_ref, zc_ref, o_ref, zbuf):
        i = pl.program_id(0)

        @pl.when(i == 0)
        def _():
            zbuf[0:HALO, :] = jnp.zeros((HALO, CONV_W), F32)

        zbuf[HALO:HALO + T, :] = v1_ref[...] * _sigmoid(v2_ref[...])
        for r0 in range(0, T, CONV_SB):
            zc_ref[r0:r0 + CONV_SB, :] = b_ref[...] + _causal_taps(zbuf, w_ref, r0, first_tap, False)
        zbuf[0:HALO, :] = zbuf[T:T + HALO, :]
        zn, _ = _head_norm(zc_ref[...], m_ref[...])
        zz = zn * g_ref[...] + be_ref[...]
        o_ref[:, 0:S5_W] = s5_ref[...]
        o_ref[:, S5_W:S5_W + CONV_W] = (zz * _sigmoid(zz)).astype(BF16)

    const = lambda shape: pl.BlockSpec(shape, lambda i: (0, 0))
    vec = const((1, CONV_W))
    return _pallas(body, [u, u, o_s5, wdw, bdw, lng, lnb, mavg], name="conv_fwd", grid=(n,),
                   in_specs=[pl.BlockSpec((T, CONV_W), lambda i: (i, 1)), pl.BlockSpec((T, CONV_W), lambda i: (i, 2)),
                             pl.BlockSpec((T, S5_W), lambda i: (i, 0)), const((CONV_K, CONV_W)), vec, vec, vec,
                             const((CONV_W, CONV_W))],
                   out_specs=[pl.BlockSpec((T, CONV_W), lambda i: (i, 0)),
                              pl.BlockSpec((T, S5_W + CONV_W), lambda i: (i, 0))],
                   out_shape=[jax.ShapeDtypeStruct((L, CONV_W), F32), jax.ShapeDtypeStruct((L, S5_W + CONV_W), BF16)],
                   scratch_shapes=[pltpu.VMEM((T + HALO, CONV_W), F32)], comm=comm)


def _conv_bwd(dm, zc, u, du_s5, wdw, lng, lnb, mavg, comm=()):
    L = u.shape[0]
    T = min(TS, L)
    n = L // T
    hb = T // HALO
    first_tap = HALO - (CONV_K - 1)

    def body(dm_ref, zc_ref, v1_ref, v2_ref, p1_ref, p2_ref, s5_ref, w_ref, g_ref, be_ref, m_ref,
             du_ref, dw_ref, db_ref, dg_ref, dbe_ref, zbuf, dzbuf, head):
        i = pl.program_id(0)
        first = i == 0

        @pl.when(first)
        def _():
            head[...] = jnp.zeros_like(head)
            dw_ref[...] = jnp.zeros_like(dw_ref)

        zn, rstd = _head_norm(zc_ref[...], m_ref[...])
        zz = zn * g_ref[...] + be_ref[...]
        sg = _sigmoid(zz)
        dzz = dm_ref[...].astype(F32) * sg * (1.0 + zz * (1.0 - sg))
        _accumulate(dbe_ref, first, _colsum(dzz))
        _accumulate(dg_ref, first, _colsum(dzz * zn))
        dzn = dzz * g_ref[...]
        dzc = rstd * (dzn - _head_mean(dzn, m_ref[...]) - zn * _head_mean(dzn * zn, m_ref[...]))
        _accumulate(db_ref, first, _colsum(dzc))

        dzbuf[0:T, :] = dzc
        dzbuf[T:T + HALO, :] = head[...]
        head[...] = dzbuf[0:HALO, :]
        keep = (i < n - 1).astype(F32)
        zbuf[0:HALO, :] = p1_ref[...] * _sigmoid(p2_ref[...]) * keep
        zbuf[HALO:HALO + T, :] = v1_ref[...] * _sigmoid(v2_ref[...])
        du_ref[:, 0:S5_W] = s5_ref[...]

        for r0 in range(0, T, CONV_SB):
            rows = slice(r0, r0 + CONV_SB)
            dzc_b = dzbuf[rows, :]
            for s, taps in sorted(_taps_by_phase(first_tap).items()):
                pad = ([jnp.zeros((s, CONV_W), F32)] if s else []) + [dzc_b] + ([jnp.zeros((8 - s, CONV_W), F32)] if s else [])
                shifted = jnp.concatenate(pad, axis=0) if s else dzc_b
                n_rows = shifted.shape[0]
                for m, k in taps:
                    prod = shifted * zbuf[pl.ds(r0 + 8 * m, n_rows), :]
                    dw_ref[8 * k:8 * k + 8, :] += jnp.sum(prod.reshape(n_rows // 8, 8, CONV_W), axis=0)
            dz = _causal_taps(dzbuf, w_ref, r0, 0, True)
            v1 = v1_ref[rows, :]
            sg2 = _sigmoid(v2_ref[rows, :])
            du_ref[rows, S5_W:S5_W + CONV_W] = (dz * sg2).astype(BF16)
            du_ref[rows, S5_W + CONV_W:S5_W + 2 * CONV_W] = (dz * v1 * sg2 * (1.0 - sg2)).astype(BF16)

    rev = lambda c: (lambda i: (n - 1 - i, c))
    prev = lambda c: (lambda i: (jnp.maximum((n - 1 - i) * hb - 1, 0), c))
    const = lambda shape: pl.BlockSpec(shape, lambda i: (0, 0))
    vec = const((1, CONV_W))
    vshape = jax.ShapeDtypeStruct((1, CONV_W), F32)
    return _pallas(
        body, [dm, zc, u, u, u, u, du_s5, wdw, lng, lnb, mavg], name="conv_bwd", grid=(n,),
        in_specs=[pl.BlockSpec((T, CONV_W), rev(1)), pl.BlockSpec((T, CONV_W), rev(0)),
                  pl.BlockSpec((T, CONV_W), rev(1)), pl.BlockSpec((T, CONV_W), rev(2)),
                  pl.BlockSpec((HALO, CONV_W), prev(1)), pl.BlockSpec((HALO, CONV_W), prev(2)),
                  pl.BlockSpec((T, S5_W), rev(0)), const((CONV_K, CONV_W)), vec, vec, const((CONV_W, CONV_W))],
        out_specs=[pl.BlockSpec((T, S5_W + 2 * CONV_W), rev(0)), const((8 * HALO, CONV_W)), vec, vec, vec],
        out_shape=[jax.ShapeDtypeStruct((L, S5_W + 2 * CONV_W), BF16), jax.ShapeDtypeStruct((8 * HALO, CONV_W), F32),
                   vshape, vshape, vshape],
        scratch_shapes=[pltpu.VMEM((T + HALO, CONV_W), F32), pltpu.VMEM((T + HALO, CONV_W), F32),
                        pltpu.VMEM((HALO, CONV_W), F32)], comm=comm)


def _gather_all(v, comm=()):
    rows, cols = v.shape

    def body(x_ref, out_ref, send_sems, recv_sems, local_sem):
        x, y, c, chips = _position()
        me, sibling = (x, y, c), (x, y, 1 - c)

        def block(px, py, pc):
            return out_ref.at[pl.ds((4 * px + 2 * py + pc) * rows, rows), :]

        def copy(k, blk, to, src=None):
            return pltpu.make_async_remote_copy(
                src_ref=block(*blk) if src is None else src, dst_ref=block(*blk), send_sem=send_sems.at[k],
                recv_sem=recv_sems.at[k], device_id=to, device_id_type=MESH)

        mine = pltpu.make_async_copy(x_ref, block(*me), local_sem)
        mine.start()
        first = [copy(0, me, sibling, src=x_ref)]
        first += [copy(1 + j, me, (*chip, c), src=x_ref) for j, chip in enumerate(chips)]
        for cp in first:
            cp.start()
        passed = [copy(4 + j, (*chip, c), sibling) for j, chip in enumerate(chips)]
        for j, chip in enumerate(chips):
            copy(1 + j, (*chip, c), me).wait_recv()
            passed[j].start()
        copy(0, sibling, me).wait_recv()
        for j, chip in enumerate(chips):
            copy(4 + j, (*chip, 1 - c), me).wait_recv()
        for cp in first + passed:
            cp.wait_send()
        mine.wait()

    whole = pl.BlockSpec(memory_space=pltpu.VMEM)
    return _pallas(body, [v], name="gather_small", grid=(1,), in_specs=[whole], out_specs=[whole],
                   out_shape=[jax.ShapeDtypeStruct((N_DEV * rows, cols), v.dtype)],
                   scratch_shapes=[pltpu.SemaphoreType.DMA((7,)), pltpu.SemaphoreType.DMA((7,)), pltpu.SemaphoreType.DMA],
                   comm=comm)[0]


def _adamw(w, g, m, v):
    m = ADAM_B1 * m + (1.0 - ADAM_B1) * g
    v = ADAM_B2 * v + (1.0 - ADAM_B2) * jnp.square(g)
    m_hat = m / (1.0 - ADAM_B1 ** ADAM_STEP)
    v_hat = v / (1.0 - ADAM_B2 ** ADAM_STEP)
    return -ADAM_LR * (m_hat / (jnp.sqrt(v_hat) + ADAM_EPS) + ADAM_WD * w), m, v


def _sum_slots(recv, name, comm=()):
    _, rows, cols = recv.shape
    tr = _row_tile(rows, cols)

    def body(r_ref, o_ref):
        acc = r_ref[0].astype(F32)
        for s in range(1, N_CHIPS):
            acc = acc + r_ref[s].astype(F32)
        o_ref[...] = acc

    return _pallas(body, [recv], name=name, grid=(rows // tr,),
                   in_specs=[pl.BlockSpec((N_CHIPS, tr, cols), lambda i: (0, i, 0))],
                   out_specs=[pl.BlockSpec((tr, cols), lambda i: (i, 0))],
                   out_shape=[jax.ShapeDtypeStruct((rows, cols), F32)], comm=comm)[0]


def _add_halves(mine, theirs, name):
    slots, rows, cols = mine.shape
    tr = _row_tile(rows, cols * slots)

    def body(a_ref, b_ref, o_ref):
        o_ref[...] = (a_ref[...].astype(F32) + b_ref[...].astype(F32)).astype(o_ref.dtype)

    spec = pl.BlockSpec((slots, tr, cols), lambda i: (0, i, 0))
    return _pallas(body, [mine, theirs], name=name, grid=(rows // tr,), in_specs=[spec, spec], out_specs=[spec],
                   out_shape=[jax.ShapeDtypeStruct(mine.shape, mine.dtype)])[0]


def _adamw_sharded(w, parts, m, v, name, comm=()):
    rows, cols = w.shape
    tr = _row_tile(rows, cols)
    n = len(parts)

    def body(w_ref, *refs):
        p_refs, (m_ref, v_ref, g_ref, d_ref, nm_ref, nv_ref) = refs[:n], refs[n:]
        g = p_refs[0][...]
        for p_ref in p_refs[1:]:
            g = g + p_ref[...]
        g_ref[...] = g
        d_ref[...], nm_ref[...], nv_ref[...] = _adamw(w_ref[...], g, m_ref[...], v_ref[...])

    spec = pl.BlockSpec((tr, cols), lambda i: (i, 0))
    shape = jax.ShapeDtypeStruct((rows, cols), F32)
    return _pallas(body, [w, *parts, m, v], name=name, grid=(rows // tr,), in_specs=[spec] * (n + 3),
                   out_specs=[spec] * 4, out_shape=[shape] * 4, comm=comm)


def _adamw_small(w, gathered, m, v):
    rows, cols = w.shape

    def body(w_ref, a_ref, m_ref, v_ref, g_ref, d_ref, nm_ref, nv_ref):
        g = a_ref[0:rows, :]
        for dev in range(1, N_DEV):
            g = g + a_ref[dev * rows:(dev + 1) * rows, :]
        g_ref[...] = g
        d_ref[...], nm_ref[...], nv_ref[...] = _adamw(w_ref[...], g, m_ref[...], v_ref[...])

    shape = jax.ShapeDtypeStruct((rows, cols), F32)
    return pl.pallas_call(
        body, name="adamw_small", out_shape=[shape] * 4,
        compiler_params=pltpu.CompilerParams(vmem_limit_bytes=VMEM_LIMIT))(w, gathered, m, v)


PACK_TILE = 8 * LANES


def _pack_small(vals, last_row):
    rows = []
    for name in SMALL:
        flat = vals[name].reshape(-1).astype(F32)
        rows.append(jnp.pad(flat, (0, -flat.size % PACK_TILE)).reshape(-1, LANES))
    rows.append(jnp.pad(last_row, ((0, 7), (0, 0))))
    return jnp.concatenate(rows, axis=0)


def _unpack_small(packed, like):
    out, r = {}, 0
    for name in SMALL:
        size = like[name].size
        out[name] = packed[r:r + -(-size // LANES)].reshape(-1)[:size].reshape(like[name].shape)
        r += 8 * -(-size // PACK_TILE)
    return out, packed[r, 0]


def _shard2d(name, v):
    v = v.reshape(v.shape[-2:])
    return v.T if name in FFN_T else v


def _unshard(name, v, shape):
    return (v.T if name in FFN_T else v).reshape(shape)


def _train_step(x3, tgt3, wts, ms, vs):
    x, tgt = x3[0], tgt3[0]
    L, D = x.shape
    row = lambda v: v.reshape(1, -1)
    shards = {k: _shard2d(k, wts[k]) for k in SHARDED}
    sends = {k: shards[k] if k == "conv_w_dw" else _cast_bf16(shards[k], "cast_" + k) for k in SHARDED}
    gat = {k: _Gather(sends[k]) for k in SHARDED}
    w = lambda k: gat[k].result[0]

    s = {k: wts[k] for k in SMALL}
    lr, li = s["s5_lam_re"].reshape(1, S5_N), s["s5_lam_im"].reshape(1, S5_N)
    ldt = jnp.repeat(s["s5_log_dt"].reshape(S5_G), S5_P).reshape(1, S5_N)
    brc, bic = _compact_b(s["s5_b_re"].reshape(S5_G, S5_P, S5_GC)), _compact_b(s["s5_b_im"].reshape(S5_G, S5_P, S5_GC))
    crc = _compact_b(s["s5_c_re"].reshape(S5_G, S5_GC, S5_P).transpose(0, 2, 1)).T
    cic = _compact_b(s["s5_c_im"].reshape(S5_G, S5_GC, S5_P).transpose(0, 2, 1)).T
    d_skip, b_glu = row(s["s5_d"]), row(s["s5_b_glu"])
    b_dw, ln_g, ln_b = row(s["conv_b_dw"]), row(s["conv_ln_g"]), row(s["conv_ln_b"])
    g1, gm, g2, gf = row(s["ffn1_norm"]), row(s["mix_norm"]), row(s["ffn2_norm"]), row(s["final_norm"])
    mavg = jnp.asarray(_HEAD_MEAN, dtype=BF16)

    h1 = _rms_fwd(x, g1, "rms1", comm=[gat["ffn1_w_gate"], gat["ffn1_w_up"]])
    a1, b1, act1 = _ffn_up(h1, w("ffn1_w_gate"), w("ffn1_w_up"), "ffn1_up", comm=[gat["ffn1_w_down"]])
    x1, h2 = _ffn_down(act1, w("ffn1_w_down"), x, gm, "ffn1_down", comm=[gat["w_in"], gat["s5_w_glu"]])
    u = _mm_grouped(h2, w("w_in"), "in_proj", comm=[gat["conv_w_dw"], gat["w_out"]])
    are, aim, bre, bim, cre, cim = _s5_params_fwd(lr, li, ldt, brc, bic, crc, cic)
    are_t, aim_t = are.reshape(S5_TILES, LANES), aim.reshape(S5_TILES, LANES)
    w_glu = w("s5_w_glu").reshape(S5_W, S5_W)
    *states, y_pre, o_s5 = _s5_fwd(u, are_t, aim_t, bre, bim, cre, cim, d_skip, w_glu, b_glu, comm=[gat["ffn2_w_gate"]])
    w_dw = w("conv_w_dw").transpose(1, 0, 2).reshape(CONV_K, CONV_W)
    zc, cat = _conv_fwd(u, o_s5, w_dw, b_dw, ln_g, ln_b, mavg, comm=[gat["ffn2_w_up"]])
    w_out = w("w_out").reshape(-1, D)
    x2, h3 = _mix_out(cat, w_out, x1, g2, "mix_out")
    a2, b2, act2 = _ffn_up(h3, w("ffn2_w_gate"), w("ffn2_w_up"), "ffn2_up", comm=[gat["ffn2_w_down"]])
    dx3, dx3b, loss_part, d_gf = _ffn_down_loss(act2, w("ffn2_w_down"), x2, gf, tgt, "ffn2_down_loss")

    gs, sc, waiting = {"final_norm": d_gf}, {}, []

    def grad(key, g):
        if g.shape[1] % (2 * BF16_ROWS) == 0 and g.dtype == BF16:
            waiting.append(_SwapHalf(g, key))
        else:
            sc[key] = _Scatter(g)
            waiting.append(sc[key])

    def carry(call, *args, **kw):
        ops = list(waiting)
        waiting.clear()
        res = call(*args, comm=ops, **kw)
        for op in ops:
            if isinstance(op, _SwapHalf):
                sc[op.key] = _Scatter(_add_halves(*op.result, "add_" + op.key))
                waiting.append(sc[op.key])
        return res

    da2, db2 = _ffn_bwd_act(dx3b, w("ffn2_w_down"), a2, b2, "ffn2_bwd_act")
    grad("ffn2_w_down", _mm_tn(act2, dx3b[None], 0.5, "ffn2_dwd", N_CHIPS))
    grad("ffn2_w_gate", _mm_tn(da2, h3[None], 1.0, "ffn2_dwg", N_CHIPS))
    grad("ffn2_w_up", _mm_tn(db2, h3[None], 1.0, "ffn2_dwu", N_CHIPS))
    dx2, dx2b, gs["ffn2_norm"] = carry(_mm_rmsbwd, [da2, db2], [w("ffn2_w_gate"), w("ffn2_w_up")], False, False, x2, g2,
                                       dx3, "ffn2_bwd_dx")

    dm = _mm_nt(dx2b, w_out, "mix_bwd")
    grad("w_out", _mm_tn(cat[None], dx2b[None], 1.0, "dwout", 1).reshape(N_CHIPS, -1, D))
    (du_s5, d_wglu, gs["s5_b_glu"], gs["s5_d"], d_crc, d_cic, d_bre, d_bim, d_are, d_aim) = carry(
        _s5_bwd, dm, y_pre, u, states, are_t, aim_t, bre, bim, cre, cim, d_skip, w_glu, b_glu)
    grad("s5_w_glu", d_wglu.astype(BF16).reshape(N_CHIPS, -1, S5_W))
    g_lr, g_li, g_ldt, g_brc, g_bic = _s5_params_bwd(lr, li, ldt, brc, bic, d_are.reshape(1, S5_N),
                                                     d_aim.reshape(1, S5_N), d_bre, d_bim)
    gs["s5_lam_re"], gs["s5_lam_im"] = g_lr, g_li
    gs["s5_log_dt"] = jnp.sum(g_ldt.reshape(S5_G, S5_P), axis=1)
    gs["s5_b_re"], gs["s5_b_im"] = _uncompact_b(g_brc), _uncompact_b(g_bic)
    gs["s5_c_re"] = _uncompact_b(d_crc.T).transpose(0, 2, 1)
    gs["s5_c_im"] = _uncompact_b(d_cic.T).transpose(0, 2, 1)
    du, d_wdw, gs["conv_b_dw"], gs["conv_ln_g"], gs["conv_ln_b"] = carry(_conv_bwd, dm, zc, u, du_s5, w_dw, ln_g, ln_b, mavg)
    d_wdw = jnp.sum(d_wdw.reshape(HALO, 8, CONV_W), axis=1)[:CONV_K]
    grad("conv_w_dw", d_wdw.reshape(CONV_K, N_CHIPS, -1).transpose(1, 0, 2))
    grad("w_in", _mm_tn(h2[None], du, 1.0, "dwin", N_CHIPS, b_cols=True))
    dx1, dx1b, gs["mix_norm"] = carry(_mm_rmsbwd, [du], [w("w_in")], True, True, x1, gm, dx2, "in_proj_bwd")

    da1, db1 = carry(_ffn_bwd_act, dx1b, w("ffn1_w_down"), a1, b1, "ffn1_bwd_act")
    grad("ffn1_w_down", _mm_tn(act1, dx1b[None], 0.5, "ffn1_dwd", N_CHIPS))
    grad("ffn1_w_gate", carry(_mm_tn, da1, h1[None], 1.0, "ffn1_dwg", N_CHIPS))
    grad("ffn1_w_up", carry(_mm_tn, db1, h1[None], 1.0, "ffn1_dwu", N_CHIPS))
    grad_x, _, gs["ffn1_norm"] = carry(_mm_rmsbwd, [da1, db1], [w("ffn1_w_gate"), w("ffn1_w_up")], False, False, x, g1,
                                       dx1, "ffn1_bwd_dx")

    order = ("ffn2_w_down", "ffn2_w_gate", "ffn2_w_up", "w_out", "s5_w_glu", "conv_w_dw", "w_in", "ffn1_w_down",
             "ffn1_w_gate", "ffn1_w_up")
    back = {}

    def sum_chips(k):
        part = _sum_slots(sc[k].result[0], "sum_" + k)
        back[k] = _SwapBack(part) if part.shape != shards[k].shape else _Swap(part)

    for k in order[:-1]:
        sum_chips(k)
    waiting.extend(back[k] for k in order[:-1])

    out = {}
    gsmall = {k: gs[k].reshape(wts[k].shape) for k in SMALL}
    zero_row = jnp.zeros((1, LANES), F32)
    g_all = carry(_gather_all, _pack_small(gsmall, loss_part))
    sum_chips(order[-1])
    waiting.append(back[order[-1]])
    res = _adamw_small(_pack_small(s, zero_row), g_all, _pack_small({k: ms[k] for k in SMALL}, zero_row),
                       _pack_small({k: vs[k] for k in SMALL}, zero_row))
    unpacked = [_unpack_small(r, s) for r in res]
    loss = unpacked[0][1]
    for k in SMALL:
        out[k] = [u_[0][k] for u_ in unpacked]

    for k in order:
        parts = [back[k].result[0]] if isinstance(back[k], _SwapBack) else [back[k].ins[0], back[k].result[0]]
        res = carry(_adamw_sharded, shards[k], parts, _shard2d(k, ms[k]), _shard2d(k, vs[k]), "adamw_" + k)
        out[k] = [_unshard(k, r, wts[k].shape) for r in res]
    return loss, grad_x[None], out


def kernel(x, ffn1_norm, ffn1_w_gate, ffn1_w_up, ffn1_w_down, mix_norm, w_in, s5_lam_re, s5_lam_im, s5_log_dt, s5_b_re, s5_b_im, s5_c_re, s5_c_im, s5_d, s5_w_glu, s5_b_glu, conv_w_dw, conv_b_dw, conv_ln_g, conv_ln_b, w_out, ffn2_norm, ffn2_w_gate, ffn2_w_up, ffn2_w_down, final_norm, loss_target, m_ffn1_norm, m_ffn1_w_gate, m_ffn1_w_up, m_ffn1_w_down, m_mix_norm, m_w_in, m_s5_lam_re, m_s5_lam_im, m_s5_log_dt, m_s5_b_re, m_s5_b_im, m_s5_c_re, m_s5_c_im, m_s5_d, m_s5_w_glu, m_s5_b_glu, m_conv_w_dw, m_conv_b_dw, m_conv_ln_g, m_conv_ln_b, m_w_out, m_ffn2_norm, m_ffn2_w_gate, m_ffn2_w_up, m_ffn2_w_down, m_final_norm, v_ffn1_norm, v_ffn1_w_gate, v_ffn1_w_up, v_ffn1_w_down, v_mix_norm, v_w_in, v_s5_lam_re, v_s5_lam_im, v_s5_log_dt, v_s5_b_re, v_s5_b_im, v_s5_c_re, v_s5_c_im, v_s5_d, v_s5_w_glu, v_s5_b_glu, v_conv_w_dw, v_conv_b_dw, v_conv_ln_g, v_conv_ln_b, v_w_out, v_ffn2_norm, v_ffn2_w_gate, v_ffn2_w_up, v_ffn2_w_down, v_final_norm):
    given = dict(locals())
    wts = {k: given[k] for k in WEIGHTS}
    ms = {k: given["m_" + k] for k in WEIGHTS}
    vs = {k: given["v_" + k] for k in WEIGHTS}
    loss, grad_x, out = _train_step(x, loss_target, wts, ms, vs)
    return (loss, grad_x, *[out[k][0] for k in WEIGHTS], *[out[k][1] for k in WEIGHTS],
            *[out[k][2] for k in WEIGHTS], *[out[k][3] for k in WEIGHTS])
```

```python
import functools

import jax
import jax.numpy as jnp
import numpy as np
from jax import lax
from jax.experimental import pallas as pl
from jax.experimental.pallas import tpu as pltpu

F32, BF16 = jnp.float32, jnp.bfloat16
MESH = pl.DeviceIdType.MESH

EPS = 1e-6
ADAM_LR, ADAM_B1, ADAM_B2, ADAM_EPS, ADAM_WD, ADAM_STEP = 0.001, 0.9, 0.999, 1e-08, 0.01, 10

N_CHIPS = 4
N_DEV = 8
LANES = 128
BF16_ROWS = 16
S5_W, S5_G, S5_GC, S5_P = 512, 32, 16, 64
S5_N = S5_G * S5_P
S5_TILES = S5_N // LANES
S5_HALF = 8
CONV_W, CONV_K, CONV_HD = 512, 31, 64
HALO = 32
CONV_SB = 32
TM = 512
TMS = 1024
TK = 2048
TS = 256
VMEM_LIMIT = 48 << 20
GELU_C0, GELU_C1 = 0.7978845608028654, 0.044715

FFN_T = ("ffn1_w_gate", "ffn1_w_up", "ffn2_w_gate", "ffn2_w_up")
SHARDED = ("ffn1_w_gate", "ffn1_w_up", "ffn1_w_down", "w_in", "s5_w_glu", "conv_w_dw", "w_out",
           "ffn2_w_gate", "ffn2_w_up", "ffn2_w_down")
SMALL = ("ffn1_norm", "mix_norm", "s5_lam_re", "s5_lam_im", "s5_log_dt", "s5_b_re", "s5_b_im", "s5_c_re",
         "s5_c_im", "s5_d", "s5_b_glu", "conv_b_dw", "conv_ln_g", "conv_ln_b", "ffn2_norm", "final_norm")
WEIGHTS = ("ffn1_norm", "ffn1_w_gate", "ffn1_w_up", "ffn1_w_down", "mix_norm", "w_in", "s5_lam_re", "s5_lam_im",
           "s5_log_dt", "s5_b_re", "s5_b_im", "s5_c_re", "s5_c_im", "s5_d", "s5_w_glu", "s5_b_glu", "conv_w_dw",
           "conv_b_dw", "conv_ln_g", "conv_ln_b", "w_out", "ffn2_norm", "ffn2_w_gate", "ffn2_w_up", "ffn2_w_down",
           "final_norm")


def _dot(a, b):
    return jnp.dot(a, b, preferred_element_type=F32)


def _dot_nt(a, b):
    return lax.dot_general(a, b, (((1,), (1,)), ((), ())), preferred_element_type=F32)


def _dot_tn(a, b):
    return lax.dot_general(a, b, (((0,), (0,)), ((), ())), preferred_element_type=F32)


def _colsum(v):
    return jnp.sum(v, axis=0, keepdims=True)


def _sigmoid(v):
    return 1.0 / (1.0 + jnp.exp(-v))


def _accumulate(ref, first, value):
    @pl.when(first)
    def _():
        ref[...] = value

    @pl.when(jnp.logical_not(first))
    def _():
        ref[...] += value


def _position():
    x, y, c = lax.axis_index("x"), lax.axis_index("y"), lax.axis_index("c")
    return x, y, c, [(1 - x, y), (x, 1 - y), (1 - x, 1 - y)]


def _remote(src, dst, sems, send, recv, device):
    return pltpu.make_async_remote_copy(src_ref=src, dst_ref=dst, send_sem=sems.at[send], recv_sem=sems.at[recv],
                                        device_id=device, device_id_type=MESH)


class _Gather:
    def __init__(self, shard):
        self.ins = [shard]
        self.outs = [jax.ShapeDtypeStruct((N_CHIPS,) + shard.shape, shard.dtype)]
        self.rows = shard.shape[0]
        self.halve = shard.dtype == BF16 and self.rows % (2 * BF16_ROWS) == 0
        self.n_sem = 13 if self.halve else 7
        self.result = None

    def _copies(self, ins, outs, sems, s0, pos):
        x, y, c, chips = pos
        src, dst = ins[0], outs[0]
        me = 2 * x + y
        if self.halve:
            hr = self.rows // 2
            mine, theirs = pl.ds(c * hr, hr), pl.ds((1 - c) * hr, hr)
            part = lambda slot, rows: dst.at[slot, rows]
            my_src = src.at[mine]
        else:
            mine = theirs = None
            part = lambda slot, rows: dst.at[slot]
            my_src = src
        slot = lambda j: 2 * chips[j][0] + chips[j][1]
        local = lambda: pltpu.make_async_copy(src, dst.at[me], sems.at[s0])
        send = lambda j: _remote(my_src, part(me, mine), sems, s0 + 1 + j, s0 + 4 + j, (*chips[j], c))
        land = lambda j: _remote(my_src, part(slot(j), mine), sems, s0 + 1 + j, s0 + 4 + j, (*chips[j], c))
        fwd = lambda j: _remote(part(slot(j), mine), part(slot(j), mine), sems, s0 + 7 + j, s0 + 10 + j, (x, y, 1 - c))
        got = lambda j: _remote(part(slot(j), theirs), part(slot(j), theirs), sems, s0 + 7 + j, s0 + 10 + j,
                                (x, y, 1 - c))
        return local, send, land, fwd, got

    def start(self, ins, outs, sems, s0, pos):
        local, send, _, _, _ = self._copies(ins, outs, sems, s0, pos)
        local().start()
        for j in range(N_CHIPS - 1):
            send(j).start()

    def finish(self, ins, outs, sems, s0, pos):
        local, send, land, fwd, got = self._copies(ins, outs, sems, s0, pos)
        others = range(N_CHIPS - 1)
        for j in others:
            land(j).wait_recv()
            if self.halve:
                fwd(j).start()
        for j in others:
            if self.halve:
                got(j).wait_recv()
        for j in others:
            send(j).wait_send()
            if self.halve:
                fwd(j).wait_send()
        local().wait()


class _Scatter:
    def __init__(self, grad):
        self.ins = [grad]
        self.outs = [jax.ShapeDtypeStruct(grad.shape, grad.dtype)]
        self.n_sem = 7
        self.result = None

    def _copies(self, ins, outs, sems, s0, pos):
        x, y, c, chips = pos
        src, dst = ins[0], outs[0]
        me = 2 * x + y
        slot = lambda j: 2 * chips[j][0] + chips[j][1]
        local = lambda: pltpu.make_async_copy(src.at[me], dst.at[me], sems.at[s0])
        send = lambda j: _remote(src.at[slot(j)], dst.at[me], sems, s0 + 1 + j, s0 + 4 + j, (*chips[j], c))
        land = lambda j: _remote(src.at[me], dst.at[slot(j)], sems, s0 + 1 + j, s0 + 4 + j, (*chips[j], c))
        return local, send, land

    def start(self, ins, outs, sems, s0, pos):
        local, send, _ = self._copies(ins, outs, sems, s0, pos)
        local().start()
        for j in range(N_CHIPS - 1):
            send(j).start()

    def finish(self, ins, outs, sems, s0, pos):
        local, send, land = self._copies(ins, outs, sems, s0, pos)
        for j in range(N_CHIPS - 1):
            land(j).wait_recv()
        for j in range(N_CHIPS - 1):
            send(j).wait_send()
        local().wait()


class _Swap:
    def __init__(self, part):
        self.ins = [part]
        self.outs = [jax.ShapeDtypeStruct(part.shape, part.dtype)]
        self.n_sem = 2
        self.result = None

    def _copy(self, ins, outs, sems, s0, pos):
        x, y, c, _ = pos
        return _remote(ins[0], outs[0], sems, s0, s0 + 1, (x, y, 1 - c))

    def start(self, ins, outs, sems, s0, pos):
        self._copy(ins, outs, sems, s0, pos).start()

    def finish(self, ins, outs, sems, s0, pos):
        self._copy(ins, outs, sems, s0, pos).wait()


class _SwapHalf:
    def __init__(self, grad, key):
        slots, rows, cols = grad.shape
        self.ins, self.outs, self.key = [grad], [jax.ShapeDtypeStruct((slots, rows // 2, cols), grad.dtype)], key
        self.hr = rows // 2
        self.n_sem = 2
        self.result = None

    def _copy(self, ins, outs, sems, s0, pos):
        x, y, c, _ = pos
        return _remote(ins[0].at[:, pl.ds((1 - c) * self.hr, self.hr)], outs[0], sems, s0, s0 + 1, (x, y, 1 - c))

    def start(self, ins, outs, sems, s0, pos):
        self._copy(ins, outs, sems, s0, pos).start()

    def finish(self, ins, outs, sems, s0, pos):
        self._copy(ins, outs, sems, s0, pos).wait()


class _SwapBack:
    def __init__(self, part):
        hr, cols = part.shape
        self.ins, self.outs = [part], [jax.ShapeDtypeStruct((2 * hr, cols), part.dtype)]
        self.hr = hr
        self.n_sem = 3
        self.result = None

    def _copies(self, ins, outs, sems, s0, pos):
        x, y, c, _ = pos
        mine, theirs = pl.ds(c * self.hr, self.hr), pl.ds((1 - c) * self.hr, self.hr)
        local = lambda: pltpu.make_async_copy(ins[0], outs[0].at[mine], sems.at[s0])
        send = lambda: _remote(ins[0], outs[0].at[mine], sems, s0 + 1, s0 + 2, (x, y, 1 - c))
        land = lambda: _remote(ins[0], outs[0].at[theirs], sems, s0 + 1, s0 + 2, (x, y, 1 - c))
        return local, send, land

    def start(self, ins, outs, sems, s0, pos):
        local, send, _ = self._copies(ins, outs, sems, s0, pos)
        local().start()
        send().start()

    def finish(self, ins, outs, sems, s0, pos):
        local, send, land = self._copies(ins, outs, sems, s0, pos)
        land().wait_recv()
        send().wait_send()
        local().wait()


def _pallas(body, args, *, name, grid, in_specs, out_specs, out_shape, scratch_shapes=(), comm=()):
    comm = list(comm)
    n_in, n_out, n_scr = len(in_specs), len(out_specs), len(scratch_shapes)
    c_in = [a for op in comm for a in op.ins]
    c_out = [s for op in comm for s in op.outs]
    n_sem = sum(op.n_sem for op in comm)

    def full(*refs):
        o0 = n_in + len(c_in)
        s0 = o0 + n_out + len(c_out)
        ins, cin = refs[:n_in], refs[n_in:o0]
        outs, cout = refs[o0:o0 + n_out], refs[o0 + n_out:s0]
        scratch = refs[s0:s0 + n_scr]
        if comm:
            sems = refs[s0 + n_scr]
            ids = [pl.program_id(d) for d in range(len(grid))]
            first = functools.reduce(jnp.logical_and, [i == 0 for i in ids])
            last = functools.reduce(jnp.logical_and, [i == g - 1 for i, g in zip(ids, grid)])
            pos = _position()

            def each(step):
                ci = co = cs = 0
                for op in comm:
                    getattr(op, step)(cin[ci:ci + len(op.ins)], cout[co:co + len(op.outs)], sems, cs, pos)
                    ci, co, cs = ci + len(op.ins), co + len(op.outs), cs + op.n_sem

            @pl.when(first)
            def _():
                each("start")

        body(*ins, *outs, *scratch)
        if comm:
            @pl.when(last)
            def _():
                each("finish")

    hbm = pl.BlockSpec(memory_space=pl.ANY)
    res = pl.pallas_call(
        full, name=name, grid=grid,
        in_specs=list(in_specs) + [hbm] * len(c_in), out_specs=list(out_specs) + [hbm] * len(c_out),
        out_shape=list(out_shape) + c_out,
        scratch_shapes=list(scratch_shapes) + ([pltpu.SemaphoreType.DMA((n_sem,))] if comm else []),
        compiler_params=pltpu.CompilerParams(dimension_semantics=("arbitrary",) * len(grid),
                                             vmem_limit_bytes=VMEM_LIMIT))(*args, *c_in)
    k = n_out
    for op in comm:
        op.result = list(res[k:k + len(op.outs)])
        k += len(op.outs)
    return list(res[:n_out])


def _row_tile(rows, cols, itemsize=4, budget=1 << 20):
    t = rows
    while t % (2 * BF16_ROWS) == 0 and t * cols * itemsize > budget:
        t //= 2
    return t


def _cast_bf16(w, name):
    rows, cols = w.shape
    tr = _row_tile(rows, cols)

    def body(w_ref, o_ref):
        o_ref[...] = w_ref[...].astype(BF16)

    spec = pl.BlockSpec((tr, cols), lambda i: (i, 0))
    return _pallas(body, [w], name=name, grid=(rows // tr,), in_specs=[spec], out_specs=[spec],
                   out_shape=[jax.ShapeDtypeStruct((rows, cols), BF16)])[0]


def _rms_fwd(x, g, name, comm=()):
    L, D = x.shape

    def body(x_ref, g_ref, h_ref):
        xf = x_ref[...]
        r = lax.rsqrt(jnp.mean(xf * xf, axis=-1, keepdims=True) + EPS)
        h_ref[...] = (xf * r * g_ref[...]).astype(BF16)

    row = pl.BlockSpec((TMS, D), lambda i: (i, 0))
    return _pallas(body, [x, g], name=name, grid=(L // TMS,),
                   in_specs=[row, pl.BlockSpec((1, D), lambda i: (0, 0))], out_specs=[row],
                   out_shape=[jax.ShapeDtypeStruct((L, D), BF16)], comm=comm)[0]


def _resident(shape):
    return pl.BlockSpec(shape, lambda *_: (0,) * len(shape), pipeline_mode=pl.Buffered(1))


def _ffn_up(h, wg_t, wu_t, name, comm=()):
    L, D = h.shape
    G, FS, _ = wg_t.shape

    def body(h_ref, wg_ref, wu_ref, a_ref, b_ref, act_ref):
        j = pl.program_id(1)
        hv = h_ref[...]
        a = _dot_nt(hv, wg_ref[j])
        b = _dot_nt(hv, wu_ref[j])
        a_ref[...] = a.astype(BF16)
        b_ref[...] = b.astype(BF16)
        act_ref[...] = (a * _sigmoid(a) * b).astype(BF16)

    ospec = pl.BlockSpec((None, TMS, FS), lambda i, j: (j, i, 0))
    oshape = jax.ShapeDtypeStruct((G, L, FS), BF16)
    return _pallas(body, [h, wg_t, wu_t], name=name, grid=(L // TMS, G),
                   in_specs=[pl.BlockSpec((TMS, D), lambda i, j: (i, 0)), _resident((G, FS, D)), _resident((G, FS, D))],
                   out_specs=[ospec, ospec, ospec], out_shape=[oshape, oshape, oshape], comm=comm)


def _group_sum(a_ref, w_ref, groups, mm=_dot):
    acc = mm(a_ref[0], w_ref[0])
    for j in range(1, groups):
        acc = acc + mm(a_ref[j], w_ref[j])
    return acc


def _ffn_down(act, wd, x, g_next, name, comm=()):
    G, L, FS = act.shape
    D = wd.shape[2]

    def body(act_ref, wd_ref, x_ref, g_ref, xn_ref, hn_ref):
        xn = x_ref[...] + 0.5 * _group_sum(act_ref, wd_ref, G)
        xn_ref[...] = xn
        r = lax.rsqrt(jnp.mean(xn * xn, axis=-1, keepdims=True) + EPS)
        hn_ref[...] = (xn * r * g_ref[...]).astype(BF16)

    row = pl.BlockSpec((TM, D), lambda i: (i, 0))
    return _pallas(body, [act, wd, x, g_next], name=name, grid=(L // TM,),
                   in_specs=[pl.BlockSpec((G, TM, FS), lambda i: (0, i, 0)), _resident((G, FS, D)), row,
                             pl.BlockSpec((1, D), lambda i: (0, 0))],
                   out_specs=[row, row],
                   out_shape=[jax.ShapeDtypeStruct((L, D), F32), jax.ShapeDtypeStruct((L, D), BF16)], comm=comm)


def _ffn_down_loss(act, wd, x, gf, tgt, name):
    G, L, FS = act.shape
    D = wd.shape[2]

    def body(act_ref, wd_ref, x_ref, g_ref, t_ref, dx_ref, dxb_ref, loss_ref, dg_ref):
        i = pl.program_id(0)
        xn = x_ref[...] + 0.5 * _group_sum(act_ref, wd_ref, G)
        r = lax.rsqrt(jnp.mean(xn * xn, axis=-1, keepdims=True) + EPS)
        xh = xn * r
        gv = g_ref[...]
        e = xh * gv - t_ref[...]
        part = 0.5 * jnp.sum(_colsum(e * e), axis=1, keepdims=True) / D
        dy = e / D
        _accumulate(loss_ref, i == 0, jnp.broadcast_to(part, (1, LANES)))
        _accumulate(dg_ref, i == 0, _colsum(dy * xh))
        dxh = dy * gv
        dx = r * (dxh - xh * jnp.mean(dxh * xh, axis=-1, keepdims=True))
        dx_ref[...] = dx
        dxb_ref[...] = dx.astype(BF16)

    row = pl.BlockSpec((TM, D), lambda i: (i, 0))
    return _pallas(body, [act, wd, x, gf, tgt], name=name, grid=(L // TM,),
                   in_specs=[pl.BlockSpec((G, TM, FS), lambda i: (0, i, 0)), _resident((G, FS, D)), row,
                             pl.BlockSpec((1, D), lambda i: (0, 0)), row],
                   out_specs=[row, row, pl.BlockSpec((1, LANES), lambda i: (0, 0)),
                              pl.BlockSpec((1, D), lambda i: (0, 0))],
                   out_shape=[jax.ShapeDtypeStruct((L, D), F32), jax.ShapeDtypeStruct((L, D), BF16),
                              jax.ShapeDtypeStruct((1, LANES), F32), jax.ShapeDtypeStruct((1, D), F32)])


def _ffn_bwd_act(dxb, wd, a, b, name, comm=()):
    L, D = dxb.shape
    G, FS, _ = wd.shape

    def body(dx_ref, wd_ref, a_ref, b_ref, da_ref, db_ref):
        dact = 0.5 * _dot_nt(dx_ref[...], wd_ref[pl.program_id(1)])
        av = a_ref[...].astype(F32)
        bv = b_ref[...].astype(F32)
        sg = _sigmoid(av)
        da_ref[...] = (dact * bv * sg * (1.0 + av * (1.0 - sg))).astype(BF16)
        db_ref[...] = (dact * av * sg).astype(BF16)

    gspec = pl.BlockSpec((None, TMS, FS), lambda i, j: (j, i, 0))
    oshape = jax.ShapeDtypeStruct((G, L, FS), BF16)
    return _pallas(body, [dxb, wd, a, b], name=name, grid=(L // TMS, G),
                   in_specs=[pl.BlockSpec((TMS, D), lambda i, j: (i, 0)), _resident((G, FS, D)), gspec, gspec],
                   out_specs=[gspec, gspec], out_shape=[oshape, oshape], comm=comm)


def _mm_grouped(a, w, name):
    L, K = a.shape
    G, _, N = w.shape

    def body(a_ref, w_ref, o_ref):
        o_ref[...] = _dot(a_ref[...], w_ref[pl.program_id(1)])

    return _pallas(body, [a, w], name=name, grid=(L // TMS, G),
                   in_specs=[pl.BlockSpec((TMS, K), lambda i, g: (i, 0)), _resident((G, K, N))],
                   out_specs=[pl.BlockSpec((TMS, N), lambda i, g: (i, g))],
                   out_shape=[jax.ShapeDtypeStruct((L, G * N), F32)])[0]


def _mm_nt(a, w, name):
    L, K = a.shape
    N = w.shape[0]

    def body(a_ref, w_ref, o_ref):
        o_ref[...] = _dot_nt(a_ref[...], w_ref[...]).astype(BF16)

    return _pallas(body, [a, w], name=name, grid=(L // TMS,),
                   in_specs=[pl.BlockSpec((TMS, K), lambda i: (i, 0)), _resident((N, K))],
                   out_specs=[pl.BlockSpec((TMS, N), lambda i: (i, 0))],
                   out_shape=[jax.ShapeDtypeStruct((L, N), BF16)])[0]


def _mm_tn(a, b, scale, name, groups, b_cols=False, comm=()):
    L, M = a.shape[1], a.shape[2]
    N = b.shape[1] // groups if b_cols else b.shape[2]
    tk = min(L, TK)
    nk = L // tk

    def spec(v, cols):
        if cols:
            return pl.BlockSpec((tk, v.shape[1] // groups), lambda g, k: (k, g))
        if v.shape[0] > 1:
            return pl.BlockSpec((None, tk, v.shape[2]), lambda g, k: (g, k, 0))
        return pl.BlockSpec((None, tk, v.shape[2]), lambda g, k: (0, k, 0))

    def body(a_ref, b_ref, o_ref, acc):
        k = pl.program_id(1)
        p = _dot_tn(a_ref[...], b_ref[...])
        if nk == 1:
            o_ref[...] = (p * scale).astype(BF16)
        else:
            _accumulate(acc, k == 0, p)

            @pl.when(k == nk - 1)
            def _():
                o_ref[...] = (acc[...] * scale).astype(BF16)

    return _pallas(body, [a, b], name=name, grid=(groups, nk),
                   in_specs=[spec(a, False), spec(b, b_cols)],
                   out_specs=[pl.BlockSpec((None, M, N), lambda g, k: (g, 0, 0))],
                   out_shape=[jax.ShapeDtypeStruct((groups, M, N), BF16)],
                   scratch_shapes=[pltpu.VMEM((M, N), F32)], comm=comm)[0]


def _mm_rmsbwd(a_list, w_list, nt, a_cols, x_in, g, dx_out, name, comm=()):
    P = len(a_list)
    G = w_list[0].shape[0]
    L, D = x_in.shape
    mm = _dot_nt if nt else _dot

    def body(*refs):
        a_refs, w_refs = refs[:P], refs[P:2 * P]
        x_ref, g_ref, dxo_ref, dx_ref, dxb_ref, dg_ref = refs[2 * P:]
        i = pl.program_id(0)
        dh = None
        for a_ref, w_ref in zip(a_refs, w_refs):
            for j in range(G):
                if a_cols:
                    kw = a_ref.shape[1] // G
                    term = mm(a_ref[:, j * kw:(j + 1) * kw], w_ref[j])
                else:
                    term = mm(a_ref[j], w_ref[j])
                dh = term if dh is None else dh + term
        xf = x_ref[...]
        r = lax.rsqrt(jnp.mean(xf * xf, axis=-1, keepdims=True) + EPS)
        xh = xf * r
        _accumulate(dg_ref, i == 0, _colsum(dh * xh))
        dxh = dh * g_ref[...]
        dx = dxo_ref[...] + r * (dxh - xh * jnp.mean(dxh * xh, axis=-1, keepdims=True))
        dx_ref[...] = dx
        dxb_ref[...] = dx.astype(BF16)

    row = pl.BlockSpec((TM, D), lambda i: (i, 0))
    vec = pl.BlockSpec((1, D), lambda i: (0, 0))
    if a_cols:
        a_specs = [pl.BlockSpec((TM, a.shape[1]), lambda i: (i, 0)) for a in a_list]
    else:
        a_specs = [pl.BlockSpec((G, TM, a.shape[2]), lambda i: (0, i, 0)) for a in a_list]
    w_specs = [_resident(w.shape) for w in w_list]
    return _pallas(body, [*a_list, *w_list, x_in, g, dx_out], name=name, grid=(L // TM,),
                   in_specs=a_specs + w_specs + [row, vec, row], out_specs=[row, row, vec],
                   out_shape=[jax.ShapeDtypeStruct((L, D), F32), jax.ShapeDtypeStruct((L, D), BF16),
                              jax.ShapeDtypeStruct((1, D), F32)], comm=comm)


def _mix_out(cat, wout, x1, g_next, name):
    L, K = cat.shape
    D = wout.shape[1]

    def body(c_ref, w_ref, x_ref, g_ref, xn_ref, hn_ref):
        xn = x_ref[...] + _dot(c_ref[...], w_ref[...])
        xn_ref[...] = xn
        r = lax.rsqrt(jnp.mean(xn * xn, axis=-1, keepdims=True) + EPS)
        hn_ref[...] = (xn * r * g_ref[...]).astype(BF16)

    row = pl.BlockSpec((TMS, D), lambda i: (i, 0))
    return _pallas(body, [cat, wout, x1, g_next], name=name, grid=(L // TMS,),
                   in_specs=[pl.BlockSpec((TMS, K), lambda i: (i, 0)), pl.BlockSpec((K, D), lambda i: (0, 0)), row,
                             pl.BlockSpec((1, D), lambda i: (0, 0))],
                   out_specs=[row, row],
                   out_shape=[jax.ShapeDtypeStruct((L, D), F32), jax.ShapeDtypeStruct((L, D), BF16)])


def _s5_disc(lr, li, ldt, brc, bic):
    dt = jnp.exp(ldt)
    mag = jnp.exp(lr * dt)
    are = mag * jnp.cos(li * dt)
    aim = mag * jnp.sin(li * dt)
    den = lr * lr + li * li
    nre = are - 1.0
    fre = (nre * lr + aim * li) / den
    fim = (aim * lr - nre * li) / den
    return are, aim, fre * brc - fim * bic, fre * bic + fim * brc


def _s5_params_fwd(lr, li, ldt, brc, bic, crc, cic):
    def body(lr_ref, li_ref, ldt_ref, br_ref, bi_ref, cr_ref, ci_ref, are_ref, aim_ref, bre_ref, bim_ref, cre_ref, cim_ref):
        are, aim, bre, bim = _s5_disc(lr_ref[...], li_ref[...], ldt_ref[...], br_ref[...], bi_ref[...])
        are_ref[...] = are
        aim_ref[...] = aim
        bre_ref[...] = bre.astype(BF16)
        bim_ref[...] = bim.astype(BF16)
        cre_ref[...] = cr_ref[...].astype(BF16)
        cim_ref[...] = ci_ref[...].astype(BF16)

    vec = jax.ShapeDtypeStruct((1, S5_N), F32)
    return pl.pallas_call(
        body, name="s5_params_fwd",
        out_shape=[vec, vec, jax.ShapeDtypeStruct((LANES, S5_N), BF16), jax.ShapeDtypeStruct((LANES, S5_N), BF16),
                   jax.ShapeDtypeStruct((S5_N, LANES), BF16), jax.ShapeDtypeStruct((S5_N, LANES), BF16)],
        compiler_params=pltpu.CompilerParams(vmem_limit_bytes=VMEM_LIMIT))(lr, li, ldt, brc, bic, crc, cic)


def _s5_params_bwd(lr, li, ldt, brc, bic, dare, daim, dbre, dbim):
    def body(lr_ref, li_ref, ldt_ref, br_ref, bi_ref, dare_ref, daim_ref, dbre_ref, dbim_ref,
             glr_ref, gli_ref, gldt_ref, gbr_ref, gbi_ref):
        _, vjp = jax.vjp(_s5_disc, lr_ref[...], li_ref[...], ldt_ref[...], br_ref[...], bi_ref[...])
        glr, gli, gldt, gbr, gbi = vjp((dare_ref[...], daim_ref[...], dbre_ref[...], dbim_ref[...]))
        glr_ref[...] = glr
        gli_ref[...] = gli
        gldt_ref[...] = gldt
        gbr_ref[...] = gbr
        gbi_ref[...] = gbi

    vec = jax.ShapeDtypeStruct((1, S5_N), F32)
    mat = jax.ShapeDtypeStruct((LANES, S5_N), F32)
    return pl.pallas_call(
        body, name="s5_params_bwd", out_shape=[vec, vec, vec, mat, mat],
        compiler_params=pltpu.CompilerParams(vmem_limit_bytes=VMEM_LIMIT))(lr, li, ldt, brc, bic, dare, daim, dbre, dbim)


def _gelu_parts(y):
    th = jnp.tanh(GELU_C0 * (y + GELU_C1 * y * y * y))
    return 0.5 * y * (1.0 + th), th


def _state_rows(q, T):
    return pl.ds(q % S5_HALF, T, stride=S5_HALF)


def _load_tiles(bufs, ct, T, dtype):
    return jnp.concatenate([bufs[q // S5_HALF][_state_rows(q, T), :].astype(dtype) for q in range(4 * ct, 4 * ct + 4)],
                           axis=1)


def _store_tiles(bufs, ct, T, value):
    for k, q in enumerate(range(4 * ct, 4 * ct + 4)):
        bufs[q // S5_HALF][_state_rows(q, T), :] = value[:, k * LANES:(k + 1) * LANES]


def _s5_fwd(u, are, aim, bre, bim, cre, cim, d_skip, wglu, bglu, comm=()):
    L = u.shape[0]
    T = min(TS, L)
    n = L // T

    def body(u_ref, are_ref, aim_ref, bre_ref, bim_ref, cre_ref, cim_ref, d_ref, wg_ref, bg_ref,
             sre_lo, sre_hi, sim_lo, sim_hi, y_ref, o_ref, st_re, st_im):
        i = pl.program_id(0)
        sre, sim = (sre_lo, sre_hi), (sim_lo, sim_hi)

        @pl.when(i == 0)
        def _():
            st_re[...] = jnp.zeros_like(st_re)
            st_im[...] = jnp.zeros_like(st_im)

        uf = u_ref[...]
        ub = uf.astype(BF16)
        for ct in range(4):
            uq = ub[:, ct * LANES:(ct + 1) * LANES]
            win = slice(4 * ct * LANES, 4 * (ct + 1) * LANES)
            _store_tiles(sre, ct, T, _dot(uq, bre_ref[:, win]))
            _store_tiles(sim, ct, T, _dot(uq, bim_ref[:, win]))
        halves = [slice(h * S5_HALF, (h + 1) * S5_HALF) for h in range(2)]
        a_re = [are_ref[hs, :] for hs in halves]
        a_im = [aim_ref[hs, :] for hs in halves]

        def step(t, carry):
            rows = pl.ds(pl.multiple_of(t * S5_HALF, S5_HALF), S5_HALF)
            out = []
            for h in range(2):
                s_re, s_im = carry[2 * h], carry[2 * h + 1]
                n_re = a_re[h] * s_re - a_im[h] * s_im + sre[h][rows, :]
                n_im = a_re[h] * s_im + a_im[h] * s_re + sim[h][rows, :]
                sre[h][rows, :] = n_re
                sim[h][rows, :] = n_im
                out += [n_re, n_im]
            return tuple(out)

        init = (st_re[halves[0], :], st_im[halves[0], :], st_re[halves[1], :], st_im[halves[1], :])
        fin = lax.fori_loop(0, T, step, init, unroll=4)
        for h in range(2):
            st_re[halves[h], :] = fin[2 * h]
            st_im[halves[h], :] = fin[2 * h + 1]
        tiles = []
        for ct in range(4):
            win = slice(4 * ct * LANES, 4 * (ct + 1) * LANES)
            tiles.append(_dot(_load_tiles(sre, ct, T, BF16), cre_ref[win, :])
                         - _dot(_load_tiles(sim, ct, T, BF16), cim_ref[win, :]))
        y = jnp.concatenate(tiles, axis=1) + d_ref[...] * uf
        y_ref[...] = y
        yg, _ = _gelu_parts(y)
        gate = _sigmoid(_dot(yg.astype(BF16), wg_ref[...]) + bg_ref[...])
        o_ref[...] = (yg * gate).astype(BF16)

    const = lambda shape: pl.BlockSpec(shape, lambda i: (0, 0))
    sspec = pl.BlockSpec((T * S5_HALF, LANES), lambda i: (i, 0))
    sshape = jax.ShapeDtypeStruct((L * S5_HALF, LANES), F32)
    chunk = pl.BlockSpec((T, S5_W), lambda i: (i, 0))
    return _pallas(body, [u, are, aim, bre, bim, cre, cim, d_skip, wglu, bglu], name="s5_fwd", grid=(n,),
                   in_specs=[chunk, const((S5_TILES, LANES)), const((S5_TILES, LANES)),
                             const((LANES, S5_N)), const((LANES, S5_N)), const((S5_N, LANES)), const((S5_N, LANES)),
                             const((1, S5_W)), const((S5_W, S5_W)), const((1, S5_W))],
                   out_specs=[sspec] * 4 + [chunk, chunk],
                   out_shape=[sshape] * 4 + [jax.ShapeDtypeStruct((L, S5_W), F32), jax.ShapeDtypeStruct((L, S5_W), BF16)],
                   scratch_shapes=[pltpu.VMEM((S5_TILES, LANES), F32), pltpu.VMEM((S5_TILES, LANES), F32)], comm=comm)


def _s5_bwd(dm, y_pre, u, states, are, aim, bre, bim, cre, cim, d_skip, wglu, bglu, comm=()):
    L = u.shape[0]
    T = min(TS, L)
    n = L // T

    def body(dm_ref, y_ref, u_ref, sre_lo, sre_hi, sim_lo, sim_hi, pre_lo, pre_hi, pim_lo, pim_hi,
             are_ref, aim_ref, bre_ref, bim_ref, cre_ref, cim_ref, d_ref, wg_ref, bg_ref,
             du_ref, dwg_ref, dbg_ref, dd_ref, dcre_ref, dcim_ref, dbre_ref, dbim_ref, dare_ref, daim_ref,
             gre_lo, gre_hi, gim_lo, gim_hi, car_re, car_im):
        i = pl.program_id(0)
        first = i == 0
        sre, sim = (sre_lo, sre_hi), (sim_lo, sim_hi)
        gre, gim = (gre_lo, gre_hi), (gim_lo, gim_hi)
        pre, pim = (pre_lo, pre_hi), (pim_lo, pim_hi)

        @pl.when(first)
        def _():
            car_re[...] = jnp.zeros_like(car_re)
            car_im[...] = jnp.zeros_like(car_im)
            dcre_ref[...] = jnp.zeros_like(dcre_ref)
            dcim_ref[...] = jnp.zeros_like(dcim_ref)
            dbre_ref[...] = jnp.zeros_like(dbre_ref)
            dbim_ref[...] = jnp.zeros_like(dbim_ref)

        y = y_ref[...]
        uf = u_ref[...]
        yg, th = _gelu_parts(y)
        dgelu = 0.5 * (1.0 + th) + 0.5 * y * (1.0 - th * th) * GELU_C0 * (1.0 + 3.0 * GELU_C1 * y * y)
        ygb = yg.astype(BF16)
        sg = _sigmoid(_dot(ygb, wg_ref[...]) + bg_ref[...])
        dout = dm_ref[...].astype(F32)
        dgp = dout * yg * sg * (1.0 - sg)
        dgpb = dgp.astype(BF16)
        dyg = dout * sg + _dot_nt(dgpb, wg_ref[...])
        _accumulate(dwg_ref, first, _dot_tn(ygb, dgpb))
        _accumulate(dbg_ref, first, _colsum(dgp))
        dy = dyg * dgelu
        _accumulate(dd_ref, first, _colsum(dy * uf))
        dyb = dy.astype(BF16)
        ub = uf.astype(BF16)

        for ct in range(4):
            win = slice(4 * ct * LANES, 4 * (ct + 1) * LANES)
            dyq = dyb[:, ct * LANES:(ct + 1) * LANES]
            dcre_ref[win, :] += _dot_tn(_load_tiles(sre, ct, T, BF16), dyq)
            dcim_ref[win, :] -= _dot_tn(_load_tiles(sim, ct, T, BF16), dyq)
            _store_tiles(gre, ct, T, _dot_nt(dyq, cre_ref[win, :]))
            _store_tiles(gim, ct, T, -_dot_nt(dyq, cim_ref[win, :]))

        halves = [slice(h * S5_HALF, (h + 1) * S5_HALF) for h in range(2)]
        a_re = [are_ref[hs, :] for hs in halves]
        a_im = [aim_ref[hs, :] for hs in halves]

        def adjoint(t, h, g_re, g_im):
            rows = pl.ds(pl.multiple_of(t * S5_HALF, S5_HALF), S5_HALF)
            n_re = gre[h][rows, :] + a_re[h] * g_re + a_im[h] * g_im
            n_im = gim[h][rows, :] + a_re[h] * g_im - a_im[h] * g_re
            gre[h][rows, :] = n_re
            gim[h][rows, :] = n_im
            return n_re, n_im

        def step(k, carry):
            out = []
            for h in range(2):
                out += adjoint(T - 1 - k, h, carry[2 * h], carry[2 * h + 1])
            return tuple(out)

        init = (car_re[halves[0], :], car_im[halves[0], :], car_re[halves[1], :], car_im[halves[1], :])
        fin = lax.fori_loop(0, T, step, init, unroll=4)
        keep = (i < n - 1).astype(F32)
        later, earlier = slice(S5_HALF, T * S5_HALF), slice(0, (T - 1) * S5_HALF)
        fold = lambda v: jnp.sum(v.reshape(T - 1, S5_HALF, LANES), axis=0)
        for h in range(2):
            g_re, g_im = fin[2 * h], fin[2 * h + 1]
            car_re[halves[h], :] = g_re
            car_im[halves[h], :] = g_im
            p_re, p_im = pre[h][...] * keep, pim[h][...] * keep
            da_re = fold(gre[h][later, :] * sre[h][earlier, :] + gim[h][later, :] * sim[h][earlier, :])
            da_im = fold(gim[h][later, :] * sre[h][earlier, :] - gre[h][later, :] * sim[h][earlier, :])
            da_re = da_re + g_re * p_re + g_im * p_im
            da_im = da_im + g_im * p_re - g_re * p_im

            @pl.when(first)
            def _():
                dare_ref[halves[h], :] = da_re
                daim_ref[halves[h], :] = da_im

            @pl.when(jnp.logical_not(first))
            def _():
                dare_ref[halves[h], :] += da_re
                daim_ref[halves[h], :] += da_im

        tiles = []
        for ct in range(4):
            uq = ub[:, ct * LANES:(ct + 1) * LANES]
            win = slice(4 * ct * LANES, 4 * (ct + 1) * LANES)
            g_re, g_im = _load_tiles(gre, ct, T, BF16), _load_tiles(gim, ct, T, BF16)
            tiles.append(d_ref[:, ct * LANES:(ct + 1) * LANES] * dy[:, ct * LANES:(ct + 1) * LANES]
                         + _dot_nt(g_re, bre_ref[:, win]) + _dot_nt(g_im, bim_ref[:, win]))
            dbre_ref[:, win] += _dot_tn(uq, g_re)
            dbim_ref[:, win] += _dot_tn(uq, g_im)
        du_ref[...] = jnp.concatenate(tiles, axis=1).astype(BF16)

    rev = lambda i: (n - 1 - i, 0)
    const = lambda shape: pl.BlockSpec(shape, lambda i: (0, 0))
    chunk = pl.BlockSpec((T, S5_W), rev)
    sspec = pl.BlockSpec((T * S5_HALF, LANES), rev)
    pspec = pl.BlockSpec((S5_HALF, LANES), lambda i: (jnp.maximum((n - 1 - i) * T - 1, 0), 0))
    tile = jax.ShapeDtypeStruct((S5_TILES, LANES), F32)
    vec = jax.ShapeDtypeStruct((1, S5_W), F32)
    sbuf = pltpu.VMEM((T * S5_HALF, LANES), F32)
    return _pallas(
        body, [dm, y_pre, u, *states, *states, are, aim, bre, bim, cre, cim, d_skip, wglu, bglu],
        name="s5_bwd", grid=(n,),
        in_specs=[chunk, chunk, chunk] + [sspec] * 4 + [pspec] * 4 + [
            const((S5_TILES, LANES)), const((S5_TILES, LANES)), const((LANES, S5_N)), const((LANES, S5_N)),
            const((S5_N, LANES)), const((S5_N, LANES)), const((1, S5_W)), const((S5_W, S5_W)), const((1, S5_W))],
        out_specs=[chunk, const((S5_W, S5_W)), const((1, S5_W)), const((1, S5_W)), const((S5_N, LANES)),
                   const((S5_N, LANES)), const((LANES, S5_N)), const((LANES, S5_N)), const((S5_TILES, LANES)),
                   const((S5_TILES, LANES))],
        out_shape=[jax.ShapeDtypeStruct((L, S5_W), BF16), jax.ShapeDtypeStruct((S5_W, S5_W), F32), vec, vec,
                   jax.ShapeDtypeStruct((S5_N, LANES), F32), jax.ShapeDtypeStruct((S5_N, LANES), F32),
                   jax.ShapeDtypeStruct((LANES, S5_N), F32), jax.ShapeDtypeStruct((LANES, S5_N), F32), tile, tile],
        scratch_shapes=[sbuf, sbuf, sbuf, sbuf, pltpu.VMEM((S5_TILES, LANES), F32), pltpu.VMEM((S5_TILES, LANES), F32)],
        comm=comm)


_EYE8 = np.eye(8, dtype=np.float32)


def _compact_b(b):
    return jnp.einsum("akpc,kj->jcakp", b.reshape(4, 8, S5_P, S5_GC), _EYE8).reshape(LANES, S5_N)


def _uncompact_b(m):
    return jnp.einsum("kcakp->akpc", m.reshape(8, S5_GC, 4, 8, S5_P)).reshape(S5_G, S5_P, S5_GC)


_HEAD_MEAN = np.kron(np.eye(CONV_W // CONV_HD, dtype=np.float32), np.full((CONV_HD, CONV_HD), 1.0 / CONV_HD, np.float32))


def _head_mean(v, m):
    hi = v.astype(BF16)
    lo = (v - hi.astype(F32)).astype(BF16)
    return _dot(hi, m) + _dot(lo, m)


def _head_norm(zc, m):
    d = zc - _head_mean(zc, m)
    rstd = lax.rsqrt(_head_mean(d * d, m) + EPS)
    return d * rstd, rstd


def _taps_by_phase(first):
    groups = {}
    for k in range(CONV_K):
        m, s = divmod(first + k, 8)
        groups.setdefault(s, []).append((m, k))
    return groups


def _causal_taps(buf, w_ref, r0, first, flip):
    acc = None
    for s, taps in sorted(_taps_by_phase(first).items()):
        rows = CONV_SB + (8 if s else 0)
        y = None
        for m, k in taps:
            kk = CONV_K - 1 - k if flip else k
            term = w_ref[kk:kk + 1, :] * buf[pl.ds(r0 + 8 * m, rows), :]
            y = term if y is None else y + term
        y = y[s:s + CONV_SB, :]
        acc = y if acc is None else acc + y
    return acc


def _conv_fwd(u, o_s5, wdw, bdw, lng, lnb, mavg, comm=()):
    L = u.shape[0]
    T = min(TS, L)
    n = L // T
    first_tap = HALO - (CONV_K - 1)

    def body(v1_ref, v2_ref, s5_ref, w_ref, b_ref, g_ref, be_ref, m_ref, zc_ref, o_ref, zbuf):
        i = pl.program_id(0)

        @pl.when(i == 0)
        def _():
            zbuf[0:HALO, :] = jnp.zeros((HALO, CONV_W), F32)

        zbuf[HALO:HALO + T, :] = v1_ref[...] * _sigmoid(v2_ref[...])
        for r0 in range(0, T, CONV_SB):
            zc_ref[r0:r0 + CONV_SB, :] = b_ref[...] + _causal_taps(zbuf, w_ref, r0, first_tap, False)
        zbuf[0:HALO, :] = zbuf[T:T + HALO, :]
        zn, _ = _head_norm(zc_ref[...], m_ref[...])
        zz = zn * g_ref[...] + be_ref[...]
        o_ref[:, 0:S5_W] = s5_ref[...]
        o_ref[:, S5_W:S5_W + CONV_W] = (zz * _sigmoid(zz)).astype(BF16)

    const = lambda shape: pl.BlockSpec(shape, lambda i: (0, 0))
    vec = const((1, CONV_W))
    return _pallas(body, [u, u, o_s5, wdw, bdw, lng, lnb, mavg], name="conv_fwd", grid=(n,),
                   in_specs=[pl.BlockSpec((T, CONV_W), lambda i: (i, 1)), pl.BlockSpec((T, CONV_W), lambda i: (i, 2)),
                             pl.BlockSpec((T, S5_W), lambda i: (i, 0)), const((CONV_K, CONV_W)), vec, vec, vec,
                             const((CONV_W, CONV_W))],
                   out_specs=[pl.BlockSpec((T, CONV_W), lambda i: (i, 0)),
                              pl.BlockSpec((T, S5_W + CONV_W), lambda i: (i, 0))],
                   out_shape=[jax.ShapeDtypeStruct((L, CONV_W), F32), jax.ShapeDtypeStruct((L, S5_W + CONV_W), BF16)],
                   scratch_shapes=[pltpu.VMEM((T + HALO, CONV_W), F32)], comm=comm)


def _conv_bwd(dm, zc, u, du_s5, wdw, lng, lnb, mavg, comm=()):
    L = u.shape[0]
    T = min(TS, L)
    n = L // T
    hb = T // HALO
    first_tap = HALO - (CONV_K - 1)

    def body(dm_ref, zc_ref, v1_ref, v2_ref, p1_ref, p2_ref, s5_ref, w_ref, g_ref, be_ref, m_ref,
             du_ref, dw_ref, db_ref, dg_ref, dbe_ref, zbuf, dzbuf, head):
        i = pl.program_id(0)
        first = i == 0

        @pl.when(first)
        def _():
            head[...] = jnp.zeros_like(head)
            dw_ref[...] = jnp.zeros_like(dw_ref)

        zn, rstd = _head_norm(zc_ref[...], m_ref[...])
        zz = zn * g_ref[...] + be_ref[...]
        sg = _sigmoid(zz)
        dzz = dm_ref[...].astype(F32) * sg * (1.0 + zz * (1.0 - sg))
        _accumulate(dbe_ref, first, _colsum(dzz))
        _accumulate(dg_ref, first, _colsum(dzz * zn))
        dzn = dzz * g_ref[...]
        dzc = rstd * (dzn - _head_mean(dzn, m_ref[...]) - zn * _head_mean(dzn * zn, m_ref[...]))
        _accumulate(db_ref, first, _colsum(dzc))

        dzbuf[0:T, :] = dzc
        dzbuf[T:T + HALO, :] = head[...]
        head[...] = dzbuf[0:HALO, :]
        keep = (i < n - 1).astype(F32)
        zbuf[0:HALO, :] = p1_ref[...] * _sigmoid(p2_ref[...]) * keep
        zbuf[HALO:HALO + T, :] = v1_ref[...] * _sigmoid(v2_ref[...])
        du_ref[:, 0:S5_W] = s5_ref[...]

        for r0 in range(0, T, CONV_SB):
            rows = slice(r0, r0 + CONV_SB)
            dzc_b = dzbuf[rows, :]
            for s, taps in sorted(_taps_by_phase(first_tap).items()):
                pad = ([jnp.zeros((s, CONV_W), F32)] if s else []) + [dzc_b] + ([jnp.zeros((8 - s, CONV_W), F32)] if s else [])
                shifted = jnp.concatenate(pad, axis=0) if s else dzc_b
                n_rows = shifted.shape[0]
                for m, k in taps:
                    prod = shifted * zbuf[pl.ds(r0 + 8 * m, n_rows), :]
                    dw_ref[8 * k:8 * k + 8, :] += jnp.sum(prod.reshape(n_rows // 8, 8, CONV_W), axis=0)
            dz = _causal_taps(dzbuf, w_ref, r0, 0, True)
            v1 = v1_ref[rows, :]
            sg2 = _sigmoid(v2_ref[rows, :])
            du_ref[rows, S5_W:S5_W + CONV_W] = (dz * sg2).astype(BF16)
            du_ref[rows, S5_W + CONV_W:S5_W + 2 * CONV_W] = (dz * v1 * sg2 * (1.0 - sg2)).astype(BF16)

    rev = lambda c: (lambda i: (n - 1 - i, c))
    prev = lambda c: (lambda i: (jnp.maximum((n - 1 - i) * hb - 1, 0), c))
    const = lambda shape: pl.BlockSpec(shape, lambda i: (0, 0))
    vec = const((1, CONV_W))
    vshape = jax.ShapeDtypeStruct((1, CONV_W), F32)
    return _pallas(
        body, [dm, zc, u, u, u, u, du_s5, wdw, lng, lnb, mavg], name="conv_bwd", grid=(n,),
        in_specs=[pl.BlockSpec((T, CONV_W), rev(1)), pl.BlockSpec((T, CONV_W), rev(0)),
                  pl.BlockSpec((T, CONV_W), rev(1)), pl.BlockSpec((T, CONV_W), rev(2)),
                  pl.BlockSpec((HALO, CONV_W), prev(1)), pl.BlockSpec((HALO, CONV_W), prev(2)),
                  pl.BlockSpec((T, S5_W), rev(0)), const((CONV_K, CONV_W)), vec, vec, const((CONV_W, CONV_W))],
        out_specs=[pl.BlockSpec((T, S5_W + 2 * CONV_W), rev(0)), const((8 * HALO, CONV_W)), vec, vec, vec],
        out_shape=[jax.ShapeDtypeStruct((L, S5_W + 2 * CONV_W), BF16), jax.ShapeDtypeStruct((8 * HALO, CONV_W), F32),
                   vshape, vshape, vshape],
        scratch_shapes=[pltpu.VMEM((T + HALO, CONV_W), F32), pltpu.VMEM((T + HALO, CONV_W), F32),
                        pltpu.VMEM((HALO, CONV_W), F32)], comm=comm)


def _gather_all(v, comm=()):
    rows, cols = v.shape

    def body(x_ref, out_ref, send_sems, recv_sems, local_sem):
        x, y, c, chips = _position()
        me, sibling = (x, y, c), (x, y, 1 - c)

        def block(px, py, pc):
            return out_ref.at[pl.ds((4 * px + 2 * py + pc) * rows, rows), :]

        def copy(k, blk, to, src=None):
            return pltpu.make_async_remote_copy(
                src_ref=block(*blk) if src is None else src, dst_ref=block(*blk), send_sem=send_sems.at[k],
                recv_sem=recv_sems.at[k], device_id=to, device_id_type=MESH)

        mine = pltpu.make_async_copy(x_ref, block(*me), local_sem)
        mine.start()
        first = [copy(0, me, sibling, src=x_ref)]
        first += [copy(1 + j, me, (*chip, c), src=x_ref) for j, chip in enumerate(chips)]
        for cp in first:
            cp.start()
        passed = [copy(4 + j, (*chip, c), sibling) for j, chip in enumerate(chips)]
        for j, chip in enumerate(chips):
            copy(1 + j, (*chip, c), me).wait_recv()
            passed[j].start()
        copy(0, sibling, me).wait_recv()
        for j, chip in enumerate(chips):
            copy(4 + j, (*chip, 1 - c), me).wait_recv()
        for cp in first + passed:
            cp.wait_send()
        mine.wait()

    whole = pl.BlockSpec(memory_space=pltpu.VMEM)
    return _pallas(body, [v], name="gather_small", grid=(1,), in_specs=[whole], out_specs=[whole],
                   out_shape=[jax.ShapeDtypeStruct((N_DEV * rows, cols), v.dtype)],
                   scratch_shapes=[pltpu.SemaphoreType.DMA((7,)), pltpu.SemaphoreType.DMA((7,)), pltpu.SemaphoreType.DMA],
                   comm=comm)[0]


def _adamw(w, g, m, v):
    m = ADAM_B1 * m + (1.0 - ADAM_B1) * g
    v = ADAM_B2 * v + (1.0 - ADAM_B2) * jnp.square(g)
    m_hat = m / (1.0 - ADAM_B1 ** ADAM_STEP)
    v_hat = v / (1.0 - ADAM_B2 ** ADAM_STEP)
    return -ADAM_LR * (m_hat / (jnp.sqrt(v_hat) + ADAM_EPS) + ADAM_WD * w), m, v


def _sum_slots(recv, name, comm=()):
    _, rows, cols = recv.shape
    tr = _row_tile(rows, cols)

    def body(r_ref, o_ref):
        acc = r_ref[0].astype(F32)
        for s in range(1, N_CHIPS):
            acc = acc + r_ref[s].astype(F32)
        o_ref[...] = acc

    return _pallas(body, [recv], name=name, grid=(rows // tr,),
                   in_specs=[pl.BlockSpec((N_CHIPS, tr, cols), lambda i: (0, i, 0))],
                   out_specs=[pl.BlockSpec((tr, cols), lambda i: (i, 0))],
                   out_shape=[jax.ShapeDtypeStruct((rows, cols), F32)], comm=comm)[0]


def _add_halves(grad, theirs, name):
    slots, rows, cols = theirs.shape
    tr = _row_tile(rows, cols * slots)
    nb = rows // tr

    def body(lo_ref, hi_ref, b_ref, o_ref):
        mine = jnp.where(lax.axis_index("c") == 0, lo_ref[...], hi_ref[...])
        o_ref[...] = (mine.astype(F32) + b_ref[...].astype(F32)).astype(o_ref.dtype)

    spec = pl.BlockSpec((slots, tr, cols), lambda i: (0, i, 0))
    upper = pl.BlockSpec((slots, tr, cols), lambda i: (0, nb + i, 0))
    return _pallas(body, [grad, grad, theirs], name=name, grid=(nb,), in_specs=[spec, upper, spec], out_specs=[spec],
                   out_shape=[jax.ShapeDtypeStruct(theirs.shape, theirs.dtype)])[0]


def _adamw_sharded(w, parts, m, v, name, comm=()):
    rows, cols = w.shape
    tr = _row_tile(rows, cols)
    n = len(parts)

    def body(w_ref, *refs):
        p_refs, (m_ref, v_ref, g_ref, d_ref, nm_ref, nv_ref) = refs[:n], refs[n:]
        g = p_refs[0][...]
        for p_ref in p_refs[1:]:
            g = g + p_ref[...]
        g_ref[...] = g
        d_ref[...], nm_ref[...], nv_ref[...] = _adamw(w_ref[...], g, m_ref[...], v_ref[...])

    spec = pl.BlockSpec((tr, cols), lambda i: (i, 0))
    shape = jax.ShapeDtypeStruct((rows, cols), F32)
    return _pallas(body, [w, *parts, m, v], name=name, grid=(rows // tr,), in_specs=[spec] * (n + 3),
                   out_specs=[spec] * 4, out_shape=[shape] * 4, comm=comm)


def _adamw_small(w, gathered, m, v):
    rows, cols = w.shape

    def body(w_ref, a_ref, m_ref, v_ref, g_ref, d_ref, nm_ref, nv_ref):
        g = a_ref[0:rows, :]
        for dev in range(1, N_DEV):
            g = g + a_ref[dev * rows:(dev + 1) * rows, :]
        g_ref[...] = g
        d_ref[...], nm_ref[...], nv_ref[...] = _adamw(w_ref[...], g, m_ref[...], v_ref[...])

    shape = jax.ShapeDtypeStruct((rows, cols), F32)
    return pl.pallas_call(
        body, name="adamw_small", out_shape=[shape] * 4,
        compiler_params=pltpu.CompilerParams(vmem_limit_bytes=VMEM_LIMIT))(w, gathered, m, v)


PACK_TILE = 8 * LANES


def _pack_small(vals, last_row):
    rows = []
    for name in SMALL:
        flat = vals[name].reshape(-1).astype(F32)
        rows.append(jnp.pad(flat, (0, -flat.size % PACK_TILE)).reshape(-1, LANES))
    rows.append(jnp.pad(last_row, ((0, 7), (0, 0))))
    return jnp.concatenate(rows, axis=0)


def _unpack_small(packed, like):
    out, r = {}, 0
    for name in SMALL:
        size = like[name].size
        out[name] = packed[r:r + -(-size // LANES)].reshape(-1)[:size].reshape(like[name].shape)
        r += 8 * -(-size // PACK_TILE)
    return out, packed[r, 0]


def _shard2d(name, v):
    v = v.reshape(v.shape[-2:])
    return v.T if name in FFN_T else v


def _unshard(name, v, shape):
    return (v.T if name in FFN_T else v).reshape(shape)


def _train_step(x3, tgt3, wts, ms, vs):
    x, tgt = x3[0], tgt3[0]
    L, D = x.shape
    row = lambda v: v.reshape(1, -1)
    shards = {k: _shard2d(k, wts[k]) for k in SHARDED}
    sends = {k: shards[k] if k == "conv_w_dw" else _cast_bf16(shards[k], "cast_" + k) for k in SHARDED}
    gat = {k: _Gather(sends[k]) for k in SHARDED}
    w = lambda k: gat[k].result[0]

    s = {k: wts[k] for k in SMALL}
    lr, li = s["s5_lam_re"].reshape(1, S5_N), s["s5_lam_im"].reshape(1, S5_N)
    ldt = jnp.repeat(s["s5_log_dt"].reshape(S5_G), S5_P).reshape(1, S5_N)
    brc, bic = _compact_b(s["s5_b_re"].reshape(S5_G, S5_P, S5_GC)), _compact_b(s["s5_b_im"].reshape(S5_G, S5_P, S5_GC))
    crc = _compact_b(s["s5_c_re"].reshape(S5_G, S5_GC, S5_P).transpose(0, 2, 1)).T
    cic = _compact_b(s["s5_c_im"].reshape(S5_G, S5_GC, S5_P).transpose(0, 2, 1)).T
    d_skip, b_glu = row(s["s5_d"]), row(s["s5_b_glu"])
    b_dw, ln_g, ln_b = row(s["conv_b_dw"]), row(s["conv_ln_g"]), row(s["conv_ln_b"])
    g1, gm, g2, gf = row(s["ffn1_norm"]), row(s["mix_norm"]), row(s["ffn2_norm"]), row(s["final_norm"])
    mavg = jnp.asarray(_HEAD_MEAN, dtype=BF16)

    h1 = _rms_fwd(x, g1, "rms1", comm=[gat["ffn1_w_gate"], gat["ffn1_w_up"]])
    a1, b1, act1 = _ffn_up(h1, w("ffn1_w_gate"), w("ffn1_w_up"), "ffn1_up", comm=[gat["ffn1_w_down"]])
    x1, h2 = _ffn_down(act1, w("ffn1_w_down"), x, gm, "ffn1_down",
                       comm=[gat["w_in"], gat["s5_w_glu"], gat["conv_w_dw"], gat["w_out"]])
    u = _mm_grouped(h2, w("w_in"), "in_proj")
    are, aim, bre, bim, cre, cim = _s5_params_fwd(lr, li, ldt, brc, bic, crc, cic)
    are_t, aim_t = are.reshape(S5_TILES, LANES), aim.reshape(S5_TILES, LANES)
    w_glu = w("s5_w_glu").reshape(S5_W, S5_W)
    *states, y_pre, o_s5 = _s5_fwd(u, are_t, aim_t, bre, bim, cre, cim, d_skip, w_glu, b_glu,
                                   comm=[gat["ffn2_w_gate"], gat["ffn2_w_up"]])
    w_dw = w("conv_w_dw").transpose(1, 0, 2).reshape(CONV_K, CONV_W)
    zc, cat = _conv_fwd(u, o_s5, w_dw, b_dw, ln_g, ln_b, mavg, comm=[gat["ffn2_w_down"]])
    w_out = w("w_out").reshape(-1, D)
    x2, h3 = _mix_out(cat, w_out, x1, g2, "mix_out")
    a2, b2, act2 = _ffn_up(h3, w("ffn2_w_gate"), w("ffn2_w_up"), "ffn2_up")
    dx3, dx3b, loss_part, d_gf = _ffn_down_loss(act2, w("ffn2_w_down"), x2, gf, tgt, "ffn2_down_loss")

    gs, sc, waiting = {"final_norm": d_gf}, {}, []

    def grad(key, g):
        if g.shape[1] % (2 * BF16_ROWS) == 0 and g.dtype == BF16:
            waiting.append(_SwapHalf(g, key))
        else:
            sc[key] = _Scatter(g)
            waiting.append(sc[key])

    def carry(call, *args, **kw):
        ops = list(waiting)
        waiting.clear()
        res = call(*args, comm=ops, **kw)
        for op in ops:
            if isinstance(op, _SwapHalf):
                sc[op.key] = _Scatter(_add_halves(op.ins[0], op.result[0], "add_" + op.key))
                waiting.append(sc[op.key])
        return res

    da2, db2 = _ffn_bwd_act(dx3b, w("ffn2_w_down"), a2, b2, "ffn2_bwd_act")
    grad("ffn2_w_down", _mm_tn(act2, dx3b[None], 0.5, "ffn2_dwd", N_CHIPS))
    grad("ffn2_w_gate", carry(_mm_tn, da2, h3[None], 1.0, "ffn2_dwg", N_CHIPS))
    grad("ffn2_w_up", carry(_mm_tn, db2, h3[None], 1.0, "ffn2_dwu", N_CHIPS))
    dx2, dx2b, gs["ffn2_norm"] = carry(_mm_rmsbwd, [da2, db2], [w("ffn2_w_gate"), w("ffn2_w_up")], False, False, x2, g2,
                                       dx3, "ffn2_bwd_dx")

    dm = _mm_nt(dx2b, w_out, "mix_bwd")
    grad("w_out", _mm_tn(cat[None], dx2b[None], 1.0, "dwout", 1).reshape(N_CHIPS, -1, D))
    (du_s5, d_wglu, gs["s5_b_glu"], gs["s5_d"], d_crc, d_cic, d_bre, d_bim, d_are, d_aim) = carry(
        _s5_bwd, dm, y_pre, u, states, are_t, aim_t, bre, bim, cre, cim, d_skip, w_glu, b_glu)
    grad("s5_w_glu", d_wglu.astype(BF16).reshape(N_CHIPS, -1, S5_W))
    g_lr, g_li, g_ldt, g_brc, g_bic = _s5_params_bwd(lr, li, ldt, brc, bic, d_are.reshape(1, S5_N),
                                                     d_aim.reshape(1, S5_N), d_bre, d_bim)
    gs["s5_lam_re"], gs["s5_lam_im"] = g_lr, g_li
    gs["s5_log_dt"] = jnp.sum(g_ldt.reshape(S5_G, S5_P), axis=1)
    gs["s5_b_re"], gs["s5_b_im"] = _uncompact_b(g_brc), _uncompact_b(g_bic)
    gs["s5_c_re"] = _uncompact_b(d_crc.T).transpose(0, 2, 1)
    gs["s5_c_im"] = _uncompact_b(d_cic.T).transpose(0, 2, 1)
    du, d_wdw, gs["conv_b_dw"], gs["conv_ln_g"], gs["conv_ln_b"] = carry(_conv_bwd, dm, zc, u, du_s5, w_dw, ln_g, ln_b, mavg)
    d_wdw = jnp.sum(d_wdw.reshape(HALO, 8, CONV_W), axis=1)[:CONV_K]
    grad("conv_w_dw", d_wdw.reshape(CONV_K, N_CHIPS, -1).transpose(1, 0, 2))
    grad("w_in", carry(_mm_tn, h2[None], du, 1.0, "dwin", N_CHIPS, b_cols=True))
    dx1, dx1b, gs["mix_norm"] = carry(_mm_rmsbwd, [du], [w("w_in")], True, True, x1, gm, dx2, "in_proj_bwd")

    da1, db1 = carry(_ffn_bwd_act, dx1b, w("ffn1_w_down"), a1, b1, "ffn1_bwd_act")
    grad("ffn1_w_down", _mm_tn(act1, dx1b[None], 0.5, "ffn1_dwd", N_CHIPS))
    grad("ffn1_w_gate", carry(_mm_tn, da1, h1[None], 1.0, "ffn1_dwg", N_CHIPS))
    grad("ffn1_w_up", carry(_mm_tn, db1, h1[None], 1.0, "ffn1_dwu", N_CHIPS))
    grad_x, _, gs["ffn1_norm"] = carry(_mm_rmsbwd, [da1, db1], [w("ffn1_w_gate"), w("ffn1_w_up")], False, False, x, g1,
                                       dx1, "ffn1_bwd_dx")

    out = {}
    gsmall = {k: gs[k].reshape(wts[k].shape) for k in SMALL}
    zero_row = jnp.zeros((1, LANES), F32)
    g_all = carry(_gather_all, _pack_small(gsmall, loss_part))
    res = _adamw_small(_pack_small(s, zero_row), g_all, _pack_small({k: ms[k] for k in SMALL}, zero_row),
                       _pack_small({k: vs[k] for k in SMALL}, zero_row))
    unpacked = [_unpack_small(r, s) for r in res]
    loss = unpacked[0][1]
    for k in SMALL:
        out[k] = [u_[0][k] for u_ in unpacked]

    order = ("ffn2_w_down", "ffn2_w_gate", "ffn2_w_up", "w_out", "s5_w_glu", "conv_w_dw", "w_in", "ffn1_w_down",
             "ffn1_w_gate", "ffn1_w_up")
    back = {}
    for k in order:
        part = carry(_sum_slots, sc[k].result[0], "sum_" + k)
        back[k] = _SwapBack(part) if part.shape != shards[k].shape else _Swap(part)
        waiting.append(back[k])
    for k in order:
        parts = [back[k].result[0]] if isinstance(back[k], _SwapBack) else [back[k].ins[0], back[k].result[0]]
        res = carry(_adamw_sharded, shards[k], parts, _shard2d(k, ms[k]), _shard2d(k, vs[k]), "adamw_" + k)
        out[k] = [_unshard(k, r, wts[k].shape) for r in res]
    return loss, grad_x[None], out


def kernel(x, ffn1_norm, ffn1_w_gate, ffn1_w_up, ffn1_w_down, mix_norm, w_in, s5_lam_re, s5_lam_im, s5_log_dt, s5_b_re, s5_b_im, s5_c_re, s5_c_im, s5_d, s5_w_glu, s5_b_glu, conv_w_dw, conv_b_dw, conv_ln_g, conv_ln_b, w_out, ffn2_norm, ffn2_w_gate, ffn2_w_up, ffn2_w_down, final_norm, loss_target, m_ffn1_norm, m_ffn1_w_gate, m_ffn1_w_up, m_ffn1_w_down, m_mix_norm, m_w_in, m_s5_lam_re, m_s5_lam_im, m_s5_log_dt, m_s5_b_re, m_s5_b_im, m_s5_c_re, m_s5_c_im, m_s5_d, m_s5_w_glu, m_s5_b_glu, m_conv_w_dw, m_conv_b_dw, m_conv_ln_g, m_conv_ln_b, m_w_out, m_ffn2_norm, m_ffn2_w_gate, m_ffn2_w_up, m_ffn2_w_down, m_final_norm, v_ffn1_norm, v_ffn1_w_gate, v_ffn1_w_up, v_ffn1_w_down, v_mix_norm, v_w_in, v_s5_lam_re, v_s5_lam_im, v_s5_log_dt, v_s5_b_re, v_s5_b_im, v_s5_c_re, v_s5_c_im, v_s5_d, v_s5_w_glu, v_s5_b_glu, v_conv_w_dw, v_conv_b_dw, v_conv_ln_g, v_conv_ln_b, v_w_out, v_ffn2_norm, v_ffn2_w_gate, v_ffn2_w_up, v_ffn2_w_down, v_final_norm):
    given = dict(locals())
    wts = {k: given[k] for k in WEIGHTS}
    ms = {k: given["m_" + k] for k in WEIGHTS}
    vs = {k: given["v_" + k] for k in WEIGHTS}
    loss, grad_x, out = _train_step(x, loss_target, wts, ms, vs)
    return (loss, grad_x, *[out[k][0] for k in WEIGHTS], *[out[k][1] for k in WEIGHTS],
            *[out[k][2] for k in WEIGHTS], *[out[k][3] for k in WEIGHTS])
```

```python
import functools

import jax
import jax.numpy as jnp
import numpy as np
from jax import lax
from jax.experimental import pallas as pl
from jax.experimental.pallas import tpu as pltpu

F32, BF16 = jnp.float32, jnp.bfloat16
MESH = pl.DeviceIdType.MESH

EPS = 1e-6
ADAM_LR, ADAM_B1, ADAM_B2, ADAM_EPS, ADAM_WD, ADAM_STEP = 0.001, 0.9, 0.999, 1e-08, 0.01, 10

N_CHIPS = 4
N_DEV = 8
LANES = 128
BF16_ROWS = 16
S5_W, S5_G, S5_GC, S5_P = 512, 32, 16, 64
S5_N = S5_G * S5_P
S5_TILES = S5_N // LANES
S5_HALF = 8
CONV_W, CONV_K, CONV_HD = 512, 31, 64
HALO = 32
CONV_SB = 32
TM = 512
TMS = 1024
TK = 2048
TS = 256
VMEM_LIMIT = 48 << 20
GELU_C0, GELU_C1 = 0.7978845608028654, 0.044715

FFN_T = ("ffn1_w_gate", "ffn1_w_up", "ffn2_w_gate", "ffn2_w_up")
SHARDED = ("ffn1_w_gate", "ffn1_w_up", "ffn1_w_down", "w_in", "s5_w_glu", "conv_w_dw", "w_out",
           "ffn2_w_gate", "ffn2_w_up", "ffn2_w_down")
SMALL = ("ffn1_norm", "mix_norm", "s5_lam_re", "s5_lam_im", "s5_log_dt", "s5_b_re", "s5_b_im", "s5_c_re",
         "s5_c_im", "s5_d", "s5_b_glu", "conv_b_dw", "conv_ln_g", "conv_ln_b", "ffn2_norm", "final_norm")
WEIGHTS = ("ffn1_norm", "ffn1_w_gate", "ffn1_w_up", "ffn1_w_down", "mix_norm", "w_in", "s5_lam_re", "s5_lam_im",
           "s5_log_dt", "s5_b_re", "s5_b_im", "s5_c_re", "s5_c_im", "s5_d", "s5_w_glu", "s5_b_glu", "conv_w_dw",
           "conv_b_dw", "conv_ln_g", "conv_ln_b", "w_out", "ffn2_norm", "ffn2_w_gate", "ffn2_w_up", "ffn2_w_down",
           "final_norm")


def _dot(a, b):
    return jnp.dot(a, b, preferred_element_type=F32)


def _dot_nt(a, b):
    return lax.dot_general(a, b, (((1,), (1,)), ((), ())), preferred_element_type=F32)


def _dot_tn(a, b):
    return lax.dot_general(a, b, (((0,), (0,)), ((), ())), preferred_element_type=F32)


def _colsum(v):
    return jnp.sum(v, axis=0, keepdims=True)


def _sigmoid(v):
    return 1.0 / (1.0 + jnp.exp(-v))


def _accumulate(ref, first, value):
    @pl.when(first)
    def _():
        ref[...] = value

    @pl.when(jnp.logical_not(first))
    def _():
        ref[...] += value


def _position():
    x, y, c = lax.axis_index("x"), lax.axis_index("y"), lax.axis_index("c")
    return x, y, c, [(1 - x, y), (x, 1 - y), (1 - x, 1 - y)]


def _remote(src, dst, sems, send, recv, device):
    return pltpu.make_async_remote_copy(src_ref=src, dst_ref=dst, send_sem=sems.at[send], recv_sem=sems.at[recv],
                                        device_id=device, device_id_type=MESH)


class _Gather:
    def __init__(self, shard):
        self.ins = [shard]
        self.outs = [jax.ShapeDtypeStruct((N_CHIPS,) + shard.shape, shard.dtype)]
        self.rows = shard.shape[0]
        self.halve = shard.dtype == BF16 and self.rows % (2 * BF16_ROWS) == 0
        self.n_sem = 13 if self.halve else 7
        self.result = None

    def _copies(self, ins, outs, sems, s0, pos):
        x, y, c, chips = pos
        src, dst = ins[0], outs[0]
        me = 2 * x + y
        if self.halve:
            hr = self.rows // 2
            mine, theirs = pl.ds(c * hr, hr), pl.ds((1 - c) * hr, hr)
            part = lambda slot, rows: dst.at[slot, rows]
            my_src = src.at[mine]
        else:
            mine = theirs = None
            part = lambda slot, rows: dst.at[slot]
            my_src = src
        slot = lambda j: 2 * chips[j][0] + chips[j][1]
        local = lambda: pltpu.make_async_copy(src, dst.at[me], sems.at[s0])
        send = lambda j: _remote(my_src, part(me, mine), sems, s0 + 1 + j, s0 + 4 + j, (*chips[j], c))
        land = lambda j: _remote(my_src, part(slot(j), mine), sems, s0 + 1 + j, s0 + 4 + j, (*chips[j], c))
        fwd = lambda j: _remote(part(slot(j), mine), part(slot(j), mine), sems, s0 + 7 + j, s0 + 10 + j, (x, y, 1 - c))
        got = lambda j: _remote(part(slot(j), theirs), part(slot(j), theirs), sems, s0 + 7 + j, s0 + 10 + j,
                                (x, y, 1 - c))
        return local, send, land, fwd, got

    def start(self, ins, outs, sems, s0, pos):
        local, send, _, _, _ = self._copies(ins, outs, sems, s0, pos)
        local().start()
        for j in range(N_CHIPS - 1):
            send(j).start()

    def finish(self, ins, outs, sems, s0, pos):
        local, send, land, fwd, got = self._copies(ins, outs, sems, s0, pos)
        others = range(N_CHIPS - 1)
        for j in others:
            land(j).wait_recv()
            if self.halve:
                fwd(j).start()
        for j in others:
            if self.halve:
                got(j).wait_recv()
        for j in others:
            send(j).wait_send()
            if self.halve:
                fwd(j).wait_send()
        local().wait()


class _Scatter:
    def __init__(self, grad):
        self.ins = [grad]
        self.outs = [jax.ShapeDtypeStruct(grad.shape, grad.dtype)]
        self.n_sem = 7
        self.result = None

    def _copies(self, ins, outs, sems, s0, pos):
        x, y, c, chips = pos
        src, dst = ins[0], outs[0]
        me = 2 * x + y
        slot = lambda j: 2 * chips[j][0] + chips[j][1]
        local = lambda: pltpu.make_async_copy(src.at[me], dst.at[me], sems.at[s0])
        send = lambda j: _remote(src.at[slot(j)], dst.at[me], sems, s0 + 1 + j, s0 + 4 + j, (*chips[j], c))
        land = lambda j: _remote(src.at[me], dst.at[slot(j)], sems, s0 + 1 + j, s0 + 4 + j, (*chips[j], c))
        return local, send, land

    def start(self, ins, outs, sems, s0, pos):
        local, send, _ = self._copies(ins, outs, sems, s0, pos)
        local().start()
        for j in range(N_CHIPS - 1):
            send(j).start()

    def finish(self, ins, outs, sems, s0, pos):
        local, send, land = self._copies(ins, outs, sems, s0, pos)
        for j in range(N_CHIPS - 1):
            land(j).wait_recv()
        for j in range(N_CHIPS - 1):
            send(j).wait_send()
        local().wait()


class _Swap:
    def __init__(self, part):
        self.ins = [part]
        self.outs = [jax.ShapeDtypeStruct(part.shape, part.dtype)]
        self.n_sem = 2
        self.result = None

    def _copy(self, ins, outs, sems, s0, pos):
        x, y, c, _ = pos
        return _remote(ins[0], outs[0], sems, s0, s0 + 1, (x, y, 1 - c))

    def start(self, ins, outs, sems, s0, pos):
        self._copy(ins, outs, sems, s0, pos).start()

    def finish(self, ins, outs, sems, s0, pos):
        self._copy(ins, outs, sems, s0, pos).wait()


class _SwapHalf:
    def __init__(self, grad, key):
        slots, rows, cols = grad.shape
        half = jax.ShapeDtypeStruct((slots, rows // 2, cols), grad.dtype)
        self.ins, self.outs, self.key = [grad], [half, half], key
        self.hr = rows // 2
        self.n_sem = 3
        self.result = None

    def _copies(self, ins, outs, sems, s0, pos):
        x, y, c, _ = pos
        mine, theirs = pl.ds(c * self.hr, self.hr), pl.ds((1 - c) * self.hr, self.hr)
        local = pltpu.make_async_copy(ins[0].at[:, mine], outs[0], sems.at[s0])
        remote = _remote(ins[0].at[:, theirs], outs[1], sems, s0 + 1, s0 + 2, (x, y, 1 - c))
        return local, remote

    def start(self, ins, outs, sems, s0, pos):
        for cp in self._copies(ins, outs, sems, s0, pos):
            cp.start()

    def finish(self, ins, outs, sems, s0, pos):
        for cp in self._copies(ins, outs, sems, s0, pos):
            cp.wait()


class _SwapBack:
    def __init__(self, part):
        hr, cols = part.shape
        self.ins, self.outs = [part], [jax.ShapeDtypeStruct((2 * hr, cols), part.dtype)]
        self.hr = hr
        self.n_sem = 3
        self.result = None

    def _copies(self, ins, outs, sems, s0, pos):
        x, y, c, _ = pos
        mine, theirs = pl.ds(c * self.hr, self.hr), pl.ds((1 - c) * self.hr, self.hr)
        local = lambda: pltpu.make_async_copy(ins[0], outs[0].at[mine], sems.at[s0])
        send = lambda: _remote(ins[0], outs[0].at[mine], sems, s0 + 1, s0 + 2, (x, y, 1 - c))
        land = lambda: _remote(ins[0], outs[0].at[theirs], sems, s0 + 1, s0 + 2, (x, y, 1 - c))
        return local, send, land

    def start(self, ins, outs, sems, s0, pos):
        local, send, _ = self._copies(ins, outs, sems, s0, pos)
        local().start()
        send().start()

    def finish(self, ins, outs, sems, s0, pos):
        local, send, land = self._copies(ins, outs, sems, s0, pos)
        land().wait_recv()
        send().wait_send()
        local().wait()


def _pallas(body, args, *, name, grid, in_specs, out_specs, out_shape, scratch_shapes=(), comm=()):
    comm = list(comm)
    n_in, n_out, n_scr = len(in_specs), len(out_specs), len(scratch_shapes)
    c_in = [a for op in comm for a in op.ins]
    c_out = [s for op in comm for s in op.outs]
    n_sem = sum(op.n_sem for op in comm)

    def full(*refs):
        o0 = n_in + len(c_in)
        s0 = o0 + n_out + len(c_out)
        ins, cin = refs[:n_in], refs[n_in:o0]
        outs, cout = refs[o0:o0 + n_out], refs[o0 + n_out:s0]
        scratch = refs[s0:s0 + n_scr]
        if comm:
            sems = refs[s0 + n_scr]
            ids = [pl.program_id(d) for d in range(len(grid))]
            first = functools.reduce(jnp.logical_and, [i == 0 for i in ids])
            last = functools.reduce(jnp.logical_and, [i == g - 1 for i, g in zip(ids, grid)])
            pos = _position()

            def each(step):
                ci = co = cs = 0
                for op in comm:
                    getattr(op, step)(cin[ci:ci + len(op.ins)], cout[co:co + len(op.outs)], sems, cs, pos)
                    ci, co, cs = ci + len(op.ins), co + len(op.outs), cs + op.n_sem

            @pl.when(first)
            def _():
                each("start")

        body(*ins, *outs, *scratch)
        if comm:
            @pl.when(last)
            def _():
                each("finish")

    hbm = pl.BlockSpec(memory_space=pl.ANY)
    res = pl.pallas_call(
        full, name=name, grid=grid,
        in_specs=list(in_specs) + [hbm] * len(c_in), out_specs=list(out_specs) + [hbm] * len(c_out),
        out_shape=list(out_shape) + c_out,
        scratch_shapes=list(scratch_shapes) + ([pltpu.SemaphoreType.DMA((n_sem,))] if comm else []),
        compiler_params=pltpu.CompilerParams(dimension_semantics=("arbitrary",) * len(grid),
                                             vmem_limit_bytes=VMEM_LIMIT))(*args, *c_in)
    k = n_out
    for op in comm:
        op.result = list(res[k:k + len(op.outs)])
        k += len(op.outs)
    return list(res[:n_out])


def _row_tile(rows, cols, itemsize=4, budget=1 << 20):
    t = rows
    while t % (2 * BF16_ROWS) == 0 and t * cols * itemsize > budget:
        t //= 2
    return t


def _cast_bf16(w, name):
    rows, cols = w.shape
    tr = _row_tile(rows, cols)

    def body(w_ref, o_ref):
        o_ref[...] = w_ref[...].astype(BF16)

    spec = pl.BlockSpec((tr, cols), lambda i: (i, 0))
    return _pallas(body, [w], name=name, grid=(rows // tr,), in_specs=[spec], out_specs=[spec],
                   out_shape=[jax.ShapeDtypeStruct((rows, cols), BF16)])[0]


def _rms_fwd(x, g, name, comm=()):
    L, D = x.shape

    def body(x_ref, g_ref, h_ref):
        xf = x_ref[...]
        r = lax.rsqrt(jnp.mean(xf * xf, axis=-1, keepdims=True) + EPS)
        h_ref[...] = (xf * r * g_ref[...]).astype(BF16)

    row = pl.BlockSpec((TMS, D), lambda i: (i, 0))
    return _pallas(body, [x, g], name=name, grid=(L // TMS,),
                   in_specs=[row, pl.BlockSpec((1, D), lambda i: (0, 0))], out_specs=[row],
                   out_shape=[jax.ShapeDtypeStruct((L, D), BF16)], comm=comm)[0]


def _resident(shape):
    return pl.BlockSpec(shape, lambda *_: (0,) * len(shape), pipeline_mode=pl.Buffered(1))


def _ffn_up(h, wg_t, wu_t, name, comm=()):
    L, D = h.shape
    G, FS, _ = wg_t.shape

    def body(h_ref, wg_ref, wu_ref, a_ref, b_ref, act_ref):
        j = pl.program_id(1)
        hv = h_ref[...]
        a = _dot_nt(hv, wg_ref[j])
        b = _dot_nt(hv, wu_ref[j])
        a_ref[...] = a.astype(BF16)
        b_ref[...] = b.astype(BF16)
        act_ref[...] = (a * _sigmoid(a) * b).astype(BF16)

    ospec = pl.BlockSpec((None, TMS, FS), lambda i, j: (j, i, 0))
    oshape = jax.ShapeDtypeStruct((G, L, FS), BF16)
    return _pallas(body, [h, wg_t, wu_t], name=name, grid=(L // TMS, G),
                   in_specs=[pl.BlockSpec((TMS, D), lambda i, j: (i, 0)), _resident((G, FS, D)), _resident((G, FS, D))],
                   out_specs=[ospec, ospec, ospec], out_shape=[oshape, oshape, oshape], comm=comm)


def _group_sum(a_ref, w_ref, groups, mm=_dot):
    acc = mm(a_ref[0], w_ref[0])
    for j in range(1, groups):
        acc = acc + mm(a_ref[j], w_ref[j])
    return acc


def _ffn_down(act, wd, x, g_next, name, comm=()):
    G, L, FS = act.shape
    D = wd.shape[2]

    def body(act_ref, wd_ref, x_ref, g_ref, xn_ref, hn_ref):
        xn = x_ref[...] + 0.5 * _group_sum(act_ref, wd_ref, G)
        xn_ref[...] = xn
        r = lax.rsqrt(jnp.mean(xn * xn, axis=-1, keepdims=True) + EPS)
        hn_ref[...] = (xn * r * g_ref[...]).astype(BF16)

    row = pl.BlockSpec((TM, D), lambda i: (i, 0))
    return _pallas(body, [act, wd, x, g_next], name=name, grid=(L // TM,),
                   in_specs=[pl.BlockSpec((G, TM, FS), lambda i: (0, i, 0)), _resident((G, FS, D)), row,
                             pl.BlockSpec((1, D), lambda i: (0, 0))],
                   out_specs=[row, row],
                   out_shape=[jax.ShapeDtypeStruct((L, D), F32), jax.ShapeDtypeStruct((L, D), BF16)], comm=comm)


def _ffn_down_loss(act, wd, x, gf, tgt, name):
    G, L, FS = act.shape
    D = wd.shape[2]

    def body(act_ref, wd_ref, x_ref, g_ref, t_ref, dx_ref, dxb_ref, loss_ref, dg_ref):
        i = pl.program_id(0)
        xn = x_ref[...] + 0.5 * _group_sum(act_ref, wd_ref, G)
        r = lax.rsqrt(jnp.mean(xn * xn, axis=-1, keepdims=True) + EPS)
        xh = xn * r
        gv = g_ref[...]
        e = xh * gv - t_ref[...]
        part = 0.5 * jnp.sum(_colsum(e * e), axis=1, keepdims=True) / D
        dy = e / D
        _accumulate(loss_ref, i == 0, jnp.broadcast_to(part, (1, LANES)))
        _accumulate(dg_ref, i == 0, _colsum(dy * xh))
        dxh = dy * gv
        dx = r * (dxh - xh * jnp.mean(dxh * xh, axis=-1, keepdims=True))
        dx_ref[...] = dx
        dxb_ref[...] = dx.astype(BF16)

    row = pl.BlockSpec((TM, D), lambda i: (i, 0))
    return _pallas(body, [act, wd, x, gf, tgt], name=name, grid=(L // TM,),
                   in_specs=[pl.BlockSpec((G, TM, FS), lambda i: (0, i, 0)), _resident((G, FS, D)), row,
                             pl.BlockSpec((1, D), lambda i: (0, 0)), row],
                   out_specs=[row, row, pl.BlockSpec((1, LANES), lambda i: (0, 0)),
                              pl.BlockSpec((1, D), lambda i: (0, 0))],
                   out_shape=[jax.ShapeDtypeStruct((L, D), F32), jax.ShapeDtypeStruct((L, D), BF16),
                              jax.ShapeDtypeStruct((1, LANES), F32), jax.ShapeDtypeStruct((1, D), F32)])


def _ffn_bwd_act(dxb, wd, a, b, name, comm=()):
    L, D = dxb.shape
    G, FS, _ = wd.shape

    def body(dx_ref, wd_ref, a_ref, b_ref, da_ref, db_ref):
        dact = 0.5 * _dot_nt(dx_ref[...], wd_ref[pl.program_id(1)])
        av = a_ref[...].astype(F32)
        bv = b_ref[...].astype(F32)
        sg = _sigmoid(av)
        da_ref[...] = (dact * bv * sg * (1.0 + av * (1.0 - sg))).astype(BF16)
        db_ref[...] = (dact * av * sg).astype(BF16)

    gspec = pl.BlockSpec((None, TMS, FS), lambda i, j: (j, i, 0))
    oshape = jax.ShapeDtypeStruct((G, L, FS), BF16)
    return _pallas(body, [dxb, wd, a, b], name=name, grid=(L // TMS, G),
                   in_specs=[pl.BlockSpec((TMS, D), lambda i, j: (i, 0)), _resident((G, FS, D)), gspec, gspec],
                   out_specs=[gspec, gspec], out_shape=[oshape, oshape], comm=comm)


def _mm_grouped(a, w, name):
    L, K = a.shape
    G, _, N = w.shape

    def body(a_ref, w_ref, o_ref):
        o_ref[...] = _dot(a_ref[...], w_ref[pl.program_id(1)])

    return _pallas(body, [a, w], name=name, grid=(L // TMS, G),
                   in_specs=[pl.BlockSpec((TMS, K), lambda i, g: (i, 0)), _resident((G, K, N))],
                   out_specs=[pl.BlockSpec((TMS, N), lambda i, g: (i, g))],
                   out_shape=[jax.ShapeDtypeStruct((L, G * N), F32)])[0]


def _mm_nt(a, w, name):
    L, K = a.shape
    N = w.shape[0]

    def body(a_ref, w_ref, o_ref):
        o_ref[...] = _dot_nt(a_ref[...], w_ref[...]).astype(BF16)

    return _pallas(body, [a, w], name=name, grid=(L // TMS,),
                   in_specs=[pl.BlockSpec((TMS, K), lambda i: (i, 0)), _resident((N, K))],
                   out_specs=[pl.BlockSpec((TMS, N), lambda i: (i, 0))],
                   out_shape=[jax.ShapeDtypeStruct((L, N), BF16)])[0]


def _mm_tn(a, b, scale, name, groups, b_cols=False, comm=()):
    L, M = a.shape[1], a.shape[2]
    N = b.shape[1] // groups if b_cols else b.shape[2]
    tk = min(L, TK)
    nk = L // tk

    def spec(v, cols):
        if cols:
            return pl.BlockSpec((tk, v.shape[1] // groups), lambda g, k: (k, g))
        if v.shape[0] > 1:
            return pl.BlockSpec((None, tk, v.shape[2]), lambda g, k: (g, k, 0))
        return pl.BlockSpec((None, tk, v.shape[2]), lambda g, k: (0, k, 0))

    def body(a_ref, b_ref, o_ref, acc):
        k = pl.program_id(1)
        p = _dot_tn(a_ref[...], b_ref[...])
        if nk == 1:
            o_ref[...] = (p * scale).astype(BF16)
        else:
            _accumulate(acc, k == 0, p)

            @pl.when(k == nk - 1)
            def _():
                o_ref[...] = (acc[...] * scale).astype(BF16)

    return _pallas(body, [a, b], name=name, grid=(groups, nk),
                   in_specs=[spec(a, False), spec(b, b_cols)],
                   out_specs=[pl.BlockSpec((None, M, N), lambda g, k: (g, 0, 0))],
                   out_shape=[jax.ShapeDtypeStruct((groups, M, N), BF16)],
                   scratch_shapes=[pltpu.VMEM((M, N), F32)], comm=comm)[0]


def _mm_rmsbwd(a_list, w_list, nt, a_cols, x_in, g, dx_out, name, comm=()):
    P = len(a_list)
    G = w_list[0].shape[0]
    L, D = x_in.shape
    mm = _dot_nt if nt else _dot

    def body(*refs):
        a_refs, w_refs = refs[:P], refs[P:2 * P]
        x_ref, g_ref, dxo_ref, dx_ref, dxb_ref, dg_ref = refs[2 * P:]
        i = pl.program_id(0)
        dh = None
        for a_ref, w_ref in zip(a_refs, w_refs):
            for j in range(G):
                if a_cols:
                    kw = a_ref.shape[1] // G
                    term = mm(a_ref[:, j * kw:(j + 1) * kw], w_ref[j])
                else:
                    term = mm(a_ref[j], w_ref[j])
                dh = term if dh is None else dh + term
        xf = x_ref[...]
        r = lax.rsqrt(jnp.mean(xf * xf, axis=-1, keepdims=True) + EPS)
        xh = xf * r
        _accumulate(dg_ref, i == 0, _colsum(dh * xh))
        dxh = dh * g_ref[...]
        dx = dxo_ref[...] + r * (dxh - xh * jnp.mean(dxh * xh, axis=-1, keepdims=True))
        dx_ref[...] = dx
        dxb_ref[...] = dx.astype(BF16)

    row = pl.BlockSpec((TM, D), lambda i: (i, 0))
    vec = pl.BlockSpec((1, D), lambda i: (0, 0))
    if a_cols:
        a_specs = [pl.BlockSpec((TM, a.shape[1]), lambda i: (i, 0)) for a in a_list]
    else:
        a_specs = [pl.BlockSpec((G, TM, a.shape[2]), lambda i: (0, i, 0)) for a in a_list]
    w_specs = [_resident(w.shape) for w in w_list]
    return _pallas(body, [*a_list, *w_list, x_in, g, dx_out], name=name, grid=(L // TM,),
                   in_specs=a_specs + w_specs + [row, vec, row], out_specs=[row, row, vec],
                   out_shape=[jax.ShapeDtypeStruct((L, D), F32), jax.ShapeDtypeStruct((L, D), BF16),
                              jax.ShapeDtypeStruct((1, D), F32)], comm=comm)


def _mix_out(cat, wout, x1, g_next, name):
    L, K = cat.shape
    D = wout.shape[1]

    def body(c_ref, w_ref, x_ref, g_ref, xn_ref, hn_ref):
        xn = x_ref[...] + _dot(c_ref[...], w_ref[...])
        xn_ref[...] = xn
        r = lax.rsqrt(jnp.mean(xn * xn, axis=-1, keepdims=True) + EPS)
        hn_ref[...] = (xn * r * g_ref[...]).astype(BF16)

    row = pl.BlockSpec((TMS, D), lambda i: (i, 0))
    return _pallas(body, [cat, wout, x1, g_next], name=name, grid=(L // TMS,),
                   in_specs=[pl.BlockSpec((TMS, K), lambda i: (i, 0)), pl.BlockSpec((K, D), lambda i: (0, 0)), row,
                             pl.BlockSpec((1, D), lambda i: (0, 0))],
                   out_specs=[row, row],
                   out_shape=[jax.ShapeDtypeStruct((L, D), F32), jax.ShapeDtypeStruct((L, D), BF16)])


def _s5_disc(lr, li, ldt, brc, bic):
    dt = jnp.exp(ldt)
    mag = jnp.exp(lr * dt)
    are = mag * jnp.cos(li * dt)
    aim = mag * jnp.sin(li * dt)
    den = lr * lr + li * li
    nre = are - 1.0
    fre = (nre * lr + aim * li) / den
    fim = (aim * lr - nre * li) / den
    return are, aim, fre * brc - fim * bic, fre * bic + fim * brc


def _s5_params_fwd(lr, li, ldt, brc, bic, crc, cic):
    def body(lr_ref, li_ref, ldt_ref, br_ref, bi_ref, cr_ref, ci_ref, are_ref, aim_ref, bre_ref, bim_ref, cre_ref, cim_ref):
        are, aim, bre, bim = _s5_disc(lr_ref[...], li_ref[...], ldt_ref[...], br_ref[...], bi_ref[...])
        are_ref[...] = are
        aim_ref[...] = aim
        bre_ref[...] = bre.astype(BF16)
        bim_ref[...] = bim.astype(BF16)
        cre_ref[...] = cr_ref[...].astype(BF16)
        cim_ref[...] = ci_ref[...].astype(BF16)

    vec = jax.ShapeDtypeStruct((1, S5_N), F32)
    return pl.pallas_call(
        body, name="s5_params_fwd",
        out_shape=[vec, vec, jax.ShapeDtypeStruct((LANES, S5_N), BF16), jax.ShapeDtypeStruct((LANES, S5_N), BF16),
                   jax.ShapeDtypeStruct((S5_N, LANES), BF16), jax.ShapeDtypeStruct((S5_N, LANES), BF16)],
        compiler_params=pltpu.CompilerParams(vmem_limit_bytes=VMEM_LIMIT))(lr, li, ldt, brc, bic, crc, cic)


def _s5_params_bwd(lr, li, ldt, brc, bic, dare, daim, dbre, dbim):
    def body(lr_ref, li_ref, ldt_ref, br_ref, bi_ref, dare_ref, daim_ref, dbre_ref, dbim_ref,
             glr_ref, gli_ref, gldt_ref, gbr_ref, gbi_ref):
        _, vjp = jax.vjp(_s5_disc, lr_ref[...], li_ref[...], ldt_ref[...], br_ref[...], bi_ref[...])
        glr, gli, gldt, gbr, gbi = vjp((dare_ref[...], daim_ref[...], dbre_ref[...], dbim_ref[...]))
        glr_ref[...] = glr
        gli_ref[...] = gli
        gldt_ref[...] = gldt
        gbr_ref[...] = gbr
        gbi_ref[...] = gbi

    vec = jax.ShapeDtypeStruct((1, S5_N), F32)
    mat = jax.ShapeDtypeStruct((LANES, S5_N), F32)
    return pl.pallas_call(
        body, name="s5_params_bwd", out_shape=[vec, vec, vec, mat, mat],
        compiler_params=pltpu.CompilerParams(vmem_limit_bytes=VMEM_LIMIT))(lr, li, ldt, brc, bic, dare, daim, dbre, dbim)


def _gelu_parts(y):
    th = jnp.tanh(GELU_C0 * (y + GELU_C1 * y * y * y))
    return 0.5 * y * (1.0 + th), th


def _state_rows(q, T):
    return pl.ds(q % S5_HALF, T, stride=S5_HALF)


def _load_tiles(bufs, ct, T, dtype):
    return jnp.concatenate([bufs[q // S5_HALF][_state_rows(q, T), :].astype(dtype) for q in range(4 * ct, 4 * ct + 4)],
                           axis=1)


def _store_tiles(bufs, ct, T, value):
    for k, q in enumerate(range(4 * ct, 4 * ct + 4)):
        bufs[q // S5_HALF][_state_rows(q, T), :] = value[:, k * LANES:(k + 1) * LANES]


def _s5_fwd(u, are, aim, bre, bim, cre, cim, d_skip, wglu, bglu, comm=()):
    L = u.shape[0]
    T = min(TS, L)
    n = L // T

    def body(u_ref, are_ref, aim_ref, bre_ref, bim_ref, cre_ref, cim_ref, d_ref, wg_ref, bg_ref,
             sre_lo, sre_hi, sim_lo, sim_hi, y_ref, o_ref, st_re, st_im):
        i = pl.program_id(0)
        sre, sim = (sre_lo, sre_hi), (sim_lo, sim_hi)

        @pl.when(i == 0)
        def _():
            st_re[...] = jnp.zeros_like(st_re)
            st_im[...] = jnp.zeros_like(st_im)

        uf = u_ref[...]
        ub = uf.astype(BF16)
        for ct in range(4):
            uq = ub[:, ct * LANES:(ct + 1) * LANES]
            win = slice(4 * ct * LANES, 4 * (ct + 1) * LANES)
            _store_tiles(sre, ct, T, _dot(uq, bre_ref[:, win]))
            _store_tiles(sim, ct, T, _dot(uq, bim_ref[:, win]))
        halves = [slice(h * S5_HALF, (h + 1) * S5_HALF) for h in range(2)]
        a_re = [are_ref[hs, :] for hs in halves]
        a_im = [aim_ref[hs, :] for hs in halves]

        def step(t, carry):
            rows = pl.ds(pl.multiple_of(t * S5_HALF, S5_HALF), S5_HALF)
            out = []
            for h in range(2):
                s_re, s_im = carry[2 * h], carry[2 * h + 1]
                n_re = a_re[h] * s_re - a_im[h] * s_im + sre[h][rows, :]
                n_im = a_re[h] * s_im + a_im[h] * s_re + sim[h][rows, :]
                sre[h][rows, :] = n_re
                sim[h][rows, :] = n_im
                out += [n_re, n_im]
            return tuple(out)

        init = (st_re[halves[0], :], st_im[halves[0], :], st_re[halves[1], :], st_im[halves[1], :])
        fin = lax.fori_loop(0, T, step, init, unroll=4)
        for h in range(2):
            st_re[halves[h], :] = fin[2 * h]
            st_im[halves[h], :] = fin[2 * h + 1]
        tiles = []
        for ct in range(4):
            win = slice(4 * ct * LANES, 4 * (ct + 1) * LANES)
            tiles.append(_dot(_load_tiles(sre, ct, T, BF16), cre_ref[win, :])
                         - _dot(_load_tiles(sim, ct, T, BF16), cim_ref[win, :]))
        y = jnp.concatenate(tiles, axis=1) + d_ref[...] * uf
        y_ref[...] = y
        yg, _ = _gelu_parts(y)
        gate = _sigmoid(_dot(yg.astype(BF16), wg_ref[...]) + bg_ref[...])
        o_ref[...] = (yg * gate).astype(BF16)

    const = lambda shape: pl.BlockSpec(shape, lambda i: (0, 0))
    sspec = pl.BlockSpec((T * S5_HALF, LANES), lambda i: (i, 0))
    sshape = jax.ShapeDtypeStruct((L * S5_HALF, LANES), F32)
    chunk = pl.BlockSpec((T, S5_W), lambda i: (i, 0))
    return _pallas(body, [u, are, aim, bre, bim, cre, cim, d_skip, wglu, bglu], name="s5_fwd", grid=(n,),
                   in_specs=[chunk, const((S5_TILES, LANES)), const((S5_TILES, LANES)),
                             const((LANES, S5_N)), const((LANES, S5_N)), const((S5_N, LANES)), const((S5_N, LANES)),
                             const((1, S5_W)), const((S5_W, S5_W)), const((1, S5_W))],
                   out_specs=[sspec] * 4 + [chunk, chunk],
                   out_shape=[sshape] * 4 + [jax.ShapeDtypeStruct((L, S5_W), F32), jax.ShapeDtypeStruct((L, S5_W), BF16)],
                   scratch_shapes=[pltpu.VMEM((S5_TILES, LANES), F32), pltpu.VMEM((S5_TILES, LANES), F32)], comm=comm)


def _s5_bwd(dm, y_pre, u, states, are, aim, bre, bim, cre, cim, d_skip, wglu, bglu, comm=()):
    L = u.shape[0]
    T = min(TS, L)
    n = L // T

    def body(dm_ref, y_ref, u_ref, sre_lo, sre_hi, sim_lo, sim_hi, pre_lo, pre_hi, pim_lo, pim_hi,
             are_ref, aim_ref, bre_ref, bim_ref, cre_ref, cim_ref, d_ref, wg_ref, bg_ref,
             du_ref, dwg_ref, dbg_ref, dd_ref, dcre_ref, dcim_ref, dbre_ref, dbim_ref, dare_ref, daim_ref,
             gre_lo, gre_hi, gim_lo, gim_hi, car_re, car_im):
        i = pl.program_id(0)
        first = i == 0
        sre, sim = (sre_lo, sre_hi), (sim_lo, sim_hi)
        gre, gim = (gre_lo, gre_hi), (gim_lo, gim_hi)
        pre, pim = (pre_lo, pre_hi), (pim_lo, pim_hi)

        @pl.when(first)
        def _():
            car_re[...] = jnp.zeros_like(car_re)
            car_im[...] = jnp.zeros_like(car_im)
            dcre_ref[...] = jnp.zeros_like(dcre_ref)
            dcim_ref[...] = jnp.zeros_like(dcim_ref)
            dbre_ref[...] = jnp.zeros_like(dbre_ref)
            dbim_ref[...] = jnp.zeros_like(dbim_ref)

        y = y_ref[...]
        uf = u_ref[...]
        yg, th = _gelu_parts(y)
        dgelu = 0.5 * (1.0 + th) + 0.5 * y * (1.0 - th * th) * GELU_C0 * (1.0 + 3.0 * GELU_C1 * y * y)
        ygb = yg.astype(BF16)
        sg = _sigmoid(_dot(ygb, wg_ref[...]) + bg_ref[...])
        dout = dm_ref[...].astype(F32)
        dgp = dout * yg * sg * (1.0 - sg)
        dgpb = dgp.astype(BF16)
        dyg = dout * sg + _dot_nt(dgpb, wg_ref[...])
        _accumulate(dwg_ref, first, _dot_tn(ygb, dgpb))
        _accumulate(dbg_ref, first, _colsum(dgp))
        dy = dyg * dgelu
        _accumulate(dd_ref, first, _colsum(dy * uf))
        dyb = dy.astype(BF16)
        ub = uf.astype(BF16)

        for ct in range(4):
            win = slice(4 * ct * LANES, 4 * (ct + 1) * LANES)
            dyq = dyb[:, ct * LANES:(ct + 1) * LANES]
            dcre_ref[win, :] += _dot_tn(_load_tiles(sre, ct, T, BF16), dyq)
            dcim_ref[win, :] -= _dot_tn(_load_tiles(sim, ct, T, BF16), dyq)
            _store_tiles(gre, ct, T, _dot_nt(dyq, cre_ref[win, :]))
            _store_tiles(gim, ct, T, -_dot_nt(dyq, cim_ref[win, :]))

        halves = [slice(h * S5_HALF, (h + 1) * S5_HALF) for h in range(2)]
        a_re = [are_ref[hs, :] for hs in halves]
        a_im = [aim_ref[hs, :] for hs in halves]

        def adjoint(t, h, g_re, g_im):
            rows = pl.ds(pl.multiple_of(t * S5_HALF, S5_HALF), S5_HALF)
            n_re = gre[h][rows, :] + a_re[h] * g_re + a_im[h] * g_im
            n_im = gim[h][rows, :] + a_re[h] * g_im - a_im[h] * g_re
            gre[h][rows, :] = n_re
            gim[h][rows, :] = n_im
            return n_re, n_im

        def step(k, carry):
            out = []
            for h in range(2):
                out += adjoint(T - 1 - k, h, carry[2 * h], carry[2 * h + 1])
            return tuple(out)

        init = (car_re[halves[0], :], car_im[halves[0], :], car_re[halves[1], :], car_im[halves[1], :])
        fin = lax.fori_loop(0, T, step, init, unroll=4)
        keep = (i < n - 1).astype(F32)
        later, earlier = slice(S5_HALF, T * S5_HALF), slice(0, (T - 1) * S5_HALF)
        fold = lambda v: jnp.sum(v.reshape(T - 1, S5_HALF, LANES), axis=0)
        for h in range(2):
            g_re, g_im = fin[2 * h], fin[2 * h + 1]
            car_re[halves[h], :] = g_re
            car_im[halves[h], :] = g_im
            p_re, p_im = pre[h][...] * keep, pim[h][...] * keep
            da_re = fold(gre[h][later, :] * sre[h][earlier, :] + gim[h][later, :] * sim[h][earlier, :])
            da_im = fold(gim[h][later, :] * sre[h][earlier, :] - gre[h][later, :] * sim[h][earlier, :])
            da_re = da_re + g_re * p_re + g_im * p_im
            da_im = da_im + g_im * p_re - g_re * p_im

            @pl.when(first)
            def _():
                dare_ref[halves[h], :] = da_re
                daim_ref[halves[h], :] = da_im

            @pl.when(jnp.logical_not(first))
            def _():
                dare_ref[halves[h], :] += da_re
                daim_ref[halves[h], :] += da_im

        tiles = []
        for ct in range(4):
            uq = ub[:, ct * LANES:(ct + 1) * LANES]
            win = slice(4 * ct * LANES, 4 * (ct + 1) * LANES)
            g_re, g_im = _load_tiles(gre, ct, T, BF16), _load_tiles(gim, ct, T, BF16)
            tiles.append(d_ref[:, ct * LANES:(ct + 1) * LANES] * dy[:, ct * LANES:(ct + 1) * LANES]
                         + _dot_nt(g_re, bre_ref[:, win]) + _dot_nt(g_im, bim_ref[:, win]))
            dbre_ref[:, win] += _dot_tn(uq, g_re)
            dbim_ref[:, win] += _dot_tn(uq, g_im)
        du_ref[...] = jnp.concatenate(tiles, axis=1).astype(BF16)

    rev = lambda i: (n - 1 - i, 0)
    const = lambda shape: pl.BlockSpec(shape, lambda i: (0, 0))
    chunk = pl.BlockSpec((T, S5_W), rev)
    sspec = pl.BlockSpec((T * S5_HALF, LANES), rev)
    pspec = pl.BlockSpec((S5_HALF, LANES), lambda i: (jnp.maximum((n - 1 - i) * T - 1, 0), 0))
    tile = jax.ShapeDtypeStruct((S5_TILES, LANES), F32)
    vec = jax.ShapeDtypeStruct((1, S5_W), F32)
    sbuf = pltpu.VMEM((T * S5_HALF, LANES), F32)
    return _pallas(
        body, [dm, y_pre, u, *states, *states, are, aim, bre, bim, cre, cim, d_skip, wglu, bglu],
        name="s5_bwd", grid=(n,),
        in_specs=[chunk, chunk, chunk] + [sspec] * 4 + [pspec] * 4 + [
            const((S5_TILES, LANES)), const((S5_TILES, LANES)), const((LANES, S5_N)), const((LANES, S5_N)),
            const((S5_N, LANES)), const((S5_N, LANES)), const((1, S5_W)), const((S5_W, S5_W)), const((1, S5_W))],
        out_specs=[chunk, const((S5_W, S5_W)), const((1, S5_W)), const((1, S5_W)), const((S5_N, LANES)),
                   const((S5_N, LANES)), const((LANES, S5_N)), const((LANES, S5_N)), const((S5_TILES, LANES)),
                   const((S5_TILES, LANES))],
        out_shape=[jax.ShapeDtypeStruct((L, S5_W), BF16), jax.ShapeDtypeStruct((S5_W, S5_W), F32), vec, vec,
                   jax.ShapeDtypeStruct((S5_N, LANES), F32), jax.ShapeDtypeStruct((S5_N, LANES), F32),
                   jax.ShapeDtypeStruct((LANES, S5_N), F32), jax.ShapeDtypeStruct((LANES, S5_N), F32), tile, tile],
        scratch_shapes=[sbuf, sbuf, sbuf, sbuf, pltpu.VMEM((S5_TILES, LANES), F32), pltpu.VMEM((S5_TILES, LANES), F32)],
        comm=comm)


_EYE8 = np.eye(8, dtype=np.float32)


def _compact_b(b):
    return jnp.einsum("akpc,kj->jcakp", b.reshape(4, 8, S5_P, S5_GC), _EYE8).reshape(LANES, S5_N)


def _uncompact_b(m):
    return jnp.einsum("kcakp->akpc", m.reshape(8, S5_GC, 4, 8, S5_P)).reshape(S5_G, S5_P, S5_GC)


_HEAD_MEAN = np.kron(np.eye(CONV_W // CONV_HD, dtype=np.float32), np.full((CONV_HD, CONV_HD), 1.0 / CONV_HD, np.float32))


def _head_mean(v, m):
    hi = v.astype(BF16)
    lo = (v - hi.astype(F32)).astype(BF16)
    return _dot(hi, m) + _dot(lo, m)


def _head_norm(zc, m):
    d = zc - _head_mean(zc, m)
    rstd = lax.rsqrt(_head_mean(d * d, m) + EPS)
    return d * rstd, rstd


def _taps_by_phase(first):
    groups = {}
    for k in range(CONV_K):
        m, s = divmod(first + k, 8)
        groups.setdefault(s, []).append((m, k))
    return groups


def _causal_taps(buf, w_ref, r0, first, flip):
    acc = None
    for s, taps in sorted(_taps_by_phase(first).items()):
        rows = CONV_SB + (8 if s else 0)
        y = None
        for m, k in taps:
            kk = CONV_K - 1 - k if flip else k
            term = w_ref[kk:kk + 1, :] * buf[pl.ds(r0 + 8 * m, rows), :]
            y = term if y is None else y + term
        y = y[s:s + CONV_SB, :]
        acc = y if acc is None else acc + y
    return acc


def _conv_fwd(u, o_s5, wdw, bdw, lng, lnb, mavg, comm=()):
    L = u.shape[0]
    T = min(TS, L)
    n = L // T
    first_tap = HALO - (CONV_K - 1)

    def body(v1_ref, v2_ref, s5_ref, w_ref, b_ref, g_ref, be_ref, m_ref, zc_ref, o_ref, zbuf):
        i = pl.program_id(0)

        @pl.when(i == 0)
        def _():
            zbuf[0:HALO, :] = jnp.zeros((HALO, CONV_W), F32)

        zbuf[HALO:HALO + T, :] = v1_ref[...] * _sigmoid(v2_ref[...])
        for r0 in range(0, T, CONV_SB):
            zc_ref[r0:r0 + CONV_SB, :] = b_ref[...] + _causal_taps(zbuf, w_ref, r0, first_tap, False)
        zbuf[0:HALO, :] = zbuf[T:T + HALO, :]
        zn, _ = _head_norm(zc_ref[...], m_ref[...])
        zz = zn * g_ref[...] + be_ref[...]
        o_ref[:, 0:S5_W] = s5_ref[...]
        o_ref[:, S5_W:S5_W + CONV_W] = (zz * _sigmoid(zz)).astype(BF16)

    const = lambda shape: pl.BlockSpec(shape, lambda i: (0, 0))
    vec = const((1, CONV_W))
    return _pallas(body, [u, u, o_s5, wdw, bdw, lng, lnb, mavg], name="conv_fwd", grid=(n,),
                   in_specs=[pl.BlockSpec((T, CONV_W), lambda i: (i, 1)), pl.BlockSpec((T, CONV_W), lambda i: (i, 2)),
                             pl.BlockSpec((T, S5_W), lambda i: (i, 0)), const((CONV_K, CONV_W)), vec, vec, vec,
                             const((CONV_W, CONV_W))],
                   out_specs=[pl.BlockSpec((T, CONV_W), lambda i: (i, 0)),
                              pl.BlockSpec((T, S5_W + CONV_W), lambda i: (i, 0))],
                   out_shape=[jax.ShapeDtypeStruct((L, CONV_W), F32), jax.ShapeDtypeStruct((L, S5_W + CONV_W), BF16)],
                   scratch_shapes=[pltpu.VMEM((T + HALO, CONV_W), F32)], comm=comm)


def _conv_bwd(dm, zc, u, du_s5, wdw, lng, lnb, mavg, comm=()):
    L = u.shape[0]
    T = min(TS, L)
    n = L // T
    hb = T // HALO
    first_tap = HALO - (CONV_K - 1)

    def body(dm_ref, zc_ref, v1_ref, v2_ref, p1_ref, p2_ref, s5_ref, w_ref, g_ref, be_ref, m_ref,
             du_ref, dw_ref, db_ref, dg_ref, dbe_ref, zbuf, dzbuf, head):
        i = pl.program_id(0)
        first = i == 0

        @pl.when(first)
        def _():
            head[...] = jnp.zeros_like(head)
            dw_ref[...] = jnp.zeros_like(dw_ref)

        zn, rstd = _head_norm(zc_ref[...], m_ref[...])
        zz = zn * g_ref[...] + be_ref[...]
        sg = _sigmoid(zz)
        dzz = dm_ref[...].astype(F32) * sg * (1.0 + zz * (1.0 - sg))
        _accumulate(dbe_ref, first, _colsum(dzz))
        _accumulate(dg_ref, first, _colsum(dzz * zn))
        dzn = dzz * g_ref[...]
        dzc = rstd * (dzn - _head_mean(dzn, m_ref[...]) - zn * _head_mean(dzn * zn, m_ref[...]))
        _accumulate(db_ref, first, _colsum(dzc))

        dzbuf[0:T, :] = dzc
        dzbuf[T:T + HALO, :] = head[...]
        head[...] = dzbuf[0:HALO, :]
        keep = (i < n - 1).astype(F32)
        zbuf[0:HALO, :] = p1_ref[...] * _sigmoid(p2_ref[...]) * keep
        zbuf[HALO:HALO + T, :] = v1_ref[...] * _sigmoid(v2_ref[...])
        du_ref[:, 0:S5_W] = s5_ref[...]

        for r0 in range(0, T, CONV_SB):
            rows = slice(r0, r0 + CONV_SB)
            dzc_b = dzbuf[rows, :]
            for s, taps in sorted(_taps_by_phase(first_tap).items()):
                pad = ([jnp.zeros((s, CONV_W), F32)] if s else []) + [dzc_b] + ([jnp.zeros((8 - s, CONV_W), F32)] if s else [])
                shifted = jnp.concatenate(pad, axis=0) if s else dzc_b
                n_rows = shifted.shape[0]
                for m, k in taps:
                    prod = shifted * zbuf[pl.ds(r0 + 8 * m, n_rows), :]
                    dw_ref[8 * k:8 * k + 8, :] += jnp.sum(prod.reshape(n_rows // 8, 8, CONV_W), axis=0)
            dz = _causal_taps(dzbuf, w_ref, r0, 0, True)
            v1 = v1_ref[rows, :]
            sg2 = _sigmoid(v2_ref[rows, :])
            du_ref[rows, S5_W:S5_W + CONV_W] = (dz * sg2).astype(BF16)
            du_ref[rows, S5_W + CONV_W:S5_W + 2 * CONV_W] = (dz * v1 * sg2 * (1.0 - sg2)).astype(BF16)

    rev = lambda c: (lambda i: (n - 1 - i, c))
    prev = lambda c: (lambda i: (jnp.maximum((n - 1 - i) * hb - 1, 0), c))
    const = lambda shape: pl.BlockSpec(shape, lambda i: (0, 0))
    vec = const((1, CONV_W))
    vshape = jax.ShapeDtypeStruct((1, CONV_W), F32)
    return _pallas(
        body, [dm, zc, u, u, u, u, du_s5, wdw, lng, lnb, mavg], name="conv_bwd", grid=(n,),
        in_specs=[pl.BlockSpec((T, CONV_W), rev(1)), pl.BlockSpec((T, CONV_W), rev(0)),
                  pl.BlockSpec((T, CONV_W), rev(1)), pl.BlockSpec((T, CONV_W), rev(2)),
                  pl.BlockSpec((HALO, CONV_W), prev(1)), pl.BlockSpec((HALO, CONV_W), prev(2)),
                  pl.BlockSpec((T, S5_W), rev(0)), const((CONV_K, CONV_W)), vec, vec, const((CONV_W, CONV_W))],
        out_specs=[pl.BlockSpec((T, S5_W + 2 * CONV_W), rev(0)), const((8 * HALO, CONV_W)), vec, vec, vec],
        out_shape=[jax.ShapeDtypeStruct((L, S5_W + 2 * CONV_W), BF16), jax.ShapeDtypeStruct((8 * HALO, CONV_W), F32),
                   vshape, vshape, vshape],
        scratch_shapes=[pltpu.VMEM((T + HALO, CONV_W), F32), pltpu.VMEM((T + HALO, CONV_W), F32),
                        pltpu.VMEM((HALO, CONV_W), F32)], comm=comm)


def _gather_all(v, comm=()):
    rows, cols = v.shape

    def body(x_ref, out_ref, send_sems, recv_sems, local_sem):
        x, y, c, chips = _position()
        me, sibling = (x, y, c), (x, y, 1 - c)

        def block(px, py, pc):
            return out_ref.at[pl.ds((4 * px + 2 * py + pc) * rows, rows), :]

        def copy(k, blk, to, src=None):
            return pltpu.make_async_remote_copy(
                src_ref=block(*blk) if src is None else src, dst_ref=block(*blk), send_sem=send_sems.at[k],
                recv_sem=recv_sems.at[k], device_id=to, device_id_type=MESH)

        mine = pltpu.make_async_copy(x_ref, block(*me), local_sem)
        mine.start()
        first = [copy(0, me, sibling, src=x_ref)]
        first += [copy(1 + j, me, (*chip, c), src=x_ref) for j, chip in enumerate(chips)]
        for cp in first:
            cp.start()
        passed = [copy(4 + j, (*chip, c), sibling) for j, chip in enumerate(chips)]
        for j, chip in enumerate(chips):
            copy(1 + j, (*chip, c), me).wait_recv()
            passed[j].start()
        copy(0, sibling, me).wait_recv()
        for j, chip in enumerate(chips):
            copy(4 + j, (*chip, 1 - c), me).wait_recv()
        for cp in first + passed:
            cp.wait_send()
        mine.wait()

    whole = pl.BlockSpec(memory_space=pltpu.VMEM)
    return _pallas(body, [v], name="gather_small", grid=(1,), in_specs=[whole], out_specs=[whole],
                   out_shape=[jax.ShapeDtypeStruct((N_DEV * rows, cols), v.dtype)],
                   scratch_shapes=[pltpu.SemaphoreType.DMA((7,)), pltpu.SemaphoreType.DMA((7,)), pltpu.SemaphoreType.DMA],
                   comm=comm)[0]


def _adamw(w, g, m, v):
    m = ADAM_B1 * m + (1.0 - ADAM_B1) * g
    v = ADAM_B2 * v + (1.0 - ADAM_B2) * jnp.square(g)
    m_hat = m / (1.0 - ADAM_B1 ** ADAM_STEP)
    v_hat = v / (1.0 - ADAM_B2 ** ADAM_STEP)
    return -ADAM_LR * (m_hat / (jnp.sqrt(v_hat) + ADAM_EPS) + ADAM_WD * w), m, v


def _sum_slots(recv, name, comm=()):
    _, rows, cols = recv.shape
    tr = _row_tile(rows, cols)

    def body(r_ref, o_ref):
        acc = r_ref[0].astype(F32)
        for s in range(1, N_CHIPS):
            acc = acc + r_ref[s].astype(F32)
        o_ref[...] = acc

    return _pallas(body, [recv], name=name, grid=(rows // tr,),
                   in_specs=[pl.BlockSpec((N_CHIPS, tr, cols), lambda i: (0, i, 0))],
                   out_specs=[pl.BlockSpec((tr, cols), lambda i: (i, 0))],
                   out_shape=[jax.ShapeDtypeStruct((rows, cols), F32)], comm=comm)[0]


def _add_halves(mine, theirs, name):
    slots, rows, cols = mine.shape
    tr = _row_tile(rows, cols * slots)

    def body(a_ref, b_ref, o_ref):
        o_ref[...] = (a_ref[...].astype(F32) + b_ref[...].astype(F32)).astype(o_ref.dtype)

    spec = pl.BlockSpec((slots, tr, cols), lambda i: (0, i, 0))
    return _pallas(body, [mine, theirs], name=name, grid=(rows // tr,), in_specs=[spec, spec], out_specs=[spec],
                   out_shape=[jax.ShapeDtypeStruct(mine.shape, mine.dtype)])[0]


def _adamw_sharded(w, parts, m, v, name, comm=()):
    rows, cols = w.shape
    tr = _row_tile(rows, cols)
    n = len(parts)

    def body(w_ref, *refs):
        p_refs, (m_ref, v_ref, g_ref, d_ref, nm_ref, nv_ref) = refs[:n], refs[n:]
        g = p_refs[0][...]
        for p_ref in p_refs[1:]:
            g = g + p_ref[...]
        g_ref[...] = g
        d_ref[...], nm_ref[...], nv_ref[...] = _adamw(w_ref[...], g, m_ref[...], v_ref[...])

    spec = pl.BlockSpec((tr, cols), lambda i: (i, 0))
    shape = jax.ShapeDtypeStruct((rows, cols), F32)
    return _pallas(body, [w, *parts, m, v], name=name, grid=(rows // tr,), in_specs=[spec] * (n + 3),
                   out_specs=[spec] * 4, out_shape=[shape] * 4, comm=comm)


def _adamw_small(w, gathered, m, v):
    rows, cols = w.shape

    def body(w_ref, a_ref, m_ref, v_ref, g_ref, d_ref, nm_ref, nv_ref):
        g = a_ref[0:rows, :]
        for dev in range(1, N_DEV):
            g = g + a_ref[dev * rows:(dev + 1) * rows, :]
        g_ref[...] = g
        d_ref[...], nm_ref[...], nv_ref[...] = _adamw(w_ref[...], g, m_ref[...], v_ref[...])

    shape = jax.ShapeDtypeStruct((rows, cols), F32)
    return pl.pallas_call(
        body, name="adamw_small", out_shape=[shape] * 4,
        compiler_params=pltpu.CompilerParams(vmem_limit_bytes=VMEM_LIMIT))(w, gathered, m, v)


PACK_TILE = 8 * LANES


STATE_MINOR = ("s5_b_re", "s5_b_im")


def _pack_small(vals, last_row):
    rows = []
    for name in SMALL:
        v = jnp.swapaxes(vals[name], -1, -2) if name in STATE_MINOR else vals[name]
        flat = v.reshape(-1).astype(F32)
        rows.append(jnp.pad(flat, (0, -flat.size % PACK_TILE)).reshape(-1, LANES))
    rows.append(jnp.pad(last_row, ((0, 7), (0, 0))))
    return jnp.concatenate(rows, axis=0)


def _unpack_small(packed, like):
    out, r = {}, 0
    for name in SMALL:
        size, shape = like[name].size, like[name].shape
        flat = packed[r:r + -(-size // LANES)].reshape(-1)[:size]
        if name in STATE_MINOR:
            out[name] = jnp.swapaxes(flat.reshape(shape[:-2] + (shape[-1], shape[-2])), -1, -2)
        else:
            out[name] = flat.reshape(shape)
        r += 8 * -(-size // PACK_TILE)
    return out, packed[r, 0]


def _shard2d(name, v):
    v = v.reshape(v.shape[-2:])
    return v.T if name in FFN_T else v


def _unshard(name, v, shape):
    return (v.T if name in FFN_T else v).reshape(shape)


def _train_step(x3, tgt3, wts, ms, vs):
    x, tgt = x3[0], tgt3[0]
    L, D = x.shape
    row = lambda v: v.reshape(1, -1)
    shards = {k: _shard2d(k, wts[k]) for k in SHARDED}
    sends = {k: shards[k] if k == "conv_w_dw" else _cast_bf16(shards[k], "cast_" + k) for k in SHARDED}
    gat = {k: _Gather(sends[k]) for k in SHARDED}
    w = lambda k: gat[k].result[0]

    s = {k: wts[k] for k in SMALL}
    lr, li = s["s5_lam_re"].reshape(1, S5_N), s["s5_lam_im"].reshape(1, S5_N)
    ldt = jnp.repeat(s["s5_log_dt"].reshape(S5_G), S5_P).reshape(1, S5_N)
    brc, bic = _compact_b(s["s5_b_re"].reshape(S5_G, S5_P, S5_GC)), _compact_b(s["s5_b_im"].reshape(S5_G, S5_P, S5_GC))
    crc = _compact_b(s["s5_c_re"].reshape(S5_G, S5_GC, S5_P).transpose(0, 2, 1)).T
    cic = _compact_b(s["s5_c_im"].reshape(S5_G, S5_GC, S5_P).transpose(0, 2, 1)).T
    d_skip, b_glu = row(s["s5_d"]), row(s["s5_b_glu"])
    b_dw, ln_g, ln_b = row(s["conv_b_dw"]), row(s["conv_ln_g"]), row(s["conv_ln_b"])
    g1, gm, g2, gf = row(s["ffn1_norm"]), row(s["mix_norm"]), row(s["ffn2_norm"]), row(s["final_norm"])
    mavg = jnp.asarray(_HEAD_MEAN, dtype=BF16)

    h1 = _rms_fwd(x, g1, "rms1", comm=[gat["ffn1_w_gate"], gat["ffn1_w_up"]])
    a1, b1, act1 = _ffn_up(h1, w("ffn1_w_gate"), w("ffn1_w_up"), "ffn1_up", comm=[gat["ffn1_w_down"]])
    x1, h2 = _ffn_down(act1, w("ffn1_w_down"), x, gm, "ffn1_down",
                       comm=[gat["w_in"], gat["s5_w_glu"], gat["conv_w_dw"], gat["w_out"]])
    u = _mm_grouped(h2, w("w_in"), "in_proj")
    are, aim, bre, bim, cre, cim = _s5_params_fwd(lr, li, ldt, brc, bic, crc, cic)
    are_t, aim_t = are.reshape(S5_TILES, LANES), aim.reshape(S5_TILES, LANES)
    w_glu = w("s5_w_glu").reshape(S5_W, S5_W)
    *states, y_pre, o_s5 = _s5_fwd(u, are_t, aim_t, bre, bim, cre, cim, d_skip, w_glu, b_glu,
                                   comm=[gat["ffn2_w_gate"], gat["ffn2_w_up"]])
    w_dw = w("conv_w_dw").transpose(1, 0, 2).reshape(CONV_K, CONV_W)
    zc, cat = _conv_fwd(u, o_s5, w_dw, b_dw, ln_g, ln_b, mavg, comm=[gat["ffn2_w_down"]])
    w_out = w("w_out").reshape(-1, D)
    x2, h3 = _mix_out(cat, w_out, x1, g2, "mix_out")
    a2, b2, act2 = _ffn_up(h3, w("ffn2_w_gate"), w("ffn2_w_up"), "ffn2_up")
    dx3, dx3b, loss_part, d_gf = _ffn_down_loss(act2, w("ffn2_w_down"), x2, gf, tgt, "ffn2_down_loss")

    gs, sc, waiting = {"final_norm": d_gf}, {}, []

    def grad(key, g):
        if g.shape[1] % (2 * BF16_ROWS) == 0 and g.dtype == BF16:
            waiting.append(_SwapHalf(g, key))
        else:
            sc[key] = _Scatter(g)
            waiting.append(sc[key])

    def carry(call, *args, **kw):
        ops = list(waiting)
        waiting.clear()
        res = call(*args, comm=ops, **kw)
        for op in ops:
            if isinstance(op, _SwapHalf):
                sc[op.key] = _Scatter(_add_halves(*op.result, "add_" + op.key))
                waiting.append(sc[op.key])
        return res

    da2, db2 = _ffn_bwd_act(dx3b, w("ffn2_w_down"), a2, b2, "ffn2_bwd_act")
    grad("ffn2_w_down", _mm_tn(act2, dx3b[None], 0.5, "ffn2_dwd", N_CHIPS))
    grad("ffn2_w_gate", carry(_mm_tn, da2, h3[None], 1.0, "ffn2_dwg", N_CHIPS))
    grad("ffn2_w_up", carry(_mm_tn, db2, h3[None], 1.0, "ffn2_dwu", N_CHIPS))
    dx2, dx2b, gs["ffn2_norm"] = carry(_mm_rmsbwd, [da2, db2], [w("ffn2_w_gate"), w("ffn2_w_up")], False, False, x2, g2,
                                       dx3, "ffn2_bwd_dx")

    dm = _mm_nt(dx2b, w_out, "mix_bwd")
    grad("w_out", _mm_tn(cat[None], dx2b[None], 1.0, "dwout", 1).reshape(N_CHIPS, -1, D))
    (du_s5, d_wglu, gs["s5_b_glu"], gs["s5_d"], d_crc, d_cic, d_bre, d_bim, d_are, d_aim) = carry(
        _s5_bwd, dm, y_pre, u, states, are_t, aim_t, bre, bim, cre, cim, d_skip, w_glu, b_glu)
    grad("s5_w_glu", d_wglu.astype(BF16).reshape(N_CHIPS, -1, S5_W))
    g_lr, g_li, g_ldt, g_brc, g_bic = _s5_params_bwd(lr, li, ldt, brc, bic, d_are.reshape(1, S5_N),
                                                     d_aim.reshape(1, S5_N), d_bre, d_bim)
    gs["s5_lam_re"], gs["s5_lam_im"] = g_lr, g_li
    gs["s5_log_dt"] = jnp.sum(g_ldt.reshape(S5_G, S5_P), axis=1)
    gs["s5_b_re"], gs["s5_b_im"] = _uncompact_b(g_brc), _uncompact_b(g_bic)
    gs["s5_c_re"] = _uncompact_b(d_crc.T).transpose(0, 2, 1)
    gs["s5_c_im"] = _uncompact_b(d_cic.T).transpose(0, 2, 1)
    du, d_wdw, gs["conv_b_dw"], gs["conv_ln_g"], gs["conv_ln_b"] = carry(_conv_bwd, dm, zc, u, du_s5, w_dw, ln_g, ln_b, mavg)
    d_wdw = jnp.sum(d_wdw.reshape(HALO, 8, CONV_W), axis=1)[:CONV_K]
    grad("conv_w_dw", d_wdw.reshape(CONV_K, N_CHIPS, -1).transpose(1, 0, 2))
    grad("w_in", carry(_mm_tn, h2[None], du, 1.0, "dwin", N_CHIPS, b_cols=True))
    dx1, dx1b, gs["mix_norm"] = carry(_mm_rmsbwd, [du], [w("w_in")], True, True, x1, gm, dx2, "in_proj_bwd")

    da1, db1 = carry(_ffn_bwd_act, dx1b, w("ffn1_w_down"), a1, b1, "ffn1_bwd_act")
    grad("ffn1_w_down", _mm_tn(act1, dx1b[None], 0.5, "ffn1_dwd", N_CHIPS))
    grad("ffn1_w_gate", carry(_mm_tn, da1, h1[None], 1.0, "ffn1_dwg", N_CHIPS))
    grad("ffn1_w_up", carry(_mm_tn, db1, h1[None], 1.0, "ffn1_dwu", N_CHIPS))
    grad_x, _, gs["ffn1_norm"] = carry(_mm_rmsbwd, [da1, db1], [w("ffn1_w_gate"), w("ffn1_w_up")], False, False, x, g1,
                                       dx1, "ffn1_bwd_dx")

    out = {}
    gsmall = {k: gs[k].reshape(wts[k].shape) for k in SMALL}
    zero_row = jnp.zeros((1, LANES), F32)
    g_all = carry(_gather_all, _pack_small(gsmall, loss_part))
    res = _adamw_small(_pack_small(s, zero_row), g_all, _pack_small({k: ms[k] for k in SMALL}, zero_row),
                       _pack_small({k: vs[k] for k in SMALL}, zero_row))
    unpacked = [_unpack_small(r, s) for r in res]
    loss = unpacked[0][1]
    for k in SMALL:
        out[k] = [u_[0][k] for u_ in unpacked]

    order = ("ffn2_w_down", "ffn2_w_gate", "ffn2_w_up", "w_out", "s5_w_glu", "conv_w_dw", "w_in", "ffn1_w_down",
             "ffn1_w_gate", "ffn1_w_up")
    back = {}
    for k in order:
        part = carry(_sum_slots, sc[k].result[0], "sum_" + k)
        back[k] = _SwapBack(part) if part.shape != shards[k].shape else _Swap(part)
        waiting.append(back[k])
    for k in order:
        parts = [back[k].result[0]] if isinstance(back[k], _SwapBack) else [back[k].ins[0], back[k].result[0]]
        res = carry(_adamw_sharded, shards[k], parts, _shard2d(k, ms[k]), _shard2d(k, vs[k]), "adamw_" + k)
        out[k] = [_unshard(k, r, wts[k].shape) for r in res]
    return loss, grad_x[None], out


def kernel(x, ffn1_norm, ffn1_w_gate, ffn1_w_up, ffn1_w_down, mix_norm, w_in, s5_lam_re, s5_lam_im, s5_log_dt, s5_b_re, s5_b_im, s5_c_re, s5_c_im, s5_d, s5_w_glu, s5_b_glu, conv_w_dw, conv_b_dw, conv_ln_g, conv_ln_b, w_out, ffn2_norm, ffn2_w_gate, ffn2_w_up, ffn2_w_down, final_norm, loss_target, m_ffn1_norm, m_ffn1_w_gate, m_ffn1_w_up, m_ffn1_w_down, m_mix_norm, m_w_in, m_s5_lam_re, m_s5_lam_im, m_s5_log_dt, m_s5_b_re, m_s5_b_im, m_s5_c_re, m_s5_c_im, m_s5_d, m_s5_w_glu, m_s5_b_glu, m_conv_w_dw, m_conv_b_dw, m_conv_ln_g, m_conv_ln_b, m_w_out, m_ffn2_norm, m_ffn2_w_gate, m_ffn2_w_up, m_ffn2_w_down, m_final_norm, v_ffn1_norm, v_ffn1_w_gate, v_ffn1_w_up, v_ffn1_w_down, v_mix_norm, v_w_in, v_s5_lam_re, v_s5_lam_im, v_s5_log_dt, v_s5_b_re, v_s5_b_im, v_s5_c_re, v_s5_c_im, v_s5_d, v_s5_w_glu, v_s5_b_glu, v_conv_w_dw, v_conv_b_dw, v_conv_ln_g, v_conv_ln_b, v_w_out, v_ffn2_norm, v_ffn2_w_gate, v_ffn2_w_up, v_ffn2_w_down, v_final_norm):
    given = dict(locals())
    wts = {k: given[k] for k in WEIGHTS}
    ms = {k: given["m_" + k] for k in WEIGHTS}
    vs = {k: given["v_" + k] for k in WEIGHTS}
    loss, grad_x, out = _train_step(x, loss_target, wts, ms, vs)
    return (loss, grad_x, *[out[k][0] for k in WEIGHTS], *[out[k][1] for k in WEIGHTS],
            *[out[k][2] for k in WEIGHTS], *[out[k][3] for k in WEIGHTS])
```

```python
import functools

import jax
import jax.numpy as jnp
import numpy as np
from jax import lax
from jax.experimental import pallas as pl
from jax.experimental.pallas import tpu as pltpu

F32, BF16 = jnp.float32, jnp.bfloat16
MESH = pl.DeviceIdType.MESH

EPS = 1e-6
ADAM_LR, ADAM_B1, ADAM_B2, ADAM_EPS, ADAM_WD, ADAM_STEP = 0.001, 0.9, 0.999, 1e-08, 0.01, 10

N_CHIPS = 4
N_DEV = 8
LANES = 128
BF16_ROWS = 16
S5_W, S5_G, S5_GC, S5_P = 512, 32, 16, 64
S5_N = S5_G * S5_P
S5_TILES = S5_N // LANES
S5_HALF = 8
CONV_W, CONV_K, CONV_HD = 512, 31, 64
HALO = 32
CONV_SB = 32
TM = 512
TMS = 1024
TK = 2048
TS = 256
VMEM_LIMIT = 48 << 20
GELU_C0, GELU_C1 = 0.7978845608028654, 0.044715

FFN_T = ("ffn1_w_gate", "ffn1_w_up", "ffn2_w_gate", "ffn2_w_up")
SHARDED = ("ffn1_w_gate", "ffn1_w_up", "ffn1_w_down", "w_in", "s5_w_glu", "conv_w_dw", "w_out",
           "ffn2_w_gate", "ffn2_w_up", "ffn2_w_down")
SMALL = ("ffn1_norm", "mix_norm", "s5_lam_re", "s5_lam_im", "s5_log_dt", "s5_b_re", "s5_b_im", "s5_c_re",
         "s5_c_im", "s5_d", "s5_b_glu", "conv_b_dw", "conv_ln_g", "conv_ln_b", "ffn2_norm", "final_norm")
WEIGHTS = ("ffn1_norm", "ffn1_w_gate", "ffn1_w_up", "ffn1_w_down", "mix_norm", "w_in", "s5_lam_re", "s5_lam_im",
           "s5_log_dt", "s5_b_re", "s5_b_im", "s5_c_re", "s5_c_im", "s5_d", "s5_w_glu", "s5_b_glu", "conv_w_dw",
           "conv_b_dw", "conv_ln_g", "conv_ln_b", "w_out", "ffn2_norm", "ffn2_w_gate", "ffn2_w_up", "ffn2_w_down",
           "final_norm")


def _dot(a, b):
    return jnp.dot(a, b, preferred_element_type=F32)


def _dot_nt(a, b):
    return lax.dot_general(a, b, (((1,), (1,)), ((), ())), preferred_element_type=F32)


def _dot_tn(a, b):
    return lax.dot_general(a, b, (((0,), (0,)), ((), ())), preferred_element_type=F32)


def _colsum(v):
    return jnp.sum(v, axis=0, keepdims=True)


def _sigmoid(v):
    return 1.0 / (1.0 + jnp.exp(-v))


def _accumulate(ref, first, value):
    @pl.when(first)
    def _():
        ref[...] = value

    @pl.when(jnp.logical_not(first))
    def _():
        ref[...] += value


def _position():
    x, y, c = lax.axis_index("x"), lax.axis_index("y"), lax.axis_index("c")
    return x, y, c, [(1 - x, y), (x, 1 - y), (1 - x, 1 - y)]


def _remote(src, dst, sems, send, recv, device):
    return pltpu.make_async_remote_copy(src_ref=src, dst_ref=dst, send_sem=sems.at[send], recv_sem=sems.at[recv],
                                        device_id=device, device_id_type=MESH)


class _Gather:
    def __init__(self, shard):
        self.ins = [shard]
        self.outs = [jax.ShapeDtypeStruct((N_CHIPS,) + shard.shape, shard.dtype)]
        self.rows = shard.shape[0]
        self.halve = shard.dtype == BF16 and self.rows % (2 * BF16_ROWS) == 0
        self.n_sem = 13 if self.halve else 7
        self.result = None

    def _copies(self, ins, outs, sems, s0, pos):
        x, y, c, chips = pos
        src, dst = ins[0], outs[0]
        me = 2 * x + y
        if self.halve:
            hr = self.rows // 2
            mine, theirs = pl.ds(c * hr, hr), pl.ds((1 - c) * hr, hr)
            part = lambda slot, rows: dst.at[slot, rows]
            my_src = src.at[mine]
        else:
            mine = theirs = None
            part = lambda slot, rows: dst.at[slot]
            my_src = src
        slot = lambda j: 2 * chips[j][0] + chips[j][1]
        local = lambda: pltpu.make_async_copy(src, dst.at[me], sems.at[s0])
        send = lambda j: _remote(my_src, part(me, mine), sems, s0 + 1 + j, s0 + 4 + j, (*chips[j], c))
        land = lambda j: _remote(my_src, part(slot(j), mine), sems, s0 + 1 + j, s0 + 4 + j, (*chips[j], c))
        fwd = lambda j: _remote(part(slot(j), mine), part(slot(j), mine), sems, s0 + 7 + j, s0 + 10 + j, (x, y, 1 - c))
        got = lambda j: _remote(part(slot(j), theirs), part(slot(j), theirs), sems, s0 + 7 + j, s0 + 10 + j,
                                (x, y, 1 - c))
        return local, send, land, fwd, got

    def start(self, ins, outs, sems, s0, pos):
        local, send, _, _, _ = self._copies(ins, outs, sems, s0, pos)
        local().start()
        for j in range(N_CHIPS - 1):
            send(j).start()

    def finish(self, ins, outs, sems, s0, pos):
        local, send, land, fwd, got = self._copies(ins, outs, sems, s0, pos)
        others = range(N_CHIPS - 1)
        for j in others:
            land(j).wait_recv()
            if self.halve:
                fwd(j).start()
        for j in others:
            if self.halve:
                got(j).wait_recv()
        for j in others:
            send(j).wait_send()
            if self.halve:
                fwd(j).wait_send()
        local().wait()


class _Scatter:
    def __init__(self, grad):
        self.ins = [grad]
        self.outs = [jax.ShapeDtypeStruct(grad.shape, grad.dtype)]
        self.n_sem = 7
        self.result = None

    def _copies(self, ins, outs, sems, s0, pos):
        x, y, c, chips = pos
        src, dst = ins[0], outs[0]
        me = 2 * x + y
        slot = lambda j: 2 * chips[j][0] + chips[j][1]
        local = lambda: pltpu.make_async_copy(src.at[me], dst.at[me], sems.at[s0])
        send = lambda j: _remote(src.at[slot(j)], dst.at[me], sems, s0 + 1 + j, s0 + 4 + j, (*chips[j], c))
        land = lambda j: _remote(src.at[me], dst.at[slot(j)], sems, s0 + 1 + j, s0 + 4 + j, (*chips[j], c))
        return local, send, land

    def start(self, ins, outs, sems, s0, pos):
        local, send, _ = self._copies(ins, outs, sems, s0, pos)
        local().start()
        for j in range(N_CHIPS - 1):
            send(j).start()

    def finish(self, ins, outs, sems, s0, pos):
        local, send, land = self._copies(ins, outs, sems, s0, pos)
        for j in range(N_CHIPS - 1):
            land(j).wait_recv()
        for j in range(N_CHIPS - 1):
            send(j).wait_send()
        local().wait()


class _Swap:
    def __init__(self, part):
        self.ins = [part]
        self.outs = [jax.ShapeDtypeStruct(part.shape, part.dtype)]
        self.n_sem = 2
        self.result = None

    def _copy(self, ins, outs, sems, s0, pos):
        x, y, c, _ = pos
        return _remote(ins[0], outs[0], sems, s0, s0 + 1, (x, y, 1 - c))

    def start(self, ins, outs, sems, s0, pos):
        self._copy(ins, outs, sems, s0, pos).start()

    def finish(self, ins, outs, sems, s0, pos):
        self._copy(ins, outs, sems, s0, pos).wait()


class _SwapHalf:
    def __init__(self, grad, key):
        slots, rows, cols = grad.shape
        half = jax.ShapeDtypeStruct((slots, rows // 2, cols), grad.dtype)
        self.ins, self.outs, self.key = [grad], [half, half], key
        self.hr = rows // 2
        self.n_sem = 3
        self.result = None

    def _copies(self, ins, outs, sems, s0, pos):
        x, y, c, _ = pos
        mine, theirs = pl.ds(c * self.hr, self.hr), pl.ds((1 - c) * self.hr, self.hr)
        local = pltpu.make_async_copy(ins[0].at[:, mine], outs[0], sems.at[s0])
        remote = _remote(ins[0].at[:, theirs], outs[1], sems, s0 + 1, s0 + 2, (x, y, 1 - c))
        return local, remote

    def start(self, ins, outs, sems, s0, pos):
        for cp in self._copies(ins, outs, sems, s0, pos):
            cp.start()

    def finish(self, ins, outs, sems, s0, pos):
        for cp in self._copies(ins, outs, sems, s0, pos):
            cp.wait()


class _SwapBack:
    def __init__(self, part):
        hr, cols = part.shape
        self.ins, self.outs = [part], [jax.ShapeDtypeStruct((2 * hr, cols), part.dtype)]
        self.hr = hr
        self.n_sem = 3
        self.result = None

    def _copies(self, ins, outs, sems, s0, pos):
        x, y, c, _ = pos
        mine, theirs = pl.ds(c * self.hr, self.hr), pl.ds((1 - c) * self.hr, self.hr)
        local = lambda: pltpu.make_async_copy(ins[0], outs[0].at[mine], sems.at[s0])
        send = lambda: _remote(ins[0], outs[0].at[mine], sems, s0 + 1, s0 + 2, (x, y, 1 - c))
        land = lambda: _remote(ins[0], outs[0].at[theirs], sems, s0 + 1, s0 + 2, (x, y, 1 - c))
        return local, send, land

    def start(self, ins, outs, sems, s0, pos):
        local, send, _ = self._copies(ins, outs, sems, s0, pos)
        local().start()
        send().start()

    def finish(self, ins, outs, sems, s0, pos):
        local, send, land = self._copies(ins, outs, sems, s0, pos)
        land().wait_recv()
        send().wait_send()
        local().wait()


def _pallas(body, args, *, name, grid, in_specs, out_specs, out_shape, scratch_shapes=(), comm=()):
    comm = list(comm)
    n_in, n_out, n_scr = len(in_specs), len(out_specs), len(scratch_shapes)
    c_in = [a for op in comm for a in op.ins]
    c_out = [s for op in comm for s in op.outs]
    n_sem = sum(op.n_sem for op in comm)

    def full(*refs):
        o0 = n_in + len(c_in)
        s0 = o0 + n_out + len(c_out)
        ins, cin = refs[:n_in], refs[n_in:o0]
        outs, cout = refs[o0:o0 + n_out], refs[o0 + n_out:s0]
        scratch = refs[s0:s0 + n_scr]
        if comm:
            sems = refs[s0 + n_scr]
            ids = [pl.program_id(d) for d in range(len(grid))]
            first = functools.reduce(jnp.logical_and, [i == 0 for i in ids])
            last = functools.reduce(jnp.logical_and, [i == g - 1 for i, g in zip(ids, grid)])
            pos = _position()

            def each(step):
                ci = co = cs = 0
                for op in comm:
                    getattr(op, step)(cin[ci:ci + len(op.ins)], cout[co:co + len(op.outs)], sems, cs, pos)
                    ci, co, cs = ci + len(op.ins), co + len(op.outs), cs + op.n_sem

            @pl.when(first)
            def _():
                each("start")

        body(*ins, *outs, *scratch)
        if comm:
            @pl.when(last)
            def _():
                each("finish")

    hbm = pl.BlockSpec(memory_space=pl.ANY)
    res = pl.pallas_call(
        full, name=name, grid=grid,
        in_specs=list(in_specs) + [hbm] * len(c_in), out_specs=list(out_specs) + [hbm] * len(c_out),
        out_shape=list(out_shape) + c_out,
        scratch_shapes=list(scratch_shapes) + ([pltpu.SemaphoreType.DMA((n_sem,))] if comm else []),
        compiler_params=pltpu.CompilerParams(dimension_semantics=("arbitrary",) * len(grid),
                                             vmem_limit_bytes=VMEM_LIMIT))(*args, *c_in)
    k = n_out
    for op in comm:
        op.result = list(res[k:k + len(op.outs)])
        k += len(op.outs)
    return list(res[:n_out])


def _row_tile(rows, cols, itemsize=4, budget=1 << 20):
    t = rows
    while t % (2 * BF16_ROWS) == 0 and t * cols * itemsize > budget:
        t //= 2
    return t


def _cast_bf16(w, name):
    rows, cols = w.shape
    tr = _row_tile(rows, cols)

    def body(w_ref, o_ref):
        o_ref[...] = w_ref[...].astype(BF16)

    spec = pl.BlockSpec((tr, cols), lambda i: (i, 0))
    return _pallas(body, [w], name=name, grid=(rows // tr,), in_specs=[spec], out_specs=[spec],
                   out_shape=[jax.ShapeDtypeStruct((rows, cols), BF16)])[0]


def _rms_fwd(x, g, name, comm=()):
    L, D = x.shape

    def body(x_ref, g_ref, h_ref):
        xf = x_ref[...]
        r = lax.rsqrt(jnp.mean(xf * xf, axis=-1, keepdims=True) + EPS)
        h_ref[...] = (xf * r * g_ref[...]).astype(BF16)

    row = pl.BlockSpec((TMS, D), lambda i: (i, 0))
    return _pallas(body, [x, g], name=name, grid=(L // TMS,),
                   in_specs=[row, pl.BlockSpec((1, D), lambda i: (0, 0))], out_specs=[row],
                   out_shape=[jax.ShapeDtypeStruct((L, D), BF16)], comm=comm)[0]


def _resident(shape):
    return pl.BlockSpec(shape, lambda *_: (0,) * len(shape), pipeline_mode=pl.Buffered(1))


def _ffn_up(h, wg_t, wu_t, name, comm=()):
    L, D = h.shape
    G, FS, _ = wg_t.shape

    def body(h_ref, wg_ref, wu_ref, a_ref, b_ref, act_ref):
        j = pl.program_id(1)
        hv = h_ref[...]
        a = _dot_nt(hv, wg_ref[j])
        b = _dot_nt(hv, wu_ref[j])
        a_ref[...] = a.astype(BF16)
        b_ref[...] = b.astype(BF16)
        act_ref[...] = (a * _sigmoid(a) * b).astype(BF16)

    ospec = pl.BlockSpec((None, TMS, FS), lambda i, j: (j, i, 0))
    oshape = jax.ShapeDtypeStruct((G, L, FS), BF16)
    return _pallas(body, [h, wg_t, wu_t], name=name, grid=(L // TMS, G),
                   in_specs=[pl.BlockSpec((TMS, D), lambda i, j: (i, 0)), _resident((G, FS, D)), _resident((G, FS, D))],
                   out_specs=[ospec, ospec, ospec], out_shape=[oshape, oshape, oshape], comm=comm)


def _group_sum(a_ref, w_ref, groups, mm=_dot):
    acc = mm(a_ref[0], w_ref[0])
    for j in range(1, groups):
        acc = acc + mm(a_ref[j], w_ref[j])
    return acc


def _ffn_down(act, wd, x, g_next, name, comm=()):
    G, L, FS = act.shape
    D = wd.shape[2]

    def body(act_ref, wd_ref, x_ref, g_ref, xn_ref, hn_ref):
        xn = x_ref[...] + 0.5 * _group_sum(act_ref, wd_ref, G)
        xn_ref[...] = xn
        r = lax.rsqrt(jnp.mean(xn * xn, axis=-1, keepdims=True) + EPS)
        hn_ref[...] = (xn * r * g_ref[...]).astype(BF16)

    row = pl.BlockSpec((TM, D), lambda i: (i, 0))
    return _pallas(body, [act, wd, x, g_next], name=name, grid=(L // TM,),
                   in_specs=[pl.BlockSpec((G, TM, FS), lambda i: (0, i, 0)), _resident((G, FS, D)), row,
                             pl.BlockSpec((1, D), lambda i: (0, 0))],
                   out_specs=[row, row],
                   out_shape=[jax.ShapeDtypeStruct((L, D), F32), jax.ShapeDtypeStruct((L, D), BF16)], comm=comm)


def _ffn_down_loss(act, wd, x, gf, tgt, name):
    G, L, FS = act.shape
    D = wd.shape[2]

    def body(act_ref, wd_ref, x_ref, g_ref, t_ref, dx_ref, dxb_ref, loss_ref, dg_ref):
        i = pl.program_id(0)
        xn = x_ref[...] + 0.5 * _group_sum(act_ref, wd_ref, G)
        r = lax.rsqrt(jnp.mean(xn * xn, axis=-1, keepdims=True) + EPS)
        xh = xn * r
        gv = g_ref[...]
        e = xh * gv - t_ref[...]
        part = 0.5 * jnp.sum(_colsum(e * e), axis=1, keepdims=True) / D
        dy = e / D
        _accumulate(loss_ref, i == 0, jnp.broadcast_to(part, (1, LANES)))
        _accumulate(dg_ref, i == 0, _colsum(dy * xh))
        dxh = dy * gv
        dx = r * (dxh - xh * jnp.mean(dxh * xh, axis=-1, keepdims=True))
        dx_ref[...] = dx
        dxb_ref[...] = dx.astype(BF16)

    row = pl.BlockSpec((TM, D), lambda i: (i, 0))
    return _pallas(body, [act, wd, x, gf, tgt], name=name, grid=(L // TM,),
                   in_specs=[pl.BlockSpec((G, TM, FS), lambda i: (0, i, 0)), _resident((G, FS, D)), row,
                             pl.BlockSpec((1, D), lambda i: (0, 0)), row],
                   out_specs=[row, row, pl.BlockSpec((1, LANES), lambda i: (0, 0)),
                              pl.BlockSpec((1, D), lambda i: (0, 0))],
                   out_shape=[jax.ShapeDtypeStruct((L, D), F32), jax.ShapeDtypeStruct((L, D), BF16),
                              jax.ShapeDtypeStruct((1, LANES), F32), jax.ShapeDtypeStruct((1, D), F32)])


def _ffn_bwd_act(dxb, wd, a, b, name, comm=()):
    L, D = dxb.shape
    G, FS, _ = wd.shape

    def body(dx_ref, wd_ref, a_ref, b_ref, da_ref, db_ref):
        dact = 0.5 * _dot_nt(dx_ref[...], wd_ref[pl.program_id(1)])
        av = a_ref[...].astype(F32)
        bv = b_ref[...].astype(F32)
        sg = _sigmoid(av)
        da_ref[...] = (dact * bv * sg * (1.0 + av * (1.0 - sg))).astype(BF16)
        db_ref[...] = (dact * av * sg).astype(BF16)

    gspec = pl.BlockSpec((None, TMS, FS), lambda i, j: (j, i, 0))
    oshape = jax.ShapeDtypeStruct((G, L, FS), BF16)
    return _pallas(body, [dxb, wd, a, b], name=name, grid=(L // TMS, G),
                   in_specs=[pl.BlockSpec((TMS, D), lambda i, j: (i, 0)), _resident((G, FS, D)), gspec, gspec],
                   out_specs=[gspec, gspec], out_shape=[oshape, oshape], comm=comm)


def _mm_grouped(a, w, name):
    L, K = a.shape
    G, _, N = w.shape

    def body(a_ref, w_ref, o_ref):
        o_ref[...] = _dot(a_ref[...], w_ref[pl.program_id(1)])

    return _pallas(body, [a, w], name=name, grid=(L // TMS, G),
                   in_specs=[pl.BlockSpec((TMS, K), lambda i, g: (i, 0)), _resident((G, K, N))],
                   out_specs=[pl.BlockSpec((TMS, N), lambda i, g: (i, g))],
                   out_shape=[jax.ShapeDtypeStruct((L, G * N), F32)])[0]


def _mm_nt(a, w, name):
    L, K = a.shape
    N = w.shape[0]

    def body(a_ref, w_ref, o_ref):
        o_ref[...] = _dot_nt(a_ref[...], w_ref[...]).astype(BF16)

    return _pallas(body, [a, w], name=name, grid=(L // TMS,),
                   in_specs=[pl.BlockSpec((TMS, K), lambda i: (i, 0)), _resident((N, K))],
                   out_specs=[pl.BlockSpec((TMS, N), lambda i: (i, 0))],
                   out_shape=[jax.ShapeDtypeStruct((L, N), BF16)])[0]


def _mm_tn(a, b, scale, name, groups, b_cols=False, comm=()):
    L, M = a.shape[1], a.shape[2]
    N = b.shape[1] // groups if b_cols else b.shape[2]
    tk = min(L, TK)
    nk = L // tk

    def spec(v, cols):
        if cols:
            return pl.BlockSpec((tk, v.shape[1] // groups), lambda g, k: (k, g))
        if v.shape[0] > 1:
            return pl.BlockSpec((None, tk, v.shape[2]), lambda g, k: (g, k, 0))
        return pl.BlockSpec((None, tk, v.shape[2]), lambda g, k: (0, k, 0))

    def body(a_ref, b_ref, o_ref, acc):
        k = pl.program_id(1)
        p = _dot_tn(a_ref[...], b_ref[...])
        if nk == 1:
            o_ref[...] = (p * scale).astype(BF16)
        else:
            _accumulate(acc, k == 0, p)

            @pl.when(k == nk - 1)
            def _():
                o_ref[...] = (acc[...] * scale).astype(BF16)

    return _pallas(body, [a, b], name=name, grid=(groups, nk),
                   in_specs=[spec(a, False), spec(b, b_cols)],
                   out_specs=[pl.BlockSpec((None, M, N), lambda g, k: (g, 0, 0))],
                   out_shape=[jax.ShapeDtypeStruct((groups, M, N), BF16)],
                   scratch_shapes=[pltpu.VMEM((M, N), F32)], comm=comm)[0]


def _mm_rmsbwd(a_list, w_list, nt, a_cols, x_in, g, dx_out, name, comm=()):
    P = len(a_list)
    G = w_list[0].shape[0]
    L, D = x_in.shape
    mm = _dot_nt if nt else _dot

    def body(*refs):
        a_refs, w_refs = refs[:P], refs[P:2 * P]
        x_ref, g_ref, dxo_ref, dx_ref, dxb_ref, dg_ref = refs[2 * P:]
        i = pl.program_id(0)
        dh = None
        for a_ref, w_ref in zip(a_refs, w_refs):
            for j in range(G):
                if a_cols:
                    kw = a_ref.shape[1] // G
                    term = mm(a_ref[:, j * kw:(j + 1) * kw], w_ref[j])
                else:
                    term = mm(a_ref[j], w_ref[j])
                dh = term if dh is None else dh + term
        xf = x_ref[...]
        r = lax.rsqrt(jnp.mean(xf * xf, axis=-1, keepdims=True) + EPS)
        xh = xf * r
        _accumulate(dg_ref, i == 0, _colsum(dh * xh))
        dxh = dh * g_ref[...]
        dx = dxo_ref[...] + r * (dxh - xh * jnp.mean(dxh * xh, axis=-1, keepdims=True))
        dx_ref[...] = dx
        dxb_ref[...] = dx.astype(BF16)

    row = pl.BlockSpec((TM, D), lambda i: (i, 0))
    vec = pl.BlockSpec((1, D), lambda i: (0, 0))
    if a_cols:
        a_specs = [pl.BlockSpec((TM, a.shape[1]), lambda i: (i, 0)) for a in a_list]
    else:
        a_specs = [pl.BlockSpec((G, TM, a.shape[2]), lambda i: (0, i, 0)) for a in a_list]
    w_specs = [_resident(w.shape) for w in w_list]
    return _pallas(body, [*a_list, *w_list, x_in, g, dx_out], name=name, grid=(L // TM,),
                   in_specs=a_specs + w_specs + [row, vec, row], out_specs=[row, row, vec],
                   out_shape=[jax.ShapeDtypeStruct((L, D), F32), jax.ShapeDtypeStruct((L, D), BF16),
                              jax.ShapeDtypeStruct((1, D), F32)], comm=comm)


def _mix_out(cat, wout, x1, g_next, name):
    L, K = cat.shape
    D = wout.shape[1]

    def body(c_ref, w_ref, x_ref, g_ref, xn_ref, hn_ref):
        xn = x_ref[...] + _dot(c_ref[...], w_ref[...])
        xn_ref[...] = xn
        r = lax.rsqrt(jnp.mean(xn * xn, axis=-1, keepdims=True) + EPS)
        hn_ref[...] = (xn * r * g_ref[...]).astype(BF16)

    row = pl.BlockSpec((TMS, D), lambda i: (i, 0))
    return _pallas(body, [cat, wout, x1, g_next], name=name, grid=(L // TMS,),
                   in_specs=[pl.BlockSpec((TMS, K), lambda i: (i, 0)), pl.BlockSpec((K, D), lambda i: (0, 0)), row,
                             pl.BlockSpec((1, D), lambda i: (0, 0))],
                   out_specs=[row, row],
                   out_shape=[jax.ShapeDtypeStruct((L, D), F32), jax.ShapeDtypeStruct((L, D), BF16)])


def _s5_disc(lr, li, ldt, brc, bic):
    dt = jnp.exp(ldt)
    mag = jnp.exp(lr * dt)
    are = mag * jnp.cos(li * dt)
    aim = mag * jnp.sin(li * dt)
    den = lr * lr + li * li
    nre = are - 1.0
    fre = (nre * lr + aim * li) / den
    fim = (aim * lr - nre * li) / den
    return are, aim, fre * brc - fim * bic, fre * bic + fim * brc


def _s5_params_fwd(lr, li, ldt, brc, bic, crc, cic):
    def body(lr_ref, li_ref, ldt_ref, br_ref, bi_ref, cr_ref, ci_ref, are_ref, aim_ref, bre_ref, bim_ref, cre_ref, cim_ref):
        are, aim, bre, bim = _s5_disc(lr_ref[...], li_ref[...], ldt_ref[...], br_ref[...], bi_ref[...])
        are_ref[...] = are
        aim_ref[...] = aim
        bre_ref[...] = bre.astype(BF16)
        bim_ref[...] = bim.astype(BF16)
        cre_ref[...] = cr_ref[...].astype(BF16)
        cim_ref[...] = ci_ref[...].astype(BF16)

    vec = jax.ShapeDtypeStruct((1, S5_N), F32)
    return pl.pallas_call(
        body, name="s5_params_fwd",
        out_shape=[vec, vec, jax.ShapeDtypeStruct((LANES, S5_N), BF16), jax.ShapeDtypeStruct((LANES, S5_N), BF16),
                   jax.ShapeDtypeStruct((S5_N, LANES), BF16), jax.ShapeDtypeStruct((S5_N, LANES), BF16)],
        compiler_params=pltpu.CompilerParams(vmem_limit_bytes=VMEM_LIMIT))(lr, li, ldt, brc, bic, crc, cic)


def _s5_params_bwd(lr, li, ldt, brc, bic, dare, daim, dbre, dbim):
    def body(lr_ref, li_ref, ldt_ref, br_ref, bi_ref, dare_ref, daim_ref, dbre_ref, dbim_ref,
             glr_ref, gli_ref, gldt_ref, gbr_ref, gbi_ref):
        _, vjp = jax.vjp(_s5_disc, lr_ref[...], li_ref[...], ldt_ref[...], br_ref[...], bi_ref[...])
        glr, gli, gldt, gbr, gbi = vjp((dare_ref[...], daim_ref[...], dbre_ref[...], dbim_ref[...]))
        glr_ref[...] = glr
        gli_ref[...] = gli
        gldt_ref[...] = gldt
        gbr_ref[...] = gbr
        gbi_ref[...] = gbi

    vec = jax.ShapeDtypeStruct((1, S5_N), F32)
    mat = jax.ShapeDtypeStruct((LANES, S5_N), F32)
    return pl.pallas_call(
        body, name="s5_params_bwd", out_shape=[vec, vec, vec, mat, mat],
        compiler_params=pltpu.CompilerParams(vmem_limit_bytes=VMEM_LIMIT))(lr, li, ldt, brc, bic, dare, daim, dbre, dbim)


def _gelu_parts(y):
    th = jnp.tanh(GELU_C0 * (y + GELU_C1 * y * y * y))
    return 0.5 * y * (1.0 + th), th


def _state_rows(q, T):
    return pl.ds(q % S5_HALF, T, stride=S5_HALF)


def _load_tiles(bufs, ct, T, dtype):
    return jnp.concatenate([bufs[q // S5_HALF][_state_rows(q, T), :].astype(dtype) for q in range(4 * ct, 4 * ct + 4)],
                           axis=1)


def _store_tiles(bufs, ct, T, value):
    for k, q in enumerate(range(4 * ct, 4 * ct + 4)):
        bufs[q // S5_HALF][_state_rows(q, T), :] = value[:, k * LANES:(k + 1) * LANES]


def _s5_fwd(u, are, aim, bre, bim, cre, cim, d_skip, wglu, bglu, comm=()):
    L = u.shape[0]
    T = min(TS, L)
    n = L // T

    def body(u_ref, are_ref, aim_ref, bre_ref, bim_ref, cre_ref, cim_ref, d_ref, wg_ref, bg_ref,
             sre_lo, sre_hi, sim_lo, sim_hi, y_ref, o_ref, st_re, st_im):
        i = pl.program_id(0)
        sre, sim = (sre_lo, sre_hi), (sim_lo, sim_hi)

        @pl.when(i == 0)
        def _():
            st_re[...] = jnp.zeros_like(st_re)
            st_im[...] = jnp.zeros_like(st_im)

        uf = u_ref[...]
        ub = uf.astype(BF16)
        for ct in range(4):
            uq = ub[:, ct * LANES:(ct + 1) * LANES]
            win = slice(4 * ct * LANES, 4 * (ct + 1) * LANES)
            _store_tiles(sre, ct, T, _dot(uq, bre_ref[:, win]))
            _store_tiles(sim, ct, T, _dot(uq, bim_ref[:, win]))
        halves = [slice(h * S5_HALF, (h + 1) * S5_HALF) for h in range(2)]
        a_re = [are_ref[hs, :] for hs in halves]
        a_im = [aim_ref[hs, :] for hs in halves]

        def step(t, carry):
            rows = pl.ds(pl.multiple_of(t * S5_HALF, S5_HALF), S5_HALF)
            out = []
            for h in range(2):
                s_re, s_im = carry[2 * h], carry[2 * h + 1]
                n_re = a_re[h] * s_re - a_im[h] * s_im + sre[h][rows, :]
                n_im = a_re[h] * s_im + a_im[h] * s_re + sim[h][rows, :]
                sre[h][rows, :] = n_re
                sim[h][rows, :] = n_im
                out += [n_re, n_im]
            return tuple(out)

        init = (st_re[halves[0], :], st_im[halves[0], :], st_re[halves[1], :], st_im[halves[1], :])
        fin = lax.fori_loop(0, T, step, init, unroll=4)
        for h in range(2):
            st_re[halves[h], :] = fin[2 * h]
            st_im[halves[h], :] = fin[2 * h + 1]
        tiles = []
        for ct in range(4):
            win = slice(4 * ct * LANES, 4 * (ct + 1) * LANES)
            tiles.append(_dot(_load_tiles(sre, ct, T, BF16), cre_ref[win, :])
                         - _dot(_load_tiles(sim, ct, T, BF16), cim_ref[win, :]))
        y = jnp.concatenate(tiles, axis=1) + d_ref[...] * uf
        y_ref[...] = y
        yg, _ = _gelu_parts(y)
        gate = _sigmoid(_dot(yg.astype(BF16), wg_ref[...]) + bg_ref[...])
        o_ref[...] = (yg * gate).astype(BF16)

    const = lambda shape: pl.BlockSpec(shape, lambda i: (0, 0))
    sspec = pl.BlockSpec((T * S5_HALF, LANES), lambda i: (i, 0))
    sshape = jax.ShapeDtypeStruct((L * S5_HALF, LANES), F32)
    chunk = pl.BlockSpec((T, S5_W), lambda i: (i, 0))
    return _pallas(body, [u, are, aim, bre, bim, cre, cim, d_skip, wglu, bglu], name="s5_fwd", grid=(n,),
                   in_specs=[chunk, const((S5_TILES, LANES)), const((S5_TILES, LANES)),
                             const((LANES, S5_N)), const((LANES, S5_N)), const((S5_N, LANES)), const((S5_N, LANES)),
                             const((1, S5_W)), const((S5_W, S5_W)), const((1, S5_W))],
                   out_specs=[sspec] * 4 + [chunk, chunk],
                   out_shape=[sshape] * 4 + [jax.ShapeDtypeStruct((L, S5_W), F32), jax.ShapeDtypeStruct((L, S5_W), BF16)],
                   scratch_shapes=[pltpu.VMEM((S5_TILES, LANES), F32), pltpu.VMEM((S5_TILES, LANES), F32)], comm=comm)


def _s5_bwd(dm, y_pre, u, states, are, aim, bre, bim, cre, cim, d_skip, wglu, bglu, comm=()):
    L = u.shape[0]
    T = min(TS, L)
    n = L // T

    def body(dm_ref, y_ref, u_ref, sre_lo, sre_hi, sim_lo, sim_hi, pre_lo, pre_hi, pim_lo, pim_hi,
             are_ref, aim_ref, bre_ref, bim_ref, cre_ref, cim_ref, d_ref, wg_ref, bg_ref,
             du_ref, dwg_ref, dbg_ref, dd_ref, dcre_ref, dcim_ref, dbre_ref, dbim_ref, dare_ref, daim_ref,
             gre_lo, gre_hi, gim_lo, gim_hi, car_re, car_im):
        i = pl.program_id(0)
        first = i == 0
        sre, sim = (sre_lo, sre_hi), (sim_lo, sim_hi)
        gre, gim = (gre_lo, gre_hi), (gim_lo, gim_hi)
        pre, pim = (pre_lo, pre_hi), (pim_lo, pim_hi)

        @pl.when(first)
        def _():
            car_re[...] = jnp.zeros_like(car_re)
            car_im[...] = jnp.zeros_like(car_im)
            dcre_ref[...] = jnp.zeros_like(dcre_ref)
            dcim_ref[...] = jnp.zeros_like(dcim_ref)
            dbre_ref[...] = jnp.zeros_like(dbre_ref)
            dbim_ref[...] = jnp.zeros_like(dbim_ref)

        y = y_ref[...]
        uf = u_ref[...]
        yg, th = _gelu_parts(y)
        dgelu = 0.5 * (1.0 + th) + 0.5 * y * (1.0 - th * th) * GELU_C0 * (1.0 + 3.0 * GELU_C1 * y * y)
        ygb = yg.astype(BF16)
        sg = _sigmoid(_dot(ygb, wg_ref[...]) + bg_ref[...])
        dout = dm_ref[...].astype(F32)
        dgp = dout * yg * sg * (1.0 - sg)
        dgpb = dgp.astype(BF16)
        dyg = dout * sg + _dot_nt(dgpb, wg_ref[...])
        _accumulate(dwg_ref, first, _dot_tn(ygb, dgpb))
        _accumulate(dbg_ref, first, _colsum(dgp))
        dy = dyg * dgelu
        _accumulate(dd_ref, first, _colsum(dy * uf))
        dyb = dy.astype(BF16)
        ub = uf.astype(BF16)

        for ct in range(4):
            win = slice(4 * ct * LANES, 4 * (ct + 1) * LANES)
            dyq = dyb[:, ct * LANES:(ct + 1) * LANES]
            dcre_ref[win, :] += _dot_tn(_load_tiles(sre, ct, T, BF16), dyq)
            dcim_ref[win, :] -= _dot_tn(_load_tiles(sim, ct, T, BF16), dyq)
            _store_tiles(gre, ct, T, _dot_nt(dyq, cre_ref[win, :]))
            _store_tiles(gim, ct, T, -_dot_nt(dyq, cim_ref[win, :]))

        halves = [slice(h * S5_HALF, (h + 1) * S5_HALF) for h in range(2)]
        a_re = [are_ref[hs, :] for hs in halves]
        a_im = [aim_ref[hs, :] for hs in halves]

        def adjoint(t, h, g_re, g_im):
            rows = pl.ds(pl.multiple_of(t * S5_HALF, S5_HALF), S5_HALF)
            n_re = gre[h][rows, :] + a_re[h] * g_re + a_im[h] * g_im
            n_im = gim[h][rows, :] + a_re[h] * g_im - a_im[h] * g_re
            gre[h][rows, :] = n_re
            gim[h][rows, :] = n_im
            return n_re, n_im

        def step(k, carry):
            out = []
            for h in range(2):
                out += adjoint(T - 1 - k, h, carry[2 * h], carry[2 * h + 1])
            return tuple(out)

        init = (car_re[halves[0], :], car_im[halves[0], :], car_re[halves[1], :], car_im[halves[1], :])
        fin = lax.fori_loop(0, T, step, init, unroll=4)
        keep = (i < n - 1).astype(F32)
        later, earlier = slice(S5_HALF, T * S5_HALF), slice(0, (T - 1) * S5_HALF)
        fold = lambda v: jnp.sum(v.reshape(T - 1, S5_HALF, LANES), axis=0)
        for h in range(2):
            g_re, g_im = fin[2 * h], fin[2 * h + 1]
            car_re[halves[h], :] = g_re
            car_im[halves[h], :] = g_im
            p_re, p_im = pre[h][...] * keep, pim[h][...] * keep
            da_re = fold(gre[h][later, :] * sre[h][earlier, :] + gim[h][later, :] * sim[h][earlier, :])
            da_im = fold(gim[h][later, :] * sre[h][earlier, :] - gre[h][later, :] * sim[h][earlier, :])
            da_re = da_re + g_re * p_re + g_im * p_im
            da_im = da_im + g_im * p_re - g_re * p_im

            @pl.when(first)
            def _():
                dare_ref[halves[h], :] = da_re
                daim_ref[halves[h], :] = da_im

            @pl.when(jnp.logical_not(first))
            def _():
                dare_ref[halves[h], :] += da_re
                daim_ref[halves[h], :] += da_im

        tiles = []
        for ct in range(4):
            uq = ub[:, ct * LANES:(ct + 1) * LANES]
            win = slice(4 * ct * LANES, 4 * (ct + 1) * LANES)
            g_re, g_im = _load_tiles(gre, ct, T, BF16), _load_tiles(gim, ct, T, BF16)
            tiles.append(d_ref[:, ct * LANES:(ct + 1) * LANES] * dy[:, ct * LANES:(ct + 1) * LANES]
                         + _dot_nt(g_re, bre_ref[:, win]) + _dot_nt(g_im, bim_ref[:, win]))
            dbre_ref[:, win] += _dot_tn(uq, g_re)
            dbim_ref[:, win] += _dot_tn(uq, g_im)
        du_ref[...] = jnp.concatenate(tiles, axis=1).astype(BF16)

    rev = lambda i: (n - 1 - i, 0)
    const = lambda shape: pl.BlockSpec(shape, lambda i: (0, 0))
    chunk = pl.BlockSpec((T, S5_W), rev)
    sspec = pl.BlockSpec((T * S5_HALF, LANES), rev)
    pspec = pl.BlockSpec((S5_HALF, LANES), lambda i: (jnp.maximum((n - 1 - i) * T - 1, 0), 0))
    tile = jax.ShapeDtypeStruct((S5_TILES, LANES), F32)
    vec = jax.ShapeDtypeStruct((1, S5_W), F32)
    sbuf = pltpu.VMEM((T * S5_HALF, LANES), F32)
    return _pallas(
        body, [dm, y_pre, u, *states, *states, are, aim, bre, bim, cre, cim, d_skip, wglu, bglu],
        name="s5_bwd", grid=(n,),
        in_specs=[chunk, chunk, chunk] + [sspec] * 4 + [pspec] * 4 + [
            const((S5_TILES, LANES)), const((S5_TILES, LANES)), const((LANES, S5_N)), const((LANES, S5_N)),
            const((S5_N, LANES)), const((S5_N, LANES)), const((1, S5_W)), const((S5_W, S5_W)), const((1, S5_W))],
        out_specs=[chunk, const((S5_W, S5_W)), const((1, S5_W)), const((1, S5_W)), const((S5_N, LANES)),
                   const((S5_N, LANES)), const((LANES, S5_N)), const((LANES, S5_N)), const((S5_TILES, LANES)),
                   const((S5_TILES, LANES))],
        out_shape=[jax.ShapeDtypeStruct((L, S5_W), BF16), jax.ShapeDtypeStruct((S5_W, S5_W), F32), vec, vec,
                   jax.ShapeDtypeStruct((S5_N, LANES), F32), jax.ShapeDtypeStruct((S5_N, LANES), F32),
                   jax.ShapeDtypeStruct((LANES, S5_N), F32), jax.ShapeDtypeStruct((LANES, S5_N), F32), tile, tile],
        scratch_shapes=[sbuf, sbuf, sbuf, sbuf, pltpu.VMEM((S5_TILES, LANES), F32), pltpu.VMEM((S5_TILES, LANES), F32)],
        comm=comm)


_EYE8 = np.eye(8, dtype=np.float32)


def _compact_b(b):
    return jnp.einsum("akpc,kj->jcakp", b.reshape(4, 8, S5_P, S5_GC), _EYE8).reshape(LANES, S5_N)


def _uncompact_b(m):
    return jnp.einsum("kcakp->akpc", m.reshape(8, S5_GC, 4, 8, S5_P)).reshape(S5_G, S5_P, S5_GC)


_HEAD_MEAN = np.kron(np.eye(CONV_W // CONV_HD, dtype=np.float32), np.full((CONV_HD, CONV_HD), 1.0 / CONV_HD, np.float32))


def _head_mean(v, m):
    hi = v.astype(BF16)
    lo = (v - hi.astype(F32)).astype(BF16)
    return _dot(hi, m) + _dot(lo, m)


def _head_norm(zc, m):
    d = zc - _head_mean(zc, m)
    rstd = lax.rsqrt(_head_mean(d * d, m) + EPS)
    return d * rstd, rstd


def _taps_by_phase(first):
    groups = {}
    for k in range(CONV_K):
        m, s = divmod(first + k, 8)
        groups.setdefault(s, []).append((m, k))
    return groups


def _causal_taps(buf, w_ref, r0, first, flip):
    acc = None
    for s, taps in sorted(_taps_by_phase(first).items()):
        rows = CONV_SB + (8 if s else 0)
        y = None
        for m, k in taps:
            kk = CONV_K - 1 - k if flip else k
            term = w_ref[kk:kk + 1, :] * buf[pl.ds(r0 + 8 * m, rows), :]
            y = term if y is None else y + term
        y = y[s:s + CONV_SB, :]
        acc = y if acc is None else acc + y
    return acc


def _conv_fwd(u, o_s5, wdw, bdw, lng, lnb, mavg, comm=()):
    L = u.shape[0]
    T = min(TS, L)
    n = L // T
    first_tap = HALO - (CONV_K - 1)

    def body(v1_ref, v2_ref, s5_ref, w_ref, b_ref, g_ref, be_ref, m_ref, zc_ref, o_ref, zbuf):
        i = pl.program_id(0)

        @pl.when(i == 0)
        def _():
            zbuf[0:HALO, :] = jnp.zeros((HALO, CONV_W), F32)

        zbuf[HALO:HALO + T, :] = v1_ref[...] * _sigmoid(v2_ref[...])
        for r0 in range(0, T, CONV_SB):
            zc_ref[r0:r0 + CONV_SB, :] = b_ref[...] + _causal_taps(zbuf, w_ref, r0, first_tap, False)
        zbuf[0:HALO, :] = zbuf[T:T + HALO, :]
        zn, _ = _head_norm(zc_ref[...], m_ref[...])
        zz = zn * g_ref[...] + be_ref[...]
        o_ref[:, 0:S5_W] = s5_ref[...]
        o_ref[:, S5_W:S5_W + CONV_W] = (zz * _sigmoid(zz)).astype(BF16)

    const = lambda shape: pl.BlockSpec(shape, lambda i: (0, 0))
    vec = const((1, CONV_W))
    return _pallas(body, [u, u, o_s5, wdw, bdw, lng, lnb, mavg], name="conv_fwd", grid=(n,),
                   in_specs=[pl.BlockSpec((T, CONV_W), lambda i: (i, 1)), pl.BlockSpec((T, CONV_W), lambda i: (i, 2)),
                             pl.BlockSpec((T, S5_W), lambda i: (i, 0)), const((CONV_K, CONV_W)), vec, vec, vec,
                             const((CONV_W, CONV_W))],
                   out_specs=[pl.BlockSpec((T, CONV_W), lambda i: (i, 0)),
                              pl.BlockSpec((T, S5_W + CONV_W), lambda i: (i, 0))],
                   out_shape=[jax.ShapeDtypeStruct((L, CONV_W), F32), jax.ShapeDtypeStruct((L, S5_W + CONV_W), BF16)],
                   scratch_shapes=[pltpu.VMEM((T + HALO, CONV_W), F32)], comm=comm)


def _conv_bwd(dm, zc, u, du_s5, wdw, lng, lnb, mavg, comm=()):
    L = u.shape[0]
    T = min(TS, L)
    n = L // T
    hb = T // HALO
    first_tap = HALO - (CONV_K - 1)

    def body(dm_ref, zc_ref, v1_ref, v2_ref, p1_ref, p2_ref, s5_ref, w_ref, g_ref, be_ref, m_ref,
             du_ref, dw_ref, db_ref, dg_ref, dbe_ref, zbuf, dzbuf, head):
        i = pl.program_id(0)
        first = i == 0

        @pl.when(first)
        def _():
            head[...] = jnp.zeros_like(head)
            dw_ref[...] = jnp.zeros_like(dw_ref)

        zn, rstd = _head_norm(zc_ref[...], m_ref[...])
        zz = zn * g_ref[...] + be_ref[...]
        sg = _sigmoid(zz)
        dzz = dm_ref[...].astype(F32) * sg * (1.0 + zz * (1.0 - sg))
        _accumulate(dbe_ref, first, _colsum(dzz))
        _accumulate(dg_ref, first, _colsum(dzz * zn))
        dzn = dzz * g_ref[...]
        dzc = rstd * (dzn - _head_mean(dzn, m_ref[...]) - zn * _head_mean(dzn * zn, m_ref[...]))
        _accumulate(db_ref, first, _colsum(dzc))

        dzbuf[0:T, :] = dzc
        dzbuf[T:T + HALO, :] = head[...]
        head[...] = dzbuf[0:HALO, :]
        keep = (i < n - 1).astype(F32)
        zbuf[0:HALO, :] = p1_ref[...] * _sigmoid(p2_ref[...]) * keep
        zbuf[HALO:HALO + T, :] = v1_ref[...] * _sigmoid(v2_ref[...])
        du_ref[:, 0:S5_W] = s5_ref[...]

        for r0 in range(0, T, CONV_SB):
            rows = slice(r0, r0 + CONV_SB)
            dzc_b = dzbuf[rows, :]
            for s, taps in sorted(_taps_by_phase(first_tap).items()):
                pad = ([jnp.zeros((s, CONV_W), F32)] if s else []) + [dzc_b] + ([jnp.zeros((8 - s, CONV_W), F32)] if s else [])
                shifted = jnp.concatenate(pad, axis=0) if s else dzc_b
                n_rows = shifted.shape[0]
                for m, k in taps:
                    prod = shifted * zbuf[pl.ds(r0 + 8 * m, n_rows), :]
                    dw_ref[8 * k:8 * k + 8, :] += jnp.sum(prod.reshape(n_rows // 8, 8, CONV_W), axis=0)
            dz = _causal_taps(dzbuf, w_ref, r0, 0, True)
            v1 = v1_ref[rows, :]
            sg2 = _sigmoid(v2_ref[rows, :])
            du_ref[rows, S5_W:S5_W + CONV_W] = (dz * sg2).astype(BF16)
            du_ref[rows, S5_W + CONV_W:S5_W + 2 * CONV_W] = (dz * v1 * sg2 * (1.0 - sg2)).astype(BF16)

    rev = lambda c: (lambda i: (n - 1 - i, c))
    prev = lambda c: (lambda i: (jnp.maximum((n - 1 - i) * hb - 1, 0), c))
    const = lambda shape: pl.BlockSpec(shape, lambda i: (0, 0))
    vec = const((1, CONV_W))
    vshape = jax.ShapeDtypeStruct((1, CONV_W), F32)
    return _pallas(
        body, [dm, zc, u, u, u, u, du_s5, wdw, lng, lnb, mavg], name="conv_bwd", grid=(n,),
        in_specs=[pl.BlockSpec((T, CONV_W), rev(1)), pl.BlockSpec((T, CONV_W), rev(0)),
                  pl.BlockSpec((T, CONV_W), rev(1)), pl.BlockSpec((T, CONV_W), rev(2)),
                  pl.BlockSpec((HALO, CONV_W), prev(1)), pl.BlockSpec((HALO, CONV_W), prev(2)),
                  pl.BlockSpec((T, S5_W), rev(0)), const((CONV_K, CONV_W)), vec, vec, const((CONV_W, CONV_W))],
        out_specs=[pl.BlockSpec((T, S5_W + 2 * CONV_W), rev(0)), const((8 * HALO, CONV_W)), vec, vec, vec],
        out_shape=[jax.ShapeDtypeStruct((L, S5_W + 2 * CONV_W), BF16), jax.ShapeDtypeStruct((8 * HALO, CONV_W), F32),
                   vshape, vshape, vshape],
        scratch_shapes=[pltpu.VMEM((T + HALO, CONV_W), F32), pltpu.VMEM((T + HALO, CONV_W), F32),
                        pltpu.VMEM((HALO, CONV_W), F32)], comm=comm)


def _gather_all(v, comm=()):
    rows, cols = v.shape

    def body(x_ref, out_ref, send_sems, recv_sems, local_sem):
        x, y, c, chips = _position()
        me, sibling = (x, y, c), (x, y, 1 - c)

        def block(px, py, pc):
            return out_ref.at[pl.ds((4 * px + 2 * py + pc) * rows, rows), :]

        def copy(k, blk, to, src=None):
            return pltpu.make_async_remote_copy(
                src_ref=block(*blk) if src is None else src, dst_ref=block(*blk), send_sem=send_sems.at[k],
                recv_sem=recv_sems.at[k], device_id=to, device_id_type=MESH)

        mine = pltpu.make_async_copy(x_ref, block(*me), local_sem)
        mine.start()
        first = [copy(0, me, sibling, src=x_ref)]
        first += [copy(1 + j, me, (*chip, c), src=x_ref) for j, chip in enumerate(chips)]
        for cp in first:
            cp.start()
        passed = [copy(4 + j, (*chip, c), sibling) for j, chip in enumerate(chips)]
        for j, chip in enumerate(chips):
            copy(1 + j, (*chip, c), me).wait_recv()
            passed[j].start()
        copy(0, sibling, me).wait_recv()
        for j, chip in enumerate(chips):
            copy(4 + j, (*chip, 1 - c), me).wait_recv()
        for cp in first + passed:
            cp.wait_send()
        mine.wait()

    whole = pl.BlockSpec(memory_space=pltpu.VMEM)
    return _pallas(body, [v], name="gather_small", grid=(1,), in_specs=[whole], out_specs=[whole],
                   out_shape=[jax.ShapeDtypeStruct((N_DEV * rows, cols), v.dtype)],
                   scratch_shapes=[pltpu.SemaphoreType.DMA((7,)), pltpu.SemaphoreType.DMA((7,)), pltpu.SemaphoreType.DMA],
                   comm=comm)[0]


def _adamw(w, g, m, v):
    m = ADAM_B1 * m + (1.0 - ADAM_B1) * g
    v = ADAM_B2 * v + (1.0 - ADAM_B2) * jnp.square(g)
    m_hat = m / (1.0 - ADAM_B1 ** ADAM_STEP)
    v_hat = v / (1.0 - ADAM_B2 ** ADAM_STEP)
    return -ADAM_LR * (m_hat / (jnp.sqrt(v_hat) + ADAM_EPS) + ADAM_WD * w), m, v


def _sum_slots(recv, name, comm=()):
    _, rows, cols = recv.shape
    tr = _row_tile(rows, cols)

    def body(r_ref, o_ref):
        acc = r_ref[0].astype(F32)
        for s in range(1, N_CHIPS):
            acc = acc + r_ref[s].astype(F32)
        o_ref[...] = acc

    return _pallas(body, [recv], name=name, grid=(rows // tr,),
                   in_specs=[pl.BlockSpec((N_CHIPS, tr, cols), lambda i: (0, i, 0))],
                   out_specs=[pl.BlockSpec((tr, cols), lambda i: (i, 0))],
                   out_shape=[jax.ShapeDtypeStruct((rows, cols), F32)], comm=comm)[0]


def _add_halves(mine, theirs, name):
    slots, rows, cols = mine.shape
    tr = _row_tile(rows, cols * slots)

    def body(a_ref, b_ref, o_ref):
        o_ref[...] = (a_ref[...].astype(F32) + b_ref[...].astype(F32)).astype(o_ref.dtype)

    spec = pl.BlockSpec((slots, tr, cols), lambda i: (0, i, 0))
    return _pallas(body, [mine, theirs], name=name, grid=(rows // tr,), in_specs=[spec, spec], out_specs=[spec],
                   out_shape=[jax.ShapeDtypeStruct(mine.shape, mine.dtype)])[0]


def _adamw_sharded(w, parts, m, v, name, comm=()):
    rows, cols = w.shape
    tr = _row_tile(rows, cols)
    n = len(parts)

    def body(w_ref, *refs):
        p_refs, (m_ref, v_ref, g_ref, d_ref, nm_ref, nv_ref) = refs[:n], refs[n:]
        g = p_refs[0][...]
        for p_ref in p_refs[1:]:
            g = g + p_ref[...]
        g_ref[...] = g
        d_ref[...], nm_ref[...], nv_ref[...] = _adamw(w_ref[...], g, m_ref[...], v_ref[...])

    spec = pl.BlockSpec((tr, cols), lambda i: (i, 0))
    shape = jax.ShapeDtypeStruct((rows, cols), F32)
    return _pallas(body, [w, *parts, m, v], name=name, grid=(rows // tr,), in_specs=[spec] * (n + 3),
                   out_specs=[spec] * 4, out_shape=[shape] * 4, comm=comm)


def _adamw_small(w, gathered, m, v):
    rows, cols = w.shape

    def body(w_ref, a_ref, m_ref, v_ref, g_ref, d_ref, nm_ref, nv_ref):
        g = a_ref[0:rows, :]
        for dev in range(1, N_DEV):
            g = g + a_ref[dev * rows:(dev + 1) * rows, :]
        g_ref[...] = g
        d_ref[...], nm_ref[...], nv_ref[...] = _adamw(w_ref[...], g, m_ref[...], v_ref[...])

    shape = jax.ShapeDtypeStruct((rows, cols), F32)
    return pl.pallas_call(
        body, name="adamw_small", out_shape=[shape] * 4,
        compiler_params=pltpu.CompilerParams(vmem_limit_bytes=VMEM_LIMIT))(w, gathered, m, v)


PACK_TILE = 8 * LANES


STATE_MINOR = ("s5_b_re", "s5_b_im")


def _pack_small(vals, last_row):
    rows = []
    for name in SMALL:
        v = jnp.swapaxes(vals[name], -1, -2) if name in STATE_MINOR else vals[name]
        flat = v.reshape(-1).astype(F32)
        rows.append(jnp.pad(flat, (0, -flat.size % PACK_TILE)).reshape(-1, LANES))
    rows.append(jnp.pad(last_row, ((0, 7), (0, 0))))
    return jnp.concatenate(rows, axis=0)


def _unpack_small(packed, like):
    out, r = {}, 0
    for name in SMALL:
        size, shape = like[name].size, like[name].shape
        flat = packed[r:r + -(-size // LANES)].reshape(-1)[:size]
        if name in STATE_MINOR:
            out[name] = jnp.swapaxes(flat.reshape(shape[:-2] + (shape[-1], shape[-2])), -1, -2)
        else:
            out[name] = flat.reshape(shape)
        r += 8 * -(-size // PACK_TILE)
    return out, packed[r, 0]


def _shard2d(name, v):
    v = v.reshape(v.shape[-2:])
    return v.T if name in FFN_T else v


def _unshard(name, v, shape):
    return (v.T if name in FFN_T else v).reshape(shape)


def _train_step(x3, tgt3, wts, ms, vs):
    x, tgt = x3[0], tgt3[0]
    L, D = x.shape
    row = lambda v: v.reshape(1, -1)
    shards = {k: _shard2d(k, wts[k]) for k in SHARDED}
    sends = {k: shards[k] if k == "conv_w_dw" else _cast_bf16(shards[k], "cast_" + k) for k in SHARDED}
    gat = {k: _Gather(sends[k]) for k in SHARDED}
    w = lambda k: gat[k].result[0]

    s = {k: wts[k] for k in SMALL}
    lr, li = s["s5_lam_re"].reshape(1, S5_N), s["s5_lam_im"].reshape(1, S5_N)
    ldt = jnp.repeat(s["s5_log_dt"].reshape(S5_G), S5_P).reshape(1, S5_N)
    brc, bic = _compact_b(s["s5_b_re"].reshape(S5_G, S5_P, S5_GC)), _compact_b(s["s5_b_im"].reshape(S5_G, S5_P, S5_GC))
    crc = _compact_b(s["s5_c_re"].reshape(S5_G, S5_GC, S5_P).transpose(0, 2, 1)).T
    cic = _compact_b(s["s5_c_im"].reshape(S5_G, S5_GC, S5_P).transpose(0, 2, 1)).T
    d_skip, b_glu = row(s["s5_d"]), row(s["s5_b_glu"])
    b_dw, ln_g, ln_b = row(s["conv_b_dw"]), row(s["conv_ln_g"]), row(s["conv_ln_b"])
    g1, gm, g2, gf = row(s["ffn1_norm"]), row(s["mix_norm"]), row(s["ffn2_norm"]), row(s["final_norm"])
    mavg = jnp.asarray(_HEAD_MEAN, dtype=BF16)

    h1 = _rms_fwd(x, g1, "rms1", comm=[gat["ffn1_w_gate"], gat["ffn1_w_up"]])
    a1, b1, act1 = _ffn_up(h1, w("ffn1_w_gate"), w("ffn1_w_up"), "ffn1_up", comm=[gat["ffn1_w_down"]])
    x1, h2 = _ffn_down(act1, w("ffn1_w_down"), x, gm, "ffn1_down",
                       comm=[gat["w_in"], gat["s5_w_glu"], gat["conv_w_dw"], gat["w_out"]])
    u = _mm_grouped(h2, w("w_in"), "in_proj")
    are, aim, bre, bim, cre, cim = _s5_params_fwd(lr, li, ldt, brc, bic, crc, cic)
    are_t, aim_t = are.reshape(S5_TILES, LANES), aim.reshape(S5_TILES, LANES)
    w_glu = w("s5_w_glu").reshape(S5_W, S5_W)
    *states, y_pre, o_s5 = _s5_fwd(u, are_t, aim_t, bre, bim, cre, cim, d_skip, w_glu, b_glu,
                                   comm=[gat["ffn2_w_gate"], gat["ffn2_w_up"]])
    w_dw = w("conv_w_dw").transpose(1, 0, 2).reshape(CONV_K, CONV_W)
    zc, cat = _conv_fwd(u, o_s5, w_dw, b_dw, ln_g, ln_b, mavg, comm=[gat["ffn2_w_down"]])
    w_out = w("w_out").reshape(-1, D)
    x2, h3 = _mix_out(cat, w_out, x1, g2, "mix_out")
    a2, b2, act2 = _ffn_up(h3, w("ffn2_w_gate"), w("ffn2_w_up"), "ffn2_up")
    dx3, dx3b, loss_part, d_gf = _ffn_down_loss(act2, w("ffn2_w_down"), x2, gf, tgt, "ffn2_down_loss")

    gs, sc, waiting = {"final_norm": d_gf}, {}, []

    def grad(key, g):
        if g.shape[1] % (2 * BF16_ROWS) == 0 and g.dtype == BF16:
            waiting.append(_SwapHalf(g, key))
        else:
            sc[key] = _Scatter(g)
            waiting.append(sc[key])

    def carry(call, *args, **kw):
        ops = list(waiting)
        waiting.clear()
        res = call(*args, comm=ops, **kw)
        for op in ops:
            if isinstance(op, _SwapHalf):
                sc[op.key] = _Scatter(_add_halves(*op.result, "add_" + op.key))
                waiting.append(sc[op.key])
        return res

    da2, db2 = _ffn_bwd_act(dx3b, w("ffn2_w_down"), a2, b2, "ffn2_bwd_act")
    grad("ffn2_w_down", _mm_tn(act2, dx3b[None], 0.5, "ffn2_dwd", N_CHIPS))
    grad("ffn2_w_gate", carry(_mm_tn, da2, h3[None], 1.0, "ffn2_dwg", N_CHIPS))
    grad("ffn2_w_up", carry(_mm_tn, db2, h3[None], 1.0, "ffn2_dwu", N_CHIPS))
    dx2, dx2b, gs["ffn2_norm"] = carry(_mm_rmsbwd, [da2, db2], [w("ffn2_w_gate"), w("ffn2_w_up")], False, False, x2, g2,
                                       dx3, "ffn2_bwd_dx")

    dm = _mm_nt(dx2b, w_out, "mix_bwd")
    grad("w_out", _mm_tn(cat[None], dx2b[None], 1.0, "dwout", 1).reshape(N_CHIPS, -1, D))
    (du_s5, d_wglu, gs["s5_b_glu"], gs["s5_d"], d_crc, d_cic, d_bre, d_bim, d_are, d_aim) = carry(
        _s5_bwd, dm, y_pre, u, states, are_t, aim_t, bre, bim, cre, cim, d_skip, w_glu, b_glu)
    grad("s5_w_glu", d_wglu.astype(BF16).reshape(N_CHIPS, -1, S5_W))
    g_lr, g_li, g_ldt, g_brc, g_bic = _s5_params_bwd(lr, li, ldt, brc, bic, d_are.reshape(1, S5_N),
                                                     d_aim.reshape(1, S5_N), d_bre, d_bim)
    gs["s5_lam_re"], gs["s5_lam_im"] = g_lr, g_li
    gs["s5_log_dt"] = jnp.sum(g_ldt.reshape(S5_G, S5_P), axis=1)
    gs["s5_b_re"], gs["s5_b_im"] = _uncompact_b(g_brc), _uncompact_b(g_bic)
    gs["s5_c_re"] = _uncompact_b(d_crc.T).transpose(0, 2, 1)
    gs["s5_c_im"] = _uncompact_b(d_cic.T).transpose(0, 2, 1)
    du, d_wdw, gs["conv_b_dw"], gs["conv_ln_g"], gs["conv_ln_b"] = carry(_conv_bwd, dm, zc, u, du_s5, w_dw, ln_g, ln_b, mavg)
    d_wdw = jnp.sum(d_wdw.reshape(HALO, 8, CONV_W), axis=1)[:CONV_K]
    grad("conv_w_dw", d_wdw.reshape(CONV_K, N_CHIPS, -1).transpose(1, 0, 2))
    grad("w_in", _mm_tn(h2[None], du, 1.0, "dwin", N_CHIPS, b_cols=True))
    dx1, dx1b, gs["mix_norm"] = carry(_mm_rmsbwd, [du], [w("w_in")], True, True, x1, gm, dx2, "in_proj_bwd")

    da1, db1 = carry(_ffn_bwd_act, dx1b, w("ffn1_w_down"), a1, b1, "ffn1_bwd_act")
    grad("ffn1_w_down", _mm_tn(act1, dx1b[None], 0.5, "ffn1_dwd", N_CHIPS))
    grad("ffn1_w_gate", carry(_mm_tn, da1, h1[None], 1.0, "ffn1_dwg", N_CHIPS))
    grad("ffn1_w_up", carry(_mm_tn, db1, h1[None], 1.0, "ffn1_dwu", N_CHIPS))
    grad_x, _, gs["ffn1_norm"] = carry(_mm_rmsbwd, [da1, db1], [w("ffn1_w_gate"), w("ffn1_w_up")], False, False, x, g1,
                                       dx1, "ffn1_bwd_dx")

    out = {}
    gsmall = {k: gs[k].reshape(wts[k].shape) for k in SMALL}
    zero_row = jnp.zeros((1, LANES), F32)
    g_all = carry(_gather_all, _pack_small(gsmall, loss_part))
    res = _adamw_small(_pack_small(s, zero_row), g_all, _pack_small({k: ms[k] for k in SMALL}, zero_row),
                       _pack_small({k: vs[k] for k in SMALL}, zero_row))
    unpacked = [_unpack_small(r, s) for r in res]
    loss = unpacked[0][1]
    for k in SMALL:
        out[k] = [u_[0][k] for u_ in unpacked]

    order = ("ffn2_w_down", "ffn2_w_gate", "ffn2_w_up", "w_out", "s5_w_glu", "conv_w_dw", "w_in", "ffn1_w_down",
             "ffn1_w_gate", "ffn1_w_up")
    back = {}
    for k in order:
        part = carry(_sum_slots, sc[k].result[0], "sum_" + k)
        back[k] = _SwapBack(part) if part.shape != shards[k].shape else _Swap(part)
        waiting.append(back[k])
    for k in order:
        parts = [back[k].result[0]] if isinstance(back[k], _SwapBack) else [back[k].ins[0], back[k].result[0]]
        res = carry(_adamw_sharded, shards[k], parts, _shard2d(k, ms[k]), _shard2d(k, vs[k]), "adamw_" + k)
        out[k] = [_unshard(k, r, wts[k].shape) for r in res]
    return loss, grad_x[None], out


def kernel(x, ffn1_norm, ffn1_w_gate, ffn1_w_up, ffn1_w_down, mix_norm, w_in, s5_lam_re, s5_lam_im, s5_log_dt, s5_b_re, s5_b_im, s5_c_re, s5_c_im, s5_d, s5_w_glu, s5_b_glu, conv_w_dw, conv_b_dw, conv_ln_g, conv_ln_b, w_out, ffn2_norm, ffn2_w_gate, ffn2_w_up, ffn2_w_down, final_norm, loss_target, m_ffn1_norm, m_ffn1_w_gate, m_ffn1_w_up, m_ffn1_w_down, m_mix_norm, m_w_in, m_s5_lam_re, m_s5_lam_im, m_s5_log_dt, m_s5_b_re, m_s5_b_im, m_s5_c_re, m_s5_c_im, m_s5_d, m_s5_w_glu, m_s5_b_glu, m_conv_w_dw, m_conv_b_dw, m_conv_ln_g, m_conv_ln_b, m_w_out, m_ffn2_norm, m_ffn2_w_gate, m_ffn2_w_up, m_ffn2_w_down, m_final_norm, v_ffn1_norm, v_ffn1_w_gate, v_ffn1_w_up, v_ffn1_w_down, v_mix_norm, v_w_in, v_s5_lam_re, v_s5_lam_im, v_s5_log_dt, v_s5_b_re, v_s5_b_im, v_s5_c_re, v_s5_c_im, v_s5_d, v_s5_w_glu, v_s5_b_glu, v_conv_w_dw, v_conv_b_dw, v_conv_ln_g, v_conv_ln_b, v_w_out, v_ffn2_norm, v_ffn2_w_gate, v_ffn2_w_up, v_ffn2_w_down, v_final_norm):
    given = dict(locals())
    wts = {k: given[k] for k in WEIGHTS}
    ms = {k: given["m_" + k] for k in WEIGHTS}
    vs = {k: given["v_" + k] for k in WEIGHTS}
    loss, grad_x, out = _train_step(x, loss_target, wts, ms, vs)
    return (loss, grad_x, *[out[k][0] for k in WEIGHTS], *[out[k][1] for k in WEIGHTS],
            *[out[k][2] for k in WEIGHTS], *[out[k][3] for k in WEIGHTS])
```

```python
import functools

import jax
import jax.numpy as jnp
import numpy as np
from jax import lax
from jax.experimental import pallas as pl
from jax.experimental.pallas import tpu as pltpu

F32, BF16 = jnp.float32, jnp.bfloat16
MESH = pl.DeviceIdType.MESH

EPS = 1e-6
ADAM_LR, ADAM_B1, ADAM_B2, ADAM_EPS, ADAM_WD, ADAM_STEP = 0.001, 0.9, 0.999, 1e-08, 0.01, 10

N_CHIPS = 4
N_DEV = 8
LANES = 128
BF16_ROWS = 16
S5_W, S5_G, S5_GC, S5_P = 512, 32, 16, 64
S5_N = S5_G * S5_P
S5_TILES = S5_N // LANES
S5_HALF = 8
CONV_W, CONV_K, CONV_HD = 512, 31, 64
HALO = 32
CONV_SB = 32
TM = 512
TMS = 1024
TK = 2048
TS = 256
VMEM_LIMIT = 48 << 20
GELU_C0, GELU_C1 = 0.7978845608028654, 0.044715

FFN_T = ("ffn1_w_gate", "ffn1_w_up", "ffn2_w_gate", "ffn2_w_up")
SHARDED = ("ffn1_w_gate", "ffn1_w_up", "ffn1_w_down", "w_in", "s5_w_glu", "conv_w_dw", "w_out",
           "ffn2_w_gate", "ffn2_w_up", "ffn2_w_down")
SMALL = ("ffn1_norm", "mix_norm", "s5_lam_re", "s5_lam_im", "s5_log_dt", "s5_b_re", "s5_b_im", "s5_c_re",
         "s5_c_im", "s5_d", "s5_b_glu", "conv_b_dw", "conv_ln_g", "conv_ln_b", "ffn2_norm", "final_norm")
WEIGHTS = ("ffn1_norm", "ffn1_w_gate", "ffn1_w_up", "ffn1_w_down", "mix_norm", "w_in", "s5_lam_re", "s5_lam_im",
           "s5_log_dt", "s5_b_re", "s5_b_im", "s5_c_re", "s5_c_im", "s5_d", "s5_w_glu", "s5_b_glu", "conv_w_dw",
           "conv_b_dw", "conv_ln_g", "conv_ln_b", "w_out", "ffn2_norm", "ffn2_w_gate", "ffn2_w_up", "ffn2_w_down",
           "final_norm")


def _dot(a, b):
    return jnp.dot(a, b, preferred_element_type=F32)


def _dot_nt(a, b):
    return lax.dot_general(a, b, (((1,), (1,)), ((), ())), preferred_element_type=F32)


def _dot_tn(a, b):
    return lax.dot_general(a, b, (((0,), (0,)), ((), ())), preferred_element_type=F32)


def _colsum(v):
    return jnp.sum(v, axis=0, keepdims=True)


def _sigmoid(v):
    return 1.0 / (1.0 + jnp.exp(-v))


def _accumulate(ref, first, value):
    @pl.when(first)
    def _():
        ref[...] = value

    @pl.when(jnp.logical_not(first))
    def _():
        ref[...] += value


def _position():
    x, y, c = lax.axis_index("x"), lax.axis_index("y"), lax.axis_index("c")
    return x, y, c, [(1 - x, y), (x, 1 - y), (1 - x, 1 - y)]


def _remote(src, dst, sems, send, recv, device):
    return pltpu.make_async_remote_copy(src_ref=src, dst_ref=dst, send_sem=sems.at[send], recv_sem=sems.at[recv],
                                        device_id=device, device_id_type=MESH)


class _Gather:
    def __init__(self, shard):
        self.ins = [shard]
        self.outs = [jax.ShapeDtypeStruct((N_CHIPS,) + shard.shape, shard.dtype)]
        self.rows = shard.shape[0]
        self.halve = shard.dtype == BF16 and self.rows % (2 * BF16_ROWS) == 0
        self.n_sem = 13 if self.halve else 7
        self.result = None

    def _copies(self, ins, outs, sems, s0, pos):
        x, y, c, chips = pos
        src, dst = ins[0], outs[0]
        me = 2 * x + y
        if self.halve:
            hr = self.rows // 2
            mine, theirs = pl.ds(c * hr, hr), pl.ds((1 - c) * hr, hr)
            part = lambda slot, rows: dst.at[slot, rows]
            my_src = src.at[mine]
        else:
            mine = theirs = None
            part = lambda slot, rows: dst.at[slot]
            my_src = src
        slot = lambda j: 2 * chips[j][0] + chips[j][1]
        local = lambda: pltpu.make_async_copy(src, dst.at[me], sems.at[s0])
        send = lambda j: _remote(my_src, part(me, mine), sems, s0 + 1 + j, s0 + 4 + j, (*chips[j], c))
        land = lambda j: _remote(my_src, part(slot(j), mine), sems, s0 + 1 + j, s0 + 4 + j, (*chips[j], c))
        fwd = lambda j: _remote(part(slot(j), mine), part(slot(j), mine), sems, s0 + 7 + j, s0 + 10 + j, (x, y, 1 - c))
        got = lambda j: _remote(part(slot(j), theirs), part(slot(j), theirs), sems, s0 + 7 + j, s0 + 10 + j,
                                (x, y, 1 - c))
        return local, send, land, fwd, got

    def start(self, ins, outs, sems, s0, pos):
        local, send, _, _, _ = self._copies(ins, outs, sems, s0, pos)
        local().start()
        for j in range(N_CHIPS - 1):
            send(j).start()

    def finish(self, ins, outs, sems, s0, pos):
        local, send, land, fwd, got = self._copies(ins, outs, sems, s0, pos)
        others = range(N_CHIPS - 1)
        for j in others:
            land(j).wait_recv()
            if self.halve:
                fwd(j).start()
        for j in others:
            if self.halve:
                got(j).wait_recv()
        for j in others:
            send(j).wait_send()
            if self.halve:
                fwd(j).wait_send()
        local().wait()


class _Scatter:
    def __init__(self, grad):
        self.ins = [grad]
        self.outs = [jax.ShapeDtypeStruct(grad.shape, grad.dtype)]
        self.n_sem = 7
        self.result = None

    def _copies(self, ins, outs, sems, s0, pos):
        x, y, c, chips = pos
        src, dst = ins[0], outs[0]
        me = 2 * x + y
        slot = lambda j: 2 * chips[j][0] + chips[j][1]
        local = lambda: pltpu.make_async_copy(src.at[me], dst.at[me], sems.at[s0])
        send = lambda j: _remote(src.at[slot(j)], dst.at[me], sems, s0 + 1 + j, s0 + 4 + j, (*chips[j], c))
        land = lambda j: _remote(src.at[me], dst.at[slot(j)], sems, s0 + 1 + j, s0 + 4 + j, (*chips[j], c))
        return local, send, land

    def start(self, ins, outs, sems, s0, pos):
        local, send, _ = self._copies(ins, outs, sems, s0, pos)
        local().start()
        for j in range(N_CHIPS - 1):
            send(j).start()

    def finish(self, ins, outs, sems, s0, pos):
        local, send, land = self._copies(ins, outs, sems, s0, pos)
        for j in range(N_CHIPS - 1):
            land(j).wait_recv()
        for j in range(N_CHIPS - 1):
            send(j).wait_send()
        local().wait()


class _Swap:
    def __init__(self, part):
        self.ins = [part]
        self.outs = [jax.ShapeDtypeStruct(part.shape, part.dtype)]
        self.n_sem = 2
        self.result = None

    def _copy(self, ins, outs, sems, s0, pos):
        x, y, c, _ = pos
        return _remote(ins[0], outs[0], sems, s0, s0 + 1, (x, y, 1 - c))

    def start(self, ins, outs, sems, s0, pos):
        self._copy(ins, outs, sems, s0, pos).start()

    def finish(self, ins, outs, sems, s0, pos):
        self._copy(ins, outs, sems, s0, pos).wait()


class _SwapHalf:
    def __init__(self, grad, key):
        slots, rows, cols = grad.shape
        half = jax.ShapeDtypeStruct((slots, rows // 2, cols), grad.dtype)
        self.ins, self.outs, self.key = [grad], [half, half], key
        self.hr = rows // 2
        self.n_sem = 3
        self.result = None

    def _copies(self, ins, outs, sems, s0, pos):
        x, y, c, _ = pos
        mine, theirs = pl.ds(c * self.hr, self.hr), pl.ds((1 - c) * self.hr, self.hr)
        local = pltpu.make_async_copy(ins[0].at[:, mine], outs[0], sems.at[s0])
        remote = _remote(ins[0].at[:, theirs], outs[1], sems, s0 + 1, s0 + 2, (x, y, 1 - c))
        return local, remote

    def start(self, ins, outs, sems, s0, pos):
        for cp in self._copies(ins, outs, sems, s0, pos):
            cp.start()

    def finish(self, ins, outs, sems, s0, pos):
        for cp in self._copies(ins, outs, sems, s0, pos):
            cp.wait()


class _SwapBack:
    def __init__(self, part):
        hr, cols = part.shape
        self.ins, self.outs = [part], [jax.ShapeDtypeStruct((2 * hr, cols), part.dtype)]
        self.hr = hr
        self.n_sem = 3
        self.result = None

    def _copies(self, ins, outs, sems, s0, pos):
        x, y, c, _ = pos
        mine, theirs = pl.ds(c * self.hr, self.hr), pl.ds((1 - c) * self.hr, self.hr)
        local = lambda: pltpu.make_async_copy(ins[0], outs[0].at[mine], sems.at[s0])
        send = lambda: _remote(ins[0], outs[0].at[mine], sems, s0 + 1, s0 + 2, (x, y, 1 - c))
        land = lambda: _remote(ins[0], outs[0].at[theirs], sems, s0 + 1, s0 + 2, (x, y, 1 - c))
        return local, send, land

    def start(self, ins, outs, sems, s0, pos):
        local, send, _ = self._copies(ins, outs, sems, s0, pos)
        local().start()
        send().start()

    def finish(self, ins, outs, sems, s0, pos):
        local, send, land = self._copies(ins, outs, sems, s0, pos)
        land().wait_recv()
        send().wait_send()
        local().wait()


def _pallas(body, args, *, name, grid, in_specs, out_specs, out_shape, scratch_shapes=(), comm=()):
    comm = list(comm)
    n_in, n_out, n_scr = len(in_specs), len(out_specs), len(scratch_shapes)
    c_in = [a for op in comm for a in op.ins]
    c_out = [s for op in comm for s in op.outs]
    n_sem = sum(op.n_sem for op in comm)

    def full(*refs):
        o0 = n_in + len(c_in)
        s0 = o0 + n_out + len(c_out)
        ins, cin = refs[:n_in], refs[n_in:o0]
        outs, cout = refs[o0:o0 + n_out], refs[o0 + n_out:s0]
        scratch = refs[s0:s0 + n_scr]
        if comm:
            sems = refs[s0 + n_scr]
            ids = [pl.program_id(d) for d in range(len(grid))]
            first = functools.reduce(jnp.logical_and, [i == 0 for i in ids])
            last = functools.reduce(jnp.logical_and, [i == g - 1 for i, g in zip(ids, grid)])
            pos = _position()

            def each(step):
                ci = co = cs = 0
                for op in comm:
                    getattr(op, step)(cin[ci:ci + len(op.ins)], cout[co:co + len(op.outs)], sems, cs, pos)
                    ci, co, cs = ci + len(op.ins), co + len(op.outs), cs + op.n_sem

            @pl.when(first)
            def _():
                each("start")

        body(*ins, *outs, *scratch)
        if comm:
            @pl.when(last)
            def _():
                each("finish")

    hbm = pl.BlockSpec(memory_space=pl.ANY)
    res = pl.pallas_call(
        full, name=name, grid=grid,
        in_specs=list(in_specs) + [hbm] * len(c_in), out_specs=list(out_specs) + [hbm] * len(c_out),
        out_shape=list(out_shape) + c_out,
        scratch_shapes=list(scratch_shapes) + ([pltpu.SemaphoreType.DMA((n_sem,))] if comm else []),
        compiler_params=pltpu.CompilerParams(dimension_semantics=("arbitrary",) * len(grid),
                                             vmem_limit_bytes=VMEM_LIMIT))(*args, *c_in)
    k = n_out
    for op in comm:
        op.result = list(res[k:k + len(op.outs)])
        k += len(op.outs)
    return list(res[:n_out])


def _row_tile(rows, cols, itemsize=4, budget=1 << 20):
    t = rows
    while t % (2 * BF16_ROWS) == 0 and t * cols * itemsize > budget:
        t //= 2
    return t


def _cast_bf16(w, name):
    rows, cols = w.shape
    tr = _row_tile(rows, cols)

    def body(w_ref, o_ref):
        o_ref[...] = w_ref[...].astype(BF16)

    spec = pl.BlockSpec((tr, cols), lambda i: (i, 0))
    return _pallas(body, [w], name=name, grid=(rows // tr,), in_specs=[spec], out_specs=[spec],
                   out_shape=[jax.ShapeDtypeStruct((rows, cols), BF16)])[0]


def _rms_fwd(x, g, name, comm=()):
    L, D = x.shape

    def body(x_ref, g_ref, h_ref):
        xf = x_ref[...]
        r = lax.rsqrt(jnp.mean(xf * xf, axis=-1, keepdims=True) + EPS)
        h_ref[...] = (xf * r * g_ref[...]).astype(BF16)

    row = pl.BlockSpec((TMS, D), lambda i: (i, 0))
    return _pallas(body, [x, g], name=name, grid=(L // TMS,),
                   in_specs=[row, pl.BlockSpec((1, D), lambda i: (0, 0))], out_specs=[row],
                   out_shape=[jax.ShapeDtypeStruct((L, D), BF16)], comm=comm)[0]


def _resident(shape):
    return pl.BlockSpec(shape, lambda *_: (0,) * len(shape), pipeline_mode=pl.Buffered(1))


def _ffn_up(h, wg_t, wu_t, name, comm=()):
    L, D = h.shape
    G, FS, _ = wg_t.shape

    def body(h_ref, wg_ref, wu_ref, a_ref, b_ref, act_ref):
        j = pl.program_id(1)
        hv = h_ref[...]
        a = _dot_nt(hv, wg_ref[j])
        b = _dot_nt(hv, wu_ref[j])
        a_ref[...] = a.astype(BF16)
        b_ref[...] = b.astype(BF16)
        act_ref[...] = (a * _sigmoid(a) * b).astype(BF16)

    ospec = pl.BlockSpec((None, TMS, FS), lambda i, j: (j, i, 0))
    oshape = jax.ShapeDtypeStruct((G, L, FS), BF16)
    return _pallas(body, [h, wg_t, wu_t], name=name, grid=(L // TMS, G),
                   in_specs=[pl.BlockSpec((TMS, D), lambda i, j: (i, 0)), _resident((G, FS, D)), _resident((G, FS, D))],
                   out_specs=[ospec, ospec, ospec], out_shape=[oshape, oshape, oshape], comm=comm)


def _group_sum(a_ref, w_ref, groups, mm=_dot):
    acc = mm(a_ref[0], w_ref[0])
    for j in range(1, groups):
        acc = acc + mm(a_ref[j], w_ref[j])
    return acc


def _ffn_down(act, wd, x, g_next, name, comm=()):
    G, L, FS = act.shape
    D = wd.shape[2]

    def body(act_ref, wd_ref, x_ref, g_ref, xn_ref, hn_ref):
        xn = x_ref[...] + 0.5 * _group_sum(act_ref, wd_ref, G)
        xn_ref[...] = xn
        r = lax.rsqrt(jnp.mean(xn * xn, axis=-1, keepdims=True) + EPS)
        hn_ref[...] = (xn * r * g_ref[...]).astype(BF16)

    row = pl.BlockSpec((TM, D), lambda i: (i, 0))
    return _pallas(body, [act, wd, x, g_next], name=name, grid=(L // TM,),
                   in_specs=[pl.BlockSpec((G, TM, FS), lambda i: (0, i, 0)), _resident((G, FS, D)), row,
                             pl.BlockSpec((1, D), lambda i: (0, 0))],
                   out_specs=[row, row],
                   out_shape=[jax.ShapeDtypeStruct((L, D), F32), jax.ShapeDtypeStruct((L, D), BF16)], comm=comm)


def _ffn_down_loss(act, wd, x, gf, tgt, name):
    G, L, FS = act.shape
    D = wd.shape[2]

    def body(act_ref, wd_ref, x_ref, g_ref, t_ref, dx_ref, dxb_ref, loss_ref, dg_ref):
        i = pl.program_id(0)
        xn = x_ref[...] + 0.5 * _group_sum(act_ref, wd_ref, G)
        r = lax.rsqrt(jnp.mean(xn * xn, axis=-1, keepdims=True) + EPS)
        xh = xn * r
        gv = g_ref[...]
        e = xh * gv - t_ref[...]
        part = 0.5 * jnp.sum(_colsum(e * e), axis=1, keepdims=True) / D
        dy = e / D
        _accumulate(loss_ref, i == 0, jnp.broadcast_to(part, (1, LANES)))
        _accumulate(dg_ref, i == 0, _colsum(dy * xh))
        dxh = dy * gv
        dx = r * (dxh - xh * jnp.mean(dxh * xh, axis=-1, keepdims=True))
        dx_ref[...] = dx
        dxb_ref[...] = dx.astype(BF16)

    row = pl.BlockSpec((TM, D), lambda i: (i, 0))
    return _pallas(body, [act, wd, x, gf, tgt], name=name, grid=(L // TM,),
                   in_specs=[pl.BlockSpec((G, TM, FS), lambda i: (0, i, 0)), _resident((G, FS, D)), row,
                             pl.BlockSpec((1, D), lambda i: (0, 0)), row],
                   out_specs=[row, row, pl.BlockSpec((1, LANES), lambda i: (0, 0)),
                              pl.BlockSpec((1, D), lambda i: (0, 0))],
                   out_shape=[jax.ShapeDtypeStruct((L, D), F32), jax.ShapeDtypeStruct((L, D), BF16),
                              jax.ShapeDtypeStruct((1, LANES), F32), jax.ShapeDtypeStruct((1, D), F32)])


def _ffn_bwd_act(dxb, wd, a, b, name, comm=()):
    L, D = dxb.shape
    G, FS, _ = wd.shape

    def body(dx_ref, wd_ref, a_ref, b_ref, da_ref, db_ref):
        dact = 0.5 * _dot_nt(dx_ref[...], wd_ref[pl.program_id(1)])
        av = a_ref[...].astype(F32)
        bv = b_ref[...].astype(F32)
        sg = _sigmoid(av)
        da_ref[...] = (dact * bv * sg * (1.0 + av * (1.0 - sg))).astype(BF16)
        db_ref[...] = (dact * av * sg).astype(BF16)

    gspec = pl.BlockSpec((None, TMS, FS), lambda i, j: (j, i, 0))
    oshape = jax.ShapeDtypeStruct((G, L, FS), BF16)
    return _pallas(body, [dxb, wd, a, b], name=name, grid=(L // TMS, G),
                   in_specs=[pl.BlockSpec((TMS, D), lambda i, j: (i, 0)), _resident((G, FS, D)), gspec, gspec],
                   out_specs=[gspec, gspec], out_shape=[oshape, oshape], comm=comm)


def _mm_grouped(a, w, name):
    L, K = a.shape
    G, _, N = w.shape

    def body(a_ref, w_ref, o_ref):
        o_ref[...] = _dot(a_ref[...], w_ref[pl.program_id(1)])

    return _pallas(body, [a, w], name=name, grid=(L // TMS, G),
                   in_specs=[pl.BlockSpec((TMS, K), lambda i, g: (i, 0)), _resident((G, K, N))],
                   out_specs=[pl.BlockSpec((TMS, N), lambda i, g: (i, g))],
                   out_shape=[jax.ShapeDtypeStruct((L, G * N), F32)])[0]


def _mm_nt(a, w, name):
    L, K = a.shape
    N = w.shape[0]

    def body(a_ref, w_ref, o_ref):
        o_ref[...] = _dot_nt(a_ref[...], w_ref[...]).astype(BF16)

    return _pallas(body, [a, w], name=name, grid=(L // TMS,),
                   in_specs=[pl.BlockSpec((TMS, K), lambda i: (i, 0)), _resident((N, K))],
                   out_specs=[pl.BlockSpec((TMS, N), lambda i: (i, 0))],
                   out_shape=[jax.ShapeDtypeStruct((L, N), BF16)])[0]


def _mm_tn(a, b, scale, name, groups, b_cols=False, comm=()):
    L, M = a.shape[1], a.shape[2]
    N = b.shape[1] // groups if b_cols else b.shape[2]
    tk = min(L, TK)
    nk = L // tk

    def spec(v, cols):
        if cols:
            return pl.BlockSpec((tk, v.shape[1] // groups), lambda g, k: (k, g))
        if v.shape[0] > 1:
            return pl.BlockSpec((None, tk, v.shape[2]), lambda g, k: (g, k, 0))
        return pl.BlockSpec((None, tk, v.shape[2]), lambda g, k: (0, k, 0))

    def body(a_ref, b_ref, o_ref, acc):
        k = pl.program_id(1)
        p = _dot_tn(a_ref[...], b_ref[...])
        if nk == 1:
            o_ref[...] = (p * scale).astype(BF16)
        else:
            _accumulate(acc, k == 0, p)

            @pl.when(k == nk - 1)
            def _():
                o_ref[...] = (acc[...] * scale).astype(BF16)

    return _pallas(body, [a, b], name=name, grid=(groups, nk),
                   in_specs=[spec(a, False), spec(b, b_cols)],
                   out_specs=[pl.BlockSpec((None, M, N), lambda g, k: (g, 0, 0))],
                   out_shape=[jax.ShapeDtypeStruct((groups, M, N), BF16)],
                   scratch_shapes=[pltpu.VMEM((M, N), F32)], comm=comm)[0]


def _mm_tn_pair(a1, a2, b, name, comm=()):
    G, L, M = a1.shape
    N = b.shape[1]
    tk = min(L, TK)
    nk = L // tk

    def body(a1_ref, a2_ref, b_ref, o1_ref, o2_ref, acc1, acc2):
        k = pl.program_id(1)
        bv = b_ref[...]
        _accumulate(acc1, k == 0, _dot_tn(a1_ref[...], bv))
        _accumulate(acc2, k == 0, _dot_tn(a2_ref[...], bv))

        @pl.when(k == nk - 1)
        def _():
            o1_ref[...] = acc1[...].astype(BF16)
            o2_ref[...] = acc2[...].astype(BF16)

    aspec = pl.BlockSpec((None, tk, M), lambda g, k: (g, k, 0))
    ospec = pl.BlockSpec((None, M, N), lambda g, k: (g, 0, 0))
    oshape = jax.ShapeDtypeStruct((G, M, N), BF16)
    return _pallas(body, [a1, a2, b], name=name, grid=(G, nk),
                   in_specs=[aspec, aspec, pl.BlockSpec((tk, N), lambda g, k: (k, 0))],
                   out_specs=[ospec, ospec], out_shape=[oshape, oshape],
                   scratch_shapes=[pltpu.VMEM((M, N), F32), pltpu.VMEM((M, N), F32)], comm=comm)


def _mm_rmsbwd(a_list, w_list, nt, a_cols, x_in, g, dx_out, name, comm=()):
    P = len(a_list)
    G = w_list[0].shape[0]
    L, D = x_in.shape
    mm = _dot_nt if nt else _dot

    def body(*refs):
        a_refs, w_refs = refs[:P], refs[P:2 * P]
        x_ref, g_ref, dxo_ref, dx_ref, dxb_ref, dg_ref = refs[2 * P:]
        i = pl.program_id(0)
        dh = None
        for a_ref, w_ref in zip(a_refs, w_refs):
            for j in range(G):
                if a_cols:
                    kw = a_ref.shape[1] // G
                    term = mm(a_ref[:, j * kw:(j + 1) * kw], w_ref[j])
                else:
                    term = mm(a_ref[j], w_ref[j])
                dh = term if dh is None else dh + term
        xf = x_ref[...]
        r = lax.rsqrt(jnp.mean(xf * xf, axis=-1, keepdims=True) + EPS)
        xh = xf * r
        _accumulate(dg_ref, i == 0, _colsum(dh * xh))
        dxh = dh * g_ref[...]
        dx = dxo_ref[...] + r * (dxh - xh * jnp.mean(dxh * xh, axis=-1, keepdims=True))
        dx_ref[...] = dx
        dxb_ref[...] = dx.astype(BF16)

    row = pl.BlockSpec((TM, D), lambda i: (i, 0))
    vec = pl.BlockSpec((1, D), lambda i: (0, 0))
    if a_cols:
        a_specs = [pl.BlockSpec((TM, a.shape[1]), lambda i: (i, 0)) for a in a_list]
    else:
        a_specs = [pl.BlockSpec((G, TM, a.shape[2]), lambda i: (0, i, 0)) for a in a_list]
    w_specs = [_resident(w.shape) for w in w_list]
    return _pallas(body, [*a_list, *w_list, x_in, g, dx_out], name=name, grid=(L // TM,),
                   in_specs=a_specs + w_specs + [row, vec, row], out_specs=[row, row, vec],
                   out_shape=[jax.ShapeDtypeStruct((L, D), F32), jax.ShapeDtypeStruct((L, D), BF16),
                              jax.ShapeDtypeStruct((1, D), F32)], comm=comm)


def _mix_out(cat, wout, x1, g_next, name):
    L, K = cat.shape
    D = wout.shape[1]

    def body(c_ref, w_ref, x_ref, g_ref, xn_ref, hn_ref):
        xn = x_ref[...] + _dot(c_ref[...], w_ref[...])
        xn_ref[...] = xn
        r = lax.rsqrt(jnp.mean(xn * xn, axis=-1, keepdims=True) + EPS)
        hn_ref[...] = (xn * r * g_ref[...]).astype(BF16)

    row = pl.BlockSpec((TMS, D), lambda i: (i, 0))
    return _pallas(body, [cat, wout, x1, g_next], name=name, grid=(L // TMS,),
                   in_specs=[pl.BlockSpec((TMS, K), lambda i: (i, 0)), pl.BlockSpec((K, D), lambda i: (0, 0)), row,
                             pl.BlockSpec((1, D), lambda i: (0, 0))],
                   out_specs=[row, row],
                   out_shape=[jax.ShapeDtypeStruct((L, D), F32), jax.ShapeDtypeStruct((L, D), BF16)])


def _s5_disc(lr, li, ldt, brc, bic):
    dt = jnp.exp(ldt)
    mag = jnp.exp(lr * dt)
    are = mag * jnp.cos(li * dt)
    aim = mag * jnp.sin(li * dt)
    den = lr * lr + li * li
    nre = are - 1.0
    fre = (nre * lr + aim * li) / den
    fim = (aim * lr - nre * li) / den
    return are, aim, fre * brc - fim * bic, fre * bic + fim * brc


def _s5_params_fwd(lr, li, ldt, brc, bic, crc, cic):
    def body(lr_ref, li_ref, ldt_ref, br_ref, bi_ref, cr_ref, ci_ref, are_ref, aim_ref, bre_ref, bim_ref, cre_ref, cim_ref):
        are, aim, bre, bim = _s5_disc(lr_ref[...], li_ref[...], ldt_ref[...], br_ref[...], bi_ref[...])
        are_ref[...] = are
        aim_ref[...] = aim
        bre_ref[...] = bre.astype(BF16)
        bim_ref[...] = bim.astype(BF16)
        cre_ref[...] = cr_ref[...].astype(BF16)
        cim_ref[...] = ci_ref[...].astype(BF16)

    vec = jax.ShapeDtypeStruct((1, S5_N), F32)
    return pl.pallas_call(
        body, name="s5_params_fwd",
        out_shape=[vec, vec, jax.ShapeDtypeStruct((LANES, S5_N), BF16), jax.ShapeDtypeStruct((LANES, S5_N), BF16),
                   jax.ShapeDtypeStruct((S5_N, LANES), BF16), jax.ShapeDtypeStruct((S5_N, LANES), BF16)],
        compiler_params=pltpu.CompilerParams(vmem_limit_bytes=VMEM_LIMIT))(lr, li, ldt, brc, bic, crc, cic)


def _s5_params_bwd(lr, li, ldt, brc, bic, dare, daim, dbre, dbim):
    def body(lr_ref, li_ref, ldt_ref, br_ref, bi_ref, dare_ref, daim_ref, dbre_ref, dbim_ref,
             glr_ref, gli_ref, gldt_ref, gbr_ref, gbi_ref):
        _, vjp = jax.vjp(_s5_disc, lr_ref[...], li_ref[...], ldt_ref[...], br_ref[...], bi_ref[...])
        glr, gli, gldt, gbr, gbi = vjp((dare_ref[...], daim_ref[...], dbre_ref[...], dbim_ref[...]))
        glr_ref[...] = glr
        gli_ref[...] = gli
        gldt_ref[...] = gldt
        gbr_ref[...] = gbr
        gbi_ref[...] = gbi

    vec = jax.ShapeDtypeStruct((1, S5_N), F32)
    mat = jax.ShapeDtypeStruct((LANES, S5_N), F32)
    return pl.pallas_call(
        body, name="s5_params_bwd", out_shape=[vec, vec, vec, mat, mat],
        compiler_params=pltpu.CompilerParams(vmem_limit_bytes=VMEM_LIMIT))(lr, li, ldt, brc, bic, dare, daim, dbre, dbim)


def _gelu_parts(y):
    th = jnp.tanh(GELU_C0 * (y + GELU_C1 * y * y * y))
    return 0.5 * y * (1.0 + th), th


def _state_rows(q, T):
    return pl.ds(q % S5_HALF, T, stride=S5_HALF)


def _load_tiles(bufs, ct, T, dtype):
    return jnp.concatenate([bufs[q // S5_HALF][_state_rows(q, T), :].astype(dtype) for q in range(4 * ct, 4 * ct + 4)],
                           axis=1)


def _store_tiles(bufs, ct, T, value):
    for k, q in enumerate(range(4 * ct, 4 * ct + 4)):
        bufs[q // S5_HALF][_state_rows(q, T), :] = value[:, k * LANES:(k + 1) * LANES]


def _s5_fwd(u, are, aim, bre, bim, cre, cim, d_skip, wglu, bglu, comm=()):
    L = u.shape[0]
    T = min(TS, L)
    n = L // T

    def body(u_ref, are_ref, aim_ref, bre_ref, bim_ref, cre_ref, cim_ref, d_ref, wg_ref, bg_ref,
             sre_lo, sre_hi, sim_lo, sim_hi, y_ref, o_ref, st_re, st_im):
        i = pl.program_id(0)
        sre, sim = (sre_lo, sre_hi), (sim_lo, sim_hi)

        @pl.when(i == 0)
        def _():
            st_re[...] = jnp.zeros_like(st_re)
            st_im[...] = jnp.zeros_like(st_im)

        uf = u_ref[...]
        ub = uf.astype(BF16)
        for ct in range(4):
            uq = ub[:, ct * LANES:(ct + 1) * LANES]
            win = slice(4 * ct * LANES, 4 * (ct + 1) * LANES)
            _store_tiles(sre, ct, T, _dot(uq, bre_ref[:, win]))
            _store_tiles(sim, ct, T, _dot(uq, bim_ref[:, win]))
        halves = [slice(h * S5_HALF, (h + 1) * S5_HALF) for h in range(2)]
        a_re = [are_ref[hs, :] for hs in halves]
        a_im = [aim_ref[hs, :] for hs in halves]

        def step(t, carry):
            rows = pl.ds(pl.multiple_of(t * S5_HALF, S5_HALF), S5_HALF)
            out = []
            for h in range(2):
                s_re, s_im = carry[2 * h], carry[2 * h + 1]
                n_re = a_re[h] * s_re - a_im[h] * s_im + sre[h][rows, :]
                n_im = a_re[h] * s_im + a_im[h] * s_re + sim[h][rows, :]
                sre[h][rows, :] = n_re
                sim[h][rows, :] = n_im
                out += [n_re, n_im]
            return tuple(out)

        init = (st_re[halves[0], :], st_im[halves[0], :], st_re[halves[1], :], st_im[halves[1], :])
        fin = lax.fori_loop(0, T, step, init, unroll=4)
        for h in range(2):
            st_re[halves[h], :] = fin[2 * h]
            st_im[halves[h], :] = fin[2 * h + 1]
        tiles = []
        for ct in range(4):
            win = slice(4 * ct * LANES, 4 * (ct + 1) * LANES)
            tiles.append(_dot(_load_tiles(sre, ct, T, BF16), cre_ref[win, :])
                         - _dot(_load_tiles(sim, ct, T, BF16), cim_ref[win, :]))
        y = jnp.concatenate(tiles, axis=1) + d_ref[...] * uf
        y_ref[...] = y
        yg, _ = _gelu_parts(y)
        gate = _sigmoid(_dot(yg.astype(BF16), wg_ref[...]) + bg_ref[...])
        o_ref[...] = (yg * gate).astype(BF16)

    const = lambda shape: pl.BlockSpec(shape, lambda i: (0, 0))
    sspec = pl.BlockSpec((T * S5_HALF, LANES), lambda i: (i, 0))
    sshape = jax.ShapeDtypeStruct((L * S5_HALF, LANES), F32)
    chunk = pl.BlockSpec((T, S5_W), lambda i: (i, 0))
    return _pallas(body, [u, are, aim, bre, bim, cre, cim, d_skip, wglu, bglu], name="s5_fwd", grid=(n,),
                   in_specs=[chunk, const((S5_TILES, LANES)), const((S5_TILES, LANES)),
                             const((LANES, S5_N)), const((LANES, S5_N)), const((S5_N, LANES)), const((S5_N, LANES)),
                             const((1, S5_W)), const((S5_W, S5_W)), const((1, S5_W))],
                   out_specs=[sspec] * 4 + [chunk, chunk],
                   out_shape=[sshape] * 4 + [jax.ShapeDtypeStruct((L, S5_W), F32), jax.ShapeDtypeStruct((L, S5_W), BF16)],
                   scratch_shapes=[pltpu.VMEM((S5_TILES, LANES), F32), pltpu.VMEM((S5_TILES, LANES), F32)], comm=comm)


def _s5_bwd(dm, y_pre, u, states, are, aim, bre, bim, cre, cim, d_skip, wglu, bglu, comm=()):
    L = u.shape[0]
    T = min(TS, L)
    n = L // T

    def body(dm_ref, y_ref, u_ref, sre_lo, sre_hi, sim_lo, sim_hi, pre_lo, pre_hi, pim_lo, pim_hi,
             are_ref, aim_ref, bre_ref, bim_ref, cre_ref, cim_ref, d_ref, wg_ref, bg_ref,
             du_ref, dwg_ref, dbg_ref, dd_ref, dcre_ref, dcim_ref, dbre_ref, dbim_ref, dare_ref, daim_ref,
             gre_lo, gre_hi, gim_lo, gim_hi, car_re, car_im):
        i = pl.program_id(0)
        first = i == 0
        sre, sim = (sre_lo, sre_hi), (sim_lo, sim_hi)
        gre, gim = (gre_lo, gre_hi), (gim_lo, gim_hi)
        pre, pim = (pre_lo, pre_hi), (pim_lo, pim_hi)

        @pl.when(first)
        def _():
            car_re[...] = jnp.zeros_like(car_re)
            car_im[...] = jnp.zeros_like(car_im)
            dcre_ref[...] = jnp.zeros_like(dcre_ref)
            dcim_ref[...] = jnp.zeros_like(dcim_ref)
            dbre_ref[...] = jnp.zeros_like(dbre_ref)
            dbim_ref[...] = jnp.zeros_like(dbim_ref)

        y = y_ref[...]
        uf = u_ref[...]
        yg, th = _gelu_parts(y)
        dgelu = 0.5 * (1.0 + th) + 0.5 * y * (1.0 - th * th) * GELU_C0 * (1.0 + 3.0 * GELU_C1 * y * y)
        ygb = yg.astype(BF16)
        sg = _sigmoid(_dot(ygb, wg_ref[...]) + bg_ref[...])
        dout = dm_ref[...].astype(F32)
        dgp = dout * yg * sg * (1.0 - sg)
        dgpb = dgp.astype(BF16)
        dyg = dout * sg + _dot_nt(dgpb, wg_ref[...])
        _accumulate(dwg_ref, first, _dot_tn(ygb, dgpb))
        _accumulate(dbg_ref, first, _colsum(dgp))
        dy = dyg * dgelu
        _accumulate(dd_ref, first, _colsum(dy * uf))
        dyb = dy.astype(BF16)
        ub = uf.astype(BF16)

        for ct in range(4):
            win = slice(4 * ct * LANES, 4 * (ct + 1) * LANES)
            dyq = dyb[:, ct * LANES:(ct + 1) * LANES]
            dcre_ref[win, :] += _dot_tn(_load_tiles(sre, ct, T, BF16), dyq)
            dcim_ref[win, :] -= _dot_tn(_load_tiles(sim, ct, T, BF16), dyq)
            _store_tiles(gre, ct, T, _dot_nt(dyq, cre_ref[win, :]))
            _store_tiles(gim, ct, T, -_dot_nt(dyq, cim_ref[win, :]))

        halves = [slice(h * S5_HALF, (h + 1) * S5_HALF) for h in range(2)]
        a_re = [are_ref[hs, :] for hs in halves]
        a_im = [aim_ref[hs, :] for hs in halves]

        def adjoint(t, h, g_re, g_im):
            rows = pl.ds(pl.multiple_of(t * S5_HALF, S5_HALF), S5_HALF)
            n_re = gre[h][rows, :] + a_re[h] * g_re + a_im[h] * g_im
            n_im = gim[h][rows, :] + a_re[h] * g_im - a_im[h] * g_re
            gre[h][rows, :] = n_re
            gim[h][rows, :] = n_im
            return n_re, n_im

        def step(k, carry):
            out = []
            for h in range(2):
                out += adjoint(T - 1 - k, h, carry[2 * h], carry[2 * h + 1])
            return tuple(out)

        init = (car_re[halves[0], :], car_im[halves[0], :], car_re[halves[1], :], car_im[halves[1], :])
        fin = lax.fori_loop(0, T, step, init, unroll=4)
        keep = (i < n - 1).astype(F32)
        later, earlier = slice(S5_HALF, T * S5_HALF), slice(0, (T - 1) * S5_HALF)
        fold = lambda v: jnp.sum(v.reshape(T - 1, S5_HALF, LANES), axis=0)
        for h in range(2):
            g_re, g_im = fin[2 * h], fin[2 * h + 1]
            car_re[halves[h], :] = g_re
            car_im[halves[h], :] = g_im
            p_re, p_im = pre[h][...] * keep, pim[h][...] * keep
            da_re = fold(gre[h][later, :] * sre[h][earlier, :] + gim[h][later, :] * sim[h][earlier, :])
            da_im = fold(gim[h][later, :] * sre[h][earlier, :] - gre[h][later, :] * sim[h][earlier, :])
            da_re = da_re + g_re * p_re + g_im * p_im
            da_im = da_im + g_im * p_re - g_re * p_im

            @pl.when(first)
            def _():
                dare_ref[halves[h], :] = da_re
                daim_ref[halves[h], :] = da_im

            @pl.when(jnp.logical_not(first))
            def _():
                dare_ref[halves[h], :] += da_re
                daim_ref[halves[h], :] += da_im

        tiles = []
        for ct in range(4):
            uq = ub[:, ct * LANES:(ct + 1) * LANES]
            win = slice(4 * ct * LANES, 4 * (ct + 1) * LANES)
            g_re, g_im = _load_tiles(gre, ct, T, BF16), _load_tiles(gim, ct, T, BF16)
            tiles.append(d_ref[:, ct * LANES:(ct + 1) * LANES] * dy[:, ct * LANES:(ct + 1) * LANES]
                         + _dot_nt(g_re, bre_ref[:, win]) + _dot_nt(g_im, bim_ref[:, win]))
            dbre_ref[:, win] += _dot_tn(uq, g_re)
            dbim_ref[:, win] += _dot_tn(uq, g_im)
        du_ref[...] = jnp.concatenate(tiles, axis=1).astype(BF16)

    rev = lambda i: (n - 1 - i, 0)
    const = lambda shape: pl.BlockSpec(shape, lambda i: (0, 0))
    chunk = pl.BlockSpec((T, S5_W), rev)
    sspec = pl.BlockSpec((T * S5_HALF, LANES), rev)
    pspec = pl.BlockSpec((S5_HALF, LANES), lambda i: (jnp.maximum((n - 1 - i) * T - 1, 0), 0))
    tile = jax.ShapeDtypeStruct((S5_TILES, LANES), F32)
    vec = jax.ShapeDtypeStruct((1, S5_W), F32)
    sbuf = pltpu.VMEM((T * S5_HALF, LANES), F32)
    return _pallas(
        body, [dm, y_pre, u, *states, *states, are, aim, bre, bim, cre, cim, d_skip, wglu, bglu],
        name="s5_bwd", grid=(n,),
        in_specs=[chunk, chunk, chunk] + [sspec] * 4 + [pspec] * 4 + [
            const((S5_TILES, LANES)), const((S5_TILES, LANES)), const((LANES, S5_N)), const((LANES, S5_N)),
            const((S5_N, LANES)), const((S5_N, LANES)), const((1, S5_W)), const((S5_W, S5_W)), const((1, S5_W))],
        out_specs=[chunk, const((S5_W, S5_W)), const((1, S5_W)), const((1, S5_W)), const((S5_N, LANES)),
                   const((S5_N, LANES)), const((LANES, S5_N)), const((LANES, S5_N)), const((S5_TILES, LANES)),
                   const((S5_TILES, LANES))],
        out_shape=[jax.ShapeDtypeStruct((L, S5_W), BF16), jax.ShapeDtypeStruct((S5_W, S5_W), F32), vec, vec,
                   jax.ShapeDtypeStruct((S5_N, LANES), F32), jax.ShapeDtypeStruct((S5_N, LANES), F32),
                   jax.ShapeDtypeStruct((LANES, S5_N), F32), jax.ShapeDtypeStruct((LANES, S5_N), F32), tile, tile],
        scratch_shapes=[sbuf, sbuf, sbuf, sbuf, pltpu.VMEM((S5_TILES, LANES), F32), pltpu.VMEM((S5_TILES, LANES), F32)],
        comm=comm)


_EYE8 = np.eye(8, dtype=np.float32)


def _compact_b(b):
    return jnp.einsum("akpc,kj->jcakp", b.reshape(4, 8, S5_P, S5_GC), _EYE8).reshape(LANES, S5_N)


def _uncompact_b(m):
    return jnp.einsum("kcakp->akpc", m.reshape(8, S5_GC, 4, 8, S5_P)).reshape(S5_G, S5_P, S5_GC)


_HEAD_MEAN = np.kron(np.eye(CONV_W // CONV_HD, dtype=np.float32), np.full((CONV_HD, CONV_HD), 1.0 / CONV_HD, np.float32))


def _head_mean(v, m):
    hi = v.astype(BF16)
    lo = (v - hi.astype(F32)).astype(BF16)
    return _dot(hi, m) + _dot(lo, m)


def _head_norm(zc, m):
    d = zc - _head_mean(zc, m)
    rstd = lax.rsqrt(_head_mean(d * d, m) + EPS)
    return d * rstd, rstd


def _taps_by_phase(first):
    groups = {}
    for k in range(CONV_K):
        m, s = divmod(first + k, 8)
        groups.setdefault(s, []).append((m, k))
    return groups


def _causal_taps(buf, w_ref, r0, first, flip):
    acc = None
    for s, taps in sorted(_taps_by_phase(first).items()):
        rows = CONV_SB + (8 if s else 0)
        y = None
        for m, k in taps:
            kk = CONV_K - 1 - k if flip else k
            term = w_ref[kk:kk + 1, :] * buf[pl.ds(r0 + 8 * m, rows), :]
            y = term if y is None else y + term
        y = y[s:s + CONV_SB, :]
        acc = y if acc is None else acc + y
    return acc


def _conv_fwd(u, o_s5, wdw, bdw, lng, lnb, mavg, comm=()):
    L = u.shape[0]
    T = min(TS, L)
    n = L // T
    first_tap = HALO - (CONV_K - 1)

    def body(v1_ref, v2_ref, s5_ref, w_ref, b_ref, g_ref, be_ref, m_ref, zc_ref, o_ref, zbuf):
        i = pl.program_id(0)

        @pl.when(i == 0)
        def _():
            zbuf[0:HALO, :] = jnp.zeros((HALO, CONV_W), F32)

        zbuf[HALO:HALO + T, :] = v1_ref[...] * _sigmoid(v2_ref[...])
        for r0 in range(0, T, CONV_SB):
            zc_ref[r0:r0 + CONV_SB, :] = b_ref[...] + _causal_taps(zbuf, w_ref, r0, first_tap, False)
        zbuf[0:HALO, :] = zbuf[T:T + HALO, :]
        zn, _ = _head_norm(zc_ref[...], m_ref[...])
        zz = zn * g_ref[...] + be_ref[...]
        o_ref[:, 0:S5_W] = s5_ref[...]
        o_ref[:, S5_W:S5_W + CONV_W] = (zz * _sigmoid(zz)).astype(BF16)

    const = lambda shape: pl.BlockSpec(shape, lambda i: (0, 0))
    vec = const((1, CONV_W))
    return _pallas(body, [u, u, o_s5, wdw, bdw, lng, lnb, mavg], name="conv_fwd", grid=(n,),
                   in_specs=[pl.BlockSpec((T, CONV_W), lambda i: (i, 1)), pl.BlockSpec((T, CONV_W), lambda i: (i, 2)),
                             pl.BlockSpec((T, S5_W), lambda i: (i, 0)), const((CONV_K, CONV_W)), vec, vec, vec,
                             const((CONV_W, CONV_W))],
                   out_specs=[pl.BlockSpec((T, CONV_W), lambda i: (i, 0)),
                              pl.BlockSpec((T, S5_W + CONV_W), lambda i: (i, 0))],
                   out_shape=[jax.ShapeDtypeStruct((L, CONV_W), F32), jax.ShapeDtypeStruct((L, S5_W + CONV_W), BF16)],
                   scratch_shapes=[pltpu.VMEM((T + HALO, CONV_W), F32)], comm=comm)


def _conv_bwd(dm, zc, u, du_s5, wdw, lng, lnb, mavg, comm=()):
    L = u.shape[0]
    T = min(TS, L)
    n = L // T
    hb = T // HALO
    first_tap = HALO - (CONV_K - 1)

    def body(dm_ref, zc_ref, v1_ref, v2_ref, p1_ref, p2_ref, s5_ref, w_ref, g_ref, be_ref, m_ref,
             du_ref, dw_ref, db_ref, dg_ref, dbe_ref, zbuf, dzbuf, head):
        i = pl.program_id(0)
        first = i == 0

        @pl.when(first)
        def _():
            head[...] = jnp.zeros_like(head)
            dw_ref[...] = jnp.zeros_like(dw_ref)

        zn, rstd = _head_norm(zc_ref[...], m_ref[...])
        zz = zn * g_ref[...] + be_ref[...]
        sg = _sigmoid(zz)
        dzz = dm_ref[...].astype(F32) * sg * (1.0 + zz * (1.0 - sg))
        _accumulate(dbe_ref, first, _colsum(dzz))
        _accumulate(dg_ref, first, _colsum(dzz * zn))
        dzn = dzz * g_ref[...]
        dzc = rstd * (dzn - _head_mean(dzn, m_ref[...]) - zn * _head_mean(dzn * zn, m_ref[...]))
        _accumulate(db_ref, first, _colsum(dzc))

        dzbuf[0:T, :] = dzc
        dzbuf[T:T + HALO, :] = head[...]
        head[...] = dzbuf[0:HALO, :]
        keep = (i < n - 1).astype(F32)
        zbuf[0:HALO, :] = p1_ref[...] * _sigmoid(p2_ref[...]) * keep
        zbuf[HALO:HALO + T, :] = v1_ref[...] * _sigmoid(v2_ref[...])
        du_ref[:, 0:S5_W] = s5_ref[...]

        for r0 in range(0, T, CONV_SB):
            rows = slice(r0, r0 + CONV_SB)
            dzc_b = dzbuf[rows, :]
            for s, taps in sorted(_taps_by_phase(first_tap).items()):
                pad = ([jnp.zeros((s, CONV_W), F32)] if s else []) + [dzc_b] + ([jnp.zeros((8 - s, CONV_W), F32)] if s else [])
                shifted = jnp.concatenate(pad, axis=0) if s else dzc_b
                n_rows = shifted.shape[0]
                for m, k in taps:
                    prod = shifted * zbuf[pl.ds(r0 + 8 * m, n_rows), :]
                    dw_ref[8 * k:8 * k + 8, :] += jnp.sum(prod.reshape(n_rows // 8, 8, CONV_W), axis=0)
            dz = _causal_taps(dzbuf, w_ref, r0, 0, True)
            v1 = v1_ref[rows, :]
            sg2 = _sigmoid(v2_ref[rows, :])
            du_ref[rows, S5_W:S5_W + CONV_W] = (dz * sg2).astype(BF16)
            du_ref[rows, S5_W + CONV_W:S5_W + 2 * CONV_W] = (dz * v1 * sg2 * (1.0 - sg2)).astype(BF16)

    rev = lambda c: (lambda i: (n - 1 - i, c))
    prev = lambda c: (lambda i: (jnp.maximum((n - 1 - i) * hb - 1, 0), c))
    const = lambda shape: pl.BlockSpec(shape, lambda i: (0, 0))
    vec = const((1, CONV_W))
    vshape = jax.ShapeDtypeStruct((1, CONV_W), F32)
    return _pallas(
        body, [dm, zc, u, u, u, u, du_s5, wdw, lng, lnb, mavg], name="conv_bwd", grid=(n,),
        in_specs=[pl.BlockSpec((T, CONV_W), rev(1)), pl.BlockSpec((T, CONV_W), rev(0)),
                  pl.BlockSpec((T, CONV_W), rev(1)), pl.BlockSpec((T, CONV_W), rev(2)),
                  pl.BlockSpec((HALO, CONV_W), prev(1)), pl.BlockSpec((HALO, CONV_W), prev(2)),
                  pl.BlockSpec((T, S5_W), rev(0)), const((CONV_K, CONV_W)), vec, vec, const((CONV_W, CONV_W))],
        out_specs=[pl.BlockSpec((T, S5_W + 2 * CONV_W), rev(0)), const((8 * HALO, CONV_W)), vec, vec, vec],
        out_shape=[jax.ShapeDtypeStruct((L, S5_W + 2 * CONV_W), BF16), jax.ShapeDtypeStruct((8 * HALO, CONV_W), F32),
                   vshape, vshape, vshape],
        scratch_shapes=[pltpu.VMEM((T + HALO, CONV_W), F32), pltpu.VMEM((T + HALO, CONV_W), F32),
                        pltpu.VMEM((HALO, CONV_W), F32)], comm=comm)


def _gather_all(v, comm=()):
    rows, cols = v.shape

    def body(x_ref, out_ref, send_sems, recv_sems, local_sem):
        x, y, c, chips = _position()
        me, sibling = (x, y, c), (x, y, 1 - c)

        def block(px, py, pc):
            return out_ref.at[pl.ds((4 * px + 2 * py + pc) * rows, rows), :]

        def copy(k, blk, to, src=None):
            return pltpu.make_async_remote_copy(
                src_ref=block(*blk) if src is None else src, dst_ref=block(*blk), send_sem=send_sems.at[k],
                recv_sem=recv_sems.at[k], device_id=to, device_id_type=MESH)

        mine = pltpu.make_async_copy(x_ref, block(*me), local_sem)
        mine.start()
        first = [copy(0, me, sibling, src=x_ref)]
        first += [copy(1 + j, me, (*chip, c), src=x_ref) for j, chip in enumerate(chips)]
        for cp in first:
            cp.start()
        passed = [copy(4 + j, (*chip, c), sibling) for j, chip in enumerate(chips)]
        for j, chip in enumerate(chips):
            copy(1 + j, (*chip, c), me).wait_recv()
            passed[j].start()
        copy(0, sibling, me).wait_recv()
        for j, chip in enumerate(chips):
            copy(4 + j, (*chip, 1 - c), me).wait_recv()
        for cp in first + passed:
            cp.wait_send()
        mine.wait()

    whole = pl.BlockSpec(memory_space=pltpu.VMEM)
    return _pallas(body, [v], name="gather_small", grid=(1,), in_specs=[whole], out_specs=[whole],
                   out_shape=[jax.ShapeDtypeStruct((N_DEV * rows, cols), v.dtype)],
                   scratch_shapes=[pltpu.SemaphoreType.DMA((7,)), pltpu.SemaphoreType.DMA((7,)), pltpu.SemaphoreType.DMA],
                   comm=comm)[0]


def _adamw(w, g, m, v):
    m = ADAM_B1 * m + (1.0 - ADAM_B1) * g
    v = ADAM_B2 * v + (1.0 - ADAM_B2) * jnp.square(g)
    m_hat = m / (1.0 - ADAM_B1 ** ADAM_STEP)
    v_hat = v / (1.0 - ADAM_B2 ** ADAM_STEP)
    return -ADAM_LR * (m_hat / (jnp.sqrt(v_hat) + ADAM_EPS) + ADAM_WD * w), m, v


def _sum_slots(recv, name, comm=()):
    _, rows, cols = recv.shape
    tr = _row_tile(rows, cols)

    def body(r_ref, o_ref):
        acc = r_ref[0].astype(F32)
        for s in range(1, N_CHIPS):
            acc = acc + r_ref[s].astype(F32)
        o_ref[...] = acc

    return _pallas(body, [recv], name=name, grid=(rows // tr,),
                   in_specs=[pl.BlockSpec((N_CHIPS, tr, cols), lambda i: (0, i, 0))],
                   out_specs=[pl.BlockSpec((tr, cols), lambda i: (i, 0))],
                   out_shape=[jax.ShapeDtypeStruct((rows, cols), F32)], comm=comm)[0]


def _add_halves(mine, theirs, name):
    slots, rows, cols = mine.shape
    tr = _row_tile(rows, cols * slots)

    def body(a_ref, b_ref, o_ref):
        o_ref[...] = (a_ref[...].astype(F32) + b_ref[...].astype(F32)).astype(o_ref.dtype)

    spec = pl.BlockSpec((slots, tr, cols), lambda i: (0, i, 0))
    return _pallas(body, [mine, theirs], name=name, grid=(rows // tr,), in_specs=[spec, spec], out_specs=[spec],
                   out_shape=[jax.ShapeDtypeStruct(mine.shape, mine.dtype)])[0]


def _adamw_sharded(w, parts, m, v, name, comm=()):
    rows, cols = w.shape
    tr = _row_tile(rows, cols)
    n = len(parts)

    def body(w_ref, *refs):
        p_refs, (m_ref, v_ref, g_ref, d_ref, nm_ref, nv_ref) = refs[:n], refs[n:]
        g = p_refs[0][...]
        for p_ref in p_refs[1:]:
            g = g + p_ref[...]
        g_ref[...] = g
        d_ref[...], nm_ref[...], nv_ref[...] = _adamw(w_ref[...], g, m_ref[...], v_ref[...])

    spec = pl.BlockSpec((tr, cols), lambda i: (i, 0))
    shape = jax.ShapeDtypeStruct((rows, cols), F32)
    return _pallas(body, [w, *parts, m, v], name=name, grid=(rows // tr,), in_specs=[spec] * (n + 3),
                   out_specs=[spec] * 4, out_shape=[shape] * 4, comm=comm)


def _adamw_small(w, gathered, m, v):
    rows, cols = w.shape

    def body(w_ref, a_ref, m_ref, v_ref, g_ref, d_ref, nm_ref, nv_ref):
        g = a_ref[0:rows, :]
        for dev in range(1, N_DEV):
            g = g + a_ref[dev * rows:(dev + 1) * rows, :]
        g_ref[...] = g
        d_ref[...], nm_ref[...], nv_ref[...] = _adamw(w_ref[...], g, m_ref[...], v_ref[...])

    shape = jax.ShapeDtypeStruct((rows, cols), F32)
    return pl.pallas_call(
        body, name="adamw_small", out_shape=[shape] * 4,
        compiler_params=pltpu.CompilerParams(vmem_limit_bytes=VMEM_LIMIT))(w, gathered, m, v)


PACK_TILE = 8 * LANES


STATE_MINOR = ("s5_b_re", "s5_b_im")


def _pack_small(vals, last_row):
    rows = []
    for name in SMALL:
        v = jnp.swapaxes(vals[name], -1, -2) if name in STATE_MINOR else vals[name]
        flat = v.reshape(-1).astype(F32)
        rows.append(jnp.pad(flat, (0, -flat.size % PACK_TILE)).reshape(-1, LANES))
    rows.append(jnp.pad(last_row, ((0, 7), (0, 0))))
    return jnp.concatenate(rows, axis=0)


def _unpack_small(packed, like):
    out, r = {}, 0
    for name in SMALL:
        size, shape = like[name].size, like[name].shape
        flat = packed[r:r + -(-size // LANES)].reshape(-1)[:size]
        if name in STATE_MINOR:
            out[name] = jnp.swapaxes(flat.reshape(shape[:-2] + (shape[-1], shape[-2])), -1, -2)
        else:
            out[name] = flat.reshape(shape)
        r += 8 * -(-size // PACK_TILE)
    return out, packed[r, 0]


def _shard2d(name, v):
    v = v.reshape(v.shape[-2:])
    return v.T if name in FFN_T else v


def _unshard(name, v, shape):
    return (v.T if name in FFN_T else v).reshape(shape)


def _train_step(x3, tgt3, wts, ms, vs):
    x, tgt = x3[0], tgt3[0]
    L, D = x.shape
    row = lambda v: v.reshape(1, -1)
    shards = {k: _shard2d(k, wts[k]) for k in SHARDED}
    sends = {k: shards[k] if k == "conv_w_dw" else _cast_bf16(shards[k], "cast_" + k) for k in SHARDED}
    gat = {k: _Gather(sends[k]) for k in SHARDED}
    w = lambda k: gat[k].result[0]

    s = {k: wts[k] for k in SMALL}
    lr, li = s["s5_lam_re"].reshape(1, S5_N), s["s5_lam_im"].reshape(1, S5_N)
    ldt = jnp.repeat(s["s5_log_dt"].reshape(S5_G), S5_P).reshape(1, S5_N)
    brc, bic = _compact_b(s["s5_b_re"].reshape(S5_G, S5_P, S5_GC)), _compact_b(s["s5_b_im"].reshape(S5_G, S5_P, S5_GC))
    crc = _compact_b(s["s5_c_re"].reshape(S5_G, S5_GC, S5_P).transpose(0, 2, 1)).T
    cic = _compact_b(s["s5_c_im"].reshape(S5_G, S5_GC, S5_P).transpose(0, 2, 1)).T
    d_skip, b_glu = row(s["s5_d"]), row(s["s5_b_glu"])
    b_dw, ln_g, ln_b = row(s["conv_b_dw"]), row(s["conv_ln_g"]), row(s["conv_ln_b"])
    g1, gm, g2, gf = row(s["ffn1_norm"]), row(s["mix_norm"]), row(s["ffn2_norm"]), row(s["final_norm"])
    mavg = jnp.asarray(_HEAD_MEAN, dtype=BF16)

    h1 = _rms_fwd(x, g1, "rms1", comm=[gat["ffn1_w_gate"], gat["ffn1_w_up"]])
    a1, b1, act1 = _ffn_up(h1, w("ffn1_w_gate"), w("ffn1_w_up"), "ffn1_up", comm=[gat["ffn1_w_down"]])
    x1, h2 = _ffn_down(act1, w("ffn1_w_down"), x, gm, "ffn1_down",
                       comm=[gat["w_in"], gat["s5_w_glu"], gat["conv_w_dw"], gat["w_out"]])
    u = _mm_grouped(h2, w("w_in"), "in_proj")
    are, aim, bre, bim, cre, cim = _s5_params_fwd(lr, li, ldt, brc, bic, crc, cic)
    are_t, aim_t = are.reshape(S5_TILES, LANES), aim.reshape(S5_TILES, LANES)
    w_glu = w("s5_w_glu").reshape(S5_W, S5_W)
    *states, y_pre, o_s5 = _s5_fwd(u, are_t, aim_t, bre, bim, cre, cim, d_skip, w_glu, b_glu,
                                   comm=[gat["ffn2_w_gate"], gat["ffn2_w_up"]])
    w_dw = w("conv_w_dw").transpose(1, 0, 2).reshape(CONV_K, CONV_W)
    zc, cat = _conv_fwd(u, o_s5, w_dw, b_dw, ln_g, ln_b, mavg, comm=[gat["ffn2_w_down"]])
    w_out = w("w_out").reshape(-1, D)
    x2, h3 = _mix_out(cat, w_out, x1, g2, "mix_out")
    a2, b2, act2 = _ffn_up(h3, w("ffn2_w_gate"), w("ffn2_w_up"), "ffn2_up")
    dx3, dx3b, loss_part, d_gf = _ffn_down_loss(act2, w("ffn2_w_down"), x2, gf, tgt, "ffn2_down_loss")

    gs, sc, waiting = {"final_norm": d_gf}, {}, []

    def grad(key, g):
        if g.shape[1] % (2 * BF16_ROWS) == 0 and g.dtype == BF16:
            waiting.append(_SwapHalf(g, key))
        else:
            sc[key] = _Scatter(g)
            waiting.append(sc[key])

    def carry(call, *args, **kw):
        ops = list(waiting)
        waiting.clear()
        res = call(*args, comm=ops, **kw)
        for op in ops:
            if isinstance(op, _SwapHalf):
                sc[op.key] = _Scatter(_add_halves(*op.result, "add_" + op.key))
                waiting.append(sc[op.key])
        return res

    da2, db2 = _ffn_bwd_act(dx3b, w("ffn2_w_down"), a2, b2, "ffn2_bwd_act")
    grad("ffn2_w_down", _mm_tn(act2, dx3b[None], 0.5, "ffn2_dwd", N_CHIPS))
    d_gate2, d_up2 = carry(_mm_tn_pair, da2, db2, h3, "ffn2_dwgu")
    grad("ffn2_w_gate", d_gate2)
    dx2, dx2b, gs["ffn2_norm"] = carry(_mm_rmsbwd, [da2, db2], [w("ffn2_w_gate"), w("ffn2_w_up")], False, False, x2, g2,
                                       dx3, "ffn2_bwd_dx")
    grad("ffn2_w_up", d_up2)

    dm = _mm_nt(dx2b, w_out, "mix_bwd")
    grad("w_out", _mm_tn(cat[None], dx2b[None], 1.0, "dwout", 1).reshape(N_CHIPS, -1, D))
    (du_s5, d_wglu, gs["s5_b_glu"], gs["s5_d"], d_crc, d_cic, d_bre, d_bim, d_are, d_aim) = carry(
        _s5_bwd, dm, y_pre, u, states, are_t, aim_t, bre, bim, cre, cim, d_skip, w_glu, b_glu)
    grad("s5_w_glu", d_wglu.astype(BF16).reshape(N_CHIPS, -1, S5_W))
    g_lr, g_li, g_ldt, g_brc, g_bic = _s5_params_bwd(lr, li, ldt, brc, bic, d_are.reshape(1, S5_N),
                                                     d_aim.reshape(1, S5_N), d_bre, d_bim)
    gs["s5_lam_re"], gs["s5_lam_im"] = g_lr, g_li
    gs["s5_log_dt"] = jnp.sum(g_ldt.reshape(S5_G, S5_P), axis=1)
    gs["s5_b_re"], gs["s5_b_im"] = _uncompact_b(g_brc), _uncompact_b(g_bic)
    gs["s5_c_re"] = _uncompact_b(d_crc.T).transpose(0, 2, 1)
    gs["s5_c_im"] = _uncompact_b(d_cic.T).transpose(0, 2, 1)
    du, d_wdw, gs["conv_b_dw"], gs["conv_ln_g"], gs["conv_ln_b"] = carry(_conv_bwd, dm, zc, u, du_s5, w_dw, ln_g, ln_b, mavg)
    d_wdw = jnp.sum(d_wdw.reshape(HALO, 8, CONV_W), axis=1)[:CONV_K]
    grad("conv_w_dw", d_wdw.reshape(CONV_K, N_CHIPS, -1).transpose(1, 0, 2))
    grad("w_in", carry(_mm_tn, h2[None], du, 1.0, "dwin", N_CHIPS, b_cols=True))
    dx1, dx1b, gs["mix_norm"] = carry(_mm_rmsbwd, [du], [w("w_in")], True, True, x1, gm, dx2, "in_proj_bwd")

    da1, db1 = carry(_ffn_bwd_act, dx1b, w("ffn1_w_down"), a1, b1, "ffn1_bwd_act")
    grad("ffn1_w_down", _mm_tn(act1, dx1b[None], 0.5, "ffn1_dwd", N_CHIPS))
    grad("ffn1_w_gate", carry(_mm_tn, da1, h1[None], 1.0, "ffn1_dwg", N_CHIPS))
    grad("ffn1_w_up", carry(_mm_tn, db1, h1[None], 1.0, "ffn1_dwu", N_CHIPS))
    grad_x, _, gs["ffn1_norm"] = carry(_mm_rmsbwd, [da1, db1], [w("ffn1_w_gate"), w("ffn1_w_up")], False, False, x, g1,
                                       dx1, "ffn1_bwd_dx")

    out = {}
    gsmall = {k: gs[k].reshape(wts[k].shape) for k in SMALL}
    zero_row = jnp.zeros((1, LANES), F32)
    g_all = carry(_gather_all, _pack_small(gsmall, loss_part))
    res = _adamw_small(_pack_small(s, zero_row), g_all, _pack_small({k: ms[k] for k in SMALL}, zero_row),
                       _pack_small({k: vs[k] for k in SMALL}, zero_row))
    unpacked = [_unpack_small(r, s) for r in res]
    loss = unpacked[0][1]
    for k in SMALL:
        out[k] = [u_[0][k] for u_ in unpacked]

    order = ("ffn2_w_down", "ffn2_w_gate", "ffn2_w_up", "w_out", "s5_w_glu", "conv_w_dw", "w_in", "ffn1_w_down",
             "ffn1_w_gate", "ffn1_w_up")
    back = {}
    for k in order:
        part = carry(_sum_slots, sc[k].result[0], "sum_" + k)
        back[k] = _SwapBack(part) if part.shape != shards[k].shape else _Swap(part)
        waiting.append(back[k])
    for k in order:
        parts = [back[k].result[0]] if isinstance(back[k], _SwapBack) else [back[k].ins[0], back[k].result[0]]
        res = carry(_adamw_sharded, shards[k], parts, _shard2d(k, ms[k]), _shard2d(k, vs[k]), "adamw_" + k)
        out[k] = [_unshard(k, r, wts[k].shape) for r in res]
    return loss, grad_x[None], out


def kernel(x, ffn1_norm, ffn1_w_gate, ffn1_w_up, ffn1_w_down, mix_norm, w_in, s5_lam_re, s5_lam_im, s5_log_dt, s5_b_re, s5_b_im, s5_c_re, s5_c_im, s5_d, s5_w_glu, s5_b_glu, conv_w_dw, conv_b_dw, conv_ln_g, conv_ln_b, w_out, ffn2_norm, ffn2_w_gate, ffn2_w_up, ffn2_w_down, final_norm, loss_target, m_ffn1_norm, m_ffn1_w_gate, m_ffn1_w_up, m_ffn1_w_down, m_mix_norm, m_w_in, m_s5_lam_re, m_s5_lam_im, m_s5_log_dt, m_s5_b_re, m_s5_b_im, m_s5_c_re, m_s5_c_im, m_s5_d, m_s5_w_glu, m_s5_b_glu, m_conv_w_dw, m_conv_b_dw, m_conv_ln_g, m_conv_ln_b, m_w_out, m_ffn2_norm, m_ffn2_w_gate, m_ffn2_w_up, m_ffn2_w_down, m_final_norm, v_ffn1_norm, v_ffn1_w_gate, v_ffn1_w_up, v_ffn1_w_down, v_mix_norm, v_w_in, v_s5_lam_re, v_s5_lam_im, v_s5_log_dt, v_s5_b_re, v_s5_b_im, v_s5_c_re, v_s5_c_im, v_s5_d, v_s5_w_glu, v_s5_b_glu, v_conv_w_dw, v_conv_b_dw, v_conv_ln_g, v_conv_ln_b, v_w_out, v_ffn2_norm, v_ffn2_w_gate, v_ffn2_w_up, v_ffn2_w_down, v_final_norm):
    given = dict(locals())
    wts = {k: given[k] for k in WEIGHTS}
    ms = {k: given["m_" + k] for k in WEIGHTS}
    vs = {k: given["v_" + k] for k in WEIGHTS}
    loss, grad_x, out = _train_step(x, loss_target, wts, ms, vs)
    return (loss, grad_x, *[out[k][0] for k in WEIGHTS], *[out[k][1] for k in WEIGHTS],
            *[out[k][2] for k in WEIGHTS], *[out[k][3] for k in WEIGHTS])
```
